```python
import jax, jax.numpy as jnp
from jax import lax
import numpy as np

D_MODEL = 1024
BATCH = 8
SEQ = 4096
DEPTH = 2

ATT_HEADS = 8
HEAD_DIM = 64
D_ATT = ATT_HEADS * HEAD_DIM
Q_BLOCK = 128
ATT_SCALE = HEAD_DIM ** -0.5
FORGET_BIAS_MEAN = 2.0
D_CONV = 256
CONV_WIDTH = 31
POOL_WINDOWS = (2, 4, 8, 16)
N_POOL = len(POOL_WINDOWS)
POOL_GROUP = 64
D_POOL = N_POOL * POOL_GROUP
D_MIX = D_ATT + D_CONV + D_POOL
D_IN = 3 * D_ATT + ATT_HEADS + 2 * D_CONV + D_POOL
D_FF = 2816
FFN_CONV_WIDTH = 3
EPS = 1e-6
NEG = -1e30

kernel_name = 'hybrid_fox_conformer_pool_block'


def rms_norm(x, g):
    xf = x.astype(jnp.float32)
    y = xf * lax.rsqrt(jnp.mean(xf * xf, axis=-1, keepdims=True) + EPS)
    return (y * g.astype(jnp.float32)).astype(x.dtype)


def layer_norm(x, g, b):
    xf = x.astype(jnp.float32)
    mu = jnp.mean(xf, axis=-1, keepdims=True)
    var = jnp.mean(jnp.square(xf - mu), axis=-1, keepdims=True)
    y = (xf - mu) * lax.rsqrt(var + EPS)
    return (y * g.astype(jnp.float32) + b.astype(jnp.float32)).astype(x.dtype)


def causal_depthwise_conv(x, w):
    width, channels = w.shape
    return lax.conv_general_dilated(
        x, w.astype(x.dtype)[:, None, :], window_strides=(1,),
        padding=[(width - 1, 0)], dimension_numbers=('NWC', 'WIO', 'NWC'),
        feature_group_count=channels)


def forgetting_attention(q, k, v, fg_logit):
    B, S, H, Dh = q.shape
    log_f = jax.nn.log_sigmoid(fg_logit.astype(jnp.float32))
    cum = jnp.transpose(jnp.cumsum(log_f, axis=1), (0, 2, 1))
    outs = []
    for start in range(0, S, Q_BLOCK):
        end = start + Q_BLOCK
        qb = q[:, start:end]
        kb = k[:, :end]
        vb = v[:, :end]
        s = jnp.einsum('bqhd,bkhd->bhqk', qb, kb).astype(jnp.float32) * ATT_SCALE
        s = s + cum[:, :, start:end, None] - cum[:, :, None, :end]
        qi = jnp.arange(start, end)[:, None]
        ki = jnp.arange(end)[None, :]
        s = jnp.where(ki <= qi, s, NEG)
        p = jax.nn.softmax(s, axis=-1).astype(vb.dtype)
        outs.append(jnp.einsum('bhqk,bkhd->bqhd', p, vb))
    return jnp.concatenate(outs, axis=1)


def conformer_conv(u, dw_w, dw_b, ln_g, ln_b, pw_w):
    a, b = jnp.split(u, 2, axis=-1)
    h = a * jax.nn.sigmoid(b)
    h = causal_depthwise_conv(h, dw_w) + dw_b.astype(h.dtype)
    h = layer_norm(h, ln_g, ln_b)
    h = jax.nn.silu(h)
    return h @ pw_w.astype(h.dtype)


def multiscale_pool(u, pool_w, pool_scale):
    B, S, _ = u.shape
    uf = u.astype(jnp.float32)
    cs = jnp.cumsum(uf, axis=1)
    pos = jnp.arange(1, S + 1, dtype=jnp.float32)[:, None]
    diffs = []
    for g, w in enumerate(POOL_WINDOWS):
        sl = slice(g * POOL_GROUP, (g + 1) * POOL_GROUP)
        c = cs[..., sl]
        prev = jnp.pad(c, ((0, 0), (w, 0), (0, 0)))[:, :S]
        mean = (c - prev) / jnp.minimum(pos, float(w))
        diffs.append(mean - uf[..., sl])
    d = jnp.stack(diffs, axis=2)
    y = jnp.einsum('bsgc,gcd->bsgd', d, pool_w.astype(jnp.float32))
    y = y.reshape(B, S, D_POOL) * pool_scale.astype(jnp.float32)
    return y.astype(u.dtype)


def _fwd_setup_inputs(seed: int = 0) -> dict:
    key = jax.random.key(seed)
    ks = jax.random.split(key, 20)
    f32 = jnp.float32
    L = DEPTH

    def nrm(k, shape, scale):
        return jax.random.normal(k, shape, f32) * scale

    return {
        'x': nrm(ks[0], (BATCH, SEQ, D_MODEL), 1.0),
        'norm1_g': 1.0 + nrm(ks[1], (L, D_MODEL), 0.02),
        'w_in': nrm(ks[2], (L, D_MODEL, D_IN), D_MODEL ** -0.5),
        'b_f': FORGET_BIAS_MEAN + nrm(ks[3], (L, ATT_HEADS), 0.5),
        'q_norm_g': 1.0 + nrm(ks[4], (L, HEAD_DIM), 0.02),
        'k_norm_g': 1.0 + nrm(ks[5], (L, HEAD_DIM), 0.02),
        'conv_dw_w': nrm(ks[6], (L, CONV_WIDTH, D_CONV), CONV_WIDTH ** -0.5),
        'conv_dw_b': nrm(ks[7], (L, D_CONV), 0.02),
        'conv_ln_g': 1.0 + nrm(ks[8], (L, D_CONV), 0.02),
        'conv_ln_b': nrm(ks[9], (L, D_CONV), 0.02),
        'conv_pw_w': nrm(ks[10], (L, D_CONV, D_CONV), D_CONV ** -0.5),
        'pool_w': nrm(ks[11], (L, N_POOL, POOL_GROUP, POOL_GROUP), POOL_GROUP ** -0.5),
        'pool_scale': 1.0 + nrm(ks[12], (L, D_POOL), 0.1),
        'w_out': nrm(ks[13], (L, D_MIX, D_MODEL), D_MIX ** -0.5),
        'norm2_g': 1.0 + nrm(ks[14], (L, D_MODEL), 0.02),
        'w_up': nrm(ks[15], (L, D_MODEL, 2 * D_FF), D_MODEL ** -0.5),
        'ffn_dw_w': nrm(ks[16], (L, FFN_CONV_WIDTH, 2 * D_FF), FFN_CONV_WIDTH ** -0.5),
        'w_down': nrm(ks[17], (L, D_FF, D_MODEL), D_FF ** -0.5),
    }


def _fwd_reference(x, norm1_g, w_in, b_f, q_norm_g, k_norm_g, conv_dw_w, conv_dw_b,
              conv_ln_g, conv_ln_b, conv_pw_w, pool_w, pool_scale, w_out,
              norm2_g, w_up, ffn_dw_w, w_down):
    B, S, _ = x.shape
    for l in range(DEPTH):
        h = rms_norm(x, norm1_g[l])
        proj = h @ w_in[l].astype(h.dtype)
        o = 0
        q = proj[..., o:o + D_ATT]; o += D_ATT
        k = proj[..., o:o + D_ATT]; o += D_ATT
        v = proj[..., o:o + D_ATT]; o += D_ATT
        fg = proj[..., o:o + ATT_HEADS]; o += ATT_HEADS
        conv_in = proj[..., o:o + 2 * D_CONV]; o += 2 * D_CONV
        pool_in = proj[..., o:o + D_POOL]

        q = rms_norm(q.reshape(B, S, ATT_HEADS, HEAD_DIM), q_norm_g[l])
        k = rms_norm(k.reshape(B, S, ATT_HEADS, HEAD_DIM), k_norm_g[l])
        v = v.reshape(B, S, ATT_HEADS, HEAD_DIM)
        fg = fg + b_f[l].astype(fg.dtype)
        att = forgetting_attention(q, k, v, fg).reshape(B, S, D_ATT)

        conv = conformer_conv(conv_in, conv_dw_w[l], conv_dw_b[l],
                              conv_ln_g[l], conv_ln_b[l], conv_pw_w[l])
        pool = multiscale_pool(pool_in, pool_w[l], pool_scale[l])

        mix = jnp.concatenate([att, conv, pool], axis=-1)
        x = x + mix @ w_out[l].astype(mix.dtype)

        h2 = rms_norm(x, norm2_g[l])
        up = h2 @ w_up[l].astype(h2.dtype)
        up = causal_depthwise_conv(up, ffn_dw_w[l])
        gate, val = jnp.split(up, 2, axis=-1)
        x = x + (jax.nn.silu(gate) * val) @ w_down[l].astype(up.dtype)
    return x


import jax as _jax
import jax.numpy as _jnp

TWIN_FORMAT = 'train_step'
FWD_PARAMS = ['x', 'norm1_g', 'w_in', 'b_f', 'q_norm_g', 'k_norm_g', 'conv_dw_w', 'conv_dw_b', 'conv_ln_g', 'conv_ln_b', 'conv_pw_w', 'pool_w', 'pool_scale', 'w_out', 'norm2_g', 'w_up', 'ffn_dw_w', 'w_down']
TWIN_WEIGHTS = ['norm1_g', 'w_in', 'b_f', 'q_norm_g', 'k_norm_g', 'conv_dw_w', 'conv_dw_b', 'conv_ln_g', 'conv_ln_b', 'conv_pw_w', 'pool_w', 'pool_scale', 'w_out', 'norm2_g', 'w_up', 'ffn_dw_w', 'w_down']
TWIN_DIFF_INPUT = 'x'
TWIN_INPUTS = ['x', 'norm1_g', 'w_in', 'b_f', 'q_norm_g', 'k_norm_g', 'conv_dw_w', 'conv_dw_b', 'conv_ln_g', 'conv_ln_b', 'conv_pw_w', 'pool_w', 'pool_scale', 'w_out', 'norm2_g', 'w_up', 'ffn_dw_w', 'w_down', 'loss_target', 'm_norm1_g', 'm_w_in', 'm_b_f', 'm_q_norm_g', 'm_k_norm_g', 'm_conv_dw_w', 'm_conv_dw_b', 'm_conv_ln_g', 'm_conv_ln_b', 'm_conv_pw_w', 'm_pool_w', 'm_pool_scale', 'm_w_out', 'm_norm2_g', 'm_w_up', 'm_ffn_dw_w', 'm_w_down', 'v_norm1_g', 'v_w_in', 'v_b_f', 'v_q_norm_g', 'v_k_norm_g', 'v_conv_dw_w', 'v_conv_dw_b', 'v_conv_ln_g', 'v_conv_ln_b', 'v_conv_pw_w', 'v_pool_w', 'v_pool_scale', 'v_w_out', 'v_norm2_g', 'v_w_up', 'v_ffn_dw_w', 'v_w_down']
TWIN_OUTPUTS = ['loss', 'grad_x', 'grad_norm1_g', 'grad_w_in', 'grad_b_f', 'grad_q_norm_g', 'grad_k_norm_g', 'grad_conv_dw_w', 'grad_conv_dw_b', 'grad_conv_ln_g', 'grad_conv_ln_b', 'grad_conv_pw_w', 'grad_pool_w', 'grad_pool_scale', 'grad_w_out', 'grad_norm2_g', 'grad_w_up', 'grad_ffn_dw_w', 'grad_w_down', 'delta_norm1_g', 'delta_w_in', 'delta_b_f', 'delta_q_norm_g', 'delta_k_norm_g', 'delta_conv_dw_w', 'delta_conv_dw_b', 'delta_conv_ln_g', 'delta_conv_ln_b', 'delta_conv_pw_w', 'delta_pool_w', 'delta_pool_scale', 'delta_w_out', 'delta_norm2_g', 'delta_w_up', 'delta_ffn_dw_w', 'delta_w_down', 'new_m_norm1_g', 'new_m_w_in', 'new_m_b_f', 'new_m_q_norm_g', 'new_m_k_norm_g', 'new_m_conv_dw_w', 'new_m_conv_dw_b', 'new_m_conv_ln_g', 'new_m_conv_ln_b', 'new_m_conv_pw_w', 'new_m_pool_w', 'new_m_pool_scale', 'new_m_w_out', 'new_m_norm2_g', 'new_m_w_up', 'new_m_ffn_dw_w', 'new_m_w_down', 'new_v_norm1_g', 'new_v_w_in', 'new_v_b_f', 'new_v_q_norm_g', 'new_v_k_norm_g', 'new_v_conv_dw_w', 'new_v_conv_dw_b', 'new_v_conv_ln_g', 'new_v_conv_ln_b', 'new_v_conv_pw_w', 'new_v_pool_w', 'new_v_pool_scale', 'new_v_w_out', 'new_v_norm2_g', 'new_v_w_up', 'new_v_ffn_dw_w', 'new_v_w_down']
TWIN_LEAF_KINDS = {'loss': 'loss', 'grad_x': 'grad_x', 'grad_norm1_g': 'grad_w', 'grad_w_in': 'grad_w', 'grad_b_f': 'grad_w', 'grad_q_norm_g': 'grad_w', 'grad_k_norm_g': 'grad_w', 'grad_conv_dw_w': 'grad_w', 'grad_conv_dw_b': 'grad_w', 'grad_conv_ln_g': 'grad_w', 'grad_conv_ln_b': 'grad_w', 'grad_conv_pw_w': 'grad_w', 'grad_pool_w': 'grad_w', 'grad_pool_scale': 'grad_w', 'grad_w_out': 'grad_w', 'grad_norm2_g': 'grad_w', 'grad_w_up': 'grad_w', 'grad_ffn_dw_w': 'grad_w', 'grad_w_down': 'grad_w', 'delta_norm1_g': 'delta_w', 'delta_w_in': 'delta_w', 'delta_b_f': 'delta_w', 'delta_q_norm_g': 'delta_w', 'delta_k_norm_g': 'delta_w', 'delta_conv_dw_w': 'delta_w', 'delta_conv_dw_b': 'delta_w', 'delta_conv_ln_g': 'delta_w', 'delta_conv_ln_b': 'delta_w', 'delta_conv_pw_w': 'delta_w', 'delta_pool_w': 'delta_w', 'delta_pool_scale': 'delta_w', 'delta_w_out': 'delta_w', 'delta_norm2_g': 'delta_w', 'delta_w_up': 'delta_w', 'delta_ffn_dw_w': 'delta_w', 'delta_w_down': 'delta_w', 'new_m_norm1_g': 'new_m', 'new_m_w_in': 'new_m', 'new_m_b_f': 'new_m', 'new_m_q_norm_g': 'new_m', 'new_m_k_norm_g': 'new_m', 'new_m_conv_dw_w': 'new_m', 'new_m_conv_dw_b': 'new_m', 'new_m_conv_ln_g': 'new_m', 'new_m_conv_ln_b': 'new_m', 'new_m_conv_pw_w': 'new_m', 'new_m_pool_w': 'new_m', 'new_m_pool_scale': 'new_m', 'new_m_w_out': 'new_m', 'new_m_norm2_g': 'new_m', 'new_m_w_up': 'new_m', 'new_m_ffn_dw_w': 'new_m', 'new_m_w_down': 'new_m', 'new_v_norm1_g': 'new_v', 'new_v_w_in': 'new_v', 'new_v_b_f': 'new_v', 'new_v_q_norm_g': 'new_v', 'new_v_k_norm_g': 'new_v', 'new_v_conv_dw_w': 'new_v', 'new_v_conv_dw_b': 'new_v', 'new_v_conv_ln_g': 'new_v', 'new_v_conv_ln_b': 'new_v', 'new_v_conv_pw_w': 'new_v', 'new_v_pool_w': 'new_v', 'new_v_pool_scale': 'new_v', 'new_v_w_out': 'new_v', 'new_v_norm2_g': 'new_v', 'new_v_w_up': 'new_v', 'new_v_ffn_dw_w': 'new_v', 'new_v_w_down': 'new_v'}


def _forward(args):
    return _fwd_reference(*[args[k] for k in FWD_PARAMS])


def _output_shape():
    out = _jax.eval_shape(lambda: _forward(_fwd_setup_inputs(0)))
    return out.shape, out.dtype

N_MICROBATCH = 1
ADAM_LR = 0.001
ADAM_B1 = 0.9
ADAM_B2 = 0.999
ADAM_EPS = 1e-08
ADAM_WD = 0.01
ADAM_STEP = 10
PER_EXAMPLE_BATCH_AXIS = {'x': 0, 'loss_target': 0}
SHARED_INPUTS = []
_WEIGHT_DTYPES = {'norm1_g': _jnp.float32, 'w_in': _jnp.float32, 'b_f': _jnp.float32, 'q_norm_g': _jnp.float32, 'k_norm_g': _jnp.float32, 'conv_dw_w': _jnp.float32, 'conv_dw_b': _jnp.float32, 'conv_ln_g': _jnp.float32, 'conv_ln_b': _jnp.float32, 'conv_pw_w': _jnp.float32, 'pool_w': _jnp.float32, 'pool_scale': _jnp.float32, 'w_out': _jnp.float32, 'norm2_g': _jnp.float32, 'w_up': _jnp.float32, 'ffn_dw_w': _jnp.float32, 'w_down': _jnp.float32}
MOMENT_SCALE = {'norm1_g': 9.446896e+00, 'w_in': 5.417513e-01, 'b_f': 1.046528e+02, 'q_norm_g': 1.478015e+01, 'k_norm_g': 1.478568e+01, 'conv_dw_w': 6.572274e-01, 'conv_dw_b': 1.195108e+01, 'conv_ln_g': 1.429005e+01, 'conv_ln_b': 1.090234e+01, 'conv_pw_w': 2.714819e+00, 'pool_w': 3.189063e+00, 'pool_scale': 2.610605e+01, 'w_out': 1.238356e+00, 'norm2_g': 2.524568e+01, 'w_up': 3.891788e-01, 'ffn_dw_w': 3.474855e+00, 'w_down': 4.082905e-01}


def _to_microbatches(a, axis):
    t = _jnp.moveaxis(a, axis, 0)
    t = t.reshape((N_MICROBATCH, t.shape[0] // N_MICROBATCH) + t.shape[1:])
    return _jnp.moveaxis(t, 1, axis + 1)


def setup_inputs(seed: int = 0) -> dict:
    inp = _fwd_setup_inputs(seed)
    key = _jax.random.fold_in(_jax.random.key(seed), 7919)
    shape, _ = _output_shape()
    out = dict(inp)
    out["loss_target"] = _jax.random.normal(_jax.random.fold_in(key, 0), shape, _jnp.float32)
    for i, name in enumerate(TWIN_WEIGHTS):
        w = inp[name].astype(_jnp.float32)
        if MOMENT_SCALE is None:
            s = _jnp.sqrt(_jnp.mean(_jnp.square(w)) + 1e-30)
        else:
            s = MOMENT_SCALE[name]
        km, kv = _jax.random.split(_jax.random.fold_in(key, i + 1))
        out[name] = w
        out["m_" + name] = s * _jax.random.normal(km, w.shape, _jnp.float32)
        out["v_" + name] = (s * s) * _jax.random.uniform(kv, w.shape, _jnp.float32, 0.5, 1.5)
    if N_MICROBATCH > 1:
        for name, axis in PER_EXAMPLE_BATCH_AXIS.items():
            out[name] = _to_microbatches(out[name], axis)
    return {'x': out['x'], 'norm1_g': out['norm1_g'], 'w_in': out['w_in'], 'b_f': out['b_f'], 'q_norm_g': out['q_norm_g'], 'k_norm_g': out['k_norm_g'], 'conv_dw_w': out['conv_dw_w'], 'conv_dw_b': out['conv_dw_b'], 'conv_ln_g': out['conv_ln_g'], 'conv_ln_b': out['conv_ln_b'], 'conv_pw_w': out['conv_pw_w'], 'pool_w': out['pool_w'], 'pool_scale': out['pool_scale'], 'w_out': out['w_out'], 'norm2_g': out['norm2_g'], 'w_up': out['w_up'], 'ffn_dw_w': out['ffn_dw_w'], 'w_down': out['w_down'], 'loss_target': out['loss_target'], 'm_norm1_g': out['m_norm1_g'], 'm_w_in': out['m_w_in'], 'm_b_f': out['m_b_f'], 'm_q_norm_g': out['m_q_norm_g'], 'm_k_norm_g': out['m_k_norm_g'], 'm_conv_dw_w': out['m_conv_dw_w'], 'm_conv_dw_b': out['m_conv_dw_b'], 'm_conv_ln_g': out['m_conv_ln_g'], 'm_conv_ln_b': out['m_conv_ln_b'], 'm_conv_pw_w': out['m_conv_pw_w'], 'm_pool_w': out['m_pool_w'], 'm_pool_scale': out['m_pool_scale'], 'm_w_out': out['m_w_out'], 'm_norm2_g': out['m_norm2_g'], 'm_w_up': out['m_w_up'], 'm_ffn_dw_w': out['m_ffn_dw_w'], 'm_w_down': out['m_w_down'], 'v_norm1_g': out['v_norm1_g'], 'v_w_in': out['v_w_in'], 'v_b_f': out['v_b_f'], 'v_q_norm_g': out['v_q_norm_g'], 'v_k_norm_g': out['v_k_norm_g'], 'v_conv_dw_w': out['v_conv_dw_w'], 'v_conv_dw_b': out['v_conv_dw_b'], 'v_conv_ln_g': out['v_conv_ln_g'], 'v_conv_ln_b': out['v_conv_ln_b'], 'v_conv_pw_w': out['v_conv_pw_w'], 'v_pool_w': out['v_pool_w'], 'v_pool_scale': out['v_pool_scale'], 'v_w_out': out['v_w_out'], 'v_norm2_g': out['v_norm2_g'], 'v_w_up': out['v_w_up'], 'v_ffn_dw_w': out['v_ffn_dw_w'], 'v_w_down': out['v_w_down']}


def _loss(weights, diff, rest, loss_target):
    with _jax.named_scope("forward"):
        args = {**rest, TWIN_DIFF_INPUT: diff, **{k: w.astype(_WEIGHT_DTYPES[k]) for k, w in weights.items()}}
        y = _forward(args)
    with _jax.named_scope("loss_head"):
        err = _jnp.square(y.astype(_jnp.float32) - loss_target)
        return 0.5 * _jnp.sum(_jnp.mean(err, axis=-1)) if err.ndim else 0.5 * err


def _adamw(w, g, m, v):
    m = ADAM_B1 * m + (1.0 - ADAM_B1) * g
    v = ADAM_B2 * v + (1.0 - ADAM_B2) * _jnp.square(g)
    m_hat = m / (1.0 - ADAM_B1 ** ADAM_STEP)
    v_hat = v / (1.0 - ADAM_B2 ** ADAM_STEP)
    delta = -ADAM_LR * (m_hat / (_jnp.sqrt(v_hat) + ADAM_EPS) + ADAM_WD * w)
    return delta, m, v


def reference(x, norm1_g, w_in, b_f, q_norm_g, k_norm_g, conv_dw_w, conv_dw_b, conv_ln_g, conv_ln_b, conv_pw_w, pool_w, pool_scale, w_out, norm2_g, w_up, ffn_dw_w, w_down, loss_target, m_norm1_g, m_w_in, m_b_f, m_q_norm_g, m_k_norm_g, m_conv_dw_w, m_conv_dw_b, m_conv_ln_g, m_conv_ln_b, m_conv_pw_w, m_pool_w, m_pool_scale, m_w_out, m_norm2_g, m_w_up, m_ffn_dw_w, m_w_down, v_norm1_g, v_w_in, v_b_f, v_q_norm_g, v_k_norm_g, v_conv_dw_w, v_conv_dw_b, v_conv_ln_g, v_conv_ln_b, v_conv_pw_w, v_pool_w, v_pool_scale, v_w_out, v_norm2_g, v_w_up, v_ffn_dw_w, v_w_down):
    given = dict(x=x, norm1_g=norm1_g, w_in=w_in, b_f=b_f, q_norm_g=q_norm_g, k_norm_g=k_norm_g, conv_dw_w=conv_dw_w, conv_dw_b=conv_dw_b, conv_ln_g=conv_ln_g, conv_ln_b=conv_ln_b, conv_pw_w=conv_pw_w, pool_w=pool_w, pool_scale=pool_scale, w_out=w_out, norm2_g=norm2_g, w_up=w_up, ffn_dw_w=ffn_dw_w, w_down=w_down, loss_target=loss_target, m_norm1_g=m_norm1_g, m_w_in=m_w_in, m_b_f=m_b_f, m_q_norm_g=m_q_norm_g, m_k_norm_g=m_k_norm_g, m_conv_dw_w=m_conv_dw_w, m_conv_dw_b=m_conv_dw_b, m_conv_ln_g=m_conv_ln_g, m_conv_ln_b=m_conv_ln_b, m_conv_pw_w=m_conv_pw_w, m_pool_w=m_pool_w, m_pool_scale=m_pool_scale, m_w_out=m_w_out, m_norm2_g=m_norm2_g, m_w_up=m_w_up, m_ffn_dw_w=m_ffn_dw_w, m_w_down=m_w_down, v_norm1_g=v_norm1_g, v_w_in=v_w_in, v_b_f=v_b_f, v_q_norm_g=v_q_norm_g, v_k_norm_g=v_k_norm_g, v_conv_dw_w=v_conv_dw_w, v_conv_dw_b=v_conv_dw_b, v_conv_ln_g=v_conv_ln_g, v_conv_ln_b=v_conv_ln_b, v_conv_pw_w=v_conv_pw_w, v_pool_w=v_pool_w, v_pool_scale=v_pool_scale, v_w_out=v_w_out, v_norm2_g=v_norm2_g, v_w_up=v_w_up, v_ffn_dw_w=v_ffn_dw_w, v_w_down=v_w_down)
    weights = {n: given[n] for n in TWIN_WEIGHTS}
    shared = {n: given[n] for n in SHARED_INPUTS}
    per_example = {n: given[n] for n in ['x']}
    grad_fn = _jax.value_and_grad(_loss, argnums=(0, 1))

    def one_microbatch(ex, loss_target):
        ex = dict(ex)
        diff = ex.pop(TWIN_DIFF_INPUT)
        return grad_fn(weights, diff, {**shared, **ex}, loss_target)

    if N_MICROBATCH == 1:
        loss, (grad_w, grad_x) = one_microbatch(per_example, given["loss_target"])
    else:
        def body(carry, xs):
            loss_sum, grad_sum = carry
            l_k, (gw_k, gx_k) = one_microbatch(xs[0], xs[1])
            with _jax.named_scope("update"):
                return (loss_sum + l_k, _jax.tree.map(_jnp.add, grad_sum, gw_k)), gx_k

        init = (_jnp.zeros((), _jnp.float32), _jax.tree.map(_jnp.zeros_like, weights))
        (loss, grad_w), grad_x = _jax.lax.scan(body, init, (per_example, given["loss_target"]))
    with _jax.named_scope("update"):
        delta_w, new_m, new_v = {}, {}, {}
        for n in TWIN_WEIGHTS:
            delta_w[n], new_m[n], new_v[n] = _adamw(weights[n], grad_w[n], given["m_" + n], given["v_" + n])
    return (loss, grad_x, *[grad_w[n] for n in TWIN_WEIGHTS], *[delta_w[n] for n in TWIN_WEIGHTS],
            *[new_m[n] for n in TWIN_WEIGHTS], *[new_v[n] for n in TWIN_WEIGHTS])
```

```python
import functools

import jax
import jax.numpy as jnp
from jax import lax
from jax.experimental import pallas as pl
from jax.experimental.pallas import tpu as pltpu

F32 = jnp.float32
BF16 = jnp.bfloat16

N_HEADS = 8
HEAD_DIM = 64
D_ATT = N_HEADS * HEAD_DIM
D_CONV = 256
D_POOL = 256
D_QKV = 3 * D_ATT
D_PROJ_A = D_QKV + 2 * D_CONV + D_POOL
FG_ROWS = 16
CONV_WIDTH = 31
CONV_HALO = 32
POOL_WINDOWS = (2, 4, 8, 16)
POOL_GROUP = 64
POOL_HALO = 16
FFN_CONV_WIDTH = 3
FFN_HALO = 8
ATT_SCALE = HEAD_DIM ** -0.5
EPS = 1e-6
NEG = -1e30
LANES = 128

ADAM_LR = 0.001
ADAM_B1 = 0.9
ADAM_B2 = 0.999
ADAM_EPS = 1e-08
ADAM_WD = 0.01
ADAM_STEP = 10

N_CHIPS = 4
N_DEV = 8
VMEM_LIMIT_BYTES = 56 * 1024 * 1024

SHARDED = ('w_in', 'conv_dw_w', 'conv_pw_w', 'w_out', 'w_up', 'ffn_dw_w', 'w_down')
REPLICATED = ('norm1_g', 'b_f', 'q_norm_g', 'k_norm_g', 'conv_dw_b', 'conv_ln_g', 'conv_ln_b',
              'pool_w', 'pool_scale', 'norm2_g')
WEIGHTS = ('norm1_g', 'w_in', 'b_f', 'q_norm_g', 'k_norm_g', 'conv_dw_w', 'conv_dw_b', 'conv_ln_g',
           'conv_ln_b', 'conv_pw_w', 'pool_w', 'pool_scale', 'w_out', 'norm2_g', 'w_up', 'ffn_dw_w', 'w_down')
SHARD_AXIS = {'w_in': 2, 'conv_dw_w': 2, 'conv_pw_w': 1, 'w_out': 1, 'w_up': 2, 'ffn_dw_w': 2, 'w_down': 1}


def _tile(dim, pref, unit=LANES):
    if dim <= pref:
        return dim
    t = (pref // unit) * unit
    while t >= unit:
        if dim % t == 0:
            return t
        t -= unit
    raise ValueError(f'no tile for {dim} (preferred {pref})')


def _params(*sem):
    return pltpu.CompilerParams(dimension_semantics=sem, vmem_limit_bytes=VMEM_LIMIT_BYTES)


def _sigmoid(x):
    return 1.0 / (1.0 + jnp.exp(-x))


def _dot(a, b, ca, cb):
    return lax.dot_general(a, b, (((ca,), (cb,)), ((), ())), preferred_element_type=F32)


def _split3(y):
    y1 = y.astype(BF16)
    r1 = y - y1.astype(F32)
    y2 = r1.astype(BF16)
    y3 = (r1 - y2.astype(F32)).astype(BF16)
    return y1, y2, y3


def _dot3(y, e, ca=1, cb=0):
    y1, y2, y3 = _split3(y)
    return _dot(y1, e, ca, cb) + _dot(y2, e, ca, cb) + _dot(y3, e, ca, cb)


def _mm(a, b, *, name, ta=False, tb=False, res=None, out_dtype=F32, tm=512, tn=512, tk=1024):
    if ta:
        K, M = a.shape
    else:
        M, K = a.shape
    if tb:
        N, Kb = b.shape
    else:
        Kb, N = b.shape
    assert K == Kb, (a.shape, b.shape)
    tm, tn, tk = _tile(M, tm), _tile(N, tn), _tile(K, tk)
    nk = K // tk
    ca = 0 if ta else 1
    cb = 1 if tb else 0
    has_res = res is not None

    def body(*refs):
        if has_res:
            a_ref, b_ref, r_ref, o_ref = refs[:4]
            scratch = refs[4:]
        else:
            a_ref, b_ref, o_ref = refs[:3]
            r_ref = None
            scratch = refs[3:]
        p = _dot(a_ref[...].astype(BF16), b_ref[...].astype(BF16), ca, cb)
        if nk == 1:
            if has_res:
                p = p + r_ref[...]
            o_ref[...] = p.astype(out_dtype)
        else:
            acc = scratch[0]
            k = pl.program_id(2)

            @pl.when(k == 0)
            def _():
                acc[...] = p

            @pl.when(k > 0)
            def _():
                acc[...] += p

            @pl.when(k == nk - 1)
            def _():
                r = acc[...]
                if has_res:
                    r = r + r_ref[...]
                o_ref[...] = r.astype(out_dtype)

    a_spec = pl.BlockSpec((tk, tm), lambda i, j, k: (k, i)) if ta else pl.BlockSpec((tm, tk), lambda i, j, k: (i, k))
    b_spec = pl.BlockSpec((tn, tk), lambda i, j, k: (j, k)) if tb else pl.BlockSpec((tk, tn), lambda i, j, k: (k, j))
    o_spec = pl.BlockSpec((tm, tn), lambda i, j, k: (i, j))
    in_specs = [a_spec, b_spec] + ([o_spec] if has_res else [])
    args = (a, b) + ((res,) if has_res else ())
    return pl.pallas_call(
        body, name=name,
        out_shape=jax.ShapeDtypeStruct((M, N), out_dtype),
        grid=(M // tm, N // tn, nk),
        in_specs=in_specs, out_specs=o_spec,
        scratch_shapes=[pltpu.VMEM((tm, tn), F32)] if nk > 1 else [],
        compiler_params=_params('parallel', 'parallel', 'arbitrary'),
    )(*args)


def _rms_fwd(x, g, *, name, ts=512):
    S, D = x.shape
    ts = _tile(S, ts, 8)

    def body(x_ref, g_ref, o_ref):
        xv = x_ref[...]
        r = lax.rsqrt(jnp.mean(xv * xv, axis=-1, keepdims=True) + EPS)
        o_ref[...] = (xv * r * g_ref[...]).astype(BF16)

    return pl.pallas_call(
        body, name=name, out_shape=jax.ShapeDtypeStruct((S, D), BF16), grid=(S // ts,),
        in_specs=[pl.BlockSpec((ts, D), lambda i: (i, 0)), pl.BlockSpec((1, D), lambda i: (0, 0))],
        out_specs=pl.BlockSpec((ts, D), lambda i: (i, 0)),
        compiler_params=_params('parallel'),
    )(x, g)


def _rms_bwd(x, g, dh, dres, *, name, ts=512):
    S, D = x.shape
    ts = _tile(S, ts, 8)

    def body(x_ref, g_ref, dh_ref, dr_ref, dx_ref, dg_ref):
        i = pl.program_id(0)
        xv = x_ref[...]
        r = lax.rsqrt(jnp.mean(xv * xv, axis=-1, keepdims=True) + EPS)
        y = xv * r
        dh_v = dh_ref[...]
        dy = dh_v * g_ref[...]
        dx_ref[...] = dr_ref[...] + r * (dy - y * jnp.mean(dy * y, axis=-1, keepdims=True))
        part = jnp.sum(dh_v * y, axis=0, keepdims=True)

        @pl.when(i == 0)
        def _():
            dg_ref[...] = part

        @pl.when(i > 0)
        def _():
            dg_ref[...] += part

    row = pl.BlockSpec((ts, D), lambda i: (i, 0))
    vec = pl.BlockSpec((1, D), lambda i: (0, 0))
    return pl.pallas_call(
        body, name=name,
        out_shape=(jax.ShapeDtypeStruct((S, D), F32), jax.ShapeDtypeStruct((1, D), F32)),
        grid=(S // ts,), in_specs=[row, vec, row, row], out_specs=(row, vec),
        compiler_params=_params('arbitrary'),
    )(x, g, dh, dres)


def _group_ones():
    i = lax.broadcasted_iota(jnp.int32, (D_ATT, D_ATT), 0) // HEAD_DIM
    j = lax.broadcasted_iota(jnp.int32, (D_ATT, D_ATT), 1) // HEAD_DIM
    return (i == j).astype(BF16)


def _qk_prep_fwd(proj_a, qg, kg, *, name, ts=512):
    S = proj_a.shape[0]
    ts = _tile(S, ts, 16)

    def body(q_ref, k_ref, v_ref, qg_ref, kg_ref, e_ref, o_ref):
        e = e_ref[...]

        def norm(xv, gain):
            ms = _dot3(xv * xv, e) * (1.0 / HEAD_DIM)
            return xv * lax.rsqrt(ms + EPS) * gain

        o_ref[:, 0:D_ATT] = (norm(q_ref[...], qg_ref[...]) * ATT_SCALE).astype(BF16)
        o_ref[:, D_ATT:2 * D_ATT] = norm(k_ref[...], kg_ref[...]).astype(BF16)
        o_ref[:, 2 * D_ATT:3 * D_ATT] = v_ref[...].astype(BF16)

    col = lambda c: pl.BlockSpec((ts, D_ATT), lambda i: (i, c))
    vec = pl.BlockSpec((1, D_ATT), lambda i: (0, 0))
    return pl.pallas_call(
        body, name=name, out_shape=jax.ShapeDtypeStruct((S, D_QKV), BF16), grid=(S // ts,),
        in_specs=[col(0), col(1), col(2), vec, vec, pl.BlockSpec((D_ATT, D_ATT), lambda i: (0, 0))],
        out_specs=pl.BlockSpec((ts, D_QKV), lambda i: (i, 0)),
        compiler_params=_params('parallel'),
    )(proj_a, proj_a, proj_a, qg, kg, _group_ones())


def _qk_prep_bwd(proj_a, dq, dk, dv, qg, kg, *, name, ts=512):
    S = proj_a.shape[0]
    ts = _tile(S, ts, 16)

    def body(q_ref, k_ref, dq_ref, dk_ref, dv_ref, qg_ref, kg_ref, e_ref, o_ref, dqg_ref, dkg_ref):
        i = pl.program_id(0)
        e = e_ref[...]

        def norm_bwd(xv, dn, gain, scale):
            ms = _dot3(xv * xv, e) * (1.0 / HEAD_DIM)
            r = lax.rsqrt(ms + EPS)
            y = xv * r
            dy = dn * (gain * scale)
            mean = _dot3(dy * y, e) * (1.0 / HEAD_DIM)
            return r * (dy - y * mean), jnp.sum(dn * y, axis=0, keepdims=True) * scale

        dq_raw, dqg = norm_bwd(q_ref[...], dq_ref[...], qg_ref[...], ATT_SCALE)
        dk_raw, dkg = norm_bwd(k_ref[...], dk_ref[...], kg_ref[...], 1.0)
        o_ref[:, 0:D_ATT] = dq_raw.astype(BF16)
        o_ref[:, D_ATT:2 * D_ATT] = dk_raw.astype(BF16)
        o_ref[:, 2 * D_ATT:3 * D_ATT] = dv_ref[...].astype(BF16)

        @pl.when(i == 0)
        def _():
            dqg_ref[...] = dqg
            dkg_ref[...] = dkg

        @pl.when(i > 0)
        def _():
            dqg_ref[...] += dqg
            dkg_ref[...] += dkg

    col = lambda c: pl.BlockSpec((ts, D_ATT), lambda i: (i, c))
    vec = pl.BlockSpec((1, D_ATT), lambda i: (0, 0))
    return pl.pallas_call(
        body, name=name,
        out_shape=(jax.ShapeDtypeStruct((S, D_QKV), BF16), jax.ShapeDtypeStruct((1, D_ATT), F32),
                   jax.ShapeDtypeStruct((1, D_ATT), F32)),
        grid=(S // ts,),
        in_specs=[col(0), col(1), col(0), col(0), col(0), vec, vec, pl.BlockSpec((D_ATT, D_ATT), lambda i: (0, 0))],
        out_specs=(pl.BlockSpec((ts, D_QKV), lambda i: (i, 0)), vec, vec),
        compiler_params=_params('arbitrary'),
    )(proj_a, proj_a, dq, dk, dv, qg, kg, _group_ones())


def _tri_ones(upper):
    i = lax.broadcasted_iota(jnp.int32, (LANES, LANES), 0)
    j = lax.broadcasted_iota(jnp.int32, (LANES, LANES), 1)
    return ((i <= j) if upper else (i >= j)).astype(BF16)


def _forget_fwd(z_raw, b_col, *, name):
    R, S = z_raw.shape
    nb = S // LANES

    def body(z_ref, b_ref, u_ref, f_ref):
        u = u_ref[...]
        carry = jnp.zeros((R, 1), F32)
        for j in range(nb):
            z = z_ref[:, j * LANES:(j + 1) * LANES] + b_ref[...]
            logf = jnp.minimum(z, 0.0) - jnp.log(1.0 + jnp.exp(-jnp.abs(z)))
            f_ref[:, j * LANES:(j + 1) * LANES] = _dot3(logf, u) + carry
            carry = carry + jnp.sum(logf, axis=1, keepdims=True)

    return pl.pallas_call(
        body, name=name, out_shape=jax.ShapeDtypeStruct((R, S), F32),
        compiler_params=pltpu.CompilerParams(vmem_limit_bytes=VMEM_LIMIT_BYTES),
    )(z_raw, b_col, _tri_ones(True))


def _forget_bwd(z_raw, b_col, df, *, name):
    R, S = z_raw.shape
    nb = S // LANES

    def body(z_ref, b_ref, df_ref, l_ref, dz_ref, db_ref):
        low = l_ref[...]
        carry = jnp.zeros((R, 1), F32)
        db = jnp.zeros((R, 1), F32)
        for j in reversed(range(nb)):
            d = df_ref[:, j * LANES:(j + 1) * LANES]
            dlogf = _dot3(d, low) + carry
            carry = carry + jnp.sum(d, axis=1, keepdims=True)
            z = z_ref[:, j * LANES:(j + 1) * LANES] + b_ref[...]
            dz = dlogf * _sigmoid(-z)
            dz_ref[:, j * LANES:(j + 1) * LANES] = dz
            db = db + jnp.sum(dz, axis=1, keepdims=True)
        db_ref[...] = db

    return pl.pallas_call(
        body, name=name,
        out_shape=(jax.ShapeDtypeStruct((R, S), F32), jax.ShapeDtypeStruct((R, 1), F32)),
        compiler_params=pltpu.CompilerParams(vmem_limit_bytes=VMEM_LIMIT_BYTES),
    )(z_raw, b_col, df, _tri_ones(False))


def _head_mask(hh):
    lane = lax.broadcasted_iota(jnp.int32, (1, LANES), 1)
    return (lane // HEAD_DIM) == hh


def _causal(s, qi, ki, t):
    rows = qi * t + lax.broadcasted_iota(jnp.int32, (t, t), 0)
    cols = ki * t + lax.broadcasted_iota(jnp.int32, (t, t), 1)
    return jnp.where(cols <= rows, s, NEG)


def _attn_fwd(qkv, f3, *, name, t=512):
    S = qkv.shape[0]
    t = _tile(S, t)
    n = S // t
    npair = N_HEADS // 2

    def body(q_ref, k_ref, v_ref, f_ref, o_ref, lse_ref, m_s, l_s, acc_s):
        qi, ki = pl.program_id(1), pl.program_id(2)

        @pl.when(ki == 0)
        def _():
            m_s[...] = jnp.full(m_s.shape, NEG, F32)
            l_s[...] = jnp.zeros(l_s.shape, F32)
            acc_s[...] = jnp.zeros(acc_s.shape, F32)

        @pl.when(ki <= qi)
        def _():
            q, k, v = q_ref[...], k_ref[...], v_ref[...]
            for hh in range(2):
                msk = _head_mask(hh)
                qm = jnp.where(msk, q, jnp.zeros_like(q))
                vm = jnp.where(msk, v, jnp.zeros_like(v))
                s = _causal(_dot(qm, k, 1, 1) - f_ref[0, hh:hh + 1, :], qi, ki, t)
                m_prev = m_s[hh]
                m_new = jnp.maximum(m_prev, jnp.max(s, axis=1, keepdims=True))
                alpha = jnp.exp(m_prev - m_new)
                p = jnp.exp(s - m_new)
                l_s[hh] = alpha * l_s[hh] + jnp.sum(p, axis=1, keepdims=True)
                acc_s[hh] = alpha * acc_s[hh] + _dot(p.astype(BF16), vm, 1, 0)
                m_s[hh] = m_new

        @pl.when(ki == qi)
        def _():
            o_ref[...] = acc_s[0] / l_s[0] + acc_s[1] / l_s[1]
            lse0 = m_s[0] + jnp.log(l_s[0])
            lse1 = m_s[1] + jnp.log(l_s[1])
            lse_ref[...] = jnp.where(_head_mask(0), lse0, lse1)

    return pl.pallas_call(
        body, name=name,
        out_shape=(jax.ShapeDtypeStruct((S, D_ATT), F32), jax.ShapeDtypeStruct((S, D_ATT), F32)),
        grid=(npair, n, n),
        in_specs=[pl.BlockSpec((t, LANES), lambda h, i, j: (i, h)),
                  pl.BlockSpec((t, LANES), lambda h, i, j: (jnp.minimum(i, j), npair + h)),
                  pl.BlockSpec((t, LANES), lambda h, i, j: (jnp.minimum(i, j), 2 * npair + h)),
                  pl.BlockSpec((1, 2, t), lambda h, i, j: (h, 0, jnp.minimum(i, j)))],
        out_specs=(pl.BlockSpec((t, LANES), lambda h, i, j: (i, h)),
                   pl.BlockSpec((t, LANES), lambda h, i, j: (i, h))),
        scratch_shapes=[pltpu.VMEM((2, t, 1), F32), pltpu.VMEM((2, t, 1), F32), pltpu.VMEM((2, t, LANES), F32)],
        compiler_params=_params('parallel', 'parallel', 'arbitrary'),
    )(qkv, qkv, qkv, f3)


def _attn_bwd(qkv, f3, att, lse, d_mix, *, name, t=512):
    S = qkv.shape[0]
    t = _tile(S, t)
    n = S // t
    npair = N_HEADS // 2

    def body(q_ref, k_ref, v_ref, f_ref, o_ref, lse_ref, do_ref, dq_ref, dk_ref, dv_ref, df_ref, dr_ref, dk_s, dv_s, df_s):
        ki, qi = pl.program_id(1), pl.program_id(2)

        @pl.when(qi == ki)
        def _():
            dk_s[...] = jnp.zeros(dk_s.shape, F32)
            dv_s[...] = jnp.zeros(dv_s.shape, F32)
            df_s[...] = jnp.zeros(df_s.shape, F32)

        @pl.when(qi >= ki)
        def _():
            q, k, v = q_ref[...], k_ref[...], v_ref[...]
            do, o, lse = do_ref[...], o_ref[...], lse_ref[...]
            dq_blk = jnp.zeros((t, LANES), F32)
            dr_blk = jnp.zeros((t, LANES), F32)
            for hh in range(2):
                msk = _head_mask(hh)
                qm = jnp.where(msk, q, jnp.zeros_like(q))
                km = jnp.where(msk, k, jnp.zeros_like(k))
                dom = jnp.where(msk, do, 0.0).astype(BF16)
                s = _causal(_dot(qm, k, 1, 1) - f_ref[0, hh:hh + 1, :], qi, ki, t)
                lse_h = jnp.max(jnp.where(msk, lse, NEG), axis=1, keepdims=True)
                p = jnp.exp(s - lse_h)
                dp = _dot(dom, v, 1, 1)
                delta = jnp.sum(dom.astype(F32) * o, axis=1, keepdims=True)
                ds = p * (dp - delta)
                dsb = ds.astype(BF16)
                dv_s[...] += _dot(p.astype(BF16), dom, 0, 0)
                dk_s[...] += _dot(dsb, qm, 0, 0)
                dq_blk = dq_blk + _dot(dsb, km, 1, 0)
                df_s[hh] -= jnp.sum(ds, axis=0, keepdims=True)
                dr_blk = dr_blk + jnp.where(msk, jnp.sum(ds, axis=1, keepdims=True), 0.0)
            rows = pl.ds(pl.multiple_of(qi * t, t), t)

            @pl.when(ki == 0)
            def _():
                dq_ref[rows, :] = dq_blk
                dr_ref[rows, :] = dr_blk

            @pl.when(ki > 0)
            def _():
                dq_ref[rows, :] += dq_blk
                dr_ref[rows, :] += dr_blk

        @pl.when(qi == n - 1)
        def _():
            dk_ref[...] = dk_s[...]
            dv_ref[...] = dv_s[...]
            df_ref[0, 0:1, :] = df_s[0]
            df_ref[0, 1:2, :] = df_s[1]

    qrow = lambda h, j, i: (jnp.maximum(i, j), h)
    return pl.pallas_call(
        body, name=name,
        out_shape=(jax.ShapeDtypeStruct((S, D_ATT), F32), jax.ShapeDtypeStruct((S, D_ATT), F32),
                   jax.ShapeDtypeStruct((S, D_ATT), F32), jax.ShapeDtypeStruct((npair, 2, S), F32),
                   jax.ShapeDtypeStruct((S, D_ATT), F32)),
        grid=(npair, n, n),
        in_specs=[pl.BlockSpec((t, LANES), qrow),
                  pl.BlockSpec((t, LANES), lambda h, j, i: (j, npair + h)),
                  pl.BlockSpec((t, LANES), lambda h, j, i: (j, 2 * npair + h)),
                  pl.BlockSpec((1, 2, t), lambda h, j, i: (h, 0, j)),
                  pl.BlockSpec((t, LANES), qrow),
                  pl.BlockSpec((t, LANES), qrow),
                  pl.BlockSpec((t, LANES), qrow)],
        out_specs=(pl.BlockSpec((S, LANES), lambda h, j, i: (0, h)),
                   pl.BlockSpec((t, LANES), lambda h, j, i: (j, h)),
                   pl.BlockSpec((t, LANES), lambda h, j, i: (j, h)),
                   pl.BlockSpec((1, 2, t), lambda h, j, i: (h, 0, j)),
                   pl.BlockSpec((S, LANES), lambda h, j, i: (0, h))),
        scratch_shapes=[pltpu.VMEM((t, LANES), F32), pltpu.VMEM((t, LANES), F32), pltpu.VMEM((2, 1, t), F32)],
        compiler_params=_params('parallel', 'arbitrary', 'arbitrary'),
    )(qkv, qkv, qkv, f3, att, lse, d_mix)


A_COL = D_QKV // D_CONV
B_COL = A_COL + 1
P_COL = B_COL + 1


def _layer_norm_stats(c):
    mu = jnp.mean(c, axis=-1, keepdims=True)
    xc = c - mu
    rstd = lax.rsqrt(jnp.mean(xc * xc, axis=-1, keepdims=True) + EPS)
    return xc * rstd, rstd


def _glu_into(buf, a_ref, b_ref, ah_ref, bh_ref, first, ts):
    halo = ah_ref[...] * _sigmoid(bh_ref[...])
    buf[0:CONV_HALO, :] = jnp.where(first, 0.0, halo)
    buf[CONV_HALO:CONV_HALO + ts, :] = a_ref[...] * _sigmoid(b_ref[...])


def _dwconv(buf, w_ref, ts):
    off = CONV_HALO - (CONV_WIDTH - 1)
    acc = w_ref[0:1, :] * buf[pl.ds(off, ts), :]
    for k in range(1, CONV_WIDTH):
        acc = acc + w_ref[k:k + 1, :] * buf[pl.ds(off + k, ts), :]
    return acc


def _conv_specs(ts, tmap):
    hb = ts // CONV_HALO
    cur = lambda c: pl.BlockSpec((ts, D_CONV), lambda i: (tmap(i), c))
    halo = lambda c: pl.BlockSpec((CONV_HALO, D_CONV), lambda i: (jnp.maximum(tmap(i) * hb - 1, 0), c))
    return cur, halo


def _conv_fwd(proj_a, dw_w, dw_b, ln_g, ln_b, pw_w, *, name, ts=512):
    S = proj_a.shape[0]
    ts = _tile(S, ts, CONV_HALO)

    def body(a_ref, b_ref, ah_ref, bh_ref, w_ref, wb_ref, g_ref, bb_ref, pw_ref, o_ref, buf):
        _glu_into(buf, a_ref, b_ref, ah_ref, bh_ref, pl.program_id(0) == 0, ts)
        c = _dwconv(buf, w_ref, ts) + wb_ref[...]
        yhat, _ = _layer_norm_stats(c)
        y = yhat * g_ref[...] + bb_ref[...]
        hs = y * _sigmoid(y)
        o_ref[...] = _dot(hs.astype(BF16), pw_ref[...], 1, 0).astype(BF16)

    cur, halo = _conv_specs(ts, lambda i: i)
    vec = pl.BlockSpec((1, D_CONV), lambda i: (0, 0))
    return pl.pallas_call(
        body, name=name, out_shape=jax.ShapeDtypeStruct((S, D_CONV), BF16), grid=(S // ts,),
        in_specs=[cur(A_COL), cur(B_COL), halo(A_COL), halo(B_COL),
                  pl.BlockSpec((CONV_HALO, D_CONV), lambda i: (0, 0)), vec, vec, vec,
                  pl.BlockSpec((D_CONV, D_CONV), lambda i: (0, 0))],
        out_specs=pl.BlockSpec((ts, D_CONV), lambda i: (i, 0)),
        scratch_shapes=[pltpu.VMEM((CONV_HALO + ts, D_CONV), F32)],
        compiler_params=_params('parallel'),
    )(proj_a, proj_a, proj_a, proj_a, dw_w, dw_b, ln_g, ln_b, pw_w)


def _conv_bwd(proj_a, d_mix, dw_w, dw_b, ln_g, ln_b, pw_w, *, name, ts=512):
    S = proj_a.shape[0]
    ts = _tile(S, ts, CONV_HALO)
    n = S // ts
    d_col = D_ATT // D_CONV

    def body(a_ref, b_ref, ah_ref, bh_ref, dy_ref, w_ref, wb_ref, g_ref, bb_ref, pw_ref,
             o_ref, dw_ref, dwb_ref, dg_ref, dbb_ref, dpw_ref, buf, dcbuf):
        i = pl.program_id(0)
        _glu_into(buf, a_ref, b_ref, ah_ref, bh_ref, i == n - 1, ts)
        c = _dwconv(buf, w_ref, ts) + wb_ref[...]
        yhat, rstd = _layer_norm_stats(c)
        y = yhat * g_ref[...] + bb_ref[...]
        sg = _sigmoid(y)
        hs = y * sg
        dout = dy_ref[...].astype(BF16)
        d_hs = _dot(dout, pw_ref[...], 1, 1)
        d_y = d_hs * (sg * (1.0 + y * (1.0 - sg)))
        d_yhat = d_y * g_ref[...]
        d_c = rstd * (d_yhat - jnp.mean(d_yhat, axis=-1, keepdims=True)
                      - yhat * jnp.mean(d_yhat * yhat, axis=-1, keepdims=True))

        @pl.when(i == 0)
        def _():
            dcbuf[ts:ts + CONV_HALO, :] = jnp.zeros((CONV_HALO, D_CONV), F32)
            dw_ref[...] = jnp.zeros(dw_ref.shape, F32)
            dwb_ref[...] = jnp.zeros(dwb_ref.shape, F32)
            dg_ref[...] = jnp.zeros(dg_ref.shape, F32)
            dbb_ref[...] = jnp.zeros(dbb_ref.shape, F32)
            dpw_ref[...] = jnp.zeros(dpw_ref.shape, F32)

        dcbuf[0:ts, :] = d_c
        dpw_ref[...] += _dot(hs.astype(BF16), dout, 0, 0)
        dg_ref[...] += jnp.sum(d_y * yhat, axis=0, keepdims=True)
        dbb_ref[...] += jnp.sum(d_y, axis=0, keepdims=True)
        dwb_ref[...] += jnp.sum(d_c, axis=0, keepdims=True)
        off = CONV_HALO - (CONV_WIDTH - 1)
        d_h = jnp.zeros((ts, D_CONV), F32)
        for k in range(CONV_WIDTH):
            d_h = d_h + w_ref[k:k + 1, :] * dcbuf[pl.ds(CONV_WIDTH - 1 - k, ts), :]
            dw_ref[k:k + 1, :] += jnp.sum(d_c * buf[pl.ds(off + k, ts), :], axis=0, keepdims=True)
        dcbuf[ts:ts + CONV_HALO, :] = d_c[0:CONV_HALO, :]
        a, sb = a_ref[...], _sigmoid(b_ref[...])
        o_ref[:, 0:D_CONV] = (d_h * sb).astype(BF16)
        o_ref[:, D_CONV:2 * D_CONV] = (d_h * a * sb * (1.0 - sb)).astype(BF16)

    rev = lambda i: n - 1 - i
    cur, halo = _conv_specs(ts, rev)
    vec = pl.BlockSpec((1, D_CONV), lambda i: (0, 0))
    wspec = pl.BlockSpec((CONV_HALO, D_CONV), lambda i: (0, 0))
    sq = pl.BlockSpec((D_CONV, D_CONV), lambda i: (0, 0))
    return pl.pallas_call(
        body, name=name,
        out_shape=(jax.ShapeDtypeStruct((S, 2 * D_CONV), BF16), jax.ShapeDtypeStruct((CONV_HALO, D_CONV), F32),
                   jax.ShapeDtypeStruct((1, D_CONV), F32), jax.ShapeDtypeStruct((1, D_CONV), F32),
                   jax.ShapeDtypeStruct((1, D_CONV), F32), jax.ShapeDtypeStruct((D_CONV, D_CONV), F32)),
        grid=(n,),
        in_specs=[cur(A_COL), cur(B_COL), halo(A_COL), halo(B_COL),
                  pl.BlockSpec((ts, D_CONV), lambda i: (rev(i), d_col)), wspec, vec, vec, vec, sq],
        out_specs=(pl.BlockSpec((ts, 2 * D_CONV), lambda i: (rev(i), 0)), wspec, vec, vec, vec, sq),
        scratch_shapes=[pltpu.VMEM((CONV_HALO + ts, D_CONV), F32), pltpu.VMEM((ts + CONV_HALO, D_CONV), F32)],
        compiler_params=_params('arbitrary'),
    )(proj_a, proj_a, proj_a, proj_a, d_mix, dw_w, dw_b, ln_g, ln_b, pw_w)


def _pool_window():
    lane = lax.broadcasted_iota(jnp.int32, (1, D_POOL), 1)
    w = jnp.full((1, D_POOL), POOL_WINDOWS[0], jnp.int32)
    for g in range(1, len(POOL_WINDOWS)):
        w = jnp.where(lane // POOL_GROUP == g, POOL_WINDOWS[g], w)
    return w


def _pool_diff(buf, u_ref, uh_ref, first, tile, ts):
    buf[0:POOL_HALO, :] = jnp.where(first, 0.0, uh_ref[...])
    u = u_ref[...]
    buf[POOL_HALO:POOL_HALO + ts, :] = u
    wl = _pool_window()
    acc = u
    for j in range(1, max(POOL_WINDOWS)):
        acc = acc + jnp.where(j < wl, buf[pl.ds(POOL_HALO - j, ts), :], 0.0)
    pos = tile * ts + lax.broadcasted_iota(jnp.int32, (ts, 1), 0)
    cnt = jnp.minimum(pos + 1, wl).astype(F32)
    return acc / cnt - u, cnt


def _pool_specs(ts, tmap):
    hb = ts // POOL_HALO
    cur = pl.BlockSpec((ts, D_POOL), lambda i: (tmap(i), P_COL))
    halo = pl.BlockSpec((POOL_HALO, D_POOL), lambda i: (jnp.maximum(tmap(i) * hb - 1, 0), P_COL))
    return cur, halo


def _pool_fwd(proj_a, wbd, scale, *, name, ts=512):
    S = proj_a.shape[0]
    ts = _tile(S, ts, POOL_HALO)

    def body(u_ref, uh_ref, w_ref, s_ref, o_ref, buf):
        i = pl.program_id(0)
        d, _ = _pool_diff(buf, u_ref, uh_ref, i == 0, i, ts)
        o_ref[...] = (_dot(d.astype(BF16), w_ref[...], 1, 0) * s_ref[...]).astype(BF16)

    cur, halo = _pool_specs(ts, lambda i: i)
    return pl.pallas_call(
        body, name=name, out_shape=jax.ShapeDtypeStruct((S, D_POOL), BF16), grid=(S // ts,),
        in_specs=[cur, halo, pl.BlockSpec((D_POOL, D_POOL), lambda i: (0, 0)), pl.BlockSpec((1, D_POOL), lambda i: (0, 0))],
        out_specs=pl.BlockSpec((ts, D_POOL), lambda i: (i, 0)),
        scratch_shapes=[pltpu.VMEM((POOL_HALO + ts, D_POOL), F32)],
        compiler_params=_params('parallel'),
    )(proj_a, proj_a, wbd, scale)


def _pool_bwd(proj_a, d_mix, wbd, scale, *, name, ts=512):
    S = proj_a.shape[0]
    ts = _tile(S, ts, POOL_HALO)
    n = S // ts
    d_col = (D_ATT + D_CONV) // D_POOL

    def body(u_ref, uh_ref, dy_ref, w_ref, s_ref, o_ref, dw_ref, ds_ref, buf, ebuf):
        i = pl.program_id(0)
        tile = n - 1 - i
        d, cnt = _pool_diff(buf, u_ref, uh_ref, tile == 0, tile, ts)
        db = d.astype(BF16)
        ypre = _dot(db, w_ref[...], 1, 0)
        dout = dy_ref[...]
        d_y = (dout * s_ref[...]).astype(BF16)
        d_d = _dot(d_y, w_ref[...], 1, 1)

        @pl.when(i == 0)
        def _():
            ebuf[ts:ts + POOL_HALO, :] = jnp.zeros((POOL_HALO, D_POOL), F32)
            dw_ref[...] = jnp.zeros(dw_ref.shape, F32)
            ds_ref[...] = jnp.zeros(ds_ref.shape, F32)

        dw_ref[...] += _dot(db, d_y, 0, 0)
        ds_ref[...] += jnp.sum(dout * ypre, axis=0, keepdims=True)
        e = d_d / cnt
        ebuf[0:ts, :] = e
        wl = _pool_window()
        acc = e
        for j in range(1, max(POOL_WINDOWS)):
            acc = acc + jnp.where(j < wl, ebuf[pl.ds(j, ts), :], 0.0)
        ebuf[ts:ts + POOL_HALO, :] = e[0:POOL_HALO, :]
        o_ref[...] = (acc - d_d).astype(BF16)

    rev = lambda i: n - 1 - i
    cur, halo = _pool_specs(ts, rev)
    sq = pl.BlockSpec((D_POOL, D_POOL), lambda i: (0, 0))
    vec = pl.BlockSpec((1, D_POOL), lambda i: (0, 0))
    return pl.pallas_call(
        body, name=name,
        out_shape=(jax.ShapeDtypeStruct((S, D_POOL), BF16), jax.ShapeDtypeStruct((D_POOL, D_POOL), F32),
                   jax.ShapeDtypeStruct((1, D_POOL), F32)),
        grid=(n,),
        in_specs=[cur, halo, pl.BlockSpec((ts, D_POOL), lambda i: (rev(i), d_col)), sq, vec],
        out_specs=(pl.BlockSpec((ts, D_POOL), lambda i: (rev(i), 0)), sq, vec),
        scratch_shapes=[pltpu.VMEM((POOL_HALO + ts, D_POOL), F32), pltpu.VMEM((ts + POOL_HALO, D_POOL), F32)],
        compiler_params=_params('arbitrary'),
    )(proj_a, proj_a, d_mix, wbd, scale)


def _ffn_conv(buf, w_ref, ts):
    off = FFN_HALO - (FFN_CONV_WIDTH - 1)
    acc = w_ref[0:1, :] * buf[pl.ds(off, ts), :]
    for k in range(1, FFN_CONV_WIDTH):
        acc = acc + w_ref[k:k + 1, :] * buf[pl.ds(off + k, ts), :]
    return acc


def _ffn_specs(ts, tc2, tmap):
    hb = ts // FFN_HALO
    cur = pl.BlockSpec((ts, tc2), lambda c, i: (tmap(i), c))
    halo = pl.BlockSpec((FFN_HALO, tc2), lambda c, i: (jnp.maximum(tmap(i) * hb - 1, 0), c))
    wspec = pl.BlockSpec((FFN_HALO, tc2), lambda c, i: (0, c))
    return cur, halo, wspec


def _ffn_act_fwd(up, w, *, name, ts=256):
    S, F2 = up.shape
    tc = F2 // 4
    ts = _tile(S, ts, FFN_HALO)

    def body(x_ref, xh_ref, w_ref, o_ref, buf):
        buf[0:FFN_HALO, :] = jnp.where(pl.program_id(1) == 0, 0.0, xh_ref[...])
        buf[FFN_HALO:FFN_HALO + ts, :] = x_ref[...]
        cv = _ffn_conv(buf, w_ref, ts)
        gate, val = cv[:, 0:tc], cv[:, tc:2 * tc]
        o_ref[...] = (gate * _sigmoid(gate) * val).astype(BF16)

    cur, halo, wspec = _ffn_specs(ts, 2 * tc, lambda i: i)
    return pl.pallas_call(
        body, name=name, out_shape=jax.ShapeDtypeStruct((S, F2 // 2), BF16), grid=(2, S // ts),
        in_specs=[cur, halo, wspec],
        out_specs=pl.BlockSpec((ts, tc), lambda c, i: (i, c)),
        scratch_shapes=[pltpu.VMEM((FFN_HALO + ts, 2 * tc), F32)],
        compiler_params=_params('parallel', 'parallel'),
    )(up, up, w)


def _ffn_act_bwd(up, d_act, w, *, name, ts=256):
    S, F2 = up.shape
    tc = F2 // 4
    ts = _tile(S, ts, FFN_HALO)
    n = S // ts

    def body(x_ref, xh_ref, da_ref, w_ref, o_ref, dw_ref, buf, dcbuf):
        i = pl.program_id(1)
        buf[0:FFN_HALO, :] = jnp.where(i == n - 1, 0.0, xh_ref[...])
        buf[FFN_HALO:FFN_HALO + ts, :] = x_ref[...]
        cv = _ffn_conv(buf, w_ref, ts)
        gate, val = cv[:, 0:tc], cv[:, tc:2 * tc]
        sg = _sigmoid(gate)
        da = da_ref[...]
        d_gate = da * val * (sg * (1.0 + gate * (1.0 - sg)))
        d_val = da * (gate * sg)

        @pl.when(i == 0)
        def _():
            dcbuf[ts:ts + FFN_HALO, :] = jnp.zeros((FFN_HALO, 2 * tc), F32)
            dw_ref[...] = jnp.zeros(dw_ref.shape, F32)

        dcbuf[0:ts, 0:tc] = d_gate
        dcbuf[0:ts, tc:2 * tc] = d_val
        d_c = dcbuf[0:ts, :]
        off = FFN_HALO - (FFN_CONV_WIDTH - 1)
        d_x = jnp.zeros((ts, 2 * tc), F32)
        for k in range(FFN_CONV_WIDTH):
            d_x = d_x + w_ref[k:k + 1, :] * dcbuf[pl.ds(FFN_CONV_WIDTH - 1 - k, ts), :]
            dw_ref[k:k + 1, :] += jnp.sum(d_c * buf[pl.ds(off + k, ts), :], axis=0, keepdims=True)
        dcbuf[ts:ts + FFN_HALO, :] = d_c[0:FFN_HALO, :]
        o_ref[...] = d_x.astype(BF16)

    rev = lambda i: n - 1 - i
    cur, halo, wspec = _ffn_specs(ts, 2 * tc, rev)
    return pl.pallas_call(
        body, name=name,
        out_shape=(jax.ShapeDtypeStruct((S, F2), BF16), jax.ShapeDtypeStruct((FFN_HALO, F2), F32)),
        grid=(2, n),
        in_specs=[cur, halo, pl.BlockSpec((ts, tc), lambda c, i: (rev(i), c)), wspec],
        out_specs=(cur, wspec),
        scratch_shapes=[pltpu.VMEM((FFN_HALO + ts, 2 * tc), F32), pltpu.VMEM((ts + FFN_HALO, 2 * tc), F32)],
        compiler_params=_params('parallel', 'arbitrary'),
    )(up, up, d_act, w)


def _loss_head(y, target, *, name, ts=512):
    S, D = y.shape
    ts = _tile(S, ts, 8)

    def body(y_ref, t_ref, l_ref, dy_ref):
        i = pl.program_id(0)
        err = y_ref[...] - t_ref[...]
        dy_ref[...] = err * (1.0 / D)
        part = jnp.sum(jnp.sum(err * err, axis=1, keepdims=True), axis=0, keepdims=True) * (0.5 / D)

        @pl.when(i == 0)
        def _():
            l_ref[...] = part

        @pl.when(i > 0)
        def _():
            l_ref[...] += part

    row = pl.BlockSpec((ts, D), lambda i: (i, 0))
    return pl.pallas_call(
        body, name=name,
        out_shape=(jax.ShapeDtypeStruct((1, 1), F32), jax.ShapeDtypeStruct((S, D), F32)),
        grid=(S // ts,), in_specs=[row, row], out_specs=(pl.BlockSpec((1, 1), lambda i: (0, 0)), row),
        compiler_params=_params('arbitrary'),
    )(y, target)


def _pair_cols(w):
    lead, f2 = w.shape[:-1], w.shape[-1]
    return w.reshape(lead + (2, 2, f2 // 4)).swapaxes(-3, -2).reshape(lead + (f2,))


def _pad_rows(w, rows):
    return jnp.concatenate([w, jnp.zeros((rows - w.shape[0],) + w.shape[1:], w.dtype)], axis=0)


def _block_diag(pool_w):
    g = pool_w.shape[0]
    rows = [jnp.concatenate([pool_w[i] if i == j else jnp.zeros_like(pool_w[i]) for j in range(g)], axis=1) for i in range(g)]
    return jnp.concatenate(rows, axis=0)


def _layer_weights(w, l):
    w_in = w['w_in'][l]
    fg0 = D_QKV
    fg1 = D_QKV + N_HEADS
    return dict(
        norm1_g=w['norm1_g'][l][None, :],
        w_a=jnp.concatenate([w_in[:, :fg0], w_in[:, fg1:]], axis=1),
        w_fg_t=_pad_rows(w_in[:, fg0:fg1].T, FG_ROWS),
        b_col=_pad_rows(w['b_f'][l][:, None], FG_ROWS),
        qg=jnp.tile(w['q_norm_g'][l], N_HEADS)[None, :],
        kg=jnp.tile(w['k_norm_g'][l], N_HEADS)[None, :],
        dw_w=_pad_rows(w['conv_dw_w'][l].astype(F32), CONV_HALO),
        dw_b=w['conv_dw_b'][l][None, :], ln_g=w['conv_ln_g'][l][None, :], ln_b=w['conv_ln_b'][l][None, :],
        pw_w=w['conv_pw_w'][l],
        wbd=_block_diag(w['pool_w'][l]).astype(BF16),
        pool_scale=w['pool_scale'][l][None, :],
        w_out=w['w_out'][l],
        norm2_g=w['norm2_g'][l][None, :],
        w_up=_pair_cols(w['w_up'][l]),
        ffn_w=_pad_rows(_pair_cols(w['ffn_dw_w'][l].astype(F32)), FFN_HALO),
        w_down=w['w_down'][l],
    )


def _layer_fwd(x, p, l):
    n = lambda s: f'l{l}_{s}'
    S = x.shape[0]
    h = _rms_fwd(x, p['norm1_g'], name=n('norm1'))
    proj_a = _mm(h, p['w_a'], name=n('proj_a'), tn=768)
    z_raw = _mm(p['w_fg_t'], h, tb=True, name=n('proj_fg'))
    qkv = _qk_prep_fwd(proj_a, p['qg'], p['kg'], name=n('qk_norm'))
    f_cum = _forget_fwd(z_raw, p['b_col'], name=n('forget'))
    f3 = f_cum[:N_HEADS].reshape(N_HEADS // 2, 2, S)
    att, lse = _attn_fwd(qkv, f3, name=n('attn'))
    conv = _conv_fwd(proj_a, p['dw_w'], p['dw_b'], p['ln_g'], p['ln_b'], p['pw_w'], name=n('conv'))
    pool = _pool_fwd(proj_a, p['wbd'], p['pool_scale'], name=n('pool'))
    mix = jnp.concatenate([att.astype(BF16), conv, pool], axis=1)
    x1 = _mm(mix, p['w_out'], res=x, name=n('out_proj'), tn=1024)
    h2 = _rms_fwd(x1, p['norm2_g'], name=n('norm2'))
    up = _mm(h2, p['w_up'], name=n('up_proj'), tn=1408)
    act = _ffn_act_fwd(up, p['ffn_w'], name=n('ffn_act'))
    x2 = _mm(act, p['w_down'], res=x1, name=n('down_proj'), tn=1024, tk=1408)
    saved = dict(x=x, h=h, proj_a=proj_a, z_raw=z_raw, qkv=qkv, f3=f3, att=att, lse=lse, mix=mix, x1=x1, h2=h2, up=up, act=act)
    return x2, saved


def _layer_bwd(dx2, p, s, l):
    n = lambda t: f'l{l}_{t}'
    S = dx2.shape[0]
    g = {}
    d_act = _mm(dx2, p['w_down'], tb=True, name=n('d_act'), tn=1408)
    g['w_down'] = _mm(s['act'], dx2, ta=True, name=n('d_w_down'), tm=1408, tn=1024)
    d_up, d_ffn_w = _ffn_act_bwd(s['up'], d_act, p['ffn_w'], name=n('ffn_act_bwd'))
    g['ffn_dw_w'] = _pair_cols(d_ffn_w[:FFN_CONV_WIDTH])
    d_h2 = _mm(d_up, p['w_up'], tb=True, name=n('d_h2'), tn=1024, tk=1408)
    g['w_up'] = _pair_cols(_mm(s['h2'], d_up, ta=True, name=n('d_w_up'), tm=1024, tn=1408))
    dx1, dg2 = _rms_bwd(s['x1'], p['norm2_g'], d_h2, dx2, name=n('norm2_bwd'))
    g['norm2_g'] = dg2[0]
    d_mix = _mm(dx1, p['w_out'], tb=True, name=n('d_mix'), tn=1024)
    g['w_out'] = _mm(s['mix'], dx1, ta=True, name=n('d_w_out'), tm=1024, tn=1024)
    d_pool_in, d_wbd, d_scale = _pool_bwd(s['proj_a'], d_mix, p['wbd'], p['pool_scale'], name=n('pool_bwd'))
    g['pool_w'] = jnp.stack([d_wbd[i * POOL_GROUP:(i + 1) * POOL_GROUP, i * POOL_GROUP:(i + 1) * POOL_GROUP]
                             for i in range(len(POOL_WINDOWS))])
    g['pool_scale'] = d_scale[0]
    d_conv_in, d_dw_w, d_dw_b, d_ln_g, d_ln_b, d_pw = _conv_bwd(
        s['proj_a'], d_mix, p['dw_w'], p['dw_b'], p['ln_g'], p['ln_b'], p['pw_w'], name=n('conv_bwd'))
    g['conv_dw_w'], g['conv_dw_b'] = d_dw_w[:CONV_WIDTH], d_dw_b[0]
    g['conv_ln_g'], g['conv_ln_b'], g['conv_pw_w'] = d_ln_g[0], d_ln_b[0], d_pw
    dq, dk, dv, df3, dr = _attn_bwd(s['qkv'], s['f3'], s['att'], s['lse'], d_mix, name=n('attn_bwd'))
    df = _pad_rows(df3.reshape(N_HEADS, S) + dr[:, ::HEAD_DIM].T, FG_ROWS)
    d_z, d_b = _forget_bwd(s['z_raw'], p['b_col'], df, name=n('forget_bwd'))
    g['b_f'] = d_b[:N_HEADS, 0]
    d_qkv, d_qg, d_kg = _qk_prep_bwd(s['proj_a'], dq, dk, dv, p['qg'], p['kg'], name=n('qk_norm_bwd'))
    g['q_norm_g'] = d_qg.reshape(N_HEADS, HEAD_DIM).sum(axis=0)
    g['k_norm_g'] = d_kg.reshape(N_HEADS, HEAD_DIM).sum(axis=0)
    d_proj_a = jnp.concatenate([d_qkv, d_conv_in, d_pool_in], axis=1)
    d_h_fg = _mm(d_z, p['w_fg_t'], ta=True, name=n('d_h_fg'), tn=1024)
    d_h = _mm(d_proj_a, p['w_a'], tb=True, res=d_h_fg, name=n('d_h'), tn=1024, tk=768)
    d_w_a = _mm(s['h'], d_proj_a, ta=True, name=n('d_w_a'), tm=1024, tn=768)
    d_w_fg_t = _mm(d_z, s['h'], name=n('d_w_fg'), tn=1024)
    g['w_in'] = jnp.concatenate([d_w_a[:, :D_QKV], d_w_fg_t[:N_HEADS].T, d_w_a[:, D_QKV:]], axis=1)
    dx, dg1 = _rms_bwd(s['x'], p['norm1_g'], d_h, dx1, name=n('norm1_bwd'))
    g['norm1_g'] = dg1[0]
    return dx, g


def _local_step(x, target, w):
    depth = w['norm1_g'].shape[0]
    ps, saved = [], []
    for l in range(depth):
        p = _layer_weights(w, l)
        x, s = _layer_fwd(x, p, l)
        ps.append(p)
        saved.append(s)
    loss, dx = _loss_head(x, target, name='loss_head')
    grads = [None] * depth
    for l in reversed(range(depth)):
        dx, grads[l] = _layer_bwd(dx, ps[l], saved[l], l)
    return loss, dx, {k: jnp.stack([grads[l][k] for l in range(depth)]) for k in WEIGHTS}


MESH = pl.DeviceIdType.MESH
HBM_SPEC = pl.BlockSpec(memory_space=pltpu.HBM)
F32_WIRE = ('conv_dw_w', 'ffn_dw_w')
HALF_ROW_UNIT = 16


def _place():
    return lax.axis_index('x'), lax.axis_index('y'), lax.axis_index('c')


def _other_chips(x, y):
    return [(1 - x, y), (x, 1 - y), (1 - x, 1 - y)]


def _pack_rows(parts, row_unit):
    flat = jnp.concatenate(parts)
    unit = row_unit * LANES
    pad = (-flat.shape[0]) % unit
    if pad:
        flat = jnp.concatenate([flat, jnp.zeros((pad,), flat.dtype)])
    return flat.reshape(-1, LANES)


def _gather_weights(packed):
    m = packed.shape[0] // 2

    def body(x_ref, out_ref, send_sems, recv_sems, local_sem):
        x, y, c = _place()
        sibling = (x, y, 1 - c)
        chips = _other_chips(x, y)

        def rows(px, py, pc):
            return out_ref.at[2 * px + py, pl.ds(pc * m, m), :]

        def copy(k, block, to, src=None):
            return pltpu.make_async_remote_copy(
                src_ref=rows(*block) if src is None else src, dst_ref=rows(*block),
                send_sem=send_sems.at[k], recv_sem=recv_sems.at[k], device_id=to, device_id_type=MESH)

        mine = pltpu.make_async_copy(x_ref, out_ref.at[2 * x + y], local_sem)
        mine.start()
        my_half = x_ref.at[pl.ds(c * m, m), :]
        first = [copy(j, (x, y, c), (*chip, c), src=my_half) for j, chip in enumerate(chips)]
        for cp in first:
            cp.start()
        passed = [copy(3 + j, (*chip, c), sibling) for j, chip in enumerate(chips)]
        for j, chip in enumerate(chips):
            copy(j, (*chip, c), (x, y, c)).wait_recv()
            passed[j].start()
        for j, chip in enumerate(chips):
            copy(3 + j, (*chip, 1 - c), (x, y, c)).wait_recv()
        for cp in first + passed:
            cp.wait_send()
        mine.wait()

    return pl.pallas_call(
        body, name='gather_weights',
        out_shape=jax.ShapeDtypeStruct((N_CHIPS, 2 * m, LANES), packed.dtype),
        in_specs=[HBM_SPEC], out_specs=HBM_SPEC,
        scratch_shapes=[pltpu.SemaphoreType.DMA((6,)), pltpu.SemaphoreType.DMA((6,)), pltpu.SemaphoreType.DMA],
    )(packed)


def _cast_bf16(a, *, name, ts=2200):
    R = a.shape[0]
    ts = _tile(R, ts, HALF_ROW_UNIT)

    def body(a_ref, o_ref):
        o_ref[...] = a_ref[...].astype(BF16)

    row = pl.BlockSpec((ts, LANES), lambda i: (i, 0))
    return pl.pallas_call(body, name=name, out_shape=jax.ShapeDtypeStruct(a.shape, BF16), grid=(R // ts,),
                          in_specs=[row], out_specs=row, compiler_params=_params('parallel'))(a)


def _swap_sibling_halves(gb):
    hr = gb.shape[2]

    def body(g_ref, r_ref, send_sems, recv_sems):
        x, y, c = _place()
        copies = [pltpu.make_async_remote_copy(
            src_ref=g_ref.at[q, 1 - c], dst_ref=r_ref.at[q], send_sem=send_sems.at[q], recv_sem=recv_sems.at[q],
            device_id=(x, y, 1 - c), device_id_type=MESH) for q in range(N_CHIPS)]
        for cp in copies:
            cp.start()
        for cp in copies:
            cp.wait()

    return pl.pallas_call(
        body, name='rs_swap_halves', out_shape=jax.ShapeDtypeStruct((N_CHIPS, hr, LANES), BF16),
        in_specs=[HBM_SPEC], out_specs=HBM_SPEC,
        scratch_shapes=[pltpu.SemaphoreType.DMA((N_CHIPS,)), pltpu.SemaphoreType.DMA((N_CHIPS,))],
    )(gb)


def _add_sibling(g, ra, c_arr, *, ts=2200):
    hr = g.shape[2]
    ts = _tile(hr, ts, HALF_ROW_UNIT)

    def body(c_ref, g_ref, r_ref, p_ref, pb_ref):
        p = g_ref[...] + r_ref[...].astype(F32)
        p_ref[...] = p
        pb_ref[...] = p.astype(BF16)

    out = pl.BlockSpec((None, ts, LANES), lambda q, i, c_ref: (q, i, 0))
    return pl.pallas_call(
        body, name='rs_add_sibling',
        out_shape=(jax.ShapeDtypeStruct((N_CHIPS, hr, LANES), F32), jax.ShapeDtypeStruct((N_CHIPS, hr, LANES), BF16)),
        grid_spec=pltpu.PrefetchScalarGridSpec(
            num_scalar_prefetch=1, grid=(N_CHIPS, hr // ts),
            in_specs=[pl.BlockSpec((None, None, ts, LANES), lambda q, i, c_ref: (q, c_ref[0], i, 0)), out],
            out_specs=(out, out)),
        compiler_params=_params('parallel', 'parallel'),
    )(c_arr, g, ra)


def _send_chip_partials(pb):
    hr = pb.shape[1]

    def body(p_ref, r_ref, send_sems, recv_sems):
        x, y, c = _place()
        copies = [pltpu.make_async_remote_copy(
            src_ref=p_ref.at[2 * cx + cy], dst_ref=r_ref.at[j], send_sem=send_sems.at[j], recv_sem=recv_sems.at[j],
            device_id=(cx, cy, c), device_id_type=MESH) for j, (cx, cy) in enumerate(_other_chips(x, y))]
        for cp in copies:
            cp.start()
        for cp in copies:
            cp.wait()

    return pl.pallas_call(
        body, name='rs_send_partials', out_shape=jax.ShapeDtypeStruct((N_CHIPS - 1, hr, LANES), BF16),
        in_specs=[HBM_SPEC], out_specs=HBM_SPEC,
        scratch_shapes=[pltpu.SemaphoreType.DMA((N_CHIPS - 1,)), pltpu.SemaphoreType.DMA((N_CHIPS - 1,))],
    )(pb)


def _add_partials(p, rb, chip_arr, *, ts=2200):
    hr = p.shape[1]
    ts = _tile(hr, ts, HALF_ROW_UNIT)

    def body(chip_ref, p_ref, r_ref, o_ref):
        acc = p_ref[...]
        for j in range(N_CHIPS - 1):
            acc = acc + r_ref[j].astype(F32)
        o_ref[...] = acc

    return pl.pallas_call(
        body, name='rs_add_partials', out_shape=jax.ShapeDtypeStruct((hr, LANES), F32),
        grid_spec=pltpu.PrefetchScalarGridSpec(
            num_scalar_prefetch=1, grid=(hr // ts,),
            in_specs=[pl.BlockSpec((None, ts, LANES), lambda i, chip_ref: (chip_ref[0], i, 0)),
                      pl.BlockSpec((N_CHIPS - 1, ts, LANES), lambda i, chip_ref: (0, i, 0))],
            out_specs=pl.BlockSpec((ts, LANES), lambda i, chip_ref: (i, 0))),
        compiler_params=_params('parallel'),
    )(chip_arr, p, rb)


def _share_with_sibling(fin):
    hr = fin.shape[0]

    def body(f_ref, o_ref, send_sem, recv_sem, local_sem):
        x, y, c = _place()
        local = pltpu.make_async_copy(f_ref, o_ref.at[c], local_sem)
        local.start()
        cp = pltpu.make_async_remote_copy(src_ref=f_ref, dst_ref=o_ref.at[c], send_sem=send_sem, recv_sem=recv_sem,
                                          device_id=(x, y, 1 - c), device_id_type=MESH)
        cp.start()
        cp.wait()
        local.wait()

    return pl.pallas_call(
        body, name='rs_share_halves', out_shape=jax.ShapeDtypeStruct((2, hr, LANES), F32),
        in_specs=[HBM_SPEC], out_specs=HBM_SPEC,
        scratch_shapes=[pltpu.SemaphoreType.DMA, pltpu.SemaphoreType.DMA, pltpu.SemaphoreType.DMA],
    )(fin)


def _all_reduce_small(v):
    r = v.shape[0]

    def body(x_ref, tot_ref, all_ref, send_sems, recv_sems):
        x, y, c = _place()
        me, sibling = (x, y, c), (x, y, 1 - c)
        chips = _other_chips(x, y)

        def rows(px, py, pc):
            return all_ref.at[pl.ds((4 * px + 2 * py + pc) * r, r), :]

        def copy(k, block, to, src=None):
            return pltpu.make_async_remote_copy(
                src_ref=rows(*block) if src is None else src, dst_ref=rows(*block),
                send_sem=send_sems.at[k], recv_sem=recv_sems.at[k], device_id=to, device_id_type=MESH)

        rows(*me)[...] = x_ref[...]
        first = [copy(0, me, sibling, src=x_ref)]
        first += [copy(1 + j, me, (*chip, c), src=x_ref) for j, chip in enumerate(chips)]
        for cp in first:
            cp.start()
        passed = [copy(4 + j, (*chip, c), sibling) for j, chip in enumerate(chips)]
        for j, chip in enumerate(chips):
            copy(1 + j, (*chip, c), me).wait_recv()
            passed[j].start()
        copy(0, sibling, me).wait_recv()
        for j, chip in enumerate(chips):
            copy(4 + j, (*chip, 1 - c), me).wait_recv()
        for cp in first + passed:
            cp.wait_send()
        acc = all_ref[0:r, :]
        for d in range(1, N_DEV):
            acc = acc + all_ref[d * r:(d + 1) * r, :]
        tot_ref[...] = acc

    return pl.pallas_call(
        body, name='all_reduce_small', out_shape=jax.ShapeDtypeStruct((r, LANES), F32),
        in_specs=[pl.BlockSpec(memory_space=pltpu.VMEM)], out_specs=pl.BlockSpec(memory_space=pltpu.VMEM),
        scratch_shapes=[pltpu.VMEM((N_DEV * r, LANES), F32), pltpu.SemaphoreType.DMA((7,)), pltpu.SemaphoreType.DMA((7,))],
    )(v)


def _adamw(w, g, m, v, *, name, ts=256):
    R, C = w.shape
    ts = _tile(R, ts, 8)
    c1 = 1.0 - ADAM_B1 ** ADAM_STEP
    c2 = 1.0 - ADAM_B2 ** ADAM_STEP

    def body(w_ref, g_ref, m_ref, v_ref, d_ref, nm_ref, nv_ref):
        gv = g_ref[...]
        nm = ADAM_B1 * m_ref[...] + (1.0 - ADAM_B1) * gv
        nv = ADAM_B2 * v_ref[...] + (1.0 - ADAM_B2) * (gv * gv)
        d_ref[...] = -ADAM_LR * ((nm / c1) / (jnp.sqrt(nv / c2) + ADAM_EPS) + ADAM_WD * w_ref[...])
        nm_ref[...] = nm
        nv_ref[...] = nv

    blk = pl.BlockSpec((ts, C), lambda i: (i, 0))
    shape = jax.ShapeDtypeStruct((R, C), F32)
    return pl.pallas_call(body, name=name, out_shape=(shape, shape, shape), grid=(R // ts,),
                          in_specs=[blk, blk, blk, blk], out_specs=(blk, blk, blk),
                          compiler_params=_params('parallel'))(w, g, m, v)


def _wire_bf16(name, a):
    if name in F32_WIRE:
        return lax.bitcast_convert_type(a, BF16).reshape(-1)
    return a.astype(BF16).reshape(-1)


def _from_wire(name, flat, shape):
    if name in F32_WIRE:
        return lax.bitcast_convert_type(flat.reshape(shape + (2,)), F32)
    return flat.reshape(shape)


def _unshard(a, axis):
    a = jnp.moveaxis(a, 0, axis)
    return a.reshape(a.shape[:axis] + (a.shape[axis] * a.shape[axis + 1],) + a.shape[axis + 2:])


def _to_chunks(a, axis):
    n = a.shape[axis] // N_CHIPS
    return jnp.moveaxis(a.reshape(a.shape[:axis] + (N_CHIPS, n) + a.shape[axis + 1:]), axis, 0)


def _as_2d(a):
    return a.reshape(-1, a.shape[-1])


def kernel(x, norm1_g, w_in, b_f, q_norm_g, k_norm_g, conv_dw_w, conv_dw_b, conv_ln_g, conv_ln_b, conv_pw_w, pool_w, pool_scale, w_out, norm2_g, w_up, ffn_dw_w, w_down, loss_target, m_norm1_g, m_w_in, m_b_f, m_q_norm_g, m_k_norm_g, m_conv_dw_w, m_conv_dw_b, m_conv_ln_g, m_conv_ln_b, m_conv_pw_w, m_pool_w, m_pool_scale, m_w_out, m_norm2_g, m_w_up, m_ffn_dw_w, m_w_down, v_norm1_g, v_w_in, v_b_f, v_q_norm_g, v_k_norm_g, v_conv_dw_w, v_conv_dw_b, v_conv_ln_g, v_conv_ln_b, v_conv_pw_w, v_pool_w, v_pool_scale, v_w_out, v_norm2_g, v_w_up, v_ffn_dw_w, v_w_down):
    given = dict(locals())
    w = {k: given[k] for k in WEIGHTS}
    mom_m = {k: given['m_' + k] for k in WEIGHTS}
    mom_v = {k: given['v_' + k] for k in WEIGHTS}
    cx, cy, cc = _place()
    c_arr = jnp.reshape(cc, (1,)).astype(jnp.int32)
    chip_arr = jnp.reshape(2 * cx + cy, (1,)).astype(jnp.int32)

    packed = _pack_rows([_wire_bf16(k, w[k]) for k in SHARDED], 2 * HALF_ROW_UNIT)
    gathered = _gather_weights(packed).reshape(N_CHIPS, -1)
    full = {k: w[k] for k in REPLICATED}
    off = 0
    for k in SHARDED:
        size = w[k].size * (2 if k in F32_WIRE else 1)
        full[k] = _unshard(_from_wire(k, gathered[:, off:off + size], (N_CHIPS,) + w[k].shape), SHARD_AXIS[k])
        off += size

    loss_part, grad_x, g_full = _local_step(x[0], loss_target[0], full)
    loss = lax.psum(loss_part[0, 0], ('x', 'y', 'c'))

    chunks = [_to_chunks(g_full[k], SHARD_AXIS[k]).reshape(N_CHIPS, -1) for k in SHARDED]
    flat = jnp.concatenate(chunks, axis=1)
    pad = (-flat.shape[1]) % (2 * HALF_ROW_UNIT * LANES)
    if pad:
        flat = jnp.concatenate([flat, jnp.zeros((N_CHIPS, pad), F32)], axis=1)
    hr = flat.shape[1] // (2 * LANES)
    g4 = flat.reshape(N_CHIPS, 2, hr, LANES)
    gb = _cast_bf16(flat.reshape(-1, LANES), name='rs_cast').reshape(N_CHIPS, 2, hr, LANES)
    part, part_b = _add_sibling(g4, _swap_sibling_halves(gb), c_arr)
    fin = _add_partials(part, _send_chip_partials(part_b), chip_arr)
    summed = _share_with_sibling(fin).reshape(-1)
    g_sum = {}
    off = 0
    for k in SHARDED:
        g_sum[k] = summed[off:off + w[k].size].reshape(w[k].shape)
        off += w[k].size

    small = _pack_rows([g_full[k].reshape(-1) for k in REPLICATED], 8)
    small_sum = _all_reduce_small(small)

    delta, new_m, new_v = {}, {}, {}
    for k in SHARDED:
        d, nm, nv = _adamw(_as_2d(w[k]), _as_2d(g_sum[k]), _as_2d(mom_m[k]), _as_2d(mom_v[k]), name='adamw_' + k)
        delta[k], new_m[k], new_v[k] = d.reshape(w[k].shape), nm.reshape(w[k].shape), nv.reshape(w[k].shape)
    pack_small = lambda t: _pack_rows([t[k].reshape(-1) for k in REPLICATED], 8)
    outs = (small_sum,) + _adamw(pack_small(w), small_sum, pack_small(mom_m), pack_small(mom_v), name='adamw_small')
    off = 0
    for k in REPLICATED:
        pieces = [o.reshape(-1)[off:off + w[k].size].reshape(w[k].shape) for o in outs]
        g_sum[k], delta[k], new_m[k], new_v[k] = pieces
        off += w[k].size

    return (loss, grad_x[None], *[g_sum[k] for k in WEIGHTS], *[delta[k] for k in WEIGHTS],
            *[new_m[k] for k in WEIGHTS], *[new_v[k] for k in WEIGHTS])
```

```python
import functools

import jax
import jax.numpy as jnp
from jax import lax
from jax.experimental import pallas as pl
from jax.experimental.pallas import tpu as pltpu

F32 = jnp.float32
BF16 = jnp.bfloat16

N_HEADS = 8
HEAD_DIM = 64
D_ATT = N_HEADS * HEAD_DIM
D_CONV = 256
D_POOL = 256
D_MIX = D_ATT + D_CONV + D_POOL
D_QKV = 3 * D_ATT
D_PROJ_A = D_QKV + 2 * D_CONV + D_POOL
D_IN = D_PROJ_A + N_HEADS
FG_ROWS = 128
CONV_WIDTH = 31
CONV_HALO = 32
POOL_WINDOWS = (2, 4, 8, 16)
POOL_GROUP = 64
POOL_HALO = 16
FFN_CONV_WIDTH = 3
FFN_HALO = 8
ATT_SCALE = HEAD_DIM ** -0.5
EPS = 1e-6
NEG = -1e30
LANES = 128

ADAM_LR = 0.001
ADAM_B1 = 0.9
ADAM_B2 = 0.999
ADAM_EPS = 1e-08
ADAM_WD = 0.01
ADAM_STEP = 10

N_CHIPS = 4
N_DEV = 8
VMEM_LIMIT_BYTES = 56 * 1024 * 1024

REPLICATED = ('norm1_g', 'b_f', 'q_norm_g', 'k_norm_g', 'conv_dw_b', 'conv_ln_g', 'conv_ln_b',
              'pool_w', 'pool_scale', 'norm2_g')
WEIGHTS = ('norm1_g', 'w_in', 'b_f', 'q_norm_g', 'k_norm_g', 'conv_dw_w', 'conv_dw_b', 'conv_ln_g',
           'conv_ln_b', 'conv_pw_w', 'pool_w', 'pool_scale', 'w_out', 'norm2_g', 'w_up', 'ffn_dw_w', 'w_down')


def _tile(dim, pref, unit=LANES):
    if dim <= pref:
        return dim
    t = (pref // unit) * unit
    while t >= unit:
        if dim % t == 0:
            return t
        t -= unit
    raise ValueError(f'no tile for {dim} (preferred {pref})')


def _params(*sem):
    return pltpu.CompilerParams(dimension_semantics=sem, vmem_limit_bytes=VMEM_LIMIT_BYTES)


def _sigmoid(x):
    return 1.0 / (1.0 + jnp.exp(-x))


def _dot(a, b, ca, cb):
    return lax.dot_general(a, b, (((ca,), (cb,)), ((), ())), preferred_element_type=F32)


def _split3(y):
    y1 = y.astype(BF16)
    r1 = y - y1.astype(F32)
    y2 = r1.astype(BF16)
    y3 = (r1 - y2.astype(F32)).astype(BF16)
    return y1, y2, y3


def _dot3(y, e, ca=1, cb=0):
    y1, y2, y3 = _split3(y)
    return _dot(y1, e, ca, cb) + _dot(y2, e, ca, cb) + _dot(y3, e, ca, cb)


def _lead(spec_shape, imap, lead):
    if lead is None:
        return pl.BlockSpec(spec_shape, imap)
    return pl.BlockSpec((None,) + spec_shape, lambda *g: (lead,) + imap(*g))


ANY_SPEC = pl.BlockSpec(memory_space=pl.ANY)


def _mm(a, b, *, name, ta=False, tb=False, res=None, out_dtype=F32, tm=512, tn=512, tk=1024,
        a_lead=None, b_lead=None, out_lead=None, out_depth=None, out_buf=None):
    a2, b2 = a.shape[-2:], b.shape[-2:]
    K, M = a2 if ta else a2[::-1]
    N, Kb = b2 if tb else b2[::-1]
    assert K == Kb, (a.shape, b.shape)
    tm, tn, tk = _tile(M, tm), _tile(N, tn), _tile(K, tk)
    nk = K // tk
    ca = 0 if ta else 1
    cb = 1 if tb else 0
    has_res = res is not None
    has_buf = out_buf is not None
    n_in = 2 + has_res + has_buf

    def body(*refs):
        a_ref, b_ref = refs[:2]
        r_ref = refs[2] if has_res else None
        o_ref = refs[n_in]
        scratch = refs[n_in + 1:]
        p = _dot(a_ref[...].astype(BF16), b_ref[...].astype(BF16), ca, cb)
        if nk == 1:
            if has_res:
                p = p + r_ref[...]
            o_ref[...] = p.astype(out_dtype)
        else:
            acc = scratch[0]
            k = pl.program_id(2)

            @pl.when(k == 0)
            def _():
                acc[...] = p

            @pl.when(k > 0)
            def _():
                acc[...] += p

            @pl.when(k == nk - 1)
            def _():
                r = acc[...]
                if has_res:
                    r = r + r_ref[...]
                o_ref[...] = r.astype(out_dtype)

    a_spec = _lead((tk, tm), lambda i, j, k: (k, i), a_lead) if ta else _lead((tm, tk), lambda i, j, k: (i, k), a_lead)
    b_spec = _lead((tn, tk), lambda i, j, k: (j, k), b_lead) if tb else _lead((tk, tn), lambda i, j, k: (k, j), b_lead)
    o_map = lambda i, j, k: (i, j)
    in_specs = [a_spec, b_spec] + ([pl.BlockSpec((tm, tn), o_map)] if has_res else []) + ([ANY_SPEC] if has_buf else [])
    args = (a, b) + ((res,) if has_res else ()) + ((out_buf,) if has_buf else ())
    out_shape = (M, N) if out_depth is None else (out_depth, M, N)
    return pl.pallas_call(
        body, name=name,
        out_shape=jax.ShapeDtypeStruct(out_shape, out_dtype),
        grid=(M // tm, N // tn, nk),
        in_specs=in_specs, out_specs=_lead((tm, tn), o_map, out_lead),
        scratch_shapes=[pltpu.VMEM((tm, tn), F32)] if nk > 1 else [],
        input_output_aliases={n_in - 1: 0} if has_buf else {},
        compiler_params=_params('parallel', 'parallel', 'arbitrary'),
    )(*args)


def _rms_fwd(x, g, *, name, ts=512):
    S, D = x.shape
    ts = _tile(S, ts, 8)

    def body(x_ref, g_ref, o_ref):
        xv = x_ref[...]
        r = lax.rsqrt(jnp.mean(xv * xv, axis=-1, keepdims=True) + EPS)
        o_ref[...] = (xv * r * g_ref[...]).astype(BF16)

    return pl.pallas_call(
        body, name=name, out_shape=jax.ShapeDtypeStruct((S, D), BF16), grid=(S // ts,),
        in_specs=[pl.BlockSpec((ts, D), lambda i: (i, 0)), pl.BlockSpec((1, D), lambda i: (0, 0))],
        out_specs=pl.BlockSpec((ts, D), lambda i: (i, 0)),
        compiler_params=_params('parallel'),
    )(x, g)


def _rms_bwd(x, g, dh, dres, *, name, ts=512):
    S, D = x.shape
    ts = _tile(S, ts, 8)

    def body(x_ref, g_ref, dh_ref, dr_ref, dx_ref, dg_ref):
        i = pl.program_id(0)
        xv = x_ref[...]
        r = lax.rsqrt(jnp.mean(xv * xv, axis=-1, keepdims=True) + EPS)
        y = xv * r
        dh_v = dh_ref[...]
        dy = dh_v * g_ref[...]
        dx_ref[...] = dr_ref[...] + r * (dy - y * jnp.mean(dy * y, axis=-1, keepdims=True))
        part = jnp.sum(dh_v * y, axis=0, keepdims=True)

        @pl.when(i == 0)
        def _():
            dg_ref[...] = part

        @pl.when(i > 0)
        def _():
            dg_ref[...] += part

    row = pl.BlockSpec((ts, D), lambda i: (i, 0))
    vec = pl.BlockSpec((1, D), lambda i: (0, 0))
    return pl.pallas_call(
        body, name=name,
        out_shape=(jax.ShapeDtypeStruct((S, D), F32), jax.ShapeDtypeStruct((1, D), F32)),
        grid=(S // ts,), in_specs=[row, vec, row, row], out_specs=(row, vec),
        compiler_params=_params('arbitrary'),
    )(x, g, dh, dres)


def _group_ones():
    i = lax.broadcasted_iota(jnp.int32, (D_ATT, D_ATT), 0) // HEAD_DIM
    j = lax.broadcasted_iota(jnp.int32, (D_ATT, D_ATT), 1) // HEAD_DIM
    return (i == j).astype(BF16)


def _qk_prep_fwd(proj_a, qg, kg, *, name, ts=512):
    S = proj_a.shape[0]
    ts = _tile(S, ts, 16)

    def body(q_ref, k_ref, v_ref, qg_ref, kg_ref, e_ref, o_ref):
        e = e_ref[...]

        def norm(xv, gain):
            ms = _dot3(xv * xv, e) * (1.0 / HEAD_DIM)
            return xv * lax.rsqrt(ms + EPS) * gain

        o_ref[:, 0:D_ATT] = (norm(q_ref[...], qg_ref[...]) * ATT_SCALE).astype(BF16)
        o_ref[:, D_ATT:2 * D_ATT] = norm(k_ref[...], kg_ref[...]).astype(BF16)
        o_ref[:, 2 * D_ATT:3 * D_ATT] = v_ref[...].astype(BF16)

    col = lambda c: pl.BlockSpec((ts, D_ATT), lambda i: (i, c))
    vec = pl.BlockSpec((1, D_ATT), lambda i: (0, 0))
    return pl.pallas_call(
        body, name=name, out_shape=jax.ShapeDtypeStruct((S, D_QKV), BF16), grid=(S // ts,),
        in_specs=[col(0), col(1), col(2), vec, vec, pl.BlockSpec((D_ATT, D_ATT), lambda i: (0, 0))],
        out_specs=pl.BlockSpec((ts, D_QKV), lambda i: (i, 0)),
        compiler_params=_params('parallel'),
    )(proj_a, proj_a, proj_a, qg, kg, _group_ones())


def _qk_prep_bwd(proj_a, dq, dk, dv, qg, kg, *, name, ts=512):
    S = proj_a.shape[0]
    ts = _tile(S, ts, 16)

    def body(q_ref, k_ref, dq_ref, dk_ref, dv_ref, qg_ref, kg_ref, e_ref, o_ref, dqg_ref, dkg_ref):
        i = pl.program_id(0)
        e = e_ref[...]

        def norm_bwd(xv, dn, gain, scale):
            ms = _dot3(xv * xv, e) * (1.0 / HEAD_DIM)
            r = lax.rsqrt(ms + EPS)
            y = xv * r
            dy = dn * (gain * scale)
            mean = _dot3(dy * y, e) * (1.0 / HEAD_DIM)
            return r * (dy - y * mean), jnp.sum(dn * y, axis=0, keepdims=True) * scale

        dq_raw, dqg = norm_bwd(q_ref[...], dq_ref[...], qg_ref[...], ATT_SCALE)
        dk_raw, dkg = norm_bwd(k_ref[...], dk_ref[...], kg_ref[...], 1.0)
        o_ref[:, 0:D_ATT] = dq_raw.astype(BF16)
        o_ref[:, D_ATT:2 * D_ATT] = dk_raw.astype(BF16)
        o_ref[:, 2 * D_ATT:3 * D_ATT] = dv_ref[...].astype(BF16)

        @pl.when(i == 0)
        def _():
            dqg_ref[...] = dqg
            dkg_ref[...] = dkg

        @pl.when(i > 0)
        def _():
            dqg_ref[...] += dqg
            dkg_ref[...] += dkg

    col = lambda c: pl.BlockSpec((ts, D_ATT), lambda i: (i, c))
    vec = pl.BlockSpec((1, D_ATT), lambda i: (0, 0))
    return pl.pallas_call(
        body, name=name,
        out_shape=(jax.ShapeDtypeStruct((S, D_PROJ_A), BF16), jax.ShapeDtypeStruct((1, D_ATT), F32),
                   jax.ShapeDtypeStruct((1, D_ATT), F32)),
        grid=(S // ts,),
        in_specs=[col(0), col(1), col(0), col(0), col(0), vec, vec, pl.BlockSpec((D_ATT, D_ATT), lambda i: (0, 0))],
        out_specs=(pl.BlockSpec((ts, D_QKV), lambda i: (i, 0)), vec, vec),
        compiler_params=_params('arbitrary'),
    )(proj_a, proj_a, dq, dk, dv, qg, kg, _group_ones())


def _tri_ones(upper):
    i = lax.broadcasted_iota(jnp.int32, (LANES, LANES), 0)
    j = lax.broadcasted_iota(jnp.int32, (LANES, LANES), 1)
    return ((i <= j) if upper else (i >= j)).astype(BF16)


def _forget_fwd(z_raw, b_col, *, name):
    R, S = z_raw.shape
    nb = S // LANES

    def body(z_ref, b_ref, u_ref, f_ref):
        u = u_ref[...]
        carry = jnp.zeros((R, 1), F32)
        for j in range(nb):
            z = z_ref[:, j * LANES:(j + 1) * LANES] + b_ref[...]
            logf = jnp.minimum(z, 0.0) - jnp.log(1.0 + jnp.exp(-jnp.abs(z)))
            f_ref[:, j * LANES:(j + 1) * LANES] = _dot3(logf, u) + carry
            carry = carry + jnp.sum(logf, axis=1, keepdims=True)

    return pl.pallas_call(
        body, name=name, out_shape=jax.ShapeDtypeStruct((R, S), F32),
        compiler_params=pltpu.CompilerParams(vmem_limit_bytes=VMEM_LIMIT_BYTES),
    )(z_raw, b_col, _tri_ones(True))


def _forget_bwd(z_raw, b_col, df, *, name):
    R, S = z_raw.shape
    nb = S // LANES

    def body(z_ref, b_ref, df_ref, l_ref, dz_ref, db_ref):
        low = l_ref[...]
        carry = jnp.zeros((R, 1), F32)
        db = jnp.zeros((R, 1), F32)
        for j in reversed(range(nb)):
            d = df_ref[:, j * LANES:(j + 1) * LANES]
            dlogf = _dot3(d, low) + carry
            carry = carry + jnp.sum(d, axis=1, keepdims=True)
            z = z_ref[:, j * LANES:(j + 1) * LANES] + b_ref[...]
            dz = dlogf * _sigmoid(-z)
            dz_ref[:, j * LANES:(j + 1) * LANES] = dz
            db = db + jnp.sum(dz, axis=1, keepdims=True)
        db_ref[...] = db

    return pl.pallas_call(
        body, name=name,
        out_shape=(jax.ShapeDtypeStruct((R, S), F32), jax.ShapeDtypeStruct((R, 1), F32)),
        compiler_params=pltpu.CompilerParams(vmem_limit_bytes=VMEM_LIMIT_BYTES),
    )(z_raw, b_col, df, _tri_ones(False))


def _head_mask(hh):
    lane = lax.broadcasted_iota(jnp.int32, (1, LANES), 1)
    return (lane // HEAD_DIM) == hh


def _causal(s, qi, ki, t):
    rows = qi * t + lax.broadcasted_iota(jnp.int32, (t, t), 0)
    cols = ki * t + lax.broadcasted_iota(jnp.int32, (t, t), 1)
    return jnp.where(cols <= rows, s, NEG)


def _attn_fwd(qkv, f3, *, name, t=512):
    S = qkv.shape[0]
    t = _tile(S, t)
    n = S // t
    npair = N_HEADS // 2

    def body(q_ref, k_ref, v_ref, f_ref, mix_ref, o_ref, lse_ref, m_s, l_s, acc_s):
        qi, ki = pl.program_id(1), pl.program_id(2)

        @pl.when(ki == 0)
        def _():
            m_s[...] = jnp.full(m_s.shape, NEG, F32)
            l_s[...] = jnp.zeros(l_s.shape, F32)
            acc_s[...] = jnp.zeros(acc_s.shape, F32)

        @pl.when(ki <= qi)
        def _():
            q, k, v = q_ref[...], k_ref[...], v_ref[...]
            for hh in range(2):
                msk = _head_mask(hh)
                qm = jnp.where(msk, q, jnp.zeros_like(q))
                vm = jnp.where(msk, v, jnp.zeros_like(v))
                s = _causal(_dot(qm, k, 1, 1) - f_ref[0, hh:hh + 1, :], qi, ki, t)
                m_prev = m_s[hh]
                m_new = jnp.maximum(m_prev, jnp.max(s, axis=1, keepdims=True))
                alpha = jnp.exp(m_prev - m_new)
                p = jnp.exp(s - m_new)
                l_s[hh] = alpha * l_s[hh] + jnp.sum(p, axis=1, keepdims=True)
                acc_s[hh] = alpha * acc_s[hh] + _dot(p.astype(BF16), vm, 1, 0)
                m_s[hh] = m_new

        @pl.when(ki == qi)
        def _():
            o = acc_s[0] / l_s[0] + acc_s[1] / l_s[1]
            o_ref[...] = o
            mix_ref[...] = o.astype(BF16)
            lse0 = m_s[0] + jnp.log(l_s[0])
            lse1 = m_s[1] + jnp.log(l_s[1])
            lse_ref[...] = jnp.where(_head_mask(0), lse0, lse1)

    out = pl.BlockSpec((t, LANES), lambda h, i, j: (i, h))
    return pl.pallas_call(
        body, name=name,
        out_shape=(jax.ShapeDtypeStruct((S, D_MIX), BF16), jax.ShapeDtypeStruct((S, D_ATT), F32),
                   jax.ShapeDtypeStruct((S, D_ATT), F32)),
        grid=(npair, n, n),
        in_specs=[pl.BlockSpec((t, LANES), lambda h, i, j: (i, h)),
                  pl.BlockSpec((t, LANES), lambda h, i, j: (jnp.minimum(i, j), npair + h)),
                  pl.BlockSpec((t, LANES), lambda h, i, j: (jnp.minimum(i, j), 2 * npair + h)),
                  pl.BlockSpec((1, 2, t), lambda h, i, j: (h, 0, jnp.minimum(i, j)))],
        out_specs=(out, out, out),
        scratch_shapes=[pltpu.VMEM((2, t, 1), F32), pltpu.VMEM((2, t, 1), F32), pltpu.VMEM((2, t, LANES), F32)],
        compiler_params=_params('parallel', 'parallel', 'arbitrary'),
    )(qkv, qkv, qkv, f3)


def _attn_bwd(qkv, f3, att, lse, d_mix, *, name, t=512):
    S = qkv.shape[0]
    t = _tile(S, t)
    n = S // t
    npair = N_HEADS // 2

    def body(q_ref, k_ref, v_ref, f_ref, o_ref, lse_ref, do_ref, dq_ref, dk_ref, dv_ref, df_ref, dr_ref, dk_s, dv_s, df_s):
        ki, qi = pl.program_id(1), pl.program_id(2)

        @pl.when(qi == ki)
        def _():
            dk_s[...] = jnp.zeros(dk_s.shape, F32)
            dv_s[...] = jnp.zeros(dv_s.shape, F32)
            df_s[...] = jnp.zeros(df_s.shape, F32)

        @pl.when(qi >= ki)
        def _():
            q, k, v = q_ref[...], k_ref[...], v_ref[...]
            do, o, lse = do_ref[...], o_ref[...], lse_ref[...]
            dq_blk = jnp.zeros((t, LANES), F32)
            dr_blk = jnp.zeros((t, LANES), F32)
            for hh in range(2):
                msk = _head_mask(hh)
                qm = jnp.where(msk, q, jnp.zeros_like(q))
                km = jnp.where(msk, k, jnp.zeros_like(k))
                dom = jnp.where(msk, do, 0.0).astype(BF16)
                s = _causal(_dot(qm, k, 1, 1) - f_ref[0, hh:hh + 1, :], qi, ki, t)
                lse_h = jnp.max(jnp.where(msk, lse, NEG), axis=1, keepdims=True)
                p = jnp.exp(s - lse_h)
                dp = _dot(dom, v, 1, 1)
                delta = jnp.sum(dom.astype(F32) * o, axis=1, keepdims=True)
                ds = p * (dp - delta)
                dsb = ds.astype(BF16)
                dv_s[...] += _dot(p.astype(BF16), dom, 0, 0)
                dk_s[...] += _dot(dsb, qm, 0, 0)
                dq_blk = dq_blk + _dot(dsb, km, 1, 0)
                df_s[hh] -= jnp.sum(ds, axis=0, keepdims=True)
                dr_blk = dr_blk + jnp.where(msk, jnp.sum(ds, axis=1, keepdims=True), 0.0)
            rows = pl.ds(pl.multiple_of(qi * t, t), t)

            @pl.when(ki == 0)
            def _():
                dq_ref[rows, :] = dq_blk
                dr_ref[rows, :] = dr_blk

            @pl.when(ki > 0)
            def _():
                dq_ref[rows, :] += dq_blk
                dr_ref[rows, :] += dr_blk

        @pl.when(qi == n - 1)
        def _():
            dk_ref[...] = dk_s[...]
            dv_ref[...] = dv_s[...]
            df_ref[0, 0:1, :] = df_s[0]
            df_ref[0, 1:2, :] = df_s[1]

    qrow = lambda h, j, i: (jnp.maximum(i, j), h)
    return pl.pallas_call(
        body, name=name,
        out_shape=(jax.ShapeDtypeStruct((S, D_ATT), F32), jax.ShapeDtypeStruct((S, D_ATT), F32),
                   jax.ShapeDtypeStruct((S, D_ATT), F32), jax.ShapeDtypeStruct((npair, 2, S), F32),
                   jax.ShapeDtypeStruct((S, D_ATT), F32)),
        grid=(npair, n, n),
        in_specs=[pl.BlockSpec((t, LANES), qrow),
                  pl.BlockSpec((t, LANES), lambda h, j, i: (j, npair + h)),
                  pl.BlockSpec((t, LANES), lambda h, j, i: (j, 2 * npair + h)),
                  pl.BlockSpec((1, 2, t), lambda h, j, i: (h, 0, j)),
                  pl.BlockSpec((t, LANES), qrow),
                  pl.BlockSpec((t, LANES), qrow),
                  pl.BlockSpec((t, LANES), qrow)],
        out_specs=(pl.BlockSpec((S, LANES), lambda h, j, i: (0, h)),
                   pl.BlockSpec((t, LANES), lambda h, j, i: (j, h)),
                   pl.BlockSpec((t, LANES), lambda h, j, i: (j, h)),
                   pl.BlockSpec((1, 2, t), lambda h, j, i: (h, 0, j)),
                   pl.BlockSpec((S, LANES), lambda h, j, i: (0, h))),
        scratch_shapes=[pltpu.VMEM((t, LANES), F32), pltpu.VMEM((t, LANES), F32), pltpu.VMEM((2, 1, t), F32)],
        compiler_params=_params('parallel', 'arbitrary', 'arbitrary'),
    )(qkv, qkv, qkv, f3, att, lse, d_mix)


A_COL = D_QKV // D_CONV
B_COL = A_COL + 1
P_COL = B_COL + 1


def _layer_norm_stats(c):
    mu = jnp.mean(c, axis=-1, keepdims=True)
    xc = c - mu
    rstd = lax.rsqrt(jnp.mean(xc * xc, axis=-1, keepdims=True) + EPS)
    return xc * rstd, rstd


def _glu_into(buf, a_ref, b_ref, ah_ref, bh_ref, first, ts):
    halo = ah_ref[...] * _sigmoid(bh_ref[...])
    buf[0:CONV_HALO, :] = jnp.where(first, 0.0, halo)
    buf[CONV_HALO:CONV_HALO + ts, :] = a_ref[...] * _sigmoid(b_ref[...])


def _dwconv(buf, w_ref, ts):
    off = CONV_HALO - (CONV_WIDTH - 1)
    acc = w_ref[0:1, :] * buf[pl.ds(off, ts), :]
    for k in range(1, CONV_WIDTH):
        acc = acc + w_ref[k:k + 1, :] * buf[pl.ds(off + k, ts), :]
    return acc


def _conv_specs(ts, tmap):
    hb = ts // CONV_HALO
    cur = lambda c: pl.BlockSpec((ts, D_CONV), lambda i: (tmap(i), c))
    halo = lambda c: pl.BlockSpec((CONV_HALO, D_CONV), lambda i: (jnp.maximum(tmap(i) * hb - 1, 0), c))
    return cur, halo


def _conv_fwd(proj_a, mix, dw_w, dw_b, ln_g, ln_b, pw_w, l, *, name, ts=512):
    S = proj_a.shape[0]
    ts = _tile(S, ts, CONV_HALO)

    def body(a_ref, b_ref, ah_ref, bh_ref, w_ref, wb_ref, g_ref, bb_ref, pw_ref, mix_in, o_ref, buf):
        _glu_into(buf, a_ref, b_ref, ah_ref, bh_ref, pl.program_id(0) == 0, ts)
        c = _dwconv(buf, w_ref, ts) + wb_ref[...]
        yhat, _ = _layer_norm_stats(c)
        y = yhat * g_ref[...] + bb_ref[...]
        hs = y * _sigmoid(y)
        o_ref[...] = _dot(hs.astype(BF16), pw_ref[...], 1, 0).astype(BF16)

    cur, halo = _conv_specs(ts, lambda i: i)
    vec = pl.BlockSpec((1, D_CONV), lambda i: (0, 0))
    return pl.pallas_call(
        body, name=name, out_shape=jax.ShapeDtypeStruct(mix.shape, BF16), grid=(S // ts,),
        in_specs=[cur(A_COL), cur(B_COL), halo(A_COL), halo(B_COL),
                  pl.BlockSpec((None, CONV_HALO, D_CONV), lambda i: (l, 0, 0)), vec, vec, vec,
                  pl.BlockSpec((None, D_CONV, D_CONV), lambda i: (l, 0, 0)), ANY_SPEC],
        out_specs=pl.BlockSpec((ts, D_CONV), lambda i: (i, D_ATT // D_CONV)),
        scratch_shapes=[pltpu.VMEM((CONV_HALO + ts, D_CONV), F32)],
        input_output_aliases={9: 0},
        compiler_params=_params('parallel'),
    )(proj_a, proj_a, proj_a, proj_a, dw_w, dw_b, ln_g, ln_b, pw_w, mix)


def _conv_bwd(proj_a, d_mix, d_proj, dw_w, dw_b, ln_g, ln_b, pw_w, l, *, name, ts=512):
    S = proj_a.shape[0]
    ts = _tile(S, ts, CONV_HALO)
    n = S // ts
    d_col = D_ATT // D_CONV

    def body(a_ref, b_ref, ah_ref, bh_ref, dy_ref, w_ref, wb_ref, g_ref, bb_ref, pw_ref, dp_in,
             o_ref, dw_ref, dwb_ref, dg_ref, dbb_ref, dpw_ref, buf, dcbuf):
        i = pl.program_id(0)
        _glu_into(buf, a_ref, b_ref, ah_ref, bh_ref, i == n - 1, ts)
        c = _dwconv(buf, w_ref, ts) + wb_ref[...]
        yhat, rstd = _layer_norm_stats(c)
        y = yhat * g_ref[...] + bb_ref[...]
        sg = _sigmoid(y)
        hs = y * sg
        dout = dy_ref[...].astype(BF16)
        d_hs = _dot(dout, pw_ref[...], 1, 1)
        d_y = d_hs * (sg * (1.0 + y * (1.0 - sg)))
        d_yhat = d_y * g_ref[...]
        d_c = rstd * (d_yhat - jnp.mean(d_yhat, axis=-1, keepdims=True)
                      - yhat * jnp.mean(d_yhat * yhat, axis=-1, keepdims=True))

        @pl.when(i == 0)
        def _():
            dcbuf[ts:ts + CONV_HALO, :] = jnp.zeros((CONV_HALO, D_CONV), F32)
            dw_ref[...] = jnp.zeros(dw_ref.shape, F32)
            dwb_ref[...] = jnp.zeros(dwb_ref.shape, F32)
            dg_ref[...] = jnp.zeros(dg_ref.shape, F32)
            dbb_ref[...] = jnp.zeros(dbb_ref.shape, F32)
            dpw_ref[...] = jnp.zeros(dpw_ref.shape, F32)

        dcbuf[0:ts, :] = d_c
        dpw_ref[...] += _dot(hs.astype(BF16), dout, 0, 0)
        dg_ref[...] += jnp.sum(d_y * yhat, axis=0, keepdims=True)
        dbb_ref[...] += jnp.sum(d_y, axis=0, keepdims=True)
        dwb_ref[...] += jnp.sum(d_c, axis=0, keepdims=True)
        off = CONV_HALO - (CONV_WIDTH - 1)
        d_h = jnp.zeros((ts, D_CONV), F32)
        for k in range(CONV_WIDTH):
            d_h = d_h + w_ref[k:k + 1, :] * dcbuf[pl.ds(CONV_WIDTH - 1 - k, ts), :]
            dw_ref[k:k + 1, :] += jnp.sum(d_c * buf[pl.ds(off + k, ts), :], axis=0, keepdims=True)
        dcbuf[ts:ts + CONV_HALO, :] = d_c[0:CONV_HALO, :]
        a, sb = a_ref[...], _sigmoid(b_ref[...])
        o_ref[:, 0:D_CONV] = (d_h * sb).astype(BF16)
        o_ref[:, D_CONV:2 * D_CONV] = (d_h * a * sb * (1.0 - sb)).astype(BF16)

    rev = lambda i: n - 1 - i
    cur, halo = _conv_specs(ts, rev)
    vec = pl.BlockSpec((1, D_CONV), lambda i: (0, 0))
    wspec = pl.BlockSpec((CONV_HALO, D_CONV), lambda i: (0, 0))
    sq = pl.BlockSpec((D_CONV, D_CONV), lambda i: (0, 0))
    return pl.pallas_call(
        body, name=name,
        out_shape=(jax.ShapeDtypeStruct(d_proj.shape, BF16), jax.ShapeDtypeStruct((CONV_HALO, D_CONV), F32),
                   jax.ShapeDtypeStruct((1, D_CONV), F32), jax.ShapeDtypeStruct((1, D_CONV), F32),
                   jax.ShapeDtypeStruct((1, D_CONV), F32), jax.ShapeDtypeStruct((D_CONV, D_CONV), F32)),
        grid=(n,),
        in_specs=[cur(A_COL), cur(B_COL), halo(A_COL), halo(B_COL),
                  pl.BlockSpec((ts, D_CONV), lambda i: (rev(i), d_col)),
                  pl.BlockSpec((None, CONV_HALO, D_CONV), lambda i: (l, 0, 0)), vec, vec, vec,
                  pl.BlockSpec((None, D_CONV, D_CONV), lambda i: (l, 0, 0)), ANY_SPEC],
        out_specs=(pl.BlockSpec((ts, 2 * D_CONV), lambda i: (rev(i), D_QKV // (2 * D_CONV))), wspec, vec, vec, vec, sq),
        scratch_shapes=[pltpu.VMEM((CONV_HALO + ts, D_CONV), F32), pltpu.VMEM((ts + CONV_HALO, D_CONV), F32)],
        input_output_aliases={10: 0},
        compiler_params=_params('arbitrary'),
    )(proj_a, proj_a, proj_a, proj_a, d_mix, dw_w, dw_b, ln_g, ln_b, pw_w, d_proj)


def _pool_window():
    lane = lax.broadcasted_iota(jnp.int32, (1, D_POOL), 1)
    w = jnp.full((1, D_POOL), POOL_WINDOWS[0], jnp.int32)
    for g in range(1, len(POOL_WINDOWS)):
        w = jnp.where(lane // POOL_GROUP == g, POOL_WINDOWS[g], w)
    return w


def _pool_diff(buf, u_ref, uh_ref, first, tile, ts):
    buf[0:POOL_HALO, :] = jnp.where(first, 0.0, uh_ref[...])
    u = u_ref[...]
    buf[POOL_HALO:POOL_HALO + ts, :] = u
    wl = _pool_window()
    acc = u
    for j in range(1, max(POOL_WINDOWS)):
        acc = acc + jnp.where(j < wl, buf[pl.ds(POOL_HALO - j, ts), :], 0.0)
    pos = tile * ts + lax.broadcasted_iota(jnp.int32, (ts, 1), 0)
    cnt = jnp.minimum(pos + 1, wl).astype(F32)
    return acc / cnt - u, cnt


def _pool_specs(ts, tmap):
    hb = ts // POOL_HALO
    cur = pl.BlockSpec((ts, D_POOL), lambda i: (tmap(i), P_COL))
    halo = pl.BlockSpec((POOL_HALO, D_POOL), lambda i: (jnp.maximum(tmap(i) * hb - 1, 0), P_COL))
    return cur, halo


def _pool_fwd(proj_a, mix, wbd, scale, *, name, ts=512):
    S = proj_a.shape[0]
    ts = _tile(S, ts, POOL_HALO)

    def body(u_ref, uh_ref, w_ref, s_ref, mix_in, o_ref, buf):
        i = pl.program_id(0)
        d, _ = _pool_diff(buf, u_ref, uh_ref, i == 0, i, ts)
        o_ref[...] = (_dot(d.astype(BF16), w_ref[...], 1, 0) * s_ref[...]).astype(BF16)

    cur, halo = _pool_specs(ts, lambda i: i)
    return pl.pallas_call(
        body, name=name, out_shape=jax.ShapeDtypeStruct(mix.shape, BF16), grid=(S // ts,),
        in_specs=[cur, halo, pl.BlockSpec((D_POOL, D_POOL), lambda i: (0, 0)), pl.BlockSpec((1, D_POOL), lambda i: (0, 0)),
                  ANY_SPEC],
        out_specs=pl.BlockSpec((ts, D_POOL), lambda i: (i, (D_ATT + D_CONV) // D_POOL)),
        scratch_shapes=[pltpu.VMEM((POOL_HALO + ts, D_POOL), F32)],
        input_output_aliases={4: 0},
        compiler_params=_params('parallel'),
    )(proj_a, proj_a, wbd, scale, mix)


def _pool_bwd(proj_a, d_mix, d_proj, wbd, scale, *, name, ts=512):
    S = proj_a.shape[0]
    ts = _tile(S, ts, POOL_HALO)
    n = S // ts
    d_col = (D_ATT + D_CONV) // D_POOL

    def body(u_ref, uh_ref, dy_ref, w_ref, s_ref, dp_in, o_ref, dw_ref, ds_ref, buf, ebuf):
        i = pl.program_id(0)
        tile = n - 1 - i
        d, cnt = _pool_diff(buf, u_ref, uh_ref, tile == 0, tile, ts)
        db = d.astype(BF16)
        ypre = _dot(db, w_ref[...], 1, 0)
        dout = dy_ref[...]
        d_y = (dout * s_ref[...]).astype(BF16)
        d_d = _dot(d_y, w_ref[...], 1, 1)

        @pl.when(i == 0)
        def _():
            ebuf[ts:ts + POOL_HALO, :] = jnp.zeros((POOL_HALO, D_POOL), F32)
            dw_ref[...] = jnp.zeros(dw_ref.shape, F32)
            ds_ref[...] = jnp.zeros(ds_ref.shape, F32)

        dw_ref[...] += _dot(db, d_y, 0, 0)
        ds_ref[...] += jnp.sum(dout * ypre, axis=0, keepdims=True)
        e = d_d / cnt
        ebuf[0:ts, :] = e
        wl = _pool_window()
        acc = e
        for j in range(1, max(POOL_WINDOWS)):
            acc = acc + jnp.where(j < wl, ebuf[pl.ds(j, ts), :], 0.0)
        ebuf[ts:ts + POOL_HALO, :] = e[0:POOL_HALO, :]
        o_ref[...] = (acc - d_d).astype(BF16)

    rev = lambda i: n - 1 - i
    cur, halo = _pool_specs(ts, rev)
    sq = pl.BlockSpec((D_POOL, D_POOL), lambda i: (0, 0))
    vec = pl.BlockSpec((1, D_POOL), lambda i: (0, 0))
    return pl.pallas_call(
        body, name=name,
        out_shape=(jax.ShapeDtypeStruct(d_proj.shape, BF16), jax.ShapeDtypeStruct((D_POOL, D_POOL), F32),
                   jax.ShapeDtypeStruct((1, D_POOL), F32)),
        grid=(n,),
        in_specs=[cur, halo, pl.BlockSpec((ts, D_POOL), lambda i: (rev(i), d_col)), sq, vec, ANY_SPEC],
        out_specs=(pl.BlockSpec((ts, D_POOL), lambda i: (rev(i), P_COL)), sq, vec),
        scratch_shapes=[pltpu.VMEM((POOL_HALO + ts, D_POOL), F32), pltpu.VMEM((ts + POOL_HALO, D_POOL), F32)],
        input_output_aliases={5: 0},
        compiler_params=_params('arbitrary'),
    )(proj_a, proj_a, d_mix, wbd, scale, d_proj)


def _ffn_conv(buf, w_ref, ts):
    off = FFN_HALO - (FFN_CONV_WIDTH - 1)
    acc = w_ref[0:1, :] * buf[pl.ds(off, ts), :]
    for k in range(1, FFN_CONV_WIDTH):
        acc = acc + w_ref[k:k + 1, :] * buf[pl.ds(off + k, ts), :]
    return acc


def _ffn_specs(ts, tc2, tmap, l):
    hb = ts // FFN_HALO
    cur = pl.BlockSpec((ts, tc2), lambda c, i: (tmap(i), c))
    halo = pl.BlockSpec((FFN_HALO, tc2), lambda c, i: (jnp.maximum(tmap(i) * hb - 1, 0), c))
    wspec = pl.BlockSpec((None, FFN_HALO, tc2), lambda c, i: (l, 0, c))
    return cur, halo, wspec


def _ffn_act_fwd(up, w, l, *, name, ts=256):
    S, F2 = up.shape
    tc = F2 // 4
    ts = _tile(S, ts, FFN_HALO)

    def body(x_ref, xh_ref, w_ref, o_ref, buf):
        buf[0:FFN_HALO, :] = jnp.where(pl.program_id(1) == 0, 0.0, xh_ref[...])
        buf[FFN_HALO:FFN_HALO + ts, :] = x_ref[...]
        cv = _ffn_conv(buf, w_ref, ts)
        gate, val = cv[:, 0:tc], cv[:, tc:2 * tc]
        o_ref[...] = (gate * _sigmoid(gate) * val).astype(BF16)

    cur, halo, wspec = _ffn_specs(ts, 2 * tc, lambda i: i, l)
    return pl.pallas_call(
        body, name=name, out_shape=jax.ShapeDtypeStruct((S, F2 // 2), BF16), grid=(2, S // ts),
        in_specs=[cur, halo, wspec],
        out_specs=pl.BlockSpec((ts, tc), lambda c, i: (i, c)),
        scratch_shapes=[pltpu.VMEM((FFN_HALO + ts, 2 * tc), F32)],
        compiler_params=_params('parallel', 'parallel'),
    )(up, up, w)


def _ffn_act_bwd(up, d_act, w, l, *, name, ts=256):
    S, F2 = up.shape
    tc = F2 // 4
    ts = _tile(S, ts, FFN_HALO)
    n = S // ts

    def body(x_ref, xh_ref, da_ref, w_ref, o_ref, dw_ref, buf, dcbuf):
        i = pl.program_id(1)
        buf[0:FFN_HALO, :] = jnp.where(i == n - 1, 0.0, xh_ref[...])
        buf[FFN_HALO:FFN_HALO + ts, :] = x_ref[...]
        cv = _ffn_conv(buf, w_ref, ts)
        gate, val = cv[:, 0:tc], cv[:, tc:2 * tc]
        sg = _sigmoid(gate)
        da = da_ref[...]
        d_gate = da * val * (sg * (1.0 + gate * (1.0 - sg)))
        d_val = da * (gate * sg)

        @pl.when(i == 0)
        def _():
            dcbuf[ts:ts + FFN_HALO, :] = jnp.zeros((FFN_HALO, 2 * tc), F32)
            dw_ref[...] = jnp.zeros(dw_ref.shape, F32)

        dcbuf[0:ts, 0:tc] = d_gate
        dcbuf[0:ts, tc:2 * tc] = d_val
        d_c = dcbuf[0:ts, :]
        off = FFN_HALO - (FFN_CONV_WIDTH - 1)
        d_x = jnp.zeros((ts, 2 * tc), F32)
        for k in range(FFN_CONV_WIDTH):
            d_x = d_x + w_ref[k:k + 1, :] * dcbuf[pl.ds(FFN_CONV_WIDTH - 1 - k, ts), :]
            dw_ref[k:k + 1, :] += jnp.sum(d_c * buf[pl.ds(off + k, ts), :], axis=0, keepdims=True)
        dcbuf[ts:ts + FFN_HALO, :] = d_c[0:FFN_HALO, :]
        o_ref[...] = d_x.astype(BF16)

    rev = lambda i: n - 1 - i
    cur, halo, wspec = _ffn_specs(ts, 2 * tc, rev, l)
    return pl.pallas_call(
        body, name=name,
        out_shape=(jax.ShapeDtypeStruct((S, F2), BF16), jax.ShapeDtypeStruct((FFN_HALO, F2), F32)),
        grid=(2, n),
        in_specs=[cur, halo, pl.BlockSpec((ts, tc), lambda c, i: (rev(i), c)), wspec],
        out_specs=(cur, pl.BlockSpec((FFN_HALO, 2 * tc), lambda c, i: (0, c))),
        scratch_shapes=[pltpu.VMEM((FFN_HALO + ts, 2 * tc), F32), pltpu.VMEM((ts + FFN_HALO, 2 * tc), F32)],
        compiler_params=_params('parallel', 'arbitrary'),
    )(up, up, d_act, w)


def _loss_head(y, target, *, name, ts=512):
    S, D = y.shape
    ts = _tile(S, ts, 8)

    def body(y_ref, t_ref, l_ref, dy_ref):
        i = pl.program_id(0)
        err = y_ref[...] - t_ref[...]
        dy_ref[...] = err * (1.0 / D)
        part = jnp.sum(jnp.sum(err * err, axis=1, keepdims=True), axis=0, keepdims=True) * (0.5 / D)

        @pl.when(i == 0)
        def _():
            l_ref[...] = part

        @pl.when(i > 0)
        def _():
            l_ref[...] += part

    row = pl.BlockSpec((ts, D), lambda i: (i, 0))
    return pl.pallas_call(
        body, name=name,
        out_shape=(jax.ShapeDtypeStruct((1, 1), F32), jax.ShapeDtypeStruct((S, D), F32)),
        grid=(S // ts,), in_specs=[row, row], out_specs=(pl.BlockSpec((1, 1), lambda i: (0, 0)), row),
        compiler_params=_params('arbitrary'),
    )(y, target)


def _pair_cols(w):
    lead, f2 = w.shape[:-1], w.shape[-1]
    return w.reshape(lead + (2, 2, f2 // 4)).swapaxes(-3, -2).reshape(lead + (f2,))


def _pad_axis(w, size, axis):
    pad = [(0, 0)] * w.ndim
    pad[axis] = (0, size - w.shape[axis])
    return jnp.pad(w, pad)


def _block_diag(pool_w):
    g = pool_w.shape[0]
    rows = [jnp.concatenate([pool_w[i] if i == j else jnp.zeros_like(pool_w[i]) for j in range(g)], axis=1) for i in range(g)]
    return jnp.concatenate(rows, axis=0)


def _small_weights(w, l):
    return dict(
        norm1_g=w['norm1_g'][l][None, :],
        b_col=_pad_axis(w['b_f'][l][:, None], FG_ROWS, 0),
        qg=jnp.tile(w['q_norm_g'][l], N_HEADS)[None, :],
        kg=jnp.tile(w['k_norm_g'][l], N_HEADS)[None, :],
        dw_b=w['conv_dw_b'][l][None, :], ln_g=w['conv_ln_g'][l][None, :], ln_b=w['conv_ln_b'][l][None, :],
        wbd=_block_diag(w['pool_w'][l]).astype(BF16),
        pool_scale=w['pool_scale'][l][None, :],
        norm2_g=w['norm2_g'][l][None, :],
    )


def _layer_fwd(x, W, p, l):
    n = lambda s: f'l{l}_{s}'
    S = x.shape[0]
    h = _rms_fwd(x, p['norm1_g'], name=n('norm1'))
    proj_a = _mm(h, W['w_a'], b_lead=l, name=n('proj_a'), tn=768)
    z_raw = _mm(W['w_fg_t'], h, a_lead=l, tb=True, name=n('proj_fg'))
    qkv = _qk_prep_fwd(proj_a, p['qg'], p['kg'], name=n('qk_norm'))
    f_cum = _forget_fwd(z_raw, p['b_col'], name=n('forget'))
    f3 = f_cum[:N_HEADS].reshape(N_HEADS // 2, 2, S)
    mix, att, lse = _attn_fwd(qkv, f3, name=n('attn'))
    mix = _conv_fwd(proj_a, mix, W['dw_w'], p['dw_b'], p['ln_g'], p['ln_b'], W['pw_w'], l, name=n('conv'))
    mix = _pool_fwd(proj_a, mix, p['wbd'], p['pool_scale'], name=n('pool'))
    x1 = _mm(mix, W['w_out'], b_lead=l, res=x, name=n('out_proj'), tn=1024)
    h2 = _rms_fwd(x1, p['norm2_g'], name=n('norm2'))
    up = _mm(h2, W['w_up'], b_lead=l, name=n('up_proj'), tn=1408)
    act = _ffn_act_fwd(up, W['ffn_w'], l, name=n('ffn_act'))
    x2 = _mm(act, W['w_down'], b_lead=l, res=x1, name=n('down_proj'), tn=1024, tk=1408)
    saved = dict(x=x, h=h, proj_a=proj_a, z_raw=z_raw, qkv=qkv, f3=f3, att=att, lse=lse, mix=mix, x1=x1, h2=h2, up=up, act=act)
    return x2, saved


def _layer_bwd(dx2, W, p, s, l, depth, G):
    n = lambda t: f'l{l}_{t}'
    S = dx2.shape[0]
    g = {}

    def stacked(key, a, b, **kw):
        G[key] = _mm(a, b, ta=True, out_lead=l, out_depth=depth, out_buf=G.get(key), name=n('d_' + key), **kw)

    d_act = _mm(dx2, W['w_down'], b_lead=l, tb=True, name=n('d_act'), tn=1408)
    stacked('w_down', s['act'], dx2, tm=1408, tn=1024)
    d_up, d_ffn_w = _ffn_act_bwd(s['up'], d_act, W['ffn_w'], l, name=n('ffn_act_bwd'))
    g['ffn_dw_w'] = _pair_cols(d_ffn_w[:FFN_CONV_WIDTH])
    d_h2 = _mm(d_up, W['w_up'], b_lead=l, tb=True, name=n('d_h2'), tn=1024, tk=1408)
    stacked('w_up', s['h2'], d_up, tm=1024, tn=1408)
    dx1, dg2 = _rms_bwd(s['x1'], p['norm2_g'], d_h2, dx2, name=n('norm2_bwd'))
    g['norm2_g'] = dg2[0]
    d_mix = _mm(dx1, W['w_out'], b_lead=l, tb=True, name=n('d_mix'), tn=1024)
    stacked('w_out', s['mix'], dx1, tm=1024, tn=1024)
    dq, dk, dv, df3, dr = _attn_bwd(s['qkv'], s['f3'], s['att'], s['lse'], d_mix, name=n('attn_bwd'))
    df = _pad_axis(df3.reshape(N_HEADS, S) + dr[:, ::HEAD_DIM].T, FG_ROWS, 0)
    d_z, d_b = _forget_bwd(s['z_raw'], p['b_col'], df, name=n('forget_bwd'))
    g['b_f'] = d_b[:N_HEADS, 0]
    d_proj, d_qg, d_kg = _qk_prep_bwd(s['proj_a'], dq, dk, dv, p['qg'], p['kg'], name=n('qk_norm_bwd'))
    g['q_norm_g'] = d_qg.reshape(N_HEADS, HEAD_DIM).sum(axis=0)
    g['k_norm_g'] = d_kg.reshape(N_HEADS, HEAD_DIM).sum(axis=0)
    d_proj, d_dw_w, d_dw_b, d_ln_g, d_ln_b, d_pw = _conv_bwd(
        s['proj_a'], d_mix, d_proj, W['dw_w'], p['dw_b'], p['ln_g'], p['ln_b'], W['pw_w'], l, name=n('conv_bwd'))
    g['conv_dw_w'], g['conv_dw_b'] = d_dw_w[:CONV_WIDTH], d_dw_b[0]
    g['conv_ln_g'], g['conv_ln_b'], g['conv_pw_w'] = d_ln_g[0], d_ln_b[0], d_pw
    d_proj, d_wbd, d_scale = _pool_bwd(s['proj_a'], d_mix, d_proj, p['wbd'], p['pool_scale'], name=n('pool_bwd'))
    g['pool_w'] = jnp.stack([d_wbd[i * POOL_GROUP:(i + 1) * POOL_GROUP, i * POOL_GROUP:(i + 1) * POOL_GROUP]
                             for i in range(len(POOL_WINDOWS))])
    g['pool_scale'] = d_scale[0]
    d_h_fg = _mm(d_z, W['w_fg_t'], b_lead=l, ta=True, name=n('d_h_fg'), tn=1024)
    d_h = _mm(d_proj, W['w_a'], b_lead=l, tb=True, res=d_h_fg, name=n('d_h'), tn=1024, tk=768)
    stacked('w_a', s['h'], d_proj, tm=1024, tn=768)
    g['w_fg'] = _mm(d_z, s['h'], name=n('d_w_fg'), tn=1024).T
    dx, dg1 = _rms_bwd(s['x'], p['norm1_g'], d_h, dx1, name=n('norm1_bwd'))
    g['norm1_g'] = dg1[0]
    return dx, g


SMALL_GRADS = REPLICATED + ('conv_dw_w', 'conv_pw_w', 'ffn_dw_w')


def _local_step(x, target, W, w_small):
    depth = w_small['norm1_g'].shape[0]
    ps, saved = [], []
    for l in range(depth):
        p = _small_weights(w_small, l)
        x, s = _layer_fwd(x, W, p, l)
        ps.append(p)
        saved.append(s)
    loss, dx = _loss_head(x, target, name='loss_head')
    small = [None] * depth
    G = {}
    for l in reversed(range(depth)):
        dx, small[l] = _layer_bwd(dx, W, ps[l], saved[l], l, depth, G)
    G['w_fg'] = jnp.stack([small[l]['w_fg'] for l in range(depth)])
    return loss, dx, G, {k: jnp.stack([small[l][k] for l in range(depth)]) for k in SMALL_GRADS}


W_IN_SHARD = D_IN // N_CHIPS
W_IN_PAD = 640
N_A_TILES = D_PROJ_A // LANES
FG_COL0 = D_QKV


def _a_tile_base(j):
    if j == N_A_TILES:
        return FG_COL0, N_HEADS
    return (j * LANES if j * LANES < FG_COL0 else j * LANES + N_HEADS), LANES


def _shift_select(rows, cols, shift, row_max, col_max):
    r = lax.broadcasted_iota(jnp.int32, (rows, cols), 0)
    c = lax.broadcasted_iota(jnp.int32, (rows, cols), 1)
    return ((r + shift == c) & (r < row_max) & (c < col_max)).astype(BF16)


def _select_w_in(raw, *, tm=256):
    L, _, D, _ = raw.shape
    tm = _tile(D, tm, 16)
    plan = []
    for j in range(N_A_TILES + 1):
        base, cmax = _a_tile_base(j)
        parts = []
        for p in range(N_CHIPS):
            delta = base - W_IN_SHARD * p
            lo, hi = max(0, delta), min(W_IN_SHARD - 1, delta + cmax - 1)
            if lo > hi:
                continue
            a0 = (lo // LANES) * LANES
            kw = min(-(-(hi + 1 - a0) // LANES) * LANES, W_IN_PAD - a0)
            parts.append((p, a0, kw, delta))
        plan.append((cmax, parts))

    def body(raw_ref, wa_ref, fg_ref):
        for j, (cmax, parts) in enumerate(plan):
            acc = None
            for p, a0, kw, delta in parts:
                sel = _shift_select(kw, LANES, a0 - delta, W_IN_SHARD - a0, cmax)
                t = _dot(raw_ref[p, :, a0:a0 + kw], sel, 1, 0)
                acc = t if acc is None else acc + t
            if j == N_A_TILES:
                fg_ref[...] = acc.astype(BF16)
            else:
                wa_ref[:, j * LANES:(j + 1) * LANES] = acc.astype(BF16)

    return pl.pallas_call(
        body, name='select_w_in',
        out_shape=(jax.ShapeDtypeStruct((L, D, D_PROJ_A), BF16), jax.ShapeDtypeStruct((L, D, LANES), BF16)),
        grid=(L, D // tm),
        in_specs=[pl.BlockSpec((None, N_CHIPS, tm, W_IN_PAD), lambda l, i: (l, 0, i, 0))],
        out_specs=(pl.BlockSpec((None, tm, D_PROJ_A), lambda l, i: (l, i, 0)),
                   pl.BlockSpec((None, tm, LANES), lambda l, i: (l, i, 0))),
        compiler_params=_params('parallel', 'parallel'),
    )(raw)


def _select_w_in_grads(p_a, p_fg, *, tm=256):
    D = p_a.shape[0]
    tm = _tile(D, tm, 16)
    n_local = W_IN_PAD // LANES
    plan = []
    for p in range(N_CHIPS):
        for i in range(n_local):
            cmax = max(0, min(LANES, W_IN_SHARD - i * LANES))
            parts = []
            for j in range(N_A_TILES + 1):
                base, rmax = _a_tile_base(j)
                e = base - W_IN_SHARD * p - i * LANES
                if e + rmax - 1 < 0 or e > cmax - 1:
                    continue
                parts.append((j, e, rmax))
            plan.append((p, i, cmax, parts))

    def body(a_ref, fg_ref, o32_ref, o16_ref):
        terms = {}

        def src(j):
            if j not in terms:
                v = fg_ref[...] if j == N_A_TILES else a_ref[:, j * LANES:(j + 1) * LANES]
                terms[j] = _split3(v)
            return terms[j]

        for p, i, cmax, parts in plan:
            acc = jnp.zeros((tm, LANES), F32)
            for j, e, rmax in parts:
                sel = _shift_select(LANES, LANES, e, rmax, cmax)
                for term in src(j):
                    acc = acc + _dot(term, sel, 1, 0)
            o32_ref[p, :, i * LANES:(i + 1) * LANES] = acc
            o16_ref[p, :, i * LANES:(i + 1) * LANES] = acc.astype(BF16)

    out = pl.BlockSpec((N_CHIPS, tm, W_IN_PAD), lambda i: (0, i, 0))
    return pl.pallas_call(
        body, name='select_w_in_grads',
        out_shape=(jax.ShapeDtypeStruct((N_CHIPS, D, W_IN_PAD), F32), jax.ShapeDtypeStruct((N_CHIPS, D, W_IN_PAD), BF16)),
        grid=(D // tm,),
        in_specs=[pl.BlockSpec((tm, D_PROJ_A), lambda i: (i, 0)), pl.BlockSpec((tm, LANES), lambda i: (i, 0))],
        out_specs=(out, out),
        compiler_params=_params('parallel'),
    )(p_a, p_fg)


MESH = pl.DeviceIdType.MESH
HBM_SPEC = pl.BlockSpec(memory_space=pltpu.HBM)


def _place():
    return lax.axis_index('x'), lax.axis_index('y'), lax.axis_index('c')


def _other_chips(x, y):
    return [(1 - x, y), (x, 1 - y), (1 - x, 1 - y)]


def _up_pos(q):
    return (q % 2) * 2 + q // 2


CHUNKS = {
    'w_in': ('lead', None),
    'w_up': ('cols', None),
    'w_down': ('rows', None),
    'w_out': ('rows', None),
    'conv_pw_w': ('rows', None),
    'conv_dw_w': ('lead', None),
    'ffn_dw_w': ('lead', None),
}


def _window(ref, kind, l, q):
    at = (lambda *idx: ref.at[idx]) if l is None else (lambda *idx: ref.at[(l,) + idx])
    shape = ref.shape if l is None else ref.shape[1:]
    if kind == 'lead':
        return at(q)
    if kind == 'rows':
        cs = shape[0] // N_CHIPS
        return at(pl.ds(pl.multiple_of(q * cs, 16), cs), slice(None))
    cs = shape[1] // N_CHIPS
    return at(slice(None), pl.ds(pl.multiple_of(_up_pos(q) * cs, LANES), cs))


def _place_shard(src, pos_arr, full_shape, kind, *, name, tm=256):
    L, m, n = src.shape
    bm = _tile(m, tm, 16) if kind != 'rows' else m

    def body(pos_ref, s_ref, o_ref):
        o_ref[...] = s_ref[...].astype(BF16)

    if kind == 'lead':
        out = pl.BlockSpec((None, None, bm, n), lambda l, i, pos: (l, pos[0], i, 0))
    elif kind == 'rows':
        out = pl.BlockSpec((None, bm, n), lambda l, i, pos: (l, pos[0], 0))
    else:
        out = pl.BlockSpec((None, bm, n), lambda l, i, pos: (l, i, pos[0]))
    return pl.pallas_call(
        body, name=name, out_shape=jax.ShapeDtypeStruct(full_shape, BF16),
        grid_spec=pltpu.PrefetchScalarGridSpec(
            num_scalar_prefetch=1, grid=(L, m // bm),
            in_specs=[pl.BlockSpec((None, bm, n), lambda l, i, pos: (l, i, 0))], out_specs=out),
        compiler_params=_params('parallel', 'parallel'),
    )(pos_arr, src)


GATHERED = ('w_in', 'w_up', 'w_down', 'w_out', 'conv_pw_w', 'conv_dw_w', 'ffn_dw_w')


def _gather_weights(bufs):
    nb = len(GATHERED)
    kinds = [CHUNKS[k][0] for k in GATHERED]

    def body(*refs):
        outs = refs[nb:2 * nb]
        send_sems, recv_sems = refs[2 * nb:]
        x, y, c = _place()
        chips = _other_chips(x, y)

        def copy(k, b, layer, q, to):
            w = _window(outs[b], kinds[b], layer, q)
            return pltpu.make_async_remote_copy(src_ref=w, dst_ref=w, send_sem=send_sems.at[k], recv_sem=recv_sems.at[k],
                                                device_id=to, device_id_type=MESH)

        first = [copy(3 * b + j, b, c, 2 * x + y, (*chip, c)) for b in range(nb) for j, chip in enumerate(chips)]
        for cp in first:
            cp.start()
        passed = []
        for b in range(nb):
            for j, (cx, cy) in enumerate(chips):
                copy(3 * b + j, b, c, 2 * cx + cy, (x, y, c)).wait_recv()
                cp = copy(3 * nb + 3 * b + j, b, c, 2 * cx + cy, (x, y, 1 - c))
                cp.start()
                passed.append(cp)
        for b in range(nb):
            for j, (cx, cy) in enumerate(chips):
                copy(3 * nb + 3 * b + j, b, 1 - c, 2 * cx + cy, (x, y, c)).wait_recv()
        for cp in first + passed:
            cp.wait_send()

    return pl.pallas_call(
        body, name='gather_weights',
        out_shape=tuple(jax.ShapeDtypeStruct(b.shape, b.dtype) for b in bufs),
        in_specs=[HBM_SPEC] * nb, out_specs=tuple([HBM_SPEC] * nb),
        scratch_shapes=[pltpu.SemaphoreType.DMA((6 * nb,)), pltpu.SemaphoreType.DMA((6 * nb,))],
        input_output_aliases={b: b for b in range(nb)},
    )(*bufs)


def _elementwise(body, ins, prefetch, in_maps, out_shapes, out_maps, block, grid, *, name):
    def spec(shape, imap):
        lead = len(shape) - 2
        return pl.BlockSpec((None,) * lead + block, imap)
    return pl.pallas_call(
        body, name=name,
        out_shape=tuple(out_shapes),
        grid_spec=pltpu.PrefetchScalarGridSpec(
            num_scalar_prefetch=len(prefetch), grid=grid,
            in_specs=[spec(a.shape, m) for a, m in zip(ins, in_maps)],
            out_specs=tuple(spec(s.shape, m) for s, m in zip(out_shapes, out_maps))),
        compiler_params=_params(*(['parallel'] * len(grid))),
    )(*prefetch, *ins)


def _rs_block(M, N):
    return (_tile(M, 256, 16), _tile(N, 2048))


def _rs_cast_other_layer(g, c_arr, *, name):
    _, M, N = g.shape
    bm, bn = _rs_block(M, N)

    def body(c_ref, g_ref, o_ref):
        o_ref[...] = g_ref[...].astype(BF16)

    return _elementwise(body, [g], [c_arr], [lambda i, j, c: (1 - c[0], i, j)],
                        [jax.ShapeDtypeStruct((M, N), BF16)], [lambda i, j, c: (i, j)], (bm, bn), (M // bm, N // bn), name=name)[0]


def _rs_add_sibling(g, ra, c_arr, *, name, want_bf16):
    _, M, N = g.shape
    bm, bn = _rs_block(M, N)

    def body(c_ref, g_ref, r_ref, p_ref, *pb_ref):
        p = g_ref[...] + r_ref[...].astype(F32)
        p_ref[...] = p
        if want_bf16:
            pb_ref[0][...] = p.astype(BF16)

    flat = lambda i, j, c: (i, j)
    outs = [jax.ShapeDtypeStruct((M, N), F32)] + ([jax.ShapeDtypeStruct((M, N), BF16)] if want_bf16 else [])
    return _elementwise(body, [g, ra], [c_arr], [lambda i, j, c: (c[0], i, j), flat], outs, [flat] * len(outs),
                        (bm, bn), (M // bm, N // bn), name=name)


def _rs_finish(p, rb, kind, chip_arr, c_arr, *, name):
    m, n = rb.shape[1:]
    bm, bn = _rs_block(m, n)
    nbm, nbn = m // bm, n // bn

    def body(q_ref, c_ref, p_ref, r_ref, o_ref):
        acc = p_ref[...]
        for j in range(N_CHIPS - 1):
            acc = acc + r_ref[j].astype(F32)
        o_ref[...] = acc

    if kind == 'lead':
        p_map = lambda i, j, q, c: (q[0], i, j)
    elif kind == 'rows':
        p_map = lambda i, j, q, c: (q[0] * nbm + i, j)
    else:
        p_map = lambda i, j, q, c: (i, q[0] * nbn + j)
    r_spec = pl.BlockSpec((N_CHIPS - 1, bm, bn), lambda i, j, q, c: (0, i, j))
    p_spec = pl.BlockSpec(((None,) if kind == 'lead' else ()) + (bm, bn), p_map)
    return pl.pallas_call(
        body, name=name, out_shape=jax.ShapeDtypeStruct((2, m, n), F32),
        grid_spec=pltpu.PrefetchScalarGridSpec(
            num_scalar_prefetch=2, grid=(nbm, nbn), in_specs=[p_spec, r_spec],
            out_specs=pl.BlockSpec((None, bm, bn), lambda i, j, q, c: (c[0], i, j))),
        compiler_params=_params('parallel', 'parallel'),
    )(chip_arr, c_arr, p, rb)


def _swap_with_sibling(bufs):
    nb = len(bufs)

    def body(*refs):
        ins, outs = refs[:nb], refs[nb:2 * nb]
        send_sems, recv_sems = refs[2 * nb:]
        x, y, c = _place()
        copies = [pltpu.make_async_remote_copy(src_ref=ins[b], dst_ref=outs[b], send_sem=send_sems.at[b],
                                               recv_sem=recv_sems.at[b], device_id=(x, y, 1 - c), device_id_type=MESH)
                  for b in range(nb)]
        for cp in copies:
            cp.start()
        for cp in copies:
            cp.wait()

    return pl.pallas_call(
        body, name='rs_swap_layers', out_shape=tuple(jax.ShapeDtypeStruct(b.shape, b.dtype) for b in bufs),
        in_specs=[HBM_SPEC] * nb, out_specs=tuple([HBM_SPEC] * nb),
        scratch_shapes=[pltpu.SemaphoreType.DMA((nb,)), pltpu.SemaphoreType.DMA((nb,))],
    )(*bufs)


def _send_chip_partials(bufs, kinds):
    nb = len(bufs)

    def chunk_shape(b, kind):
        if kind == 'lead':
            return b.shape[1:]
        if kind == 'rows':
            return (b.shape[0] // N_CHIPS, b.shape[1])
        return (b.shape[0], b.shape[1] // N_CHIPS)

    def body(*refs):
        ins, outs = refs[:nb], refs[nb:2 * nb]
        send_sems, recv_sems = refs[2 * nb:]
        x, y, c = _place()
        copies = []
        for b in range(nb):
            for j, (cx, cy) in enumerate(_other_chips(x, y)):
                k = 3 * b + j
                copies.append(pltpu.make_async_remote_copy(
                    src_ref=_window(ins[b], kinds[b], None, 2 * cx + cy), dst_ref=outs[b].at[j],
                    send_sem=send_sems.at[k], recv_sem=recv_sems.at[k], device_id=(cx, cy, c), device_id_type=MESH))
        for cp in copies:
            cp.start()
        for cp in copies:
            cp.wait()

    return pl.pallas_call(
        body, name='rs_send_partials',
        out_shape=tuple(jax.ShapeDtypeStruct((N_CHIPS - 1,) + chunk_shape(b, k), b.dtype) for b, k in zip(bufs, kinds)),
        in_specs=[HBM_SPEC] * nb, out_specs=tuple([HBM_SPEC] * nb),
        scratch_shapes=[pltpu.SemaphoreType.DMA((3 * nb,)), pltpu.SemaphoreType.DMA((3 * nb,))],
    )(*bufs)


def _share_layers(bufs):
    nb = len(bufs)

    def body(*refs):
        outs = refs[nb:2 * nb]
        send_sems, recv_sems = refs[2 * nb:]
        x, y, c = _place()
        copies = [pltpu.make_async_remote_copy(src_ref=outs[b].at[c], dst_ref=outs[b].at[c], send_sem=send_sems.at[b],
                                               recv_sem=recv_sems.at[b], device_id=(x, y, 1 - c), device_id_type=MESH)
                  for b in range(nb)]
        for cp in copies:
            cp.start()
        for b in range(nb):
            copies[b].wait_send()
            pltpu.make_async_remote_copy(src_ref=outs[b].at[1 - c], dst_ref=outs[b].at[1 - c], send_sem=send_sems.at[b],
                                         recv_sem=recv_sems.at[b], device_id=(x, y, 1 - c), device_id_type=MESH).wait_recv()

    return pl.pallas_call(
        body, name='rs_share_layers', out_shape=tuple(jax.ShapeDtypeStruct(b.shape, b.dtype) for b in bufs),
        in_specs=[HBM_SPEC] * nb, out_specs=tuple([HBM_SPEC] * nb),
        scratch_shapes=[pltpu.SemaphoreType.DMA((nb,)), pltpu.SemaphoreType.DMA((nb,))],
        input_output_aliases={b: b for b in range(nb)},
    )(*bufs)


def _all_reduce_small(v):
    r = v.shape[0]

    def body(x_ref, tot_ref, all_ref, send_sems, recv_sems):
        x, y, c = _place()
        me, sibling = (x, y, c), (x, y, 1 - c)
        chips = _other_chips(x, y)

        def rows(px, py, pc):
            return all_ref.at[pl.ds((4 * px + 2 * py + pc) * r, r), :]

        def copy(k, block, to, src=None):
            return pltpu.make_async_remote_copy(
                src_ref=rows(*block) if src is None else src, dst_ref=rows(*block),
                send_sem=send_sems.at[k], recv_sem=recv_sems.at[k], device_id=to, device_id_type=MESH)

        rows(*me)[...] = x_ref[...]
        first = [copy(0, me, sibling, src=x_ref)]
        first += [copy(1 + j, me, (*chip, c), src=x_ref) for j, chip in enumerate(chips)]
        for cp in first:
            cp.start()
        passed = [copy(4 + j, (*chip, c), sibling) for j, chip in enumerate(chips)]
        for j, chip in enumerate(chips):
            copy(1 + j, (*chip, c), me).wait_recv()
            passed[j].start()
        copy(0, sibling, me).wait_recv()
        for j, chip in enumerate(chips):
            copy(4 + j, (*chip, 1 - c), me).wait_recv()
        for cp in first + passed:
            cp.wait_send()
        acc = all_ref[0:r, :]
        for d in range(1, N_DEV):
            acc = acc + all_ref[d * r:(d + 1) * r, :]
        tot_ref[...] = acc

    return pl.pallas_call(
        body, name='all_reduce_small', out_shape=jax.ShapeDtypeStruct((r, LANES), F32),
        in_specs=[pl.BlockSpec(memory_space=pltpu.VMEM)], out_specs=pl.BlockSpec(memory_space=pltpu.VMEM),
        scratch_shapes=[pltpu.VMEM((N_DEV * r, LANES), F32), pltpu.SemaphoreType.DMA((7,)), pltpu.SemaphoreType.DMA((7,))],
    )(v)


def _adamw(w, g, m, v, *, name, ts=256):
    R, C = w.shape
    Cg = g.shape[1]
    ts = _tile(R, ts, 8)
    c1 = 1.0 - ADAM_B1 ** ADAM_STEP
    c2 = 1.0 - ADAM_B2 ** ADAM_STEP

    def body(w_ref, g_ref, m_ref, v_ref, go_ref, d_ref, nm_ref, nv_ref):
        gv = g_ref[:, 0:C]
        nm = ADAM_B1 * m_ref[...] + (1.0 - ADAM_B1) * gv
        nv = ADAM_B2 * v_ref[...] + (1.0 - ADAM_B2) * (gv * gv)
        d_ref[...] = -ADAM_LR * ((nm / c1) / (jnp.sqrt(nv / c2) + ADAM_EPS) + ADAM_WD * w_ref[...])
        go_ref[...] = gv
        nm_ref[...] = nm
        nv_ref[...] = nv

    blk = pl.BlockSpec((ts, C), lambda i: (i, 0))
    shape = jax.ShapeDtypeStruct((R, C), F32)
    return pl.pallas_call(body, name=name, out_shape=(shape, shape, shape, shape), grid=(R // ts,),
                          in_specs=[blk, pl.BlockSpec((ts, Cg), lambda i: (i, 0)), blk, blk], out_specs=(blk, blk, blk, blk),
                          compiler_params=_params('parallel'))(w, g, m, v)


def _pack_rows(parts, row_unit):
    flat = jnp.concatenate(parts)
    flat = _pad_axis(flat, -(-flat.shape[0] // (row_unit * LANES)) * row_unit * LANES, 0)
    return flat.reshape(-1, LANES)


def _as_2d(a):
    return a.reshape(-1, a.shape[-1])


def _mesh_place():
    cx, cy, cc = _place()
    chip = 2 * cx + cy
    as_arr = lambda v: jnp.reshape(v, (1,)).astype(jnp.int32)
    return chip, as_arr(cc), as_arr(chip), as_arr(_up_pos(chip))


def _gather_full(w, place):
    chip, _, chip_arr, up_pos_arr = place
    L, D = w['w_in'].shape[:2]

    def full_shape(k, kind):
        s = w[k].shape
        return (L, N_CHIPS * s[1], s[2]) if kind == 'rows' else (L, s[1], N_CHIPS * s[2])

    placed = {
        'w_in': _place_shard(_pad_axis(w['w_in'], W_IN_PAD, 2), chip_arr, (L, N_CHIPS, D, W_IN_PAD), 'lead', name='place_w_in'),
        'w_up': _place_shard(w['w_up'], up_pos_arr, full_shape('w_up', 'cols'), 'cols', name='place_w_up'),
    }
    for k in ('w_down', 'w_out', 'conv_pw_w'):
        placed[k] = _place_shard(w[k], chip_arr, full_shape(k, 'rows'), 'rows', name='place_' + k)
    for k in ('conv_dw_w', 'ffn_dw_w'):
        placed[k] = lax.dynamic_update_slice_in_dim(jnp.zeros((L, N_CHIPS) + w[k].shape[1:], F32), w[k][:, None], chip, axis=1)
    full = dict(zip(GATHERED, _gather_weights([placed[k] for k in GATHERED])))
    w_a, w_fg = _select_w_in(full['w_in'])
    unchunk = lambda a: jnp.moveaxis(a, 1, 2).reshape(a.shape[0], a.shape[2], -1)
    return dict(w_a=w_a, w_fg_t=jnp.swapaxes(w_fg, 1, 2), w_up=full['w_up'], w_down=full['w_down'], w_out=full['w_out'],
                pw_w=full['conv_pw_w'],
                dw_w=_pad_axis(unchunk(full['conv_dw_w']), CONV_HALO, 1),
                ffn_w=_pad_axis(_pair_cols(unchunk(full['ffn_dw_w'])), FFN_HALO, 1))


RS_WIRE = ('w_in', 'w_up', 'w_down', 'w_out')


def _reduce_scatter(G, place):
    _, c_arr, chip_arr, up_pos_arr = place
    big = ('w_a', 'w_fg', 'w_up', 'w_down', 'w_out')
    recv = _swap_with_sibling([_rs_cast_other_layer(G[k], c_arr, name='rs_cast_' + k) for k in big])
    part, part_b = {}, {}
    for k, ra in zip(big, recv):
        res = _rs_add_sibling(G[k], ra, c_arr, name='rs_add_' + k, want_bf16=k not in ('w_a', 'w_fg'))
        part[k] = res[0]
        if len(res) > 1:
            part_b[k] = res[1]
    part['w_in'], part_b['w_in'] = _select_w_in_grads(part['w_a'], part['w_fg'])
    kinds = [CHUNKS[k][0] for k in RS_WIRE]
    partials = _send_chip_partials([part_b[k] for k in RS_WIRE], kinds)
    fin = [_rs_finish(part[k], rb, kind, up_pos_arr if kind == 'cols' else chip_arr, c_arr, name='rs_finish_' + k)
           for k, rb, kind in zip(RS_WIRE, partials, kinds)]
    return dict(zip(RS_WIRE, _share_layers(fin)))


def kernel(x, norm1_g, w_in, b_f, q_norm_g, k_norm_g, conv_dw_w, conv_dw_b, conv_ln_g, conv_ln_b, conv_pw_w, pool_w, pool_scale, w_out, norm2_g, w_up, ffn_dw_w, w_down, loss_target, m_norm1_g, m_w_in, m_b_f, m_q_norm_g, m_k_norm_g, m_conv_dw_w, m_conv_dw_b, m_conv_ln_g, m_conv_ln_b, m_conv_pw_w, m_pool_w, m_pool_scale, m_w_out, m_norm2_g, m_w_up, m_ffn_dw_w, m_w_down, v_norm1_g, v_w_in, v_b_f, v_q_norm_g, v_k_norm_g, v_conv_dw_w, v_conv_dw_b, v_conv_ln_g, v_conv_ln_b, v_conv_pw_w, v_pool_w, v_pool_scale, v_w_out, v_norm2_g, v_w_up, v_ffn_dw_w, v_w_down):
    given = dict(locals())
    w = {k: given[k] for k in WEIGHTS}
    mom_m = {k: given['m_' + k] for k in WEIGHTS}
    mom_v = {k: given['v_' + k] for k in WEIGHTS}
    place = _mesh_place()
    chip = place[0]
    W = _gather_full(w, place)

    loss_part, grad_x, G, g_small = _local_step(x[0], loss_target[0], W, {k: w[k] for k in REPLICATED})
    loss = lax.psum(loss_part[0, 0], ('x', 'y', 'c'))

    g_sum = _reduce_scatter(G, place)

    small = _pack_rows([g_small[k].reshape(-1) for k in SMALL_GRADS], 8)
    small_sum = _all_reduce_small(small)

    delta, new_m, new_v = {}, {}, {}
    for k in RS_WIRE:
        outs = _adamw(_as_2d(w[k]), _as_2d(g_sum[k]), _as_2d(mom_m[k]), _as_2d(mom_v[k]), name='adamw_' + k)
        g_sum[k], delta[k], new_m[k], new_v[k] = [o.reshape(w[k].shape) for o in outs]
    off = 0
    small_full = {}
    for k in SMALL_GRADS:
        small_full[k] = small_sum.reshape(-1)[off:off + g_small[k].size].reshape(g_small[k].shape)
        off += g_small[k].size
    small_g = {k: small_full[k] for k in REPLICATED}
    small_g['conv_dw_w'] = lax.dynamic_slice_in_dim(small_full['conv_dw_w'], chip * w['conv_dw_w'].shape[2], w['conv_dw_w'].shape[2], axis=2)
    small_g['conv_pw_w'] = lax.dynamic_slice_in_dim(small_full['conv_pw_w'], chip * w['conv_pw_w'].shape[1], w['conv_pw_w'].shape[1], axis=1)
    small_g['ffn_dw_w'] = lax.dynamic_slice_in_dim(small_full['ffn_dw_w'], chip * w['ffn_dw_w'].shape[2], w['ffn_dw_w'].shape[2], axis=2)
    pack_small = lambda t: _pack_rows([t[k].reshape(-1) for k in SMALL_GRADS], 8)
    outs = _adamw(pack_small(w), pack_small(small_g), pack_small(mom_m), pack_small(mom_v), name='adamw_small')
    off = 0
    for k in SMALL_GRADS:
        pieces = [o.reshape(-1)[off:off + w[k].size].reshape(w[k].shape) for o in outs]
        g_sum[k], delta[k], new_m[k], new_v[k] = pieces
        off += w[k].size

    return (loss, grad_x[None], *[g_sum[k] for k in WEIGHTS], *[delta[k] for k in WEIGHTS],
            *[new_m[k] for k in WEIGHTS], *[new_v[k] for k in WEIGHTS])
```

```python
import functools

import jax
import jax.numpy as jnp
from jax import lax
from jax.experimental import pallas as pl
from jax.experimental.pallas import tpu as pltpu

F32 = jnp.float32
BF16 = jnp.bfloat16

N_HEADS = 8
HEAD_DIM = 64
D_ATT = N_HEADS * HEAD_DIM
D_CONV = 256
D_POOL = 256
D_MIX = D_ATT + D_CONV + D_POOL
D_QKV = 3 * D_ATT
D_PROJ_A = D_QKV + 2 * D_CONV + D_POOL
D_IN = D_PROJ_A + N_HEADS
FG_ROWS = 128
CONV_WIDTH = 31
CONV_HALO = 32
POOL_WINDOWS = (2, 4, 8, 16)
POOL_GROUP = 64
POOL_HALO = 16
FFN_CONV_WIDTH = 3
FFN_HALO = 8
ATT_SCALE = HEAD_DIM ** -0.5
EPS = 1e-6
NEG = -1e30
LANES = 128

ADAM_LR = 0.001
ADAM_B1 = 0.9
ADAM_B2 = 0.999
ADAM_EPS = 1e-08
ADAM_WD = 0.01
ADAM_STEP = 10

N_CHIPS = 4
N_DEV = 8
VMEM_LIMIT_BYTES = 56 * 1024 * 1024

REPLICATED = ('norm1_g', 'b_f', 'q_norm_g', 'k_norm_g', 'conv_dw_b', 'conv_ln_g', 'conv_ln_b',
              'pool_w', 'pool_scale', 'norm2_g')
WEIGHTS = ('norm1_g', 'w_in', 'b_f', 'q_norm_g', 'k_norm_g', 'conv_dw_w', 'conv_dw_b', 'conv_ln_g',
           'conv_ln_b', 'conv_pw_w', 'pool_w', 'pool_scale', 'w_out', 'norm2_g', 'w_up', 'ffn_dw_w', 'w_down')


def _tile(dim, pref, unit=LANES):
    if dim <= pref:
        return dim
    t = (pref // unit) * unit
    while t >= unit:
        if dim % t == 0:
            return t
        t -= unit
    raise ValueError(f'no tile for {dim} (preferred {pref})')


def _params(*sem):
    return pltpu.CompilerParams(dimension_semantics=sem, vmem_limit_bytes=VMEM_LIMIT_BYTES)


def _sigmoid(x):
    return 1.0 / (1.0 + jnp.exp(-x))


def _dot(a, b, ca, cb):
    return lax.dot_general(a, b, (((ca,), (cb,)), ((), ())), preferred_element_type=F32)


def _split3(y):
    y1 = y.astype(BF16)
    r1 = y - y1.astype(F32)
    y2 = r1.astype(BF16)
    y3 = (r1 - y2.astype(F32)).astype(BF16)
    return y1, y2, y3


def _dot3(y, e, ca=1, cb=0):
    y1, y2, y3 = _split3(y)
    return _dot(y1, e, ca, cb) + _dot(y2, e, ca, cb) + _dot(y3, e, ca, cb)


def _lead(spec_shape, imap, lead):
    if lead is None:
        return pl.BlockSpec(spec_shape, imap)
    return pl.BlockSpec((None,) + spec_shape, lambda *g: (lead,) + imap(*g))


ANY_SPEC = pl.BlockSpec(memory_space=pl.ANY)


def _mm(a, b, *, name, ta=False, tb=False, res=None, out_dtype=F32, tm=512, tn=512, tk=1024,
        a_lead=None, b_lead=None, out_lead=None, out_depth=None, out_buf=None):
    a2, b2 = a.shape[-2:], b.shape[-2:]
    K, M = a2 if ta else a2[::-1]
    N, Kb = b2 if tb else b2[::-1]
    assert K == Kb, (a.shape, b.shape)
    tm, tn, tk = _tile(M, tm), _tile(N, tn), _tile(K, tk)
    nk = K // tk
    ca = 0 if ta else 1
    cb = 1 if tb else 0
    has_res = res is not None
    has_buf = out_buf is not None
    n_in = 2 + has_res + has_buf

    def body(*refs):
        a_ref, b_ref = refs[:2]
        r_ref = refs[2] if has_res else None
        o_ref = refs[n_in]
        scratch = refs[n_in + 1:]
        p = _dot(a_ref[...].astype(BF16), b_ref[...].astype(BF16), ca, cb)
        if nk == 1:
            if has_res:
                p = p + r_ref[...]
            o_ref[...] = p.astype(out_dtype)
        else:
            acc = scratch[0]
            k = pl.program_id(2)

            @pl.when(k == 0)
            def _():
                acc[...] = p

            @pl.when(k > 0)
            def _():
                acc[...] += p

            @pl.when(k == nk - 1)
            def _():
                r = acc[...]
                if has_res:
                    r = r + r_ref[...]
                o_ref[...] = r.astype(out_dtype)

    a_spec = _lead((tk, tm), lambda i, j, k: (k, i), a_lead) if ta else _lead((tm, tk), lambda i, j, k: (i, k), a_lead)
    b_spec = _lead((tn, tk), lambda i, j, k: (j, k), b_lead) if tb else _lead((tk, tn), lambda i, j, k: (k, j), b_lead)
    o_map = lambda i, j, k: (i, j)
    in_specs = [a_spec, b_spec] + ([pl.BlockSpec((tm, tn), o_map)] if has_res else []) + ([ANY_SPEC] if has_buf else [])
    args = (a, b) + ((res,) if has_res else ()) + ((out_buf,) if has_buf else ())
    out_shape = (M, N) if out_depth is None else (out_depth, M, N)
    return pl.pallas_call(
        body, name=name,
        out_shape=jax.ShapeDtypeStruct(out_shape, out_dtype),
        grid=(M // tm, N // tn, nk),
        in_specs=in_specs, out_specs=_lead((tm, tn), o_map, out_lead),
        scratch_shapes=[pltpu.VMEM((tm, tn), F32)] if nk > 1 else [],
        input_output_aliases={n_in - 1: 0} if has_buf else {},
        compiler_params=_params('parallel', 'parallel', 'arbitrary'),
    )(*args)


def _rms_fwd(x, g, *, name, ts=512):
    S, D = x.shape
    ts = _tile(S, ts, 8)

    def body(x_ref, g_ref, o_ref):
        xv = x_ref[...]
        r = lax.rsqrt(jnp.mean(xv * xv, axis=-1, keepdims=True) + EPS)
        o_ref[...] = (xv * r * g_ref[...]).astype(BF16)

    return pl.pallas_call(
        body, name=name, out_shape=jax.ShapeDtypeStruct((S, D), BF16), grid=(S // ts,),
        in_specs=[pl.BlockSpec((ts, D), lambda i: (i, 0)), pl.BlockSpec((1, D), lambda i: (0, 0))],
        out_specs=pl.BlockSpec((ts, D), lambda i: (i, 0)),
        compiler_params=_params('parallel'),
    )(x, g)


def _rms_bwd(x, g, dh, dres, *, name, ts=512):
    S, D = x.shape
    ts = _tile(S, ts, 8)

    def body(x_ref, g_ref, dh_ref, dr_ref, dx_ref, dg_ref):
        i = pl.program_id(0)
        xv = x_ref[...]
        r = lax.rsqrt(jnp.mean(xv * xv, axis=-1, keepdims=True) + EPS)
        y = xv * r
        dh_v = dh_ref[...]
        dy = dh_v * g_ref[...]
        dx_ref[...] = dr_ref[...] + r * (dy - y * jnp.mean(dy * y, axis=-1, keepdims=True))
        part = jnp.sum(dh_v * y, axis=0, keepdims=True)

        @pl.when(i == 0)
        def _():
            dg_ref[...] = part

        @pl.when(i > 0)
        def _():
            dg_ref[...] += part

    row = pl.BlockSpec((ts, D), lambda i: (i, 0))
    vec = pl.BlockSpec((1, D), lambda i: (0, 0))
    return pl.pallas_call(
        body, name=name,
        out_shape=(jax.ShapeDtypeStruct((S, D), F32), jax.ShapeDtypeStruct((1, D), F32)),
        grid=(S // ts,), in_specs=[row, vec, row, row], out_specs=(row, vec),
        compiler_params=_params('arbitrary'),
    )(x, g, dh, dres)


def _group_ones():
    i = lax.broadcasted_iota(jnp.int32, (D_ATT, D_ATT), 0) // HEAD_DIM
    j = lax.broadcasted_iota(jnp.int32, (D_ATT, D_ATT), 1) // HEAD_DIM
    return (i == j).astype(BF16)


def _qk_prep_fwd(proj_a, qg, kg, *, name, ts=512):
    S = proj_a.shape[0]
    ts = _tile(S, ts, 16)

    def body(q_ref, k_ref, v_ref, qg_ref, kg_ref, e_ref, o_ref):
        e = e_ref[...]

        def norm(xv, gain):
            ms = _dot3(xv * xv, e) * (1.0 / HEAD_DIM)
            return xv * lax.rsqrt(ms + EPS) * gain

        o_ref[:, 0:D_ATT] = (norm(q_ref[...], qg_ref[...]) * ATT_SCALE).astype(BF16)
        o_ref[:, D_ATT:2 * D_ATT] = norm(k_ref[...], kg_ref[...]).astype(BF16)
        o_ref[:, 2 * D_ATT:3 * D_ATT] = v_ref[...].astype(BF16)

    col = lambda c: pl.BlockSpec((ts, D_ATT), lambda i: (i, c))
    vec = pl.BlockSpec((1, D_ATT), lambda i: (0, 0))
    return pl.pallas_call(
        body, name=name, out_shape=jax.ShapeDtypeStruct((S, D_QKV), BF16), grid=(S // ts,),
        in_specs=[col(0), col(1), col(2), vec, vec, pl.BlockSpec((D_ATT, D_ATT), lambda i: (0, 0))],
        out_specs=pl.BlockSpec((ts, D_QKV), lambda i: (i, 0)),
        compiler_params=_params('parallel'),
    )(proj_a, proj_a, proj_a, qg, kg, _group_ones())


def _qk_prep_bwd(proj_a, dq, dk, dv, qg, kg, *, name, ts=512):
    S = proj_a.shape[0]
    ts = _tile(S, ts, 16)

    def body(q_ref, k_ref, dq_ref, dk_ref, dv_ref, qg_ref, kg_ref, e_ref, o_ref, dqg_ref, dkg_ref):
        i = pl.program_id(0)
        e = e_ref[...]

        def norm_bwd(xv, dn, gain, scale):
            ms = _dot3(xv * xv, e) * (1.0 / HEAD_DIM)
            r = lax.rsqrt(ms + EPS)
            y = xv * r
            dy = dn * (gain * scale)
            mean = _dot3(dy * y, e) * (1.0 / HEAD_DIM)
            return r * (dy - y * mean), jnp.sum(dn * y, axis=0, keepdims=True) * scale

        dq_raw, dqg = norm_bwd(q_ref[...], dq_ref[...], qg_ref[...], ATT_SCALE)
        dk_raw, dkg = norm_bwd(k_ref[...], dk_ref[...], kg_ref[...], 1.0)
        o_ref[:, 0:D_ATT] = dq_raw.astype(BF16)
        o_ref[:, D_ATT:2 * D_ATT] = dk_raw.astype(BF16)
        o_ref[:, 2 * D_ATT:3 * D_ATT] = dv_ref[...].astype(BF16)

        @pl.when(i == 0)
        def _():
            dqg_ref[...] = dqg
            dkg_ref[...] = dkg

        @pl.when(i > 0)
        def _():
            dqg_ref[...] += dqg
            dkg_ref[...] += dkg

    col = lambda c: pl.BlockSpec((ts, D_ATT), lambda i: (i, c))
    vec = pl.BlockSpec((1, D_ATT), lambda i: (0, 0))
    return pl.pallas_call(
        body, name=name,
        out_shape=(jax.ShapeDtypeStruct((S, D_PROJ_A), BF16), jax.ShapeDtypeStruct((1, D_ATT), F32),
                   jax.ShapeDtypeStruct((1, D_ATT), F32)),
        grid=(S // ts,),
        in_specs=[col(0), col(1), col(0), col(0), col(0), vec, vec, pl.BlockSpec((D_ATT, D_ATT), lambda i: (0, 0))],
        out_specs=(pl.BlockSpec((ts, D_QKV), lambda i: (i, 0)), vec, vec),
        compiler_params=_params('arbitrary'),
    )(proj_a, proj_a, dq, dk, dv, qg, kg, _group_ones())


def _tri_ones(upper):
    i = lax.broadcasted_iota(jnp.int32, (LANES, LANES), 0)
    j = lax.broadcasted_iota(jnp.int32, (LANES, LANES), 1)
    return ((i <= j) if upper else (i >= j)).astype(BF16)


def _forget_fwd(z_raw, b_col, *, name):
    R, S = z_raw.shape
    nb = S // LANES

    def body(z_ref, b_ref, u_ref, f_ref):
        u = u_ref[...]
        carry = jnp.zeros((R, 1), F32)
        for j in range(nb):
            z = z_ref[:, j * LANES:(j + 1) * LANES] + b_ref[...]
            logf = jnp.minimum(z, 0.0) - jnp.log(1.0 + jnp.exp(-jnp.abs(z)))
            f_ref[:, j * LANES:(j + 1) * LANES] = _dot3(logf, u) + carry
            carry = carry + jnp.sum(logf, axis=1, keepdims=True)

    return pl.pallas_call(
        body, name=name, out_shape=jax.ShapeDtypeStruct((R, S), F32),
        compiler_params=pltpu.CompilerParams(vmem_limit_bytes=VMEM_LIMIT_BYTES),
    )(z_raw, b_col, _tri_ones(True))


def _forget_bwd(z_raw, b_col, df, *, name):
    R, S = z_raw.shape
    nb = S // LANES

    def body(z_ref, b_ref, df_ref, l_ref, dz_ref, db_ref):
        low = l_ref[...]
        carry = jnp.zeros((R, 1), F32)
        db = jnp.zeros((R, 1), F32)
        for j in reversed(range(nb)):
            d = df_ref[:, j * LANES:(j + 1) * LANES]
            dlogf = _dot3(d, low) + carry
            carry = carry + jnp.sum(d, axis=1, keepdims=True)
            z = z_ref[:, j * LANES:(j + 1) * LANES] + b_ref[...]
            dz = dlogf * _sigmoid(-z)
            dz_ref[:, j * LANES:(j + 1) * LANES] = dz
            db = db + jnp.sum(dz, axis=1, keepdims=True)
        db_ref[...] = db

    return pl.pallas_call(
        body, name=name,
        out_shape=(jax.ShapeDtypeStruct((R, S), F32), jax.ShapeDtypeStruct((R, 1), F32)),
        compiler_params=pltpu.CompilerParams(vmem_limit_bytes=VMEM_LIMIT_BYTES),
    )(z_raw, b_col, df, _tri_ones(False))


def _head_mask(hh):
    lane = lax.broadcasted_iota(jnp.int32, (1, LANES), 1)
    return (lane // HEAD_DIM) == hh


def _causal(s, qi, ki, t):
    rows = qi * t + lax.broadcasted_iota(jnp.int32, (t, t), 0)
    cols = ki * t + lax.broadcasted_iota(jnp.int32, (t, t), 1)
    return jnp.where(cols <= rows, s, NEG)


def _attn_fwd(qkv, f3, *, name, t=512):
    S = qkv.shape[0]
    t = _tile(S, t)
    n = S // t
    npair = N_HEADS // 2

    def body(q_ref, k_ref, v_ref, f_ref, mix_ref, o_ref, lse_ref, m_s, l_s, acc_s):
        qi, ki = pl.program_id(1), pl.program_id(2)

        @pl.when(ki == 0)
        def _():
            m_s[...] = jnp.full(m_s.shape, NEG, F32)
            l_s[...] = jnp.zeros(l_s.shape, F32)
            acc_s[...] = jnp.zeros(acc_s.shape, F32)

        @pl.when(ki <= qi)
        def _():
            q, k, v = q_ref[...], k_ref[...], v_ref[...]
            for hh in range(2):
                msk = _head_mask(hh)
                qm = jnp.where(msk, q, jnp.zeros_like(q))
                vm = jnp.where(msk, v, jnp.zeros_like(v))
                s = _causal(_dot(qm, k, 1, 1) - f_ref[0, hh:hh + 1, :], qi, ki, t)
                m_prev = m_s[hh]
                m_new = jnp.maximum(m_prev, jnp.max(s, axis=1, keepdims=True))
                alpha = jnp.exp(m_prev - m_new)
                p = jnp.exp(s - m_new)
                l_s[hh] = alpha * l_s[hh] + jnp.sum(p, axis=1, keepdims=True)
                acc_s[hh] = alpha * acc_s[hh] + _dot(p.astype(BF16), vm, 1, 0)
                m_s[hh] = m_new

        @pl.when(ki == qi)
        def _():
            o = acc_s[0] / l_s[0] + acc_s[1] / l_s[1]
            o_ref[...] = o
            mix_ref[...] = o.astype(BF16)
            lse0 = m_s[0] + jnp.log(l_s[0])
            lse1 = m_s[1] + jnp.log(l_s[1])
            lse_ref[...] = jnp.where(_head_mask(0), lse0, lse1)

    out = pl.BlockSpec((t, LANES), lambda h, i, j: (i, h))
    return pl.pallas_call(
        body, name=name,
        out_shape=(jax.ShapeDtypeStruct((S, D_MIX), BF16), jax.ShapeDtypeStruct((S, D_ATT), F32),
                   jax.ShapeDtypeStruct((S, D_ATT), F32)),
        grid=(npair, n, n),
        in_specs=[pl.BlockSpec((t, LANES), lambda h, i, j: (i, h)),
                  pl.BlockSpec((t, LANES), lambda h, i, j: (jnp.minimum(i, j), npair + h)),
                  pl.BlockSpec((t, LANES), lambda h, i, j: (jnp.minimum(i, j), 2 * npair + h)),
                  pl.BlockSpec((1, 2, t), lambda h, i, j: (h, 0, jnp.minimum(i, j)))],
        out_specs=(out, out, out),
        scratch_shapes=[pltpu.VMEM((2, t, 1), F32), pltpu.VMEM((2, t, 1), F32), pltpu.VMEM((2, t, LANES), F32)],
        compiler_params=_params('parallel', 'parallel', 'arbitrary'),
    )(qkv, qkv, qkv, f3)


def _attn_bwd(qkv, f3, att, lse, d_mix, *, name, t=512):
    S = qkv.shape[0]
    t = _tile(S, t)
    n = S // t
    npair = N_HEADS // 2

    def body(q_ref, k_ref, v_ref, f_ref, o_ref, lse_ref, do_ref, dq_ref, dk_ref, dv_ref, df_ref, dr_ref, dk_s, dv_s, df_s):
        ki, qi = pl.program_id(1), pl.program_id(2)

        @pl.when(qi == ki)
        def _():
            dk_s[...] = jnp.zeros(dk_s.shape, F32)
            dv_s[...] = jnp.zeros(dv_s.shape, F32)
            df_s[...] = jnp.zeros(df_s.shape, F32)

        @pl.when(qi >= ki)
        def _():
            q, k, v = q_ref[...], k_ref[...], v_ref[...]
            do, o, lse = do_ref[...], o_ref[...], lse_ref[...]
            dq_blk = jnp.zeros((t, LANES), F32)
            dr_blk = jnp.zeros((t, LANES), F32)
            for hh in range(2):
                msk = _head_mask(hh)
                qm = jnp.where(msk, q, jnp.zeros_like(q))
                km = jnp.where(msk, k, jnp.zeros_like(k))
                dom = jnp.where(msk, do, 0.0).astype(BF16)
                s = _causal(_dot(qm, k, 1, 1) - f_ref[0, hh:hh + 1, :], qi, ki, t)
                lse_h = jnp.max(jnp.where(msk, lse, NEG), axis=1, keepdims=True)
                p = jnp.exp(s - lse_h)
                dp = _dot(dom, v, 1, 1)
                delta = jnp.sum(dom.astype(F32) * o, axis=1, keepdims=True)
                ds = p * (dp - delta)
                dsb = ds.astype(BF16)
                dv_s[...] += _dot(p.astype(BF16), dom, 0, 0)
                dk_s[...] += _dot(dsb, qm, 0, 0)
                dq_blk = dq_blk + _dot(dsb, km, 1, 0)
                df_s[hh] -= jnp.sum(ds, axis=0, keepdims=True)
                dr_blk = dr_blk + jnp.where(msk, jnp.sum(ds, axis=1, keepdims=True), 0.0)
            rows = pl.ds(pl.multiple_of(qi * t, t), t)

            @pl.when(ki == 0)
            def _():
                dq_ref[rows, :] = dq_blk
                dr_ref[rows, :] = dr_blk

            @pl.when(ki > 0)
            def _():
                dq_ref[rows, :] += dq_blk
                dr_ref[rows, :] += dr_blk

        @pl.when(qi == n - 1)
        def _():
            dk_ref[...] = dk_s[...]
            dv_ref[...] = dv_s[...]
            df_ref[0, 0:1, :] = df_s[0]
            df_ref[0, 1:2, :] = df_s[1]

    qrow = lambda h, j, i: (jnp.maximum(i, j), h)
    return pl.pallas_call(
        body, name=name,
        out_shape=(jax.ShapeDtypeStruct((S, D_ATT), F32), jax.ShapeDtypeStruct((S, D_ATT), F32),
                   jax.ShapeDtypeStruct((S, D_ATT), F32), jax.ShapeDtypeStruct((npair, 2, S), F32),
                   jax.ShapeDtypeStruct((S, D_ATT), F32)),
        grid=(npair, n, n),
        in_specs=[pl.BlockSpec((t, LANES), qrow),
                  pl.BlockSpec((t, LANES), lambda h, j, i: (j, npair + h)),
                  pl.BlockSpec((t, LANES), lambda h, j, i: (j, 2 * npair + h)),
                  pl.BlockSpec((1, 2, t), lambda h, j, i: (h, 0, j)),
                  pl.BlockSpec((t, LANES), qrow),
                  pl.BlockSpec((t, LANES), qrow),
                  pl.BlockSpec((t, LANES), qrow)],
        out_specs=(pl.BlockSpec((S, LANES), lambda h, j, i: (0, h)),
                   pl.BlockSpec((t, LANES), lambda h, j, i: (j, h)),
                   pl.BlockSpec((t, LANES), lambda h, j, i: (j, h)),
                   pl.BlockSpec((1, 2, t), lambda h, j, i: (h, 0, j)),
                   pl.BlockSpec((S, LANES), lambda h, j, i: (0, h))),
        scratch_shapes=[pltpu.VMEM((t, LANES), F32), pltpu.VMEM((t, LANES), F32), pltpu.VMEM((2, 1, t), F32)],
        compiler_params=_params('parallel', 'arbitrary', 'arbitrary'),
    )(qkv, qkv, qkv, f3, att, lse, d_mix)


A_COL = D_QKV // D_CONV
B_COL = A_COL + 1
P_COL = B_COL + 1


def _layer_norm_stats(c):
    mu = jnp.mean(c, axis=-1, keepdims=True)
    xc = c - mu
    rstd = lax.rsqrt(jnp.mean(xc * xc, axis=-1, keepdims=True) + EPS)
    return xc * rstd, rstd


def _glu_into(buf, a_ref, b_ref, ah_ref, bh_ref, first, ts):
    halo = ah_ref[...] * _sigmoid(bh_ref[...])
    buf[0:CONV_HALO, :] = jnp.where(first, 0.0, halo)
    buf[CONV_HALO:CONV_HALO + ts, :] = a_ref[...] * _sigmoid(b_ref[...])


def _dwconv(buf, w_ref, ts):
    off = CONV_HALO - (CONV_WIDTH - 1)
    acc = w_ref[0:1, :] * buf[pl.ds(off, ts), :]
    for k in range(1, CONV_WIDTH):
        acc = acc + w_ref[k:k + 1, :] * buf[pl.ds(off + k, ts), :]
    return acc


def _conv_specs(ts, tmap):
    hb = ts // CONV_HALO
    cur = lambda c: pl.BlockSpec((ts, D_CONV), lambda i: (tmap(i), c))
    halo = lambda c: pl.BlockSpec((CONV_HALO, D_CONV), lambda i: (jnp.maximum(tmap(i) * hb - 1, 0), c))
    return cur, halo


def _conv_fwd(proj_a, mix, dw_w, dw_b, ln_g, ln_b, pw_w, l, *, name, ts=512):
    S = proj_a.shape[0]
    ts = _tile(S, ts, CONV_HALO)

    def body(a_ref, b_ref, ah_ref, bh_ref, w_ref, wb_ref, g_ref, bb_ref, pw_ref, mix_in, o_ref, buf):
        _glu_into(buf, a_ref, b_ref, ah_ref, bh_ref, pl.program_id(0) == 0, ts)
        c = _dwconv(buf, w_ref, ts) + wb_ref[...]
        yhat, _ = _layer_norm_stats(c)
        y = yhat * g_ref[...] + bb_ref[...]
        hs = y * _sigmoid(y)
        o_ref[...] = _dot(hs.astype(BF16), pw_ref[...], 1, 0).astype(BF16)

    cur, halo = _conv_specs(ts, lambda i: i)
    vec = pl.BlockSpec((1, D_CONV), lambda i: (0, 0))
    return pl.pallas_call(
        body, name=name, out_shape=jax.ShapeDtypeStruct(mix.shape, BF16), grid=(S // ts,),
        in_specs=[cur(A_COL), cur(B_COL), halo(A_COL), halo(B_COL),
                  pl.BlockSpec((None, CONV_HALO, D_CONV), lambda i: (l, 0, 0)), vec, vec, vec,
                  pl.BlockSpec((None, D_CONV, D_CONV), lambda i: (l, 0, 0)), ANY_SPEC],
        out_specs=pl.BlockSpec((ts, D_CONV), lambda i: (i, D_ATT // D_CONV)),
        scratch_shapes=[pltpu.VMEM((CONV_HALO + ts, D_CONV), F32)],
        input_output_aliases={9: 0},
        compiler_params=_params('parallel'),
    )(proj_a, proj_a, proj_a, proj_a, dw_w, dw_b, ln_g, ln_b, pw_w, mix)


def _conv_bwd(proj_a, d_mix, d_proj, dw_w, dw_b, ln_g, ln_b, pw_w, l, *, name, ts=512):
    S = proj_a.shape[0]
    ts = _tile(S, ts, CONV_HALO)
    n = S // ts
    d_col = D_ATT // D_CONV

    def body(a_ref, b_ref, ah_ref, bh_ref, dy_ref, w_ref, wb_ref, g_ref, bb_ref, pw_ref, dp_in,
             o_ref, dw_ref, dwb_ref, dg_ref, dbb_ref, dpw_ref, buf, dcbuf):
        i = pl.program_id(0)
        _glu_into(buf, a_ref, b_ref, ah_ref, bh_ref, i == n - 1, ts)
        c = _dwconv(buf, w_ref, ts) + wb_ref[...]
        yhat, rstd = _layer_norm_stats(c)
        y = yhat * g_ref[...] + bb_ref[...]
        sg = _sigmoid(y)
        hs = y * sg
        dout = dy_ref[...].astype(BF16)
        d_hs = _dot(dout, pw_ref[...], 1, 1)
        d_y = d_hs * (sg * (1.0 + y * (1.0 - sg)))
        d_yhat = d_y * g_ref[...]
        d_c = rstd * (d_yhat - jnp.mean(d_yhat, axis=-1, keepdims=True)
                      - yhat * jnp.mean(d_yhat * yhat, axis=-1, keepdims=True))

        @pl.when(i == 0)
        def _():
            dcbuf[ts:ts + CONV_HALO, :] = jnp.zeros((CONV_HALO, D_CONV), F32)
            dw_ref[...] = jnp.zeros(dw_ref.shape, F32)
            dwb_ref[...] = jnp.zeros(dwb_ref.shape, F32)
            dg_ref[...] = jnp.zeros(dg_ref.shape, F32)
            dbb_ref[...] = jnp.zeros(dbb_ref.shape, F32)
            dpw_ref[...] = jnp.zeros(dpw_ref.shape, F32)

        dcbuf[0:ts, :] = d_c
        dpw_ref[...] += _dot(hs.astype(BF16), dout, 0, 0)
        dg_ref[...] += jnp.sum(d_y * yhat, axis=0, keepdims=True)
        dbb_ref[...] += jnp.sum(d_y, axis=0, keepdims=True)
        dwb_ref[...] += jnp.sum(d_c, axis=0, keepdims=True)
        off = CONV_HALO - (CONV_WIDTH - 1)
        d_h = jnp.zeros((ts, D_CONV), F32)
        for k in range(CONV_WIDTH):
            d_h = d_h + w_ref[k:k + 1, :] * dcbuf[pl.ds(CONV_WIDTH - 1 - k, ts), :]
            dw_ref[k:k + 1, :] += jnp.sum(d_c * buf[pl.ds(off + k, ts), :], axis=0, keepdims=True)
        dcbuf[ts:ts + CONV_HALO, :] = d_c[0:CONV_HALO, :]
        a, sb = a_ref[...], _sigmoid(b_ref[...])
        o_ref[:, 0:D_CONV] = (d_h * sb).astype(BF16)
        o_ref[:, D_CONV:2 * D_CONV] = (d_h * a * sb * (1.0 - sb)).astype(BF16)

    rev = lambda i: n - 1 - i
    cur, halo = _conv_specs(ts, rev)
    vec = pl.BlockSpec((1, D_CONV), lambda i: (0, 0))
    wspec = pl.BlockSpec((CONV_HALO, D_CONV), lambda i: (0, 0))
    sq = pl.BlockSpec((D_CONV, D_CONV), lambda i: (0, 0))
    return pl.pallas_call(
        body, name=name,
        out_shape=(jax.ShapeDtypeStruct(d_proj.shape, BF16), jax.ShapeDtypeStruct((CONV_HALO, D_CONV), F32),
                   jax.ShapeDtypeStruct((1, D_CONV), F32), jax.ShapeDtypeStruct((1, D_CONV), F32),
                   jax.ShapeDtypeStruct((1, D_CONV), F32), jax.ShapeDtypeStruct((D_CONV, D_CONV), F32)),
        grid=(n,),
        in_specs=[cur(A_COL), cur(B_COL), halo(A_COL), halo(B_COL),
                  pl.BlockSpec((ts, D_CONV), lambda i: (rev(i), d_col)),
                  pl.BlockSpec((None, CONV_HALO, D_CONV), lambda i: (l, 0, 0)), vec, vec, vec,
                  pl.BlockSpec((None, D_CONV, D_CONV), lambda i: (l, 0, 0)), ANY_SPEC],
        out_specs=(pl.BlockSpec((ts, 2 * D_CONV), lambda i: (rev(i), D_QKV // (2 * D_CONV))), wspec, vec, vec, vec, sq),
        scratch_shapes=[pltpu.VMEM((CONV_HALO + ts, D_CONV), F32), pltpu.VMEM((ts + CONV_HALO, D_CONV), F32)],
        input_output_aliases={10: 0},
        compiler_params=_params('arbitrary'),
    )(proj_a, proj_a, proj_a, proj_a, d_mix, dw_w, dw_b, ln_g, ln_b, pw_w, d_proj)


def _pool_window():
    lane = lax.broadcasted_iota(jnp.int32, (1, D_POOL), 1)
    w = jnp.full((1, D_POOL), POOL_WINDOWS[0], jnp.int32)
    for g in range(1, len(POOL_WINDOWS)):
        w = jnp.where(lane // POOL_GROUP == g, POOL_WINDOWS[g], w)
    return w


def _pool_diff(buf, u_ref, uh_ref, first, tile, ts):
    buf[0:POOL_HALO, :] = jnp.where(first, 0.0, uh_ref[...])
    u = u_ref[...]
    buf[POOL_HALO:POOL_HALO + ts, :] = u
    wl = _pool_window()
    acc = u
    for j in range(1, max(POOL_WINDOWS)):
        acc = acc + jnp.where(j < wl, buf[pl.ds(POOL_HALO - j, ts), :], 0.0)
    pos = tile * ts + lax.broadcasted_iota(jnp.int32, (ts, 1), 0)
    cnt = jnp.minimum(pos + 1, wl).astype(F32)
    return acc / cnt - u, cnt


def _pool_specs(ts, tmap):
    hb = ts // POOL_HALO
    cur = pl.BlockSpec((ts, D_POOL), lambda i: (tmap(i), P_COL))
    halo = pl.BlockSpec((POOL_HALO, D_POOL), lambda i: (jnp.maximum(tmap(i) * hb - 1, 0), P_COL))
    return cur, halo


def _pool_fwd(proj_a, mix, wbd, scale, *, name, ts=512):
    S = proj_a.shape[0]
    ts = _tile(S, ts, POOL_HALO)

    def body(u_ref, uh_ref, w_ref, s_ref, mix_in, o_ref, buf):
        i = pl.program_id(0)
        d, _ = _pool_diff(buf, u_ref, uh_ref, i == 0, i, ts)
        o_ref[...] = (_dot(d.astype(BF16), w_ref[...], 1, 0) * s_ref[...]).astype(BF16)

    cur, halo = _pool_specs(ts, lambda i: i)
    return pl.pallas_call(
        body, name=name, out_shape=jax.ShapeDtypeStruct(mix.shape, BF16), grid=(S // ts,),
        in_specs=[cur, halo, pl.BlockSpec((D_POOL, D_POOL), lambda i: (0, 0)), pl.BlockSpec((1, D_POOL), lambda i: (0, 0)),
                  ANY_SPEC],
        out_specs=pl.BlockSpec((ts, D_POOL), lambda i: (i, (D_ATT + D_CONV) // D_POOL)),
        scratch_shapes=[pltpu.VMEM((POOL_HALO + ts, D_POOL), F32)],
        input_output_aliases={4: 0},
        compiler_params=_params('parallel'),
    )(proj_a, proj_a, wbd, scale, mix)


def _pool_bwd(proj_a, d_mix, d_proj, wbd, scale, *, name, ts=512):
    S = proj_a.shape[0]
    ts = _tile(S, ts, POOL_HALO)
    n = S // ts
    d_col = (D_ATT + D_CONV) // D_POOL

    def body(u_ref, uh_ref, dy_ref, w_ref, s_ref, dp_in, o_ref, dw_ref, ds_ref, buf, ebuf):
        i = pl.program_id(0)
        tile = n - 1 - i
        d, cnt = _pool_diff(buf, u_ref, uh_ref, tile == 0, tile, ts)
        db = d.astype(BF16)
        ypre = _dot(db, w_ref[...], 1, 0)
        dout = dy_ref[...]
        d_y = (dout * s_ref[...]).astype(BF16)
        d_d = _dot(d_y, w_ref[...], 1, 1)

        @pl.when(i == 0)
        def _():
            ebuf[ts:ts + POOL_HALO, :] = jnp.zeros((POOL_HALO, D_POOL), F32)
            dw_ref[...] = jnp.zeros(dw_ref.shape, F32)
            ds_ref[...] = jnp.zeros(ds_ref.shape, F32)

        dw_ref[...] += _dot(db, d_y, 0, 0)
        ds_ref[...] += jnp.sum(dout * ypre, axis=0, keepdims=True)
        e = d_d / cnt
        ebuf[0:ts, :] = e
        wl = _pool_window()
        acc = e
        for j in range(1, max(POOL_WINDOWS)):
            acc = acc + jnp.where(j < wl, ebuf[pl.ds(j, ts), :], 0.0)
        ebuf[ts:ts + POOL_HALO, :] = e[0:POOL_HALO, :]
        o_ref[...] = (acc - d_d).astype(BF16)

    rev = lambda i: n - 1 - i
    cur, halo = _pool_specs(ts, rev)
    sq = pl.BlockSpec((D_POOL, D_POOL), lambda i: (0, 0))
    vec = pl.BlockSpec((1, D_POOL), lambda i: (0, 0))
    return pl.pallas_call(
        body, name=name,
        out_shape=(jax.ShapeDtypeStruct(d_proj.shape, BF16), jax.ShapeDtypeStruct((D_POOL, D_POOL), F32),
                   jax.ShapeDtypeStruct((1, D_POOL), F32)),
        grid=(n,),
        in_specs=[cur, halo, pl.BlockSpec((ts, D_POOL), lambda i: (rev(i), d_col)), sq, vec, ANY_SPEC],
        out_specs=(pl.BlockSpec((ts, D_POOL), lambda i: (rev(i), P_COL)), sq, vec),
        scratch_shapes=[pltpu.VMEM((POOL_HALO + ts, D_POOL), F32), pltpu.VMEM((ts + POOL_HALO, D_POOL), F32)],
        input_output_aliases={5: 0},
        compiler_params=_params('arbitrary'),
    )(proj_a, proj_a, d_mix, wbd, scale, d_proj)


FFN_LANES = 128
FFN_GROUP = 8 * 8


def _ffn_rows(ref, c, row0, j):
    return ref.at[c][pl.ds(row0 + j, 8, stride=8), :]


def _ffn_specs(ts, tc2, tmap, l):
    hb = ts // FFN_HALO
    cur = pl.BlockSpec((ts, tc2), lambda c, i: (tmap(i), c))
    halo = pl.BlockSpec((FFN_HALO, tc2), lambda c, i: (jnp.maximum(tmap(i) * hb - 1, 0), c))
    wspec = pl.BlockSpec((None, FFN_HALO, tc2), lambda c, i: (l, 0, c))
    return cur, halo, wspec


def _ffn_fill(buf, x_ref, xh_ref, first, ts, nblk):
    for c in range(nblk):
        cs = slice(c * FFN_LANES, (c + 1) * FFN_LANES)
        buf[c, 0:FFN_HALO, :] = jnp.where(first, 0.0, xh_ref[:, cs])
        buf[c, FFN_HALO:FFN_HALO + ts, :] = x_ref[:, cs]


def _ffn_conv_piece(buf, w_ref, r0, c):
    ws = [w_ref[k:k + 1, c * FFN_LANES:(c + 1) * FFN_LANES] for k in range(FFN_CONV_WIDTH)]
    xs = [_ffn_rows(buf, c, FFN_HALO + r0, j) for j in range(1 - FFN_CONV_WIDTH, 8)]
    outs = []
    for j in range(8):
        acc = ws[0] * xs[j]
        for k in range(1, FFN_CONV_WIDTH):
            acc = acc + ws[k] * xs[j + k]
        outs.append(acc)
    return outs, xs


def _ffn_act_fwd(up, w, l, *, name, ts=256):
    S, F2 = up.shape
    tc = F2 // 4
    nb = tc // FFN_LANES
    ts = _tile(S, ts, FFN_GROUP)

    def body(x_ref, xh_ref, w_ref, o_ref, buf, stage):
        _ffn_fill(buf, x_ref, xh_ref, pl.program_id(1) == 0, ts, 2 * nb)
        for c in range(nb):
            for r0 in range(0, ts, FFN_GROUP):
                gates, _ = _ffn_conv_piece(buf, w_ref, r0, c)
                vals, _ = _ffn_conv_piece(buf, w_ref, r0, nb + c)
                for j in range(8):
                    stage.at[c][pl.ds(r0 + j, 8, stride=8), :] = gates[j] * _sigmoid(gates[j]) * vals[j]
            o_ref[:, c * FFN_LANES:(c + 1) * FFN_LANES] = stage[c].astype(BF16)

    cur, halo, wspec = _ffn_specs(ts, 2 * tc, lambda i: i, l)
    return pl.pallas_call(
        body, name=name, out_shape=jax.ShapeDtypeStruct((S, F2 // 2), BF16), grid=(2, S // ts),
        in_specs=[cur, halo, wspec],
        out_specs=pl.BlockSpec((ts, tc), lambda c, i: (i, c)),
        scratch_shapes=[pltpu.VMEM((2 * nb, FFN_HALO + ts, FFN_LANES), F32), pltpu.VMEM((nb, ts, FFN_LANES), F32)],
        compiler_params=_params('parallel', 'parallel'),
    )(up, up, w)


def _ffn_act_bwd(up, d_act, w, l, *, name, ts=256):
    S, F2 = up.shape
    tc = F2 // 4
    nb = tc // FFN_LANES
    ts = _tile(S, ts, FFN_GROUP)
    n = S // ts

    def body(x_ref, xh_ref, da_ref, w_ref, o_ref, dw_ref, buf, dcbuf, stage):
        i = pl.program_id(1)
        _ffn_fill(buf, x_ref, xh_ref, i == n - 1, ts, 2 * nb)

        @pl.when(i == 0)
        def _():
            dcbuf[:, ts:ts + FFN_HALO, :] = jnp.zeros((2 * nb, FFN_HALO, FFN_LANES), F32)
            dw_ref[...] = jnp.zeros(dw_ref.shape, F32)

        for c in range(nb):
            blocks = (c, nb + c)
            stage[c, :, :] = da_ref[:, c * FFN_LANES:(c + 1) * FFN_LANES]
            dws = [[jnp.zeros((8, FFN_LANES), F32) for _ in range(FFN_CONV_WIDTH)] for _ in range(2)]
            for r0 in range(0, ts, FFN_GROUP):
                gates, xg = _ffn_conv_piece(buf, w_ref, r0, blocks[0])
                vals, xv = _ffn_conv_piece(buf, w_ref, r0, blocks[1])
                for j in range(8):
                    sg = _sigmoid(gates[j])
                    da = _ffn_rows(stage, c, r0, j)
                    d_cs = (da * vals[j] * (sg * (1.0 + gates[j] * (1.0 - sg))), da * (gates[j] * sg))
                    for half, (d_c, xs) in enumerate(zip(d_cs, (xg, xv))):
                        dcbuf.at[blocks[half]][pl.ds(r0 + j, 8, stride=8), :] = d_c
                        for k in range(FFN_CONV_WIDTH):
                            dws[half][k] = dws[half][k] + d_c * xs[j + k]
            for half in range(2):
                cs = slice(blocks[half] * FFN_LANES, (blocks[half] + 1) * FFN_LANES)
                for k in range(FFN_CONV_WIDTH):
                    dw_ref[k:k + 1, cs] += jnp.sum(dws[half][k], axis=0, keepdims=True)
            for b in blocks:
                cs = slice(b * FFN_LANES, (b + 1) * FFN_LANES)
                ws = [w_ref[k:k + 1, cs] for k in range(FFN_CONV_WIDTH)]
                for r0 in range(0, ts, FFN_GROUP):
                    ds = [_ffn_rows(dcbuf, b, r0, j) for j in range(8 + FFN_CONV_WIDTH - 1)]
                    for j in range(8):
                        d_x = ws[FFN_CONV_WIDTH - 1] * ds[j]
                        for k in range(FFN_CONV_WIDTH - 1):
                            d_x = d_x + ws[k] * ds[j + FFN_CONV_WIDTH - 1 - k]
                        stage.at[c][pl.ds(r0 + j, 8, stride=8), :] = d_x
                o_ref[:, cs] = stage[c].astype(BF16)
                dcbuf[b, ts:ts + FFN_HALO, :] = dcbuf[b, 0:FFN_HALO, :]

    rev = lambda i: n - 1 - i
    cur, halo, wspec = _ffn_specs(ts, 2 * tc, rev, l)
    return pl.pallas_call(
        body, name=name,
        out_shape=(jax.ShapeDtypeStruct((S, F2), BF16), jax.ShapeDtypeStruct((FFN_HALO, F2), F32)),
        grid=(2, n),
        in_specs=[cur, halo, pl.BlockSpec((ts, tc), lambda c, i: (rev(i), c)), wspec],
        out_specs=(cur, pl.BlockSpec((FFN_HALO, 2 * tc), lambda c, i: (0, c))),
        scratch_shapes=[pltpu.VMEM((2 * nb, FFN_HALO + ts, FFN_LANES), F32), pltpu.VMEM((2 * nb, ts + FFN_HALO, FFN_LANES), F32),
                        pltpu.VMEM((nb, ts, FFN_LANES), F32)],
        compiler_params=_params('parallel', 'arbitrary'),
    )(up, up, d_act, w)


def _loss_head(y, target, *, name, ts=512):
    S, D = y.shape
    ts = _tile(S, ts, 8)

    def body(y_ref, t_ref, l_ref, dy_ref):
        i = pl.program_id(0)
        err = y_ref[...] - t_ref[...]
        dy_ref[...] = err * (1.0 / D)
        part = jnp.sum(jnp.sum(err * err, axis=1, keepdims=True), axis=0, keepdims=True) * (0.5 / D)

        @pl.when(i == 0)
        def _():
            l_ref[...] = part

        @pl.when(i > 0)
        def _():
            l_ref[...] += part

    row = pl.BlockSpec((ts, D), lambda i: (i, 0))
    return pl.pallas_call(
        body, name=name,
        out_shape=(jax.ShapeDtypeStruct((1, 1), F32), jax.ShapeDtypeStruct((S, D), F32)),
        grid=(S // ts,), in_specs=[row, row], out_specs=(pl.BlockSpec((1, 1), lambda i: (0, 0)), row),
        compiler_params=_params('arbitrary'),
    )(y, target)


def _pair_cols(w):
    lead, f2 = w.shape[:-1], w.shape[-1]
    return w.reshape(lead + (2, 2, f2 // 4)).swapaxes(-3, -2).reshape(lead + (f2,))


def _pad_axis(w, size, axis):
    pad = [(0, 0)] * w.ndim
    pad[axis] = (0, size - w.shape[axis])
    return jnp.pad(w, pad)


def _block_diag(pool_w):
    g = pool_w.shape[0]
    rows = [jnp.concatenate([pool_w[i] if i == j else jnp.zeros_like(pool_w[i]) for j in range(g)], axis=1) for i in range(g)]
    return jnp.concatenate(rows, axis=0)


def _small_weights(w, l):
    return dict(
        norm1_g=w['norm1_g'][l][None, :],
        b_col=_pad_axis(w['b_f'][l][:, None], FG_ROWS, 0),
        qg=jnp.tile(w['q_norm_g'][l], N_HEADS)[None, :],
        kg=jnp.tile(w['k_norm_g'][l], N_HEADS)[None, :],
        dw_b=w['conv_dw_b'][l][None, :], ln_g=w['conv_ln_g'][l][None, :], ln_b=w['conv_ln_b'][l][None, :],
        wbd=_block_diag(w['pool_w'][l]).astype(BF16),
        pool_scale=w['pool_scale'][l][None, :],
        norm2_g=w['norm2_g'][l][None, :],
    )


def _layer_fwd(x, W, p, l):
    n = lambda s: f'l{l}_{s}'
    S = x.shape[0]
    h = _rms_fwd(x, p['norm1_g'], name=n('norm1'))
    proj_a = _mm(h, W.get('w_a', h), b_lead=0, name=n('proj_a'), tn=768)
    z_raw = _mm(W.get('w_fg_t', h), h, a_lead=0, tb=True, name=n('proj_fg'))
    qkv = _qk_prep_fwd(proj_a, p['qg'], p['kg'], name=n('qk_norm'))
    f_cum = _forget_fwd(z_raw, p['b_col'], name=n('forget'))
    f3 = f_cum[:N_HEADS].reshape(N_HEADS // 2, 2, S)
    mix, att, lse = _attn_fwd(qkv, f3, name=n('attn'))
    mix = _conv_fwd(proj_a, mix, W.get('dw_w', h), p['dw_b'], p['ln_g'], p['ln_b'], W.get('pw_w', h), 0, name=n('conv'))
    mix = _pool_fwd(proj_a, mix, p['wbd'], p['pool_scale'], name=n('pool'))
    x1 = _mm(mix, W.get('w_out', mix), b_lead=0, res=x, name=n('out_proj'), tn=1024)
    h2 = _rms_fwd(x1, p['norm2_g'], name=n('norm2'))
    up = _mm(h2, W.get('w_up', mix), b_lead=0, name=n('up_proj'), tn=1408)
    act = _ffn_act_fwd(up, W.get('ffn_w', h), 0, name=n('ffn_act'))
    x2 = _mm(act, W.get('w_down', mix), b_lead=0, res=x1, name=n('down_proj'), tn=1024, tk=1408)
    saved = dict(x=x, h=h, proj_a=proj_a, z_raw=z_raw, qkv=qkv, f3=f3, att=att, lse=lse, mix=mix, x1=x1, h2=h2, up=up, act=act)
    return x2, saved


def _layer_bwd(dx2, W, p, s, l, depth, G):
    n = lambda t: f'l{l}_{t}'
    S = dx2.shape[0]
    g = {}
    W = W.ready

    def stacked(key, a, b, **kw):
        G[key] = _mm(a, b, ta=True, out_lead=l, out_depth=depth, out_buf=G.get(key), name=n('d_' + key), **kw)

    d_act = _mm(dx2, W['w_down'], b_lead=0, tb=True, name=n('d_act'), tn=1408)
    stacked('w_down', s['act'], dx2, tm=1408, tn=1024)
    d_up, d_ffn_w = _ffn_act_bwd(s['up'], d_act, W['ffn_w'], 0, name=n('ffn_act_bwd'))
    g['ffn_dw_w'] = _pair_cols(d_ffn_w[:FFN_CONV_WIDTH])
    d_h2 = _mm(d_up, W['w_up'], b_lead=0, tb=True, name=n('d_h2'), tn=1024, tk=1408)
    stacked('w_up', s['h2'], d_up, tm=1024, tn=1408)
    dx1, dg2 = _rms_bwd(s['x1'], p['norm2_g'], d_h2, dx2, name=n('norm2_bwd'))
    g['norm2_g'] = dg2[0]
    d_mix = _mm(dx1, W['w_out'], b_lead=0, tb=True, name=n('d_mix'), tn=1024)
    stacked('w_out', s['mix'], dx1, tm=1024, tn=1024)
    dq, dk, dv, df3, dr = _attn_bwd(s['qkv'], s['f3'], s['att'], s['lse'], d_mix, name=n('attn_bwd'))
    df = _pad_axis(df3.reshape(N_HEADS, S) + dr[:, ::HEAD_DIM].T, FG_ROWS, 0)
    d_z, d_b = _forget_bwd(s['z_raw'], p['b_col'], df, name=n('forget_bwd'))
    g['b_f'] = d_b[:N_HEADS, 0]
    d_proj, d_qg, d_kg = _qk_prep_bwd(s['proj_a'], dq, dk, dv, p['qg'], p['kg'], name=n('qk_norm_bwd'))
    g['q_norm_g'] = d_qg.reshape(N_HEADS, HEAD_DIM).sum(axis=0)
    g['k_norm_g'] = d_kg.reshape(N_HEADS, HEAD_DIM).sum(axis=0)
    d_proj, d_dw_w, d_dw_b, d_ln_g, d_ln_b, d_pw = _conv_bwd(
        s['proj_a'], d_mix, d_proj, W['dw_w'], p['dw_b'], p['ln_g'], p['ln_b'], W['pw_w'], 0, name=n('conv_bwd'))
    g['conv_dw_w'], g['conv_dw_b'] = d_dw_w[:CONV_WIDTH], d_dw_b[0]
    g['conv_ln_g'], g['conv_ln_b'], g['conv_pw_w'] = d_ln_g[0], d_ln_b[0], d_pw
    d_proj, d_wbd, d_scale = _pool_bwd(s['proj_a'], d_mix, d_proj, p['wbd'], p['pool_scale'], name=n('pool_bwd'))
    g['pool_w'] = jnp.stack([d_wbd[i * POOL_GROUP:(i + 1) * POOL_GROUP, i * POOL_GROUP:(i + 1) * POOL_GROUP]
                             for i in range(len(POOL_WINDOWS))])
    g['pool_scale'] = d_scale[0]
    d_h_fg = _mm(d_z, W['w_fg_t'], b_lead=0, ta=True, name=n('d_h_fg'), tn=1024)
    d_h = _mm(d_proj, W['w_a'], b_lead=0, tb=True, res=d_h_fg, name=n('d_h'), tn=1024, tk=768)
    stacked('w_a', s['h'], d_proj, tm=1024, tn=768)
    g['w_fg'] = _mm(d_z, s['h'], name=n('d_w_fg'), tn=1024).T
    dx, dg1 = _rms_bwd(s['x'], p['norm1_g'], d_h, dx1, name=n('norm1_bwd'))
    g['norm1_g'] = dg1[0]
    return dx, g


SMALL_GRADS = REPLICATED + ('conv_dw_w', 'conv_pw_w', 'ffn_dw_w')


def _local_step(x, target, W, w_small):
    depth = w_small['norm1_g'].shape[0]
    ps, saved = [], []
    for l in range(depth):
        p = _small_weights(w_small, l)
        x, s = _layer_fwd(x, W[l], p, l)
        ps.append(p)
        saved.append(s)
    loss, dx = _loss_head(x, target, name='loss_head')
    small = [None] * depth
    G = {}
    for l in reversed(range(depth)):
        dx, small[l] = _layer_bwd(dx, W[l], ps[l], saved[l], l, depth, G)
    G['w_fg'] = jnp.stack([small[l]['w_fg'] for l in range(depth)])
    return loss, dx, G, {k: jnp.stack([small[l][k] for l in range(depth)]) for k in SMALL_GRADS}


W_IN_SHARD = D_IN // N_CHIPS
W_IN_PAD = 640
N_A_TILES = D_PROJ_A // LANES
FG_COL0 = D_QKV


def _a_tile_base(j):
    if j == N_A_TILES:
        return FG_COL0, N_HEADS
    return (j * LANES if j * LANES < FG_COL0 else j * LANES + N_HEADS), LANES


def _shift_select(rows, cols, shift, row_max, col_max):
    r = lax.broadcasted_iota(jnp.int32, (rows, cols), 0)
    c = lax.broadcasted_iota(jnp.int32, (rows, cols), 1)
    return ((r + shift == c) & (r < row_max) & (c < col_max)).astype(BF16)


def _select_w_in(raw, *, name, tm=256):
    _, D, _ = raw.shape
    tm = _tile(D, tm, 16)
    plan = []
    for j in range(N_A_TILES + 1):
        base, cmax = _a_tile_base(j)
        parts = []
        for p in range(N_CHIPS):
            delta = base - W_IN_SHARD * p
            lo, hi = max(0, delta), min(W_IN_SHARD - 1, delta + cmax - 1)
            if lo > hi:
                continue
            a0 = (lo // LANES) * LANES
            kw = min(-(-(hi + 1 - a0) // LANES) * LANES, W_IN_PAD - a0)
            parts.append((p, a0, kw, delta))
        plan.append((cmax, parts))

    def body(raw_ref, wa_ref, fg_ref):
        for j, (cmax, parts) in enumerate(plan):
            acc = None
            for p, a0, kw, delta in parts:
                sel = _shift_select(kw, LANES, a0 - delta, W_IN_SHARD - a0, cmax)
                t = _dot(raw_ref[p, :, a0:a0 + kw], sel, 1, 0)
                acc = t if acc is None else acc + t
            if j == N_A_TILES:
                fg_ref[...] = acc.astype(BF16)
            else:
                wa_ref[:, j * LANES:(j + 1) * LANES] = acc.astype(BF16)

    return pl.pallas_call(
        body, name=name,
        out_shape=(jax.ShapeDtypeStruct((D, D_PROJ_A), BF16), jax.ShapeDtypeStruct((D, LANES), BF16)),
        grid=(D // tm,),
        in_specs=[pl.BlockSpec((N_CHIPS, tm, W_IN_PAD), lambda i: (0, i, 0))],
        out_specs=(pl.BlockSpec((tm, D_PROJ_A), lambda i: (i, 0)), pl.BlockSpec((tm, LANES), lambda i: (i, 0))),
        compiler_params=_params('parallel'),
    )(raw)


def _select_w_in_grads(p_a, p_fg, *, tm=256):
    D = p_a.shape[0]
    tm = _tile(D, tm, 16)
    n_local = W_IN_PAD // LANES
    plan = []
    for p in range(N_CHIPS):
        for i in range(n_local):
            cmax = max(0, min(LANES, W_IN_SHARD - i * LANES))
            parts = []
            for j in range(N_A_TILES + 1):
                base, rmax = _a_tile_base(j)
                e = base - W_IN_SHARD * p - i * LANES
                if e + rmax - 1 < 0 or e > cmax - 1:
                    continue
                parts.append((j, e, rmax))
            plan.append((p, i, cmax, parts))

    def body(a_ref, fg_ref, o32_ref, o16_ref):
        terms = {}

        def src(j):
            if j not in terms:
                v = fg_ref[...] if j == N_A_TILES else a_ref[:, j * LANES:(j + 1) * LANES]
                terms[j] = _split3(v)
            return terms[j]

        for p, i, cmax, parts in plan:
            acc = jnp.zeros((tm, LANES), F32)
            for j, e, rmax in parts:
                sel = _shift_select(LANES, LANES, e, rmax, cmax)
                for term in src(j):
                    acc = acc + _dot(term, sel, 1, 0)
            o32_ref[p, :, i * LANES:(i + 1) * LANES] = acc
            o16_ref[p, :, i * LANES:(i + 1) * LANES] = acc.astype(BF16)

    out = pl.BlockSpec((N_CHIPS, tm, W_IN_PAD), lambda i: (0, i, 0))
    return pl.pallas_call(
        body, name='select_w_in_grads',
        out_shape=(jax.ShapeDtypeStruct((N_CHIPS, D, W_IN_PAD), F32), jax.ShapeDtypeStruct((N_CHIPS, D, W_IN_PAD), BF16)),
        grid=(D // tm,),
        in_specs=[pl.BlockSpec((tm, D_PROJ_A), lambda i: (i, 0)), pl.BlockSpec((tm, LANES), lambda i: (i, 0))],
        out_specs=(out, out),
        compiler_params=_params('parallel'),
    )(p_a, p_fg)


MESH = pl.DeviceIdType.MESH
HBM_SPEC = pl.BlockSpec(memory_space=pltpu.HBM)


def _place():
    return lax.axis_index('x'), lax.axis_index('y'), lax.axis_index('c')


def _other_chips(x, y):
    return [(1 - x, y), (x, 1 - y), (1 - x, 1 - y)]


def _up_pos(q):
    return (q % 2) * 2 + q // 2


CHUNKS = {
    'w_in': ('lead', None),
    'w_up': ('cols', None),
    'w_down': ('rows', None),
    'w_out': ('rows', None),
    'conv_pw_w': ('rows', None),
    'conv_dw_w': ('lead', None),
    'ffn_dw_w': ('lead', None),
}


def _window(ref, kind, l, q):
    at = (lambda *idx: ref.at[idx]) if l is None else (lambda *idx: ref.at[(l,) + idx])
    shape = ref.shape if l is None else ref.shape[1:]
    if kind == 'lead':
        return at(q)
    if kind == 'rows':
        cs = shape[0] // N_CHIPS
        return at(pl.ds(pl.multiple_of(q * cs, 16), cs), slice(None))
    cs = shape[1] // N_CHIPS
    return at(slice(None), pl.ds(pl.multiple_of(_up_pos(q) * cs, LANES), cs))


def _place_shard(src, l, pos_arr, full_shape, kind, *, name, tm=256):
    _, m, n = src.shape
    bm = _tile(m, tm, 16) if kind != 'rows' else m

    def body(pos_ref, s_ref, o_ref):
        o_ref[...] = s_ref[...].astype(BF16)

    if kind == 'lead':
        out = pl.BlockSpec((None, bm, n), lambda i, pos: (pos[0], i, 0))
    elif kind == 'rows':
        out = pl.BlockSpec((bm, n), lambda i, pos: (pos[0], 0))
    else:
        out = pl.BlockSpec((bm, n), lambda i, pos: (i, pos[0]))
    return pl.pallas_call(
        body, name=name, out_shape=jax.ShapeDtypeStruct(full_shape, BF16),
        grid_spec=pltpu.PrefetchScalarGridSpec(
            num_scalar_prefetch=1, grid=(m // bm,),
            in_specs=[pl.BlockSpec((None, bm, n), lambda i, pos: (l, i, 0))], out_specs=out),
        compiler_params=_params('parallel'),
    )(pos_arr, src)


GATHERED = ('w_in', 'w_up', 'w_down', 'w_out', 'conv_pw_w', 'conv_dw_w', 'ffn_dw_w')
GATHER_GROUPS = ((0, ('w_in', 'conv_dw_w', 'ffn_dw_w', 'conv_pw_w')), (0, ('w_out', 'w_up', 'w_down')), (1, GATHERED))
SEM_SPEC = pl.BlockSpec(memory_space=pltpu.SEMAPHORE)
SPLIT_COPY_PARAMS = pltpu.CompilerParams(has_side_effects=pltpu.SideEffectType.DATAFLOW_SIDE_EFFECTING)


def _gather_start(bufs):
    flat = [b for group in bufs for b in group]
    nb = len(flat)

    def body(*refs):
        outs, sems = refs[nb:2 * nb], refs[2 * nb:]
        x, y, c = _place()
        pos = 0
        for g, (_, keys) in enumerate(GATHER_GROUPS):
            for i, k in enumerate(keys):
                w = _window(outs[pos], CHUNKS[k][0], None, 2 * x + y)
                pos += 1
                for j, chip in enumerate(_other_chips(x, y)):
                    pltpu.make_async_remote_copy(src_ref=w, dst_ref=w, send_sem=sems[2 * g].at[3 * i + j],
                                                 recv_sem=sems[2 * g + 1].at[3 * i + j], device_id=(*chip, c),
                                                 device_id_type=MESH).start()

    sem_shapes = [pltpu.SemaphoreType.DMA((3 * len(keys),)) for _, keys in GATHER_GROUPS for _ in range(2)]
    res = pl.pallas_call(
        body, name='gather_start',
        out_shape=tuple(jax.ShapeDtypeStruct(b.shape, b.dtype) for b in flat) + tuple(sem_shapes),
        in_specs=[HBM_SPEC] * nb, out_specs=tuple([HBM_SPEC] * nb + [SEM_SPEC] * len(sem_shapes)),
        input_output_aliases={b: b for b in range(nb)},
        compiler_params=SPLIT_COPY_PARAMS,
    )(*[pltpu.with_memory_space_constraint(b, pltpu.HBM) for b in flat])
    out_bufs, sems, pos = [], res[nb:], 0
    for group in bufs:
        out_bufs.append(list(res[pos:pos + len(group)]))
        pos += len(group)
    return out_bufs, [(sems[2 * g], sems[2 * g + 1]) for g in range(len(GATHER_GROUPS))]


def _gather_wait(g, bufs, sems, after):
    keys = GATHER_GROUPS[g][1]
    nb = len(bufs)

    def body(*refs):
        send_sems, recv_sems = refs[nb], refs[nb + 1]
        outs = refs[nb + 3:]
        x, y, c = _place()
        for i, k in enumerate(keys):
            mine = _window(outs[i], CHUNKS[k][0], None, 2 * x + y)
            for j, (cx, cy) in enumerate(_other_chips(x, y)):
                theirs = _window(outs[i], CHUNKS[k][0], None, 2 * cx + cy)
                cp = pltpu.make_async_remote_copy(src_ref=mine, dst_ref=theirs, send_sem=send_sems.at[3 * i + j],
                                                  recv_sem=recv_sems.at[3 * i + j], device_id=(cx, cy, c), device_id_type=MESH)
                cp.wait_send()
                cp.wait_recv()

    return pl.pallas_call(
        body, name=f'gather_wait_{g}',
        out_shape=tuple(jax.ShapeDtypeStruct(b.shape, b.dtype) for b in bufs),
        in_specs=[HBM_SPEC] * nb + [SEM_SPEC, SEM_SPEC, ANY_SPEC], out_specs=tuple([HBM_SPEC] * nb),
        input_output_aliases={b: b for b in range(nb)},
        compiler_params=SPLIT_COPY_PARAMS,
    )(*bufs, *sems, after)


def _elementwise(body, ins, prefetch, in_maps, out_shapes, out_maps, block, grid, *, name):
    def spec(shape, imap):
        lead = len(shape) - 2
        return pl.BlockSpec((None,) * lead + block, imap)
    return pl.pallas_call(
        body, name=name,
        out_shape=tuple(out_shapes),
        grid_spec=pltpu.PrefetchScalarGridSpec(
            num_scalar_prefetch=len(prefetch), grid=grid,
            in_specs=[spec(a.shape, m) for a, m in zip(ins, in_maps)],
            out_specs=tuple(spec(s.shape, m) for s, m in zip(out_shapes, out_maps))),
        compiler_params=_params(*(['parallel'] * len(grid))),
    )(*prefetch, *ins)


def _rs_block(M, N):
    return (_tile(M, 256, 16), _tile(N, 2048))


def _rs_cast_other_layer(g, c_arr, *, name):
    _, M, N = g.shape
    bm, bn = _rs_block(M, N)

    def body(c_ref, g_ref, o_ref):
        o_ref[...] = g_ref[...].astype(BF16)

    return _elementwise(body, [g], [c_arr], [lambda i, j, c: (1 - c[0], i, j)],
                        [jax.ShapeDtypeStruct((M, N), BF16)], [lambda i, j, c: (i, j)], (bm, bn), (M // bm, N // bn), name=name)[0]


def _rs_add_sibling(g, ra, c_arr, *, name, want_bf16):
    _, M, N = g.shape
    bm, bn = _rs_block(M, N)

    def body(c_ref, g_ref, r_ref, p_ref, *pb_ref):
        p = g_ref[...] + r_ref[...].astype(F32)
        p_ref[...] = p
        if want_bf16:
            pb_ref[0][...] = p.astype(BF16)

    flat = lambda i, j, c: (i, j)
    outs = [jax.ShapeDtypeStruct((M, N), F32)] + ([jax.ShapeDtypeStruct((M, N), BF16)] if want_bf16 else [])
    return _elementwise(body, [g, ra], [c_arr], [lambda i, j, c: (c[0], i, j), flat], outs, [flat] * len(outs),
                        (bm, bn), (M // bm, N // bn), name=name)


def _rs_finish(p, rb, kind, chip_arr, c_arr, *, name):
    m, n = rb.shape[1:]
    bm, bn = _rs_block(m, n)
    nbm, nbn = m // bm, n // bn

    def body(q_ref, c_ref, p_ref, r_ref, o_ref):
        acc = p_ref[...]
        for j in range(N_CHIPS - 1):
            acc = acc + r_ref[j].astype(F32)
        o_ref[...] = acc

    if kind == 'lead':
        p_map = lambda i, j, q, c: (q[0], i, j)
    elif kind == 'rows':
        p_map = lambda i, j, q, c: (q[0] * nbm + i, j)
    else:
        p_map = lambda i, j, q, c: (i, q[0] * nbn + j)
    r_spec = pl.BlockSpec((N_CHIPS - 1, bm, bn), lambda i, j, q, c: (0, i, j))
    p_spec = pl.BlockSpec(((None,) if kind == 'lead' else ()) + (bm, bn), p_map)
    return pl.pallas_call(
        body, name=name, out_shape=jax.ShapeDtypeStruct((2, m, n), F32),
        grid_spec=pltpu.PrefetchScalarGridSpec(
            num_scalar_prefetch=2, grid=(nbm, nbn), in_specs=[p_spec, r_spec],
            out_specs=pl.BlockSpec((None, bm, bn), lambda i, j, q, c: (c[0], i, j))),
        compiler_params=_params('parallel', 'parallel'),
    )(chip_arr, c_arr, p, rb)


def _swap_with_sibling(bufs):
    nb = len(bufs)

    def body(*refs):
        ins, outs = refs[:nb], refs[nb:2 * nb]
        send_sems, recv_sems = refs[2 * nb:]
        x, y, c = _place()
        copies = [pltpu.make_async_remote_copy(src_ref=ins[b], dst_ref=outs[b], send_sem=send_sems.at[b],
                                               recv_sem=recv_sems.at[b], device_id=(x, y, 1 - c), device_id_type=MESH)
                  for b in range(nb)]
        for cp in copies:
            cp.start()
        for cp in copies:
            cp.wait()

    return pl.pallas_call(
        body, name='rs_swap_layers', out_shape=tuple(jax.ShapeDtypeStruct(b.shape, b.dtype) for b in bufs),
        in_specs=[HBM_SPEC] * nb, out_specs=tuple([HBM_SPEC] * nb),
        scratch_shapes=[pltpu.SemaphoreType.DMA((nb,)), pltpu.SemaphoreType.DMA((nb,))],
    )(*bufs)


def _send_chip_partials(bufs, kinds):
    nb = len(bufs)

    def chunk_shape(b, kind):
        if kind == 'lead':
            return b.shape[1:]
        if kind == 'rows':
            return (b.shape[0] // N_CHIPS, b.shape[1])
        return (b.shape[0], b.shape[1] // N_CHIPS)

    def body(*refs):
        ins, outs = refs[:nb], refs[nb:2 * nb]
        send_sems, recv_sems = refs[2 * nb:]
        x, y, c = _place()
        copies = []
        for b in range(nb):
            for j, (cx, cy) in enumerate(_other_chips(x, y)):
                k = 3 * b + j
                copies.append(pltpu.make_async_remote_copy(
                    src_ref=_window(ins[b], kinds[b], None, 2 * cx + cy), dst_ref=outs[b].at[j],
                    send_sem=send_sems.at[k], recv_sem=recv_sems.at[k], device_id=(cx, cy, c), device_id_type=MESH))
        for cp in copies:
            cp.start()
        for cp in copies:
            cp.wait()

    return pl.pallas_call(
        body, name='rs_send_partials',
        out_shape=tuple(jax.ShapeDtypeStruct((N_CHIPS - 1,) + chunk_shape(b, k), b.dtype) for b, k in zip(bufs, kinds)),
        in_specs=[HBM_SPEC] * nb, out_specs=tuple([HBM_SPEC] * nb),
        scratch_shapes=[pltpu.SemaphoreType.DMA((3 * nb,)), pltpu.SemaphoreType.DMA((3 * nb,))],
    )(*bufs)


def _share_layers(bufs):
    nb = len(bufs)

    def body(*refs):
        outs = refs[nb:2 * nb]
        send_sems, recv_sems = refs[2 * nb:]
        x, y, c = _place()
        copies = [pltpu.make_async_remote_copy(src_ref=outs[b].at[c], dst_ref=outs[b].at[c], send_sem=send_sems.at[b],
                                               recv_sem=recv_sems.at[b], device_id=(x, y, 1 - c), device_id_type=MESH)
                  for b in range(nb)]
        for cp in copies:
            cp.start()
        for b in range(nb):
            copies[b].wait_send()
            pltpu.make_async_remote_copy(src_ref=outs[b].at[1 - c], dst_ref=outs[b].at[1 - c], send_sem=send_sems.at[b],
                                         recv_sem=recv_sems.at[b], device_id=(x, y, 1 - c), device_id_type=MESH).wait_recv()

    return pl.pallas_call(
        body, name='rs_share_layers', out_shape=tuple(jax.ShapeDtypeStruct(b.shape, b.dtype) for b in bufs),
        in_specs=[HBM_SPEC] * nb, out_specs=tuple([HBM_SPEC] * nb),
        scratch_shapes=[pltpu.SemaphoreType.DMA((nb,)), pltpu.SemaphoreType.DMA((nb,))],
        input_output_aliases={b: b for b in range(nb)},
    )(*bufs)


def _all_reduce_small(v):
    r = v.shape[0]

    def body(x_ref, tot_ref, all_ref, send_sems, recv_sems):
        x, y, c = _place()
        me, sibling = (x, y, c), (x, y, 1 - c)
        chips = _other_chips(x, y)

        def rows(px, py, pc):
            return all_ref.at[pl.ds((4 * px + 2 * py + pc) * r, r), :]

        def copy(k, block, to, src=None):
            return pltpu.make_async_remote_copy(
                src_ref=rows(*block) if src is None else src, dst_ref=rows(*block),
                send_sem=send_sems.at[k], recv_sem=recv_sems.at[k], device_id=to, device_id_type=MESH)

        rows(*me)[...] = x_ref[...]
        first = [copy(0, me, sibling, src=x_ref)]
        first += [copy(1 + j, me, (*chip, c), src=x_ref) for j, chip in enumerate(chips)]
        for cp in first:
            cp.start()
        passed = [copy(4 + j, (*chip, c), sibling) for j, chip in enumerate(chips)]
        for j, chip in enumerate(chips):
            copy(1 + j, (*chip, c), me).wait_recv()
            passed[j].start()
        copy(0, sibling, me).wait_recv()
        for j, chip in enumerate(chips):
            copy(4 + j, (*chip, 1 - c), me).wait_recv()
        for cp in first + passed:
            cp.wait_send()
        acc = all_ref[0:r, :]
        for d in range(1, N_DEV):
            acc = acc + all_ref[d * r:(d + 1) * r, :]
        tot_ref[...] = acc

    return pl.pallas_call(
        body, name='all_reduce_small', out_shape=jax.ShapeDtypeStruct((r, LANES), F32),
        in_specs=[pl.BlockSpec(memory_space=pltpu.VMEM)], out_specs=pl.BlockSpec(memory_space=pltpu.VMEM),
        scratch_shapes=[pltpu.VMEM((N_DEV * r, LANES), F32), pltpu.SemaphoreType.DMA((7,)), pltpu.SemaphoreType.DMA((7,))],
    )(v)


def _adamw(w, g, m, v, *, name, ts=256):
    R, C = w.shape
    Cg = g.shape[1]
    ts = _tile(R, ts, 8)
    c1 = 1.0 - ADAM_B1 ** ADAM_STEP
    c2 = 1.0 - ADAM_B2 ** ADAM_STEP

    def body(w_ref, g_ref, m_ref, v_ref, go_ref, d_ref, nm_ref, nv_ref):
        gv = g_ref[:, 0:C]
        nm = ADAM_B1 * m_ref[...] + (1.0 - ADAM_B1) * gv
        nv = ADAM_B2 * v_ref[...] + (1.0 - ADAM_B2) * (gv * gv)
        d_ref[...] = -ADAM_LR * ((nm / c1) / (jnp.sqrt(nv / c2) + ADAM_EPS) + ADAM_WD * w_ref[...])
        go_ref[...] = gv
        nm_ref[...] = nm
        nv_ref[...] = nv

    blk = pl.BlockSpec((ts, C), lambda i: (i, 0))
    shape = jax.ShapeDtypeStruct((R, C), F32)
    return pl.pallas_call(body, name=name, out_shape=(shape, shape, shape, shape), grid=(R // ts,),
                          in_specs=[blk, pl.BlockSpec((ts, Cg), lambda i: (i, 0)), blk, blk], out_specs=(blk, blk, blk, blk),
                          compiler_params=_params('parallel'))(w, g, m, v)


def _pack_rows(parts, row_unit):
    flat = jnp.concatenate(parts)
    flat = _pad_axis(flat, -(-flat.shape[0] // (row_unit * LANES)) * row_unit * LANES, 0)
    return flat.reshape(-1, LANES)


def _as_2d(a):
    return a.reshape(-1, a.shape[-1])


def _mesh_place():
    cx, cy, cc = _place()
    chip = 2 * cx + cy
    as_arr = lambda v: jnp.reshape(v, (1,)).astype(jnp.int32)
    return chip, as_arr(cc), as_arr(chip), as_arr(_up_pos(chip))


class _LayerWeights:
    def __init__(self, groups):
        self.groups = groups
        self.ready = {}

    def get(self, name, after):
        if name not in self.ready:
            for names, wait in self.groups:
                if name in names:
                    self.ready.update({k: v[None] for k, v in wait(after).items()})
        return self.ready[name]


def _gather_full(w, place):
    chip, _, chip_arr, up_pos_arr = place
    L, D = w['w_in'].shape[:2]
    w_in_pad = _pad_axis(w['w_in'], W_IN_PAD, 2)

    def placed(k, l):
        if k == 'w_in':
            return _place_shard(w_in_pad, l, chip_arr, (N_CHIPS, D, W_IN_PAD), 'lead', name=f'place_w_in_{l}')
        if k == 'w_up':
            return _place_shard(w[k], l, up_pos_arr, (w[k].shape[1], N_CHIPS * w[k].shape[2]), 'cols', name=f'place_w_up_{l}')
        if k in ('conv_dw_w', 'ffn_dw_w'):
            return lax.dynamic_update_slice_in_dim(jnp.zeros((N_CHIPS,) + w[k].shape[1:], F32), w[k][l][None], chip, axis=0)
        return _place_shard(w[k], l, chip_arr, (N_CHIPS * w[k].shape[1], w[k].shape[2]), 'rows', name=f'place_{k}_{l}')

    bufs, sems = _gather_start([[placed(k, l) for k in keys] for l, keys in GATHER_GROUPS])
    unchunk = lambda a: jnp.moveaxis(a, 0, 1).reshape(a.shape[1], -1)

    def waiter(g):
        l, keys = GATHER_GROUPS[g]

        def wait(after):
            full = dict(zip(keys, _gather_wait(g, bufs[g], sems[g], after)))
            out = {}
            if 'w_in' in full:
                out['w_a'], w_fg = _select_w_in(full['w_in'], name=f'select_w_in_{l}')
                out['w_fg_t'] = w_fg.T
            if 'conv_dw_w' in full:
                out['dw_w'] = _pad_axis(unchunk(full['conv_dw_w']), CONV_HALO, 0)
            if 'ffn_dw_w' in full:
                out['ffn_w'] = _pad_axis(_pair_cols(unchunk(full['ffn_dw_w'])), FFN_HALO, 0)
            if 'conv_pw_w' in full:
                out['pw_w'] = full['conv_pw_w']
            out.update({k: full[k] for k in ('w_up', 'w_down', 'w_out') if k in full})
            return out

        names = {'w_in': ('w_a', 'w_fg_t'), 'conv_dw_w': ('dw_w',), 'ffn_dw_w': ('ffn_w',), 'conv_pw_w': ('pw_w',)}
        return tuple(n for k in keys for n in names.get(k, (k,))), wait

    return [_LayerWeights([waiter(g) for g in range(len(GATHER_GROUPS)) if GATHER_GROUPS[g][0] == l]) for l in range(L)]


RS_WIRE = ('w_in', 'w_up', 'w_down', 'w_out')


def _reduce_scatter(G, place):
    _, c_arr, chip_arr, up_pos_arr = place
    big = ('w_a', 'w_fg', 'w_up', 'w_down', 'w_out')
    recv = _swap_with_sibling([_rs_cast_other_layer(G[k], c_arr, name='rs_cast_' + k) for k in big])
    part, part_b = {}, {}
    for k, ra in zip(big, recv):
        res = _rs_add_sibling(G[k], ra, c_arr, name='rs_add_' + k, want_bf16=k not in ('w_a', 'w_fg'))
        part[k] = res[0]
        if len(res) > 1:
            part_b[k] = res[1]
    part['w_in'], part_b['w_in'] = _select_w_in_grads(part['w_a'], part['w_fg'])
    kinds = [CHUNKS[k][0] for k in RS_WIRE]
    partials = _send_chip_partials([part_b[k] for k in RS_WIRE], kinds)
    fin = [_rs_finish(part[k], rb, kind, up_pos_arr if kind == 'cols' else chip_arr, c_arr, name='rs_finish_' + k)
           for k, rb, kind in zip(RS_WIRE, partials, kinds)]
    return dict(zip(RS_WIRE, _share_layers(fin)))


def kernel(x, norm1_g, w_in, b_f, q_norm_g, k_norm_g, conv_dw_w, conv_dw_b, conv_ln_g, conv_ln_b, conv_pw_w, pool_w, pool_scale, w_out, norm2_g, w_up, ffn_dw_w, w_down, loss_target, m_norm1_g, m_w_in, m_b_f, m_q_norm_g, m_k_norm_g, m_conv_dw_w, m_conv_dw_b, m_conv_ln_g, m_conv_ln_b, m_conv_pw_w, m_pool_w, m_pool_scale, m_w_out, m_norm2_g, m_w_up, m_ffn_dw_w, m_w_down, v_norm1_g, v_w_in, v_b_f, v_q_norm_g, v_k_norm_g, v_conv_dw_w, v_conv_dw_b, v_conv_ln_g, v_conv_ln_b, v_conv_pw_w, v_pool_w, v_pool_scale, v_w_out, v_norm2_g, v_w_up, v_ffn_dw_w, v_w_down):
    given = dict(locals())
    w = {k: given[k] for k in WEIGHTS}
    mom_m = {k: given['m_' + k] for k in WEIGHTS}
    mom_v = {k: given['v_' + k] for k in WEIGHTS}
    place = _mesh_place()
    chip = place[0]
    W = _gather_full(w, place)

    loss_part, grad_x, G, g_small = _local_step(x[0], loss_target[0], W, {k: w[k] for k in REPLICATED})
    loss = lax.psum(loss_part[0, 0], ('x', 'y', 'c'))

    g_sum = _reduce_scatter(G, place)

    small = _pack_rows([g_small[k].reshape(-1) for k in SMALL_GRADS], 8)
    small_sum = _all_reduce_small(small)

    delta, new_m, new_v = {}, {}, {}
    for k in RS_WIRE:
        outs = _adamw(_as_2d(w[k]), _as_2d(g_sum[k]), _as_2d(mom_m[k]), _as_2d(mom_v[k]), name='adamw_' + k)
        g_sum[k], delta[k], new_m[k], new_v[k] = [o.reshape(w[k].shape) for o in outs]
    off = 0
    small_full = {}
    for k in SMALL_GRADS:
        small_full[k] = small_sum.reshape(-1)[off:off + g_small[k].size].reshape(g_small[k].shape)
        off += g_small[k].size
    small_g = {k: small_full[k] for k in REPLICATED}
    small_g['conv_dw_w'] = lax.dynamic_slice_in_dim(small_full['conv_dw_w'], chip * w['conv_dw_w'].shape[2], w['conv_dw_w'].shape[2], axis=2)
    small_g['conv_pw_w'] = lax.dynamic_slice_in_dim(small_full['conv_pw_w'], chip * w['conv_pw_w'].shape[1], w['conv_pw_w'].shape[1], axis=1)
    small_g['ffn_dw_w'] = lax.dynamic_slice_in_dim(small_full['ffn_dw_w'], chip * w['ffn_dw_w'].shape[2], w['ffn_dw_w'].shape[2], axis=2)
    pack_small = lambda t: _pack_rows([t[k].reshape(-1) for k in SMALL_GRADS], 8)
    outs = _adamw(pack_small(w), pack_small(small_g), pack_small(mom_m), pack_small(mom_v), name='adamw_small')
    off = 0
    for k in SMALL_GRADS:
        pieces = [o.reshape(-1)[off:off + w[k].size].reshape(w[k].shape) for o in outs]
        g_sum[k], delta[k], new_m[k], new_v[k] = pieces
        off += w[k].size

    return (loss, grad_x[None], *[g_sum[k] for k in WEIGHTS], *[delta[k] for k in WEIGHTS],
            *[new_m[k] for k in WEIGHTS], *[new_v[k] for k in WEIGHTS])
```

```python
import functools

import jax
import jax.numpy as jnp
from jax import lax
from jax.experimental import pallas as pl
from jax.experimental.pallas import tpu as pltpu

F32 = jnp.float32
BF16 = jnp.bfloat16

N_HEADS = 8
HEAD_DIM = 64
D_ATT = N_HEADS * HEAD_DIM
D_CONV = 256
D_POOL = 256
D_MIX = D_ATT + D_CONV + D_POOL
D_QKV = 3 * D_ATT
D_PROJ_A = D_QKV + 2 * D_CONV + D_POOL
D_IN = D_PROJ_A + N_HEADS
FG_ROWS = 128
CONV_WIDTH = 31
CONV_HALO = 32
POOL_WINDOWS = (2, 4, 8, 16)
POOL_GROUP = 64
POOL_HALO = 16
FFN_CONV_WIDTH = 3
FFN_HALO = 8
ATT_SCALE = HEAD_DIM ** -0.5
EPS = 1e-6
NEG = -1e30
LANES = 128

ADAM_LR = 0.001
ADAM_B1 = 0.9
ADAM_B2 = 0.999
ADAM_EPS = 1e-08
ADAM_WD = 0.01
ADAM_STEP = 10

N_CHIPS = 4
N_DEV = 8
VMEM_LIMIT_BYTES = 56 * 1024 * 1024

REPLICATED = ('norm1_g', 'b_f', 'q_norm_g', 'k_norm_g', 'conv_dw_b', 'conv_ln_g', 'conv_ln_b',
              'pool_w', 'pool_scale', 'norm2_g')
WEIGHTS = ('norm1_g', 'w_in', 'b_f', 'q_norm_g', 'k_norm_g', 'conv_dw_w', 'conv_dw_b', 'conv_ln_g',
           'conv_ln_b', 'conv_pw_w', 'pool_w', 'pool_scale', 'w_out', 'norm2_g', 'w_up', 'ffn_dw_w', 'w_down')


def _tile(dim, pref, unit=LANES):
    if dim <= pref:
        return dim
    t = (pref // unit) * unit
    while t >= unit:
        if dim % t == 0:
            return t
        t -= unit
    raise ValueError(f'no tile for {dim} (preferred {pref})')


def _params(*sem):
    return pltpu.CompilerParams(dimension_semantics=sem, vmem_limit_bytes=VMEM_LIMIT_BYTES)


def _sigmoid(x):
    return 1.0 / (1.0 + jnp.exp(-x))


def _dot(a, b, ca, cb):
    return lax.dot_general(a, b, (((ca,), (cb,)), ((), ())), preferred_element_type=F32)


def _split3(y):
    y1 = y.astype(BF16)
    r1 = y - y1.astype(F32)
    y2 = r1.astype(BF16)
    y3 = (r1 - y2.astype(F32)).astype(BF16)
    return y1, y2, y3


def _dot3(y, e, ca=1, cb=0):
    y1, y2, y3 = _split3(y)
    return _dot(y1, e, ca, cb) + _dot(y2, e, ca, cb) + _dot(y3, e, ca, cb)


def _lead(spec_shape, imap, lead):
    if lead is None:
        return pl.BlockSpec(spec_shape, imap)
    return pl.BlockSpec((None,) + spec_shape, lambda *g: (lead,) + imap(*g))


ANY_SPEC = pl.BlockSpec(memory_space=pl.ANY)


def _mm(a, b, *, name, ta=False, tb=False, res=None, out_dtype=F32, tm=512, tn=512, tk=1024,
        a_lead=None, b_lead=None, copy16=False):
    a2, b2 = a.shape[-2:], b.shape[-2:]
    K, M = a2 if ta else a2[::-1]
    N, Kb = b2 if tb else b2[::-1]
    assert K == Kb, (a.shape, b.shape)
    tm, tn, tk = _tile(M, tm), _tile(N, tn), _tile(K, tk)
    nk = K // tk
    ca = 0 if ta else 1
    cb = 1 if tb else 0
    has_res = res is not None
    n_in = 2 + has_res
    n_out = 1 + copy16

    def body(*refs):
        a_ref, b_ref = refs[:2]
        r_ref = refs[2] if has_res else None
        o_refs = refs[n_in:n_in + n_out]
        scratch = refs[n_in + n_out:]

        def write(r):
            if has_res:
                r = r + r_ref[...]
            o_refs[0][...] = r.astype(out_dtype)
            if copy16:
                o_refs[1][...] = r.astype(BF16)

        p = _dot(a_ref[...].astype(BF16), b_ref[...].astype(BF16), ca, cb)
        if nk == 1:
            write(p)
        else:
            acc = scratch[0]
            k = pl.program_id(2)

            @pl.when(k == 0)
            def _():
                acc[...] = p

            @pl.when(k > 0)
            def _():
                acc[...] += p

            @pl.when(k == nk - 1)
            def _():
                write(acc[...])

    a_spec = _lead((tk, tm), lambda i, j, k: (k, i), a_lead) if ta else _lead((tm, tk), lambda i, j, k: (i, k), a_lead)
    b_spec = _lead((tn, tk), lambda i, j, k: (j, k), b_lead) if tb else _lead((tk, tn), lambda i, j, k: (k, j), b_lead)
    o_spec = pl.BlockSpec((tm, tn), lambda i, j, k: (i, j))
    in_specs = [a_spec, b_spec] + ([o_spec] if has_res else [])
    args = (a, b) + ((res,) if has_res else ())
    out_shape = [jax.ShapeDtypeStruct((M, N), out_dtype)] + ([jax.ShapeDtypeStruct((M, N), BF16)] if copy16 else [])
    out = pl.pallas_call(
        body, name=name,
        out_shape=tuple(out_shape),
        grid=(M // tm, N // tn, nk),
        in_specs=in_specs, out_specs=tuple([o_spec] * n_out),
        scratch_shapes=[pltpu.VMEM((tm, tn), F32)] if nk > 1 else [],
        compiler_params=_params('parallel', 'parallel', 'arbitrary'),
    )(*args)
    return out if copy16 else out[0]


def _rms_fwd(x, g, *, name, ts=512):
    S, D = x.shape
    ts = _tile(S, ts, 8)

    def body(x_ref, g_ref, o_ref):
        xv = x_ref[...]
        r = lax.rsqrt(jnp.mean(xv * xv, axis=-1, keepdims=True) + EPS)
        o_ref[...] = (xv * r * g_ref[...]).astype(BF16)

    return pl.pallas_call(
        body, name=name, out_shape=jax.ShapeDtypeStruct((S, D), BF16), grid=(S // ts,),
        in_specs=[pl.BlockSpec((ts, D), lambda i: (i, 0)), pl.BlockSpec((1, D), lambda i: (0, 0))],
        out_specs=pl.BlockSpec((ts, D), lambda i: (i, 0)),
        compiler_params=_params('parallel'),
    )(x, g)


def _rms_bwd(x, g, dh, dres, *, name, ts=512):
    S, D = x.shape
    ts = _tile(S, ts, 8)

    def body(x_ref, g_ref, dh_ref, dr_ref, dx_ref, dg_ref):
        i = pl.program_id(0)
        xv = x_ref[...]
        r = lax.rsqrt(jnp.mean(xv * xv, axis=-1, keepdims=True) + EPS)
        y = xv * r
        dh_v = dh_ref[...]
        dy = dh_v * g_ref[...]
        dx_ref[...] = dr_ref[...] + r * (dy - y * jnp.mean(dy * y, axis=-1, keepdims=True))
        part = jnp.sum(dh_v * y, axis=0, keepdims=True)

        @pl.when(i == 0)
        def _():
            dg_ref[...] = part

        @pl.when(i > 0)
        def _():
            dg_ref[...] += part

    row = pl.BlockSpec((ts, D), lambda i: (i, 0))
    vec = pl.BlockSpec((1, D), lambda i: (0, 0))
    return pl.pallas_call(
        body, name=name,
        out_shape=(jax.ShapeDtypeStruct((S, D), F32), jax.ShapeDtypeStruct((1, D), F32)),
        grid=(S // ts,), in_specs=[row, vec, row, row], out_specs=(row, vec),
        compiler_params=_params('arbitrary'),
    )(x, g, dh, dres)


def _group_ones():
    i = lax.broadcasted_iota(jnp.int32, (D_ATT, D_ATT), 0) // HEAD_DIM
    j = lax.broadcasted_iota(jnp.int32, (D_ATT, D_ATT), 1) // HEAD_DIM
    return (i == j).astype(BF16)


def _qk_prep_fwd(proj_a, qg, kg, *, name, ts=512):
    S = proj_a.shape[0]
    ts = _tile(S, ts, 16)

    def body(q_ref, k_ref, v_ref, qg_ref, kg_ref, e_ref, o_ref):
        e = e_ref[...]

        def norm(xv, gain):
            ms = _dot3(xv * xv, e) * (1.0 / HEAD_DIM)
            return xv * lax.rsqrt(ms + EPS) * gain

        o_ref[:, 0:D_ATT] = (norm(q_ref[...], qg_ref[...]) * ATT_SCALE).astype(BF16)
        o_ref[:, D_ATT:2 * D_ATT] = norm(k_ref[...], kg_ref[...]).astype(BF16)
        o_ref[:, 2 * D_ATT:3 * D_ATT] = v_ref[...].astype(BF16)

    col = lambda c: pl.BlockSpec((ts, D_ATT), lambda i: (i, c))
    vec = pl.BlockSpec((1, D_ATT), lambda i: (0, 0))
    return pl.pallas_call(
        body, name=name, out_shape=jax.ShapeDtypeStruct((S, D_QKV), BF16), grid=(S // ts,),
        in_specs=[col(0), col(1), col(2), vec, vec, pl.BlockSpec((D_ATT, D_ATT), lambda i: (0, 0))],
        out_specs=pl.BlockSpec((ts, D_QKV), lambda i: (i, 0)),
        compiler_params=_params('parallel'),
    )(proj_a, proj_a, proj_a, qg, kg, _group_ones())


def _qk_prep_bwd(proj_a, dq, dk, dv, qg, kg, *, name, ts=512):
    S = proj_a.shape[0]
    ts = _tile(S, ts, 16)

    def body(q_ref, k_ref, dq_ref, dk_ref, dv_ref, qg_ref, kg_ref, e_ref, o_ref, dqg_ref, dkg_ref):
        i = pl.program_id(0)
        e = e_ref[...]

        def norm_bwd(xv, dn, gain, scale):
            ms = _dot3(xv * xv, e) * (1.0 / HEAD_DIM)
            r = lax.rsqrt(ms + EPS)
            y = xv * r
            dy = dn * (gain * scale)
            mean = _dot3(dy * y, e) * (1.0 / HEAD_DIM)
            return r * (dy - y * mean), jnp.sum(dn * y, axis=0, keepdims=True) * scale

        dq_raw, dqg = norm_bwd(q_ref[...], dq_ref[...], qg_ref[...], ATT_SCALE)
        dk_raw, dkg = norm_bwd(k_ref[...], dk_ref[...], kg_ref[...], 1.0)
        o_ref[:, 0:D_ATT] = dq_raw.astype(BF16)
        o_ref[:, D_ATT:2 * D_ATT] = dk_raw.astype(BF16)
        o_ref[:, 2 * D_ATT:3 * D_ATT] = dv_ref[...].astype(BF16)

        @pl.when(i == 0)
        def _():
            dqg_ref[...] = dqg
            dkg_ref[...] = dkg

        @pl.when(i > 0)
        def _():
            dqg_ref[...] += dqg
            dkg_ref[...] += dkg

    col = lambda c: pl.BlockSpec((ts, D_ATT), lambda i: (i, c))
    vec = pl.BlockSpec((1, D_ATT), lambda i: (0, 0))
    return pl.pallas_call(
        body, name=name,
        out_shape=(jax.ShapeDtypeStruct((S, D_PROJ_A), BF16), jax.ShapeDtypeStruct((1, D_ATT), F32),
                   jax.ShapeDtypeStruct((1, D_ATT), F32)),
        grid=(S // ts,),
        in_specs=[col(0), col(1), col(0), col(0), col(0), vec, vec, pl.BlockSpec((D_ATT, D_ATT), lambda i: (0, 0))],
        out_specs=(pl.BlockSpec((ts, D_QKV), lambda i: (i, 0)), vec, vec),
        compiler_params=_params('arbitrary'),
    )(proj_a, proj_a, dq, dk, dv, qg, kg, _group_ones())


def _tri_ones(upper):
    i = lax.broadcasted_iota(jnp.int32, (LANES, LANES), 0)
    j = lax.broadcasted_iota(jnp.int32, (LANES, LANES), 1)
    return ((i <= j) if upper else (i >= j)).astype(BF16)


def _forget_fwd(z_raw, b_col, *, name):
    R, S = z_raw.shape
    nb = S // LANES

    def body(z_ref, b_ref, u_ref, f_ref):
        u = u_ref[...]
        carry = jnp.zeros((R, 1), F32)
        for j in range(nb):
            z = z_ref[:, j * LANES:(j + 1) * LANES] + b_ref[...]
            logf = jnp.minimum(z, 0.0) - jnp.log(1.0 + jnp.exp(-jnp.abs(z)))
            f_ref[:, j * LANES:(j + 1) * LANES] = _dot3(logf, u) + carry
            carry = carry + jnp.sum(logf, axis=1, keepdims=True)

    return pl.pallas_call(
        body, name=name, out_shape=jax.ShapeDtypeStruct((R, S), F32),
        compiler_params=pltpu.CompilerParams(vmem_limit_bytes=VMEM_LIMIT_BYTES),
    )(z_raw, b_col, _tri_ones(True))


def _forget_bwd(z_raw, b_col, df, *, name):
    R, S = z_raw.shape
    nb = S // LANES

    def body(z_ref, b_ref, df_ref, l_ref, dz_ref, db_ref):
        low = l_ref[...]
        carry = jnp.zeros((R, 1), F32)
        db = jnp.zeros((R, 1), F32)
        for j in reversed(range(nb)):
            d = df_ref[:, j * LANES:(j + 1) * LANES]
            dlogf = _dot3(d, low) + carry
            carry = carry + jnp.sum(d, axis=1, keepdims=True)
            z = z_ref[:, j * LANES:(j + 1) * LANES] + b_ref[...]
            dz = dlogf * _sigmoid(-z)
            dz_ref[:, j * LANES:(j + 1) * LANES] = dz
            db = db + jnp.sum(dz, axis=1, keepdims=True)
        db_ref[...] = db

    return pl.pallas_call(
        body, name=name,
        out_shape=(jax.ShapeDtypeStruct((R, S), F32), jax.ShapeDtypeStruct((R, 1), F32)),
        compiler_params=pltpu.CompilerParams(vmem_limit_bytes=VMEM_LIMIT_BYTES),
    )(z_raw, b_col, df, _tri_ones(False))


def _head_mask(hh):
    lane = lax.broadcasted_iota(jnp.int32, (1, LANES), 1)
    return (lane // HEAD_DIM) == hh


def _causal(s, qi, ki, t):
    rows = qi * t + lax.broadcasted_iota(jnp.int32, (t, t), 0)
    cols = ki * t + lax.broadcasted_iota(jnp.int32, (t, t), 1)
    return jnp.where(cols <= rows, s, NEG)


def _attn_fwd(qkv, f3, *, name, t=512):
    S = qkv.shape[0]
    t = _tile(S, t)
    n = S // t
    npair = N_HEADS // 2

    def body(q_ref, k_ref, v_ref, f_ref, mix_ref, o_ref, lse_ref, m_s, l_s, acc_s):
        qi, ki = pl.program_id(1), pl.program_id(2)

        @pl.when(ki == 0)
        def _():
            m_s[...] = jnp.full(m_s.shape, NEG, F32)
            l_s[...] = jnp.zeros(l_s.shape, F32)
            acc_s[...] = jnp.zeros(acc_s.shape, F32)

        @pl.when(ki <= qi)
        def _():
            q, k, v = q_ref[...], k_ref[...], v_ref[...]
            for hh in range(2):
                msk = _head_mask(hh)
                qm = jnp.where(msk, q, jnp.zeros_like(q))
                vm = jnp.where(msk, v, jnp.zeros_like(v))
                s = _causal(_dot(qm, k, 1, 1) - f_ref[0, hh:hh + 1, :], qi, ki, t)
                m_prev = m_s[hh]
                m_new = jnp.maximum(m_prev, jnp.max(s, axis=1, keepdims=True))
                alpha = jnp.exp(m_prev - m_new)
                p = jnp.exp(s - m_new)
                l_s[hh] = alpha * l_s[hh] + jnp.sum(p, axis=1, keepdims=True)
                acc_s[hh] = alpha * acc_s[hh] + _dot(p.astype(BF16), vm, 1, 0)
                m_s[hh] = m_new

        @pl.when(ki == qi)
        def _():
            o = acc_s[0] / l_s[0] + acc_s[1] / l_s[1]
            o_ref[...] = o
            mix_ref[...] = o.astype(BF16)
            lse0 = m_s[0] + jnp.log(l_s[0])
            lse1 = m_s[1] + jnp.log(l_s[1])
            lse_ref[...] = jnp.where(_head_mask(0), lse0, lse1)

    out = pl.BlockSpec((t, LANES), lambda h, i, j: (i, h))
    return pl.pallas_call(
        body, name=name,
        out_shape=(jax.ShapeDtypeStruct((S, D_MIX), BF16), jax.ShapeDtypeStruct((S, D_ATT), F32),
                   jax.ShapeDtypeStruct((S, D_ATT), F32)),
        grid=(npair, n, n),
        in_specs=[pl.BlockSpec((t, LANES), lambda h, i, j: (i, h)),
                  pl.BlockSpec((t, LANES), lambda h, i, j: (jnp.minimum(i, j), npair + h)),
                  pl.BlockSpec((t, LANES), lambda h, i, j: (jnp.minimum(i, j), 2 * npair + h)),
                  pl.BlockSpec((1, 2, t), lambda h, i, j: (h, 0, jnp.minimum(i, j)))],
        out_specs=(out, out, out),
        scratch_shapes=[pltpu.VMEM((2, t, 1), F32), pltpu.VMEM((2, t, 1), F32), pltpu.VMEM((2, t, LANES), F32)],
        compiler_params=_params('parallel', 'parallel', 'arbitrary'),
    )(qkv, qkv, qkv, f3)


def _attn_bwd(qkv, f3, att, lse, d_mix, *, name, t=512):
    S = qkv.shape[0]
    t = _tile(S, t)
    n = S // t
    npair = N_HEADS // 2

    def body(q_ref, k_ref, v_ref, f_ref, o_ref, lse_ref, do_ref, dq_ref, dk_ref, dv_ref, df_ref, dr_ref, dk_s, dv_s, df_s):
        ki, qi = pl.program_id(1), pl.program_id(2)

        @pl.when(qi == ki)
        def _():
            dk_s[...] = jnp.zeros(dk_s.shape, F32)
            dv_s[...] = jnp.zeros(dv_s.shape, F32)
            df_s[...] = jnp.zeros(df_s.shape, F32)

        @pl.when(qi >= ki)
        def _():
            q, k, v = q_ref[...], k_ref[...], v_ref[...]
            do, o, lse = do_ref[...], o_ref[...], lse_ref[...]
            dq_blk = jnp.zeros((t, LANES), F32)
            dr_blk = jnp.zeros((t, LANES), F32)
            for hh in range(2):
                msk = _head_mask(hh)
                qm = jnp.where(msk, q, jnp.zeros_like(q))
                km = jnp.where(msk, k, jnp.zeros_like(k))
                dom = jnp.where(msk, do, 0.0).astype(BF16)
                s = _causal(_dot(qm, k, 1, 1) - f_ref[0, hh:hh + 1, :], qi, ki, t)
                lse_h = jnp.max(jnp.where(msk, lse, NEG), axis=1, keepdims=True)
                p = jnp.exp(s - lse_h)
                dp = _dot(dom, v, 1, 1)
                delta = jnp.sum(dom.astype(F32) * o, axis=1, keepdims=True)
                ds = p * (dp - delta)
                dsb = ds.astype(BF16)
                dv_s[...] += _dot(p.astype(BF16), dom, 0, 0)
                dk_s[...] += _dot(dsb, qm, 0, 0)
                dq_blk = dq_blk + _dot(dsb, km, 1, 0)
                df_s[hh] -= jnp.sum(ds, axis=0, keepdims=True)
                dr_blk = dr_blk + jnp.where(msk, jnp.sum(ds, axis=1, keepdims=True), 0.0)
            rows = pl.ds(pl.multiple_of(qi * t, t), t)

            @pl.when(ki == 0)
            def _():
                dq_ref[rows, :] = dq_blk
                dr_ref[rows, :] = dr_blk

            @pl.when(ki > 0)
            def _():
                dq_ref[rows, :] += dq_blk
                dr_ref[rows, :] += dr_blk

        @pl.when(qi == n - 1)
        def _():
            dk_ref[...] = dk_s[...]
            dv_ref[...] = dv_s[...]
            df_ref[0, 0:1, :] = df_s[0]
            df_ref[0, 1:2, :] = df_s[1]

    qrow = lambda h, j, i: (jnp.maximum(i, j), h)
    return pl.pallas_call(
        body, name=name,
        out_shape=(jax.ShapeDtypeStruct((S, D_ATT), F32), jax.ShapeDtypeStruct((S, D_ATT), F32),
                   jax.ShapeDtypeStruct((S, D_ATT), F32), jax.ShapeDtypeStruct((npair, 2, S), F32),
                   jax.ShapeDtypeStruct((S, D_ATT), F32)),
        grid=(npair, n, n),
        in_specs=[pl.BlockSpec((t, LANES), qrow),
                  pl.BlockSpec((t, LANES), lambda h, j, i: (j, npair + h)),
                  pl.BlockSpec((t, LANES), lambda h, j, i: (j, 2 * npair + h)),
                  pl.BlockSpec((1, 2, t), lambda h, j, i: (h, 0, j)),
                  pl.BlockSpec((t, LANES), qrow),
                  pl.BlockSpec((t, LANES), qrow),
                  pl.BlockSpec((t, LANES), qrow)],
        out_specs=(pl.BlockSpec((S, LANES), lambda h, j, i: (0, h)),
                   pl.BlockSpec((t, LANES), lambda h, j, i: (j, h)),
                   pl.BlockSpec((t, LANES), lambda h, j, i: (j, h)),
                   pl.BlockSpec((1, 2, t), lambda h, j, i: (h, 0, j)),
                   pl.BlockSpec((S, LANES), lambda h, j, i: (0, h))),
        scratch_shapes=[pltpu.VMEM((t, LANES), F32), pltpu.VMEM((t, LANES), F32), pltpu.VMEM((2, 1, t), F32)],
        compiler_params=_params('parallel', 'arbitrary', 'arbitrary'),
    )(qkv, qkv, qkv, f3, att, lse, d_mix)


A_COL = D_QKV // D_CONV
B_COL = A_COL + 1
P_COL = B_COL + 1


def _layer_norm_stats(c):
    mu = jnp.mean(c, axis=-1, keepdims=True)
    xc = c - mu
    rstd = lax.rsqrt(jnp.mean(xc * xc, axis=-1, keepdims=True) + EPS)
    return xc * rstd, rstd


def _glu_into(buf, a_ref, b_ref, ah_ref, bh_ref, first, ts):
    halo = ah_ref[...] * _sigmoid(bh_ref[...])
    buf[0:CONV_HALO, :] = jnp.where(first, 0.0, halo)
    buf[CONV_HALO:CONV_HALO + ts, :] = a_ref[...] * _sigmoid(b_ref[...])


def _dwconv(buf, w_ref, ts):
    off = CONV_HALO - (CONV_WIDTH - 1)
    acc = w_ref[0:1, :] * buf[pl.ds(off, ts), :]
    for k in range(1, CONV_WIDTH):
        acc = acc + w_ref[k:k + 1, :] * buf[pl.ds(off + k, ts), :]
    return acc


def _conv_specs(ts, tmap):
    hb = ts // CONV_HALO
    cur = lambda c: pl.BlockSpec((ts, D_CONV), lambda i: (tmap(i), c))
    halo = lambda c: pl.BlockSpec((CONV_HALO, D_CONV), lambda i: (jnp.maximum(tmap(i) * hb - 1, 0), c))
    return cur, halo


def _conv_fwd(proj_a, mix, dw_w, dw_b, ln_g, ln_b, pw_w, l, *, name, ts=512):
    S = proj_a.shape[0]
    ts = _tile(S, ts, CONV_HALO)

    def body(a_ref, b_ref, ah_ref, bh_ref, w_ref, wb_ref, g_ref, bb_ref, pw_ref, mix_in, o_ref, buf):
        _glu_into(buf, a_ref, b_ref, ah_ref, bh_ref, pl.program_id(0) == 0, ts)
        c = _dwconv(buf, w_ref, ts) + wb_ref[...]
        yhat, _ = _layer_norm_stats(c)
        y = yhat * g_ref[...] + bb_ref[...]
        hs = y * _sigmoid(y)
        o_ref[...] = _dot(hs.astype(BF16), pw_ref[...], 1, 0).astype(BF16)

    cur, halo = _conv_specs(ts, lambda i: i)
    vec = pl.BlockSpec((1, D_CONV), lambda i: (0, 0))
    return pl.pallas_call(
        body, name=name, out_shape=jax.ShapeDtypeStruct(mix.shape, BF16), grid=(S // ts,),
        in_specs=[cur(A_COL), cur(B_COL), halo(A_COL), halo(B_COL),
                  pl.BlockSpec((None, CONV_HALO, D_CONV), lambda i: (l, 0, 0)), vec, vec, vec,
                  pl.BlockSpec((None, D_CONV, D_CONV), lambda i: (l, 0, 0)), ANY_SPEC],
        out_specs=pl.BlockSpec((ts, D_CONV), lambda i: (i, D_ATT // D_CONV)),
        scratch_shapes=[pltpu.VMEM((CONV_HALO + ts, D_CONV), F32)],
        input_output_aliases={9: 0},
        compiler_params=_params('parallel'),
    )(proj_a, proj_a, proj_a, proj_a, dw_w, dw_b, ln_g, ln_b, pw_w, mix)


def _conv_bwd(proj_a, d_mix, d_proj, dw_w, dw_b, ln_g, ln_b, pw_w, l, *, name, ts=512):
    S = proj_a.shape[0]
    ts = _tile(S, ts, CONV_HALO)
    n = S // ts
    d_col = D_ATT // D_CONV

    def body(a_ref, b_ref, ah_ref, bh_ref, dy_ref, w_ref, wb_ref, g_ref, bb_ref, pw_ref, dp_in,
             o_ref, dw_ref, dwb_ref, dg_ref, dbb_ref, dpw_ref, buf, dcbuf):
        i = pl.program_id(0)
        _glu_into(buf, a_ref, b_ref, ah_ref, bh_ref, i == n - 1, ts)
        c = _dwconv(buf, w_ref, ts) + wb_ref[...]
        yhat, rstd = _layer_norm_stats(c)
        y = yhat * g_ref[...] + bb_ref[...]
        sg = _sigmoid(y)
        hs = y * sg
        dout = dy_ref[...].astype(BF16)
        d_hs = _dot(dout, pw_ref[...], 1, 1)
        d_y = d_hs * (sg * (1.0 + y * (1.0 - sg)))
        d_yhat = d_y * g_ref[...]
        d_c = rstd * (d_yhat - jnp.mean(d_yhat, axis=-1, keepdims=True)
                      - yhat * jnp.mean(d_yhat * yhat, axis=-1, keepdims=True))

        @pl.when(i == 0)
        def _():
            dcbuf[ts:ts + CONV_HALO, :] = jnp.zeros((CONV_HALO, D_CONV), F32)
            dw_ref[...] = jnp.zeros(dw_ref.shape, F32)
            dwb_ref[...] = jnp.zeros(dwb_ref.shape, F32)
            dg_ref[...] = jnp.zeros(dg_ref.shape, F32)
            dbb_ref[...] = jnp.zeros(dbb_ref.shape, F32)
            dpw_ref[...] = jnp.zeros(dpw_ref.shape, F32)

        dcbuf[0:ts, :] = d_c
        dpw_ref[...] += _dot(hs.astype(BF16), dout, 0, 0)
        dg_ref[...] += jnp.sum(d_y * yhat, axis=0, keepdims=True)
        dbb_ref[...] += jnp.sum(d_y, axis=0, keepdims=True)
        dwb_ref[...] += jnp.sum(d_c, axis=0, keepdims=True)
        off = CONV_HALO - (CONV_WIDTH - 1)
        d_h = jnp.zeros((ts, D_CONV), F32)
        for k in range(CONV_WIDTH):
            d_h = d_h + w_ref[k:k + 1, :] * dcbuf[pl.ds(CONV_WIDTH - 1 - k, ts), :]
            dw_ref[k:k + 1, :] += jnp.sum(d_c * buf[pl.ds(off + k, ts), :], axis=0, keepdims=True)
        dcbuf[ts:ts + CONV_HALO, :] = d_c[0:CONV_HALO, :]
        a, sb = a_ref[...], _sigmoid(b_ref[...])
        o_ref[:, 0:D_CONV] = (d_h * sb).astype(BF16)
        o_ref[:, D_CONV:2 * D_CONV] = (d_h * a * sb * (1.0 - sb)).astype(BF16)

    rev = lambda i: n - 1 - i
    cur, halo = _conv_specs(ts, rev)
    vec = pl.BlockSpec((1, D_CONV), lambda i: (0, 0))
    wspec = pl.BlockSpec((CONV_HALO, D_CONV), lambda i: (0, 0))
    sq = pl.BlockSpec((D_CONV, D_CONV), lambda i: (0, 0))
    return pl.pallas_call(
        body, name=name,
        out_shape=(jax.ShapeDtypeStruct(d_proj.shape, BF16), jax.ShapeDtypeStruct((CONV_HALO, D_CONV), F32),
                   jax.ShapeDtypeStruct((1, D_CONV), F32), jax.ShapeDtypeStruct((1, D_CONV), F32),
                   jax.ShapeDtypeStruct((1, D_CONV), F32), jax.ShapeDtypeStruct((D_CONV, D_CONV), F32)),
        grid=(n,),
        in_specs=[cur(A_COL), cur(B_COL), halo(A_COL), halo(B_COL),
                  pl.BlockSpec((ts, D_CONV), lambda i: (rev(i), d_col)),
                  pl.BlockSpec((None, CONV_HALO, D_CONV), lambda i: (l, 0, 0)), vec, vec, vec,
                  pl.BlockSpec((None, D_CONV, D_CONV), lambda i: (l, 0, 0)), ANY_SPEC],
        out_specs=(pl.BlockSpec((ts, 2 * D_CONV), lambda i: (rev(i), D_QKV // (2 * D_CONV))), wspec, vec, vec, vec, sq),
        scratch_shapes=[pltpu.VMEM((CONV_HALO + ts, D_CONV), F32), pltpu.VMEM((ts + CONV_HALO, D_CONV), F32)],
        input_output_aliases={10: 0},
        compiler_params=_params('arbitrary'),
    )(proj_a, proj_a, proj_a, proj_a, d_mix, dw_w, dw_b, ln_g, ln_b, pw_w, d_proj)


def _pool_window():
    lane = lax.broadcasted_iota(jnp.int32, (1, D_POOL), 1)
    w = jnp.full((1, D_POOL), POOL_WINDOWS[0], jnp.int32)
    for g in range(1, len(POOL_WINDOWS)):
        w = jnp.where(lane // POOL_GROUP == g, POOL_WINDOWS[g], w)
    return w


def _pool_diff(buf, u_ref, uh_ref, first, tile, ts):
    buf[0:POOL_HALO, :] = jnp.where(first, 0.0, uh_ref[...])
    u = u_ref[...]
    buf[POOL_HALO:POOL_HALO + ts, :] = u
    wl = _pool_window()
    acc = u
    for j in range(1, max(POOL_WINDOWS)):
        acc = acc + jnp.where(j < wl, buf[pl.ds(POOL_HALO - j, ts), :], 0.0)
    pos = tile * ts + lax.broadcasted_iota(jnp.int32, (ts, 1), 0)
    cnt = jnp.minimum(pos + 1, wl).astype(F32)
    return acc / cnt - u, cnt


def _pool_specs(ts, tmap):
    hb = ts // POOL_HALO
    cur = pl.BlockSpec((ts, D_POOL), lambda i: (tmap(i), P_COL))
    halo = pl.BlockSpec((POOL_HALO, D_POOL), lambda i: (jnp.maximum(tmap(i) * hb - 1, 0), P_COL))
    return cur, halo


def _pool_fwd(proj_a, mix, wbd, scale, *, name, ts=512):
    S = proj_a.shape[0]
    ts = _tile(S, ts, POOL_HALO)

    def body(u_ref, uh_ref, w_ref, s_ref, mix_in, o_ref, buf):
        i = pl.program_id(0)
        d, _ = _pool_diff(buf, u_ref, uh_ref, i == 0, i, ts)
        o_ref[...] = (_dot(d.astype(BF16), w_ref[...], 1, 0) * s_ref[...]).astype(BF16)

    cur, halo = _pool_specs(ts, lambda i: i)
    return pl.pallas_call(
        body, name=name, out_shape=jax.ShapeDtypeStruct(mix.shape, BF16), grid=(S // ts,),
        in_specs=[cur, halo, pl.BlockSpec((D_POOL, D_POOL), lambda i: (0, 0)), pl.BlockSpec((1, D_POOL), lambda i: (0, 0)),
                  ANY_SPEC],
        out_specs=pl.BlockSpec((ts, D_POOL), lambda i: (i, (D_ATT + D_CONV) // D_POOL)),
        scratch_shapes=[pltpu.VMEM((POOL_HALO + ts, D_POOL), F32)],
        input_output_aliases={4: 0},
        compiler_params=_params('parallel'),
    )(proj_a, proj_a, wbd, scale, mix)


def _pool_bwd(proj_a, d_mix, d_proj, wbd, scale, *, name, ts=512):
    S = proj_a.shape[0]
    ts = _tile(S, ts, POOL_HALO)
    n = S // ts
    d_col = (D_ATT + D_CONV) // D_POOL

    def body(u_ref, uh_ref, dy_ref, w_ref, s_ref, dp_in, o_ref, dw_ref, ds_ref, buf, ebuf):
        i = pl.program_id(0)
        tile = n - 1 - i
        d, cnt = _pool_diff(buf, u_ref, uh_ref, tile == 0, tile, ts)
        db = d.astype(BF16)
        ypre = _dot(db, w_ref[...], 1, 0)
        dout = dy_ref[...]
        d_y = (dout * s_ref[...]).astype(BF16)
        d_d = _dot(d_y, w_ref[...], 1, 1)

        @pl.when(i == 0)
        def _():
            ebuf[ts:ts + POOL_HALO, :] = jnp.zeros((POOL_HALO, D_POOL), F32)
            dw_ref[...] = jnp.zeros(dw_ref.shape, F32)
            ds_ref[...] = jnp.zeros(ds_ref.shape, F32)

        dw_ref[...] += _dot(db, d_y, 0, 0)
        ds_ref[...] += jnp.sum(dout * ypre, axis=0, keepdims=True)
        e = d_d / cnt
        ebuf[0:ts, :] = e
        wl = _pool_window()
        acc = e
        for j in range(1, max(POOL_WINDOWS)):
            acc = acc + jnp.where(j < wl, ebuf[pl.ds(j, ts), :], 0.0)
        ebuf[ts:ts + POOL_HALO, :] = e[0:POOL_HALO, :]
        o_ref[...] = (acc - d_d).astype(BF16)

    rev = lambda i: n - 1 - i
    cur, halo = _pool_specs(ts, rev)
    sq = pl.BlockSpec((D_POOL, D_POOL), lambda i: (0, 0))
    vec = pl.BlockSpec((1, D_POOL), lambda i: (0, 0))
    return pl.pallas_call(
        body, name=name,
        out_shape=(jax.ShapeDtypeStruct(d_proj.shape, BF16), jax.ShapeDtypeStruct((D_POOL, D_POOL), F32),
                   jax.ShapeDtypeStruct((1, D_POOL), F32)),
        grid=(n,),
        in_specs=[cur, halo, pl.BlockSpec((ts, D_POOL), lambda i: (rev(i), d_col)), sq, vec, ANY_SPEC],
        out_specs=(pl.BlockSpec((ts, D_POOL), lambda i: (rev(i), P_COL)), sq, vec),
        scratch_shapes=[pltpu.VMEM((POOL_HALO + ts, D_POOL), F32), pltpu.VMEM((ts + POOL_HALO, D_POOL), F32)],
        input_output_aliases={5: 0},
        compiler_params=_params('arbitrary'),
    )(proj_a, proj_a, d_mix, wbd, scale, d_proj)


FFN_LANES = 128
FFN_GROUP = 8 * 8


def _ffn_rows(ref, c, row0, j):
    return ref.at[c][pl.ds(row0 + j, 8, stride=8), :]


def _ffn_specs(ts, tc2, tmap, l):
    hb = ts // FFN_HALO
    cur = pl.BlockSpec((ts, tc2), lambda c, i: (tmap(i), c))
    halo = pl.BlockSpec((FFN_HALO, tc2), lambda c, i: (jnp.maximum(tmap(i) * hb - 1, 0), c))
    wspec = pl.BlockSpec((None, FFN_HALO, tc2), lambda c, i: (l, 0, c))
    return cur, halo, wspec


def _ffn_fill(buf, x_ref, xh_ref, first, ts, nblk):
    for c in range(nblk):
        cs = slice(c * FFN_LANES, (c + 1) * FFN_LANES)
        buf[c, 0:FFN_HALO, :] = jnp.where(first, 0.0, xh_ref[:, cs])
        buf[c, FFN_HALO:FFN_HALO + ts, :] = x_ref[:, cs]


def _ffn_conv_piece(buf, w_ref, r0, c):
    ws = [w_ref[k:k + 1, c * FFN_LANES:(c + 1) * FFN_LANES] for k in range(FFN_CONV_WIDTH)]
    xs = [_ffn_rows(buf, c, FFN_HALO + r0, j) for j in range(1 - FFN_CONV_WIDTH, 8)]
    outs = []
    for j in range(8):
        acc = ws[0] * xs[j]
        for k in range(1, FFN_CONV_WIDTH):
            acc = acc + ws[k] * xs[j + k]
        outs.append(acc)
    return outs, xs


def _ffn_act_fwd(up, w, l, *, name, ts=256):
    S, F2 = up.shape
    tc = F2 // 4
    nb = tc // FFN_LANES
    ts = _tile(S, ts, FFN_GROUP)

    def body(x_ref, xh_ref, w_ref, o_ref, buf, stage):
        _ffn_fill(buf, x_ref, xh_ref, pl.program_id(1) == 0, ts, 2 * nb)
        for c in range(nb):
            for r0 in range(0, ts, FFN_GROUP):
                gates, _ = _ffn_conv_piece(buf, w_ref, r0, c)
                vals, _ = _ffn_conv_piece(buf, w_ref, r0, nb + c)
                for j in range(8):
                    stage.at[c][pl.ds(r0 + j, 8, stride=8), :] = gates[j] * _sigmoid(gates[j]) * vals[j]
            o_ref[:, c * FFN_LANES:(c + 1) * FFN_LANES] = stage[c].astype(BF16)

    cur, halo, wspec = _ffn_specs(ts, 2 * tc, lambda i: i, l)
    return pl.pallas_call(
        body, name=name, out_shape=jax.ShapeDtypeStruct((S, F2 // 2), BF16), grid=(2, S // ts),
        in_specs=[cur, halo, wspec],
        out_specs=pl.BlockSpec((ts, tc), lambda c, i: (i, c)),
        scratch_shapes=[pltpu.VMEM((2 * nb, FFN_HALO + ts, FFN_LANES), F32), pltpu.VMEM((nb, ts, FFN_LANES), F32)],
        compiler_params=_params('parallel', 'parallel'),
    )(up, up, w)


def _ffn_act_bwd(up, d_act, w, l, *, name, ts=256):
    S, F2 = up.shape
    tc = F2 // 4
    nb = tc // FFN_LANES
    ts = _tile(S, ts, FFN_GROUP)
    n = S // ts

    def body(x_ref, xh_ref, da_ref, w_ref, o_ref, dw_ref, buf, dcbuf, stage):
        i = pl.program_id(1)
        _ffn_fill(buf, x_ref, xh_ref, i == n - 1, ts, 2 * nb)

        @pl.when(i == 0)
        def _():
            dcbuf[:, ts:ts + FFN_HALO, :] = jnp.zeros((2 * nb, FFN_HALO, FFN_LANES), F32)
            dw_ref[...] = jnp.zeros(dw_ref.shape, F32)

        for c in range(nb):
            blocks = (c, nb + c)
            stage[c, :, :] = da_ref[:, c * FFN_LANES:(c + 1) * FFN_LANES]
            dws = [[jnp.zeros((8, FFN_LANES), F32) for _ in range(FFN_CONV_WIDTH)] for _ in range(2)]
            for r0 in range(0, ts, FFN_GROUP):
                gates, xg = _ffn_conv_piece(buf, w_ref, r0, blocks[0])
                vals, xv = _ffn_conv_piece(buf, w_ref, r0, blocks[1])
                for j in range(8):
                    sg = _sigmoid(gates[j])
                    da = _ffn_rows(stage, c, r0, j)
                    d_cs = (da * vals[j] * (sg * (1.0 + gates[j] * (1.0 - sg))), da * (gates[j] * sg))
                    for half, (d_c, xs) in enumerate(zip(d_cs, (xg, xv))):
                        dcbuf.at[blocks[half]][pl.ds(r0 + j, 8, stride=8), :] = d_c
                        for k in range(FFN_CONV_WIDTH):
                            dws[half][k] = dws[half][k] + d_c * xs[j + k]
            for half in range(2):
                cs = slice(blocks[half] * FFN_LANES, (blocks[half] + 1) * FFN_LANES)
                for k in range(FFN_CONV_WIDTH):
                    dw_ref[k:k + 1, cs] += jnp.sum(dws[half][k], axis=0, keepdims=True)
            for b in blocks:
                cs = slice(b * FFN_LANES, (b + 1) * FFN_LANES)
                ws = [w_ref[k:k + 1, cs] for k in range(FFN_CONV_WIDTH)]
                for r0 in range(0, ts, FFN_GROUP):
                    ds = [_ffn_rows(dcbuf, b, r0, j) for j in range(8 + FFN_CONV_WIDTH - 1)]
                    for j in range(8):
                        d_x = ws[FFN_CONV_WIDTH - 1] * ds[j]
                        for k in range(FFN_CONV_WIDTH - 1):
                            d_x = d_x + ws[k] * ds[j + FFN_CONV_WIDTH - 1 - k]
                        stage.at[c][pl.ds(r0 + j, 8, stride=8), :] = d_x
                o_ref[:, cs] = stage[c].astype(BF16)
                dcbuf[b, ts:ts + FFN_HALO, :] = dcbuf[b, 0:FFN_HALO, :]

    rev = lambda i: n - 1 - i
    cur, halo, wspec = _ffn_specs(ts, 2 * tc, rev, l)
    return pl.pallas_call(
        body, name=name,
        out_shape=(jax.ShapeDtypeStruct((S, F2), BF16), jax.ShapeDtypeStruct((FFN_HALO, F2), F32)),
        grid=(2, n),
        in_specs=[cur, halo, pl.BlockSpec((ts, tc), lambda c, i: (rev(i), c)), wspec],
        out_specs=(cur, pl.BlockSpec((FFN_HALO, 2 * tc), lambda c, i: (0, c))),
        scratch_shapes=[pltpu.VMEM((2 * nb, FFN_HALO + ts, FFN_LANES), F32), pltpu.VMEM((2 * nb, ts + FFN_HALO, FFN_LANES), F32),
                        pltpu.VMEM((nb, ts, FFN_LANES), F32)],
        compiler_params=_params('parallel', 'arbitrary'),
    )(up, up, d_act, w)


def _loss_head(y, target, *, name, ts=512):
    S, D = y.shape
    ts = _tile(S, ts, 8)

    def body(y_ref, t_ref, l_ref, dy_ref):
        i = pl.program_id(0)
        err = y_ref[...] - t_ref[...]
        dy_ref[...] = err * (1.0 / D)
        part = jnp.sum(jnp.sum(err * err, axis=1, keepdims=True), axis=0, keepdims=True) * (0.5 / D)

        @pl.when(i == 0)
        def _():
            l_ref[...] = part

        @pl.when(i > 0)
        def _():
            l_ref[...] += part

    row = pl.BlockSpec((ts, D), lambda i: (i, 0))
    return pl.pallas_call(
        body, name=name,
        out_shape=(jax.ShapeDtypeStruct((1, 1), F32), jax.ShapeDtypeStruct((S, D), F32)),
        grid=(S // ts,), in_specs=[row, row], out_specs=(pl.BlockSpec((1, 1), lambda i: (0, 0)), row),
        compiler_params=_params('arbitrary'),
    )(y, target)


def _pair_cols(w):
    lead, f2 = w.shape[:-1], w.shape[-1]
    return w.reshape(lead + (2, 2, f2 // 4)).swapaxes(-3, -2).reshape(lead + (f2,))


def _pad_axis(w, size, axis):
    pad = [(0, 0)] * w.ndim
    pad[axis] = (0, size - w.shape[axis])
    return jnp.pad(w, pad)


def _block_diag(pool_w):
    g = pool_w.shape[0]
    rows = [jnp.concatenate([pool_w[i] if i == j else jnp.zeros_like(pool_w[i]) for j in range(g)], axis=1) for i in range(g)]
    return jnp.concatenate(rows, axis=0)


def _small_weights(w, l):
    return dict(
        norm1_g=w['norm1_g'][l][None, :],
        b_col=_pad_axis(w['b_f'][l][:, None], FG_ROWS, 0),
        qg=jnp.tile(w['q_norm_g'][l], N_HEADS)[None, :],
        kg=jnp.tile(w['k_norm_g'][l], N_HEADS)[None, :],
        dw_b=w['conv_dw_b'][l][None, :], ln_g=w['conv_ln_g'][l][None, :], ln_b=w['conv_ln_b'][l][None, :],
        wbd=_block_diag(w['pool_w'][l]).astype(BF16),
        pool_scale=w['pool_scale'][l][None, :],
        norm2_g=w['norm2_g'][l][None, :],
    )


def _layer_fwd(x, W, p, l):
    n = lambda s: f'l{l}_{s}'
    S = x.shape[0]
    h = _rms_fwd(x, p['norm1_g'], name=n('norm1'))
    proj_a = _mm(h, W.get('w_a', h), b_lead=0, name=n('proj_a'), tn=768)
    z_raw = _mm(W.get('w_fg_t', h), h, a_lead=0, tb=True, name=n('proj_fg'))
    qkv = _qk_prep_fwd(proj_a, p['qg'], p['kg'], name=n('qk_norm'))
    f_cum = _forget_fwd(z_raw, p['b_col'], name=n('forget'))
    f3 = f_cum[:N_HEADS].reshape(N_HEADS // 2, 2, S)
    mix, att, lse = _attn_fwd(qkv, f3, name=n('attn'))
    mix = _conv_fwd(proj_a, mix, W.get('dw_w', h), p['dw_b'], p['ln_g'], p['ln_b'], W.get('pw_w', h), 0, name=n('conv'))
    mix = _pool_fwd(proj_a, mix, p['wbd'], p['pool_scale'], name=n('pool'))
    x1 = _mm(mix, W.get('w_out', mix), b_lead=0, res=x, name=n('out_proj'), tn=1024)
    h2 = _rms_fwd(x1, p['norm2_g'], name=n('norm2'))
    up = _mm(h2, W.get('w_up', mix), b_lead=0, name=n('up_proj'), tn=1408)
    act = _ffn_act_fwd(up, W.get('ffn_w', h), 0, name=n('ffn_act'))
    x2 = _mm(act, W.get('w_down', mix), b_lead=0, res=x1, name=n('down_proj'), tn=1024, tk=1408)
    saved = dict(x=x, h=h, proj_a=proj_a, z_raw=z_raw, qkv=qkv, f3=f3, att=att, lse=lse, mix=mix, x1=x1, h2=h2, up=up, act=act)
    return x2, saved


def _layer_bwd(dx2, W, p, s, l, sink):
    n = lambda t: f'l{l}_{t}'
    S = dx2.shape[0]
    g = {}
    W = W.ready

    def large(key, a, b, **kw):
        sink.put(l, key, *_mm(a, b, ta=True, copy16=True, name=n('d_' + key), **kw))

    d_act = _mm(dx2, W['w_down'], b_lead=0, tb=True, name=n('d_act'), tn=1408)
    large('w_down', s['act'], dx2, tm=1408, tn=1024)
    d_up, d_ffn_w = _ffn_act_bwd(s['up'], d_act, W['ffn_w'], 0, name=n('ffn_act_bwd'))
    g['ffn_dw_w'] = _pair_cols(d_ffn_w[:FFN_CONV_WIDTH])
    d_h2 = _mm(d_up, W['w_up'], b_lead=0, tb=True, name=n('d_h2'), tn=1024, tk=1408)
    large('w_up', s['h2'], d_up, tm=1024, tn=1408)
    dx1, dg2 = _rms_bwd(s['x1'], p['norm2_g'], d_h2, dx2, name=n('norm2_bwd'))
    g['norm2_g'] = dg2[0]
    sink.point(l, 'mid', dx1)
    d_mix = _mm(dx1, W['w_out'], b_lead=0, tb=True, name=n('d_mix'), tn=1024)
    large('w_out', s['mix'], dx1, tm=1024, tn=1024)
    dq, dk, dv, df3, dr = _attn_bwd(s['qkv'], s['f3'], s['att'], s['lse'], d_mix, name=n('attn_bwd'))
    df = _pad_axis(df3.reshape(N_HEADS, S) + dr[:, ::HEAD_DIM].T, FG_ROWS, 0)
    d_z, d_b = _forget_bwd(s['z_raw'], p['b_col'], df, name=n('forget_bwd'))
    g['b_f'] = d_b[:N_HEADS, 0]
    d_proj, d_qg, d_kg = _qk_prep_bwd(s['proj_a'], dq, dk, dv, p['qg'], p['kg'], name=n('qk_norm_bwd'))
    g['q_norm_g'] = d_qg.reshape(N_HEADS, HEAD_DIM).sum(axis=0)
    g['k_norm_g'] = d_kg.reshape(N_HEADS, HEAD_DIM).sum(axis=0)
    d_proj, d_dw_w, d_dw_b, d_ln_g, d_ln_b, d_pw = _conv_bwd(
        s['proj_a'], d_mix, d_proj, W['dw_w'], p['dw_b'], p['ln_g'], p['ln_b'], W['pw_w'], 0, name=n('conv_bwd'))
    g['conv_dw_w'], g['conv_dw_b'] = d_dw_w[:CONV_WIDTH], d_dw_b[0]
    g['conv_ln_g'], g['conv_ln_b'], g['conv_pw_w'] = d_ln_g[0], d_ln_b[0], d_pw
    d_proj, d_wbd, d_scale = _pool_bwd(s['proj_a'], d_mix, d_proj, p['wbd'], p['pool_scale'], name=n('pool_bwd'))
    g['pool_w'] = jnp.stack([d_wbd[i * POOL_GROUP:(i + 1) * POOL_GROUP, i * POOL_GROUP:(i + 1) * POOL_GROUP]
                             for i in range(len(POOL_WINDOWS))])
    g['pool_scale'] = d_scale[0]
    d_h_fg = _mm(d_z, W['w_fg_t'], b_lead=0, ta=True, name=n('d_h_fg'), tn=1024)
    d_h = _mm(d_proj, W['w_a'], b_lead=0, tb=True, res=d_h_fg, name=n('d_h'), tn=1024, tk=768)
    d_w_a = _mm(s['h'], d_proj, ta=True, name=n('d_w_a'), tm=1024, tn=768)
    sink.put(l, 'w_in', d_w_a, _mm(d_z, s['h'], name=n('d_w_fg'), tn=1024).T)
    dx, dg1 = _rms_bwd(s['x'], p['norm1_g'], d_h, dx1, name=n('norm1_bwd'))
    g['norm1_g'] = dg1[0]
    sink.point(l, 'end', dx)
    return dx, g


SMALL_GRADS = REPLICATED + ('conv_dw_w', 'conv_pw_w', 'ffn_dw_w')


def _local_step(x, target, W, w_small, sink):
    depth = w_small['norm1_g'].shape[0]
    ps, saved = [], []
    for l in range(depth):
        p = _small_weights(w_small, l)
        x, s = _layer_fwd(x, W[l], p, l)
        ps.append(p)
        saved.append(s)
    loss, dx = _loss_head(x, target, name='loss_head')
    small = [None] * depth
    for l in reversed(range(depth)):
        dx, small[l] = _layer_bwd(dx, W[l], ps[l], saved[l], l, sink)
    return loss, dx, {k: jnp.stack([small[l][k] for l in range(depth)]) for k in SMALL_GRADS}


W_IN_SHARD = D_IN // N_CHIPS
W_IN_PAD = 640
N_A_TILES = D_PROJ_A // LANES
FG_COL0 = D_QKV


def _a_tile_base(j):
    if j == N_A_TILES:
        return FG_COL0, N_HEADS
    return (j * LANES if j * LANES < FG_COL0 else j * LANES + N_HEADS), LANES


def _shift_select(rows, cols, shift, row_max, col_max):
    r = lax.broadcasted_iota(jnp.int32, (rows, cols), 0)
    c = lax.broadcasted_iota(jnp.int32, (rows, cols), 1)
    return ((r + shift == c) & (r < row_max) & (c < col_max)).astype(BF16)


def _select_w_in(raw, *, name, tm=256):
    _, D, _ = raw.shape
    tm = _tile(D, tm, 16)
    plan = []
    for j in range(N_A_TILES + 1):
        base, cmax = _a_tile_base(j)
        parts = []
        for p in range(N_CHIPS):
            delta = base - W_IN_SHARD * p
            lo, hi = max(0, delta), min(W_IN_SHARD - 1, delta + cmax - 1)
            if lo > hi:
                continue
            a0 = (lo // LANES) * LANES
            kw = min(-(-(hi + 1 - a0) // LANES) * LANES, W_IN_PAD - a0)
            parts.append((p, a0, kw, delta))
        plan.append((cmax, parts))

    def body(raw_ref, wa_ref, fg_ref):
        for j, (cmax, parts) in enumerate(plan):
            acc = None
            for p, a0, kw, delta in parts:
                sel = _shift_select(kw, LANES, a0 - delta, W_IN_SHARD - a0, cmax)
                t = _dot(raw_ref[p, :, a0:a0 + kw], sel, 1, 0)
                acc = t if acc is None else acc + t
            if j == N_A_TILES:
                fg_ref[...] = acc.astype(BF16)
            else:
                wa_ref[:, j * LANES:(j + 1) * LANES] = acc.astype(BF16)

    return pl.pallas_call(
        body, name=name,
        out_shape=(jax.ShapeDtypeStruct((D, D_PROJ_A), BF16), jax.ShapeDtypeStruct((D, LANES), BF16)),
        grid=(D // tm,),
        in_specs=[pl.BlockSpec((N_CHIPS, tm, W_IN_PAD), lambda i: (0, i, 0))],
        out_specs=(pl.BlockSpec((tm, D_PROJ_A), lambda i: (i, 0)), pl.BlockSpec((tm, LANES), lambda i: (i, 0))),
        compiler_params=_params('parallel'),
    )(raw)


def _select_w_in_grads(p_a, p_fg, *, name, tm=256):
    D = p_a.shape[0]
    tm = _tile(D, tm, 16)
    n_local = W_IN_PAD // LANES
    plan = []
    for p in range(N_CHIPS):
        for i in range(n_local):
            cmax = max(0, min(LANES, W_IN_SHARD - i * LANES))
            parts = []
            for j in range(N_A_TILES + 1):
                base, rmax = _a_tile_base(j)
                e = base - W_IN_SHARD * p - i * LANES
                if e + rmax - 1 < 0 or e > cmax - 1:
                    continue
                parts.append((j, e, rmax))
            plan.append((p, i, cmax, parts))

    def body(a_ref, fg_ref, o32_ref, o16_ref):
        terms = {}

        def src(j):
            if j not in terms:
                v = fg_ref[...] if j == N_A_TILES else a_ref[:, j * LANES:(j + 1) * LANES]
                terms[j] = _split3(v)
            return terms[j]

        for p, i, cmax, parts in plan:
            acc = jnp.zeros((tm, LANES), F32)
            for j, e, rmax in parts:
                sel = _shift_select(LANES, LANES, e, rmax, cmax)
                for term in src(j):
                    acc = acc + _dot(term, sel, 1, 0)
            o32_ref[p, :, i * LANES:(i + 1) * LANES] = acc
            o16_ref[p, :, i * LANES:(i + 1) * LANES] = acc.astype(BF16)

    out = pl.BlockSpec((N_CHIPS, tm, W_IN_PAD), lambda i: (0, i, 0))
    return pl.pallas_call(
        body, name=name,
        out_shape=(jax.ShapeDtypeStruct((N_CHIPS, D, W_IN_PAD), F32), jax.ShapeDtypeStruct((N_CHIPS, D, W_IN_PAD), BF16)),
        grid=(D // tm,),
        in_specs=[pl.BlockSpec((tm, D_PROJ_A), lambda i: (i, 0)), pl.BlockSpec((tm, LANES), lambda i: (i, 0))],
        out_specs=(out, out),
        compiler_params=_params('parallel'),
    )(p_a, p_fg)


MESH = pl.DeviceIdType.MESH
HBM_SPEC = pl.BlockSpec(memory_space=pltpu.HBM)


def _place():
    return lax.axis_index('x'), lax.axis_index('y'), lax.axis_index('c')


def _other_chips(x, y):
    return [(1 - x, y), (x, 1 - y), (1 - x, 1 - y)]


def _up_pos(q):
    return (q % 2) * 2 + q // 2


CHUNKS = {
    'w_in': ('lead', None),
    'w_up': ('cols', None),
    'w_down': ('rows', None),
    'w_out': ('rows', None),
    'conv_pw_w': ('rows', None),
    'conv_dw_w': ('lead', None),
    'ffn_dw_w': ('lead', None),
}


def _window(ref, kind, l, q):
    at = (lambda *idx: ref.at[idx]) if l is None else (lambda *idx: ref.at[(l,) + idx])
    shape = ref.shape if l is None else ref.shape[1:]
    if kind == 'lead':
        return at(q)
    if kind == 'rows':
        cs = shape[0] // N_CHIPS
        return at(pl.ds(pl.multiple_of(q * cs, 16), cs), slice(None))
    cs = shape[1] // N_CHIPS
    return at(slice(None), pl.ds(pl.multiple_of(_up_pos(q) * cs, LANES), cs))


def _place_shard(src, l, pos_arr, full_shape, kind, *, name, tm=256):
    _, m, n = src.shape
    bm = _tile(m, tm, 16) if kind != 'rows' else m

    def body(pos_ref, s_ref, o_ref):
        o_ref[...] = s_ref[...].astype(BF16)

    if kind == 'lead':
        out = pl.BlockSpec((None, bm, n), lambda i, pos: (pos[0], i, 0))
    elif kind == 'rows':
        out = pl.BlockSpec((bm, n), lambda i, pos: (pos[0], 0))
    else:
        out = pl.BlockSpec((bm, n), lambda i, pos: (i, pos[0]))
    return pl.pallas_call(
        body, name=name, out_shape=jax.ShapeDtypeStruct(full_shape, BF16),
        grid_spec=pltpu.PrefetchScalarGridSpec(
            num_scalar_prefetch=1, grid=(m // bm,),
            in_specs=[pl.BlockSpec((None, bm, n), lambda i, pos: (l, i, 0))], out_specs=out),
        compiler_params=_params('parallel'),
    )(pos_arr, src)


GATHERED = ('w_in', 'w_up', 'w_down', 'w_out', 'conv_pw_w', 'conv_dw_w', 'ffn_dw_w')
GATHER_GROUPS = ((0, ('w_in', 'conv_dw_w', 'ffn_dw_w', 'conv_pw_w')), (0, ('w_out', 'w_up', 'w_down')), (1, GATHERED))
SEM_SPEC = pl.BlockSpec(memory_space=pltpu.SEMAPHORE)
SPLIT_COPY_PARAMS = pltpu.CompilerParams(has_side_effects=pltpu.SideEffectType.DATAFLOW_SIDE_EFFECTING)


def _gather_start(bufs):
    flat = [b for group in bufs for b in group]
    nb = len(flat)

    def body(*refs):
        outs, sems = refs[nb:2 * nb], refs[2 * nb:]
        x, y, c = _place()
        pos = 0
        for g, (_, keys) in enumerate(GATHER_GROUPS):
            for i, k in enumerate(keys):
                w = _window(outs[pos], CHUNKS[k][0], None, 2 * x + y)
                pos += 1
                for j, chip in enumerate(_other_chips(x, y)):
                    pltpu.make_async_remote_copy(src_ref=w, dst_ref=w, send_sem=sems[2 * g].at[3 * i + j],
                                                 recv_sem=sems[2 * g + 1].at[3 * i + j], device_id=(*chip, c),
                                                 device_id_type=MESH).start()

    sem_shapes = [pltpu.SemaphoreType.DMA((3 * len(keys),)) for _, keys in GATHER_GROUPS for _ in range(2)]
    res = pl.pallas_call(
        body, name='gather_start',
        out_shape=tuple(jax.ShapeDtypeStruct(b.shape, b.dtype) for b in flat) + tuple(sem_shapes),
        in_specs=[HBM_SPEC] * nb, out_specs=tuple([HBM_SPEC] * nb + [SEM_SPEC] * len(sem_shapes)),
        input_output_aliases={b: b for b in range(nb)},
        compiler_params=SPLIT_COPY_PARAMS,
    )(*[pltpu.with_memory_space_constraint(b, pltpu.HBM) for b in flat])
    out_bufs, sems, pos = [], res[nb:], 0
    for group in bufs:
        out_bufs.append(list(res[pos:pos + len(group)]))
        pos += len(group)
    return out_bufs, [(sems[2 * g], sems[2 * g + 1]) for g in range(len(GATHER_GROUPS))]


def _gather_wait(g, bufs, sems, after):
    keys = GATHER_GROUPS[g][1]
    nb = len(bufs)

    def body(*refs):
        send_sems, recv_sems = refs[nb], refs[nb + 1]
        outs = refs[nb + 3:]
        x, y, c = _place()
        for i, k in enumerate(keys):
            mine = _window(outs[i], CHUNKS[k][0], None, 2 * x + y)
            for j, (cx, cy) in enumerate(_other_chips(x, y)):
                theirs = _window(outs[i], CHUNKS[k][0], None, 2 * cx + cy)
                cp = pltpu.make_async_remote_copy(src_ref=mine, dst_ref=theirs, send_sem=send_sems.at[3 * i + j],
                                                  recv_sem=recv_sems.at[3 * i + j], device_id=(cx, cy, c), device_id_type=MESH)
                cp.wait_send()
                cp.wait_recv()

    return pl.pallas_call(
        body, name=f'gather_wait_{g}',
        out_shape=tuple(jax.ShapeDtypeStruct(b.shape, b.dtype) for b in bufs),
        in_specs=[HBM_SPEC] * nb + [SEM_SPEC, SEM_SPEC, ANY_SPEC], out_specs=tuple([HBM_SPEC] * nb),
        input_output_aliases={b: b for b in range(nb)},
        compiler_params=SPLIT_COPY_PARAMS,
    )(*bufs, *sems, after)


def _rs_block(M, N):
    return (_tile(M, 256, 16), _tile(N, 2048))


def _chunk_shape(shape, kind):
    if kind == 'lead':
        return tuple(shape[1:])
    if kind == 'rows':
        return (shape[0] // N_CHIPS, shape[1])
    return (shape[0], shape[1] // N_CHIPS)


def _rs_start(tag, bufs, kinds):
    nb = len(bufs)
    lands = [lax.empty((N_CHIPS - 1,) + _chunk_shape(b.shape, k), b.dtype) for b, k in zip(bufs, kinds)]

    def body(*refs):
        src, land = refs[2 * nb:3 * nb], refs[3 * nb:4 * nb]
        send_sems, recv_sems = refs[4 * nb:]
        x, y, c = _place()
        for b in range(nb):
            for j, (cx, cy) in enumerate(_other_chips(x, y)):
                pltpu.make_async_remote_copy(
                    src_ref=_window(src[b], kinds[b], None, 2 * cx + cy), dst_ref=land[b].at[j],
                    send_sem=send_sems.at[3 * b + j], recv_sem=recv_sems.at[3 * b + j],
                    device_id=(cx, cy, c), device_id_type=MESH).start()

    sem = pltpu.SemaphoreType.DMA((3 * nb,))
    res = pl.pallas_call(
        body, name=f'rs_start_{tag}',
        out_shape=tuple(jax.ShapeDtypeStruct(b.shape, b.dtype) for b in list(bufs) + lands) + (sem, sem),
        in_specs=[HBM_SPEC] * (2 * nb), out_specs=tuple([HBM_SPEC] * (2 * nb) + [SEM_SPEC, SEM_SPEC]),
        input_output_aliases={b: b for b in range(2 * nb)},
        compiler_params=SPLIT_COPY_PARAMS,
    )(*[pltpu.with_memory_space_constraint(b, pltpu.HBM) for b in list(bufs) + lands])
    return res[:nb], res[nb:2 * nb], res[2 * nb:]


def _rs_wait(tag, bufs, lands, sems, kinds, after):
    nb = len(bufs)

    def body(*refs):
        send_sems, recv_sems = refs[2 * nb], refs[2 * nb + 1]
        src, land = refs[2 * nb + 3:3 * nb + 3], refs[3 * nb + 3:]
        x, y, c = _place()
        for b in range(nb):
            for j, (cx, cy) in enumerate(_other_chips(x, y)):
                cp = pltpu.make_async_remote_copy(
                    src_ref=_window(src[b], kinds[b], None, 2 * cx + cy), dst_ref=land[b].at[j],
                    send_sem=send_sems.at[3 * b + j], recv_sem=recv_sems.at[3 * b + j],
                    device_id=(cx, cy, c), device_id_type=MESH)
                cp.wait_send()
                cp.wait_recv()

    res = pl.pallas_call(
        body, name=f'rs_wait_{tag}',
        out_shape=tuple(jax.ShapeDtypeStruct(b.shape, b.dtype) for b in list(bufs) + list(lands)),
        in_specs=[HBM_SPEC] * (2 * nb) + [SEM_SPEC, SEM_SPEC, ANY_SPEC], out_specs=tuple([HBM_SPEC] * (2 * nb)),
        input_output_aliases={b: b for b in range(2 * nb)},
        compiler_params=SPLIT_COPY_PARAMS,
    )(*bufs, *lands, *sems, after)
    return res[nb:]


def _rs_sum(p, rb, kind, pos_arr, l, depth, buf, *, name):
    m, n = rb.shape[1:]
    bm, bn = _rs_block(m, n)
    nbm, nbn = m // bm, n // bn
    has_buf = buf is not None

    def body(q_ref, p_ref, r_ref, *rest):
        acc = p_ref[...]
        for j in range(N_CHIPS - 1):
            acc = acc + r_ref[j].astype(F32)
        rest[-1][...] = acc

    if kind == 'lead':
        p_map = lambda i, j, q: (q[0], i, j)
    elif kind == 'rows':
        p_map = lambda i, j, q: (q[0] * nbm + i, j)
    else:
        p_map = lambda i, j, q: (i, q[0] * nbn + j)
    r_spec = pl.BlockSpec((N_CHIPS - 1, bm, bn), lambda i, j, q: (0, i, j))
    p_spec = pl.BlockSpec(((None,) if kind == 'lead' else ()) + (bm, bn), p_map)
    return pl.pallas_call(
        body, name=name, out_shape=jax.ShapeDtypeStruct((depth, m, n), F32),
        grid_spec=pltpu.PrefetchScalarGridSpec(
            num_scalar_prefetch=1, grid=(nbm, nbn), in_specs=[p_spec, r_spec] + ([ANY_SPEC] if has_buf else []),
            out_specs=pl.BlockSpec((None, bm, bn), lambda i, j, q: (l, i, j))),
        input_output_aliases={3: 0} if has_buf else {},
        compiler_params=_params('parallel', 'parallel'),
    )(pos_arr, p, rb, *((buf,) if has_buf else ()))


def _swap_with_sibling(bufs):
    nb = len(bufs)

    def body(*refs):
        ins, outs = refs[:nb], refs[nb:2 * nb]
        send_sems, recv_sems = refs[2 * nb:]
        x, y, c = _place()
        copies = [pltpu.make_async_remote_copy(src_ref=ins[b], dst_ref=outs[b], send_sem=send_sems.at[b],
                                               recv_sem=recv_sems.at[b], device_id=(x, y, 1 - c), device_id_type=MESH)
                  for b in range(nb)]
        for cp in copies:
            cp.start()
        for cp in copies:
            cp.wait()

    return pl.pallas_call(
        body, name='rs_swap_sums', out_shape=tuple(jax.ShapeDtypeStruct(b.shape, b.dtype) for b in bufs),
        in_specs=[HBM_SPEC] * nb, out_specs=tuple([HBM_SPEC] * nb),
        scratch_shapes=[pltpu.SemaphoreType.DMA((nb,)), pltpu.SemaphoreType.DMA((nb,))],
    )(*bufs)


def _all_reduce_small(v):
    r = v.shape[0]

    def body(x_ref, tot_ref, all_ref, send_sems, recv_sems):
        x, y, c = _place()
        me, sibling = (x, y, c), (x, y, 1 - c)
        chips = _other_chips(x, y)

        def rows(px, py, pc):
            return all_ref.at[pl.ds((4 * px + 2 * py + pc) * r, r), :]

        def copy(k, block, to, src=None):
            return pltpu.make_async_remote_copy(
                src_ref=rows(*block) if src is None else src, dst_ref=rows(*block),
                send_sem=send_sems.at[k], recv_sem=recv_sems.at[k], device_id=to, device_id_type=MESH)

        rows(*me)[...] = x_ref[...]
        first = [copy(0, me, sibling, src=x_ref)]
        first += [copy(1 + j, me, (*chip, c), src=x_ref) for j, chip in enumerate(chips)]
        for cp in first:
            cp.start()
        passed = [copy(4 + j, (*chip, c), sibling) for j, chip in enumerate(chips)]
        for j, chip in enumerate(chips):
            copy(1 + j, (*chip, c), me).wait_recv()
            passed[j].start()
        copy(0, sibling, me).wait_recv()
        for j, chip in enumerate(chips):
            copy(4 + j, (*chip, 1 - c), me).wait_recv()
        for cp in first + passed:
            cp.wait_send()
        acc = all_ref[0:r, :]
        for d in range(1, N_DEV):
            acc = acc + all_ref[d * r:(d + 1) * r, :]
        tot_ref[...] = acc

    return pl.pallas_call(
        body, name='all_reduce_small', out_shape=jax.ShapeDtypeStruct((r, LANES), F32),
        in_specs=[pl.BlockSpec(memory_space=pltpu.VMEM)], out_specs=pl.BlockSpec(memory_space=pltpu.VMEM),
        scratch_shapes=[pltpu.VMEM((N_DEV * r, LANES), F32), pltpu.SemaphoreType.DMA((7,)), pltpu.SemaphoreType.DMA((7,))],
    )(v)


def _adamw(w, g, m, v, *, name, g2=None, ts=256):
    R, C = w.shape
    Cg = g.shape[1]
    ts = _tile(R, ts, 8)
    c1 = 1.0 - ADAM_B1 ** ADAM_STEP
    c2 = 1.0 - ADAM_B2 ** ADAM_STEP
    two = g2 is not None

    def body(w_ref, g_ref, *rest):
        m_ref, v_ref, go_ref, d_ref, nm_ref, nv_ref = rest[two:]
        gv = g_ref[:, 0:C]
        if two:
            gv = gv + rest[0][:, 0:C]
        nm = ADAM_B1 * m_ref[...] + (1.0 - ADAM_B1) * gv
        nv = ADAM_B2 * v_ref[...] + (1.0 - ADAM_B2) * (gv * gv)
        d_ref[...] = -ADAM_LR * ((nm / c1) / (jnp.sqrt(nv / c2) + ADAM_EPS) + ADAM_WD * w_ref[...])
        go_ref[...] = gv
        nm_ref[...] = nm
        nv_ref[...] = nv

    blk = pl.BlockSpec((ts, C), lambda i: (i, 0))
    gblk = pl.BlockSpec((ts, Cg), lambda i: (i, 0))
    shape = jax.ShapeDtypeStruct((R, C), F32)
    return pl.pallas_call(body, name=name, out_shape=(shape, shape, shape, shape), grid=(R // ts,),
                          in_specs=[blk, gblk] + ([gblk] if two else []) + [blk, blk], out_specs=(blk, blk, blk, blk),
                          compiler_params=_params('parallel'))(w, g, *((g2,) if two else ()), m, v)


def _pack_rows(parts, row_unit):
    flat = jnp.concatenate(parts)
    flat = _pad_axis(flat, -(-flat.shape[0] // (row_unit * LANES)) * row_unit * LANES, 0)
    return flat.reshape(-1, LANES)


def _as_2d(a):
    return a.reshape(-1, a.shape[-1])


def _mesh_place():
    cx, cy, cc = _place()
    chip = 2 * cx + cy
    as_arr = lambda v: jnp.reshape(v, (1,)).astype(jnp.int32)
    return chip, as_arr(cc), as_arr(chip), as_arr(_up_pos(chip))


class _LayerWeights:
    def __init__(self, groups):
        self.groups = groups
        self.ready = {}

    def get(self, name, after):
        if name not in self.ready:
            for names, wait in self.groups:
                if name in names:
                    self.ready.update({k: v[None] for k, v in wait(after).items()})
        return self.ready[name]


def _gather_full(w, place):
    chip, _, chip_arr, up_pos_arr = place
    L, D = w['w_in'].shape[:2]
    w_in_pad = _pad_axis(w['w_in'], W_IN_PAD, 2)

    def placed(k, l):
        if k == 'w_in':
            return _place_shard(w_in_pad, l, chip_arr, (N_CHIPS, D, W_IN_PAD), 'lead', name=f'place_w_in_{l}')
        if k == 'w_up':
            return _place_shard(w[k], l, up_pos_arr, (w[k].shape[1], N_CHIPS * w[k].shape[2]), 'cols', name=f'place_w_up_{l}')
        if k in ('conv_dw_w', 'ffn_dw_w'):
            return lax.dynamic_update_slice_in_dim(jnp.zeros((N_CHIPS,) + w[k].shape[1:], F32), w[k][l][None], chip, axis=0)
        return _place_shard(w[k], l, chip_arr, (N_CHIPS * w[k].shape[1], w[k].shape[2]), 'rows', name=f'place_{k}_{l}')

    bufs, sems = _gather_start([[placed(k, l) for k in keys] for l, keys in GATHER_GROUPS])
    unchunk = lambda a: jnp.moveaxis(a, 0, 1).reshape(a.shape[1], -1)

    def waiter(g):
        l, keys = GATHER_GROUPS[g]

        def wait(after):
            full = dict(zip(keys, _gather_wait(g, bufs[g], sems[g], after)))
            out = {}
            if 'w_in' in full:
                out['w_a'], w_fg = _select_w_in(full['w_in'], name=f'select_w_in_{l}')
                out['w_fg_t'] = w_fg.T
            if 'conv_dw_w' in full:
                out['dw_w'] = _pad_axis(unchunk(full['conv_dw_w']), CONV_HALO, 0)
            if 'ffn_dw_w' in full:
                out['ffn_w'] = _pad_axis(_pair_cols(unchunk(full['ffn_dw_w'])), FFN_HALO, 0)
            if 'conv_pw_w' in full:
                out['pw_w'] = full['conv_pw_w']
            out.update({k: full[k] for k in ('w_up', 'w_down', 'w_out') if k in full})
            return out

        names = {'w_in': ('w_a', 'w_fg_t'), 'conv_dw_w': ('dw_w',), 'ffn_dw_w': ('ffn_w',), 'conv_pw_w': ('pw_w',)}
        return tuple(n for k in keys for n in names.get(k, (k,))), wait

    return [_LayerWeights([waiter(g) for g in range(len(GATHER_GROUPS)) if GATHER_GROUPS[g][0] == l]) for l in range(L)]


RS_WIRE = ('w_in', 'w_up', 'w_down', 'w_out')
RS_GROUPS = (('ffn', ('w_down', 'w_up')), ('mix', ('w_out', 'w_in')))


class _GradReducer:
    def __init__(self, place, depth):
        _, _, self.chip_arr, self.up_pos_arr = place
        self.depth = depth
        self.got = {}
        self.flying = {}
        self.sums = {}

    def put(self, l, key, g32, g16):
        if key == 'w_in':
            g32, g16 = _select_w_in_grads(g32, g16, name=f'l{l}_select_w_in_grads')
        self.got[(l, key)] = (g32, g16)
        for tag, keys in RS_GROUPS:
            if key == keys[-1]:
                kinds = [CHUNKS[k][0] for k in keys]
                bufs, lands, sems = _rs_start(f'l{l}_{tag}', [self.got[(l, k)][1] for k in keys], kinds)
                self.flying[(l, tag)] = (bufs, lands, sems, kinds)

    def point(self, l, where, after):
        if where == 'mid':
            self._land(l + 1, 'mix', after)
        else:
            self._land(l, 'ffn', after)

    def _land(self, l, tag, after):
        if (l, tag) not in self.flying:
            return
        bufs, lands, sems, kinds = self.flying.pop((l, tag))
        lands = _rs_wait(f'l{l}_{tag}', bufs, lands, sems, kinds, after)
        for k, rb, kind in zip(dict(RS_GROUPS)[tag], lands, kinds):
            pos = self.up_pos_arr if kind == 'cols' else self.chip_arr
            self.sums[k] = _rs_sum(self.got.pop((l, k))[0], rb, kind, pos, l, self.depth, self.sums.get(k), name=f'l{l}_rs_sum_{k}')

    def finish(self, after):
        for l, tag in list(self.flying):
            self._land(l, tag, after)
        mine = [self.sums[k] for k in RS_WIRE]
        return {k: pair for k, pair in zip(RS_WIRE, zip(mine, _swap_with_sibling(mine)))}


def kernel(x, norm1_g, w_in, b_f, q_norm_g, k_norm_g, conv_dw_w, conv_dw_b, conv_ln_g, conv_ln_b, conv_pw_w, pool_w, pool_scale, w_out, norm2_g, w_up, ffn_dw_w, w_down, loss_target, m_norm1_g, m_w_in, m_b_f, m_q_norm_g, m_k_norm_g, m_conv_dw_w, m_conv_dw_b, m_conv_ln_g, m_conv_ln_b, m_conv_pw_w, m_pool_w, m_pool_scale, m_w_out, m_norm2_g, m_w_up, m_ffn_dw_w, m_w_down, v_norm1_g, v_w_in, v_b_f, v_q_norm_g, v_k_norm_g, v_conv_dw_w, v_conv_dw_b, v_conv_ln_g, v_conv_ln_b, v_conv_pw_w, v_pool_w, v_pool_scale, v_w_out, v_norm2_g, v_w_up, v_ffn_dw_w, v_w_down):
    given = dict(locals())
    w = {k: given[k] for k in WEIGHTS}
    mom_m = {k: given['m_' + k] for k in WEIGHTS}
    mom_v = {k: given['v_' + k] for k in WEIGHTS}
    place = _mesh_place()
    chip = place[0]
    W = _gather_full(w, place)

    reducer = _GradReducer(place, norm1_g.shape[0])
    loss_part, grad_x, g_small = _local_step(x[0], loss_target[0], W, {k: w[k] for k in REPLICATED}, reducer)
    loss = lax.psum(loss_part[0, 0], ('x', 'y', 'c'))
    sums = reducer.finish(grad_x)

    small = _pack_rows([g_small[k].reshape(-1) for k in SMALL_GRADS], 8)
    small_sum = _all_reduce_small(small)

    g_sum, delta, new_m, new_v = {}, {}, {}, {}
    for k in RS_WIRE:
        outs = _adamw(_as_2d(w[k]), _as_2d(sums[k][0]), _as_2d(mom_m[k]), _as_2d(mom_v[k]), g2=_as_2d(sums[k][1]), name='adamw_' + k)
        g_sum[k], delta[k], new_m[k], new_v[k] = [o.reshape(w[k].shape) for o in outs]
    off = 0
    small_full = {}
    for k in SMALL_GRADS:
        small_full[k] = small_sum.reshape(-1)[off:off + g_small[k].size].reshape(g_small[k].shape)
        off += g_small[k].size
    small_g = {k: small_full[k] for k in REPLICATED}
    small_g['conv_dw_w'] = lax.dynamic_slice_in_dim(small_full['conv_dw_w'], chip * w['conv_dw_w'].shape[2], w['conv_dw_w'].shape[2], axis=2)
    small_g['conv_pw_w'] = lax.dynamic_slice_in_dim(small_full['conv_pw_w'], chip * w['conv_pw_w'].shape[1], w['conv_pw_w'].shape[1], axis=1)
    small_g['ffn_dw_w'] = lax.dynamic_slice_in_dim(small_full['ffn_dw_w'], chip * w['ffn_dw_w'].shape[2], w['ffn_dw_w'].shape[2], axis=2)
    pack_small = lambda t: _pack_rows([t[k].reshape(-1) for k in SMALL_GRADS], 8)
    outs = _adamw(pack_small(w), pack_small(small_g), pack_small(mom_m), pack_small(mom_v), name='adamw_small')
    off = 0
    for k in SMALL_GRADS:
        pieces = [o.reshape(-1)[off:off + w[k].size].reshape(w[k].shape) for o in outs]
        g_sum[k], delta[k], new_m[k], new_v[k] = pieces
        off += w[k].size

    return (loss, grad_x[None], *[g_sum[k] for k in WEIGHTS], *[delta[k] for k in WEIGHTS],
            *[new_m[k] for k in WEIGHTS], *[new_v[k] for k in WEIGHTS])
```

```python
import functools

import jax
import jax.numpy as jnp
from jax import lax
from jax.experimental import pallas as pl
from jax.experimental.pallas import tpu as pltpu

F32 = jnp.float32
BF16 = jnp.bfloat16

N_HEADS = 8
HEAD_DIM = 64
D_ATT = N_HEADS * HEAD_DIM
D_CONV = 256
D_POOL = 256
D_MIX = D_ATT + D_CONV + D_POOL
D_QKV = 3 * D_ATT
D_PROJ_A = D_QKV + 2 * D_CONV + D_POOL
D_IN = D_PROJ_A + N_HEADS
FG_ROWS = 128
CONV_WIDTH = 31
CONV_HALO = 32
POOL_WINDOWS = (2, 4, 8, 16)
POOL_GROUP = 64
POOL_HALO = 16
FFN_CONV_WIDTH = 3
FFN_HALO = 8
ATT_SCALE = HEAD_DIM ** -0.5
EPS = 1e-6
NEG = -1e30
LANES = 128

ADAM_LR = 0.001
ADAM_B1 = 0.9
ADAM_B2 = 0.999
ADAM_EPS = 1e-08
ADAM_WD = 0.01
ADAM_STEP = 10

N_CHIPS = 4
N_DEV = 8
VMEM_LIMIT_BYTES = 56 * 1024 * 1024

REPLICATED = ('norm1_g', 'b_f', 'q_norm_g', 'k_norm_g', 'conv_dw_b', 'conv_ln_g', 'conv_ln_b',
              'pool_w', 'pool_scale', 'norm2_g')
WEIGHTS = ('norm1_g', 'w_in', 'b_f', 'q_norm_g', 'k_norm_g', 'conv_dw_w', 'conv_dw_b', 'conv_ln_g',
           'conv_ln_b', 'conv_pw_w', 'pool_w', 'pool_scale', 'w_out', 'norm2_g', 'w_up', 'ffn_dw_w', 'w_down')


def _tile(dim, pref, unit=LANES):
    if dim <= pref:
        return dim
    t = (pref // unit) * unit
    while t >= unit:
        if dim % t == 0:
            return t
        t -= unit
    raise ValueError(f'no tile for {dim} (preferred {pref})')


def _params(*sem):
    return pltpu.CompilerParams(dimension_semantics=sem, vmem_limit_bytes=VMEM_LIMIT_BYTES)


def _sigmoid(x):
    return 1.0 / (1.0 + jnp.exp(-x))


def _dot(a, b, ca, cb):
    return lax.dot_general(a, b, (((ca,), (cb,)), ((), ())), preferred_element_type=F32)


def _split3(y):
    y1 = y.astype(BF16)
    r1 = y - y1.astype(F32)
    y2 = r1.astype(BF16)
    y3 = (r1 - y2.astype(F32)).astype(BF16)
    return y1, y2, y3


def _dot3(y, e, ca=1, cb=0):
    y1, y2, y3 = _split3(y)
    return _dot(y1, e, ca, cb) + _dot(y2, e, ca, cb) + _dot(y3, e, ca, cb)


def _lead(spec_shape, imap, lead):
    if lead is None:
        return pl.BlockSpec(spec_shape, imap)
    return pl.BlockSpec((None,) + spec_shape, lambda *g: (lead,) + imap(*g))


ANY_SPEC = pl.BlockSpec(memory_space=pl.ANY)


def _mm(a, b, *, name, ta=False, tb=False, res=None, out_dtype=F32, tm=512, tn=512, tk=1024,
        a_lead=None, b_lead=None, copy16=False, after=None):
    a2, b2 = a.shape[-2:], b.shape[-2:]
    K, M = a2 if ta else a2[::-1]
    N, Kb = b2 if tb else b2[::-1]
    assert K == Kb, (a.shape, b.shape)
    tm, tn, tk = _tile(M, tm), _tile(N, tn), _tile(K, tk)
    nk = K // tk
    ca = 0 if ta else 1
    cb = 1 if tb else 0
    has_res = res is not None
    n_in = 2 + has_res + (after is not None)
    n_out = 1 + copy16

    def body(*refs):
        a_ref, b_ref = refs[:2]
        r_ref = refs[2] if has_res else None
        o_refs = refs[n_in:n_in + n_out]
        scratch = refs[n_in + n_out:]

        def write(r):
            if has_res:
                r = r + r_ref[...]
            o_refs[0][...] = r.astype(out_dtype)
            if copy16:
                o_refs[1][...] = r.astype(BF16)

        p = _dot(a_ref[...].astype(BF16), b_ref[...].astype(BF16), ca, cb)
        if nk == 1:
            write(p)
        else:
            acc = scratch[0]
            k = pl.program_id(2)

            @pl.when(k == 0)
            def _():
                acc[...] = p

            @pl.when(k > 0)
            def _():
                acc[...] += p

            @pl.when(k == nk - 1)
            def _():
                write(acc[...])

    a_spec = _lead((tk, tm), lambda i, j, k: (k, i), a_lead) if ta else _lead((tm, tk), lambda i, j, k: (i, k), a_lead)
    b_spec = _lead((tn, tk), lambda i, j, k: (j, k), b_lead) if tb else _lead((tk, tn), lambda i, j, k: (k, j), b_lead)
    o_spec = pl.BlockSpec((tm, tn), lambda i, j, k: (i, j))
    in_specs = [a_spec, b_spec] + ([o_spec] if has_res else []) + ([ANY_SPEC] if after is not None else [])
    args = (a, b) + ((res,) if has_res else ()) + ((after,) if after is not None else ())
    out_shape = [jax.ShapeDtypeStruct((M, N), out_dtype)] + ([jax.ShapeDtypeStruct((M, N), BF16)] if copy16 else [])
    out = pl.pallas_call(
        body, name=name,
        out_shape=tuple(out_shape),
        grid=(M // tm, N // tn, nk),
        in_specs=in_specs, out_specs=tuple([o_spec] * n_out),
        scratch_shapes=[pltpu.VMEM((tm, tn), F32)] if nk > 1 else [],
        compiler_params=_params('parallel', 'parallel', 'arbitrary'),
    )(*args)
    return out if copy16 else out[0]


def _rms_fwd(x, g, *, name, ts=512):
    S, D = x.shape
    ts = _tile(S, ts, 8)

    def body(x_ref, g_ref, o_ref):
        xv = x_ref[...]
        r = lax.rsqrt(jnp.mean(xv * xv, axis=-1, keepdims=True) + EPS)
        o_ref[...] = (xv * r * g_ref[...]).astype(BF16)

    return pl.pallas_call(
        body, name=name, out_shape=jax.ShapeDtypeStruct((S, D), BF16), grid=(S // ts,),
        in_specs=[pl.BlockSpec((ts, D), lambda i: (i, 0)), pl.BlockSpec((1, D), lambda i: (0, 0))],
        out_specs=pl.BlockSpec((ts, D), lambda i: (i, 0)),
        compiler_params=_params('parallel'),
    )(x, g)


def _rms_bwd(x, g, dh, dres, *, name, ts=512):
    S, D = x.shape
    ts = _tile(S, ts, 8)

    def body(x_ref, g_ref, dh_ref, dr_ref, dx_ref, dg_ref):
        i = pl.program_id(0)
        xv = x_ref[...]
        r = lax.rsqrt(jnp.mean(xv * xv, axis=-1, keepdims=True) + EPS)
        y = xv * r
        dh_v = dh_ref[...]
        dy = dh_v * g_ref[...]
        dx_ref[...] = dr_ref[...] + r * (dy - y * jnp.mean(dy * y, axis=-1, keepdims=True))
        part = jnp.sum(dh_v * y, axis=0, keepdims=True)

        @pl.when(i == 0)
        def _():
            dg_ref[...] = part

        @pl.when(i > 0)
        def _():
            dg_ref[...] += part

    row = pl.BlockSpec((ts, D), lambda i: (i, 0))
    vec = pl.BlockSpec((1, D), lambda i: (0, 0))
    return pl.pallas_call(
        body, name=name,
        out_shape=(jax.ShapeDtypeStruct((S, D), F32), jax.ShapeDtypeStruct((1, D), F32)),
        grid=(S // ts,), in_specs=[row, vec, row, row], out_specs=(row, vec),
        compiler_params=_params('arbitrary'),
    )(x, g, dh, dres)


def _group_ones():
    i = lax.broadcasted_iota(jnp.int32, (D_ATT, D_ATT), 0) // HEAD_DIM
    j = lax.broadcasted_iota(jnp.int32, (D_ATT, D_ATT), 1) // HEAD_DIM
    return (i == j).astype(BF16)


def _qk_prep_fwd(proj_a, qg, kg, *, name, ts=512):
    S = proj_a.shape[0]
    ts = _tile(S, ts, 16)

    def body(q_ref, k_ref, v_ref, qg_ref, kg_ref, e_ref, o_ref):
        e = e_ref[...]

        def norm(xv, gain):
            ms = _dot3(xv * xv, e) * (1.0 / HEAD_DIM)
            return xv * lax.rsqrt(ms + EPS) * gain

        o_ref[:, 0:D_ATT] = (norm(q_ref[...], qg_ref[...]) * ATT_SCALE).astype(BF16)
        o_ref[:, D_ATT:2 * D_ATT] = norm(k_ref[...], kg_ref[...]).astype(BF16)
        o_ref[:, 2 * D_ATT:3 * D_ATT] = v_ref[...].astype(BF16)

    col = lambda c: pl.BlockSpec((ts, D_ATT), lambda i: (i, c))
    vec = pl.BlockSpec((1, D_ATT), lambda i: (0, 0))
    return pl.pallas_call(
        body, name=name, out_shape=jax.ShapeDtypeStruct((S, D_QKV), BF16), grid=(S // ts,),
        in_specs=[col(0), col(1), col(2), vec, vec, pl.BlockSpec((D_ATT, D_ATT), lambda i: (0, 0))],
        out_specs=pl.BlockSpec((ts, D_QKV), lambda i: (i, 0)),
        compiler_params=_params('parallel'),
    )(proj_a, proj_a, proj_a, qg, kg, _group_ones())


def _qk_prep_bwd(proj_a, dq, dk, dv, qg, kg, *, name, ts=512):
    S = proj_a.shape[0]
    ts = _tile(S, ts, 16)

    def body(q_ref, k_ref, dq_ref, dk_ref, dv_ref, qg_ref, kg_ref, e_ref, o_ref, dqg_ref, dkg_ref):
        i = pl.program_id(0)
        e = e_ref[...]

        def norm_bwd(xv, dn, gain, scale):
            ms = _dot3(xv * xv, e) * (1.0 / HEAD_DIM)
            r = lax.rsqrt(ms + EPS)
            y = xv * r
            dy = dn * (gain * scale)
            mean = _dot3(dy * y, e) * (1.0 / HEAD_DIM)
            return r * (dy - y * mean), jnp.sum(dn * y, axis=0, keepdims=True) * scale

        dq_raw, dqg = norm_bwd(q_ref[...], dq_ref[...], qg_ref[...], ATT_SCALE)
        dk_raw, dkg = norm_bwd(k_ref[...], dk_ref[...], kg_ref[...], 1.0)
        o_ref[:, 0:D_ATT] = dq_raw.astype(BF16)
        o_ref[:, D_ATT:2 * D_ATT] = dk_raw.astype(BF16)
        o_ref[:, 2 * D_ATT:3 * D_ATT] = dv_ref[...].astype(BF16)

        @pl.when(i == 0)
        def _():
            dqg_ref[...] = dqg
            dkg_ref[...] = dkg

        @pl.when(i > 0)
        def _():
            dqg_ref[...] += dqg
            dkg_ref[...] += dkg

    col = lambda c: pl.BlockSpec((ts, D_ATT), lambda i: (i, c))
    vec = pl.BlockSpec((1, D_ATT), lambda i: (0, 0))
    return pl.pallas_call(
        body, name=name,
        out_shape=(jax.ShapeDtypeStruct((S, D_PROJ_A), BF16), jax.ShapeDtypeStruct((1, D_ATT), F32),
                   jax.ShapeDtypeStruct((1, D_ATT), F32)),
        grid=(S // ts,),
        in_specs=[col(0), col(1), col(0), col(0), col(0), vec, vec, pl.BlockSpec((D_ATT, D_ATT), lambda i: (0, 0))],
        out_specs=(pl.BlockSpec((ts, D_QKV), lambda i: (i, 0)), vec, vec),
        compiler_params=_params('arbitrary'),
    )(proj_a, proj_a, dq, dk, dv, qg, kg, _group_ones())


def _tri_ones(upper):
    i = lax.broadcasted_iota(jnp.int32, (LANES, LANES), 0)
    j = lax.broadcasted_iota(jnp.int32, (LANES, LANES), 1)
    return ((i <= j) if upper else (i >= j)).astype(BF16)


def _forget_fwd(z_raw, b_col, *, name):
    R, S = z_raw.shape
    nb = S // LANES

    def body(z_ref, b_ref, u_ref, f_ref):
        u = u_ref[...]
        carry = jnp.zeros((R, 1), F32)
        for j in range(nb):
            z = z_ref[:, j * LANES:(j + 1) * LANES] + b_ref[...]
            logf = jnp.minimum(z, 0.0) - jnp.log(1.0 + jnp.exp(-jnp.abs(z)))
            f_ref[:, j * LANES:(j + 1) * LANES] = _dot3(logf, u) + carry
            carry = carry + jnp.sum(logf, axis=1, keepdims=True)

    return pl.pallas_call(
        body, name=name, out_shape=jax.ShapeDtypeStruct((R, S), F32),
        compiler_params=pltpu.CompilerParams(vmem_limit_bytes=VMEM_LIMIT_BYTES),
    )(z_raw, b_col, _tri_ones(True))


def _forget_bwd(z_raw, b_col, df, *, name):
    R, S = z_raw.shape
    nb = S // LANES

    def body(z_ref, b_ref, df_ref, l_ref, dz_ref, db_ref):
        low = l_ref[...]
        carry = jnp.zeros((R, 1), F32)
        db = jnp.zeros((R, 1), F32)
        for j in reversed(range(nb)):
            d = df_ref[:, j * LANES:(j + 1) * LANES]
            dlogf = _dot3(d, low) + carry
            carry = carry + jnp.sum(d, axis=1, keepdims=True)
            z = z_ref[:, j * LANES:(j + 1) * LANES] + b_ref[...]
            dz = dlogf * _sigmoid(-z)
            dz_ref[:, j * LANES:(j + 1) * LANES] = dz
            db = db + jnp.sum(dz, axis=1, keepdims=True)
        db_ref[...] = db

    return pl.pallas_call(
        body, name=name,
        out_shape=(jax.ShapeDtypeStruct((R, S), F32), jax.ShapeDtypeStruct((R, 1), F32)),
        compiler_params=pltpu.CompilerParams(vmem_limit_bytes=VMEM_LIMIT_BYTES),
    )(z_raw, b_col, df, _tri_ones(False))


def _head_mask(hh):
    lane = lax.broadcasted_iota(jnp.int32, (1, LANES), 1)
    return (lane // HEAD_DIM) == hh


def _causal(s, qi, ki, t):
    rows = qi * t + lax.broadcasted_iota(jnp.int32, (t, t), 0)
    cols = ki * t + lax.broadcasted_iota(jnp.int32, (t, t), 1)
    return jnp.where(cols <= rows, s, NEG)


def _attn_fwd(qkv, f3, *, name, t=512):
    S = qkv.shape[0]
    t = _tile(S, t)
    n = S // t
    npair = N_HEADS // 2

    def body(q_ref, k_ref, v_ref, f_ref, mix_ref, o_ref, lse_ref, m_s, l_s, acc_s):
        qi, ki = pl.program_id(1), pl.program_id(2)

        @pl.when(ki == 0)
        def _():
            m_s[...] = jnp.full(m_s.shape, NEG, F32)
            l_s[...] = jnp.zeros(l_s.shape, F32)
            acc_s[...] = jnp.zeros(acc_s.shape, F32)

        @pl.when(ki <= qi)
        def _():
            q, k, v = q_ref[...], k_ref[...], v_ref[...]
            for hh in range(2):
                msk = _head_mask(hh)
                qm = jnp.where(msk, q, jnp.zeros_like(q))
                vm = jnp.where(msk, v, jnp.zeros_like(v))
                s = _causal(_dot(qm, k, 1, 1) - f_ref[0, hh:hh + 1, :], qi, ki, t)
                m_prev = m_s[hh]
                m_new = jnp.maximum(m_prev, jnp.max(s, axis=1, keepdims=True))
                alpha = jnp.exp(m_prev - m_new)
                p = jnp.exp(s - m_new)
                l_s[hh] = alpha * l_s[hh] + jnp.sum(p, axis=1, keepdims=True)
                acc_s[hh] = alpha * acc_s[hh] + _dot(p.astype(BF16), vm, 1, 0)
                m_s[hh] = m_new

        @pl.when(ki == qi)
        def _():
            o = acc_s[0] / l_s[0] + acc_s[1] / l_s[1]
            o_ref[...] = o
            mix_ref[...] = o.astype(BF16)
            lse0 = m_s[0] + jnp.log(l_s[0])
            lse1 = m_s[1] + jnp.log(l_s[1])
            lse_ref[...] = jnp.where(_head_mask(0), lse0, lse1)

    out = pl.BlockSpec((t, LANES), lambda h, i, j: (i, h))
    return pl.pallas_call(
        body, name=name,
        out_shape=(jax.ShapeDtypeStruct((S, D_MIX), BF16), jax.ShapeDtypeStruct((S, D_ATT), F32),
                   jax.ShapeDtypeStruct((S, D_ATT), F32)),
        grid=(npair, n, n),
        in_specs=[pl.BlockSpec((t, LANES), lambda h, i, j: (i, h)),
                  pl.BlockSpec((t, LANES), lambda h, i, j: (jnp.minimum(i, j), npair + h)),
                  pl.BlockSpec((t, LANES), lambda h, i, j: (jnp.minimum(i, j), 2 * npair + h)),
                  pl.BlockSpec((1, 2, t), lambda h, i, j: (h, 0, jnp.minimum(i, j)))],
        out_specs=(out, out, out),
        scratch_shapes=[pltpu.VMEM((2, t, 1), F32), pltpu.VMEM((2, t, 1), F32), pltpu.VMEM((2, t, LANES), F32)],
        compiler_params=_params('parallel', 'parallel', 'arbitrary'),
    )(qkv, qkv, qkv, f3)


def _attn_bwd(qkv, f3, att, lse, d_mix, *, name, t=512):
    S = qkv.shape[0]
    t = _tile(S, t)
    n = S // t
    npair = N_HEADS // 2

    def body(q_ref, k_ref, v_ref, f_ref, o_ref, lse_ref, do_ref, dq_ref, dk_ref, dv_ref, df_ref, dr_ref, dk_s, dv_s, df_s):
        ki, qi = pl.program_id(1), pl.program_id(2)

        @pl.when(qi == ki)
        def _():
            dk_s[...] = jnp.zeros(dk_s.shape, F32)
            dv_s[...] = jnp.zeros(dv_s.shape, F32)
            df_s[...] = jnp.zeros(df_s.shape, F32)

        @pl.when(qi >= ki)
        def _():
            q, k, v = q_ref[...], k_ref[...], v_ref[...]
            do, o, lse = do_ref[...], o_ref[...], lse_ref[...]
            dq_blk = jnp.zeros((t, LANES), F32)
            dr_blk = jnp.zeros((t, LANES), F32)
            for hh in range(2):
                msk = _head_mask(hh)
                qm = jnp.where(msk, q, jnp.zeros_like(q))
                km = jnp.where(msk, k, jnp.zeros_like(k))
                dom = jnp.where(msk, do, 0.0).astype(BF16)
                s = _causal(_dot(qm, k, 1, 1) - f_ref[0, hh:hh + 1, :], qi, ki, t)
                lse_h = jnp.max(jnp.where(msk, lse, NEG), axis=1, keepdims=True)
                p = jnp.exp(s - lse_h)
                dp = _dot(dom, v, 1, 1)
                delta = jnp.sum(dom.astype(F32) * o, axis=1, keepdims=True)
                ds = p * (dp - delta)
                dsb = ds.astype(BF16)
                dv_s[...] += _dot(p.astype(BF16), dom, 0, 0)
                dk_s[...] += _dot(dsb, qm, 0, 0)
                dq_blk = dq_blk + _dot(dsb, km, 1, 0)
                df_s[hh] -= jnp.sum(ds, axis=0, keepdims=True)
                dr_blk = dr_blk + jnp.where(msk, jnp.sum(ds, axis=1, keepdims=True), 0.0)
            rows = pl.ds(pl.multiple_of(qi * t, t), t)

            @pl.when(ki == 0)
            def _():
                dq_ref[rows, :] = dq_blk
                dr_ref[rows, :] = dr_blk

            @pl.when(ki > 0)
            def _():
                dq_ref[rows, :] += dq_blk
                dr_ref[rows, :] += dr_blk

        @pl.when(qi == n - 1)
        def _():
            dk_ref[...] = dk_s[...]
            dv_ref[...] = dv_s[...]
            df_ref[0, 0:1, :] = df_s[0]
            df_ref[0, 1:2, :] = df_s[1]

    qrow = lambda h, j, i: (jnp.maximum(i, j), h)
    return pl.pallas_call(
        body, name=name,
        out_shape=(jax.ShapeDtypeStruct((S, D_ATT), F32), jax.ShapeDtypeStruct((S, D_ATT), F32),
                   jax.ShapeDtypeStruct((S, D_ATT), F32), jax.ShapeDtypeStruct((npair, 2, S), F32),
                   jax.ShapeDtypeStruct((S, D_ATT), F32)),
        grid=(npair, n, n),
        in_specs=[pl.BlockSpec((t, LANES), qrow),
                  pl.BlockSpec((t, LANES), lambda h, j, i: (j, npair + h)),
                  pl.BlockSpec((t, LANES), lambda h, j, i: (j, 2 * npair + h)),
                  pl.BlockSpec((1, 2, t), lambda h, j, i: (h, 0, j)),
                  pl.BlockSpec((t, LANES), qrow),
                  pl.BlockSpec((t, LANES), qrow),
                  pl.BlockSpec((t, LANES), qrow)],
        out_specs=(pl.BlockSpec((S, LANES), lambda h, j, i: (0, h)),
                   pl.BlockSpec((t, LANES), lambda h, j, i: (j, h)),
                   pl.BlockSpec((t, LANES), lambda h, j, i: (j, h)),
                   pl.BlockSpec((1, 2, t), lambda h, j, i: (h, 0, j)),
                   pl.BlockSpec((S, LANES), lambda h, j, i: (0, h))),
        scratch_shapes=[pltpu.VMEM((t, LANES), F32), pltpu.VMEM((t, LANES), F32), pltpu.VMEM((2, 1, t), F32)],
        compiler_params=_params('parallel', 'arbitrary', 'arbitrary'),
    )(qkv, qkv, qkv, f3, att, lse, d_mix)


A_COL = D_QKV // D_CONV
B_COL = A_COL + 1
P_COL = B_COL + 1


def _layer_norm_stats(c):
    mu = jnp.mean(c, axis=-1, keepdims=True)
    xc = c - mu
    rstd = lax.rsqrt(jnp.mean(xc * xc, axis=-1, keepdims=True) + EPS)
    return xc * rstd, rstd


def _glu_into(buf, a_ref, b_ref, ah_ref, bh_ref, first, ts):
    halo = ah_ref[...] * _sigmoid(bh_ref[...])
    buf[0:CONV_HALO, :] = jnp.where(first, 0.0, halo)
    buf[CONV_HALO:CONV_HALO + ts, :] = a_ref[...] * _sigmoid(b_ref[...])


def _dwconv(buf, w_ref, ts):
    off = CONV_HALO - (CONV_WIDTH - 1)
    acc = w_ref[0:1, :] * buf[pl.ds(off, ts), :]
    for k in range(1, CONV_WIDTH):
        acc = acc + w_ref[k:k + 1, :] * buf[pl.ds(off + k, ts), :]
    return acc


def _conv_specs(ts, tmap):
    hb = ts // CONV_HALO
    cur = lambda c: pl.BlockSpec((ts, D_CONV), lambda i: (tmap(i), c))
    halo = lambda c: pl.BlockSpec((CONV_HALO, D_CONV), lambda i: (jnp.maximum(tmap(i) * hb - 1, 0), c))
    return cur, halo


def _conv_fwd(proj_a, mix, dw_w, dw_b, ln_g, ln_b, pw_w, l, *, name, ts=512):
    S = proj_a.shape[0]
    ts = _tile(S, ts, CONV_HALO)

    def body(a_ref, b_ref, ah_ref, bh_ref, w_ref, wb_ref, g_ref, bb_ref, pw_ref, mix_in, o_ref, buf):
        _glu_into(buf, a_ref, b_ref, ah_ref, bh_ref, pl.program_id(0) == 0, ts)
        c = _dwconv(buf, w_ref, ts) + wb_ref[...]
        yhat, _ = _layer_norm_stats(c)
        y = yhat * g_ref[...] + bb_ref[...]
        hs = y * _sigmoid(y)
        o_ref[...] = _dot(hs.astype(BF16), pw_ref[...], 1, 0).astype(BF16)

    cur, halo = _conv_specs(ts, lambda i: i)
    vec = pl.BlockSpec((1, D_CONV), lambda i: (0, 0))
    return pl.pallas_call(
        body, name=name, out_shape=jax.ShapeDtypeStruct(mix.shape, BF16), grid=(S // ts,),
        in_specs=[cur(A_COL), cur(B_COL), halo(A_COL), halo(B_COL),
                  pl.BlockSpec((None, CONV_HALO, D_CONV), lambda i: (l, 0, 0)), vec, vec, vec,
                  pl.BlockSpec((None, D_CONV, D_CONV), lambda i: (l, 0, 0)), ANY_SPEC],
        out_specs=pl.BlockSpec((ts, D_CONV), lambda i: (i, D_ATT // D_CONV)),
        scratch_shapes=[pltpu.VMEM((CONV_HALO + ts, D_CONV), F32)],
        input_output_aliases={9: 0},
        compiler_params=_params('parallel'),
    )(proj_a, proj_a, proj_a, proj_a, dw_w, dw_b, ln_g, ln_b, pw_w, mix)


def _conv_bwd(proj_a, d_mix, d_proj, dw_w, dw_b, ln_g, ln_b, pw_w, l, *, name, ts=512):
    S = proj_a.shape[0]
    ts = _tile(S, ts, CONV_HALO)
    n = S // ts
    d_col = D_ATT // D_CONV

    def body(a_ref, b_ref, ah_ref, bh_ref, dy_ref, w_ref, wb_ref, g_ref, bb_ref, pw_ref, dp_in,
             o_ref, dw_ref, dwb_ref, dg_ref, dbb_ref, dpw_ref, buf, dcbuf):
        i = pl.program_id(0)
        _glu_into(buf, a_ref, b_ref, ah_ref, bh_ref, i == n - 1, ts)
        c = _dwconv(buf, w_ref, ts) + wb_ref[...]
        yhat, rstd = _layer_norm_stats(c)
        y = yhat * g_ref[...] + bb_ref[...]
        sg = _sigmoid(y)
        hs = y * sg
        dout = dy_ref[...].astype(BF16)
        d_hs = _dot(dout, pw_ref[...], 1, 1)
        d_y = d_hs * (sg * (1.0 + y * (1.0 - sg)))
        d_yhat = d_y * g_ref[...]
        d_c = rstd * (d_yhat - jnp.mean(d_yhat, axis=-1, keepdims=True)
                      - yhat * jnp.mean(d_yhat * yhat, axis=-1, keepdims=True))

        @pl.when(i == 0)
        def _():
            dcbuf[ts:ts + CONV_HALO, :] = jnp.zeros((CONV_HALO, D_CONV), F32)
            dw_ref[...] = jnp.zeros(dw_ref.shape, F32)
            dwb_ref[...] = jnp.zeros(dwb_ref.shape, F32)
            dg_ref[...] = jnp.zeros(dg_ref.shape, F32)
            dbb_ref[...] = jnp.zeros(dbb_ref.shape, F32)
            dpw_ref[...] = jnp.zeros(dpw_ref.shape, F32)

        dcbuf[0:ts, :] = d_c
        dpw_ref[...] += _dot(hs.astype(BF16), dout, 0, 0)
        dg_ref[...] += jnp.sum(d_y * yhat, axis=0, keepdims=True)
        dbb_ref[...] += jnp.sum(d_y, axis=0, keepdims=True)
        dwb_ref[...] += jnp.sum(d_c, axis=0, keepdims=True)
        off = CONV_HALO - (CONV_WIDTH - 1)
        d_h = jnp.zeros((ts, D_CONV), F32)
        for k in range(CONV_WIDTH):
            d_h = d_h + w_ref[k:k + 1, :] * dcbuf[pl.ds(CONV_WIDTH - 1 - k, ts), :]
            dw_ref[k:k + 1, :] += jnp.sum(d_c * buf[pl.ds(off + k, ts), :], axis=0, keepdims=True)
        dcbuf[ts:ts + CONV_HALO, :] = d_c[0:CONV_HALO, :]
        a, sb = a_ref[...], _sigmoid(b_ref[...])
        o_ref[:, 0:D_CONV] = (d_h * sb).astype(BF16)
        o_ref[:, D_CONV:2 * D_CONV] = (d_h * a * sb * (1.0 - sb)).astype(BF16)

    rev = lambda i: n - 1 - i
    cur, halo = _conv_specs(ts, rev)
    vec = pl.BlockSpec((1, D_CONV), lambda i: (0, 0))
    wspec = pl.BlockSpec((CONV_HALO, D_CONV), lambda i: (0, 0))
    sq = pl.BlockSpec((D_CONV, D_CONV), lambda i: (0, 0))
    return pl.pallas_call(
        body, name=name,
        out_shape=(jax.ShapeDtypeStruct(d_proj.shape, BF16), jax.ShapeDtypeStruct((CONV_HALO, D_CONV), F32),
                   jax.ShapeDtypeStruct((1, D_CONV), F32), jax.ShapeDtypeStruct((1, D_CONV), F32),
                   jax.ShapeDtypeStruct((1, D_CONV), F32), jax.ShapeDtypeStruct((D_CONV, D_CONV), F32)),
        grid=(n,),
        in_specs=[cur(A_COL), cur(B_COL), halo(A_COL), halo(B_COL),
                  pl.BlockSpec((ts, D_CONV), lambda i: (rev(i), d_col)),
                  pl.BlockSpec((None, CONV_HALO, D_CONV), lambda i: (l, 0, 0)), vec, vec, vec,
                  pl.BlockSpec((None, D_CONV, D_CONV), lambda i: (l, 0, 0)), ANY_SPEC],
        out_specs=(pl.BlockSpec((ts, 2 * D_CONV), lambda i: (rev(i), D_QKV // (2 * D_CONV))), wspec, vec, vec, vec, sq),
        scratch_shapes=[pltpu.VMEM((CONV_HALO + ts, D_CONV), F32), pltpu.VMEM((ts + CONV_HALO, D_CONV), F32)],
        input_output_aliases={10: 0},
        compiler_params=_params('arbitrary'),
    )(proj_a, proj_a, proj_a, proj_a, d_mix, dw_w, dw_b, ln_g, ln_b, pw_w, d_proj)


def _pool_window():
    lane = lax.broadcasted_iota(jnp.int32, (1, D_POOL), 1)
    w = jnp.full((1, D_POOL), POOL_WINDOWS[0], jnp.int32)
    for g in range(1, len(POOL_WINDOWS)):
        w = jnp.where(lane // POOL_GROUP == g, POOL_WINDOWS[g], w)
    return w


def _pool_diff(buf, u_ref, uh_ref, first, tile, ts):
    buf[0:POOL_HALO, :] = jnp.where(first, 0.0, uh_ref[...])
    u = u_ref[...]
    buf[POOL_HALO:POOL_HALO + ts, :] = u
    wl = _pool_window()
    acc = u
    for j in range(1, max(POOL_WINDOWS)):
        acc = acc + jnp.where(j < wl, buf[pl.ds(POOL_HALO - j, ts), :], 0.0)
    pos = tile * ts + lax.broadcasted_iota(jnp.int32, (ts, 1), 0)
    cnt = jnp.minimum(pos + 1, wl).astype(F32)
    return acc / cnt - u, cnt


def _pool_specs(ts, tmap):
    hb = ts // POOL_HALO
    cur = pl.BlockSpec((ts, D_POOL), lambda i: (tmap(i), P_COL))
    halo = pl.BlockSpec((POOL_HALO, D_POOL), lambda i: (jnp.maximum(tmap(i) * hb - 1, 0), P_COL))
    return cur, halo


def _pool_fwd(proj_a, mix, wbd, scale, *, name, ts=512):
    S = proj_a.shape[0]
    ts = _tile(S, ts, POOL_HALO)

    def body(u_ref, uh_ref, w_ref, s_ref, mix_in, o_ref, buf):
        i = pl.program_id(0)
        d, _ = _pool_diff(buf, u_ref, uh_ref, i == 0, i, ts)
        o_ref[...] = (_dot(d.astype(BF16), w_ref[...], 1, 0) * s_ref[...]).astype(BF16)

    cur, halo = _pool_specs(ts, lambda i: i)
    return pl.pallas_call(
        body, name=name, out_shape=jax.ShapeDtypeStruct(mix.shape, BF16), grid=(S // ts,),
        in_specs=[cur, halo, pl.BlockSpec((D_POOL, D_POOL), lambda i: (0, 0)), pl.BlockSpec((1, D_POOL), lambda i: (0, 0)),
                  ANY_SPEC],
        out_specs=pl.BlockSpec((ts, D_POOL), lambda i: (i, (D_ATT + D_CONV) // D_POOL)),
        scratch_shapes=[pltpu.VMEM((POOL_HALO + ts, D_POOL), F32)],
        input_output_aliases={4: 0},
        compiler_params=_params('parallel'),
    )(proj_a, proj_a, wbd, scale, mix)


def _pool_bwd(proj_a, d_mix, d_proj, wbd, scale, *, name, ts=512):
    S = proj_a.shape[0]
    ts = _tile(S, ts, POOL_HALO)
    n = S // ts
    d_col = (D_ATT + D_CONV) // D_POOL

    def body(u_ref, uh_ref, dy_ref, w_ref, s_ref, dp_in, o_ref, dw_ref, ds_ref, buf, ebuf):
        i = pl.program_id(0)
        tile = n - 1 - i
        d, cnt = _pool_diff(buf, u_ref, uh_ref, tile == 0, tile, ts)
        db = d.astype(BF16)
        ypre = _dot(db, w_ref[...], 1, 0)
        dout = dy_ref[...]
        d_y = (dout * s_ref[...]).astype(BF16)
        d_d = _dot(d_y, w_ref[...], 1, 1)

        @pl.when(i == 0)
        def _():
            ebuf[ts:ts + POOL_HALO, :] = jnp.zeros((POOL_HALO, D_POOL), F32)
            dw_ref[...] = jnp.zeros(dw_ref.shape, F32)
            ds_ref[...] = jnp.zeros(ds_ref.shape, F32)

        dw_ref[...] += _dot(db, d_y, 0, 0)
        ds_ref[...] += jnp.sum(dout * ypre, axis=0, keepdims=True)
        e = d_d / cnt
        ebuf[0:ts, :] = e
        wl = _pool_window()
        acc = e
        for j in range(1, max(POOL_WINDOWS)):
            acc = acc + jnp.where(j < wl, ebuf[pl.ds(j, ts), :], 0.0)
        ebuf[ts:ts + POOL_HALO, :] = e[0:POOL_HALO, :]
        o_ref[...] = (acc - d_d).astype(BF16)

    rev = lambda i: n - 1 - i
    cur, halo = _pool_specs(ts, rev)
    sq = pl.BlockSpec((D_POOL, D_POOL), lambda i: (0, 0))
    vec = pl.BlockSpec((1, D_POOL), lambda i: (0, 0))
    return pl.pallas_call(
        body, name=name,
        out_shape=(jax.ShapeDtypeStruct(d_proj.shape, BF16), jax.ShapeDtypeStruct((D_POOL, D_POOL), F32),
                   jax.ShapeDtypeStruct((1, D_POOL), F32)),
        grid=(n,),
        in_specs=[cur, halo, pl.BlockSpec((ts, D_POOL), lambda i: (rev(i), d_col)), sq, vec, ANY_SPEC],
        out_specs=(pl.BlockSpec((ts, D_POOL), lambda i: (rev(i), P_COL)), sq, vec),
        scratch_shapes=[pltpu.VMEM((POOL_HALO + ts, D_POOL), F32), pltpu.VMEM((ts + POOL_HALO, D_POOL), F32)],
        input_output_aliases={5: 0},
        compiler_params=_params('arbitrary'),
    )(proj_a, proj_a, d_mix, wbd, scale, d_proj)


FFN_LANES = 128
FFN_GROUP = 8 * 8


def _ffn_rows(ref, c, row0, j):
    return ref.at[c][pl.ds(row0 + j, 8, stride=8), :]


def _ffn_specs(ts, tc2, tmap, l):
    hb = ts // FFN_HALO
    cur = pl.BlockSpec((ts, tc2), lambda c, i: (tmap(i), c))
    halo = pl.BlockSpec((FFN_HALO, tc2), lambda c, i: (jnp.maximum(tmap(i) * hb - 1, 0), c))
    wspec = pl.BlockSpec((None, FFN_HALO, tc2), lambda c, i: (l, 0, c))
    return cur, halo, wspec


def _ffn_fill(buf, x_ref, xh_ref, first, ts, nblk):
    for c in range(nblk):
        cs = slice(c * FFN_LANES, (c + 1) * FFN_LANES)
        buf[c, 0:FFN_HALO, :] = jnp.where(first, 0.0, xh_ref[:, cs])
        buf[c, FFN_HALO:FFN_HALO + ts, :] = x_ref[:, cs]


def _ffn_conv_piece(buf, w_ref, r0, c):
    ws = [w_ref[k:k + 1, c * FFN_LANES:(c + 1) * FFN_LANES] for k in range(FFN_CONV_WIDTH)]
    xs = [_ffn_rows(buf, c, FFN_HALO + r0, j) for j in range(1 - FFN_CONV_WIDTH, 8)]
    outs = []
    for j in range(8):
        acc = ws[0] * xs[j]
        for k in range(1, FFN_CONV_WIDTH):
            acc = acc + ws[k] * xs[j + k]
        outs.append(acc)
    return outs, xs


def _ffn_act_fwd(up, w, l, *, name, ts=256):
    S, F2 = up.shape
    tc = F2 // 4
    nb = tc // FFN_LANES
    ts = _tile(S, ts, FFN_GROUP)

    def body(x_ref, xh_ref, w_ref, o_ref, buf, stage):
        _ffn_fill(buf, x_ref, xh_ref, pl.program_id(1) == 0, ts, 2 * nb)
        for c in range(nb):
            for r0 in range(0, ts, FFN_GROUP):
                gates, _ = _ffn_conv_piece(buf, w_ref, r0, c)
                vals, _ = _ffn_conv_piece(buf, w_ref, r0, nb + c)
                for j in range(8):
                    stage.at[c][pl.ds(r0 + j, 8, stride=8), :] = gates[j] * _sigmoid(gates[j]) * vals[j]
            o_ref[:, c * FFN_LANES:(c + 1) * FFN_LANES] = stage[c].astype(BF16)

    cur, halo, wspec = _ffn_specs(ts, 2 * tc, lambda i: i, l)
    return pl.pallas_call(
        body, name=name, out_shape=jax.ShapeDtypeStruct((S, F2 // 2), BF16), grid=(2, S // ts),
        in_specs=[cur, halo, wspec],
        out_specs=pl.BlockSpec((ts, tc), lambda c, i: (i, c)),
        scratch_shapes=[pltpu.VMEM((2 * nb, FFN_HALO + ts, FFN_LANES), F32), pltpu.VMEM((nb, ts, FFN_LANES), F32)],
        compiler_params=_params('parallel', 'parallel'),
    )(up, up, w)


def _ffn_act_bwd(up, d_act, w, l, *, name, ts=256):
    S, F2 = up.shape
    tc = F2 // 4
    nb = tc // FFN_LANES
    ts = _tile(S, ts, FFN_GROUP)
    n = S // ts

    def body(x_ref, xh_ref, da_ref, w_ref, o_ref, dw_ref, buf, dcbuf, stage):
        i = pl.program_id(1)
        _ffn_fill(buf, x_ref, xh_ref, i == n - 1, ts, 2 * nb)

        @pl.when(i == 0)
        def _():
            dcbuf[:, ts:ts + FFN_HALO, :] = jnp.zeros((2 * nb, FFN_HALO, FFN_LANES), F32)
            dw_ref[...] = jnp.zeros(dw_ref.shape, F32)

        for c in range(nb):
            blocks = (c, nb + c)
            stage[c, :, :] = da_ref[:, c * FFN_LANES:(c + 1) * FFN_LANES]
            dws = [[jnp.zeros((8, FFN_LANES), F32) for _ in range(FFN_CONV_WIDTH)] for _ in range(2)]
            for r0 in range(0, ts, FFN_GROUP):
                gates, xg = _ffn_conv_piece(buf, w_ref, r0, blocks[0])
                vals, xv = _ffn_conv_piece(buf, w_ref, r0, blocks[1])
                for j in range(8):
                    sg = _sigmoid(gates[j])
                    da = _ffn_rows(stage, c, r0, j)
                    d_cs = (da * vals[j] * (sg * (1.0 + gates[j] * (1.0 - sg))), da * (gates[j] * sg))
                    for half, (d_c, xs) in enumerate(zip(d_cs, (xg, xv))):
                        dcbuf.at[blocks[half]][pl.ds(r0 + j, 8, stride=8), :] = d_c
                        for k in range(FFN_CONV_WIDTH):
                            dws[half][k] = dws[half][k] + d_c * xs[j + k]
            for half in range(2):
                cs = slice(blocks[half] * FFN_LANES, (blocks[half] + 1) * FFN_LANES)
                for k in range(FFN_CONV_WIDTH):
                    dw_ref[k:k + 1, cs] += jnp.sum(dws[half][k], axis=0, keepdims=True)
            for b in blocks:
                cs = slice(b * FFN_LANES, (b + 1) * FFN_LANES)
                ws = [w_ref[k:k + 1, cs] for k in range(FFN_CONV_WIDTH)]
                for r0 in range(0, ts, FFN_GROUP):
                    ds = [_ffn_rows(dcbuf, b, r0, j) for j in range(8 + FFN_CONV_WIDTH - 1)]
                    for j in range(8):
                        d_x = ws[FFN_CONV_WIDTH - 1] * ds[j]
                        for k in range(FFN_CONV_WIDTH - 1):
                            d_x = d_x + ws[k] * ds[j + FFN_CONV_WIDTH - 1 - k]
                        stage.at[c][pl.ds(r0 + j, 8, stride=8), :] = d_x
                o_ref[:, cs] = stage[c].astype(BF16)
                dcbuf[b, ts:ts + FFN_HALO, :] = dcbuf[b, 0:FFN_HALO, :]

    rev = lambda i: n - 1 - i
    cur, halo, wspec = _ffn_specs(ts, 2 * tc, rev, l)
    return pl.pallas_call(
        body, name=name,
        out_shape=(jax.ShapeDtypeStruct((S, F2), BF16), jax.ShapeDtypeStruct((FFN_HALO, F2), F32)),
        grid=(2, n),
        in_specs=[cur, halo, pl.BlockSpec((ts, tc), lambda c, i: (rev(i), c)), wspec],
        out_specs=(cur, pl.BlockSpec((FFN_HALO, 2 * tc), lambda c, i: (0, c))),
        scratch_shapes=[pltpu.VMEM((2 * nb, FFN_HALO + ts, FFN_LANES), F32), pltpu.VMEM((2 * nb, ts + FFN_HALO, FFN_LANES), F32),
                        pltpu.VMEM((nb, ts, FFN_LANES), F32)],
        compiler_params=_params('parallel', 'arbitrary'),
    )(up, up, d_act, w)


def _loss_head(y, target, *, name, ts=512):
    S, D = y.shape
    ts = _tile(S, ts, 8)

    def body(y_ref, t_ref, l_ref, dy_ref):
        i = pl.program_id(0)
        err = y_ref[...] - t_ref[...]
        dy_ref[...] = err * (1.0 / D)
        part = jnp.sum(jnp.sum(err * err, axis=1, keepdims=True), axis=0, keepdims=True) * (0.5 / D)

        @pl.when(i == 0)
        def _():
            l_ref[...] = part

        @pl.when(i > 0)
        def _():
            l_ref[...] += part

    row = pl.BlockSpec((ts, D), lambda i: (i, 0))
    return pl.pallas_call(
        body, name=name,
        out_shape=(jax.ShapeDtypeStruct((1, 1), F32), jax.ShapeDtypeStruct((S, D), F32)),
        grid=(S // ts,), in_specs=[row, row], out_specs=(pl.BlockSpec((1, 1), lambda i: (0, 0)), row),
        compiler_params=_params('arbitrary'),
    )(y, target)


def _pair_cols(w):
    lead, f2 = w.shape[:-1], w.shape[-1]
    return w.reshape(lead + (2, 2, f2 // 4)).swapaxes(-3, -2).reshape(lead + (f2,))


def _pad_axis(w, size, axis):
    pad = [(0, 0)] * w.ndim
    pad[axis] = (0, size - w.shape[axis])
    return jnp.pad(w, pad)


def _block_diag(pool_w):
    g = pool_w.shape[0]
    rows = [jnp.concatenate([pool_w[i] if i == j else jnp.zeros_like(pool_w[i]) for j in range(g)], axis=1) for i in range(g)]
    return jnp.concatenate(rows, axis=0)


def _small_weights(w, l):
    return dict(
        norm1_g=w['norm1_g'][l][None, :],
        b_col=_pad_axis(w['b_f'][l][:, None], FG_ROWS, 0),
        qg=jnp.tile(w['q_norm_g'][l], N_HEADS)[None, :],
        kg=jnp.tile(w['k_norm_g'][l], N_HEADS)[None, :],
        dw_b=w['conv_dw_b'][l][None, :], ln_g=w['conv_ln_g'][l][None, :], ln_b=w['conv_ln_b'][l][None, :],
        wbd=_block_diag(w['pool_w'][l]).astype(BF16),
        pool_scale=w['pool_scale'][l][None, :],
        norm2_g=w['norm2_g'][l][None, :],
    )


def _layer_fwd(x, W, p, l):
    n = lambda s: f'l{l}_{s}'
    S = x.shape[0]
    h = _rms_fwd(x, p['norm1_g'], name=n('norm1'))
    proj_a = _mm(h, W.get('w_a', h), b_lead=0, name=n('proj_a'), tn=768)
    z_raw = _mm(W.get('w_fg_t', h), h, a_lead=0, tb=True, name=n('proj_fg'))
    qkv = _qk_prep_fwd(proj_a, p['qg'], p['kg'], name=n('qk_norm'))
    f_cum = _forget_fwd(z_raw, p['b_col'], name=n('forget'))
    f3 = f_cum[:N_HEADS].reshape(N_HEADS // 2, 2, S)
    mix, att, lse = _attn_fwd(qkv, f3, name=n('attn'))
    mix = _conv_fwd(proj_a, mix, W.get('dw_w', h), p['dw_b'], p['ln_g'], p['ln_b'], W.get('pw_w', h), 0, name=n('conv'))
    mix = _pool_fwd(proj_a, mix, p['wbd'], p['pool_scale'], name=n('pool'))
    x1 = _mm(mix, W.get('w_out', mix), b_lead=0, res=x, name=n('out_proj'), tn=1024)
    h2 = _rms_fwd(x1, p['norm2_g'], name=n('norm2'))
    up = _mm(h2, W.get('w_up', mix), b_lead=0, name=n('up_proj'), tn=1408)
    act = _ffn_act_fwd(up, W.get('ffn_w', h), 0, name=n('ffn_act'))
    x2 = _mm(act, W.get('w_down', mix), b_lead=0, res=x1, name=n('down_proj'), tn=1024, tk=1408)
    saved = dict(x=x, h=h, proj_a=proj_a, z_raw=z_raw, qkv=qkv, f3=f3, att=att, lse=lse, mix=mix, x1=x1, h2=h2, up=up, act=act)
    return x2, saved


def _layer_bwd(dx2, W, p, s, l, sink):
    n = lambda t: f'l{l}_{t}'
    S = dx2.shape[0]
    g = {}
    W = W.ready

    def large(key, a, b, **kw):
        return sink.put(l, key, *_mm(a, b, ta=True, copy16=True, name=n('d_' + key), **kw))

    d_act = _mm(dx2, W['w_down'], b_lead=0, tb=True, name=n('d_act'), tn=1408)
    large('w_down', s['act'], dx2, tm=1408, tn=1024)
    d_up, d_ffn_w = _ffn_act_bwd(s['up'], d_act, W['ffn_w'], 0, name=n('ffn_act_bwd'))
    g['ffn_dw_w'] = _pair_cols(d_ffn_w[:FFN_CONV_WIDTH])
    d_h2 = _mm(d_up, W['w_up'], b_lead=0, tb=True, name=n('d_h2'), tn=1024, tk=1408)
    started = large('w_up', s['h2'], d_up, tm=1024, tn=1408)
    dx1, dg2 = _rms_bwd(s['x1'], p['norm2_g'], d_h2, dx2, name=n('norm2_bwd'))
    g['norm2_g'] = dg2[0]
    sink.point(l, 'mid', dx1)
    d_mix = _mm(dx1, W['w_out'], b_lead=0, tb=True, name=n('d_mix'), tn=1024, after=started)
    large('w_out', s['mix'], dx1, tm=1024, tn=1024)
    dq, dk, dv, df3, dr = _attn_bwd(s['qkv'], s['f3'], s['att'], s['lse'], d_mix, name=n('attn_bwd'))
    df = _pad_axis(df3.reshape(N_HEADS, S) + dr[:, ::HEAD_DIM].T, FG_ROWS, 0)
    d_z, d_b = _forget_bwd(s['z_raw'], p['b_col'], df, name=n('forget_bwd'))
    g['b_f'] = d_b[:N_HEADS, 0]
    d_proj, d_qg, d_kg = _qk_prep_bwd(s['proj_a'], dq, dk, dv, p['qg'], p['kg'], name=n('qk_norm_bwd'))
    g['q_norm_g'] = d_qg.reshape(N_HEADS, HEAD_DIM).sum(axis=0)
    g['k_norm_g'] = d_kg.reshape(N_HEADS, HEAD_DIM).sum(axis=0)
    d_proj, d_dw_w, d_dw_b, d_ln_g, d_ln_b, d_pw = _conv_bwd(
        s['proj_a'], d_mix, d_proj, W['dw_w'], p['dw_b'], p['ln_g'], p['ln_b'], W['pw_w'], 0, name=n('conv_bwd'))
    g['conv_dw_w'], g['conv_dw_b'] = d_dw_w[:CONV_WIDTH], d_dw_b[0]
    g['conv_ln_g'], g['conv_ln_b'], g['conv_pw_w'] = d_ln_g[0], d_ln_b[0], d_pw
    d_proj, d_wbd, d_scale = _pool_bwd(s['proj_a'], d_mix, d_proj, p['wbd'], p['pool_scale'], name=n('pool_bwd'))
    g['pool_w'] = jnp.stack([d_wbd[i * POOL_GROUP:(i + 1) * POOL_GROUP, i * POOL_GROUP:(i + 1) * POOL_GROUP]
                             for i in range(len(POOL_WINDOWS))])
    g['pool_scale'] = d_scale[0]
    d_w_a = _mm(s['h'], d_proj, ta=True, name=n('d_w_a'), tm=1024, tn=768)
    started = sink.put(l, 'w_in', d_w_a, _mm(d_z, s['h'], name=n('d_w_fg'), tn=1024).T)
    d_h_fg = _mm(d_z, W['w_fg_t'], b_lead=0, ta=True, name=n('d_h_fg'), tn=1024, after=started)
    d_h = _mm(d_proj, W['w_a'], b_lead=0, tb=True, res=d_h_fg, name=n('d_h'), tn=1024, tk=768)
    dx, dg1 = _rms_bwd(s['x'], p['norm1_g'], d_h, dx1, name=n('norm1_bwd'))
    g['norm1_g'] = dg1[0]
    sink.point(l, 'end', dx)
    return dx, g


SMALL_GRADS = REPLICATED + ('conv_dw_w', 'conv_pw_w', 'ffn_dw_w')


def _local_step(x, target, W, w_small, sink):
    depth = w_small['norm1_g'].shape[0]
    ps, saved = [], []
    for l in range(depth):
        p = _small_weights(w_small, l)
        x, s = _layer_fwd(x, W[l], p, l)
        ps.append(p)
        saved.append(s)
    loss, dx = _loss_head(x, target, name='loss_head')
    small = [None] * depth
    for l in reversed(range(depth)):
        dx, small[l] = _layer_bwd(dx, W[l], ps[l], saved[l], l, sink)
    return loss, dx, {k: jnp.stack([small[l][k] for l in range(depth)]) for k in SMALL_GRADS}


W_IN_SHARD = D_IN // N_CHIPS
W_IN_PAD = 640
N_A_TILES = D_PROJ_A // LANES
FG_COL0 = D_QKV


def _a_tile_base(j):
    if j == N_A_TILES:
        return FG_COL0, N_HEADS
    return (j * LANES if j * LANES < FG_COL0 else j * LANES + N_HEADS), LANES


def _shift_select(rows, cols, shift, row_max, col_max):
    r = lax.broadcasted_iota(jnp.int32, (rows, cols), 0)
    c = lax.broadcasted_iota(jnp.int32, (rows, cols), 1)
    return ((r + shift == c) & (r < row_max) & (c < col_max)).astype(BF16)


def _select_w_in(raw, *, name, tm=256):
    _, D, _ = raw.shape
    tm = _tile(D, tm, 16)
    plan = []
    for j in range(N_A_TILES + 1):
        base, cmax = _a_tile_base(j)
        parts = []
        for p in range(N_CHIPS):
            delta = base - W_IN_SHARD * p
            lo, hi = max(0, delta), min(W_IN_SHARD - 1, delta + cmax - 1)
            if lo > hi:
                continue
            a0 = (lo // LANES) * LANES
            kw = min(-(-(hi + 1 - a0) // LANES) * LANES, W_IN_PAD - a0)
            parts.append((p, a0, kw, delta))
        plan.append((cmax, parts))

    def body(raw_ref, wa_ref, fg_ref):
        for j, (cmax, parts) in enumerate(plan):
            acc = None
            for p, a0, kw, delta in parts:
                sel = _shift_select(kw, LANES, a0 - delta, W_IN_SHARD - a0, cmax)
                t = _dot(raw_ref[p, :, a0:a0 + kw], sel, 1, 0)
                acc = t if acc is None else acc + t
            if j == N_A_TILES:
                fg_ref[...] = acc.astype(BF16)
            else:
                wa_ref[:, j * LANES:(j + 1) * LANES] = acc.astype(BF16)

    return pl.pallas_call(
        body, name=name,
        out_shape=(jax.ShapeDtypeStruct((D, D_PROJ_A), BF16), jax.ShapeDtypeStruct((D, LANES), BF16)),
        grid=(D // tm,),
        in_specs=[pl.BlockSpec((N_CHIPS, tm, W_IN_PAD), lambda i: (0, i, 0))],
        out_specs=(pl.BlockSpec((tm, D_PROJ_A), lambda i: (i, 0)), pl.BlockSpec((tm, LANES), lambda i: (i, 0))),
        compiler_params=_params('parallel'),
    )(raw)


def _select_w_in_grads(p_a, p_fg, *, name, tm=256):
    D = p_a.shape[0]
    tm = _tile(D, tm, 16)
    n_local = W_IN_PAD // LANES
    plan = []
    for p in range(N_CHIPS):
        for i in range(n_local):
            cmax = max(0, min(LANES, W_IN_SHARD - i * LANES))
            parts = []
            for j in range(N_A_TILES + 1):
                base, rmax = _a_tile_base(j)
                e = base - W_IN_SHARD * p - i * LANES
                if e + rmax - 1 < 0 or e > cmax - 1:
                    continue
                parts.append((j, e, rmax))
            plan.append((p, i, cmax, parts))

    def body(a_ref, fg_ref, o32_ref, o16_ref):
        terms = {}

        def src(j):
            if j not in terms:
                v = fg_ref[...] if j == N_A_TILES else a_ref[:, j * LANES:(j + 1) * LANES]
                terms[j] = _split3(v)
            return terms[j]

        for p, i, cmax, parts in plan:
            acc = jnp.zeros((tm, LANES), F32)
            for j, e, rmax in parts:
                sel = _shift_select(LANES, LANES, e, rmax, cmax)
                for term in src(j):
                    acc = acc + _dot(term, sel, 1, 0)
            o32_ref[p, :, i * LANES:(i + 1) * LANES] = acc
            o16_ref[p, :, i * LANES:(i + 1) * LANES] = acc.astype(BF16)

    out = pl.BlockSpec((N_CHIPS, tm, W_IN_PAD), lambda i: (0, i, 0))
    return pl.pallas_call(
        body, name=name,
        out_shape=(jax.ShapeDtypeStruct((N_CHIPS, D, W_IN_PAD), F32), jax.ShapeDtypeStruct((N_CHIPS, D, W_IN_PAD), BF16)),
        grid=(D // tm,),
        in_specs=[pl.BlockSpec((tm, D_PROJ_A), lambda i: (i, 0)), pl.BlockSpec((tm, LANES), lambda i: (i, 0))],
        out_specs=(out, out),
        compiler_params=_params('parallel'),
    )(p_a, p_fg)


MESH = pl.DeviceIdType.MESH
HBM_SPEC = pl.BlockSpec(memory_space=pltpu.HBM)


def _place():
    return lax.axis_index('x'), lax.axis_index('y'), lax.axis_index('c')


def _other_chips(x, y):
    return [(1 - x, y), (x, 1 - y), (1 - x, 1 - y)]


def _up_pos(q):
    return (q % 2) * 2 + q // 2


CHUNKS = {
    'w_in': ('lead', None),
    'w_up': ('cols', None),
    'w_down': ('rows', None),
    'w_out': ('rows', None),
    'conv_pw_w': ('rows', None),
    'conv_dw_w': ('lead', None),
    'ffn_dw_w': ('lead', None),
}


def _window(ref, kind, l, q):
    at = (lambda *idx: ref.at[idx]) if l is None else (lambda *idx: ref.at[(l,) + idx])
    shape = ref.shape if l is None else ref.shape[1:]
    if kind == 'lead':
        return at(q)
    if kind == 'rows':
        cs = shape[0] // N_CHIPS
        return at(pl.ds(pl.multiple_of(q * cs, 16), cs), slice(None))
    cs = shape[1] // N_CHIPS
    return at(slice(None), pl.ds(pl.multiple_of(_up_pos(q) * cs, LANES), cs))


def _place_shard(src, l, pos_arr, full_shape, kind, *, name, tm=256):
    _, m, n = src.shape
    bm = _tile(m, tm, 16) if kind != 'rows' else m

    def body(pos_ref, s_ref, o_ref):
        o_ref[...] = s_ref[...].astype(BF16)

    if kind == 'lead':
        out = pl.BlockSpec((None, bm, n), lambda i, pos: (pos[0], i, 0))
    elif kind == 'rows':
        out = pl.BlockSpec((bm, n), lambda i, pos: (pos[0], 0))
    else:
        out = pl.BlockSpec((bm, n), lambda i, pos: (i, pos[0]))
    return pl.pallas_call(
        body, name=name, out_shape=jax.ShapeDtypeStruct(full_shape, BF16),
        grid_spec=pltpu.PrefetchScalarGridSpec(
            num_scalar_prefetch=1, grid=(m // bm,),
            in_specs=[pl.BlockSpec((None, bm, n), lambda i, pos: (l, i, 0))], out_specs=out),
        compiler_params=_params('parallel'),
    )(pos_arr, src)


GATHERED = ('w_in', 'w_up', 'w_down', 'w_out', 'conv_pw_w', 'conv_dw_w', 'ffn_dw_w')
GATHER_GROUPS = ((0, ('w_in', 'conv_dw_w', 'ffn_dw_w', 'conv_pw_w')), (0, ('w_out', 'w_up', 'w_down')), (1, GATHERED))
SEM_SPEC = pl.BlockSpec(memory_space=pltpu.SEMAPHORE)
SPLIT_COPY_PARAMS = pltpu.CompilerParams(has_side_effects=pltpu.SideEffectType.DATAFLOW_SIDE_EFFECTING)


def _gather_start(bufs):
    flat = [b for group in bufs for b in group]
    nb = len(flat)

    def body(*refs):
        outs, sems = refs[nb:2 * nb], refs[2 * nb:]
        x, y, c = _place()
        pos = 0
        for g, (_, keys) in enumerate(GATHER_GROUPS):
            for i, k in enumerate(keys):
                w = _window(outs[pos], CHUNKS[k][0], None, 2 * x + y)
                pos += 1
                for j, chip in enumerate(_other_chips(x, y)):
                    pltpu.make_async_remote_copy(src_ref=w, dst_ref=w, send_sem=sems[2 * g].at[3 * i + j],
                                                 recv_sem=sems[2 * g + 1].at[3 * i + j], device_id=(*chip, c),
                                                 device_id_type=MESH).start()

    sem_shapes = [pltpu.SemaphoreType.DMA((3 * len(keys),)) for _, keys in GATHER_GROUPS for _ in range(2)]
    res = pl.pallas_call(
        body, name='gather_start',
        out_shape=tuple(jax.ShapeDtypeStruct(b.shape, b.dtype) for b in flat) + tuple(sem_shapes),
        in_specs=[HBM_SPEC] * nb, out_specs=tuple([HBM_SPEC] * nb + [SEM_SPEC] * len(sem_shapes)),
        input_output_aliases={b: b for b in range(nb)},
        compiler_params=SPLIT_COPY_PARAMS,
    )(*[pltpu.with_memory_space_constraint(b, pltpu.HBM) for b in flat])
    out_bufs, sems, pos = [], res[nb:], 0
    for group in bufs:
        out_bufs.append(list(res[pos:pos + len(group)]))
        pos += len(group)
    return out_bufs, [(sems[2 * g], sems[2 * g + 1]) for g in range(len(GATHER_GROUPS))]


def _gather_wait(g, bufs, sems, after):
    keys = GATHER_GROUPS[g][1]
    nb = len(bufs)

    def body(*refs):
        send_sems, recv_sems = refs[nb], refs[nb + 1]
        outs = refs[nb + 3:]
        x, y, c = _place()
        for i, k in enumerate(keys):
            mine = _window(outs[i], CHUNKS[k][0], None, 2 * x + y)
            for j, (cx, cy) in enumerate(_other_chips(x, y)):
                theirs = _window(outs[i], CHUNKS[k][0], None, 2 * cx + cy)
                cp = pltpu.make_async_remote_copy(src_ref=mine, dst_ref=theirs, send_sem=send_sems.at[3 * i + j],
                                                  recv_sem=recv_sems.at[3 * i + j], device_id=(cx, cy, c), device_id_type=MESH)
                cp.wait_send()
                cp.wait_recv()

    return pl.pallas_call(
        body, name=f'gather_wait_{g}',
        out_shape=tuple(jax.ShapeDtypeStruct(b.shape, b.dtype) for b in bufs),
        in_specs=[HBM_SPEC] * nb + [SEM_SPEC, SEM_SPEC, ANY_SPEC], out_specs=tuple([HBM_SPEC] * nb),
        input_output_aliases={b: b for b in range(nb)},
        compiler_params=SPLIT_COPY_PARAMS,
    )(*bufs, *sems, after)


def _rs_block(M, N):
    return (_tile(M, 256, 16), _tile(N, 2048))


def _chunk_shape(shape, kind):
    if kind == 'lead':
        return tuple(shape[1:])
    if kind == 'rows':
        return (shape[0] // N_CHIPS, shape[1])
    return (shape[0], shape[1] // N_CHIPS)


def _rs_start(tag, bufs, kinds):
    nb = len(bufs)
    lands = [lax.empty((N_CHIPS - 1,) + _chunk_shape(b.shape, k), b.dtype) for b, k in zip(bufs, kinds)]

    def body(*refs):
        src, land = refs[2 * nb:3 * nb], refs[3 * nb:4 * nb]
        send_sems, recv_sems, token = refs[4 * nb:]
        token[...] = jnp.zeros(token.shape, F32)
        x, y, c = _place()
        for b in range(nb):
            for j, (cx, cy) in enumerate(_other_chips(x, y)):
                pltpu.make_async_remote_copy(
                    src_ref=_window(src[b], kinds[b], None, 2 * cx + cy), dst_ref=land[b].at[j],
                    send_sem=send_sems.at[3 * b + j], recv_sem=recv_sems.at[3 * b + j],
                    device_id=(cx, cy, c), device_id_type=MESH).start()

    sem = pltpu.SemaphoreType.DMA((3 * nb,))
    res = pl.pallas_call(
        body, name=f'rs_start_{tag}',
        out_shape=tuple(jax.ShapeDtypeStruct(b.shape, b.dtype) for b in list(bufs) + lands)
        + (sem, sem, jax.ShapeDtypeStruct((8, LANES), F32)),
        in_specs=[HBM_SPEC] * (2 * nb),
        out_specs=tuple([HBM_SPEC] * (2 * nb) + [SEM_SPEC, SEM_SPEC, pl.BlockSpec(memory_space=pltpu.VMEM)]),
        input_output_aliases={b: b for b in range(2 * nb)},
        compiler_params=SPLIT_COPY_PARAMS,
    )(*[pltpu.with_memory_space_constraint(b, pltpu.HBM) for b in list(bufs) + lands])
    return res[:nb], res[nb:2 * nb], res[2 * nb:2 * nb + 2], res[2 * nb + 2]


def _rs_wait(tag, bufs, lands, sems, kinds, after):
    nb = len(bufs)

    def body(*refs):
        send_sems, recv_sems = refs[2 * nb], refs[2 * nb + 1]
        src, land = refs[2 * nb + 3:3 * nb + 3], refs[3 * nb + 3:]
        x, y, c = _place()
        for b in range(nb):
            for j, (cx, cy) in enumerate(_other_chips(x, y)):
                cp = pltpu.make_async_remote_copy(
                    src_ref=_window(src[b], kinds[b], None, 2 * cx + cy), dst_ref=land[b].at[j],
                    send_sem=send_sems.at[3 * b + j], recv_sem=recv_sems.at[3 * b + j],
                    device_id=(cx, cy, c), device_id_type=MESH)
                cp.wait_send()
                cp.wait_recv()

    res = pl.pallas_call(
        body, name=f'rs_wait_{tag}',
        out_shape=tuple(jax.ShapeDtypeStruct(b.shape, b.dtype) for b in list(bufs) + list(lands)),
        in_specs=[HBM_SPEC] * (2 * nb) + [SEM_SPEC, SEM_SPEC, ANY_SPEC], out_specs=tuple([HBM_SPEC] * (2 * nb)),
        input_output_aliases={b: b for b in range(2 * nb)},
        compiler_params=SPLIT_COPY_PARAMS,
    )(*bufs, *lands, *sems, after)
    return res[nb:]


def _rs_sum(p, rb, kind, pos_arr, l, depth, buf, *, name):
    m, n = rb.shape[1:]
    bm, bn = _rs_block(m, n)
    nbm, nbn = m // bm, n // bn
    has_buf = buf is not None

    def body(q_ref, p_ref, r_ref, *rest):
        acc = p_ref[...]
        for j in range(N_CHIPS - 1):
            acc = acc + r_ref[j].astype(F32)
        rest[-1][...] = acc

    if kind == 'lead':
        p_map = lambda i, j, q: (q[0], i, j)
    elif kind == 'rows':
        p_map = lambda i, j, q: (q[0] * nbm + i, j)
    else:
        p_map = lambda i, j, q: (i, q[0] * nbn + j)
    r_spec = pl.BlockSpec((N_CHIPS - 1, bm, bn), lambda i, j, q: (0, i, j))
    p_spec = pl.BlockSpec(((None,) if kind == 'lead' else ()) + (bm, bn), p_map)
    return pl.pallas_call(
        body, name=name, out_shape=jax.ShapeDtypeStruct((depth, m, n), F32),
        grid_spec=pltpu.PrefetchScalarGridSpec(
            num_scalar_prefetch=1, grid=(nbm, nbn), in_specs=[p_spec, r_spec] + ([ANY_SPEC] if has_buf else []),
            out_specs=pl.BlockSpec((None, bm, bn), lambda i, j, q: (l, i, j))),
        input_output_aliases={3: 0} if has_buf else {},
        compiler_params=_params('parallel', 'parallel'),
    )(pos_arr, p, rb, *((buf,) if has_buf else ()))


def _swap_with_sibling(bufs):
    nb = len(bufs)

    def body(*refs):
        ins, outs = refs[:nb], refs[nb:2 * nb]
        send_sems, recv_sems = refs[2 * nb:]
        x, y, c = _place()
        copies = [pltpu.make_async_remote_copy(src_ref=ins[b], dst_ref=outs[b], send_sem=send_sems.at[b],
                                               recv_sem=recv_sems.at[b], device_id=(x, y, 1 - c), device_id_type=MESH)
                  for b in range(nb)]
        for cp in copies:
            cp.start()
        for cp in copies:
            cp.wait()

    return pl.pallas_call(
        body, name='rs_swap_sums', out_shape=tuple(jax.ShapeDtypeStruct(b.shape, b.dtype) for b in bufs),
        in_specs=[HBM_SPEC] * nb, out_specs=tuple([HBM_SPEC] * nb),
        scratch_shapes=[pltpu.SemaphoreType.DMA((nb,)), pltpu.SemaphoreType.DMA((nb,))],
    )(*bufs)


def _all_reduce_small(v):
    r = v.shape[0]

    def body(x_ref, tot_ref, all_ref, send_sems, recv_sems):
        x, y, c = _place()
        me, sibling = (x, y, c), (x, y, 1 - c)
        chips = _other_chips(x, y)

        def rows(px, py, pc):
            return all_ref.at[pl.ds((4 * px + 2 * py + pc) * r, r), :]

        def copy(k, block, to, src=None):
            return pltpu.make_async_remote_copy(
                src_ref=rows(*block) if src is None else src, dst_ref=rows(*block),
                send_sem=send_sems.at[k], recv_sem=recv_sems.at[k], device_id=to, device_id_type=MESH)

        rows(*me)[...] = x_ref[...]
        first = [copy(0, me, sibling, src=x_ref)]
        first += [copy(1 + j, me, (*chip, c), src=x_ref) for j, chip in enumerate(chips)]
        for cp in first:
            cp.start()
        passed = [copy(4 + j, (*chip, c), sibling) for j, chip in enumerate(chips)]
        for j, chip in enumerate(chips):
            copy(1 + j, (*chip, c), me).wait_recv()
            passed[j].start()
        copy(0, sibling, me).wait_recv()
        for j, chip in enumerate(chips):
            copy(4 + j, (*chip, 1 - c), me).wait_recv()
        for cp in first + passed:
            cp.wait_send()
        acc = all_ref[0:r, :]
        for d in range(1, N_DEV):
            acc = acc + all_ref[d * r:(d + 1) * r, :]
        tot_ref[...] = acc

    return pl.pallas_call(
        body, name='all_reduce_small', out_shape=jax.ShapeDtypeStruct((r, LANES), F32),
        in_specs=[pl.BlockSpec(memory_space=pltpu.VMEM)], out_specs=pl.BlockSpec(memory_space=pltpu.VMEM),
        scratch_shapes=[pltpu.VMEM((N_DEV * r, LANES), F32), pltpu.SemaphoreType.DMA((7,)), pltpu.SemaphoreType.DMA((7,))],
    )(v)


def _adamw(w, g, m, v, *, name, g2=None, ts=256):
    R, C = w.shape
    Cg = g.shape[1]
    ts = _tile(R, ts, 8)
    c1 = 1.0 - ADAM_B1 ** ADAM_STEP
    c2 = 1.0 - ADAM_B2 ** ADAM_STEP
    two = g2 is not None

    def body(w_ref, g_ref, *rest):
        m_ref, v_ref, go_ref, d_ref, nm_ref, nv_ref = rest[two:]
        gv = g_ref[:, 0:C]
        if two:
            gv = gv + rest[0][:, 0:C]
        nm = ADAM_B1 * m_ref[...] + (1.0 - ADAM_B1) * gv
        nv = ADAM_B2 * v_ref[...] + (1.0 - ADAM_B2) * (gv * gv)
        d_ref[...] = -ADAM_LR * ((nm / c1) / (jnp.sqrt(nv / c2) + ADAM_EPS) + ADAM_WD * w_ref[...])
        go_ref[...] = gv
        nm_ref[...] = nm
        nv_ref[...] = nv

    blk = pl.BlockSpec((ts, C), lambda i: (i, 0))
    gblk = pl.BlockSpec((ts, Cg), lambda i: (i, 0))
    shape = jax.ShapeDtypeStruct((R, C), F32)
    return pl.pallas_call(body, name=name, out_shape=(shape, shape, shape, shape), grid=(R // ts,),
                          in_specs=[blk, gblk] + ([gblk] if two else []) + [blk, blk], out_specs=(blk, blk, blk, blk),
                          compiler_params=_params('parallel'))(w, g, *((g2,) if two else ()), m, v)


def _pack_rows(parts, row_unit):
    flat = jnp.concatenate(parts)
    flat = _pad_axis(flat, -(-flat.shape[0] // (row_unit * LANES)) * row_unit * LANES, 0)
    return flat.reshape(-1, LANES)


def _as_2d(a):
    return a.reshape(-1, a.shape[-1])


def _mesh_place():
    cx, cy, cc = _place()
    chip = 2 * cx + cy
    as_arr = lambda v: jnp.reshape(v, (1,)).astype(jnp.int32)
    return chip, as_arr(cc), as_arr(chip), as_arr(_up_pos(chip))


class _LayerWeights:
    def __init__(self, groups):
        self.groups = groups
        self.ready = {}

    def get(self, name, after):
        if name not in self.ready:
            for names, wait in self.groups:
                if name in names:
                    self.ready.update({k: v[None] for k, v in wait(after).items()})
        return self.ready[name]


def _gather_full(w, place):
    chip, _, chip_arr, up_pos_arr = place
    L, D = w['w_in'].shape[:2]
    w_in_pad = _pad_axis(w['w_in'], W_IN_PAD, 2)

    def placed(k, l):
        if k == 'w_in':
            return _place_shard(w_in_pad, l, chip_arr, (N_CHIPS, D, W_IN_PAD), 'lead', name=f'place_w_in_{l}')
        if k == 'w_up':
            return _place_shard(w[k], l, up_pos_arr, (w[k].shape[1], N_CHIPS * w[k].shape[2]), 'cols', name=f'place_w_up_{l}')
        if k in ('conv_dw_w', 'ffn_dw_w'):
            return lax.dynamic_update_slice_in_dim(jnp.zeros((N_CHIPS,) + w[k].shape[1:], F32), w[k][l][None], chip, axis=0)
        return _place_shard(w[k], l, chip_arr, (N_CHIPS * w[k].shape[1], w[k].shape[2]), 'rows', name=f'place_{k}_{l}')

    bufs, sems = _gather_start([[placed(k, l) for k in keys] for l, keys in GATHER_GROUPS])
    unchunk = lambda a: jnp.moveaxis(a, 0, 1).reshape(a.shape[1], -1)

    def waiter(g):
        l, keys = GATHER_GROUPS[g]

        def wait(after):
            full = dict(zip(keys, _gather_wait(g, bufs[g], sems[g], after)))
            out = {}
            if 'w_in' in full:
                out['w_a'], w_fg = _select_w_in(full['w_in'], name=f'select_w_in_{l}')
                out['w_fg_t'] = w_fg.T
            if 'conv_dw_w' in full:
                out['dw_w'] = _pad_axis(unchunk(full['conv_dw_w']), CONV_HALO, 0)
            if 'ffn_dw_w' in full:
                out['ffn_w'] = _pad_axis(_pair_cols(unchunk(full['ffn_dw_w'])), FFN_HALO, 0)
            if 'conv_pw_w' in full:
                out['pw_w'] = full['conv_pw_w']
            out.update({k: full[k] for k in ('w_up', 'w_down', 'w_out') if k in full})
            return out

        names = {'w_in': ('w_a', 'w_fg_t'), 'conv_dw_w': ('dw_w',), 'ffn_dw_w': ('ffn_w',), 'conv_pw_w': ('pw_w',)}
        return tuple(n for k in keys for n in names.get(k, (k,))), wait

    return [_LayerWeights([waiter(g) for g in range(len(GATHER_GROUPS)) if GATHER_GROUPS[g][0] == l]) for l in range(L)]


RS_WIRE = ('w_in', 'w_up', 'w_down', 'w_out')
RS_GROUPS = (('ffn', ('w_down', 'w_up')), ('mix', ('w_out', 'w_in')))


class _GradReducer:
    def __init__(self, place, depth):
        _, _, self.chip_arr, self.up_pos_arr = place
        self.depth = depth
        self.got = {}
        self.flying = {}
        self.sums = {}

    def put(self, l, key, g32, g16):
        if key == 'w_in':
            g32, g16 = _select_w_in_grads(g32, g16, name=f'l{l}_select_w_in_grads')
        self.got[(l, key)] = (g32, g16)
        for tag, keys in RS_GROUPS:
            if key == keys[-1]:
                kinds = [CHUNKS[k][0] for k in keys]
                bufs, lands, sems, token = _rs_start(f'l{l}_{tag}', [self.got[(l, k)][1] for k in keys], kinds)
                self.flying[(l, tag)] = (bufs, lands, sems, kinds)
                return token
        return None

    def point(self, l, where, after):
        if where == 'mid':
            self._land(l + 1, 'mix', after)
        else:
            self._land(l, 'ffn', after)

    def _land(self, l, tag, after):
        if (l, tag) not in self.flying:
            return
        bufs, lands, sems, kinds = self.flying.pop((l, tag))
        lands = _rs_wait(f'l{l}_{tag}', bufs, lands, sems, kinds, after)
        for k, rb, kind in zip(dict(RS_GROUPS)[tag], lands, kinds):
            pos = self.up_pos_arr if kind == 'cols' else self.chip_arr
            self.sums[k] = _rs_sum(self.got.pop((l, k))[0], rb, kind, pos, l, self.depth, self.sums.get(k), name=f'l{l}_rs_sum_{k}')

    def finish(self, after):
        for l, tag in list(self.flying):
            self._land(l, tag, after)
        mine = [self.sums[k] for k in RS_WIRE]
        return {k: pair for k, pair in zip(RS_WIRE, zip(mine, _swap_with_sibling(mine)))}


def kernel(x, norm1_g, w_in, b_f, q_norm_g, k_norm_g, conv_dw_w, conv_dw_b, conv_ln_g, conv_ln_b, conv_pw_w, pool_w, pool_scale, w_out, norm2_g, w_up, ffn_dw_w, w_down, loss_target, m_norm1_g, m_w_in, m_b_f, m_q_norm_g, m_k_norm_g, m_conv_dw_w, m_conv_dw_b, m_conv_ln_g, m_conv_ln_b, m_conv_pw_w, m_pool_w, m_pool_scale, m_w_out, m_norm2_g, m_w_up, m_ffn_dw_w, m_w_down, v_norm1_g, v_w_in, v_b_f, v_q_norm_g, v_k_norm_g, v_conv_dw_w, v_conv_dw_b, v_conv_ln_g, v_conv_ln_b, v_conv_pw_w, v_pool_w, v_pool_scale, v_w_out, v_norm2_g, v_w_up, v_ffn_dw_w, v_w_down):
    given = dict(locals())
    w = {k: given[k] for k in WEIGHTS}
    mom_m = {k: given['m_' + k] for k in WEIGHTS}
    mom_v = {k: given['v_' + k] for k in WEIGHTS}
    place = _mesh_place()
    chip = place[0]
    W = _gather_full(w, place)

    reducer = _GradReducer(place, norm1_g.shape[0])
    loss_part, grad_x, g_small = _local_step(x[0], loss_target[0], W, {k: w[k] for k in REPLICATED}, reducer)
    loss = lax.psum(loss_part[0, 0], ('x', 'y', 'c'))
    sums = reducer.finish(grad_x)

    small = _pack_rows([g_small[k].reshape(-1) for k in SMALL_GRADS], 8)
    small_sum = _all_reduce_small(small)

    g_sum, delta, new_m, new_v = {}, {}, {}, {}
    for k in RS_WIRE:
        outs = _adamw(_as_2d(w[k]), _as_2d(sums[k][0]), _as_2d(mom_m[k]), _as_2d(mom_v[k]), g2=_as_2d(sums[k][1]), name='adamw_' + k)
        g_sum[k], delta[k], new_m[k], new_v[k] = [o.reshape(w[k].shape) for o in outs]
    off = 0
    small_full = {}
    for k in SMALL_GRADS:
        small_full[k] = small_sum.reshape(-1)[off:off + g_small[k].size].reshape(g_small[k].shape)
        off += g_small[k].size
    small_g = {k: small_full[k] for k in REPLICATED}
    small_g['conv_dw_w'] = lax.dynamic_slice_in_dim(small_full['conv_dw_w'], chip * w['conv_dw_w'].shape[2], w['conv_dw_w'].shape[2], axis=2)
    small_g['conv_pw_w'] = lax.dynamic_slice_in_dim(small_full['conv_pw_w'], chip * w['conv_pw_w'].shape[1], w['conv_pw_w'].shape[1], axis=1)
    small_g['ffn_dw_w'] = lax.dynamic_slice_in_dim(small_full['ffn_dw_w'], chip * w['ffn_dw_w'].shape[2], w['ffn_dw_w'].shape[2], axis=2)
    pack_small = lambda t: _pack_rows([t[k].reshape(-1) for k in SMALL_GRADS], 8)
    outs = _adamw(pack_small(w), pack_small(small_g), pack_small(mom_m), pack_small(mom_v), name='adamw_small')
    off = 0
    for k in SMALL_GRADS:
        pieces = [o.reshape(-1)[off:off + w[k].size].reshape(w[k].shape) for o in outs]
        g_sum[k], delta[k], new_m[k], new_v[k] = pieces
        off += w[k].size

    return (loss, grad_x[None], *[g_sum[k] for k in WEIGHTS], *[delta[k] for k in WEIGHTS],
            *[new_m[k] for k in WEIGHTS], *[new_v[k] for k in WEIGHTS])
```

```python
import functools

import jax
import jax.numpy as jnp
from jax import lax
from jax.experimental import pallas as pl
from jax.experimental.pallas import tpu as pltpu

F32 = jnp.float32
BF16 = jnp.bfloat16

N_HEADS = 8
HEAD_DIM = 64
D_ATT = N_HEADS * HEAD_DIM
D_CONV = 256
D_POOL = 256
D_MIX = D_ATT + D_CONV + D_POOL
D_QKV = 3 * D_ATT
D_PROJ_A = D_QKV + 2 * D_CONV + D_POOL
D_IN = D_PROJ_A + N_HEADS
FG_ROWS = 128
CONV_WIDTH = 31
CONV_HALO = 32
POOL_WINDOWS = (2, 4, 8, 16)
POOL_GROUP = 64
POOL_HALO = 16
FFN_CONV_WIDTH = 3
FFN_HALO = 8
ATT_SCALE = HEAD_DIM ** -0.5
EPS = 1e-6
NEG = -1e30
LANES = 128

ADAM_LR = 0.001
ADAM_B1 = 0.9
ADAM_B2 = 0.999
ADAM_EPS = 1e-08
ADAM_WD = 0.01
ADAM_STEP = 10

N_CHIPS = 4
N_DEV = 8
VMEM_LIMIT_BYTES = 56 * 1024 * 1024

REPLICATED = ('norm1_g', 'b_f', 'q_norm_g', 'k_norm_g', 'conv_dw_b', 'conv_ln_g', 'conv_ln_b',
              'pool_w', 'pool_scale', 'norm2_g')
WEIGHTS = ('norm1_g', 'w_in', 'b_f', 'q_norm_g', 'k_norm_g', 'conv_dw_w', 'conv_dw_b', 'conv_ln_g',
           'conv_ln_b', 'conv_pw_w', 'pool_w', 'pool_scale', 'w_out', 'norm2_g', 'w_up', 'ffn_dw_w', 'w_down')


def _tile(dim, pref, unit=LANES):
    if dim <= pref:
        return dim
    t = (pref // unit) * unit
    while t >= unit:
        if dim % t == 0:
            return t
        t -= unit
    raise ValueError(f'no tile for {dim} (preferred {pref})')


def _params(*sem):
    return pltpu.CompilerParams(dimension_semantics=sem, vmem_limit_bytes=VMEM_LIMIT_BYTES)


def _sigmoid(x):
    return 1.0 / (1.0 + jnp.exp(-x))


def _dot(a, b, ca, cb):
    return lax.dot_general(a, b, (((ca,), (cb,)), ((), ())), preferred_element_type=F32)


def _split3(y):
    y1 = y.astype(BF16)
    r1 = y - y1.astype(F32)
    y2 = r1.astype(BF16)
    y3 = (r1 - y2.astype(F32)).astype(BF16)
    return y1, y2, y3


def _dot3(y, e, ca=1, cb=0):
    y1, y2, y3 = _split3(y)
    return _dot(y1, e, ca, cb) + _dot(y2, e, ca, cb) + _dot(y3, e, ca, cb)


def _lead(spec_shape, imap, lead):
    if lead is None:
        return pl.BlockSpec(spec_shape, imap)
    return pl.BlockSpec((None,) + spec_shape, lambda *g: (lead,) + imap(*g))


ANY_SPEC = pl.BlockSpec(memory_space=pl.ANY)


def _mm(a, b, *, name, ta=False, tb=False, res=None, out_dtype=F32, tm=512, tn=512, tk=1024,
        a_lead=None, b_lead=None, copy16=False, after=None, cols_outer=False):
    a2, b2 = a.shape[-2:], b.shape[-2:]
    K, M = a2 if ta else a2[::-1]
    N, Kb = b2 if tb else b2[::-1]
    assert K == Kb, (a.shape, b.shape)
    tm, tn, tk = _tile(M, tm), _tile(N, tn), _tile(K, tk)
    nk = K // tk
    ca = 0 if ta else 1
    cb = 1 if tb else 0
    has_res = res is not None
    n_in = 2 + has_res + (after is not None)
    n_out = 1 + copy16

    def body(*refs):
        a_ref, b_ref = refs[:2]
        r_ref = refs[2] if has_res else None
        o_refs = refs[n_in:n_in + n_out]
        scratch = refs[n_in + n_out:]

        def write(r):
            if has_res:
                r = r + r_ref[...]
            o_refs[0][...] = r.astype(out_dtype)
            if copy16:
                o_refs[1][...] = r.astype(BF16)

        p = _dot(a_ref[...].astype(BF16), b_ref[...].astype(BF16), ca, cb)
        if nk == 1:
            write(p)
        else:
            acc = scratch[0]
            k = pl.program_id(2)

            @pl.when(k == 0)
            def _():
                acc[...] = p

            @pl.when(k > 0)
            def _():
                acc[...] += p

            @pl.when(k == nk - 1)
            def _():
                write(acc[...])

    ij = (lambda g0, g1: (g1, g0)) if cols_outer else (lambda g0, g1: (g0, g1))
    at = lambda f: (lambda g0, g1, k: f(*ij(g0, g1), k))
    a_spec = _lead((tk, tm), at(lambda i, j, k: (k, i)), a_lead) if ta else _lead((tm, tk), at(lambda i, j, k: (i, k)), a_lead)
    b_spec = _lead((tn, tk), at(lambda i, j, k: (j, k)), b_lead) if tb else _lead((tk, tn), at(lambda i, j, k: (k, j)), b_lead)
    o_spec = pl.BlockSpec((tm, tn), at(lambda i, j, k: (i, j)))
    in_specs = [a_spec, b_spec] + ([o_spec] if has_res else []) + ([ANY_SPEC] if after is not None else [])
    args = (a, b) + ((res,) if has_res else ()) + ((after,) if after is not None else ())
    out_shape = [jax.ShapeDtypeStruct((M, N), out_dtype)] + ([jax.ShapeDtypeStruct((M, N), BF16)] if copy16 else [])
    out = pl.pallas_call(
        body, name=name,
        out_shape=tuple(out_shape),
        grid=ij(M // tm, N // tn) + (nk,),
        in_specs=in_specs, out_specs=tuple([o_spec] * n_out),
        scratch_shapes=[pltpu.VMEM((tm, tn), F32)] if nk > 1 else [],
        compiler_params=_params('parallel', 'parallel', 'arbitrary'),
    )(*args)
    return out if copy16 else out[0]


def _rms_fwd(x, g, *, name, ts=512):
    S, D = x.shape
    ts = _tile(S, ts, 8)

    def body(x_ref, g_ref, o_ref):
        xv = x_ref[...]
        r = lax.rsqrt(jnp.mean(xv * xv, axis=-1, keepdims=True) + EPS)
        o_ref[...] = (xv * r * g_ref[...]).astype(BF16)

    return pl.pallas_call(
        body, name=name, out_shape=jax.ShapeDtypeStruct((S, D), BF16), grid=(S // ts,),
        in_specs=[pl.BlockSpec((ts, D), lambda i: (i, 0)), pl.BlockSpec((1, D), lambda i: (0, 0))],
        out_specs=pl.BlockSpec((ts, D), lambda i: (i, 0)),
        compiler_params=_params('parallel'),
    )(x, g)


def _rms_bwd(x, g, dh, dres, *, name, ts=512):
    S, D = x.shape
    ts = _tile(S, ts, 8)

    def body(x_ref, g_ref, dh_ref, dr_ref, dx_ref, dg_ref):
        i = pl.program_id(0)
        xv = x_ref[...]
        r = lax.rsqrt(jnp.mean(xv * xv, axis=-1, keepdims=True) + EPS)
        y = xv * r
        dh_v = dh_ref[...]
        dy = dh_v * g_ref[...]
        dx_ref[...] = dr_ref[...] + r * (dy - y * jnp.mean(dy * y, axis=-1, keepdims=True))
        part = jnp.sum(dh_v * y, axis=0, keepdims=True)

        @pl.when(i == 0)
        def _():
            dg_ref[...] = part

        @pl.when(i > 0)
        def _():
            dg_ref[...] += part

    row = pl.BlockSpec((ts, D), lambda i: (i, 0))
    vec = pl.BlockSpec((1, D), lambda i: (0, 0))
    return pl.pallas_call(
        body, name=name,
        out_shape=(jax.ShapeDtypeStruct((S, D), F32), jax.ShapeDtypeStruct((1, D), F32)),
        grid=(S // ts,), in_specs=[row, vec, row, row], out_specs=(row, vec),
        compiler_params=_params('arbitrary'),
    )(x, g, dh, dres)


def _group_ones():
    i = lax.broadcasted_iota(jnp.int32, (D_ATT, D_ATT), 0) // HEAD_DIM
    j = lax.broadcasted_iota(jnp.int32, (D_ATT, D_ATT), 1) // HEAD_DIM
    return (i == j).astype(BF16)


def _qk_prep_fwd(proj_a, qg, kg, *, name, ts=512):
    S = proj_a.shape[0]
    ts = _tile(S, ts, 16)

    def body(q_ref, k_ref, v_ref, qg_ref, kg_ref, e_ref, o_ref):
        e = e_ref[...]

        def norm(xv, gain):
            ms = _dot3(xv * xv, e) * (1.0 / HEAD_DIM)
            return xv * lax.rsqrt(ms + EPS) * gain

        o_ref[:, 0:D_ATT] = (norm(q_ref[...], qg_ref[...]) * ATT_SCALE).astype(BF16)
        o_ref[:, D_ATT:2 * D_ATT] = norm(k_ref[...], kg_ref[...]).astype(BF16)
        o_ref[:, 2 * D_ATT:3 * D_ATT] = v_ref[...].astype(BF16)

    col = lambda c: pl.BlockSpec((ts, D_ATT), lambda i: (i, c))
    vec = pl.BlockSpec((1, D_ATT), lambda i: (0, 0))
    return pl.pallas_call(
        body, name=name, out_shape=jax.ShapeDtypeStruct((S, D_QKV), BF16), grid=(S // ts,),
        in_specs=[col(0), col(1), col(2), vec, vec, pl.BlockSpec((D_ATT, D_ATT), lambda i: (0, 0))],
        out_specs=pl.BlockSpec((ts, D_QKV), lambda i: (i, 0)),
        compiler_params=_params('parallel'),
    )(proj_a, proj_a, proj_a, qg, kg, _group_ones())


def _qk_prep_bwd(proj_a, dq, dk, dv, qg, kg, *, name, ts=512):
    S = proj_a.shape[0]
    ts = _tile(S, ts, 16)

    def body(q_ref, k_ref, dq_ref, dk_ref, dv_ref, qg_ref, kg_ref, e_ref, o_ref, dqg_ref, dkg_ref):
        i = pl.program_id(0)
        e = e_ref[...]

        def norm_bwd(xv, dn, gain, scale):
            ms = _dot3(xv * xv, e) * (1.0 / HEAD_DIM)
            r = lax.rsqrt(ms + EPS)
            y = xv * r
            dy = dn * (gain * scale)
            mean = _dot3(dy * y, e) * (1.0 / HEAD_DIM)
            return r * (dy - y * mean), jnp.sum(dn * y, axis=0, keepdims=True) * scale

        dq_raw, dqg = norm_bwd(q_ref[...], dq_ref[...], qg_ref[...], ATT_SCALE)
        dk_raw, dkg = norm_bwd(k_ref[...], dk_ref[...], kg_ref[...], 1.0)
        o_ref[:, 0:D_ATT] = dq_raw.astype(BF16)
        o_ref[:, D_ATT:2 * D_ATT] = dk_raw.astype(BF16)
        o_ref[:, 2 * D_ATT:3 * D_ATT] = dv_ref[...].astype(BF16)

        @pl.when(i == 0)
        def _():
            dqg_ref[...] = dqg
            dkg_ref[...] = dkg

        @pl.when(i > 0)
        def _():
            dqg_ref[...] += dqg
            dkg_ref[...] += dkg

    col = lambda c: pl.BlockSpec((ts, D_ATT), lambda i: (i, c))
    vec = pl.BlockSpec((1, D_ATT), lambda i: (0, 0))
    return pl.pallas_call(
        body, name=name,
        out_shape=(jax.ShapeDtypeStruct((S, D_PROJ_A), BF16), jax.ShapeDtypeStruct((1, D_ATT), F32),
                   jax.ShapeDtypeStruct((1, D_ATT), F32)),
        grid=(S // ts,),
        in_specs=[col(0), col(1), col(0), col(0), col(0), vec, vec, pl.BlockSpec((D_ATT, D_ATT), lambda i: (0, 0))],
        out_specs=(pl.BlockSpec((ts, D_QKV), lambda i: (i, 0)), vec, vec),
        compiler_params=_params('arbitrary'),
    )(proj_a, proj_a, dq, dk, dv, qg, kg, _group_ones())


def _tri_ones(upper):
    i = lax.broadcasted_iota(jnp.int32, (LANES, LANES), 0)
    j = lax.broadcasted_iota(jnp.int32, (LANES, LANES), 1)
    return ((i <= j) if upper else (i >= j)).astype(BF16)


def _forget_fwd(z_raw, b_col, *, name):
    R, S = z_raw.shape
    nb = S // LANES

    def body(z_ref, b_ref, u_ref, f_ref):
        u = u_ref[...]
        carry = jnp.zeros((R, 1), F32)
        for j in range(nb):
            z = z_ref[:, j * LANES:(j + 1) * LANES] + b_ref[...]
            logf = jnp.minimum(z, 0.0) - jnp.log(1.0 + jnp.exp(-jnp.abs(z)))
            f_ref[:, j * LANES:(j + 1) * LANES] = _dot3(logf, u) + carry
            carry = carry + jnp.sum(logf, axis=1, keepdims=True)

    return pl.pallas_call(
        body, name=name, out_shape=jax.ShapeDtypeStruct((R, S), F32),
        compiler_params=pltpu.CompilerParams(vmem_limit_bytes=VMEM_LIMIT_BYTES),
    )(z_raw, b_col, _tri_ones(True))


def _forget_bwd(z_raw, b_col, df, *, name):
    R, S = z_raw.shape
    nb = S // LANES

    def body(z_ref, b_ref, df_ref, l_ref, dz_ref, db_ref):
        low = l_ref[...]
        carry = jnp.zeros((R, 1), F32)
        db = jnp.zeros((R, 1), F32)
        for j in reversed(range(nb)):
            d = df_ref[:, j * LANES:(j + 1) * LANES]
            dlogf = _dot3(d, low) + carry
            carry = carry + jnp.sum(d, axis=1, keepdims=True)
            z = z_ref[:, j * LANES:(j + 1) * LANES] + b_ref[...]
            dz = dlogf * _sigmoid(-z)
            dz_ref[:, j * LANES:(j + 1) * LANES] = dz
            db = db + jnp.sum(dz, axis=1, keepdims=True)
        db_ref[...] = db

    return pl.pallas_call(
        body, name=name,
        out_shape=(jax.ShapeDtypeStruct((R, S), F32), jax.ShapeDtypeStruct((R, 1), F32)),
        compiler_params=pltpu.CompilerParams(vmem_limit_bytes=VMEM_LIMIT_BYTES),
    )(z_raw, b_col, df, _tri_ones(False))


def _head_mask(hh):
    lane = lax.broadcasted_iota(jnp.int32, (1, LANES), 1)
    return (lane // HEAD_DIM) == hh


def _causal(s, qi, ki, t):
    rows = qi * t + lax.broadcasted_iota(jnp.int32, (t, t), 0)
    cols = ki * t + lax.broadcasted_iota(jnp.int32, (t, t), 1)
    return jnp.where(cols <= rows, s, NEG)


AUG = 2 * HEAD_DIM


def _aug_consts():
    i = lax.broadcasted_iota(jnp.int32, (D_ATT, N_HEADS * AUG), 0)
    j = lax.broadcasted_iota(jnp.int32, (D_ATT, N_HEADS * AUG), 1)
    spread = (j == (i // HEAD_DIM) * AUG + i % HEAD_DIM).astype(BF16)
    h = lax.broadcasted_iota(jnp.int32, (LANES, N_HEADS * AUG), 0)
    c = lax.broadcasted_iota(jnp.int32, (LANES, N_HEADS * AUG), 1)
    gate = [((c == h * AUG + HEAD_DIM + t) & (h < N_HEADS)).astype(BF16) for t in range(3)]
    lane = lax.broadcasted_iota(jnp.int32, (1, N_HEADS * AUG), 1) % AUG
    ones_q = ((lane >= HEAD_DIM) & (lane < HEAD_DIM + 3)).astype(F32)
    ones_v = (lane == HEAD_DIM).astype(F32)
    return spread, gate, ones_q, ones_v


def _attn_aug(qkv, f_cum, *, name, ts=512):
    S = qkv.shape[0]
    ts = _tile(S, ts)
    spread, gate, ones_q, ones_v = _aug_consts()
    W = N_HEADS * AUG

    def body(q_ref, k_ref, v_ref, f_ref, sp_ref, g0_ref, g1_ref, g2_ref, oq_ref, ov_ref, qa_ref, ka_ref, va_ref):
        sp = sp_ref[...]
        qa_ref[...] = (_dot(q_ref[...], sp, 1, 0) + oq_ref[...]).astype(BF16)
        va_ref[...] = (_dot(v_ref[...], sp, 1, 0) + ov_ref[...]).astype(BF16)
        terms = _split3(-jnp.transpose(f_ref[...]))
        ka = _dot(k_ref[...], sp, 1, 0)
        for t, g_ref in zip(terms, (g0_ref, g1_ref, g2_ref)):
            ka = ka + _dot(t, g_ref[...], 1, 0)
        ka_ref[...] = ka.astype(BF16)

    col = lambda c: pl.BlockSpec((ts, D_ATT), lambda i: (i, c))
    full = lambda a: pl.BlockSpec(a.shape, lambda i: (0, 0))
    out = pl.BlockSpec((ts, W), lambda i: (i, 0))
    shape = jax.ShapeDtypeStruct((S, W), BF16)
    consts = (spread, *gate, ones_q, ones_v)
    return pl.pallas_call(
        body, name=name, out_shape=(shape, shape, shape), grid=(S // ts,),
        in_specs=[col(0), col(1), col(2), pl.BlockSpec((FG_ROWS, ts), lambda i: (0, i))] + [full(a) for a in consts],
        out_specs=(out, out, out),
        compiler_params=_params('parallel'),
    )(qkv, qkv, qkv, f_cum, *consts)


def _attn_fwd(qa, ka, va, *, name, tq=512, tk=1024):
    S = qa.shape[0]
    tq, tk = _tile(S, tq), _tile(S, tk)
    nq, nk = S // tq, S // tk
    npair = N_HEADS // 2

    def body(q_ref, k_ref, v_ref, mix_ref, o_ref, lse_ref, m_s, acc_s):
        qi, ki = pl.program_id(1), pl.program_id(2)
        last = (qi * tq + tq - 1) // tk
        first_masked = (qi * tq) // tk

        @pl.when(ki == 0)
        def _():
            m_s[...] = jnp.full(m_s.shape, NEG, F32)
            acc_s[...] = jnp.zeros(acc_s.shape, F32)

        def step(masked):
            if masked:
                rows = qi * tq + lax.broadcasted_iota(jnp.int32, (tq, tk), 0)
                cols = ki * tk + lax.broadcasted_iota(jnp.int32, (tq, tk), 1)
                keep = cols <= rows
            m_prev = [m_s[hh] for hh in range(2)]
            acc_prev = [acc_s[hh] for hh in range(2)]
            ss = []
            for hh in range(2):
                s = _dot(q_ref[:, hh * AUG:(hh + 1) * AUG], k_ref[:, hh * AUG:(hh + 1) * AUG], 1, 1)
                ss.append(jnp.where(keep, s, NEG) if masked else s)
            m_new = [jnp.maximum(m_prev[hh], jnp.max(ss[hh], axis=1, keepdims=True)) for hh in range(2)]
            ps = [jnp.exp(ss[hh] - jnp.tile(m_new[hh], (1, tk // LANES))).astype(BF16) for hh in range(2)]
            for hh in range(2):
                alpha = jnp.exp(m_prev[hh] - m_new[hh])
                acc_s[hh] = alpha * acc_prev[hh] + _dot(ps[hh], v_ref[:, hh * AUG:(hh + 1) * AUG], 1, 0)
                m_s[hh] = m_new[hh]

        @pl.when(ki < first_masked)
        def _():
            step(False)

        @pl.when((ki >= first_masked) & (ki <= last))
        def _():
            step(True)

        @pl.when(ki == last)
        def _():
            lane = lax.broadcasted_iota(jnp.int32, (1, LANES), 1)
            outs, lses = [], []
            for hh in range(2):
                acc = acc_s[hh]
                denom = jnp.sum(jnp.where(lane == HEAD_DIM, acc, 0.0), axis=1, keepdims=True)
                outs.append(acc / denom)
                lses.append(m_s[hh] + jnp.log(denom))
            o = jnp.where(lane < HEAD_DIM, outs[0], pltpu.roll(outs[1], HEAD_DIM, 1))
            o_ref[...] = o
            mix_ref[...] = o.astype(BF16)
            lse_ref[...] = jnp.where(lane < HEAD_DIM, lses[0], lses[1])

    def kmap(h, i, j):
        return (jnp.minimum(j, (i * tq + tq - 1) // tk), h)

    out = pl.BlockSpec((tq, LANES), lambda h, i, j: (i, h))
    return pl.pallas_call(
        body, name=name,
        out_shape=(jax.ShapeDtypeStruct((S, D_MIX), BF16), jax.ShapeDtypeStruct((S, D_ATT), F32),
                   jax.ShapeDtypeStruct((S, D_ATT), F32)),
        grid=(npair, nq, nk),
        in_specs=[pl.BlockSpec((tq, 2 * AUG), lambda h, i, j: (i, h)),
                  pl.BlockSpec((tk, 2 * AUG), kmap), pl.BlockSpec((tk, 2 * AUG), kmap)],
        out_specs=(out, out, out),
        scratch_shapes=[pltpu.VMEM((2, tq, LANES), F32), pltpu.VMEM((2, tq, LANES), F32)],
        compiler_params=_params('parallel', 'parallel', 'arbitrary'),
    )(qa, ka, va)


def _attn_bwd(qkv, f3, att, lse, d_mix, *, name, t=512):
    S = qkv.shape[0]
    t = _tile(S, t)
    n = S // t
    npair = N_HEADS // 2

    def body(q_ref, k_ref, v_ref, f_ref, o_ref, lse_ref, do_ref, dq_ref, dk_ref, dv_ref, df_ref, dr_ref, dk_s, dv_s, df_s):
        ki, qi = pl.program_id(1), pl.program_id(2)

        @pl.when(qi == ki)
        def _():
            dk_s[...] = jnp.zeros(dk_s.shape, F32)
            dv_s[...] = jnp.zeros(dv_s.shape, F32)
            df_s[...] = jnp.zeros(df_s.shape, F32)

        @pl.when(qi >= ki)
        def _():
            q, k, v = q_ref[...], k_ref[...], v_ref[...]
            do, o, lse = do_ref[...], o_ref[...], lse_ref[...]
            dq_blk = jnp.zeros((t, LANES), F32)
            dr_blk = jnp.zeros((t, LANES), F32)
            for hh in range(2):
                msk = _head_mask(hh)
                qm = jnp.where(msk, q, jnp.zeros_like(q))
                km = jnp.where(msk, k, jnp.zeros_like(k))
                dom = jnp.where(msk, do, 0.0).astype(BF16)
                s = _causal(_dot(qm, k, 1, 1) - f_ref[0, hh:hh + 1, :], qi, ki, t)
                lse_h = jnp.max(jnp.where(msk, lse, NEG), axis=1, keepdims=True)
                p = jnp.exp(s - lse_h)
                dp = _dot(dom, v, 1, 1)
                delta = jnp.sum(dom.astype(F32) * o, axis=1, keepdims=True)
                ds = p * (dp - delta)
                dsb = ds.astype(BF16)
                dv_s[...] += _dot(p.astype(BF16), dom, 0, 0)
                dk_s[...] += _dot(dsb, qm, 0, 0)
                dq_blk = dq_blk + _dot(dsb, km, 1, 0)
                df_s[hh] -= jnp.sum(ds, axis=0, keepdims=True)
                dr_blk = dr_blk + jnp.where(msk, jnp.sum(ds, axis=1, keepdims=True), 0.0)
            rows = pl.ds(pl.multiple_of(qi * t, t), t)

            @pl.when(ki == 0)
            def _():
                dq_ref[rows, :] = dq_blk
                dr_ref[rows, :] = dr_blk

            @pl.when(ki > 0)
            def _():
                dq_ref[rows, :] += dq_blk
                dr_ref[rows, :] += dr_blk

        @pl.when(qi == n - 1)
        def _():
            dk_ref[...] = dk_s[...]
            dv_ref[...] = dv_s[...]
            df_ref[0, 0:1, :] = df_s[0]
            df_ref[0, 1:2, :] = df_s[1]

    qrow = lambda h, j, i: (jnp.maximum(i, j), h)
    return pl.pallas_call(
        body, name=name,
        out_shape=(jax.ShapeDtypeStruct((S, D_ATT), F32), jax.ShapeDtypeStruct((S, D_ATT), F32),
                   jax.ShapeDtypeStruct((S, D_ATT), F32), jax.ShapeDtypeStruct((npair, 2, S), F32),
                   jax.ShapeDtypeStruct((S, D_ATT), F32)),
        grid=(npair, n, n),
        in_specs=[pl.BlockSpec((t, LANES), qrow),
                  pl.BlockSpec((t, LANES), lambda h, j, i: (j, npair + h)),
                  pl.BlockSpec((t, LANES), lambda h, j, i: (j, 2 * npair + h)),
                  pl.BlockSpec((1, 2, t), lambda h, j, i: (h, 0, j)),
                  pl.BlockSpec((t, LANES), qrow),
                  pl.BlockSpec((t, LANES), qrow),
                  pl.BlockSpec((t, LANES), qrow)],
        out_specs=(pl.BlockSpec((S, LANES), lambda h, j, i: (0, h)),
                   pl.BlockSpec((t, LANES), lambda h, j, i: (j, h)),
                   pl.BlockSpec((t, LANES), lambda h, j, i: (j, h)),
                   pl.BlockSpec((1, 2, t), lambda h, j, i: (h, 0, j)),
                   pl.BlockSpec((S, LANES), lambda h, j, i: (0, h))),
        scratch_shapes=[pltpu.VMEM((t, LANES), F32), pltpu.VMEM((t, LANES), F32), pltpu.VMEM((2, 1, t), F32)],
        compiler_params=_params('parallel', 'arbitrary', 'arbitrary'),
    )(qkv, qkv, qkv, f3, att, lse, d_mix)


A_COL = D_QKV // D_CONV
B_COL = A_COL + 1
P_COL = B_COL + 1


def _layer_norm_stats(c):
    mu = jnp.mean(c, axis=-1, keepdims=True)
    xc = c - mu
    rstd = lax.rsqrt(jnp.mean(xc * xc, axis=-1, keepdims=True) + EPS)
    return xc * rstd, rstd


def _glu_into(buf, a_ref, b_ref, ah_ref, bh_ref, first, ts):
    halo = ah_ref[...] * _sigmoid(bh_ref[...])
    buf[0:CONV_HALO, :] = jnp.where(first, 0.0, halo)
    buf[CONV_HALO:CONV_HALO + ts, :] = a_ref[...] * _sigmoid(b_ref[...])


def _dwconv(buf, w_ref, ts):
    off = CONV_HALO - (CONV_WIDTH - 1)
    acc = w_ref[0:1, :] * buf[pl.ds(off, ts), :]
    for k in range(1, CONV_WIDTH):
        acc = acc + w_ref[k:k + 1, :] * buf[pl.ds(off + k, ts), :]
    return acc


def _conv_specs(ts, tmap):
    hb = ts // CONV_HALO
    cur = lambda c: pl.BlockSpec((ts, D_CONV), lambda i: (tmap(i), c))
    halo = lambda c: pl.BlockSpec((CONV_HALO, D_CONV), lambda i: (jnp.maximum(tmap(i) * hb - 1, 0), c))
    return cur, halo


def _conv_fwd(proj_a, mix, dw_w, dw_b, ln_g, ln_b, pw_w, l, *, name, ts=512):
    S = proj_a.shape[0]
    ts = _tile(S, ts, CONV_HALO)

    def body(a_ref, b_ref, ah_ref, bh_ref, w_ref, wb_ref, g_ref, bb_ref, pw_ref, mix_in, o_ref, buf):
        _glu_into(buf, a_ref, b_ref, ah_ref, bh_ref, pl.program_id(0) == 0, ts)
        c = _dwconv(buf, w_ref, ts) + wb_ref[...]
        yhat, _ = _layer_norm_stats(c)
        y = yhat * g_ref[...] + bb_ref[...]
        hs = y * _sigmoid(y)
        o_ref[...] = _dot(hs.astype(BF16), pw_ref[...], 1, 0).astype(BF16)

    cur, halo = _conv_specs(ts, lambda i: i)
    vec = pl.BlockSpec((1, D_CONV), lambda i: (0, 0))
    return pl.pallas_call(
        body, name=name, out_shape=jax.ShapeDtypeStruct(mix.shape, BF16), grid=(S // ts,),
        in_specs=[cur(A_COL), cur(B_COL), halo(A_COL), halo(B_COL),
                  pl.BlockSpec((None, CONV_HALO, D_CONV), lambda i: (l, 0, 0)), vec, vec, vec,
                  pl.BlockSpec((None, D_CONV, D_CONV), lambda i: (l, 0, 0)), ANY_SPEC],
        out_specs=pl.BlockSpec((ts, D_CONV), lambda i: (i, D_ATT // D_CONV)),
        scratch_shapes=[pltpu.VMEM((CONV_HALO + ts, D_CONV), F32)],
        input_output_aliases={9: 0},
        compiler_params=_params('parallel'),
    )(proj_a, proj_a, proj_a, proj_a, dw_w, dw_b, ln_g, ln_b, pw_w, mix)


def _conv_bwd(proj_a, d_mix, d_proj, dw_w, dw_b, ln_g, ln_b, pw_w, l, *, name, ts=512):
    S = proj_a.shape[0]
    ts = _tile(S, ts, CONV_HALO)
    n = S // ts
    d_col = D_ATT // D_CONV

    def body(a_ref, b_ref, ah_ref, bh_ref, dy_ref, w_ref, wb_ref, g_ref, bb_ref, pw_ref, dp_in,
             o_ref, dw_ref, dwb_ref, dg_ref, dbb_ref, dpw_ref, buf, dcbuf):
        i = pl.program_id(0)
        _glu_into(buf, a_ref, b_ref, ah_ref, bh_ref, i == n - 1, ts)
        c = _dwconv(buf, w_ref, ts) + wb_ref[...]
        yhat, rstd = _layer_norm_stats(c)
        y = yhat * g_ref[...] + bb_ref[...]
        sg = _sigmoid(y)
        hs = y * sg
        dout = dy_ref[...].astype(BF16)
        d_hs = _dot(dout, pw_ref[...], 1, 1)
        d_y = d_hs * (sg * (1.0 + y * (1.0 - sg)))
        d_yhat = d_y * g_ref[...]
        d_c = rstd * (d_yhat - jnp.mean(d_yhat, axis=-1, keepdims=True)
                      - yhat * jnp.mean(d_yhat * yhat, axis=-1, keepdims=True))

        @pl.when(i == 0)
        def _():
            dcbuf[ts:ts + CONV_HALO, :] = jnp.zeros((CONV_HALO, D_CONV), F32)
            dw_ref[...] = jnp.zeros(dw_ref.shape, F32)
            dwb_ref[...] = jnp.zeros(dwb_ref.shape, F32)
            dg_ref[...] = jnp.zeros(dg_ref.shape, F32)
            dbb_ref[...] = jnp.zeros(dbb_ref.shape, F32)
            dpw_ref[...] = jnp.zeros(dpw_ref.shape, F32)

        dcbuf[0:ts, :] = d_c
        dpw_ref[...] += _dot(hs.astype(BF16), dout, 0, 0)
        dg_ref[...] += jnp.sum(d_y * yhat, axis=0, keepdims=True)
        dbb_ref[...] += jnp.sum(d_y, axis=0, keepdims=True)
        dwb_ref[...] += jnp.sum(d_c, axis=0, keepdims=True)
        off = CONV_HALO - (CONV_WIDTH - 1)
        d_h = jnp.zeros((ts, D_CONV), F32)
        for k in range(CONV_WIDTH):
            d_h = d_h + w_ref[k:k + 1, :] * dcbuf[pl.ds(CONV_WIDTH - 1 - k, ts), :]
            dw_ref[k:k + 1, :] += jnp.sum(d_c * buf[pl.ds(off + k, ts), :], axis=0, keepdims=True)
        dcbuf[ts:ts + CONV_HALO, :] = d_c[0:CONV_HALO, :]
        a, sb = a_ref[...], _sigmoid(b_ref[...])
        o_ref[:, 0:D_CONV] = (d_h * sb).astype(BF16)
        o_ref[:, D_CONV:2 * D_CONV] = (d_h * a * sb * (1.0 - sb)).astype(BF16)

    rev = lambda i: n - 1 - i
    cur, halo = _conv_specs(ts, rev)
    vec = pl.BlockSpec((1, D_CONV), lambda i: (0, 0))
    wspec = pl.BlockSpec((CONV_HALO, D_CONV), lambda i: (0, 0))
    sq = pl.BlockSpec((D_CONV, D_CONV), lambda i: (0, 0))
    return pl.pallas_call(
        body, name=name,
        out_shape=(jax.ShapeDtypeStruct(d_proj.shape, BF16), jax.ShapeDtypeStruct((CONV_HALO, D_CONV), F32),
                   jax.ShapeDtypeStruct((1, D_CONV), F32), jax.ShapeDtypeStruct((1, D_CONV), F32),
                   jax.ShapeDtypeStruct((1, D_CONV), F32), jax.ShapeDtypeStruct((D_CONV, D_CONV), F32)),
        grid=(n,),
        in_specs=[cur(A_COL), cur(B_COL), halo(A_COL), halo(B_COL),
                  pl.BlockSpec((ts, D_CONV), lambda i: (rev(i), d_col)),
                  pl.BlockSpec((None, CONV_HALO, D_CONV), lambda i: (l, 0, 0)), vec, vec, vec,
                  pl.BlockSpec((None, D_CONV, D_CONV), lambda i: (l, 0, 0)), ANY_SPEC],
        out_specs=(pl.BlockSpec((ts, 2 * D_CONV), lambda i: (rev(i), D_QKV // (2 * D_CONV))), wspec, vec, vec, vec, sq),
        scratch_shapes=[pltpu.VMEM((CONV_HALO + ts, D_CONV), F32), pltpu.VMEM((ts + CONV_HALO, D_CONV), F32)],
        input_output_aliases={10: 0},
        compiler_params=_params('arbitrary'),
    )(proj_a, proj_a, proj_a, proj_a, d_mix, dw_w, dw_b, ln_g, ln_b, pw_w, d_proj)


def _pool_window():
    lane = lax.broadcasted_iota(jnp.int32, (1, D_POOL), 1)
    w = jnp.full((1, D_POOL), POOL_WINDOWS[0], jnp.int32)
    for g in range(1, len(POOL_WINDOWS)):
        w = jnp.where(lane // POOL_GROUP == g, POOL_WINDOWS[g], w)
    return w


def _pool_diff(buf, u_ref, uh_ref, first, tile, ts):
    buf[0:POOL_HALO, :] = jnp.where(first, 0.0, uh_ref[...])
    u = u_ref[...]
    buf[POOL_HALO:POOL_HALO + ts, :] = u
    wl = _pool_window()
    acc = u
    for j in range(1, max(POOL_WINDOWS)):
        acc = acc + jnp.where(j < wl, buf[pl.ds(POOL_HALO - j, ts), :], 0.0)
    pos = tile * ts + lax.broadcasted_iota(jnp.int32, (ts, 1), 0)
    cnt = jnp.minimum(pos + 1, wl).astype(F32)
    return acc / cnt - u, cnt


def _pool_specs(ts, tmap):
    hb = ts // POOL_HALO
    cur = pl.BlockSpec((ts, D_POOL), lambda i: (tmap(i), P_COL))
    halo = pl.BlockSpec((POOL_HALO, D_POOL), lambda i: (jnp.maximum(tmap(i) * hb - 1, 0), P_COL))
    return cur, halo


def _pool_fwd(proj_a, mix, wbd, scale, *, name, ts=512):
    S = proj_a.shape[0]
    ts = _tile(S, ts, POOL_HALO)

    def body(u_ref, uh_ref, w_ref, s_ref, mix_in, o_ref, buf):
        i = pl.program_id(0)
        d, _ = _pool_diff(buf, u_ref, uh_ref, i == 0, i, ts)
        o_ref[...] = (_dot(d.astype(BF16), w_ref[...], 1, 0) * s_ref[...]).astype(BF16)

    cur, halo = _pool_specs(ts, lambda i: i)
    return pl.pallas_call(
        body, name=name, out_shape=jax.ShapeDtypeStruct(mix.shape, BF16), grid=(S // ts,),
        in_specs=[cur, halo, pl.BlockSpec((D_POOL, D_POOL), lambda i: (0, 0)), pl.BlockSpec((1, D_POOL), lambda i: (0, 0)),
                  ANY_SPEC],
        out_specs=pl.BlockSpec((ts, D_POOL), lambda i: (i, (D_ATT + D_CONV) // D_POOL)),
        scratch_shapes=[pltpu.VMEM((POOL_HALO + ts, D_POOL), F32)],
        input_output_aliases={4: 0},
        compiler_params=_params('parallel'),
    )(proj_a, proj_a, wbd, scale, mix)


def _pool_bwd(proj_a, d_mix, d_proj, wbd, scale, *, name, ts=512):
    S = proj_a.shape[0]
    ts = _tile(S, ts, POOL_HALO)
    n = S // ts
    d_col = (D_ATT + D_CONV) // D_POOL

    def body(u_ref, uh_ref, dy_ref, w_ref, s_ref, dp_in, o_ref, dw_ref, ds_ref, buf, ebuf):
        i = pl.program_id(0)
        tile = n - 1 - i
        d, cnt = _pool_diff(buf, u_ref, uh_ref, tile == 0, tile, ts)
        db = d.astype(BF16)
        ypre = _dot(db, w_ref[...], 1, 0)
        dout = dy_ref[...]
        d_y = (dout * s_ref[...]).astype(BF16)
        d_d = _dot(d_y, w_ref[...], 1, 1)

        @pl.when(i == 0)
        def _():
            ebuf[ts:ts + POOL_HALO, :] = jnp.zeros((POOL_HALO, D_POOL), F32)
            dw_ref[...] = jnp.zeros(dw_ref.shape, F32)
            ds_ref[...] = jnp.zeros(ds_ref.shape, F32)

        dw_ref[...] += _dot(db, d_y, 0, 0)
        ds_ref[...] += jnp.sum(dout * ypre, axis=0, keepdims=True)
        e = d_d / cnt
        ebuf[0:ts, :] = e
        wl = _pool_window()
        acc = e
        for j in range(1, max(POOL_WINDOWS)):
            acc = acc + jnp.where(j < wl, ebuf[pl.ds(j, ts), :], 0.0)
        ebuf[ts:ts + POOL_HALO, :] = e[0:POOL_HALO, :]
        o_ref[...] = (acc - d_d).astype(BF16)

    rev = lambda i: n - 1 - i
    cur, halo = _pool_specs(ts, rev)
    sq = pl.BlockSpec((D_POOL, D_POOL), lambda i: (0, 0))
    vec = pl.BlockSpec((1, D_POOL), lambda i: (0, 0))
    return pl.pallas_call(
        body, name=name,
        out_shape=(jax.ShapeDtypeStruct(d_proj.shape, BF16), jax.ShapeDtypeStruct((D_POOL, D_POOL), F32),
                   jax.ShapeDtypeStruct((1, D_POOL), F32)),
        grid=(n,),
        in_specs=[cur, halo, pl.BlockSpec((ts, D_POOL), lambda i: (rev(i), d_col)), sq, vec, ANY_SPEC],
        out_specs=(pl.BlockSpec((ts, D_POOL), lambda i: (rev(i), P_COL)), sq, vec),
        scratch_shapes=[pltpu.VMEM((POOL_HALO + ts, D_POOL), F32), pltpu.VMEM((ts + POOL_HALO, D_POOL), F32)],
        input_output_aliases={5: 0},
        compiler_params=_params('arbitrary'),
    )(proj_a, proj_a, d_mix, wbd, scale, d_proj)


FFN_LANES = 128
FFN_GROUP = 8 * 8


def _ffn_rows(ref, c, row0, j):
    return ref.at[c][pl.ds(row0 + j, 8, stride=8), :]


def _ffn_specs(ts, tc2, tmap, l):
    hb = ts // FFN_HALO
    cur = pl.BlockSpec((ts, tc2), lambda c, i: (tmap(i), c))
    halo = pl.BlockSpec((FFN_HALO, tc2), lambda c, i: (jnp.maximum(tmap(i) * hb - 1, 0), c))
    wspec = pl.BlockSpec((None, FFN_HALO, tc2), lambda c, i: (l, 0, c))
    return cur, halo, wspec


def _ffn_fill(buf, x_ref, xh_ref, first, ts, nblk):
    for c in range(nblk):
        cs = slice(c * FFN_LANES, (c + 1) * FFN_LANES)
        buf[c, 0:FFN_HALO, :] = jnp.where(first, 0.0, xh_ref[:, cs])
        buf[c, FFN_HALO:FFN_HALO + ts, :] = x_ref[:, cs]


def _ffn_conv_piece(buf, w_ref, r0, c):
    ws = [w_ref[k:k + 1, c * FFN_LANES:(c + 1) * FFN_LANES] for k in range(FFN_CONV_WIDTH)]
    xs = [_ffn_rows(buf, c, FFN_HALO + r0, j) for j in range(1 - FFN_CONV_WIDTH, 8)]
    outs = []
    for j in range(8):
        acc = ws[0] * xs[j]
        for k in range(1, FFN_CONV_WIDTH):
            acc = acc + ws[k] * xs[j + k]
        outs.append(acc)
    return outs, xs


def _ffn_act_fwd(up, w, l, *, name, ts=256):
    S, F2 = up.shape
    tc = F2 // 4
    nb = tc // FFN_LANES
    ts = _tile(S, ts, FFN_GROUP)

    def body(x_ref, xh_ref, w_ref, o_ref, buf, stage):
        _ffn_fill(buf, x_ref, xh_ref, pl.program_id(1) == 0, ts, 2 * nb)
        for c in range(nb):
            for r0 in range(0, ts, FFN_GROUP):
                gates, _ = _ffn_conv_piece(buf, w_ref, r0, c)
                vals, _ = _ffn_conv_piece(buf, w_ref, r0, nb + c)
                for j in range(8):
                    stage.at[c][pl.ds(r0 + j, 8, stride=8), :] = gates[j] * _sigmoid(gates[j]) * vals[j]
            o_ref[:, c * FFN_LANES:(c + 1) * FFN_LANES] = stage[c].astype(BF16)

    cur, halo, wspec = _ffn_specs(ts, 2 * tc, lambda i: i, l)
    return pl.pallas_call(
        body, name=name, out_shape=jax.ShapeDtypeStruct((S, F2 // 2), BF16), grid=(2, S // ts),
        in_specs=[cur, halo, wspec],
        out_specs=pl.BlockSpec((ts, tc), lambda c, i: (i, c)),
        scratch_shapes=[pltpu.VMEM((2 * nb, FFN_HALO + ts, FFN_LANES), F32), pltpu.VMEM((nb, ts, FFN_LANES), F32)],
        compiler_params=_params('parallel', 'parallel'),
    )(up, up, w)


def _ffn_act_bwd(up, d_act, w, l, *, name, ts=256):
    S, F2 = up.shape
    tc = F2 // 4
    nb = tc // FFN_LANES
    ts = _tile(S, ts, FFN_GROUP)
    n = S // ts

    def body(x_ref, xh_ref, da_ref, w_ref, o_ref, dw_ref, buf, dcbuf, stage):
        i = pl.program_id(1)
        _ffn_fill(buf, x_ref, xh_ref, i == n - 1, ts, 2 * nb)

        @pl.when(i == 0)
        def _():
            dcbuf[:, ts:ts + FFN_HALO, :] = jnp.zeros((2 * nb, FFN_HALO, FFN_LANES), F32)
            dw_ref[...] = jnp.zeros(dw_ref.shape, F32)

        for c in range(nb):
            blocks = (c, nb + c)
            stage[c, :, :] = da_ref[:, c * FFN_LANES:(c + 1) * FFN_LANES]
            dws = [[jnp.zeros((8, FFN_LANES), F32) for _ in range(FFN_CONV_WIDTH)] for _ in range(2)]
            for r0 in range(0, ts, FFN_GROUP):
                gates, xg = _ffn_conv_piece(buf, w_ref, r0, blocks[0])
                vals, xv = _ffn_conv_piece(buf, w_ref, r0, blocks[1])
                for j in range(8):
                    sg = _sigmoid(gates[j])
                    da = _ffn_rows(stage, c, r0, j)
                    d_cs = (da * vals[j] * (sg * (1.0 + gates[j] * (1.0 - sg))), da * (gates[j] * sg))
                    for half, (d_c, xs) in enumerate(zip(d_cs, (xg, xv))):
                        dcbuf.at[blocks[half]][pl.ds(r0 + j, 8, stride=8), :] = d_c
                        for k in range(FFN_CONV_WIDTH):
                            dws[half][k] = dws[half][k] + d_c * xs[j + k]
            for half in range(2):
                cs = slice(blocks[half] * FFN_LANES, (blocks[half] + 1) * FFN_LANES)
                for k in range(FFN_CONV_WIDTH):
                    dw_ref[k:k + 1, cs] += jnp.sum(dws[half][k], axis=0, keepdims=True)
            for b in blocks:
                cs = slice(b * FFN_LANES, (b + 1) * FFN_LANES)
                ws = [w_ref[k:k + 1, cs] for k in range(FFN_CONV_WIDTH)]
                for r0 in range(0, ts, FFN_GROUP):
                    ds = [_ffn_rows(dcbuf, b, r0, j) for j in range(8 + FFN_CONV_WIDTH - 1)]
                    for j in range(8):
                        d_x = ws[FFN_CONV_WIDTH - 1] * ds[j]
                        for k in range(FFN_CONV_WIDTH - 1):
                            d_x = d_x + ws[k] * ds[j + FFN_CONV_WIDTH - 1 - k]
                        stage.at[c][pl.ds(r0 + j, 8, stride=8), :] = d_x
                o_ref[:, cs] = stage[c].astype(BF16)
                dcbuf[b, ts:ts + FFN_HALO, :] = dcbuf[b, 0:FFN_HALO, :]

    rev = lambda i: n - 1 - i
    cur, halo, wspec = _ffn_specs(ts, 2 * tc, rev, l)
    return pl.pallas_call(
        body, name=name,
        out_shape=(jax.ShapeDtypeStruct((S, F2), BF16), jax.ShapeDtypeStruct((FFN_HALO, F2), F32)),
        grid=(2, n),
        in_specs=[cur, halo, pl.BlockSpec((ts, tc), lambda c, i: (rev(i), c)), wspec],
        out_specs=(cur, pl.BlockSpec((FFN_HALO, 2 * tc), lambda c, i: (0, c))),
        scratch_shapes=[pltpu.VMEM((2 * nb, FFN_HALO + ts, FFN_LANES), F32), pltpu.VMEM((2 * nb, ts + FFN_HALO, FFN_LANES), F32),
                        pltpu.VMEM((nb, ts, FFN_LANES), F32)],
        compiler_params=_params('parallel', 'arbitrary'),
    )(up, up, d_act, w)


def _loss_head(y, target, *, name, ts=512):
    S, D = y.shape
    ts = _tile(S, ts, 8)

    def body(y_ref, t_ref, l_ref, dy_ref):
        i = pl.program_id(0)
        err = y_ref[...] - t_ref[...]
        dy_ref[...] = err * (1.0 / D)
        part = jnp.sum(jnp.sum(err * err, axis=1, keepdims=True), axis=0, keepdims=True) * (0.5 / D)

        @pl.when(i == 0)
        def _():
            l_ref[...] = part

        @pl.when(i > 0)
        def _():
            l_ref[...] += part

    row = pl.BlockSpec((ts, D), lambda i: (i, 0))
    return pl.pallas_call(
        body, name=name,
        out_shape=(jax.ShapeDtypeStruct((1, 1), F32), jax.ShapeDtypeStruct((S, D), F32)),
        grid=(S // ts,), in_specs=[row, row], out_specs=(pl.BlockSpec((1, 1), lambda i: (0, 0)), row),
        compiler_params=_params('arbitrary'),
    )(y, target)


def _pair_cols(w):
    lead, f2 = w.shape[:-1], w.shape[-1]
    return w.reshape(lead + (2, 2, f2 // 4)).swapaxes(-3, -2).reshape(lead + (f2,))


def _pad_axis(w, size, axis):
    pad = [(0, 0)] * w.ndim
    pad[axis] = (0, size - w.shape[axis])
    return jnp.pad(w, pad)


def _block_diag(pool_w):
    g = pool_w.shape[0]
    rows = [jnp.concatenate([pool_w[i] if i == j else jnp.zeros_like(pool_w[i]) for j in range(g)], axis=1) for i in range(g)]
    return jnp.concatenate(rows, axis=0)


def _small_weights(w, l):
    return dict(
        norm1_g=w['norm1_g'][l][None, :],
        b_col=_pad_axis(w['b_f'][l][:, None], FG_ROWS, 0),
        qg=jnp.tile(w['q_norm_g'][l], N_HEADS)[None, :],
        kg=jnp.tile(w['k_norm_g'][l], N_HEADS)[None, :],
        dw_b=w['conv_dw_b'][l][None, :], ln_g=w['conv_ln_g'][l][None, :], ln_b=w['conv_ln_b'][l][None, :],
        wbd=_block_diag(w['pool_w'][l]).astype(BF16),
        pool_scale=w['pool_scale'][l][None, :],
        norm2_g=w['norm2_g'][l][None, :],
    )


def _layer_fwd(x, W, p, l):
    n = lambda s: f'l{l}_{s}'
    S = x.shape[0]
    h = _rms_fwd(x, p['norm1_g'], name=n('norm1'))
    proj_a = _mm(h, W.get('w_a', h), b_lead=0, name=n('proj_a'), tn=D_PROJ_A)
    z_raw = _mm(W.get('w_fg_t', h), h, a_lead=0, tb=True, name=n('proj_fg'))
    qkv = _qk_prep_fwd(proj_a, p['qg'], p['kg'], name=n('qk_norm'))
    f_cum = _forget_fwd(z_raw, p['b_col'], name=n('forget'))
    f3 = f_cum[:N_HEADS].reshape(N_HEADS // 2, 2, S)
    mix, att, lse = _attn_fwd(*_attn_aug(qkv, f_cum, name=n('attn_aug')), name=n('attn'))
    mix = _conv_fwd(proj_a, mix, W.get('dw_w', h), p['dw_b'], p['ln_g'], p['ln_b'], W.get('pw_w', h), 0, name=n('conv'))
    mix = _pool_fwd(proj_a, mix, p['wbd'], p['pool_scale'], name=n('pool'))
    x1 = _mm(mix, W.get('w_out', mix), b_lead=0, res=x, name=n('out_proj'), tn=1024)
    h2 = _rms_fwd(x1, p['norm2_g'], name=n('norm2'))
    up = _mm(h2, W.get('w_up', mix), b_lead=0, name=n('up_proj'), tn=1408, cols_outer=True)
    act = _ffn_act_fwd(up, W.get('ffn_w', h), 0, name=n('ffn_act'))
    x2 = _mm(act, W.get('w_down', mix), b_lead=0, res=x1, name=n('down_proj'), tn=1024, tk=2816)
    saved = dict(x=x, h=h, proj_a=proj_a, z_raw=z_raw, qkv=qkv, f3=f3, att=att, lse=lse, mix=mix, x1=x1, h2=h2, up=up, act=act)
    return x2, saved


def _layer_bwd(dx2, W, p, s, l, sink):
    n = lambda t: f'l{l}_{t}'
    S = dx2.shape[0]
    g = {}
    W = W.ready

    def large(key, a, b, **kw):
        return sink.put(l, key, *_mm(a, b, ta=True, copy16=True, name=n('d_' + key), **kw))

    d_act = _mm(dx2, W['w_down'], b_lead=0, tb=True, name=n('d_act'), tn=1408, cols_outer=True)
    large('w_down', s['act'], dx2, tm=1408, tn=1024)
    d_up, d_ffn_w = _ffn_act_bwd(s['up'], d_act, W['ffn_w'], 0, name=n('ffn_act_bwd'))
    g['ffn_dw_w'] = _pair_cols(d_ffn_w[:FFN_CONV_WIDTH])
    d_h2 = _mm(d_up, W['w_up'], b_lead=0, tb=True, name=n('d_h2'), tn=1024, tk=5632)
    started = large('w_up', s['h2'], d_up, tm=1024, tn=512, tk=4096)
    dx1, dg2 = _rms_bwd(s['x1'], p['norm2_g'], d_h2, dx2, name=n('norm2_bwd'))
    g['norm2_g'] = dg2[0]
    sink.point(l, 'mid', dx1)
    d_mix = _mm(dx1, W['w_out'], b_lead=0, tb=True, name=n('d_mix'), tn=1024, after=started)
    large('w_out', s['mix'], dx1, tm=1024, tn=1024)
    dq, dk, dv, df3, dr = _attn_bwd(s['qkv'], s['f3'], s['att'], s['lse'], d_mix, name=n('attn_bwd'))
    df = _pad_axis(df3.reshape(N_HEADS, S) + dr[:, ::HEAD_DIM].T, FG_ROWS, 0)
    d_z, d_b = _forget_bwd(s['z_raw'], p['b_col'], df, name=n('forget_bwd'))
    g['b_f'] = d_b[:N_HEADS, 0]
    d_proj, d_qg, d_kg = _qk_prep_bwd(s['proj_a'], dq, dk, dv, p['qg'], p['kg'], name=n('qk_norm_bwd'))
    g['q_norm_g'] = d_qg.reshape(N_HEADS, HEAD_DIM).sum(axis=0)
    g['k_norm_g'] = d_kg.reshape(N_HEADS, HEAD_DIM).sum(axis=0)
    d_proj, d_dw_w, d_dw_b, d_ln_g, d_ln_b, d_pw = _conv_bwd(
        s['proj_a'], d_mix, d_proj, W['dw_w'], p['dw_b'], p['ln_g'], p['ln_b'], W['pw_w'], 0, name=n('conv_bwd'))
    g['conv_dw_w'], g['conv_dw_b'] = d_dw_w[:CONV_WIDTH], d_dw_b[0]
    g['conv_ln_g'], g['conv_ln_b'], g['conv_pw_w'] = d_ln_g[0], d_ln_b[0], d_pw
    d_proj, d_wbd, d_scale = _pool_bwd(s['proj_a'], d_mix, d_proj, p['wbd'], p['pool_scale'], name=n('pool_bwd'))
    g['pool_w'] = jnp.stack([d_wbd[i * POOL_GROUP:(i + 1) * POOL_GROUP, i * POOL_GROUP:(i + 1) * POOL_GROUP]
                             for i in range(len(POOL_WINDOWS))])
    g['pool_scale'] = d_scale[0]
    d_w_a = _mm(s['h'], d_proj, ta=True, name=n('d_w_a'), tm=1024, tn=768, tk=4096)
    started = sink.put(l, 'w_in', d_w_a, _mm(d_z, s['h'], name=n('d_w_fg'), tn=1024).T)
    d_h_fg = _mm(d_z, W['w_fg_t'], b_lead=0, ta=True, name=n('d_h_fg'), tn=1024, after=started)
    d_h = _mm(d_proj, W['w_a'], b_lead=0, tb=True, res=d_h_fg, name=n('d_h'), tn=1024, tk=D_PROJ_A)
    dx, dg1 = _rms_bwd(s['x'], p['norm1_g'], d_h, dx1, name=n('norm1_bwd'))
    g['norm1_g'] = dg1[0]
    sink.point(l, 'end', dx)
    return dx, g


SMALL_GRADS = REPLICATED + ('conv_dw_w', 'conv_pw_w', 'ffn_dw_w')


def _local_step(x, target, W, w_small, sink):
    depth = w_small['norm1_g'].shape[0]
    ps, saved = [], []
    for l in range(depth):
        p = _small_weights(w_small, l)
        x, s = _layer_fwd(x, W[l], p, l)
        ps.append(p)
        saved.append(s)
    loss, dx = _loss_head(x, target, name='loss_head')
    small = [None] * depth
    for l in reversed(range(depth)):
        dx, small[l] = _layer_bwd(dx, W[l], ps[l], saved[l], l, sink)
    return loss, dx, {k: jnp.stack([small[l][k] for l in range(depth)]) for k in SMALL_GRADS}


W_IN_SHARD = D_IN // N_CHIPS
W_IN_PAD = 640
N_A_TILES = D_PROJ_A // LANES
FG_COL0 = D_QKV


def _a_tile_base(j):
    if j == N_A_TILES:
        return FG_COL0, N_HEADS
    return (j * LANES if j * LANES < FG_COL0 else j * LANES + N_HEADS), LANES


def _shift_select(rows, cols, shift, row_max, col_max):
    r = lax.broadcasted_iota(jnp.int32, (rows, cols), 0)
    c = lax.broadcasted_iota(jnp.int32, (rows, cols), 1)
    return ((r + shift == c) & (r < row_max) & (c < col_max)).astype(BF16)


def _select_w_in(raw, *, name, tm=256):
    _, D, _ = raw.shape
    tm = _tile(D, tm, 16)
    plan = []
    for j in range(N_A_TILES + 1):
        base, cmax = _a_tile_base(j)
        parts = []
        for p in range(N_CHIPS):
            delta = base - W_IN_SHARD * p
            lo, hi = max(0, delta), min(W_IN_SHARD - 1, delta + cmax - 1)
            if lo > hi:
                continue
            a0 = (lo // LANES) * LANES
            kw = min(-(-(hi + 1 - a0) // LANES) * LANES, W_IN_PAD - a0)
            parts.append((p, a0, kw, delta))
        plan.append((cmax, parts))

    def body(raw_ref, wa_ref, fg_ref):
        for j, (cmax, parts) in enumerate(plan):
            acc = None
            for p, a0, kw, delta in parts:
                sel = _shift_select(kw, LANES, a0 - delta, W_IN_SHARD - a0, cmax)
                t = _dot(raw_ref[p, :, a0:a0 + kw], sel, 1, 0)
                acc = t if acc is None else acc + t
            if j == N_A_TILES:
                fg_ref[...] = acc.astype(BF16)
            else:
                wa_ref[:, j * LANES:(j + 1) * LANES] = acc.astype(BF16)

    return pl.pallas_call(
        body, name=name,
        out_shape=(jax.ShapeDtypeStruct((D, D_PROJ_A), BF16), jax.ShapeDtypeStruct((D, LANES), BF16)),
        grid=(D // tm,),
        in_specs=[pl.BlockSpec((N_CHIPS, tm, W_IN_PAD), lambda i: (0, i, 0))],
        out_specs=(pl.BlockSpec((tm, D_PROJ_A), lambda i: (i, 0)), pl.BlockSpec((tm, LANES), lambda i: (i, 0))),
        compiler_params=_params('parallel'),
    )(raw)


def _select_w_in_grads(p_a, p_fg, *, name, tm=256):
    D = p_a.shape[0]
    tm = _tile(D, tm, 16)
    n_local = W_IN_PAD // LANES
    plan = []
    for p in range(N_CHIPS):
        for i in range(n_local):
            cmax = max(0, min(LANES, W_IN_SHARD - i * LANES))
            parts = []
            for j in range(N_A_TILES + 1):
                base, rmax = _a_tile_base(j)
                e = base - W_IN_SHARD * p - i * LANES
                if e + rmax - 1 < 0 or e > cmax - 1:
                    continue
                parts.append((j, e, rmax))
            plan.append((p, i, cmax, parts))

    def body(a_ref, fg_ref, o32_ref, o16_ref):
        terms = {}

        def src(j):
            if j not in terms:
                v = fg_ref[...] if j == N_A_TILES else a_ref[:, j * LANES:(j + 1) * LANES]
                terms[j] = _split3(v)
            return terms[j]

        for p, i, cmax, parts in plan:
            acc = jnp.zeros((tm, LANES), F32)
            for j, e, rmax in parts:
                sel = _shift_select(LANES, LANES, e, rmax, cmax)
                for term in src(j):
                    acc = acc + _dot(term, sel, 1, 0)
            o32_ref[p, :, i * LANES:(i + 1) * LANES] = acc
            o16_ref[p, :, i * LANES:(i + 1) * LANES] = acc.astype(BF16)

    out = pl.BlockSpec((N_CHIPS, tm, W_IN_PAD), lambda i: (0, i, 0))
    return pl.pallas_call(
        body, name=name,
        out_shape=(jax.ShapeDtypeStruct((N_CHIPS, D, W_IN_PAD), F32), jax.ShapeDtypeStruct((N_CHIPS, D, W_IN_PAD), BF16)),
        grid=(D // tm,),
        in_specs=[pl.BlockSpec((tm, D_PROJ_A), lambda i: (i, 0)), pl.BlockSpec((tm, LANES), lambda i: (i, 0))],
        out_specs=(out, out),
        compiler_params=_params('parallel'),
    )(p_a, p_fg)


MESH = pl.DeviceIdType.MESH
HBM_SPEC = pl.BlockSpec(memory_space=pltpu.HBM)


def _place():
    return lax.axis_index('x'), lax.axis_index('y'), lax.axis_index('c')


def _other_chips(x, y):
    return [(1 - x, y), (x, 1 - y), (1 - x, 1 - y)]


def _up_pos(q):
    return (q % 2) * 2 + q // 2


CHUNKS = {
    'w_in': ('lead', None),
    'w_up': ('cols', None),
    'w_down': ('rows', None),
    'w_out': ('rows', None),
    'conv_pw_w': ('rows', None),
    'conv_dw_w': ('lead', None),
    'ffn_dw_w': ('lead', None),
}


def _window(ref, kind, l, q):
    at = (lambda *idx: ref.at[idx]) if l is None else (lambda *idx: ref.at[(l,) + idx])
    shape = ref.shape if l is None else ref.shape[1:]
    if kind == 'lead':
        return at(q)
    if kind == 'rows':
        cs = shape[0] // N_CHIPS
        return at(pl.ds(pl.multiple_of(q * cs, 16), cs), slice(None))
    cs = shape[1] // N_CHIPS
    return at(slice(None), pl.ds(pl.multiple_of(_up_pos(q) * cs, LANES), cs))


def _place_shard(src, l, pos_arr, full_shape, kind, *, name, tm=256):
    _, m, n = src.shape
    bm = _tile(m, tm, 16) if kind != 'rows' else m

    def body(pos_ref, s_ref, o_ref):
        o_ref[...] = s_ref[...].astype(BF16)

    if kind == 'lead':
        out = pl.BlockSpec((None, bm, n), lambda i, pos: (pos[0], i, 0))
    elif kind == 'rows':
        out = pl.BlockSpec((bm, n), lambda i, pos: (pos[0], 0))
    else:
        out = pl.BlockSpec((bm, n), lambda i, pos: (i, pos[0]))
    return pl.pallas_call(
        body, name=name, out_shape=jax.ShapeDtypeStruct(full_shape, BF16),
        grid_spec=pltpu.PrefetchScalarGridSpec(
            num_scalar_prefetch=1, grid=(m // bm,),
            in_specs=[pl.BlockSpec((None, bm, n), lambda i, pos: (l, i, 0))], out_specs=out),
        compiler_params=_params('parallel'),
    )(pos_arr, src)


GATHERED = ('w_in', 'w_up', 'w_down', 'w_out', 'conv_pw_w', 'conv_dw_w', 'ffn_dw_w')
GATHER_GROUPS = ((0, ('w_in', 'conv_dw_w', 'ffn_dw_w', 'conv_pw_w')), (0, ('w_out', 'w_up', 'w_down')), (1, GATHERED))
SEM_SPEC = pl.BlockSpec(memory_space=pltpu.SEMAPHORE)
SPLIT_COPY_PARAMS = pltpu.CompilerParams(has_side_effects=pltpu.SideEffectType.DATAFLOW_SIDE_EFFECTING)


def _gather_start(bufs):
    flat = [b for group in bufs for b in group]
    nb = len(flat)

    def body(*refs):
        outs, sems = refs[nb:2 * nb], refs[2 * nb:]
        x, y, c = _place()
        pos = 0
        for g, (_, keys) in enumerate(GATHER_GROUPS):
            for i, k in enumerate(keys):
                w = _window(outs[pos], CHUNKS[k][0], None, 2 * x + y)
                pos += 1
                for j, chip in enumerate(_other_chips(x, y)):
                    pltpu.make_async_remote_copy(src_ref=w, dst_ref=w, send_sem=sems[2 * g].at[3 * i + j],
                                                 recv_sem=sems[2 * g + 1].at[3 * i + j], device_id=(*chip, c),
                                                 device_id_type=MESH).start()

    sem_shapes = [pltpu.SemaphoreType.DMA((3 * len(keys),)) for _, keys in GATHER_GROUPS for _ in range(2)]
    res = pl.pallas_call(
        body, name='gather_start',
        out_shape=tuple(jax.ShapeDtypeStruct(b.shape, b.dtype) for b in flat) + tuple(sem_shapes),
        in_specs=[HBM_SPEC] * nb, out_specs=tuple([HBM_SPEC] * nb + [SEM_SPEC] * len(sem_shapes)),
        input_output_aliases={b: b for b in range(nb)},
        compiler_params=SPLIT_COPY_PARAMS,
    )(*[pltpu.with_memory_space_constraint(b, pltpu.HBM) for b in flat])
    out_bufs, sems, pos = [], res[nb:], 0
    for group in bufs:
        out_bufs.append(list(res[pos:pos + len(group)]))
        pos += len(group)
    return out_bufs, [(sems[2 * g], sems[2 * g + 1]) for g in range(len(GATHER_GROUPS))]


def _gather_wait(g, bufs, sems, after):
    keys = GATHER_GROUPS[g][1]
    nb = len(bufs)

    def body(*refs):
        send_sems, recv_sems = refs[nb], refs[nb + 1]
        outs = refs[nb + 3:]
        x, y, c = _place()
        for i, k in enumerate(keys):
            mine = _window(outs[i], CHUNKS[k][0], None, 2 * x + y)
            for j, (cx, cy) in enumerate(_other_chips(x, y)):
                theirs = _window(outs[i], CHUNKS[k][0], None, 2 * cx + cy)
                cp = pltpu.make_async_remote_copy(src_ref=mine, dst_ref=theirs, send_sem=send_sems.at[3 * i + j],
                                                  recv_sem=recv_sems.at[3 * i + j], device_id=(cx, cy, c), device_id_type=MESH)
                cp.wait_send()
                cp.wait_recv()

    return pl.pallas_call(
        body, name=f'gather_wait_{g}',
        out_shape=tuple(jax.ShapeDtypeStruct(b.shape, b.dtype) for b in bufs),
        in_specs=[HBM_SPEC] * nb + [SEM_SPEC, SEM_SPEC, ANY_SPEC], out_specs=tuple([HBM_SPEC] * nb),
        input_output_aliases={b: b for b in range(nb)},
        compiler_params=SPLIT_COPY_PARAMS,
    )(*bufs, *sems, after)


def _rs_block(M, N):
    return (_tile(M, 256, 16), _tile(N, 2048))


def _chunk_shape(shape, kind):
    if kind == 'lead':
        return tuple(shape[1:])
    if kind == 'rows':
        return (shape[0] // N_CHIPS, shape[1])
    return (shape[0], shape[1] // N_CHIPS)


def _rs_start(tag, bufs, kinds):
    nb = len(bufs)
    lands = [lax.empty((N_CHIPS - 1,) + _chunk_shape(b.shape, k), b.dtype) for b, k in zip(bufs, kinds)]

    def body(*refs):
        src, land = refs[2 * nb:3 * nb], refs[3 * nb:4 * nb]
        send_sems, recv_sems, token = refs[4 * nb:]
        token[...] = jnp.zeros(token.shape, F32)
        x, y, c = _place()
        for b in range(nb):
            for j, (cx, cy) in enumerate(_other_chips(x, y)):
                pltpu.make_async_remote_copy(
                    src_ref=_window(src[b], kinds[b], None, 2 * cx + cy), dst_ref=land[b].at[j],
                    send_sem=send_sems.at[3 * b + j], recv_sem=recv_sems.at[3 * b + j],
                    device_id=(cx, cy, c), device_id_type=MESH).start()

    sem = pltpu.SemaphoreType.DMA((3 * nb,))
    res = pl.pallas_call(
        body, name=f'rs_start_{tag}',
        out_shape=tuple(jax.ShapeDtypeStruct(b.shape, b.dtype) for b in list(bufs) + lands)
        + (sem, sem, jax.ShapeDtypeStruct((8, LANES), F32)),
        in_specs=[HBM_SPEC] * (2 * nb),
        out_specs=tuple([HBM_SPEC] * (2 * nb) + [SEM_SPEC, SEM_SPEC, pl.BlockSpec(memory_space=pltpu.VMEM)]),
        input_output_aliases={b: b for b in range(2 * nb)},
        compiler_params=SPLIT_COPY_PARAMS,
    )(*[pltpu.with_memory_space_constraint(b, pltpu.HBM) for b in list(bufs) + lands])
    return res[:nb], res[nb:2 * nb], res[2 * nb:2 * nb + 2], res[2 * nb + 2]


def _rs_wait(tag, bufs, lands, sems, kinds, after):
    nb = len(bufs)

    def body(*refs):
        send_sems, recv_sems = refs[2 * nb], refs[2 * nb + 1]
        src, land = refs[2 * nb + 3:3 * nb + 3], refs[3 * nb + 3:]
        x, y, c = _place()
        for b in range(nb):
            for j, (cx, cy) in enumerate(_other_chips(x, y)):
                cp = pltpu.make_async_remote_copy(
                    src_ref=_window(src[b], kinds[b], None, 2 * cx + cy), dst_ref=land[b].at[j],
                    send_sem=send_sems.at[3 * b + j], recv_sem=recv_sems.at[3 * b + j],
                    device_id=(cx, cy, c), device_id_type=MESH)
                cp.wait_send()
                cp.wait_recv()

    res = pl.pallas_call(
        body, name=f'rs_wait_{tag}',
        out_shape=tuple(jax.ShapeDtypeStruct(b.shape, b.dtype) for b in list(bufs) + list(lands)),
        in_specs=[HBM_SPEC] * (2 * nb) + [SEM_SPEC, SEM_SPEC, ANY_SPEC], out_specs=tuple([HBM_SPEC] * (2 * nb)),
        input_output_aliases={b: b for b in range(2 * nb)},
        compiler_params=SPLIT_COPY_PARAMS,
    )(*bufs, *lands, *sems, after)
    return res[nb:]


def _rs_sum(p, rb, kind, pos_arr, l, depth, buf, *, name):
    m, n = rb.shape[1:]
    bm, bn = _rs_block(m, n)
    nbm, nbn = m // bm, n // bn
    has_buf = buf is not None

    def body(q_ref, p_ref, r_ref, *rest):
        acc = p_ref[...]
        for j in range(N_CHIPS - 1):
            acc = acc + r_ref[j].astype(F32)
        rest[-1][...] = acc

    if kind == 'lead':
        p_map = lambda i, j, q: (q[0], i, j)
    elif kind == 'rows':
        p_map = lambda i, j, q: (q[0] * nbm + i, j)
    else:
        p_map = lambda i, j, q: (i, q[0] * nbn + j)
    r_spec = pl.BlockSpec((N_CHIPS - 1, bm, bn), lambda i, j, q: (0, i, j))
    p_spec = pl.BlockSpec(((None,) if kind == 'lead' else ()) + (bm, bn), p_map)
    return pl.pallas_call(
        body, name=name, out_shape=jax.ShapeDtypeStruct((depth, m, n), F32),
        grid_spec=pltpu.PrefetchScalarGridSpec(
            num_scalar_prefetch=1, grid=(nbm, nbn), in_specs=[p_spec, r_spec] + ([ANY_SPEC] if has_buf else []),
            out_specs=pl.BlockSpec((None, bm, bn), lambda i, j, q: (l, i, j))),
        input_output_aliases={3: 0} if has_buf else {},
        compiler_params=_params('parallel', 'parallel'),
    )(pos_arr, p, rb, *((buf,) if has_buf else ()))


def _swap_with_sibling(bufs):
    nb = len(bufs)

    def body(*refs):
        ins, outs = refs[:nb], refs[nb:2 * nb]
        send_sems, recv_sems = refs[2 * nb:]
        x, y, c = _place()
        copies = [pltpu.make_async_remote_copy(src_ref=ins[b], dst_ref=outs[b], send_sem=send_sems.at[b],
                                               recv_sem=recv_sems.at[b], device_id=(x, y, 1 - c), device_id_type=MESH)
                  for b in range(nb)]
        for cp in copies:
            cp.start()
        for cp in copies:
            cp.wait()

    return pl.pallas_call(
        body, name='rs_swap_sums', out_shape=tuple(jax.ShapeDtypeStruct(b.shape, b.dtype) for b in bufs),
        in_specs=[HBM_SPEC] * nb, out_specs=tuple([HBM_SPEC] * nb),
        scratch_shapes=[pltpu.SemaphoreType.DMA((nb,)), pltpu.SemaphoreType.DMA((nb,))],
    )(*bufs)


def _all_reduce_small(v):
    r = v.shape[0]

    def body(x_ref, tot_ref, all_ref, send_sems, recv_sems):
        x, y, c = _place()
        me, sibling = (x, y, c), (x, y, 1 - c)
        chips = _other_chips(x, y)

        def rows(px, py, pc):
            return all_ref.at[pl.ds((4 * px + 2 * py + pc) * r, r), :]

        def copy(k, block, to, src=None):
            return pltpu.make_async_remote_copy(
                src_ref=rows(*block) if src is None else src, dst_ref=rows(*block),
                send_sem=send_sems.at[k], recv_sem=recv_sems.at[k], device_id=to, device_id_type=MESH)

        rows(*me)[...] = x_ref[...]
        first = [copy(0, me, sibling, src=x_ref)]
        first += [copy(1 + j, me, (*chip, c), src=x_ref) for j, chip in enumerate(chips)]
        for cp in first:
            cp.start()
        passed = [copy(4 + j, (*chip, c), sibling) for j, chip in enumerate(chips)]
        for j, chip in enumerate(chips):
            copy(1 + j, (*chip, c), me).wait_recv()
            passed[j].start()
        copy(0, sibling, me).wait_recv()
        for j, chip in enumerate(chips):
            copy(4 + j, (*chip, 1 - c), me).wait_recv()
        for cp in first + passed:
            cp.wait_send()
        acc = all_ref[0:r, :]
        for d in range(1, N_DEV):
            acc = acc + all_ref[d * r:(d + 1) * r, :]
        tot_ref[...] = acc

    return pl.pallas_call(
        body, name='all_reduce_small', out_shape=jax.ShapeDtypeStruct((r, LANES), F32),
        in_specs=[pl.BlockSpec(memory_space=pltpu.VMEM)], out_specs=pl.BlockSpec(memory_space=pltpu.VMEM),
        scratch_shapes=[pltpu.VMEM((N_DEV * r, LANES), F32), pltpu.SemaphoreType.DMA((7,)), pltpu.SemaphoreType.DMA((7,))],
    )(v)


def _adamw(w, g, m, v, *, name, g2=None, ts=256):
    R, C = w.shape
    Cg = g.shape[1]
    ts = _tile(R, ts, 8)
    c1 = 1.0 - ADAM_B1 ** ADAM_STEP
    c2 = 1.0 - ADAM_B2 ** ADAM_STEP
    two = g2 is not None

    def body(w_ref, g_ref, *rest):
        m_ref, v_ref, go_ref, d_ref, nm_ref, nv_ref = rest[two:]
        gv = g_ref[:, 0:C]
        if two:
            gv = gv + rest[0][:, 0:C]
        nm = ADAM_B1 * m_ref[...] + (1.0 - ADAM_B1) * gv
        nv = ADAM_B2 * v_ref[...] + (1.0 - ADAM_B2) * (gv * gv)
        d_ref[...] = -ADAM_LR * ((nm / c1) / (jnp.sqrt(nv / c2) + ADAM_EPS) + ADAM_WD * w_ref[...])
        go_ref[...] = gv
        nm_ref[...] = nm
        nv_ref[...] = nv

    blk = pl.BlockSpec((ts, C), lambda i: (i, 0))
    gblk = pl.BlockSpec((ts, Cg), lambda i: (i, 0))
    shape = jax.ShapeDtypeStruct((R, C), F32)
    return pl.pallas_call(body, name=name, out_shape=(shape, shape, shape, shape), grid=(R // ts,),
                          in_specs=[blk, gblk] + ([gblk] if two else []) + [blk, blk], out_specs=(blk, blk, blk, blk),
                          compiler_params=_params('parallel'))(w, g, *((g2,) if two else ()), m, v)


def _pack_rows(parts, row_unit):
    flat = jnp.concatenate(parts)
    flat = _pad_axis(flat, -(-flat.shape[0] // (row_unit * LANES)) * row_unit * LANES, 0)
    return flat.reshape(-1, LANES)


def _as_2d(a):
    return a.reshape(-1, a.shape[-1])


def _mesh_place():
    cx, cy, cc = _place()
    chip = 2 * cx + cy
    as_arr = lambda v: jnp.reshape(v, (1,)).astype(jnp.int32)
    return chip, as_arr(cc), as_arr(chip), as_arr(_up_pos(chip))


class _LayerWeights:
    def __init__(self, groups):
        self.groups = groups
        self.ready = {}

    def get(self, name, after):
        if name not in self.ready:
            for names, wait in self.groups:
                if name in names:
                    self.ready.update({k: v[None] for k, v in wait(after).items()})
        return self.ready[name]


def _gather_full(w, place):
    chip, _, chip_arr, up_pos_arr = place
    L, D = w['w_in'].shape[:2]
    w_in_pad = _pad_axis(w['w_in'], W_IN_PAD, 2)

    def placed(k, l):
        if k == 'w_in':
            return _place_shard(w_in_pad, l, chip_arr, (N_CHIPS, D, W_IN_PAD), 'lead', name=f'place_w_in_{l}')
        if k == 'w_up':
            return _place_shard(w[k], l, up_pos_arr, (w[k].shape[1], N_CHIPS * w[k].shape[2]), 'cols', name=f'place_w_up_{l}')
        if k in ('conv_dw_w', 'ffn_dw_w'):
            return lax.dynamic_update_slice_in_dim(jnp.zeros((N_CHIPS,) + w[k].shape[1:], F32), w[k][l][None], chip, axis=0)
        return _place_shard(w[k], l, chip_arr, (N_CHIPS * w[k].shape[1], w[k].shape[2]), 'rows', name=f'place_{k}_{l}')

    bufs, sems = _gather_start([[placed(k, l) for k in keys] for l, keys in GATHER_GROUPS])
    unchunk = lambda a: jnp.moveaxis(a, 0, 1).reshape(a.shape[1], -1)

    def waiter(g):
        l, keys = GATHER_GROUPS[g]

        def wait(after):
            full = dict(zip(keys, _gather_wait(g, bufs[g], sems[g], after)))
            out = {}
            if 'w_in' in full:
                out['w_a'], w_fg = _select_w_in(full['w_in'], name=f'select_w_in_{l}')
                out['w_fg_t'] = w_fg.T
            if 'conv_dw_w' in full:
                out['dw_w'] = _pad_axis(unchunk(full['conv_dw_w']), CONV_HALO, 0)
            if 'ffn_dw_w' in full:
                out['ffn_w'] = _pad_axis(_pair_cols(unchunk(full['ffn_dw_w'])), FFN_HALO, 0)
            if 'conv_pw_w' in full:
                out['pw_w'] = full['conv_pw_w']
            out.update({k: full[k] for k in ('w_up', 'w_down', 'w_out') if k in full})
            return out

        names = {'w_in': ('w_a', 'w_fg_t'), 'conv_dw_w': ('dw_w',), 'ffn_dw_w': ('ffn_w',), 'conv_pw_w': ('pw_w',)}
        return tuple(n for k in keys for n in names.get(k, (k,))), wait

    return [_LayerWeights([waiter(g) for g in range(len(GATHER_GROUPS)) if GATHER_GROUPS[g][0] == l]) for l in range(L)]


RS_WIRE = ('w_in', 'w_up', 'w_down', 'w_out')
RS_GROUPS = (('ffn', ('w_down', 'w_up')), ('mix', ('w_out', 'w_in')))


class _GradReducer:
    def __init__(self, place, depth):
        _, _, self.chip_arr, self.up_pos_arr = place
        self.depth = depth
        self.got = {}
        self.flying = {}
        self.sums = {}

    def put(self, l, key, g32, g16):
        if key == 'w_in':
            g32, g16 = _select_w_in_grads(g32, g16, name=f'l{l}_select_w_in_grads')
        self.got[(l, key)] = (g32, g16)
        for tag, keys in RS_GROUPS:
            if key == keys[-1]:
                kinds = [CHUNKS[k][0] for k in keys]
                bufs, lands, sems, token = _rs_start(f'l{l}_{tag}', [self.got[(l, k)][1] for k in keys], kinds)
                self.flying[(l, tag)] = (bufs, lands, sems, kinds)
                return token
        return None

    def point(self, l, where, after):
        if where == 'mid':
            self._land(l + 1, 'mix', after)
        else:
            self._land(l, 'ffn', after)

    def _land(self, l, tag, after):
        if (l, tag) not in self.flying:
            return
        bufs, lands, sems, kinds = self.flying.pop((l, tag))
        lands = _rs_wait(f'l{l}_{tag}', bufs, lands, sems, kinds, after)
        for k, rb, kind in zip(dict(RS_GROUPS)[tag], lands, kinds):
            pos = self.up_pos_arr if kind == 'cols' else self.chip_arr
            self.sums[k] = _rs_sum(self.got.pop((l, k))[0], rb, kind, pos, l, self.depth, self.sums.get(k), name=f'l{l}_rs_sum_{k}')

    def finish(self, after):
        for l, tag in list(self.flying):
            self._land(l, tag, after)
        mine = [self.sums[k] for k in RS_WIRE]
        return {k: pair for k, pair in zip(RS_WIRE, zip(mine, _swap_with_sibling(mine)))}


def kernel(x, norm1_g, w_in, b_f, q_norm_g, k_norm_g, conv_dw_w, conv_dw_b, conv_ln_g, conv_ln_b, conv_pw_w, pool_w, pool_scale, w_out, norm2_g, w_up, ffn_dw_w, w_down, loss_target, m_norm1_g, m_w_in, m_b_f, m_q_norm_g, m_k_norm_g, m_conv_dw_w, m_conv_dw_b, m_conv_ln_g, m_conv_ln_b, m_conv_pw_w, m_pool_w, m_pool_scale, m_w_out, m_norm2_g, m_w_up, m_ffn_dw_w, m_w_down, v_norm1_g, v_w_in, v_b_f, v_q_norm_g, v_k_norm_g, v_conv_dw_w, v_conv_dw_b, v_conv_ln_g, v_conv_ln_b, v_conv_pw_w, v_pool_w, v_pool_scale, v_w_out, v_norm2_g, v_w_up, v_ffn_dw_w, v_w_down):
    given = dict(locals())
    w = {k: given[k] for k in WEIGHTS}
    mom_m = {k: given['m_' + k] for k in WEIGHTS}
    mom_v = {k: given['v_' + k] for k in WEIGHTS}
    place = _mesh_place()
    chip = place[0]
    W = _gather_full(w, place)

    reducer = _GradReducer(place, norm1_g.shape[0])
    loss_part, grad_x, g_small = _local_step(x[0], loss_target[0], W, {k: w[k] for k in REPLICATED}, reducer)
    loss = lax.psum(loss_part[0, 0], ('x', 'y', 'c'))
    sums = reducer.finish(grad_x)

    small = _pack_rows([g_small[k].reshape(-1) for k in SMALL_GRADS], 8)
    small_sum = _all_reduce_small(small)

    g_sum, delta, new_m, new_v = {}, {}, {}, {}
    for k in RS_WIRE:
        outs = _adamw(_as_2d(w[k]), _as_2d(sums[k][0]), _as_2d(mom_m[k]), _as_2d(mom_v[k]), g2=_as_2d(sums[k][1]), name='adamw_' + k)
        g_sum[k], delta[k], new_m[k], new_v[k] = [o.reshape(w[k].shape) for o in outs]
    off = 0
    small_full = {}
    for k in SMALL_GRADS:
        small_full[k] = small_sum.reshape(-1)[off:off + g_small[k].size].reshape(g_small[k].shape)
        off += g_small[k].size
    small_g = {k: small_full[k] for k in REPLICATED}
    small_g['conv_dw_w'] = lax.dynamic_slice_in_dim(small_full['conv_dw_w'], chip * w['conv_dw_w'].shape[2], w['conv_dw_w'].shape[2], axis=2)
    small_g['conv_pw_w'] = lax.dynamic_slice_in_dim(small_full['conv_pw_w'], chip * w['conv_pw_w'].shape[1], w['conv_pw_w'].shape[1], axis=1)
    small_g['ffn_dw_w'] = lax.dynamic_slice_in_dim(small_full['ffn_dw_w'], chip * w['ffn_dw_w'].shape[2], w['ffn_dw_w'].shape[2], axis=2)
    pack_small = lambda t: _pack_rows([t[k].reshape(-1) for k in SMALL_GRADS], 8)
    outs = _adamw(pack_small(w), pack_small(small_g), pack_small(mom_m), pack_small(mom_v), name='adamw_small')
    off = 0
    for k in SMALL_GRADS:
        pieces = [o.reshape(-1)[off:off + w[k].size].reshape(w[k].shape) for o in outs]
        g_sum[k], delta[k], new_m[k], new_v[k] = pieces
        off += w[k].size

    return (loss, grad_x[None], *[g_sum[k] for k in WEIGHTS], *[delta[k] for k in WEIGHTS],
            *[new_m[k] for k in WEIGHTS], *[new_v[k] for k in WEIGHTS])
```

```python
import functools

import jax
import jax.numpy as jnp
from jax import lax
from jax.experimental import pallas as pl
from jax.experimental.pallas import tpu as pltpu

F32 = jnp.float32
BF16 = jnp.bfloat16

N_HEADS = 8
HEAD_DIM = 64
D_ATT = N_HEADS * HEAD_DIM
D_CONV = 256
D_POOL = 256
D_MIX = D_ATT + D_CONV + D_POOL
D_QKV = 3 * D_ATT
D_PROJ_A = D_QKV + 2 * D_CONV + D_POOL
D_IN = D_PROJ_A + N_HEADS
FG_ROWS = 128
CONV_WIDTH = 31
CONV_HALO = 32
POOL_WINDOWS = (2, 4, 8, 16)
POOL_GROUP = 64
POOL_HALO = 16
FFN_CONV_WIDTH = 3
FFN_HALO = 8
ATT_SCALE = HEAD_DIM ** -0.5
EPS = 1e-6
NEG = -1e30
LANES = 128

ADAM_LR = 0.001
ADAM_B1 = 0.9
ADAM_B2 = 0.999
ADAM_EPS = 1e-08
ADAM_WD = 0.01
ADAM_STEP = 10

N_CHIPS = 4
N_DEV = 8
VMEM_LIMIT_BYTES = 56 * 1024 * 1024

REPLICATED = ('norm1_g', 'b_f', 'q_norm_g', 'k_norm_g', 'conv_dw_b', 'conv_ln_g', 'conv_ln_b',
              'pool_w', 'pool_scale', 'norm2_g')
WEIGHTS = ('norm1_g', 'w_in', 'b_f', 'q_norm_g', 'k_norm_g', 'conv_dw_w', 'conv_dw_b', 'conv_ln_g',
           'conv_ln_b', 'conv_pw_w', 'pool_w', 'pool_scale', 'w_out', 'norm2_g', 'w_up', 'ffn_dw_w', 'w_down')


def _tile(dim, pref, unit=LANES):
    if dim <= pref:
        return dim
    t = (pref // unit) * unit
    while t >= unit:
        if dim % t == 0:
            return t
        t -= unit
    raise ValueError(f'no tile for {dim} (preferred {pref})')


def _params(*sem):
    return pltpu.CompilerParams(dimension_semantics=sem, vmem_limit_bytes=VMEM_LIMIT_BYTES)


def _sigmoid(x):
    return 1.0 / (1.0 + jnp.exp(-x))


def _dot(a, b, ca, cb):
    return lax.dot_general(a, b, (((ca,), (cb,)), ((), ())), preferred_element_type=F32)


def _split3(y):
    y1 = y.astype(BF16)
    r1 = y - y1.astype(F32)
    y2 = r1.astype(BF16)
    y3 = (r1 - y2.astype(F32)).astype(BF16)
    return y1, y2, y3


def _dot3(y, e, ca=1, cb=0):
    y1, y2, y3 = _split3(y)
    return _dot(y1, e, ca, cb) + _dot(y2, e, ca, cb) + _dot(y3, e, ca, cb)


def _lead(spec_shape, imap, lead):
    if lead is None:
        return pl.BlockSpec(spec_shape, imap)
    return pl.BlockSpec((None,) + spec_shape, lambda *g: (lead,) + imap(*g))


ANY_SPEC = pl.BlockSpec(memory_space=pl.ANY)


def _mm(a, b, *, name, ta=False, tb=False, res=None, out_dtype=F32, tm=512, tn=512, tk=1024,
        a_lead=None, b_lead=None, copy16=False, after=None, cols_outer=False):
    a2, b2 = a.shape[-2:], b.shape[-2:]
    K, M = a2 if ta else a2[::-1]
    N, Kb = b2 if tb else b2[::-1]
    assert K == Kb, (a.shape, b.shape)
    tm, tn, tk = _tile(M, tm), _tile(N, tn), _tile(K, tk)
    nk = K // tk
    ca = 0 if ta else 1
    cb = 1 if tb else 0
    has_res = res is not None
    n_in = 2 + has_res + (after is not None)
    n_out = 1 + copy16

    def body(*refs):
        a_ref, b_ref = refs[:2]
        r_ref = refs[2] if has_res else None
        o_refs = refs[n_in:n_in + n_out]
        scratch = refs[n_in + n_out:]

        def write(r):
            if has_res:
                r = r + r_ref[...]
            o_refs[0][...] = r.astype(out_dtype)
            if copy16:
                o_refs[1][...] = r.astype(BF16)

        p = _dot(a_ref[...].astype(BF16), b_ref[...].astype(BF16), ca, cb)
        if nk == 1:
            write(p)
        else:
            acc = scratch[0]
            k = pl.program_id(2)

            @pl.when(k == 0)
            def _():
                acc[...] = p

            @pl.when(k > 0)
            def _():
                acc[...] += p

            @pl.when(k == nk - 1)
            def _():
                write(acc[...])

    ij = (lambda g0, g1: (g1, g0)) if cols_outer else (lambda g0, g1: (g0, g1))
    at = lambda f: (lambda g0, g1, k: f(*ij(g0, g1), k))
    a_spec = _lead((tk, tm), at(lambda i, j, k: (k, i)), a_lead) if ta else _lead((tm, tk), at(lambda i, j, k: (i, k)), a_lead)
    b_spec = _lead((tn, tk), at(lambda i, j, k: (j, k)), b_lead) if tb else _lead((tk, tn), at(lambda i, j, k: (k, j)), b_lead)
    o_spec = pl.BlockSpec((tm, tn), at(lambda i, j, k: (i, j)))
    in_specs = [a_spec, b_spec] + ([o_spec] if has_res else []) + ([ANY_SPEC] if after is not None else [])
    args = (a, b) + ((res,) if has_res else ()) + ((after,) if after is not None else ())
    out_shape = [jax.ShapeDtypeStruct((M, N), out_dtype)] + ([jax.ShapeDtypeStruct((M, N), BF16)] if copy16 else [])
    out = pl.pallas_call(
        body, name=name,
        out_shape=tuple(out_shape),
        grid=ij(M // tm, N // tn) + (nk,),
        in_specs=in_specs, out_specs=tuple([o_spec] * n_out),
        scratch_shapes=[pltpu.VMEM((tm, tn), F32)] if nk > 1 else [],
        compiler_params=_params('parallel', 'parallel', 'arbitrary'),
    )(*args)
    return out if copy16 else out[0]


def _rms_fwd(x, g, *, name, ts=512):
    S, D = x.shape
    ts = _tile(S, ts, 8)

    def body(x_ref, g_ref, o_ref):
        xv = x_ref[...]
        r = lax.rsqrt(jnp.mean(xv * xv, axis=-1, keepdims=True) + EPS)
        o_ref[...] = (xv * r * g_ref[...]).astype(BF16)

    return pl.pallas_call(
        body, name=name, out_shape=jax.ShapeDtypeStruct((S, D), BF16), grid=(S // ts,),
        in_specs=[pl.BlockSpec((ts, D), lambda i: (i, 0)), pl.BlockSpec((1, D), lambda i: (0, 0))],
        out_specs=pl.BlockSpec((ts, D), lambda i: (i, 0)),
        compiler_params=_params('parallel'),
    )(x, g)


def _rms_bwd(x, g, dh, dres, *, name, ts=512):
    S, D = x.shape
    ts = _tile(S, ts, 8)

    def body(x_ref, g_ref, dh_ref, dr_ref, dx_ref, dg_ref):
        i = pl.program_id(0)
        xv = x_ref[...]
        r = lax.rsqrt(jnp.mean(xv * xv, axis=-1, keepdims=True) + EPS)
        y = xv * r
        dh_v = dh_ref[...]
        dy = dh_v * g_ref[...]
        dx_ref[...] = dr_ref[...] + r * (dy - y * jnp.mean(dy * y, axis=-1, keepdims=True))
        part = jnp.sum(dh_v * y, axis=0, keepdims=True)

        @pl.when(i == 0)
        def _():
            dg_ref[...] = part

        @pl.when(i > 0)
        def _():
            dg_ref[...] += part

    row = pl.BlockSpec((ts, D), lambda i: (i, 0))
    vec = pl.BlockSpec((1, D), lambda i: (0, 0))
    return pl.pallas_call(
        body, name=name,
        out_shape=(jax.ShapeDtypeStruct((S, D), F32), jax.ShapeDtypeStruct((1, D), F32)),
        grid=(S // ts,), in_specs=[row, vec, row, row], out_specs=(row, vec),
        compiler_params=_params('arbitrary'),
    )(x, g, dh, dres)


def _group_ones():
    i = lax.broadcasted_iota(jnp.int32, (D_ATT, D_ATT), 0) // HEAD_DIM
    j = lax.broadcasted_iota(jnp.int32, (D_ATT, D_ATT), 1) // HEAD_DIM
    return (i == j).astype(BF16)


def _qk_prep_fwd(proj_a, qg, kg, *, name, ts=512):
    S = proj_a.shape[0]
    ts = _tile(S, ts, 16)

    def body(q_ref, k_ref, v_ref, qg_ref, kg_ref, e_ref, o_ref):
        e = e_ref[...]

        def norm(xv, gain):
            ms = _dot3(xv * xv, e) * (1.0 / HEAD_DIM)
            return xv * lax.rsqrt(ms + EPS) * gain

        o_ref[:, 0:D_ATT] = (norm(q_ref[...], qg_ref[...]) * ATT_SCALE).astype(BF16)
        o_ref[:, D_ATT:2 * D_ATT] = norm(k_ref[...], kg_ref[...]).astype(BF16)
        o_ref[:, 2 * D_ATT:3 * D_ATT] = v_ref[...].astype(BF16)

    col = lambda c: pl.BlockSpec((ts, D_ATT), lambda i: (i, c))
    vec = pl.BlockSpec((1, D_ATT), lambda i: (0, 0))
    return pl.pallas_call(
        body, name=name, out_shape=jax.ShapeDtypeStruct((S, D_QKV), BF16), grid=(S // ts,),
        in_specs=[col(0), col(1), col(2), vec, vec, pl.BlockSpec((D_ATT, D_ATT), lambda i: (0, 0))],
        out_specs=pl.BlockSpec((ts, D_QKV), lambda i: (i, 0)),
        compiler_params=_params('parallel'),
    )(proj_a, proj_a, proj_a, qg, kg, _group_ones())


def _qk_prep_bwd(proj_a, dq, dk, dv, qg, kg, *, name, ts=512):
    S = proj_a.shape[0]
    ts = _tile(S, ts, 16)

    def body(q_ref, k_ref, dq_ref, dk_ref, dv_ref, qg_ref, kg_ref, e_ref, o_ref, dqg_ref, dkg_ref):
        i = pl.program_id(0)
        e = e_ref[...]

        def norm_bwd(xv, dn, gain, scale):
            ms = _dot3(xv * xv, e) * (1.0 / HEAD_DIM)
            r = lax.rsqrt(ms + EPS)
            y = xv * r
            dy = dn * (gain * scale)
            mean = _dot3(dy * y, e) * (1.0 / HEAD_DIM)
            return r * (dy - y * mean), jnp.sum(dn * y, axis=0, keepdims=True) * scale

        dq_raw, dqg = norm_bwd(q_ref[...], dq_ref[...], qg_ref[...], ATT_SCALE)
        dk_raw, dkg = norm_bwd(k_ref[...], dk_ref[...], kg_ref[...], 1.0)
        o_ref[:, 0:D_ATT] = dq_raw.astype(BF16)
        o_ref[:, D_ATT:2 * D_ATT] = dk_raw.astype(BF16)
        o_ref[:, 2 * D_ATT:3 * D_ATT] = dv_ref[...].astype(BF16)

        @pl.when(i == 0)
        def _():
            dqg_ref[...] = dqg
            dkg_ref[...] = dkg

        @pl.when(i > 0)
        def _():
            dqg_ref[...] += dqg
            dkg_ref[...] += dkg

    col = lambda c: pl.BlockSpec((ts, D_ATT), lambda i: (i, c))
    vec = pl.BlockSpec((1, D_ATT), lambda i: (0, 0))
    return pl.pallas_call(
        body, name=name,
        out_shape=(jax.ShapeDtypeStruct((S, D_PROJ_A), BF16), jax.ShapeDtypeStruct((1, D_ATT), F32),
                   jax.ShapeDtypeStruct((1, D_ATT), F32)),
        grid=(S // ts,),
        in_specs=[col(0), col(1), col(0), col(0), col(0), vec, vec, pl.BlockSpec((D_ATT, D_ATT), lambda i: (0, 0))],
        out_specs=(pl.BlockSpec((ts, D_QKV), lambda i: (i, 0)), vec, vec),
        compiler_params=_params('arbitrary'),
    )(proj_a, proj_a, dq, dk, dv, qg, kg, _group_ones())


def _tri_ones(upper):
    i = lax.broadcasted_iota(jnp.int32, (LANES, LANES), 0)
    j = lax.broadcasted_iota(jnp.int32, (LANES, LANES), 1)
    return ((i <= j) if upper else (i >= j)).astype(BF16)


def _forget_fwd(z_raw, b_col, *, name):
    R, S = z_raw.shape
    nb = S // LANES

    def body(z_ref, b_ref, u_ref, f_ref):
        u = u_ref[...]
        carry = jnp.zeros((R, 1), F32)
        for j in range(nb):
            z = z_ref[:, j * LANES:(j + 1) * LANES] + b_ref[...]
            logf = jnp.minimum(z, 0.0) - jnp.log(1.0 + jnp.exp(-jnp.abs(z)))
            f_ref[:, j * LANES:(j + 1) * LANES] = _dot3(logf, u) + carry
            carry = carry + jnp.sum(logf, axis=1, keepdims=True)

    return pl.pallas_call(
        body, name=name, out_shape=jax.ShapeDtypeStruct((R, S), F32),
        compiler_params=pltpu.CompilerParams(vmem_limit_bytes=VMEM_LIMIT_BYTES),
    )(z_raw, b_col, _tri_ones(True))


def _forget_bwd(z_raw, b_col, df, *, name):
    R, S = z_raw.shape
    nb = S // LANES

    def body(z_ref, b_ref, df_ref, l_ref, dz_ref, db_ref):
        low = l_ref[...]
        carry = jnp.zeros((R, 1), F32)
        db = jnp.zeros((R, 1), F32)
        for j in reversed(range(nb)):
            d = df_ref[:, j * LANES:(j + 1) * LANES]
            dlogf = _dot3(d, low) + carry
            carry = carry + jnp.sum(d, axis=1, keepdims=True)
            z = z_ref[:, j * LANES:(j + 1) * LANES] + b_ref[...]
            dz = dlogf * _sigmoid(-z)
            dz_ref[:, j * LANES:(j + 1) * LANES] = dz
            db = db + jnp.sum(dz, axis=1, keepdims=True)
        db_ref[...] = db

    return pl.pallas_call(
        body, name=name,
        out_shape=(jax.ShapeDtypeStruct((R, S), F32), jax.ShapeDtypeStruct((R, 1), F32)),
        compiler_params=pltpu.CompilerParams(vmem_limit_bytes=VMEM_LIMIT_BYTES),
    )(z_raw, b_col, df, _tri_ones(False))


def _head_mask(hh):
    lane = lax.broadcasted_iota(jnp.int32, (1, LANES), 1)
    return (lane // HEAD_DIM) == hh


def _causal(s, qi, ki, t):
    rows = qi * t + lax.broadcasted_iota(jnp.int32, (t, t), 0)
    cols = ki * t + lax.broadcasted_iota(jnp.int32, (t, t), 1)
    return jnp.where(cols <= rows, s, NEG)


AUG = 2 * HEAD_DIM


def _aug_consts():
    i = lax.broadcasted_iota(jnp.int32, (D_ATT, N_HEADS * AUG), 0)
    j = lax.broadcasted_iota(jnp.int32, (D_ATT, N_HEADS * AUG), 1)
    spread = (j == (i // HEAD_DIM) * AUG + i % HEAD_DIM).astype(BF16)
    h = lax.broadcasted_iota(jnp.int32, (LANES, N_HEADS * AUG), 0)
    c = lax.broadcasted_iota(jnp.int32, (LANES, N_HEADS * AUG), 1)
    gate = [((c == h * AUG + HEAD_DIM + t) & (h < N_HEADS)).astype(BF16) for t in range(3)]
    lane = lax.broadcasted_iota(jnp.int32, (1, N_HEADS * AUG), 1) % AUG
    ones_q = ((lane >= HEAD_DIM) & (lane < HEAD_DIM + 3)).astype(F32)
    ones_v = (lane == HEAD_DIM).astype(F32)
    return spread, gate, ones_q, ones_v


def _attn_aug(qkv, f_cum, *, name, ts=512):
    S = qkv.shape[0]
    ts = _tile(S, ts)
    spread, gate, ones_q, ones_v = _aug_consts()
    W = N_HEADS * AUG

    def body(q_ref, k_ref, v_ref, f_ref, sp_ref, g0_ref, g1_ref, g2_ref, oq_ref, ov_ref, qa_ref, ka_ref, va_ref):
        sp = sp_ref[...]
        qa_ref[...] = (_dot(q_ref[...], sp, 1, 0) + oq_ref[...]).astype(BF16)
        va_ref[...] = (_dot(v_ref[...], sp, 1, 0) + ov_ref[...]).astype(BF16)
        terms = _split3(-jnp.transpose(f_ref[...]))
        ka = _dot(k_ref[...], sp, 1, 0)
        for t, g_ref in zip(terms, (g0_ref, g1_ref, g2_ref)):
            ka = ka + _dot(t, g_ref[...], 1, 0)
        ka_ref[...] = ka.astype(BF16)

    col = lambda c: pl.BlockSpec((ts, D_ATT), lambda i: (i, c))
    full = lambda a: pl.BlockSpec(a.shape, lambda i: (0, 0))
    out = pl.BlockSpec((ts, W), lambda i: (i, 0))
    shape = jax.ShapeDtypeStruct((S, W), BF16)
    consts = (spread, *gate, ones_q, ones_v)
    return pl.pallas_call(
        body, name=name, out_shape=(shape, shape, shape), grid=(S // ts,),
        in_specs=[col(0), col(1), col(2), pl.BlockSpec((FG_ROWS, ts), lambda i: (0, i))] + [full(a) for a in consts],
        out_specs=(out, out, out),
        compiler_params=_params('parallel'),
    )(qkv, qkv, qkv, f_cum, *consts)


def _attn_fwd(qa, ka, va, *, name, tq=512, tk=1024):
    S = qa.shape[0]
    tq, tk = _tile(S, tq), _tile(S, tk)
    nq, nk = S // tq, S // tk
    npair = N_HEADS // 2

    def body(q_ref, k_ref, v_ref, mix_ref, o_ref, lse_ref, m_s, acc_s):
        qi, ki = pl.program_id(1), pl.program_id(2)
        last = (qi * tq + tq - 1) // tk
        first_masked = (qi * tq) // tk

        @pl.when(ki == 0)
        def _():
            m_s[...] = jnp.full(m_s.shape, NEG, F32)
            acc_s[...] = jnp.zeros(acc_s.shape, F32)

        def step(masked):
            if masked:
                rows = qi * tq + lax.broadcasted_iota(jnp.int32, (tq, tk), 0)
                cols = ki * tk + lax.broadcasted_iota(jnp.int32, (tq, tk), 1)
                keep = cols <= rows
            m_prev = [m_s[hh] for hh in range(2)]
            acc_prev = [acc_s[hh] for hh in range(2)]
            ss = []
            for hh in range(2):
                s = _dot(q_ref[:, hh * AUG:(hh + 1) * AUG], k_ref[:, hh * AUG:(hh + 1) * AUG], 1, 1)
                ss.append(jnp.where(keep, s, NEG) if masked else s)
            m_new = [jnp.maximum(m_prev[hh], jnp.max(ss[hh], axis=1, keepdims=True)) for hh in range(2)]
            ps = [jnp.exp(ss[hh] - jnp.tile(m_new[hh], (1, tk // LANES))).astype(BF16) for hh in range(2)]
            for hh in range(2):
                alpha = jnp.exp(m_prev[hh] - m_new[hh])
                acc_s[hh] = alpha * acc_prev[hh] + _dot(ps[hh], v_ref[:, hh * AUG:(hh + 1) * AUG], 1, 0)
                m_s[hh] = m_new[hh]

        @pl.when(ki < first_masked)
        def _():
            step(False)

        @pl.when((ki >= first_masked) & (ki <= last))
        def _():
            step(True)

        @pl.when(ki == last)
        def _():
            lane = lax.broadcasted_iota(jnp.int32, (1, LANES), 1)
            outs, lses = [], []
            for hh in range(2):
                acc = acc_s[hh]
                denom = jnp.sum(jnp.where(lane == HEAD_DIM, acc, 0.0), axis=1, keepdims=True)
                outs.append(acc / denom)
                lses.append(m_s[hh] + jnp.log(denom))
            o = jnp.where(lane < HEAD_DIM, outs[0], pltpu.roll(outs[1], HEAD_DIM, 1))
            o_ref[...] = o
            mix_ref[...] = o.astype(BF16)
            lse_ref[...] = jnp.where(lane < HEAD_DIM, lses[0], lses[1])

    def kmap(h, i, j):
        return (jnp.minimum(j, (i * tq + tq - 1) // tk), h)

    out = pl.BlockSpec((tq, LANES), lambda h, i, j: (i, h))
    return pl.pallas_call(
        body, name=name,
        out_shape=(jax.ShapeDtypeStruct((S, D_MIX), BF16), jax.ShapeDtypeStruct((S, D_ATT), F32),
                   jax.ShapeDtypeStruct((S, D_ATT), F32)),
        grid=(npair, nq, nk),
        in_specs=[pl.BlockSpec((tq, 2 * AUG), lambda h, i, j: (i, h)),
                  pl.BlockSpec((tk, 2 * AUG), kmap), pl.BlockSpec((tk, 2 * AUG), kmap)],
        out_specs=(out, out, out),
        scratch_shapes=[pltpu.VMEM((2, tq, LANES), F32), pltpu.VMEM((2, tq, LANES), F32)],
        compiler_params=_params('parallel', 'parallel', 'arbitrary'),
    )(qa, ka, va)


def _attn_bwd(qkv, f3, att, lse, d_mix, *, name, t=1024):
    S = qkv.shape[0]
    t = _tile(S, t)
    n = S // t
    npair = N_HEADS // 2

    def body(q_ref, k_ref, v_ref, f_ref, o_ref, lse_ref, do_ref, dq_ref, dk_ref, dv_ref, df_ref, dr_ref, dk_s, dv_s, df_s):
        ki, qi = pl.program_id(1), pl.program_id(2)

        @pl.when(qi == ki)
        def _():
            dk_s[...] = jnp.zeros(dk_s.shape, F32)
            dv_s[...] = jnp.zeros(dv_s.shape, F32)
            df_s[...] = jnp.zeros(df_s.shape, F32)

        def step(masked):
            q, k, v = q_ref[...], k_ref[...], v_ref[...]
            do, o, lse = do_ref[...], o_ref[...], lse_ref[...]
            lane = lax.broadcasted_iota(jnp.int32, (1, LANES), 1)
            lse_sw = pltpu.roll(lse, HEAD_DIM, 1)
            delta = _dot3(do.astype(BF16).astype(F32) * o, _pair_ones())
            delta_sw = pltpu.roll(delta, HEAD_DIM, 1)
            dq_blk = jnp.zeros((t, LANES), F32)
            dr_blk = jnp.zeros((t, LANES), F32)
            for hh in range(2):
                msk = _head_mask(hh)
                first = lane < HEAD_DIM if hh == 0 else lane >= HEAD_DIM
                qm = jnp.where(msk, q, jnp.zeros_like(q))
                km = jnp.where(msk, k, jnp.zeros_like(k))
                do_h = jnp.where(msk, do, 0.0)
                dom = do_h.astype(BF16)
                s = _dot(qm, k, 1, 1) - f_ref[0, hh:hh + 1, :]
                if masked:
                    s = _causal(s, qi, ki, t)
                lse_h = jnp.where(first, lse, lse_sw)
                delta_h = jnp.where(first, delta, delta_sw)
                p = jnp.exp(s - jnp.tile(lse_h, (1, t // LANES)))
                dp = _dot(dom, v, 1, 1)
                ds = p * (dp - jnp.tile(delta_h, (1, t // LANES)))
                dsb = ds.astype(BF16)
                dv_s[...] += _dot(jnp.transpose(do_h).astype(BF16), p.astype(BF16), 1, 0)
                dk_s[...] += _dot(jnp.transpose(qm.astype(F32)).astype(BF16), dsb, 1, 0)
                dq_blk = dq_blk + _dot(dsb, km, 1, 0)
                df_s[hh] -= jnp.sum(ds, axis=0, keepdims=True)
                dr_blk = dr_blk + jnp.where(msk, jnp.sum(ds, axis=1, keepdims=True), 0.0)
            rows = pl.ds(pl.multiple_of(qi * t, t), t)

            @pl.when(ki == 0)
            def _():
                dq_ref[rows, :] = dq_blk
                dr_ref[rows, :] = dr_blk

            @pl.when(ki > 0)
            def _():
                dq_ref[rows, :] += dq_blk
                dr_ref[rows, :] += dr_blk

        @pl.when(qi > ki)
        def _():
            step(False)

        @pl.when(qi == ki)
        def _():
            step(True)

        @pl.when(qi == n - 1)
        def _():
            dk_ref[...] = jnp.transpose(dk_s[...])
            dv_ref[...] = jnp.transpose(dv_s[...])
            df_ref[0, 0:1, :] = df_s[0]
            df_ref[0, 1:2, :] = df_s[1]

    qrow = lambda h, j, i: (jnp.maximum(i, j), h)
    return pl.pallas_call(
        body, name=name,
        out_shape=(jax.ShapeDtypeStruct((S, D_ATT), F32), jax.ShapeDtypeStruct((S, D_ATT), F32),
                   jax.ShapeDtypeStruct((S, D_ATT), F32), jax.ShapeDtypeStruct((npair, 2, S), F32),
                   jax.ShapeDtypeStruct((S, D_ATT), F32)),
        grid=(npair, n, n),
        in_specs=[pl.BlockSpec((t, LANES), qrow),
                  pl.BlockSpec((t, LANES), lambda h, j, i: (j, npair + h)),
                  pl.BlockSpec((t, LANES), lambda h, j, i: (j, 2 * npair + h)),
                  pl.BlockSpec((1, 2, t), lambda h, j, i: (h, 0, j)),
                  pl.BlockSpec((t, LANES), qrow),
                  pl.BlockSpec((t, LANES), qrow),
                  pl.BlockSpec((t, LANES), qrow)],
        out_specs=(pl.BlockSpec((S, LANES), lambda h, j, i: (0, h)),
                   pl.BlockSpec((t, LANES), lambda h, j, i: (j, h)),
                   pl.BlockSpec((t, LANES), lambda h, j, i: (j, h)),
                   pl.BlockSpec((1, 2, t), lambda h, j, i: (h, 0, j)),
                   pl.BlockSpec((S, LANES), lambda h, j, i: (0, h))),
        scratch_shapes=[pltpu.VMEM((LANES, t), F32), pltpu.VMEM((LANES, t), F32), pltpu.VMEM((2, 1, t), F32)],
        compiler_params=_params('parallel', 'arbitrary', 'arbitrary'),
    )(qkv, qkv, qkv, f3, att, lse, d_mix)


def _pair_ones():
    i = lax.broadcasted_iota(jnp.int32, (LANES, LANES), 0) // HEAD_DIM
    j = lax.broadcasted_iota(jnp.int32, (LANES, LANES), 1) // HEAD_DIM
    return (i == j).astype(BF16)


A_COL = D_QKV // D_CONV
B_COL = A_COL + 1
P_COL = B_COL + 1


CONV_BLOCKS = D_CONV // LANES
CONV_GROUP = 8 * 8


def _rows8(ref, c, row):
    return ref.at[c][pl.ds(row, 8, stride=8), :]


def _put8(ref, c, row, val):
    ref.at[c][pl.ds(row, 8, stride=8), :] = val


def _lanes(c):
    return slice(c * LANES, (c + 1) * LANES)


def _glu_into(buf, a_ref, b_ref, ah_ref, bh_ref, first, ts):
    for c in range(CONV_BLOCKS):
        halo = ah_ref[:, _lanes(c)] * _sigmoid(bh_ref[:, _lanes(c)])
        buf[c, 0:CONV_HALO, :] = jnp.where(first, 0.0, halo)
        buf[c, CONV_HALO:CONV_HALO + ts, :] = a_ref[:, _lanes(c)] * _sigmoid(b_ref[:, _lanes(c)])


def _conv_taps(buf, c, r0):
    return [_rows8(buf, c, CONV_HALO + r0 + i - (CONV_WIDTH - 1)) for i in range(CONV_WIDTH - 1 + 8)]


def _dwconv8(xs, ws, bias):
    outs = []
    for j in range(8):
        acc = ws[0] * xs[j]
        for k in range(1, CONV_WIDTH):
            acc = acc + ws[k] * xs[j + k]
        outs.append(acc + bias)
    return outs


def _ln8(cs):
    inv = 1.0 / D_CONV
    mu = sum(jnp.sum(c, axis=1, keepdims=True) for c in cs) * inv
    xc = [c - mu for c in cs]
    rstd = lax.rsqrt(sum(jnp.sum(x * x, axis=1, keepdims=True) for x in xc) * inv + EPS)
    return [x * rstd for x in xc], rstd


def _conv_specs(ts, tmap):
    hb = ts // CONV_HALO
    cur = lambda c: pl.BlockSpec((ts, D_CONV), lambda i: (tmap(i), c))
    halo = lambda c: pl.BlockSpec((CONV_HALO, D_CONV), lambda i: (jnp.maximum(tmap(i) * hb - 1, 0), c))
    return cur, halo


def _conv_fwd(proj_a, mix, dw_w, dw_b, ln_g, ln_b, pw_w, l, *, name, ts=512):
    S = proj_a.shape[0]
    ts = _tile(S, ts, CONV_GROUP)

    def body(a_ref, b_ref, ah_ref, bh_ref, w_ref, wb_ref, g_ref, bb_ref, pw_ref, mix_in, o_ref, buf, stage):
        _glu_into(buf, a_ref, b_ref, ah_ref, bh_ref, pl.program_id(0) == 0, ts)
        ws = [[w_ref[k:k + 1, _lanes(c)] for k in range(CONV_WIDTH)] for c in range(CONV_BLOCKS)]
        for r0 in range(0, ts, CONV_GROUP):
            conv = [_dwconv8(_conv_taps(buf, c, r0), ws[c], wb_ref[:, _lanes(c)]) for c in range(CONV_BLOCKS)]
            for j in range(8):
                yhat, _ = _ln8([conv[c][j] for c in range(CONV_BLOCKS)])
                for c in range(CONV_BLOCKS):
                    y = yhat[c] * g_ref[:, _lanes(c)] + bb_ref[:, _lanes(c)]
                    _put8(stage, c, r0 + j, y * _sigmoid(y))
        hs = jnp.concatenate([stage[c] for c in range(CONV_BLOCKS)], axis=1)
        o_ref[...] = _dot(hs.astype(BF16), pw_ref[...], 1, 0).astype(BF16)

    cur, halo = _conv_specs(ts, lambda i: i)
    vec = pl.BlockSpec((1, D_CONV), lambda i: (0, 0))
    return pl.pallas_call(
        body, name=name, out_shape=jax.ShapeDtypeStruct(mix.shape, BF16), grid=(S // ts,),
        in_specs=[cur(A_COL), cur(B_COL), halo(A_COL), halo(B_COL),
                  pl.BlockSpec((None, CONV_HALO, D_CONV), lambda i: (l, 0, 0)), vec, vec, vec,
                  pl.BlockSpec((None, D_CONV, D_CONV), lambda i: (l, 0, 0)), ANY_SPEC],
        out_specs=pl.BlockSpec((ts, D_CONV), lambda i: (i, D_ATT // D_CONV)),
        scratch_shapes=[pltpu.VMEM((CONV_BLOCKS, CONV_HALO + ts, LANES), F32), pltpu.VMEM((CONV_BLOCKS, ts, LANES), F32)],
        input_output_aliases={9: 0},
        compiler_params=_params('parallel'),
    )(proj_a, proj_a, proj_a, proj_a, dw_w, dw_b, ln_g, ln_b, pw_w, mix)


def _conv_bwd(proj_a, d_mix, d_proj, dw_w, dw_b, ln_g, ln_b, pw_w, l, *, name, ts=512):
    S = proj_a.shape[0]
    ts = _tile(S, ts, CONV_GROUP)
    n = S // ts
    d_col = D_ATT // D_CONV
    groups = range(0, ts, CONV_GROUP)

    def body(a_ref, b_ref, ah_ref, bh_ref, dy_ref, w_ref, wb_ref, g_ref, bb_ref, pw_ref, dp_in,
             o_ref, dw_ref, dwb_ref, dg_ref, dbb_ref, dpw_ref, buf, dcbuf, stage, stage2):
        i = pl.program_id(0)
        _glu_into(buf, a_ref, b_ref, ah_ref, bh_ref, i == n - 1, ts)

        @pl.when(i == 0)
        def _():
            dcbuf[:, ts:ts + CONV_HALO, :] = jnp.zeros((CONV_BLOCKS, CONV_HALO, LANES), F32)
            dw_ref[...] = jnp.zeros(dw_ref.shape, F32)
            dwb_ref[...] = jnp.zeros(dwb_ref.shape, F32)
            dg_ref[...] = jnp.zeros(dg_ref.shape, F32)
            dbb_ref[...] = jnp.zeros(dbb_ref.shape, F32)
            dpw_ref[...] = jnp.zeros(dpw_ref.shape, F32)

        dout = dy_ref[...].astype(BF16)
        d_hs = _dot(dout, pw_ref[...], 1, 1)
        for c in range(CONV_BLOCKS):
            stage2[c, :, :] = d_hs[:, _lanes(c)]
        ws = [[w_ref[k:k + 1, _lanes(c)] for k in range(CONV_WIDTH)] for c in range(CONV_BLOCKS)]
        zero8 = jnp.zeros((8, LANES), F32)
        dg = [zero8] * CONV_BLOCKS
        dbb = [zero8] * CONV_BLOCKS
        dwb = [zero8] * CONV_BLOCKS
        for r0 in groups:
            conv = [_dwconv8(_conv_taps(buf, c, r0), ws[c], wb_ref[:, _lanes(c)]) for c in range(CONV_BLOCKS)]
            for j in range(8):
                yhat, rstd = _ln8([conv[c][j] for c in range(CONV_BLOCKS)])
                d_yhat = []
                for c in range(CONV_BLOCKS):
                    y = yhat[c] * g_ref[:, _lanes(c)] + bb_ref[:, _lanes(c)]
                    sg = _sigmoid(y)
                    _put8(stage, c, r0 + j, y * sg)
                    d_y = _rows8(stage2, c, r0 + j) * (sg * (1.0 + y * (1.0 - sg)))
                    dg[c] = dg[c] + d_y * yhat[c]
                    dbb[c] = dbb[c] + d_y
                    d_yhat.append(d_y * g_ref[:, _lanes(c)])
                inv = 1.0 / D_CONV
                m1 = sum(jnp.sum(d, axis=1, keepdims=True) for d in d_yhat) * inv
                m2 = sum(jnp.sum(d * yh, axis=1, keepdims=True) for d, yh in zip(d_yhat, yhat)) * inv
                for c in range(CONV_BLOCKS):
                    d_c = rstd * (d_yhat[c] - m1 - yhat[c] * m2)
                    dwb[c] = dwb[c] + d_c
                    _put8(dcbuf, c, r0 + j, d_c)
        for c in range(CONV_BLOCKS):
            dg_ref[:, _lanes(c)] += jnp.sum(dg[c], axis=0, keepdims=True)
            dbb_ref[:, _lanes(c)] += jnp.sum(dbb[c], axis=0, keepdims=True)
            dwb_ref[:, _lanes(c)] += jnp.sum(dwb[c], axis=0, keepdims=True)
        hs = jnp.concatenate([stage[c] for c in range(CONV_BLOCKS)], axis=1)
        dpw_ref[...] += _dot(hs.astype(BF16), dout, 0, 0)
        for c in range(CONV_BLOCKS):
            for r0 in groups:
                dcs = [_rows8(dcbuf, c, r0 + i_) for i_ in range(CONV_WIDTH - 1 + 8)]
                for j in range(8):
                    acc = ws[c][0] * dcs[j + CONV_WIDTH - 1]
                    for k in range(1, CONV_WIDTH):
                        acc = acc + ws[c][k] * dcs[j + CONV_WIDTH - 1 - k]
                    _put8(stage2, c, r0 + j, acc)
            for k in range(CONV_WIDTH):
                acc = zero8
                for r0 in groups:
                    for j in range(8):
                        acc = acc + _rows8(dcbuf, c, r0 + j) * _rows8(buf, c, CONV_HALO + r0 + j - (CONV_WIDTH - 1) + k)
                dw_ref[k:k + 1, _lanes(c)] += jnp.sum(acc, axis=0, keepdims=True)
            dcbuf[c, ts:ts + CONV_HALO, :] = dcbuf[c, 0:CONV_HALO, :]
        d_h = jnp.concatenate([stage2[c] for c in range(CONV_BLOCKS)], axis=1)
        a, sb = a_ref[...], _sigmoid(b_ref[...])
        o_ref[:, 0:D_CONV] = (d_h * sb).astype(BF16)
        o_ref[:, D_CONV:2 * D_CONV] = (d_h * a * sb * (1.0 - sb)).astype(BF16)

    rev = lambda i: n - 1 - i
    cur, halo = _conv_specs(ts, rev)
    vec = pl.BlockSpec((1, D_CONV), lambda i: (0, 0))
    wspec = pl.BlockSpec((CONV_HALO, D_CONV), lambda i: (0, 0))
    sq = pl.BlockSpec((D_CONV, D_CONV), lambda i: (0, 0))
    tile3 = pltpu.VMEM((CONV_BLOCKS, ts, LANES), F32)
    return pl.pallas_call(
        body, name=name,
        out_shape=(jax.ShapeDtypeStruct(d_proj.shape, BF16), jax.ShapeDtypeStruct((CONV_HALO, D_CONV), F32),
                   jax.ShapeDtypeStruct((1, D_CONV), F32), jax.ShapeDtypeStruct((1, D_CONV), F32),
                   jax.ShapeDtypeStruct((1, D_CONV), F32), jax.ShapeDtypeStruct((D_CONV, D_CONV), F32)),
        grid=(n,),
        in_specs=[cur(A_COL), cur(B_COL), halo(A_COL), halo(B_COL),
                  pl.BlockSpec((ts, D_CONV), lambda i: (rev(i), d_col)),
                  pl.BlockSpec((None, CONV_HALO, D_CONV), lambda i: (l, 0, 0)), vec, vec, vec,
                  pl.BlockSpec((None, D_CONV, D_CONV), lambda i: (l, 0, 0)), ANY_SPEC],
        out_specs=(pl.BlockSpec((ts, 2 * D_CONV), lambda i: (rev(i), D_QKV // (2 * D_CONV))), wspec, vec, vec, vec, sq),
        scratch_shapes=[pltpu.VMEM((CONV_BLOCKS, CONV_HALO + ts, LANES), F32),
                        pltpu.VMEM((CONV_BLOCKS, ts + CONV_HALO, LANES), F32), tile3, tile3],
        input_output_aliases={10: 0},
        compiler_params=_params('arbitrary'),
    )(proj_a, proj_a, proj_a, proj_a, d_mix, dw_w, dw_b, ln_g, ln_b, pw_w, d_proj)


def _pool_window():
    lane = lax.broadcasted_iota(jnp.int32, (1, D_POOL), 1)
    w = jnp.full((1, D_POOL), POOL_WINDOWS[0], jnp.int32)
    for g in range(1, len(POOL_WINDOWS)):
        w = jnp.where(lane // POOL_GROUP == g, POOL_WINDOWS[g], w)
    return w


def _pool_diff(buf, u_ref, uh_ref, first, tile, ts):
    buf[0:POOL_HALO, :] = jnp.where(first, 0.0, uh_ref[...])
    u = u_ref[...]
    buf[POOL_HALO:POOL_HALO + ts, :] = u
    wl = _pool_window()
    acc = u
    for j in range(1, max(POOL_WINDOWS)):
        acc = acc + jnp.where(j < wl, buf[pl.ds(POOL_HALO - j, ts), :], 0.0)
    pos = tile * ts + lax.broadcasted_iota(jnp.int32, (ts, 1), 0)
    cnt = jnp.minimum(pos + 1, wl).astype(F32)
    return acc / cnt - u, cnt


def _pool_specs(ts, tmap):
    hb = ts // POOL_HALO
    cur = pl.BlockSpec((ts, D_POOL), lambda i: (tmap(i), P_COL))
    halo = pl.BlockSpec((POOL_HALO, D_POOL), lambda i: (jnp.maximum(tmap(i) * hb - 1, 0), P_COL))
    return cur, halo


def _pool_fwd(proj_a, mix, wbd, scale, *, name, ts=512):
    S = proj_a.shape[0]
    ts = _tile(S, ts, POOL_HALO)

    def body(u_ref, uh_ref, w_ref, s_ref, mix_in, o_ref, buf):
        i = pl.program_id(0)
        d, _ = _pool_diff(buf, u_ref, uh_ref, i == 0, i, ts)
        o_ref[...] = (_dot(d.astype(BF16), w_ref[...], 1, 0) * s_ref[...]).astype(BF16)

    cur, halo = _pool_specs(ts, lambda i: i)
    return pl.pallas_call(
        body, name=name, out_shape=jax.ShapeDtypeStruct(mix.shape, BF16), grid=(S // ts,),
        in_specs=[cur, halo, pl.BlockSpec((D_POOL, D_POOL), lambda i: (0, 0)), pl.BlockSpec((1, D_POOL), lambda i: (0, 0)),
                  ANY_SPEC],
        out_specs=pl.BlockSpec((ts, D_POOL), lambda i: (i, (D_ATT + D_CONV) // D_POOL)),
        scratch_shapes=[pltpu.VMEM((POOL_HALO + ts, D_POOL), F32)],
        input_output_aliases={4: 0},
        compiler_params=_params('parallel'),
    )(proj_a, proj_a, wbd, scale, mix)


def _pool_bwd(proj_a, d_mix, d_proj, wbd, scale, *, name, ts=512):
    S = proj_a.shape[0]
    ts = _tile(S, ts, POOL_HALO)
    n = S // ts
    d_col = (D_ATT + D_CONV) // D_POOL

    def body(u_ref, uh_ref, dy_ref, w_ref, s_ref, dp_in, o_ref, dw_ref, ds_ref, buf, ebuf):
        i = pl.program_id(0)
        tile = n - 1 - i
        d, cnt = _pool_diff(buf, u_ref, uh_ref, tile == 0, tile, ts)
        db = d.astype(BF16)
        ypre = _dot(db, w_ref[...], 1, 0)
        dout = dy_ref[...]
        d_y = (dout * s_ref[...]).astype(BF16)
        d_d = _dot(d_y, w_ref[...], 1, 1)

        @pl.when(i == 0)
        def _():
            ebuf[ts:ts + POOL_HALO, :] = jnp.zeros((POOL_HALO, D_POOL), F32)
            dw_ref[...] = jnp.zeros(dw_ref.shape, F32)
            ds_ref[...] = jnp.zeros(ds_ref.shape, F32)

        dw_ref[...] += _dot(db, d_y, 0, 0)
        ds_ref[...] += jnp.sum(dout * ypre, axis=0, keepdims=True)
        e = d_d / cnt
        ebuf[0:ts, :] = e
        wl = _pool_window()
        acc = e
        for j in range(1, max(POOL_WINDOWS)):
            acc = acc + jnp.where(j < wl, ebuf[pl.ds(j, ts), :], 0.0)
        ebuf[ts:ts + POOL_HALO, :] = e[0:POOL_HALO, :]
        o_ref[...] = (acc - d_d).astype(BF16)

    rev = lambda i: n - 1 - i
    cur, halo = _pool_specs(ts, rev)
    sq = pl.BlockSpec((D_POOL, D_POOL), lambda i: (0, 0))
    vec = pl.BlockSpec((1, D_POOL), lambda i: (0, 0))
    return pl.pallas_call(
        body, name=name,
        out_shape=(jax.ShapeDtypeStruct(d_proj.shape, BF16), jax.ShapeDtypeStruct((D_POOL, D_POOL), F32),
                   jax.ShapeDtypeStruct((1, D_POOL), F32)),
        grid=(n,),
        in_specs=[cur, halo, pl.BlockSpec((ts, D_POOL), lambda i: (rev(i), d_col)), sq, vec, ANY_SPEC],
        out_specs=(pl.BlockSpec((ts, D_POOL), lambda i: (rev(i), P_COL)), sq, vec),
        scratch_shapes=[pltpu.VMEM((POOL_HALO + ts, D_POOL), F32), pltpu.VMEM((ts + POOL_HALO, D_POOL), F32)],
        input_output_aliases={5: 0},
        compiler_params=_params('arbitrary'),
    )(proj_a, proj_a, d_mix, wbd, scale, d_proj)


FFN_LANES = 128
FFN_GROUP = 8 * 8


def _ffn_rows(ref, c, row0, j):
    return ref.at[c][pl.ds(row0 + j, 8, stride=8), :]


def _ffn_specs(ts, tc2, tmap, l):
    hb = ts // FFN_HALO
    cur = pl.BlockSpec((ts, tc2), lambda c, i: (tmap(i), c))
    halo = pl.BlockSpec((FFN_HALO, tc2), lambda c, i: (jnp.maximum(tmap(i) * hb - 1, 0), c))
    wspec = pl.BlockSpec((None, FFN_HALO, tc2), lambda c, i: (l, 0, c))
    return cur, halo, wspec


def _ffn_fill(buf, x_ref, xh_ref, first, ts, nblk):
    for c in range(nblk):
        cs = slice(c * FFN_LANES, (c + 1) * FFN_LANES)
        buf[c, 0:FFN_HALO, :] = jnp.where(first, 0.0, xh_ref[:, cs])
        buf[c, FFN_HALO:FFN_HALO + ts, :] = x_ref[:, cs]


def _ffn_conv_piece(buf, w_ref, r0, c):
    ws = [w_ref[k:k + 1, c * FFN_LANES:(c + 1) * FFN_LANES] for k in range(FFN_CONV_WIDTH)]
    xs = [_ffn_rows(buf, c, FFN_HALO + r0, j) for j in range(1 - FFN_CONV_WIDTH, 8)]
    outs = []
    for j in range(8):
        acc = ws[0] * xs[j]
        for k in range(1, FFN_CONV_WIDTH):
            acc = acc + ws[k] * xs[j + k]
        outs.append(acc)
    return outs, xs


def _ffn_act_fwd(up, w, l, *, name, ts=256):
    S, F2 = up.shape
    tc = F2 // 4
    nb = tc // FFN_LANES
    ts = _tile(S, ts, FFN_GROUP)

    def body(x_ref, xh_ref, w_ref, o_ref, buf, stage):
        _ffn_fill(buf, x_ref, xh_ref, pl.program_id(1) == 0, ts, 2 * nb)
        for c in range(nb):
            for r0 in range(0, ts, FFN_GROUP):
                gates, _ = _ffn_conv_piece(buf, w_ref, r0, c)
                vals, _ = _ffn_conv_piece(buf, w_ref, r0, nb + c)
                for j in range(8):
                    stage.at[c][pl.ds(r0 + j, 8, stride=8), :] = gates[j] * _sigmoid(gates[j]) * vals[j]
            o_ref[:, c * FFN_LANES:(c + 1) * FFN_LANES] = stage[c].astype(BF16)

    cur, halo, wspec = _ffn_specs(ts, 2 * tc, lambda i: i, l)
    return pl.pallas_call(
        body, name=name, out_shape=jax.ShapeDtypeStruct((S, F2 // 2), BF16), grid=(2, S // ts),
        in_specs=[cur, halo, wspec],
        out_specs=pl.BlockSpec((ts, tc), lambda c, i: (i, c)),
        scratch_shapes=[pltpu.VMEM((2 * nb, FFN_HALO + ts, FFN_LANES), F32), pltpu.VMEM((nb, ts, FFN_LANES), F32)],
        compiler_params=_params('parallel', 'parallel'),
    )(up, up, w)


def _ffn_act_bwd(up, d_act, w, l, *, name, ts=256):
    S, F2 = up.shape
    tc = F2 // 4
    nb = tc // FFN_LANES
    ts = _tile(S, ts, FFN_GROUP)
    n = S // ts

    def body(x_ref, xh_ref, da_ref, w_ref, o_ref, dw_ref, buf, dcbuf, stage):
        i = pl.program_id(1)
        _ffn_fill(buf, x_ref, xh_ref, i == n - 1, ts, 2 * nb)

        @pl.when(i == 0)
        def _():
            dcbuf[:, ts:ts + FFN_HALO, :] = jnp.zeros((2 * nb, FFN_HALO, FFN_LANES), F32)
            dw_ref[...] = jnp.zeros(dw_ref.shape, F32)

        for c in range(nb):
            blocks = (c, nb + c)
            stage[c, :, :] = da_ref[:, c * FFN_LANES:(c + 1) * FFN_LANES]
            dws = [[jnp.zeros((8, FFN_LANES), F32) for _ in range(FFN_CONV_WIDTH)] for _ in range(2)]
            for r0 in range(0, ts, FFN_GROUP):
                gates, xg = _ffn_conv_piece(buf, w_ref, r0, blocks[0])
                vals, xv = _ffn_conv_piece(buf, w_ref, r0, blocks[1])
                for j in range(8):
                    sg = _sigmoid(gates[j])
                    da = _ffn_rows(stage, c, r0, j)
                    d_cs = (da * vals[j] * (sg * (1.0 + gates[j] * (1.0 - sg))), da * (gates[j] * sg))
                    for half, (d_c, xs) in enumerate(zip(d_cs, (xg, xv))):
                        dcbuf.at[blocks[half]][pl.ds(r0 + j, 8, stride=8), :] = d_c
                        for k in range(FFN_CONV_WIDTH):
                            dws[half][k] = dws[half][k] + d_c * xs[j + k]
            for half in range(2):
                cs = slice(blocks[half] * FFN_LANES, (blocks[half] + 1) * FFN_LANES)
                for k in range(FFN_CONV_WIDTH):
                    dw_ref[k:k + 1, cs] += jnp.sum(dws[half][k], axis=0, keepdims=True)
            for b in blocks:
                cs = slice(b * FFN_LANES, (b + 1) * FFN_LANES)
                ws = [w_ref[k:k + 1, cs] for k in range(FFN_CONV_WIDTH)]
                for r0 in range(0, ts, FFN_GROUP):
                    ds = [_ffn_rows(dcbuf, b, r0, j) for j in range(8 + FFN_CONV_WIDTH - 1)]
                    for j in range(8):
                        d_x = ws[FFN_CONV_WIDTH - 1] * ds[j]
                        for k in range(FFN_CONV_WIDTH - 1):
                            d_x = d_x + ws[k] * ds[j + FFN_CONV_WIDTH - 1 - k]
                        stage.at[c][pl.ds(r0 + j, 8, stride=8), :] = d_x
                o_ref[:, cs] = stage[c].astype(BF16)
                dcbuf[b, ts:ts + FFN_HALO, :] = dcbuf[b, 0:FFN_HALO, :]

    rev = lambda i: n - 1 - i
    cur, halo, wspec = _ffn_specs(ts, 2 * tc, rev, l)
    return pl.pallas_call(
        body, name=name,
        out_shape=(jax.ShapeDtypeStruct((S, F2), BF16), jax.ShapeDtypeStruct((FFN_HALO, F2), F32)),
        grid=(2, n),
        in_specs=[cur, halo, pl.BlockSpec((ts, tc), lambda c, i: (rev(i), c)), wspec],
        out_specs=(cur, pl.BlockSpec((FFN_HALO, 2 * tc), lambda c, i: (0, c))),
        scratch_shapes=[pltpu.VMEM((2 * nb, FFN_HALO + ts, FFN_LANES), F32), pltpu.VMEM((2 * nb, ts + FFN_HALO, FFN_LANES), F32),
                        pltpu.VMEM((nb, ts, FFN_LANES), F32)],
        compiler_params=_params('parallel', 'arbitrary'),
    )(up, up, d_act, w)


def _loss_head(y, target, *, name, ts=512):
    S, D = y.shape
    ts = _tile(S, ts, 8)

    def body(y_ref, t_ref, l_ref, dy_ref):
        i = pl.program_id(0)
        err = y_ref[...] - t_ref[...]
        dy_ref[...] = err * (1.0 / D)
        part = jnp.sum(jnp.sum(err * err, axis=1, keepdims=True), axis=0, keepdims=True) * (0.5 / D)

        @pl.when(i == 0)
        def _():
            l_ref[...] = part

        @pl.when(i > 0)
        def _():
            l_ref[...] += part

    row = pl.BlockSpec((ts, D), lambda i: (i, 0))
    return pl.pallas_call(
        body, name=name,
        out_shape=(jax.ShapeDtypeStruct((1, 1), F32), jax.ShapeDtypeStruct((S, D), F32)),
        grid=(S // ts,), in_specs=[row, row], out_specs=(pl.BlockSpec((1, 1), lambda i: (0, 0)), row),
        compiler_params=_params('arbitrary'),
    )(y, target)


def _pair_cols(w):
    lead, f2 = w.shape[:-1], w.shape[-1]
    return w.reshape(lead + (2, 2, f2 // 4)).swapaxes(-3, -2).reshape(lead + (f2,))


def _pad_axis(w, size, axis):
    pad = [(0, 0)] * w.ndim
    pad[axis] = (0, size - w.shape[axis])
    return jnp.pad(w, pad)


def _block_diag(pool_w):
    g = pool_w.shape[0]
    rows = [jnp.concatenate([pool_w[i] if i == j else jnp.zeros_like(pool_w[i]) for j in range(g)], axis=1) for i in range(g)]
    return jnp.concatenate(rows, axis=0)


def _small_weights(w, l):
    return dict(
        norm1_g=w['norm1_g'][l][None, :],
        b_col=_pad_axis(w['b_f'][l][:, None], FG_ROWS, 0),
        qg=jnp.tile(w['q_norm_g'][l], N_HEADS)[None, :],
        kg=jnp.tile(w['k_norm_g'][l], N_HEADS)[None, :],
        dw_b=w['conv_dw_b'][l][None, :], ln_g=w['conv_ln_g'][l][None, :], ln_b=w['conv_ln_b'][l][None, :],
        wbd=_block_diag(w['pool_w'][l]).astype(BF16),
        pool_scale=w['pool_scale'][l][None, :],
        norm2_g=w['norm2_g'][l][None, :],
    )


def _layer_fwd(x, W, p, l):
    n = lambda s: f'l{l}_{s}'
    S = x.shape[0]
    h = _rms_fwd(x, p['norm1_g'], name=n('norm1'))
    proj_a = _mm(h, W.get('w_a', h), b_lead=0, name=n('proj_a'), tn=D_PROJ_A)
    z_raw = _mm(W.get('w_fg_t', h), h, a_lead=0, tb=True, name=n('proj_fg'))
    qkv = _qk_prep_fwd(proj_a, p['qg'], p['kg'], name=n('qk_norm'))
    f_cum = _forget_fwd(z_raw, p['b_col'], name=n('forget'))
    f3 = f_cum[:N_HEADS].reshape(N_HEADS // 2, 2, S)
    mix, att, lse = _attn_fwd(*_attn_aug(qkv, f_cum, name=n('attn_aug')), name=n('attn'))
    mix = _conv_fwd(proj_a, mix, W.get('dw_w', h), p['dw_b'], p['ln_g'], p['ln_b'], W.get('pw_w', h), 0, name=n('conv'))
    mix = _pool_fwd(proj_a, mix, p['wbd'], p['pool_scale'], name=n('pool'))
    x1 = _mm(mix, W.get('w_out', mix), b_lead=0, res=x, name=n('out_proj'), tn=1024)
    h2 = _rms_fwd(x1, p['norm2_g'], name=n('norm2'))
    up = _mm(h2, W.get('w_up', mix), b_lead=0, name=n('up_proj'), tn=1408, cols_outer=True)
    act = _ffn_act_fwd(up, W.get('ffn_w', h), 0, name=n('ffn_act'))
    x2 = _mm(act, W.get('w_down', mix), b_lead=0, res=x1, name=n('down_proj'), tn=1024, tk=2816)
    saved = dict(x=x, h=h, proj_a=proj_a, z_raw=z_raw, qkv=qkv, f3=f3, att=att, lse=lse, mix=mix, x1=x1, h2=h2, up=up, act=act)
    return x2, saved


def _layer_bwd(dx2, W, p, s, l, sink):
    n = lambda t: f'l{l}_{t}'
    S = dx2.shape[0]
    g = {}
    W = W.ready

    def large(key, a, b, **kw):
        return sink.put(l, key, *_mm(a, b, ta=True, copy16=True, name=n('d_' + key), **kw))

    d_act = _mm(dx2, W['w_down'], b_lead=0, tb=True, name=n('d_act'), tn=1408, cols_outer=True)
    large('w_down', s['act'], dx2, tm=1408, tn=1024)
    d_up, d_ffn_w = _ffn_act_bwd(s['up'], d_act, W['ffn_w'], 0, name=n('ffn_act_bwd'))
    g['ffn_dw_w'] = _pair_cols(d_ffn_w[:FFN_CONV_WIDTH])
    d_h2 = _mm(d_up, W['w_up'], b_lead=0, tb=True, name=n('d_h2'), tn=1024, tk=5632)
    started = large('w_up', s['h2'], d_up, tm=1024, tn=512, tk=4096)
    dx1, dg2 = _rms_bwd(s['x1'], p['norm2_g'], d_h2, dx2, name=n('norm2_bwd'))
    g['norm2_g'] = dg2[0]
    sink.point(l, 'mid', dx1)
    d_mix = _mm(dx1, W['w_out'], b_lead=0, tb=True, name=n('d_mix'), tn=1024, after=started)
    large('w_out', s['mix'], dx1, tm=1024, tn=1024)
    dq, dk, dv, df3, dr = _attn_bwd(s['qkv'], s['f3'], s['att'], s['lse'], d_mix, name=n('attn_bwd'))
    df = _pad_axis(df3.reshape(N_HEADS, S) + dr[:, ::HEAD_DIM].T, FG_ROWS, 0)
    d_z, d_b = _forget_bwd(s['z_raw'], p['b_col'], df, name=n('forget_bwd'))
    g['b_f'] = d_b[:N_HEADS, 0]
    d_proj, d_qg, d_kg = _qk_prep_bwd(s['proj_a'], dq, dk, dv, p['qg'], p['kg'], name=n('qk_norm_bwd'))
    g['q_norm_g'] = d_qg.reshape(N_HEADS, HEAD_DIM).sum(axis=0)
    g['k_norm_g'] = d_kg.reshape(N_HEADS, HEAD_DIM).sum(axis=0)
    d_proj, d_dw_w, d_dw_b, d_ln_g, d_ln_b, d_pw = _conv_bwd(
        s['proj_a'], d_mix, d_proj, W['dw_w'], p['dw_b'], p['ln_g'], p['ln_b'], W['pw_w'], 0, name=n('conv_bwd'))
    g['conv_dw_w'], g['conv_dw_b'] = d_dw_w[:CONV_WIDTH], d_dw_b[0]
    g['conv_ln_g'], g['conv_ln_b'], g['conv_pw_w'] = d_ln_g[0], d_ln_b[0], d_pw
    d_proj, d_wbd, d_scale = _pool_bwd(s['proj_a'], d_mix, d_proj, p['wbd'], p['pool_scale'], name=n('pool_bwd'))
    g['pool_w'] = jnp.stack([d_wbd[i * POOL_GROUP:(i + 1) * POOL_GROUP, i * POOL_GROUP:(i + 1) * POOL_GROUP]
                             for i in range(len(POOL_WINDOWS))])
    g['pool_scale'] = d_scale[0]
    d_w_a = _mm(s['h'], d_proj, ta=True, name=n('d_w_a'), tm=1024, tn=768, tk=4096)
    started = sink.put(l, 'w_in', d_w_a, _mm(d_z, s['h'], name=n('d_w_fg'), tn=1024).T)
    d_h_fg = _mm(d_z, W['w_fg_t'], b_lead=0, ta=True, name=n('d_h_fg'), tn=1024, after=started)
    d_h = _mm(d_proj, W['w_a'], b_lead=0, tb=True, res=d_h_fg, name=n('d_h'), tn=1024, tk=D_PROJ_A)
    dx, dg1 = _rms_bwd(s['x'], p['norm1_g'], d_h, dx1, name=n('norm1_bwd'))
    g['norm1_g'] = dg1[0]
    sink.point(l, 'end', dx)
    return dx, g


SMALL_GRADS = REPLICATED + ('conv_dw_w', 'conv_pw_w', 'ffn_dw_w')


def _local_step(x, target, W, w_small, sink):
    depth = w_small['norm1_g'].shape[0]
    ps, saved = [], []
    for l in range(depth):
        p = _small_weights(w_small, l)
        x, s = _layer_fwd(x, W[l], p, l)
        ps.append(p)
        saved.append(s)
    loss, dx = _loss_head(x, target, name='loss_head')
    small = [None] * depth
    for l in reversed(range(depth)):
        dx, small[l] = _layer_bwd(dx, W[l], ps[l], saved[l], l, sink)
    return loss, dx, {k: jnp.stack([small[l][k] for l in range(depth)]) for k in SMALL_GRADS}


W_IN_SHARD = D_IN // N_CHIPS
W_IN_PAD = 640
N_A_TILES = D_PROJ_A // LANES
FG_COL0 = D_QKV


def _a_tile_base(j):
    if j == N_A_TILES:
        return FG_COL0, N_HEADS
    return (j * LANES if j * LANES < FG_COL0 else j * LANES + N_HEADS), LANES


def _shift_select(rows, cols, shift, row_max, col_max):
    r = lax.broadcasted_iota(jnp.int32, (rows, cols), 0)
    c = lax.broadcasted_iota(jnp.int32, (rows, cols), 1)
    return ((r + shift == c) & (r < row_max) & (c < col_max)).astype(BF16)


def _select_w_in(raw, *, name, tm=256):
    _, D, _ = raw.shape
    tm = _tile(D, tm, 16)
    plan = []
    for j in range(N_A_TILES + 1):
        base, cmax = _a_tile_base(j)
        parts = []
        for p in range(N_CHIPS):
            delta = base - W_IN_SHARD * p
            lo, hi = max(0, delta), min(W_IN_SHARD - 1, delta + cmax - 1)
            if lo > hi:
                continue
            a0 = (lo // LANES) * LANES
            kw = min(-(-(hi + 1 - a0) // LANES) * LANES, W_IN_PAD - a0)
            parts.append((p, a0, kw, delta))
        plan.append((cmax, parts))

    def body(raw_ref, wa_ref, fg_ref):
        for j, (cmax, parts) in enumerate(plan):
            acc = None
            for p, a0, kw, delta in parts:
                sel = _shift_select(kw, LANES, a0 - delta, W_IN_SHARD - a0, cmax)
                t = _dot(raw_ref[p, :, a0:a0 + kw], sel, 1, 0)
                acc = t if acc is None else acc + t
            if j == N_A_TILES:
                fg_ref[...] = acc.astype(BF16)
            else:
                wa_ref[:, j * LANES:(j + 1) * LANES] = acc.astype(BF16)

    return pl.pallas_call(
        body, name=name,
        out_shape=(jax.ShapeDtypeStruct((D, D_PROJ_A), BF16), jax.ShapeDtypeStruct((D, LANES), BF16)),
        grid=(D // tm,),
        in_specs=[pl.BlockSpec((N_CHIPS, tm, W_IN_PAD), lambda i: (0, i, 0))],
        out_specs=(pl.BlockSpec((tm, D_PROJ_A), lambda i: (i, 0)), pl.BlockSpec((tm, LANES), lambda i: (i, 0))),
        compiler_params=_params('parallel'),
    )(raw)


def _select_w_in_grads(p_a, p_fg, *, name, tm=256):
    D = p_a.shape[0]
    tm = _tile(D, tm, 16)
    n_local = W_IN_PAD // LANES
    plan = []
    for p in range(N_CHIPS):
        for i in range(n_local):
            cmax = max(0, min(LANES, W_IN_SHARD - i * LANES))
            parts = []
            for j in range(N_A_TILES + 1):
                base, rmax = _a_tile_base(j)
                e = base - W_IN_SHARD * p - i * LANES
                if e + rmax - 1 < 0 or e > cmax - 1:
                    continue
                parts.append((j, e, rmax))
            plan.append((p, i, cmax, parts))

    def body(a_ref, fg_ref, o32_ref, o16_ref):
        terms = {}

        def src(j):
            if j not in terms:
                v = fg_ref[...] if j == N_A_TILES else a_ref[:, j * LANES:(j + 1) * LANES]
                terms[j] = _split3(v)
            return terms[j]

        for p, i, cmax, parts in plan:
            acc = jnp.zeros((tm, LANES), F32)
            for j, e, rmax in parts:
                sel = _shift_select(LANES, LANES, e, rmax, cmax)
                for term in src(j):
                    acc = acc + _dot(term, sel, 1, 0)
            o32_ref[p, :, i * LANES:(i + 1) * LANES] = acc
            o16_ref[p, :, i * LANES:(i + 1) * LANES] = acc.astype(BF16)

    out = pl.BlockSpec((N_CHIPS, tm, W_IN_PAD), lambda i: (0, i, 0))
    return pl.pallas_call(
        body, name=name,
        out_shape=(jax.ShapeDtypeStruct((N_CHIPS, D, W_IN_PAD), F32), jax.ShapeDtypeStruct((N_CHIPS, D, W_IN_PAD), BF16)),
        grid=(D // tm,),
        in_specs=[pl.BlockSpec((tm, D_PROJ_A), lambda i: (i, 0)), pl.BlockSpec((tm, LANES), lambda i: (i, 0))],
        out_specs=(out, out),
        compiler_params=_params('parallel'),
    )(p_a, p_fg)


MESH = pl.DeviceIdType.MESH
HBM_SPEC = pl.BlockSpec(memory_space=pltpu.HBM)


def _place():
    return lax.axis_index('x'), lax.axis_index('y'), lax.axis_index('c')


def _other_chips(x, y):
    return [(1 - x, y), (x, 1 - y), (1 - x, 1 - y)]


def _up_pos(q):
    return (q % 2) * 2 + q // 2


CHUNKS = {
    'w_in': ('lead', None),
    'w_up': ('cols', None),
    'w_down': ('rows', None),
    'w_out': ('rows', None),
    'conv_pw_w': ('rows', None),
    'conv_dw_w': ('lead', None),
    'ffn_dw_w': ('lead', None),
}


def _window(ref, kind, l, q):
    at = (lambda *idx: ref.at[idx]) if l is None else (lambda *idx: ref.at[(l,) + idx])
    shape = ref.shape if l is None else ref.shape[1:]
    if kind == 'lead':
        return at(q)
    if kind == 'rows':
        cs = shape[0] // N_CHIPS
        return at(pl.ds(pl.multiple_of(q * cs, 16), cs), slice(None))
    cs = shape[1] // N_CHIPS
    return at(slice(None), pl.ds(pl.multiple_of(_up_pos(q) * cs, LANES), cs))


def _place_shard(src, l, pos_arr, full_shape, kind, *, name, tm=256):
    _, m, n = src.shape
    bm = _tile(m, tm, 16) if kind != 'rows' else m

    def body(pos_ref, s_ref, o_ref):
        o_ref[...] = s_ref[...].astype(BF16)

    if kind == 'lead':
        out = pl.BlockSpec((None, bm, n), lambda i, pos: (pos[0], i, 0))
    elif kind == 'rows':
        out = pl.BlockSpec((bm, n), lambda i, pos: (pos[0], 0))
    else:
        out = pl.BlockSpec((bm, n), lambda i, pos: (i, pos[0]))
    return pl.pallas_call(
        body, name=name, out_shape=jax.ShapeDtypeStruct(full_shape, BF16),
        grid_spec=pltpu.PrefetchScalarGridSpec(
            num_scalar_prefetch=1, grid=(m // bm,),
            in_specs=[pl.BlockSpec((None, bm, n), lambda i, pos: (l, i, 0))], out_specs=out),
        compiler_params=_params('parallel'),
    )(pos_arr, src)


GATHERED = ('w_in', 'w_up', 'w_down', 'w_out', 'conv_pw_w', 'conv_dw_w', 'ffn_dw_w')
GATHER_GROUPS = ((0, ('w_in', 'conv_dw_w', 'ffn_dw_w', 'conv_pw_w')), (0, ('w_out', 'w_up', 'w_down')), (1, GATHERED))
SEM_SPEC = pl.BlockSpec(memory_space=pltpu.SEMAPHORE)
SPLIT_COPY_PARAMS = pltpu.CompilerParams(has_side_effects=pltpu.SideEffectType.DATAFLOW_SIDE_EFFECTING)


def _gather_start(bufs):
    flat = [b for group in bufs for b in group]
    nb = len(flat)

    def body(*refs):
        outs, sems = refs[nb:2 * nb], refs[2 * nb:]
        x, y, c = _place()
        pos = 0
        for g, (_, keys) in enumerate(GATHER_GROUPS):
            for i, k in enumerate(keys):
                w = _window(outs[pos], CHUNKS[k][0], None, 2 * x + y)
                pos += 1
                for j, chip in enumerate(_other_chips(x, y)):
                    pltpu.make_async_remote_copy(src_ref=w, dst_ref=w, send_sem=sems[2 * g].at[3 * i + j],
                                                 recv_sem=sems[2 * g + 1].at[3 * i + j], device_id=(*chip, c),
                                                 device_id_type=MESH).start()

    sem_shapes = [pltpu.SemaphoreType.DMA((3 * len(keys),)) for _, keys in GATHER_GROUPS for _ in range(2)]
    res = pl.pallas_call(
        body, name='gather_start',
        out_shape=tuple(jax.ShapeDtypeStruct(b.shape, b.dtype) for b in flat) + tuple(sem_shapes),
        in_specs=[HBM_SPEC] * nb, out_specs=tuple([HBM_SPEC] * nb + [SEM_SPEC] * len(sem_shapes)),
        input_output_aliases={b: b for b in range(nb)},
        compiler_params=SPLIT_COPY_PARAMS,
    )(*[pltpu.with_memory_space_constraint(b, pltpu.HBM) for b in flat])
    out_bufs, sems, pos = [], res[nb:], 0
    for group in bufs:
        out_bufs.append(list(res[pos:pos + len(group)]))
        pos += len(group)
    return out_bufs, [(sems[2 * g], sems[2 * g + 1]) for g in range(len(GATHER_GROUPS))]


def _gather_wait(g, bufs, sems, after):
    keys = GATHER_GROUPS[g][1]
    nb = len(bufs)

    def body(*refs):
        send_sems, recv_sems = refs[nb], refs[nb + 1]
        outs = refs[nb + 3:]
        x, y, c = _place()
        for i, k in enumerate(keys):
            mine = _window(outs[i], CHUNKS[k][0], None, 2 * x + y)
            for j, (cx, cy) in enumerate(_other_chips(x, y)):
                theirs = _window(outs[i], CHUNKS[k][0], None, 2 * cx + cy)
                cp = pltpu.make_async_remote_copy(src_ref=mine, dst_ref=theirs, send_sem=send_sems.at[3 * i + j],
                                                  recv_sem=recv_sems.at[3 * i + j], device_id=(cx, cy, c), device_id_type=MESH)
                cp.wait_send()
                cp.wait_recv()

    return pl.pallas_call(
        body, name=f'gather_wait_{g}',
        out_shape=tuple(jax.ShapeDtypeStruct(b.shape, b.dtype) for b in bufs),
        in_specs=[HBM_SPEC] * nb + [SEM_SPEC, SEM_SPEC, ANY_SPEC], out_specs=tuple([HBM_SPEC] * nb),
        input_output_aliases={b: b for b in range(nb)},
        compiler_params=SPLIT_COPY_PARAMS,
    )(*bufs, *sems, after)


def _rs_block(M, N):
    return (_tile(M, 256, 16), _tile(N, 2048))


def _chunk_shape(shape, kind):
    if kind == 'lead':
        return tuple(shape[1:])
    if kind == 'rows':
        return (shape[0] // N_CHIPS, shape[1])
    return (shape[0], shape[1] // N_CHIPS)


def _rs_start(tag, bufs, kinds):
    nb = len(bufs)
    lands = [lax.empty((N_CHIPS - 1,) + _chunk_shape(b.shape, k), b.dtype) for b, k in zip(bufs, kinds)]

    def body(*refs):
        src, land = refs[2 * nb:3 * nb], refs[3 * nb:4 * nb]
        send_sems, recv_sems, token = refs[4 * nb:]
        token[...] = jnp.zeros(token.shape, F32)
        x, y, c = _place()
        for b in range(nb):
            for j, (cx, cy) in enumerate(_other_chips(x, y)):
                pltpu.make_async_remote_copy(
                    src_ref=_window(src[b], kinds[b], None, 2 * cx + cy), dst_ref=land[b].at[j],
                    send_sem=send_sems.at[3 * b + j], recv_sem=recv_sems.at[3 * b + j],
                    device_id=(cx, cy, c), device_id_type=MESH).start()

    sem = pltpu.SemaphoreType.DMA((3 * nb,))
    res = pl.pallas_call(
        body, name=f'rs_start_{tag}',
        out_shape=tuple(jax.ShapeDtypeStruct(b.shape, b.dtype) for b in list(bufs) + lands)
        + (sem, sem, jax.ShapeDtypeStruct((8, LANES), F32)),
        in_specs=[HBM_SPEC] * (2 * nb),
        out_specs=tuple([HBM_SPEC] * (2 * nb) + [SEM_SPEC, SEM_SPEC, pl.BlockSpec(memory_space=pltpu.VMEM)]),
        input_output_aliases={b: b for b in range(2 * nb)},
        compiler_params=SPLIT_COPY_PARAMS,
    )(*[pltpu.with_memory_space_constraint(b, pltpu.HBM) for b in list(bufs) + lands])
    return res[:nb], res[nb:2 * nb], res[2 * nb:2 * nb + 2], res[2 * nb + 2]


def _rs_wait(tag, bufs, lands, sems, kinds, after):
    nb = len(bufs)

    def body(*refs):
        send_sems, recv_sems = refs[2 * nb], refs[2 * nb + 1]
        src, land = refs[2 * nb + 3:3 * nb + 3], refs[3 * nb + 3:]
        x, y, c = _place()
        for b in range(nb):
            for j, (cx, cy) in enumerate(_other_chips(x, y)):
                cp = pltpu.make_async_remote_copy(
                    src_ref=_window(src[b], kinds[b], None, 2 * cx + cy), dst_ref=land[b].at[j],
                    send_sem=send_sems.at[3 * b + j], recv_sem=recv_sems.at[3 * b + j],
                    device_id=(cx, cy, c), device_id_type=MESH)
                cp.wait_send()
                cp.wait_recv()

    res = pl.pallas_call(
        body, name=f'rs_wait_{tag}',
        out_shape=tuple(jax.ShapeDtypeStruct(b.shape, b.dtype) for b in list(bufs) + list(lands)),
        in_specs=[HBM_SPEC] * (2 * nb) + [SEM_SPEC, SEM_SPEC, ANY_SPEC], out_specs=tuple([HBM_SPEC] * (2 * nb)),
        input_output_aliases={b: b for b in range(2 * nb)},
        compiler_params=SPLIT_COPY_PARAMS,
    )(*bufs, *lands, *sems, after)
    return res[nb:]


def _rs_sum(p, rb, kind, pos_arr, l, depth, buf, *, name):
    m, n = rb.shape[1:]
    bm, bn = _rs_block(m, n)
    nbm, nbn = m // bm, n // bn
    has_buf = buf is not None

    def body(q_ref, p_ref, r_ref, *rest):
        acc = p_ref[...]
        for j in range(N_CHIPS - 1):
            acc = acc + r_ref[j].astype(F32)
        rest[-1][...] = acc

    if kind == 'lead':
        p_map = lambda i, j, q: (q[0], i, j)
    elif kind == 'rows':
        p_map = lambda i, j, q: (q[0] * nbm + i, j)
    else:
        p_map = lambda i, j, q: (i, q[0] * nbn + j)
    r_spec = pl.BlockSpec((N_CHIPS - 1, bm, bn), lambda i, j, q: (0, i, j))
    p_spec = pl.BlockSpec(((None,) if kind == 'lead' else ()) + (bm, bn), p_map)
    return pl.pallas_call(
        body, name=name, out_shape=jax.ShapeDtypeStruct((depth, m, n), F32),
        grid_spec=pltpu.PrefetchScalarGridSpec(
            num_scalar_prefetch=1, grid=(nbm, nbn), in_specs=[p_spec, r_spec] + ([ANY_SPEC] if has_buf else []),
            out_specs=pl.BlockSpec((None, bm, bn), lambda i, j, q: (l, i, j))),
        input_output_aliases={3: 0} if has_buf else {},
        compiler_params=_params('parallel', 'parallel'),
    )(pos_arr, p, rb, *((buf,) if has_buf else ()))


def _swap_with_sibling(bufs):
    nb = len(bufs)

    def body(*refs):
        ins, outs = refs[:nb], refs[nb:2 * nb]
        send_sems, recv_sems = refs[2 * nb:]
        x, y, c = _place()
        copies = [pltpu.make_async_remote_copy(src_ref=ins[b], dst_ref=outs[b], send_sem=send_sems.at[b],
                                               recv_sem=recv_sems.at[b], device_id=(x, y, 1 - c), device_id_type=MESH)
                  for b in range(nb)]
        for cp in copies:
            cp.start()
        for cp in copies:
            cp.wait()

    return pl.pallas_call(
        body, name='rs_swap_sums', out_shape=tuple(jax.ShapeDtypeStruct(b.shape, b.dtype) for b in bufs),
        in_specs=[HBM_SPEC] * nb, out_specs=tuple([HBM_SPEC] * nb),
        scratch_shapes=[pltpu.SemaphoreType.DMA((nb,)), pltpu.SemaphoreType.DMA((nb,))],
    )(*bufs)


def _all_reduce_small(v):
    r = v.shape[0]

    def body(x_ref, tot_ref, all_ref, send_sems, recv_sems):
        x, y, c = _place()
        me, sibling = (x, y, c), (x, y, 1 - c)
        chips = _other_chips(x, y)

        def rows(px, py, pc):
            return all_ref.at[pl.ds((4 * px + 2 * py + pc) * r, r), :]

        def copy(k, block, to, src=None):
            return pltpu.make_async_remote_copy(
                src_ref=rows(*block) if src is None else src, dst_ref=rows(*block),
                send_sem=send_sems.at[k], recv_sem=recv_sems.at[k], device_id=to, device_id_type=MESH)

        rows(*me)[...] = x_ref[...]
        first = [copy(0, me, sibling, src=x_ref)]
        first += [copy(1 + j, me, (*chip, c), src=x_ref) for j, chip in enumerate(chips)]
        for cp in first:
            cp.start()
        passed = [copy(4 + j, (*chip, c), sibling) for j, chip in enumerate(chips)]
        for j, chip in enumerate(chips):
            copy(1 + j, (*chip, c), me).wait_recv()
            passed[j].start()
        copy(0, sibling, me).wait_recv()
        for j, chip in enumerate(chips):
            copy(4 + j, (*chip, 1 - c), me).wait_recv()
        for cp in first + passed:
            cp.wait_send()
        acc = all_ref[0:r, :]
        for d in range(1, N_DEV):
            acc = acc + all_ref[d * r:(d + 1) * r, :]
        tot_ref[...] = acc

    return pl.pallas_call(
        body, name='all_reduce_small', out_shape=jax.ShapeDtypeStruct((r, LANES), F32),
        in_specs=[pl.BlockSpec(memory_space=pltpu.VMEM)], out_specs=pl.BlockSpec(memory_space=pltpu.VMEM),
        scratch_shapes=[pltpu.VMEM((N_DEV * r, LANES), F32), pltpu.SemaphoreType.DMA((7,)), pltpu.SemaphoreType.DMA((7,))],
    )(v)


def _adamw(w, g, m, v, *, name, g2=None, ts=256):
    R, C = w.shape
    Cg = g.shape[1]
    ts = _tile(R, ts, 8)
    c1 = 1.0 - ADAM_B1 ** ADAM_STEP
    c2 = 1.0 - ADAM_B2 ** ADAM_STEP
    two = g2 is not None

    def body(w_ref, g_ref, *rest):
        m_ref, v_ref, go_ref, d_ref, nm_ref, nv_ref = rest[two:]
        gv = g_ref[:, 0:C]
        if two:
            gv = gv + rest[0][:, 0:C]
        nm = ADAM_B1 * m_ref[...] + (1.0 - ADAM_B1) * gv
        nv = ADAM_B2 * v_ref[...] + (1.0 - ADAM_B2) * (gv * gv)
        d_ref[...] = -ADAM_LR * ((nm / c1) / (jnp.sqrt(nv / c2) + ADAM_EPS) + ADAM_WD * w_ref[...])
        go_ref[...] = gv
        nm_ref[...] = nm
        nv_ref[...] = nv

    blk = pl.BlockSpec((ts, C), lambda i: (i, 0))
    gblk = pl.BlockSpec((ts, Cg), lambda i: (i, 0))
    shape = jax.ShapeDtypeStruct((R, C), F32)
    return pl.pallas_call(body, name=name, out_shape=(shape, shape, shape, shape), grid=(R // ts,),
                          in_specs=[blk, gblk] + ([gblk] if two else []) + [blk, blk], out_specs=(blk, blk, blk, blk),
                          compiler_params=_params('parallel'))(w, g, *((g2,) if two else ()), m, v)


def _pack_rows(parts, row_unit):
    flat = jnp.concatenate(parts)
    flat = _pad_axis(flat, -(-flat.shape[0] // (row_unit * LANES)) * row_unit * LANES, 0)
    return flat.reshape(-1, LANES)


def _as_2d(a):
    return a.reshape(-1, a.shape[-1])


def _mesh_place():
    cx, cy, cc = _place()
    chip = 2 * cx + cy
    as_arr = lambda v: jnp.reshape(v, (1,)).astype(jnp.int32)
    return chip, as_arr(cc), as_arr(chip), as_arr(_up_pos(chip))


class _LayerWeights:
    def __init__(self, groups):
        self.groups = groups
        self.ready = {}

    def get(self, name, after):
        if name not in self.ready:
            for names, wait in self.groups:
                if name in names:
                    self.ready.update({k: v[None] for k, v in wait(after).items()})
        return self.ready[name]


def _gather_full(w, place):
    chip, _, chip_arr, up_pos_arr = place
    L, D = w['w_in'].shape[:2]
    w_in_pad = _pad_axis(w['w_in'], W_IN_PAD, 2)

    def placed(k, l):
        if k == 'w_in':
            return _place_shard(w_in_pad, l, chip_arr, (N_CHIPS, D, W_IN_PAD), 'lead', name=f'place_w_in_{l}')
        if k == 'w_up':
            return _place_shard(w[k], l, up_pos_arr, (w[k].shape[1], N_CHIPS * w[k].shape[2]), 'cols', name=f'place_w_up_{l}')
        if k in ('conv_dw_w', 'ffn_dw_w'):
            return lax.dynamic_update_slice_in_dim(jnp.zeros((N_CHIPS,) + w[k].shape[1:], F32), w[k][l][None], chip, axis=0)
        return _place_shard(w[k], l, chip_arr, (N_CHIPS * w[k].shape[1], w[k].shape[2]), 'rows', name=f'place_{k}_{l}')

    bufs, sems = _gather_start([[placed(k, l) for k in keys] for l, keys in GATHER_GROUPS])
    unchunk = lambda a: jnp.moveaxis(a, 0, 1).reshape(a.shape[1], -1)

    def waiter(g):
        l, keys = GATHER_GROUPS[g]

        def wait(after):
            full = dict(zip(keys, _gather_wait(g, bufs[g], sems[g], after)))
            out = {}
            if 'w_in' in full:
                out['w_a'], w_fg = _select_w_in(full['w_in'], name=f'select_w_in_{l}')
                out['w_fg_t'] = w_fg.T
            if 'conv_dw_w' in full:
                out['dw_w'] = _pad_axis(unchunk(full['conv_dw_w']), CONV_HALO, 0)
            if 'ffn_dw_w' in full:
                out['ffn_w'] = _pad_axis(_pair_cols(unchunk(full['ffn_dw_w'])), FFN_HALO, 0)
            if 'conv_pw_w' in full:
                out['pw_w'] = full['conv_pw_w']
            out.update({k: full[k] for k in ('w_up', 'w_down', 'w_out') if k in full})
            return out

        names = {'w_in': ('w_a', 'w_fg_t'), 'conv_dw_w': ('dw_w',), 'ffn_dw_w': ('ffn_w',), 'conv_pw_w': ('pw_w',)}
        return tuple(n for k in keys for n in names.get(k, (k,))), wait

    return [_LayerWeights([waiter(g) for g in range(len(GATHER_GROUPS)) if GATHER_GROUPS[g][0] == l]) for l in range(L)]


RS_WIRE = ('w_in', 'w_up', 'w_down', 'w_out')
RS_GROUPS = (('ffn', ('w_down', 'w_up')), ('mix', ('w_out', 'w_in')))


class _GradReducer:
    def __init__(self, place, depth):
        _, _, self.chip_arr, self.up_pos_arr = place
        self.depth = depth
        self.got = {}
        self.flying = {}
        self.sums = {}

    def put(self, l, key, g32, g16):
        if key == 'w_in':
            g32, g16 = _select_w_in_grads(g32, g16, name=f'l{l}_select_w_in_grads')
        self.got[(l, key)] = (g32, g16)
        for tag, keys in RS_GROUPS:
            if key == keys[-1]:
                kinds = [CHUNKS[k][0] for k in keys]
                bufs, lands, sems, token = _rs_start(f'l{l}_{tag}', [self.got[(l, k)][1] for k in keys], kinds)
                self.flying[(l, tag)] = (bufs, lands, sems, kinds)
                return token
        return None

    def point(self, l, where, after):
        if where == 'mid':
            self._land(l + 1, 'mix', after)
        else:
            self._land(l, 'ffn', after)

    def _land(self, l, tag, after):
        if (l, tag) not in self.flying:
            return
        bufs, lands, sems, kinds = self.flying.pop((l, tag))
        lands = _rs_wait(f'l{l}_{tag}', bufs, lands, sems, kinds, after)
        for k, rb, kind in zip(dict(RS_GROUPS)[tag], lands, kinds):
            pos = self.up_pos_arr if kind == 'cols' else self.chip_arr
            self.sums[k] = _rs_sum(self.got.pop((l, k))[0], rb, kind, pos, l, self.depth, self.sums.get(k), name=f'l{l}_rs_sum_{k}')

    def finish(self, after):
        for l, tag in list(self.flying):
            self._land(l, tag, after)
        mine = [self.sums[k] for k in RS_WIRE]
        return {k: pair for k, pair in zip(RS_WIRE, zip(mine, _swap_with_sibling(mine)))}


def kernel(x, norm1_g, w_in, b_f, q_norm_g, k_norm_g, conv_dw_w, conv_dw_b, conv_ln_g, conv_ln_b, conv_pw_w, pool_w, pool_scale, w_out, norm2_g, w_up, ffn_dw_w, w_down, loss_target, m_norm1_g, m_w_in, m_b_f, m_q_norm_g, m_k_norm_g, m_conv_dw_w, m_conv_dw_b, m_conv_ln_g, m_conv_ln_b, m_conv_pw_w, m_pool_w, m_pool_scale, m_w_out, m_norm2_g, m_w_up, m_ffn_dw_w, m_w_down, v_norm1_g, v_w_in, v_b_f, v_q_norm_g, v_k_norm_g, v_conv_dw_w, v_conv_dw_b, v_conv_ln_g, v_conv_ln_b, v_conv_pw_w, v_pool_w, v_pool_scale, v_w_out, v_norm2_g, v_w_up, v_ffn_dw_w, v_w_down):
    given = dict(locals())
    w = {k: given[k] for k in WEIGHTS}
    mom_m = {k: given['m_' + k] for k in WEIGHTS}
    mom_v = {k: given['v_' + k] for k in WEIGHTS}
    place = _mesh_place()
    chip = place[0]
    W = _gather_full(w, place)

    reducer = _GradReducer(place, norm1_g.shape[0])
    loss_part, grad_x, g_small = _local_step(x[0], loss_target[0], W, {k: w[k] for k in REPLICATED}, reducer)
    loss = lax.psum(loss_part[0, 0], ('x', 'y', 'c'))
    sums = reducer.finish(grad_x)

    small = _pack_rows([g_small[k].reshape(-1) for k in SMALL_GRADS], 8)
    small_sum = _all_reduce_small(small)

    g_sum, delta, new_m, new_v = {}, {}, {}, {}
    for k in RS_WIRE:
        outs = _adamw(_as_2d(w[k]), _as_2d(sums[k][0]), _as_2d(mom_m[k]), _as_2d(mom_v[k]), g2=_as_2d(sums[k][1]), name='adamw_' + k)
        g_sum[k], delta[k], new_m[k], new_v[k] = [o.reshape(w[k].shape) for o in outs]
    off = 0
    small_full = {}
    for k in SMALL_GRADS:
        small_full[k] = small_sum.reshape(-1)[off:off + g_small[k].size].reshape(g_small[k].shape)
        off += g_small[k].size
    small_g = {k: small_full[k] for k in REPLICATED}
    small_g['conv_dw_w'] = lax.dynamic_slice_in_dim(small_full['conv_dw_w'], chip * w['conv_dw_w'].shape[2], w['conv_dw_w'].shape[2], axis=2)
    small_g['conv_pw_w'] = lax.dynamic_slice_in_dim(small_full['conv_pw_w'], chip * w['conv_pw_w'].shape[1], w['conv_pw_w'].shape[1], axis=1)
    small_g['ffn_dw_w'] = lax.dynamic_slice_in_dim(small_full['ffn_dw_w'], chip * w['ffn_dw_w'].shape[2], w['ffn_dw_w'].shape[2], axis=2)
    pack_small = lambda t: _pack_rows([t[k].reshape(-1) for k in SMALL_GRADS], 256)
    outs = _adamw(pack_small(w), pack_small(small_g), pack_small(mom_m), pack_small(mom_v), name='adamw_small')
    off = 0
    for k in SMALL_GRADS:
        pieces = [o.reshape(-1)[off:off + w[k].size].reshape(w[k].shape) for o in outs]
        g_sum[k], delta[k], new_m[k], new_v[k] = pieces
        off += w[k].size

    return (loss, grad_x[None], *[g_sum[k] for k in WEIGHTS], *[delta[k] for k in WEIGHTS],
            *[new_m[k] for k in WEIGHTS], *[new_v[k] for k in WEIGHTS])
```

```python
import functools

import jax
import jax.numpy as jnp
from jax import lax
from jax.experimental import pallas as pl
from jax.experimental.pallas import tpu as pltpu

F32 = jnp.float32
BF16 = jnp.bfloat16

N_HEADS = 8
HEAD_DIM = 64
D_ATT = N_HEADS * HEAD_DIM
D_CONV = 256
D_POOL = 256
D_MIX = D_ATT + D_CONV + D_POOL
D_QKV = 3 * D_ATT
D_PROJ_A = D_QKV + 2 * D_CONV + D_POOL
D_IN = D_PROJ_A + N_HEADS
FG_ROWS = 128
CONV_WIDTH = 31
CONV_HALO = 32
POOL_WINDOWS = (2, 4, 8, 16)
POOL_GROUP = 64
POOL_HALO = 16
FFN_CONV_WIDTH = 3
FFN_HALO = 8
ATT_SCALE = HEAD_DIM ** -0.5
EPS = 1e-6
NEG = -1e30
LANES = 128

ADAM_LR = 0.001
ADAM_B1 = 0.9
ADAM_B2 = 0.999
ADAM_EPS = 1e-08
ADAM_WD = 0.01
ADAM_STEP = 10

N_CHIPS = 4
N_DEV = 8
VMEM_LIMIT_BYTES = 56 * 1024 * 1024

REPLICATED = ('norm1_g', 'b_f', 'q_norm_g', 'k_norm_g', 'conv_dw_b', 'conv_ln_g', 'conv_ln_b',
              'pool_w', 'pool_scale', 'norm2_g')
WEIGHTS = ('norm1_g', 'w_in', 'b_f', 'q_norm_g', 'k_norm_g', 'conv_dw_w', 'conv_dw_b', 'conv_ln_g',
           'conv_ln_b', 'conv_pw_w', 'pool_w', 'pool_scale', 'w_out', 'norm2_g', 'w_up', 'ffn_dw_w', 'w_down')


def _tile(dim, pref, unit=LANES):
    if dim <= pref:
        return dim
    t = (pref // unit) * unit
    while t >= unit:
        if dim % t == 0:
            return t
        t -= unit
    raise ValueError(f'no tile for {dim} (preferred {pref})')


def _params(*sem):
    return pltpu.CompilerParams(dimension_semantics=sem, vmem_limit_bytes=VMEM_LIMIT_BYTES)


def _sigmoid(x):
    return 1.0 / (1.0 + jnp.exp(-x))


def _dot(a, b, ca, cb):
    return lax.dot_general(a, b, (((ca,), (cb,)), ((), ())), preferred_element_type=F32)


def _split3(y):
    y1 = y.astype(BF16)
    r1 = y - y1.astype(F32)
    y2 = r1.astype(BF16)
    y3 = (r1 - y2.astype(F32)).astype(BF16)
    return y1, y2, y3


def _dot3(y, e, ca=1, cb=0):
    y1, y2, y3 = _split3(y)
    return _dot(y1, e, ca, cb) + _dot(y2, e, ca, cb) + _dot(y3, e, ca, cb)


def _lead(spec_shape, imap, lead):
    if lead is None:
        return pl.BlockSpec(spec_shape, imap)
    return pl.BlockSpec((None,) + spec_shape, lambda *g: (lead,) + imap(*g))


ANY_SPEC = pl.BlockSpec(memory_space=pl.ANY)


def _mm(a, b, *, name, ta=False, tb=False, res=None, out_dtype=F32, tm=512, tn=512, tk=1024,
        a_lead=None, b_lead=None, copy16=False, after=None, cols_outer=False):
    a2, b2 = a.shape[-2:], b.shape[-2:]
    K, M = a2 if ta else a2[::-1]
    N, Kb = b2 if tb else b2[::-1]
    assert K == Kb, (a.shape, b.shape)
    tm, tn, tk = _tile(M, tm), _tile(N, tn), _tile(K, tk)
    nk = K // tk
    ca = 0 if ta else 1
    cb = 1 if tb else 0
    has_res = res is not None
    n_in = 2 + has_res + (after is not None)
    n_out = 1 + copy16

    def body(*refs):
        a_ref, b_ref = refs[:2]
        r_ref = refs[2] if has_res else None
        o_refs = refs[n_in:n_in + n_out]
        scratch = refs[n_in + n_out:]

        def write(r):
            if has_res:
                r = r + r_ref[...]
            o_refs[0][...] = r.astype(out_dtype)
            if copy16:
                o_refs[1][...] = r.astype(BF16)

        p = _dot(a_ref[...].astype(BF16), b_ref[...].astype(BF16), ca, cb)
        if nk == 1:
            write(p)
        else:
            acc = scratch[0]
            k = pl.program_id(2)

            @pl.when(k == 0)
            def _():
                acc[...] = p

            @pl.when(k > 0)
            def _():
                acc[...] += p

            @pl.when(k == nk - 1)
            def _():
                write(acc[...])

    ij = (lambda g0, g1: (g1, g0)) if cols_outer else (lambda g0, g1: (g0, g1))
    at = lambda f: (lambda g0, g1, k: f(*ij(g0, g1), k))
    a_spec = _lead((tk, tm), at(lambda i, j, k: (k, i)), a_lead) if ta else _lead((tm, tk), at(lambda i, j, k: (i, k)), a_lead)
    b_spec = _lead((tn, tk), at(lambda i, j, k: (j, k)), b_lead) if tb else _lead((tk, tn), at(lambda i, j, k: (k, j)), b_lead)
    o_spec = pl.BlockSpec((tm, tn), at(lambda i, j, k: (i, j)))
    in_specs = [a_spec, b_spec] + ([o_spec] if has_res else []) + ([ANY_SPEC] if after is not None else [])
    args = (a, b) + ((res,) if has_res else ()) + ((after,) if after is not None else ())
    out_shape = [jax.ShapeDtypeStruct((M, N), out_dtype)] + ([jax.ShapeDtypeStruct((M, N), BF16)] if copy16 else [])
    out = pl.pallas_call(
        body, name=name,
        out_shape=tuple(out_shape),
        grid=ij(M // tm, N // tn) + (nk,),
        in_specs=in_specs, out_specs=tuple([o_spec] * n_out),
        scratch_shapes=[pltpu.VMEM((tm, tn), F32)] if nk > 1 else [],
        compiler_params=_params('parallel', 'parallel', 'arbitrary'),
    )(*args)
    return out if copy16 else out[0]


def _rms_fwd(x, g, *, name, ts=512):
    S, D = x.shape
    ts = _tile(S, ts, 8)

    def body(x_ref, g_ref, o_ref):
        xv = x_ref[...]
        r = lax.rsqrt(jnp.mean(xv * xv, axis=-1, keepdims=True) + EPS)
        o_ref[...] = (xv * r * g_ref[...]).astype(BF16)

    return pl.pallas_call(
        body, name=name, out_shape=jax.ShapeDtypeStruct((S, D), BF16), grid=(S // ts,),
        in_specs=[pl.BlockSpec((ts, D), lambda i: (i, 0)), pl.BlockSpec((1, D), lambda i: (0, 0))],
        out_specs=pl.BlockSpec((ts, D), lambda i: (i, 0)),
        compiler_params=_params('parallel'),
    )(x, g)


def _rms_bwd(x, g, dh, dres, *, name, ts=512):
    S, D = x.shape
    ts = _tile(S, ts, 8)

    def body(x_ref, g_ref, dh_ref, dr_ref, dx_ref, dg_ref):
        i = pl.program_id(0)
        xv = x_ref[...]
        r = lax.rsqrt(jnp.mean(xv * xv, axis=-1, keepdims=True) + EPS)
        y = xv * r
        dh_v = dh_ref[...]
        dy = dh_v * g_ref[...]
        dx_ref[...] = dr_ref[...] + r * (dy - y * jnp.mean(dy * y, axis=-1, keepdims=True))
        part = jnp.sum(dh_v * y, axis=0, keepdims=True)

        @pl.when(i == 0)
        def _():
            dg_ref[...] = part

        @pl.when(i > 0)
        def _():
            dg_ref[...] += part

    row = pl.BlockSpec((ts, D), lambda i: (i, 0))
    vec = pl.BlockSpec((1, D), lambda i: (0, 0))
    return pl.pallas_call(
        body, name=name,
        out_shape=(jax.ShapeDtypeStruct((S, D), F32), jax.ShapeDtypeStruct((1, D), F32)),
        grid=(S // ts,), in_specs=[row, vec, row, row], out_specs=(row, vec),
        compiler_params=_params('arbitrary'),
    )(x, g, dh, dres)


def _pair_ones():
    i = lax.broadcasted_iota(jnp.int32, (LANES, LANES), 0) // HEAD_DIM
    j = lax.broadcasted_iota(jnp.int32, (LANES, LANES), 1) // HEAD_DIM
    return (i == j).astype(BF16)


def _head_sums(y, e):
    return jnp.concatenate([_dot3(y[:, b * LANES:(b + 1) * LANES], e) for b in range(D_ATT // LANES)], axis=1)


def _qk_prep_fwd(proj_a, qg, kg, *, name, ts=512):
    S = proj_a.shape[0]
    ts = _tile(S, ts, 16)

    def body(q_ref, k_ref, v_ref, qg_ref, kg_ref, e_ref, o_ref):
        e = e_ref[...]

        def norm(xv, gain):
            ms = _head_sums(xv * xv, e) * (1.0 / HEAD_DIM)
            return xv * lax.rsqrt(ms + EPS) * gain

        o_ref[:, 0:D_ATT] = (norm(q_ref[...], qg_ref[...]) * ATT_SCALE).astype(BF16)
        o_ref[:, D_ATT:2 * D_ATT] = norm(k_ref[...], kg_ref[...]).astype(BF16)
        o_ref[:, 2 * D_ATT:3 * D_ATT] = v_ref[...].astype(BF16)

    col = lambda c: pl.BlockSpec((ts, D_ATT), lambda i: (i, c))
    vec = pl.BlockSpec((1, D_ATT), lambda i: (0, 0))
    return pl.pallas_call(
        body, name=name, out_shape=jax.ShapeDtypeStruct((S, D_QKV), BF16), grid=(S // ts,),
        in_specs=[col(0), col(1), col(2), vec, vec, pl.BlockSpec((LANES, LANES), lambda i: (0, 0))],
        out_specs=pl.BlockSpec((ts, D_QKV), lambda i: (i, 0)),
        compiler_params=_params('parallel'),
    )(proj_a, proj_a, proj_a, qg, kg, _pair_ones())


def _qk_prep_bwd(proj_a, dq, dk, dv, qg, kg, *, name, ts=512):
    S = proj_a.shape[0]
    ts = _tile(S, ts, 16)

    def body(q_ref, k_ref, dq_ref, dk_ref, dv_ref, qg_ref, kg_ref, e_ref, o_ref, dqg_ref, dkg_ref):
        i = pl.program_id(0)
        e = e_ref[...]

        def norm_bwd(xv, dn, gain, scale):
            ms = _head_sums(xv * xv, e) * (1.0 / HEAD_DIM)
            r = lax.rsqrt(ms + EPS)
            y = xv * r
            dy = dn * (gain * scale)
            mean = _head_sums(dy * y, e) * (1.0 / HEAD_DIM)
            return r * (dy - y * mean), jnp.sum(dn * y, axis=0, keepdims=True) * scale

        dq_raw, dqg = norm_bwd(q_ref[...], dq_ref[...], qg_ref[...], ATT_SCALE)
        dk_raw, dkg = norm_bwd(k_ref[...], dk_ref[...], kg_ref[...], 1.0)
        o_ref[:, 0:D_ATT] = dq_raw.astype(BF16)
        o_ref[:, D_ATT:2 * D_ATT] = dk_raw.astype(BF16)
        o_ref[:, 2 * D_ATT:3 * D_ATT] = dv_ref[...].astype(BF16)

        @pl.when(i == 0)
        def _():
            dqg_ref[...] = dqg
            dkg_ref[...] = dkg

        @pl.when(i > 0)
        def _():
            dqg_ref[...] += dqg
            dkg_ref[...] += dkg

    col = lambda c: pl.BlockSpec((ts, D_ATT), lambda i: (i, c))
    vec = pl.BlockSpec((1, D_ATT), lambda i: (0, 0))
    return pl.pallas_call(
        body, name=name,
        out_shape=(jax.ShapeDtypeStruct((S, D_PROJ_A), BF16), jax.ShapeDtypeStruct((1, D_ATT), F32),
                   jax.ShapeDtypeStruct((1, D_ATT), F32)),
        grid=(S // ts,),
        in_specs=[col(0), col(1), col(0), col(0), col(0), vec, vec, pl.BlockSpec((LANES, LANES), lambda i: (0, 0))],
        out_specs=(pl.BlockSpec((ts, D_QKV), lambda i: (i, 0)), vec, vec),
        compiler_params=_params('arbitrary'),
    )(proj_a, proj_a, dq, dk, dv, qg, kg, _pair_ones())


def _tri_ones(upper):
    i = lax.broadcasted_iota(jnp.int32, (LANES, LANES), 0)
    j = lax.broadcasted_iota(jnp.int32, (LANES, LANES), 1)
    return ((i <= j) if upper else (i >= j)).astype(BF16)


def _forget_fwd(z_raw, b_col, *, name):
    R, S = z_raw.shape
    nb = S // LANES

    def body(z_ref, b_ref, u_ref, f_ref):
        u = u_ref[...]
        carry = jnp.zeros((R, 1), F32)
        for j in range(nb):
            z = z_ref[:, j * LANES:(j + 1) * LANES] + b_ref[...]
            logf = jnp.minimum(z, 0.0) - jnp.log(1.0 + jnp.exp(-jnp.abs(z)))
            f_ref[:, j * LANES:(j + 1) * LANES] = _dot3(logf, u) + carry
            carry = carry + jnp.sum(logf, axis=1, keepdims=True)

    return pl.pallas_call(
        body, name=name, out_shape=jax.ShapeDtypeStruct((R, S), F32),
        compiler_params=pltpu.CompilerParams(vmem_limit_bytes=VMEM_LIMIT_BYTES),
    )(z_raw, b_col, _tri_ones(True))


def _forget_bwd(z_raw, b_col, df, *, name):
    R, S = z_raw.shape
    nb = S // LANES

    def body(z_ref, b_ref, df_ref, l_ref, dz_ref, db_ref):
        low = l_ref[...]
        carry = jnp.zeros((R, 1), F32)
        db = jnp.zeros((R, 1), F32)
        for j in reversed(range(nb)):
            d = df_ref[:, j * LANES:(j + 1) * LANES]
            dlogf = _dot3(d, low) + carry
            carry = carry + jnp.sum(d, axis=1, keepdims=True)
            z = z_ref[:, j * LANES:(j + 1) * LANES] + b_ref[...]
            dz = dlogf * _sigmoid(-z)
            dz_ref[:, j * LANES:(j + 1) * LANES] = dz
            db = db + jnp.sum(dz, axis=1, keepdims=True)
        db_ref[...] = db

    return pl.pallas_call(
        body, name=name,
        out_shape=(jax.ShapeDtypeStruct((R, S), F32), jax.ShapeDtypeStruct((R, 1), F32)),
        compiler_params=pltpu.CompilerParams(vmem_limit_bytes=VMEM_LIMIT_BYTES),
    )(z_raw, b_col, df, _tri_ones(False))


def _head_mask(hh):
    lane = lax.broadcasted_iota(jnp.int32, (1, LANES), 1)
    return (lane // HEAD_DIM) == hh


def _causal(s, qi, ki, t):
    rows = qi * t + lax.broadcasted_iota(jnp.int32, (t, t), 0)
    cols = ki * t + lax.broadcasted_iota(jnp.int32, (t, t), 1)
    return jnp.where(cols <= rows, s, NEG)


AUG = 2 * HEAD_DIM


def _aug_consts():
    i = lax.broadcasted_iota(jnp.int32, (D_ATT, N_HEADS * AUG), 0)
    j = lax.broadcasted_iota(jnp.int32, (D_ATT, N_HEADS * AUG), 1)
    spread = (j == (i // HEAD_DIM) * AUG + i % HEAD_DIM).astype(BF16)
    h = lax.broadcasted_iota(jnp.int32, (LANES, N_HEADS * AUG), 0)
    c = lax.broadcasted_iota(jnp.int32, (LANES, N_HEADS * AUG), 1)
    gate = [((c == h * AUG + HEAD_DIM + t) & (h < N_HEADS)).astype(BF16) for t in range(3)]
    lane = lax.broadcasted_iota(jnp.int32, (1, N_HEADS * AUG), 1) % AUG
    ones_q = ((lane >= HEAD_DIM) & (lane < HEAD_DIM + 3)).astype(F32)
    ones_v = (lane == HEAD_DIM).astype(F32)
    return spread, gate, ones_q, ones_v


def _attn_aug(qkv, f_cum, *, name, ts=512):
    S = qkv.shape[0]
    ts = _tile(S, ts)
    spread, gate, ones_q, ones_v = _aug_consts()
    W = N_HEADS * AUG

    def body(q_ref, k_ref, v_ref, f_ref, sp_ref, g0_ref, g1_ref, g2_ref, oq_ref, ov_ref, qa_ref, ka_ref, va_ref):
        sp = sp_ref[...]
        qa_ref[...] = (_dot(q_ref[...], sp, 1, 0) + oq_ref[...]).astype(BF16)
        va_ref[...] = (_dot(v_ref[...], sp, 1, 0) + ov_ref[...]).astype(BF16)
        terms = _split3(-jnp.transpose(f_ref[...]))
        ka = _dot(k_ref[...], sp, 1, 0)
        for t, g_ref in zip(terms, (g0_ref, g1_ref, g2_ref)):
            ka = ka + _dot(t, g_ref[...], 1, 0)
        ka_ref[...] = ka.astype(BF16)

    col = lambda c: pl.BlockSpec((ts, D_ATT), lambda i: (i, c))
    full = lambda a: pl.BlockSpec(a.shape, lambda i: (0, 0))
    out = pl.BlockSpec((ts, W), lambda i: (i, 0))
    shape = jax.ShapeDtypeStruct((S, W), BF16)
    consts = (spread, *gate, ones_q, ones_v)
    return pl.pallas_call(
        body, name=name, out_shape=(shape, shape, shape), grid=(S // ts,),
        in_specs=[col(0), col(1), col(2), pl.BlockSpec((FG_ROWS, ts), lambda i: (0, i))] + [full(a) for a in consts],
        out_specs=(out, out, out),
        compiler_params=_params('parallel'),
    )(qkv, qkv, qkv, f_cum, *consts)


def _attn_fwd(qa, ka, va, *, name, tq=1024, tk=1024):
    S = qa.shape[0]
    tq, tk = _tile(S, tq), _tile(S, tk)
    nq, nk = S // tq, S // tk
    npair = N_HEADS // 2

    def body(q_ref, k_ref, v_ref, mix_ref, o_ref, lse_ref, m_s, acc_s):
        qi, ki = pl.program_id(1), pl.program_id(2)
        last = (qi * tq + tq - 1) // tk
        first_masked = (qi * tq) // tk

        @pl.when(ki == 0)
        def _():
            m_s[...] = jnp.full(m_s.shape, NEG, F32)
            acc_s[...] = jnp.zeros(acc_s.shape, F32)

        def step(masked):
            if masked:
                rows = qi * tq + lax.broadcasted_iota(jnp.int32, (tq, tk), 0)
                cols = ki * tk + lax.broadcasted_iota(jnp.int32, (tq, tk), 1)
                keep = cols <= rows
            m_prev = [m_s[hh] for hh in range(2)]
            acc_prev = [acc_s[hh] for hh in range(2)]
            ss = []
            for hh in range(2):
                s = _dot(q_ref[:, hh * AUG:(hh + 1) * AUG], k_ref[:, hh * AUG:(hh + 1) * AUG], 1, 1)
                ss.append(jnp.where(keep, s, NEG) if masked else s)
            m_new = [jnp.maximum(m_prev[hh], jnp.max(ss[hh], axis=1, keepdims=True)) for hh in range(2)]
            ps = [jnp.exp(ss[hh] - jnp.tile(m_new[hh], (1, tk // LANES))).astype(BF16) for hh in range(2)]
            for hh in range(2):
                alpha = jnp.exp(m_prev[hh] - m_new[hh])
                acc_s[hh] = alpha * acc_prev[hh] + _dot(ps[hh], v_ref[:, hh * AUG:(hh + 1) * AUG], 1, 0)
                m_s[hh] = m_new[hh]

        @pl.when(ki < first_masked)
        def _():
            step(False)

        @pl.when((ki >= first_masked) & (ki <= last))
        def _():
            step(True)

        @pl.when(ki == last)
        def _():
            lane = lax.broadcasted_iota(jnp.int32, (1, LANES), 1)
            outs, lses = [], []
            for hh in range(2):
                acc = acc_s[hh]
                denom = jnp.sum(jnp.where(lane == HEAD_DIM, acc, 0.0), axis=1, keepdims=True)
                outs.append(acc / denom)
                lses.append(m_s[hh] + jnp.log(denom))
            o = jnp.where(lane < HEAD_DIM, outs[0], pltpu.roll(outs[1], HEAD_DIM, 1))
            o_ref[...] = o
            mix_ref[...] = o.astype(BF16)
            lse_ref[...] = jnp.where(lane < HEAD_DIM, lses[0], lses[1])

    def kmap(h, i, j):
        return (jnp.minimum(j, (i * tq + tq - 1) // tk), h)

    out = pl.BlockSpec((tq, LANES), lambda h, i, j: (i, h))
    return pl.pallas_call(
        body, name=name,
        out_shape=(jax.ShapeDtypeStruct((S, D_MIX), BF16), jax.ShapeDtypeStruct((S, D_ATT), F32),
                   jax.ShapeDtypeStruct((S, D_ATT), F32)),
        grid=(npair, nq, nk),
        in_specs=[pl.BlockSpec((tq, 2 * AUG), lambda h, i, j: (i, h)),
                  pl.BlockSpec((tk, 2 * AUG), kmap), pl.BlockSpec((tk, 2 * AUG), kmap)],
        out_specs=(out, out, out),
        scratch_shapes=[pltpu.VMEM((2, tq, LANES), F32), pltpu.VMEM((2, tq, LANES), F32)],
        compiler_params=_params('parallel', 'parallel', 'arbitrary'),
    )(qa, ka, va)


def _attn_bwd(qkv, f3, att, lse, d_mix, *, name, t=1024):
    S = qkv.shape[0]
    t = _tile(S, t)
    n = S // t
    npair = N_HEADS // 2

    def body(q_ref, k_ref, v_ref, f_ref, o_ref, lse_ref, do_ref, dq_ref, dk_ref, dv_ref, df_ref, dr_ref, dk_s, dv_s, df_s):
        ki, qi = pl.program_id(1), pl.program_id(2)

        @pl.when(qi == ki)
        def _():
            dk_s[...] = jnp.zeros(dk_s.shape, F32)
            dv_s[...] = jnp.zeros(dv_s.shape, F32)
            df_s[...] = jnp.zeros(df_s.shape, F32)

        def step(masked):
            q, k, v = q_ref[...], k_ref[...], v_ref[...]
            do, o, lse = do_ref[...], o_ref[...], lse_ref[...]
            lane = lax.broadcasted_iota(jnp.int32, (1, LANES), 1)
            lse_sw = pltpu.roll(lse, HEAD_DIM, 1)
            delta = _dot3(do.astype(BF16).astype(F32) * o, _pair_ones())
            delta_sw = pltpu.roll(delta, HEAD_DIM, 1)
            dq_blk = jnp.zeros((t, LANES), F32)
            dr_blk = jnp.zeros((t, LANES), F32)
            for hh in range(2):
                msk = _head_mask(hh)
                first = lane < HEAD_DIM if hh == 0 else lane >= HEAD_DIM
                qm = jnp.where(msk, q, jnp.zeros_like(q))
                km = jnp.where(msk, k, jnp.zeros_like(k))
                do_h = jnp.where(msk, do, 0.0)
                dom = do_h.astype(BF16)
                s = _dot(qm, k, 1, 1) - f_ref[0, hh:hh + 1, :]
                if masked:
                    s = _causal(s, qi, ki, t)
                lse_h = jnp.where(first, lse, lse_sw)
                delta_h = jnp.where(first, delta, delta_sw)
                p = jnp.exp(s - jnp.tile(lse_h, (1, t // LANES)))
                dp = _dot(dom, v, 1, 1)
                ds = p * (dp - jnp.tile(delta_h, (1, t // LANES)))
                dsb = ds.astype(BF16)
                dv_s[...] += _dot(jnp.transpose(do_h).astype(BF16), p.astype(BF16), 1, 0)
                dk_s[...] += _dot(jnp.transpose(qm.astype(F32)).astype(BF16), dsb, 1, 0)
                dq_blk = dq_blk + _dot(dsb, km, 1, 0)
                df_s[hh] -= jnp.sum(ds, axis=0, keepdims=True)
                dr_blk = dr_blk + jnp.where(msk, jnp.sum(ds, axis=1, keepdims=True), 0.0)
            rows = pl.ds(pl.multiple_of(qi * t, t), t)

            @pl.when(ki == 0)
            def _():
                dq_ref[rows, :] = dq_blk
                dr_ref[rows, :] = dr_blk

            @pl.when(ki > 0)
            def _():
                dq_ref[rows, :] += dq_blk
                dr_ref[rows, :] += dr_blk

        @pl.when(qi > ki)
        def _():
            step(False)

        @pl.when(qi == ki)
        def _():
            step(True)

        @pl.when(qi == n - 1)
        def _():
            dk_ref[...] = jnp.transpose(dk_s[...])
            dv_ref[...] = jnp.transpose(dv_s[...])
            df_ref[0, 0:1, :] = df_s[0]
            df_ref[0, 1:2, :] = df_s[1]

    qrow = lambda h, j, i: (jnp.maximum(i, j), h)
    return pl.pallas_call(
        body, name=name,
        out_shape=(jax.ShapeDtypeStruct((S, D_ATT), F32), jax.ShapeDtypeStruct((S, D_ATT), F32),
                   jax.ShapeDtypeStruct((S, D_ATT), F32), jax.ShapeDtypeStruct((npair, 2, S), F32),
                   jax.ShapeDtypeStruct((S, D_ATT), F32)),
        grid=(npair, n, n),
        in_specs=[pl.BlockSpec((t, LANES), qrow),
                  pl.BlockSpec((t, LANES), lambda h, j, i: (j, npair + h)),
                  pl.BlockSpec((t, LANES), lambda h, j, i: (j, 2 * npair + h)),
                  pl.BlockSpec((1, 2, t), lambda h, j, i: (h, 0, j)),
                  pl.BlockSpec((t, LANES), qrow),
                  pl.BlockSpec((t, LANES), qrow),
                  pl.BlockSpec((t, LANES), qrow)],
        out_specs=(pl.BlockSpec((S, LANES), lambda h, j, i: (0, h)),
                   pl.BlockSpec((t, LANES), lambda h, j, i: (j, h)),
                   pl.BlockSpec((t, LANES), lambda h, j, i: (j, h)),
                   pl.BlockSpec((1, 2, t), lambda h, j, i: (h, 0, j)),
                   pl.BlockSpec((S, LANES), lambda h, j, i: (0, h))),
        scratch_shapes=[pltpu.VMEM((LANES, t), F32), pltpu.VMEM((LANES, t), F32), pltpu.VMEM((2, 1, t), F32)],
        compiler_params=_params('parallel', 'arbitrary', 'arbitrary'),
    )(qkv, qkv, qkv, f3, att, lse, d_mix)


A_COL = D_QKV // D_CONV
B_COL = A_COL + 1
P_COL = B_COL + 1


CONV_BLOCKS = D_CONV // LANES
CONV_GROUP = 8 * 8


def _rows8(ref, c, row):
    return ref.at[c][pl.ds(row, 8, stride=8), :]


def _put8(ref, c, row, val):
    ref.at[c][pl.ds(row, 8, stride=8), :] = val


def _lanes(c):
    return slice(c * LANES, (c + 1) * LANES)


def _glu_into(buf, a_ref, b_ref, ah_ref, bh_ref, first, ts):
    for c in range(CONV_BLOCKS):
        halo = ah_ref[:, _lanes(c)] * _sigmoid(bh_ref[:, _lanes(c)])
        buf[c, 0:CONV_HALO, :] = jnp.where(first, 0.0, halo)
        buf[c, CONV_HALO:CONV_HALO + ts, :] = a_ref[:, _lanes(c)] * _sigmoid(b_ref[:, _lanes(c)])


def _conv_taps(buf, c, r0):
    return [_rows8(buf, c, CONV_HALO + r0 + i - (CONV_WIDTH - 1)) for i in range(CONV_WIDTH - 1 + 8)]


def _dwconv8(xs, ws, bias):
    outs = []
    for j in range(8):
        acc = ws[0] * xs[j]
        for k in range(1, CONV_WIDTH):
            acc = acc + ws[k] * xs[j + k]
        outs.append(acc + bias)
    return outs


def _ln8(cs):
    inv = 1.0 / D_CONV
    mu = sum(jnp.sum(c, axis=1, keepdims=True) for c in cs) * inv
    xc = [c - mu for c in cs]
    rstd = lax.rsqrt(sum(jnp.sum(x * x, axis=1, keepdims=True) for x in xc) * inv + EPS)
    return [x * rstd for x in xc], rstd


def _conv_specs(ts, tmap):
    hb = ts // CONV_HALO
    cur = lambda c: pl.BlockSpec((ts, D_CONV), lambda i: (tmap(i), c))
    halo = lambda c: pl.BlockSpec((CONV_HALO, D_CONV), lambda i: (jnp.maximum(tmap(i) * hb - 1, 0), c))
    return cur, halo


def _conv_fwd(proj_a, mix, dw_w, dw_b, ln_g, ln_b, pw_w, l, *, name, ts=512):
    S = proj_a.shape[0]
    ts = _tile(S, ts, CONV_GROUP)

    def body(a_ref, b_ref, ah_ref, bh_ref, w_ref, wb_ref, g_ref, bb_ref, pw_ref, mix_in, o_ref, buf, stage):
        _glu_into(buf, a_ref, b_ref, ah_ref, bh_ref, pl.program_id(0) == 0, ts)
        ws = [[w_ref[k:k + 1, _lanes(c)] for k in range(CONV_WIDTH)] for c in range(CONV_BLOCKS)]
        for r0 in range(0, ts, CONV_GROUP):
            conv = [_dwconv8(_conv_taps(buf, c, r0), ws[c], wb_ref[:, _lanes(c)]) for c in range(CONV_BLOCKS)]
            for j in range(8):
                yhat, _ = _ln8([conv[c][j] for c in range(CONV_BLOCKS)])
                for c in range(CONV_BLOCKS):
                    y = yhat[c] * g_ref[:, _lanes(c)] + bb_ref[:, _lanes(c)]
                    _put8(stage, c, r0 + j, y * _sigmoid(y))
        hs = jnp.concatenate([stage[c] for c in range(CONV_BLOCKS)], axis=1)
        o_ref[...] = _dot(hs.astype(BF16), pw_ref[...], 1, 0).astype(BF16)

    cur, halo = _conv_specs(ts, lambda i: i)
    vec = pl.BlockSpec((1, D_CONV), lambda i: (0, 0))
    return pl.pallas_call(
        body, name=name, out_shape=jax.ShapeDtypeStruct(mix.shape, BF16), grid=(S // ts,),
        in_specs=[cur(A_COL), cur(B_COL), halo(A_COL), halo(B_COL),
                  pl.BlockSpec((None, CONV_HALO, D_CONV), lambda i: (l, 0, 0)), vec, vec, vec,
                  pl.BlockSpec((None, D_CONV, D_CONV), lambda i: (l, 0, 0)), ANY_SPEC],
        out_specs=pl.BlockSpec((ts, D_CONV), lambda i: (i, D_ATT // D_CONV)),
        scratch_shapes=[pltpu.VMEM((CONV_BLOCKS, CONV_HALO + ts, LANES), F32), pltpu.VMEM((CONV_BLOCKS, ts, LANES), F32)],
        input_output_aliases={9: 0},
        compiler_params=_params('parallel'),
    )(proj_a, proj_a, proj_a, proj_a, dw_w, dw_b, ln_g, ln_b, pw_w, mix)


def _conv_bwd(proj_a, d_mix, d_proj, dw_w, dw_b, ln_g, ln_b, pw_w, l, *, name, ts=512):
    S = proj_a.shape[0]
    ts = _tile(S, ts, CONV_GROUP)
    n = S // ts
    d_col = D_ATT // D_CONV
    groups = range(0, ts, CONV_GROUP)

    def body(a_ref, b_ref, ah_ref, bh_ref, dy_ref, w_ref, wb_ref, g_ref, bb_ref, pw_ref, dp_in,
             o_ref, dw_ref, dwb_ref, dg_ref, dbb_ref, dpw_ref, buf, dcbuf, stage, stage2):
        i = pl.program_id(0)
        _glu_into(buf, a_ref, b_ref, ah_ref, bh_ref, i == n - 1, ts)

        @pl.when(i == 0)
        def _():
            dcbuf[:, ts:ts + CONV_HALO, :] = jnp.zeros((CONV_BLOCKS, CONV_HALO, LANES), F32)
            dw_ref[...] = jnp.zeros(dw_ref.shape, F32)
            dwb_ref[...] = jnp.zeros(dwb_ref.shape, F32)
            dg_ref[...] = jnp.zeros(dg_ref.shape, F32)
            dbb_ref[...] = jnp.zeros(dbb_ref.shape, F32)
            dpw_ref[...] = jnp.zeros(dpw_ref.shape, F32)

        dout = dy_ref[...].astype(BF16)
        d_hs = _dot(dout, pw_ref[...], 1, 1)
        for c in range(CONV_BLOCKS):
            stage2[c, :, :] = d_hs[:, _lanes(c)]
        ws = [[w_ref[k:k + 1, _lanes(c)] for k in range(CONV_WIDTH)] for c in range(CONV_BLOCKS)]
        zero8 = jnp.zeros((8, LANES), F32)
        dg = [zero8] * CONV_BLOCKS
        dbb = [zero8] * CONV_BLOCKS
        dwb = [zero8] * CONV_BLOCKS
        for r0 in groups:
            conv = [_dwconv8(_conv_taps(buf, c, r0), ws[c], wb_ref[:, _lanes(c)]) for c in range(CONV_BLOCKS)]
            for j in range(8):
                yhat, rstd = _ln8([conv[c][j] for c in range(CONV_BLOCKS)])
                d_yhat = []
                for c in range(CONV_BLOCKS):
                    y = yhat[c] * g_ref[:, _lanes(c)] + bb_ref[:, _lanes(c)]
                    sg = _sigmoid(y)
                    _put8(stage, c, r0 + j, y * sg)
                    d_y = _rows8(stage2, c, r0 + j) * (sg * (1.0 + y * (1.0 - sg)))
                    dg[c] = dg[c] + d_y * yhat[c]
                    dbb[c] = dbb[c] + d_y
                    d_yhat.append(d_y * g_ref[:, _lanes(c)])
                inv = 1.0 / D_CONV
                m1 = sum(jnp.sum(d, axis=1, keepdims=True) for d in d_yhat) * inv
                m2 = sum(jnp.sum(d * yh, axis=1, keepdims=True) for d, yh in zip(d_yhat, yhat)) * inv
                for c in range(CONV_BLOCKS):
                    d_c = rstd * (d_yhat[c] - m1 - yhat[c] * m2)
                    dwb[c] = dwb[c] + d_c
                    _put8(dcbuf, c, r0 + j, d_c)
        for c in range(CONV_BLOCKS):
            dg_ref[:, _lanes(c)] += jnp.sum(dg[c], axis=0, keepdims=True)
            dbb_ref[:, _lanes(c)] += jnp.sum(dbb[c], axis=0, keepdims=True)
            dwb_ref[:, _lanes(c)] += jnp.sum(dwb[c], axis=0, keepdims=True)
        hs = jnp.concatenate([stage[c] for c in range(CONV_BLOCKS)], axis=1)
        dpw_ref[...] += _dot(hs.astype(BF16), dout, 0, 0)
        for c in range(CONV_BLOCKS):
            for r0 in groups:
                dcs = [_rows8(dcbuf, c, r0 + i_) for i_ in range(CONV_WIDTH - 1 + 8)]
                for j in range(8):
                    acc = ws[c][0] * dcs[j + CONV_WIDTH - 1]
                    for k in range(1, CONV_WIDTH):
                        acc = acc + ws[c][k] * dcs[j + CONV_WIDTH - 1 - k]
                    _put8(stage2, c, r0 + j, acc)
            for k in range(CONV_WIDTH):
                acc = zero8
                for r0 in groups:
                    for j in range(8):
                        acc = acc + _rows8(dcbuf, c, r0 + j) * _rows8(buf, c, CONV_HALO + r0 + j - (CONV_WIDTH - 1) + k)
                dw_ref[k:k + 1, _lanes(c)] += jnp.sum(acc, axis=0, keepdims=True)
            dcbuf[c, ts:ts + CONV_HALO, :] = dcbuf[c, 0:CONV_HALO, :]
        d_h = jnp.concatenate([stage2[c] for c in range(CONV_BLOCKS)], axis=1)
        a, sb = a_ref[...], _sigmoid(b_ref[...])
        o_ref[:, 0:D_CONV] = (d_h * sb).astype(BF16)
        o_ref[:, D_CONV:2 * D_CONV] = (d_h * a * sb * (1.0 - sb)).astype(BF16)

    rev = lambda i: n - 1 - i
    cur, halo = _conv_specs(ts, rev)
    vec = pl.BlockSpec((1, D_CONV), lambda i: (0, 0))
    wspec = pl.BlockSpec((CONV_HALO, D_CONV), lambda i: (0, 0))
    sq = pl.BlockSpec((D_CONV, D_CONV), lambda i: (0, 0))
    tile3 = pltpu.VMEM((CONV_BLOCKS, ts, LANES), F32)
    return pl.pallas_call(
        body, name=name,
        out_shape=(jax.ShapeDtypeStruct(d_proj.shape, BF16), jax.ShapeDtypeStruct((CONV_HALO, D_CONV), F32),
                   jax.ShapeDtypeStruct((1, D_CONV), F32), jax.ShapeDtypeStruct((1, D_CONV), F32),
                   jax.ShapeDtypeStruct((1, D_CONV), F32), jax.ShapeDtypeStruct((D_CONV, D_CONV), F32)),
        grid=(n,),
        in_specs=[cur(A_COL), cur(B_COL), halo(A_COL), halo(B_COL),
                  pl.BlockSpec((ts, D_CONV), lambda i: (rev(i), d_col)),
                  pl.BlockSpec((None, CONV_HALO, D_CONV), lambda i: (l, 0, 0)), vec, vec, vec,
                  pl.BlockSpec((None, D_CONV, D_CONV), lambda i: (l, 0, 0)), ANY_SPEC],
        out_specs=(pl.BlockSpec((ts, 2 * D_CONV), lambda i: (rev(i), D_QKV // (2 * D_CONV))), wspec, vec, vec, vec, sq),
        scratch_shapes=[pltpu.VMEM((CONV_BLOCKS, CONV_HALO + ts, LANES), F32),
                        pltpu.VMEM((CONV_BLOCKS, ts + CONV_HALO, LANES), F32), tile3, tile3],
        input_output_aliases={10: 0},
        compiler_params=_params('arbitrary'),
    )(proj_a, proj_a, proj_a, proj_a, d_mix, dw_w, dw_b, ln_g, ln_b, pw_w, d_proj)


POOL_BLOCKS = D_POOL // LANES
POOL_SPAN = max(POOL_WINDOWS) - 1


def _pool_sum8(xs, j, c, step):
    lo, hi = POOL_WINDOWS[2 * c], POOL_WINDOWS[2 * c + 1]
    acc = xs[j]
    for d in range(1, lo):
        acc = acc + xs[j + step * d]
    more = xs[j + step * lo]
    for d in range(lo + 1, hi):
        more = more + xs[j + step * d]
    lane = lax.broadcasted_iota(jnp.int32, (1, LANES), 1)
    return acc + jnp.where(lane >= POOL_GROUP, more, 0.0)


def _pool_count8(c, row):
    lane = lax.broadcasted_iota(jnp.int32, (1, LANES), 1)
    wl = jnp.where(lane >= POOL_GROUP, POOL_WINDOWS[2 * c + 1], POOL_WINDOWS[2 * c])
    pos = row + 8 * lax.broadcasted_iota(jnp.int32, (8, 1), 0)
    return jnp.minimum(pos + 1, wl).astype(F32)


def _pool_diff_into(stage, buf, u_ref, uh_ref, first, tile, ts):
    for c in range(POOL_BLOCKS):
        buf[c, 0:POOL_HALO, :] = jnp.where(first, 0.0, uh_ref[:, _lanes(c)])
        buf[c, POOL_HALO:POOL_HALO + ts, :] = u_ref[:, _lanes(c)]
        for r0 in range(0, ts, CONV_GROUP):
            xs = [_rows8(buf, c, POOL_HALO + r0 + i - POOL_SPAN) for i in range(POOL_SPAN + 8)]
            for j in range(8):
                mean = _pool_sum8(xs, j + POOL_SPAN, c, -1) / _pool_count8(c, tile * ts + r0 + j)
                _put8(stage, c, r0 + j, mean - xs[j + POOL_SPAN])


def _pool_specs(ts, tmap):
    hb = ts // POOL_HALO
    cur = pl.BlockSpec((ts, D_POOL), lambda i: (tmap(i), P_COL))
    halo = pl.BlockSpec((POOL_HALO, D_POOL), lambda i: (jnp.maximum(tmap(i) * hb - 1, 0), P_COL))
    return cur, halo


def _pool_fwd(proj_a, mix, wbd, scale, *, name, ts=512):
    S = proj_a.shape[0]
    ts = _tile(S, ts, CONV_GROUP)

    def body(u_ref, uh_ref, w_ref, s_ref, mix_in, o_ref, buf, stage):
        i = pl.program_id(0)
        _pool_diff_into(stage, buf, u_ref, uh_ref, i == 0, i, ts)
        d = jnp.concatenate([stage[c] for c in range(POOL_BLOCKS)], axis=1)
        o_ref[...] = (_dot(d.astype(BF16), w_ref[...], 1, 0) * s_ref[...]).astype(BF16)

    cur, halo = _pool_specs(ts, lambda i: i)
    return pl.pallas_call(
        body, name=name, out_shape=jax.ShapeDtypeStruct(mix.shape, BF16), grid=(S // ts,),
        in_specs=[cur, halo, pl.BlockSpec((D_POOL, D_POOL), lambda i: (0, 0)), pl.BlockSpec((1, D_POOL), lambda i: (0, 0)),
                  ANY_SPEC],
        out_specs=pl.BlockSpec((ts, D_POOL), lambda i: (i, (D_ATT + D_CONV) // D_POOL)),
        scratch_shapes=[pltpu.VMEM((POOL_BLOCKS, POOL_HALO + ts, LANES), F32), pltpu.VMEM((POOL_BLOCKS, ts, LANES), F32)],
        input_output_aliases={4: 0},
        compiler_params=_params('parallel'),
    )(proj_a, proj_a, wbd, scale, mix)


def _pool_bwd(proj_a, d_mix, d_proj, wbd, scale, *, name, ts=512):
    S = proj_a.shape[0]
    ts = _tile(S, ts, CONV_GROUP)
    n = S // ts
    d_col = (D_ATT + D_CONV) // D_POOL

    def body(u_ref, uh_ref, dy_ref, w_ref, s_ref, dp_in, o_ref, dw_ref, ds_ref, buf, ebuf, stage):
        i = pl.program_id(0)
        tile = n - 1 - i
        _pool_diff_into(stage, buf, u_ref, uh_ref, tile == 0, tile, ts)
        db = jnp.concatenate([stage[c] for c in range(POOL_BLOCKS)], axis=1).astype(BF16)
        ypre = _dot(db, w_ref[...], 1, 0)
        dout = dy_ref[...]
        d_y = (dout * s_ref[...]).astype(BF16)
        d_d = _dot(d_y, w_ref[...], 1, 1)

        @pl.when(i == 0)
        def _():
            ebuf[:, ts:ts + POOL_HALO, :] = jnp.zeros((POOL_BLOCKS, POOL_HALO, LANES), F32)
            dw_ref[...] = jnp.zeros(dw_ref.shape, F32)
            ds_ref[...] = jnp.zeros(ds_ref.shape, F32)

        dw_ref[...] += _dot(db, d_y, 0, 0)
        ds_ref[...] += jnp.sum(dout * ypre, axis=0, keepdims=True)
        for c in range(POOL_BLOCKS):
            stage[c, :, :] = d_d[:, _lanes(c)]
            for r0 in range(0, ts, CONV_GROUP):
                for j in range(8):
                    _put8(ebuf, c, r0 + j, _rows8(stage, c, r0 + j) / _pool_count8(c, tile * ts + r0 + j))
            for r0 in range(0, ts, CONV_GROUP):
                es = [_rows8(ebuf, c, r0 + i_) for i_ in range(POOL_SPAN + 8)]
                for j in range(8):
                    _put8(stage, c, r0 + j, _pool_sum8(es, j, c, 1) - _rows8(stage, c, r0 + j))
            ebuf[c, ts:ts + POOL_HALO, :] = ebuf[c, 0:POOL_HALO, :]
        o_ref[...] = jnp.concatenate([stage[c] for c in range(POOL_BLOCKS)], axis=1).astype(BF16)

    rev = lambda i: n - 1 - i
    cur, halo = _pool_specs(ts, rev)
    sq = pl.BlockSpec((D_POOL, D_POOL), lambda i: (0, 0))
    vec = pl.BlockSpec((1, D_POOL), lambda i: (0, 0))
    return pl.pallas_call(
        body, name=name,
        out_shape=(jax.ShapeDtypeStruct(d_proj.shape, BF16), jax.ShapeDtypeStruct((D_POOL, D_POOL), F32),
                   jax.ShapeDtypeStruct((1, D_POOL), F32)),
        grid=(n,),
        in_specs=[cur, halo, pl.BlockSpec((ts, D_POOL), lambda i: (rev(i), d_col)), sq, vec, ANY_SPEC],
        out_specs=(pl.BlockSpec((ts, D_POOL), lambda i: (rev(i), P_COL)), sq, vec),
        scratch_shapes=[pltpu.VMEM((POOL_BLOCKS, POOL_HALO + ts, LANES), F32),
                        pltpu.VMEM((POOL_BLOCKS, ts + POOL_HALO, LANES), F32), pltpu.VMEM((POOL_BLOCKS, ts, LANES), F32)],
        input_output_aliases={5: 0},
        compiler_params=_params('arbitrary'),
    )(proj_a, proj_a, d_mix, wbd, scale, d_proj)


FFN_LANES = 128
FFN_GROUP = 8 * 8


def _ffn_rows(ref, c, row0, j):
    return ref.at[c][pl.ds(row0 + j, 8, stride=8), :]


def _ffn_specs(ts, tc2, tmap, l):
    hb = ts // FFN_HALO
    cur = pl.BlockSpec((ts, tc2), lambda c, i: (tmap(i), c))
    halo = pl.BlockSpec((FFN_HALO, tc2), lambda c, i: (jnp.maximum(tmap(i) * hb - 1, 0), c))
    wspec = pl.BlockSpec((None, FFN_HALO, tc2), lambda c, i: (l, 0, c))
    return cur, halo, wspec


def _ffn_fill(buf, x_ref, xh_ref, first, ts, nblk):
    for c in range(nblk):
        cs = slice(c * FFN_LANES, (c + 1) * FFN_LANES)
        buf[c, 0:FFN_HALO, :] = jnp.where(first, 0.0, xh_ref[:, cs])
        buf[c, FFN_HALO:FFN_HALO + ts, :] = x_ref[:, cs]


def _ffn_conv_piece(buf, w_ref, r0, c):
    ws = [w_ref[k:k + 1, c * FFN_LANES:(c + 1) * FFN_LANES] for k in range(FFN_CONV_WIDTH)]
    xs = [_ffn_rows(buf, c, FFN_HALO + r0, j) for j in range(1 - FFN_CONV_WIDTH, 8)]
    outs = []
    for j in range(8):
        acc = ws[0] * xs[j]
        for k in range(1, FFN_CONV_WIDTH):
            acc = acc + ws[k] * xs[j + k]
        outs.append(acc)
    return outs, xs


def _ffn_act_fwd(up, w, l, *, name, ts=256):
    S, F2 = up.shape
    tc = F2 // 4
    nb = tc // FFN_LANES
    ts = _tile(S, ts, FFN_GROUP)

    def body(x_ref, xh_ref, w_ref, o_ref, buf, stage):
        _ffn_fill(buf, x_ref, xh_ref, pl.program_id(1) == 0, ts, 2 * nb)
        for c in range(nb):
            for r0 in range(0, ts, FFN_GROUP):
                gates, _ = _ffn_conv_piece(buf, w_ref, r0, c)
                vals, _ = _ffn_conv_piece(buf, w_ref, r0, nb + c)
                for j in range(8):
                    stage.at[c][pl.ds(r0 + j, 8, stride=8), :] = gates[j] * _sigmoid(gates[j]) * vals[j]
            o_ref[:, c * FFN_LANES:(c + 1) * FFN_LANES] = stage[c].astype(BF16)

    cur, halo, wspec = _ffn_specs(ts, 2 * tc, lambda i: i, l)
    return pl.pallas_call(
        body, name=name, out_shape=jax.ShapeDtypeStruct((S, F2 // 2), BF16), grid=(2, S // ts),
        in_specs=[cur, halo, wspec],
        out_specs=pl.BlockSpec((ts, tc), lambda c, i: (i, c)),
        scratch_shapes=[pltpu.VMEM((2 * nb, FFN_HALO + ts, FFN_LANES), F32), pltpu.VMEM((nb, ts, FFN_LANES), F32)],
        compiler_params=_params('parallel', 'parallel'),
    )(up, up, w)


def _ffn_act_bwd(up, d_act, w, l, *, name, ts=256):
    S, F2 = up.shape
    tc = F2 // 4
    nb = tc // FFN_LANES
    ts = _tile(S, ts, FFN_GROUP)
    n = S // ts

    def body(x_ref, xh_ref, da_ref, w_ref, o_ref, dw_ref, buf, dcbuf, stage):
        i = pl.program_id(1)
        _ffn_fill(buf, x_ref, xh_ref, i == n - 1, ts, 2 * nb)

        @pl.when(i == 0)
        def _():
            dcbuf[:, ts:ts + FFN_HALO, :] = jnp.zeros((2 * nb, FFN_HALO, FFN_LANES), F32)
            dw_ref[...] = jnp.zeros(dw_ref.shape, F32)

        for c in range(nb):
            blocks = (c, nb + c)
            stage[c, :, :] = da_ref[:, c * FFN_LANES:(c + 1) * FFN_LANES]
            dws = [[jnp.zeros((8, FFN_LANES), F32) for _ in range(FFN_CONV_WIDTH)] for _ in range(2)]
            for r0 in range(0, ts, FFN_GROUP):
                gates, xg = _ffn_conv_piece(buf, w_ref, r0, blocks[0])
                vals, xv = _ffn_conv_piece(buf, w_ref, r0, blocks[1])
                for j in range(8):
                    sg = _sigmoid(gates[j])
                    da = _ffn_rows(stage, c, r0, j)
                    d_cs = (da * vals[j] * (sg * (1.0 + gates[j] * (1.0 - sg))), da * (gates[j] * sg))
                    for half, (d_c, xs) in enumerate(zip(d_cs, (xg, xv))):
                        dcbuf.at[blocks[half]][pl.ds(r0 + j, 8, stride=8), :] = d_c
                        for k in range(FFN_CONV_WIDTH):
                            dws[half][k] = dws[half][k] + d_c * xs[j + k]
            for half in range(2):
                cs = slice(blocks[half] * FFN_LANES, (blocks[half] + 1) * FFN_LANES)
                for k in range(FFN_CONV_WIDTH):
                    dw_ref[k:k + 1, cs] += jnp.sum(dws[half][k], axis=0, keepdims=True)
            for b in blocks:
                cs = slice(b * FFN_LANES, (b + 1) * FFN_LANES)
                ws = [w_ref[k:k + 1, cs] for k in range(FFN_CONV_WIDTH)]
                for r0 in range(0, ts, FFN_GROUP):
                    ds = [_ffn_rows(dcbuf, b, r0, j) for j in range(8 + FFN_CONV_WIDTH - 1)]
                    for j in range(8):
                        d_x = ws[FFN_CONV_WIDTH - 1] * ds[j]
                        for k in range(FFN_CONV_WIDTH - 1):
                            d_x = d_x + ws[k] * ds[j + FFN_CONV_WIDTH - 1 - k]
                        stage.at[c][pl.ds(r0 + j, 8, stride=8), :] = d_x
                o_ref[:, cs] = stage[c].astype(BF16)
                dcbuf[b, ts:ts + FFN_HALO, :] = dcbuf[b, 0:FFN_HALO, :]

    rev = lambda i: n - 1 - i
    cur, halo, wspec = _ffn_specs(ts, 2 * tc, rev, l)
    return pl.pallas_call(
        body, name=name,
        out_shape=(jax.ShapeDtypeStruct((S, F2), BF16), jax.ShapeDtypeStruct((FFN_HALO, F2), F32)),
        grid=(2, n),
        in_specs=[cur, halo, pl.BlockSpec((ts, tc), lambda c, i: (rev(i), c)), wspec],
        out_specs=(cur, pl.BlockSpec((FFN_HALO, 2 * tc), lambda c, i: (0, c))),
        scratch_shapes=[pltpu.VMEM((2 * nb, FFN_HALO + ts, FFN_LANES), F32), pltpu.VMEM((2 * nb, ts + FFN_HALO, FFN_LANES), F32),
                        pltpu.VMEM((nb, ts, FFN_LANES), F32)],
        compiler_params=_params('parallel', 'arbitrary'),
    )(up, up, d_act, w)


def _loss_head(y, target, *, name, ts=512):
    S, D = y.shape
    ts = _tile(S, ts, 8)

    def body(y_ref, t_ref, l_ref, dy_ref):
        i = pl.program_id(0)
        err = y_ref[...] - t_ref[...]
        dy_ref[...] = err * (1.0 / D)
        part = jnp.sum(jnp.sum(err * err, axis=1, keepdims=True), axis=0, keepdims=True) * (0.5 / D)

        @pl.when(i == 0)
        def _():
            l_ref[...] = part

        @pl.when(i > 0)
        def _():
            l_ref[...] += part

    row = pl.BlockSpec((ts, D), lambda i: (i, 0))
    return pl.pallas_call(
        body, name=name,
        out_shape=(jax.ShapeDtypeStruct((1, 1), F32), jax.ShapeDtypeStruct((S, D), F32)),
        grid=(S // ts,), in_specs=[row, row], out_specs=(pl.BlockSpec((1, 1), lambda i: (0, 0)), row),
        compiler_params=_params('arbitrary'),
    )(y, target)


def _pair_cols(w):
    lead, f2 = w.shape[:-1], w.shape[-1]
    return w.reshape(lead + (2, 2, f2 // 4)).swapaxes(-3, -2).reshape(lead + (f2,))


def _pad_axis(w, size, axis):
    pad = [(0, 0)] * w.ndim
    pad[axis] = (0, size - w.shape[axis])
    return jnp.pad(w, pad)


def _block_diag(pool_w):
    g = pool_w.shape[0]
    rows = [jnp.concatenate([pool_w[i] if i == j else jnp.zeros_like(pool_w[i]) for j in range(g)], axis=1) for i in range(g)]
    return jnp.concatenate(rows, axis=0)


def _small_weights(w, l):
    return dict(
        norm1_g=w['norm1_g'][l][None, :],
        b_col=_pad_axis(w['b_f'][l][:, None], FG_ROWS, 0),
        qg=jnp.tile(w['q_norm_g'][l], N_HEADS)[None, :],
        kg=jnp.tile(w['k_norm_g'][l], N_HEADS)[None, :],
        dw_b=w['conv_dw_b'][l][None, :], ln_g=w['conv_ln_g'][l][None, :], ln_b=w['conv_ln_b'][l][None, :],
        wbd=_block_diag(w['pool_w'][l]).astype(BF16),
        pool_scale=w['pool_scale'][l][None, :],
        norm2_g=w['norm2_g'][l][None, :],
    )


def _layer_fwd(x, W, p, l):
    n = lambda s: f'l{l}_{s}'
    S = x.shape[0]
    h = _rms_fwd(x, p['norm1_g'], name=n('norm1'))
    proj_a = _mm(h, W.get('w_a', h), b_lead=0, name=n('proj_a'), tn=D_PROJ_A)
    z_raw = _mm(W.get('w_fg_t', h), h, a_lead=0, tb=True, name=n('proj_fg'))
    qkv = _qk_prep_fwd(proj_a, p['qg'], p['kg'], name=n('qk_norm'))
    f_cum = _forget_fwd(z_raw, p['b_col'], name=n('forget'))
    f3 = f_cum[:N_HEADS].reshape(N_HEADS // 2, 2, S)
    mix, att, lse = _attn_fwd(*_attn_aug(qkv, f_cum, name=n('attn_aug')), name=n('attn'))
    mix = _conv_fwd(proj_a, mix, W.get('dw_w', h), p['dw_b'], p['ln_g'], p['ln_b'], W.get('pw_w', h), 0, name=n('conv'))
    mix = _pool_fwd(proj_a, mix, p['wbd'], p['pool_scale'], name=n('pool'))
    x1 = _mm(mix, W.get('w_out', mix), b_lead=0, res=x, name=n('out_proj'), tn=1024)
    h2 = _rms_fwd(x1, p['norm2_g'], name=n('norm2'))
    up = _mm(h2, W.get('w_up', mix), b_lead=0, name=n('up_proj'), tn=1408, cols_outer=True)
    act = _ffn_act_fwd(up, W.get('ffn_w', h), 0, name=n('ffn_act'))
    x2 = _mm(act, W.get('w_down', mix), b_lead=0, res=x1, name=n('down_proj'), tn=1024, tk=2816)
    saved = dict(x=x, h=h, proj_a=proj_a, z_raw=z_raw, qkv=qkv, f3=f3, att=att, lse=lse, mix=mix, x1=x1, h2=h2, up=up, act=act)
    return x2, saved


def _layer_bwd(dx2, W, p, s, l, sink):
    n = lambda t: f'l{l}_{t}'
    S = dx2.shape[0]
    g = {}
    W = W.ready

    def large(key, a, b, **kw):
        return sink.put(l, key, *_mm(a, b, ta=True, copy16=True, name=n('d_' + key), **kw))

    d_act = _mm(dx2, W['w_down'], b_lead=0, tb=True, name=n('d_act'), tn=1408, cols_outer=True)
    large('w_down', s['act'], dx2, tm=1408, tn=1024)
    d_up, d_ffn_w = _ffn_act_bwd(s['up'], d_act, W['ffn_w'], 0, name=n('ffn_act_bwd'))
    g['ffn_dw_w'] = _pair_cols(d_ffn_w[:FFN_CONV_WIDTH])
    d_h2 = _mm(d_up, W['w_up'], b_lead=0, tb=True, name=n('d_h2'), tn=1024, tk=5632)
    started = large('w_up', s['h2'], d_up, tm=1024, tn=512, tk=4096)
    dx1, dg2 = _rms_bwd(s['x1'], p['norm2_g'], d_h2, dx2, name=n('norm2_bwd'))
    g['norm2_g'] = dg2[0]
    sink.point(l, 'mid', dx1)
    d_mix = _mm(dx1, W['w_out'], b_lead=0, tb=True, name=n('d_mix'), tn=1024, after=started)
    large('w_out', s['mix'], dx1, tm=1024, tn=1024)
    dq, dk, dv, df3, dr = _attn_bwd(s['qkv'], s['f3'], s['att'], s['lse'], d_mix, name=n('attn_bwd'))
    df = _pad_axis(df3.reshape(N_HEADS, S) + dr[:, ::HEAD_DIM].T, FG_ROWS, 0)
    d_z, d_b = _forget_bwd(s['z_raw'], p['b_col'], df, name=n('forget_bwd'))
    g['b_f'] = d_b[:N_HEADS, 0]
    d_proj, d_qg, d_kg = _qk_prep_bwd(s['proj_a'], dq, dk, dv, p['qg'], p['kg'], name=n('qk_norm_bwd'))
    g['q_norm_g'] = d_qg.reshape(N_HEADS, HEAD_DIM).sum(axis=0)
    g['k_norm_g'] = d_kg.reshape(N_HEADS, HEAD_DIM).sum(axis=0)
    d_proj, d_dw_w, d_dw_b, d_ln_g, d_ln_b, d_pw = _conv_bwd(
        s['proj_a'], d_mix, d_proj, W['dw_w'], p['dw_b'], p['ln_g'], p['ln_b'], W['pw_w'], 0, name=n('conv_bwd'))
    g['conv_dw_w'], g['conv_dw_b'] = d_dw_w[:CONV_WIDTH], d_dw_b[0]
    g['conv_ln_g'], g['conv_ln_b'], g['conv_pw_w'] = d_ln_g[0], d_ln_b[0], d_pw
    d_proj, d_wbd, d_scale = _pool_bwd(s['proj_a'], d_mix, d_proj, p['wbd'], p['pool_scale'], name=n('pool_bwd'))
    g['pool_w'] = jnp.stack([d_wbd[i * POOL_GROUP:(i + 1) * POOL_GROUP, i * POOL_GROUP:(i + 1) * POOL_GROUP]
                             for i in range(len(POOL_WINDOWS))])
    g['pool_scale'] = d_scale[0]
    d_w_a = _mm(s['h'], d_proj, ta=True, name=n('d_w_a'), tm=1024, tn=768, tk=4096)
    started = sink.put(l, 'w_in', d_w_a, _mm(d_z, s['h'], name=n('d_w_fg'), tn=1024).T)
    d_h_fg = _mm(d_z, W['w_fg_t'], b_lead=0, ta=True, name=n('d_h_fg'), tn=1024, after=started)
    d_h = _mm(d_proj, W['w_a'], b_lead=0, tb=True, res=d_h_fg, name=n('d_h'), tn=1024, tk=D_PROJ_A)
    dx, dg1 = _rms_bwd(s['x'], p['norm1_g'], d_h, dx1, name=n('norm1_bwd'))
    g['norm1_g'] = dg1[0]
    sink.point(l, 'end', dx)
    return dx, g


SMALL_GRADS = REPLICATED + ('conv_dw_w', 'conv_pw_w', 'ffn_dw_w')


def _local_step(x, target, W, w_small, sink):
    depth = w_small['norm1_g'].shape[0]
    ps, saved = [], []
    for l in range(depth):
        p = _small_weights(w_small, l)
        x, s = _layer_fwd(x, W[l], p, l)
        ps.append(p)
        saved.append(s)
    loss, dx = _loss_head(x, target, name='loss_head')
    small = [None] * depth
    for l in reversed(range(depth)):
        dx, small[l] = _layer_bwd(dx, W[l], ps[l], saved[l], l, sink)
    return loss, dx, {k: jnp.stack([small[l][k] for l in range(depth)]) for k in SMALL_GRADS}


W_IN_SHARD = D_IN // N_CHIPS
W_IN_PAD = 640
N_A_TILES = D_PROJ_A // LANES
FG_COL0 = D_QKV


def _a_tile_base(j):
    if j == N_A_TILES:
        return FG_COL0, N_HEADS
    return (j * LANES if j * LANES < FG_COL0 else j * LANES + N_HEADS), LANES


def _shift_select(rows, cols, shift, row_max, col_max):
    r = lax.broadcasted_iota(jnp.int32, (rows, cols), 0)
    c = lax.broadcasted_iota(jnp.int32, (rows, cols), 1)
    return ((r + shift == c) & (r < row_max) & (c < col_max)).astype(BF16)


def _select_w_in(raw, *, name, tm=256):
    _, D, _ = raw.shape
    tm = _tile(D, tm, 16)
    plan = []
    for j in range(N_A_TILES + 1):
        base, cmax = _a_tile_base(j)
        parts = []
        for p in range(N_CHIPS):
            delta = base - W_IN_SHARD * p
            lo, hi = max(0, delta), min(W_IN_SHARD - 1, delta + cmax - 1)
            if lo > hi:
                continue
            a0 = (lo // LANES) * LANES
            kw = min(-(-(hi + 1 - a0) // LANES) * LANES, W_IN_PAD - a0)
            parts.append((p, a0, kw, delta))
        plan.append((cmax, parts))

    def body(raw_ref, wa_ref, fg_ref):
        for j, (cmax, parts) in enumerate(plan):
            acc = None
            for p, a0, kw, delta in parts:
                sel = _shift_select(kw, LANES, a0 - delta, W_IN_SHARD - a0, cmax)
                t = _dot(raw_ref[p, :, a0:a0 + kw], sel, 1, 0)
                acc = t if acc is None else acc + t
            if j == N_A_TILES:
                fg_ref[...] = acc.astype(BF16)
            else:
                wa_ref[:, j * LANES:(j + 1) * LANES] = acc.astype(BF16)

    return pl.pallas_call(
        body, name=name,
        out_shape=(jax.ShapeDtypeStruct((D, D_PROJ_A), BF16), jax.ShapeDtypeStruct((D, LANES), BF16)),
        grid=(D // tm,),
        in_specs=[pl.BlockSpec((N_CHIPS, tm, W_IN_PAD), lambda i: (0, i, 0))],
        out_specs=(pl.BlockSpec((tm, D_PROJ_A), lambda i: (i, 0)), pl.BlockSpec((tm, LANES), lambda i: (i, 0))),
        compiler_params=_params('parallel'),
    )(raw)


def _select_w_in_grads(p_a, p_fg, *, name, tm=256):
    D = p_a.shape[0]
    tm = _tile(D, tm, 16)
    n_local = W_IN_PAD // LANES
    plan = []
    for p in range(N_CHIPS):
        for i in range(n_local):
            cmax = max(0, min(LANES, W_IN_SHARD - i * LANES))
            parts = []
            for j in range(N_A_TILES + 1):
                base, rmax = _a_tile_base(j)
                e = base - W_IN_SHARD * p - i * LANES
                if e + rmax - 1 < 0 or e > cmax - 1:
                    continue
                parts.append((j, e, rmax))
            plan.append((p, i, cmax, parts))

    def body(a_ref, fg_ref, o32_ref, o16_ref):
        terms = {}

        def src(j):
            if j not in terms:
                v = fg_ref[...] if j == N_A_TILES else a_ref[:, j * LANES:(j + 1) * LANES]
                terms[j] = _split3(v)
            return terms[j]

        for p, i, cmax, parts in plan:
            acc = jnp.zeros((tm, LANES), F32)
            for j, e, rmax in parts:
                sel = _shift_select(LANES, LANES, e, rmax, cmax)
                for term in src(j):
                    acc = acc + _dot(term, sel, 1, 0)
            o32_ref[p, :, i * LANES:(i + 1) * LANES] = acc
            o16_ref[p, :, i * LANES:(i + 1) * LANES] = acc.astype(BF16)

    out = pl.BlockSpec((N_CHIPS, tm, W_IN_PAD), lambda i: (0, i, 0))
    return pl.pallas_call(
        body, name=name,
        out_shape=(jax.ShapeDtypeStruct((N_CHIPS, D, W_IN_PAD), F32), jax.ShapeDtypeStruct((N_CHIPS, D, W_IN_PAD), BF16)),
        grid=(D // tm,),
        in_specs=[pl.BlockSpec((tm, D_PROJ_A), lambda i: (i, 0)), pl.BlockSpec((tm, LANES), lambda i: (i, 0))],
        out_specs=(out, out),
        compiler_params=_params('parallel'),
    )(p_a, p_fg)


MESH = pl.DeviceIdType.MESH
HBM_SPEC = pl.BlockSpec(memory_space=pltpu.HBM)


def _place():
    return lax.axis_index('x'), lax.axis_index('y'), lax.axis_index('c')


def _other_chips(x, y):
    return [(1 - x, y), (x, 1 - y), (1 - x, 1 - y)]


def _up_pos(q):
    return (q % 2) * 2 + q // 2


CHUNKS = {
    'w_in': ('lead', None),
    'w_up': ('cols', None),
    'w_down': ('rows', None),
    'w_out': ('rows', None),
    'conv_pw_w': ('rows', None),
    'conv_dw_w': ('lead', None),
    'ffn_dw_w': ('lead', None),
}


def _window(ref, kind, l, q):
    at = (lambda *idx: ref.at[idx]) if l is None else (lambda *idx: ref.at[(l,) + idx])
    shape = ref.shape if l is None else ref.shape[1:]
    if kind == 'lead':
        return at(q)
    if kind == 'rows':
        cs = shape[0] // N_CHIPS
        return at(pl.ds(pl.multiple_of(q * cs, 16), cs), slice(None))
    cs = shape[1] // N_CHIPS
    return at(slice(None), pl.ds(pl.multiple_of(_up_pos(q) * cs, LANES), cs))


def _place_shard(src, l, pos_arr, full_shape, kind, *, name, tm=256):
    _, m, n = src.shape
    bm = _tile(m, tm, 16) if kind != 'rows' else m

    def body(pos_ref, s_ref, o_ref):
        o_ref[...] = s_ref[...].astype(BF16)

    if kind == 'lead':
        out = pl.BlockSpec((None, bm, n), lambda i, pos: (pos[0], i, 0))
    elif kind == 'rows':
        out = pl.BlockSpec((bm, n), lambda i, pos: (pos[0], 0))
    else:
        out = pl.BlockSpec((bm, n), lambda i, pos: (i, pos[0]))
    return pl.pallas_call(
        body, name=name, out_shape=jax.ShapeDtypeStruct(full_shape, BF16),
        grid_spec=pltpu.PrefetchScalarGridSpec(
            num_scalar_prefetch=1, grid=(m // bm,),
            in_specs=[pl.BlockSpec((None, bm, n), lambda i, pos: (l, i, 0))], out_specs=out),
        compiler_params=_params('parallel'),
    )(pos_arr, src)


GATHERED = ('w_in', 'w_up', 'w_down', 'w_out', 'conv_pw_w', 'conv_dw_w', 'ffn_dw_w')
GATHER_GROUPS = ((0, ('w_in', 'conv_dw_w', 'ffn_dw_w', 'conv_pw_w')), (0, ('w_out', 'w_up', 'w_down')), (1, GATHERED))
SEM_SPEC = pl.BlockSpec(memory_space=pltpu.SEMAPHORE)
SPLIT_COPY_PARAMS = pltpu.CompilerParams(has_side_effects=pltpu.SideEffectType.DATAFLOW_SIDE_EFFECTING)


def _gather_start(bufs):
    flat = [b for group in bufs for b in group]
    nb = len(flat)

    def body(*refs):
        outs, sems = refs[nb:2 * nb], refs[2 * nb:]
        x, y, c = _place()
        pos = 0
        for g, (_, keys) in enumerate(GATHER_GROUPS):
            for i, k in enumerate(keys):
                w = _window(outs[pos], CHUNKS[k][0], None, 2 * x + y)
                pos += 1
                for j, chip in enumerate(_other_chips(x, y)):
                    pltpu.make_async_remote_copy(src_ref=w, dst_ref=w, send_sem=sems[2 * g].at[3 * i + j],
                                                 recv_sem=sems[2 * g + 1].at[3 * i + j], device_id=(*chip, c),
                                                 device_id_type=MESH).start()

    sem_shapes = [pltpu.SemaphoreType.DMA((3 * len(keys),)) for _, keys in GATHER_GROUPS for _ in range(2)]
    res = pl.pallas_call(
        body, name='gather_start',
        out_shape=tuple(jax.ShapeDtypeStruct(b.shape, b.dtype) for b in flat) + tuple(sem_shapes),
        in_specs=[HBM_SPEC] * nb, out_specs=tuple([HBM_SPEC] * nb + [SEM_SPEC] * len(sem_shapes)),
        input_output_aliases={b: b for b in range(nb)},
        compiler_params=SPLIT_COPY_PARAMS,
    )(*[pltpu.with_memory_space_constraint(b, pltpu.HBM) for b in flat])
    out_bufs, sems, pos = [], res[nb:], 0
    for group in bufs:
        out_bufs.append(list(res[pos:pos + len(group)]))
        pos += len(group)
    return out_bufs, [(sems[2 * g], sems[2 * g + 1]) for g in range(len(GATHER_GROUPS))]


def _gather_wait(g, bufs, sems, after):
    keys = GATHER_GROUPS[g][1]
    nb = len(bufs)

    def body(*refs):
        send_sems, recv_sems = refs[nb], refs[nb + 1]
        outs = refs[nb + 3:]
        x, y, c = _place()
        for i, k in enumerate(keys):
            mine = _window(outs[i], CHUNKS[k][0], None, 2 * x + y)
            for j, (cx, cy) in enumerate(_other_chips(x, y)):
                theirs = _window(outs[i], CHUNKS[k][0], None, 2 * cx + cy)
                cp = pltpu.make_async_remote_copy(src_ref=mine, dst_ref=theirs, send_sem=send_sems.at[3 * i + j],
                                                  recv_sem=recv_sems.at[3 * i + j], device_id=(cx, cy, c), device_id_type=MESH)
                cp.wait_send()
                cp.wait_recv()

    return pl.pallas_call(
        body, name=f'gather_wait_{g}',
        out_shape=tuple(jax.ShapeDtypeStruct(b.shape, b.dtype) for b in bufs),
        in_specs=[HBM_SPEC] * nb + [SEM_SPEC, SEM_SPEC, ANY_SPEC], out_specs=tuple([HBM_SPEC] * nb),
        input_output_aliases={b: b for b in range(nb)},
        compiler_params=SPLIT_COPY_PARAMS,
    )(*bufs, *sems, after)


def _rs_block(M, N):
    return (_tile(M, 256, 16), _tile(N, 2048))


def _chunk_shape(shape, kind):
    if kind == 'lead':
        return tuple(shape[1:])
    if kind == 'rows':
        return (shape[0] // N_CHIPS, shape[1])
    return (shape[0], shape[1] // N_CHIPS)


def _rs_start(tag, bufs, kinds):
    nb = len(bufs)
    lands = [lax.empty((N_CHIPS - 1,) + _chunk_shape(b.shape, k), b.dtype) for b, k in zip(bufs, kinds)]

    def body(*refs):
        src, land = refs[2 * nb:3 * nb], refs[3 * nb:4 * nb]
        send_sems, recv_sems, token = refs[4 * nb:]
        token[...] = jnp.zeros(token.shape, F32)
        x, y, c = _place()
        for b in range(nb):
            for j, (cx, cy) in enumerate(_other_chips(x, y)):
                pltpu.make_async_remote_copy(
                    src_ref=_window(src[b], kinds[b], None, 2 * cx + cy), dst_ref=land[b].at[j],
                    send_sem=send_sems.at[3 * b + j], recv_sem=recv_sems.at[3 * b + j],
                    device_id=(cx, cy, c), device_id_type=MESH).start()

    sem = pltpu.SemaphoreType.DMA((3 * nb,))
    res = pl.pallas_call(
        body, name=f'rs_start_{tag}',
        out_shape=tuple(jax.ShapeDtypeStruct(b.shape, b.dtype) for b in list(bufs) + lands)
        + (sem, sem, jax.ShapeDtypeStruct((8, LANES), F32)),
        in_specs=[HBM_SPEC] * (2 * nb),
        out_specs=tuple([HBM_SPEC] * (2 * nb) + [SEM_SPEC, SEM_SPEC, pl.BlockSpec(memory_space=pltpu.VMEM)]),
        input_output_aliases={b: b for b in range(2 * nb)},
        compiler_params=SPLIT_COPY_PARAMS,
    )(*[pltpu.with_memory_space_constraint(b, pltpu.HBM) for b in list(bufs) + lands])
    return res[:nb], res[nb:2 * nb], res[2 * nb:2 * nb + 2], res[2 * nb + 2]


def _rs_wait(tag, bufs, lands, sems, kinds, after):
    nb = len(bufs)

    def body(*refs):
        send_sems, recv_sems = refs[2 * nb], refs[2 * nb + 1]
        src, land = refs[2 * nb + 3:3 * nb + 3], refs[3 * nb + 3:]
        x, y, c = _place()
        for b in range(nb):
            for j, (cx, cy) in enumerate(_other_chips(x, y)):
                cp = pltpu.make_async_remote_copy(
                    src_ref=_window(src[b], kinds[b], None, 2 * cx + cy), dst_ref=land[b].at[j],
                    send_sem=send_sems.at[3 * b + j], recv_sem=recv_sems.at[3 * b + j],
                    device_id=(cx, cy, c), device_id_type=MESH)
                cp.wait_send()
                cp.wait_recv()

    res = pl.pallas_call(
        body, name=f'rs_wait_{tag}',
        out_shape=tuple(jax.ShapeDtypeStruct(b.shape, b.dtype) for b in list(bufs) + list(lands)),
        in_specs=[HBM_SPEC] * (2 * nb) + [SEM_SPEC, SEM_SPEC, ANY_SPEC], out_specs=tuple([HBM_SPEC] * (2 * nb)),
        input_output_aliases={b: b for b in range(2 * nb)},
        compiler_params=SPLIT_COPY_PARAMS,
    )(*bufs, *lands, *sems, after)
    return res[nb:]


def _rs_sum(p, rb, kind, pos_arr, l, depth, buf, *, name):
    m, n = rb.shape[1:]
    bm, bn = _rs_block(m, n)
    nbm, nbn = m // bm, n // bn
    has_buf = buf is not None

    def body(q_ref, p_ref, r_ref, *rest):
        acc = p_ref[...]
        for j in range(N_CHIPS - 1):
            acc = acc + r_ref[j].astype(F32)
        rest[-1][...] = acc

    if kind == 'lead':
        p_map = lambda i, j, q: (q[0], i, j)
    elif kind == 'rows':
        p_map = lambda i, j, q: (q[0] * nbm + i, j)
    else:
        p_map = lambda i, j, q: (i, q[0] * nbn + j)
    r_spec = pl.BlockSpec((N_CHIPS - 1, bm, bn), lambda i, j, q: (0, i, j))
    p_spec = pl.BlockSpec(((None,) if kind == 'lead' else ()) + (bm, bn), p_map)
    return pl.pallas_call(
        body, name=name, out_shape=jax.ShapeDtypeStruct((depth, m, n), F32),
        grid_spec=pltpu.PrefetchScalarGridSpec(
            num_scalar_prefetch=1, grid=(nbm, nbn), in_specs=[p_spec, r_spec] + ([ANY_SPEC] if has_buf else []),
            out_specs=pl.BlockSpec((None, bm, bn), lambda i, j, q: (l, i, j))),
        input_output_aliases={3: 0} if has_buf else {},
        compiler_params=_params('parallel', 'parallel'),
    )(pos_arr, p, rb, *((buf,) if has_buf else ()))


def _swap_with_sibling(bufs):
    nb = len(bufs)

    def body(*refs):
        ins, outs = refs[:nb], refs[nb:2 * nb]
        send_sems, recv_sems = refs[2 * nb:]
        x, y, c = _place()
        copies = [pltpu.make_async_remote_copy(src_ref=ins[b], dst_ref=outs[b], send_sem=send_sems.at[b],
                                               recv_sem=recv_sems.at[b], device_id=(x, y, 1 - c), device_id_type=MESH)
                  for b in range(nb)]
        for cp in copies:
            cp.start()
        for cp in copies:
            cp.wait()

    return pl.pallas_call(
        body, name='rs_swap_sums', out_shape=tuple(jax.ShapeDtypeStruct(b.shape, b.dtype) for b in bufs),
        in_specs=[HBM_SPEC] * nb, out_specs=tuple([HBM_SPEC] * nb),
        scratch_shapes=[pltpu.SemaphoreType.DMA((nb,)), pltpu.SemaphoreType.DMA((nb,))],
    )(*bufs)


def _all_reduce_small(v):
    r = v.shape[0]

    def body(x_ref, tot_ref, all_ref, send_sems, recv_sems):
        x, y, c = _place()
        me, sibling = (x, y, c), (x, y, 1 - c)
        chips = _other_chips(x, y)

        def rows(px, py, pc):
            return all_ref.at[pl.ds((4 * px + 2 * py + pc) * r, r), :]

        def copy(k, block, to, src=None):
            return pltpu.make_async_remote_copy(
                src_ref=rows(*block) if src is None else src, dst_ref=rows(*block),
                send_sem=send_sems.at[k], recv_sem=recv_sems.at[k], device_id=to, device_id_type=MESH)

        rows(*me)[...] = x_ref[...]
        first = [copy(0, me, sibling, src=x_ref)]
        first += [copy(1 + j, me, (*chip, c), src=x_ref) for j, chip in enumerate(chips)]
        for cp in first:
            cp.start()
        passed = [copy(4 + j, (*chip, c), sibling) for j, chip in enumerate(chips)]
        for j, chip in enumerate(chips):
            copy(1 + j, (*chip, c), me).wait_recv()
            passed[j].start()
        copy(0, sibling, me).wait_recv()
        for j, chip in enumerate(chips):
            copy(4 + j, (*chip, 1 - c), me).wait_recv()
        for cp in first + passed:
            cp.wait_send()
        acc = all_ref[0:r, :]
        for d in range(1, N_DEV):
            acc = acc + all_ref[d * r:(d + 1) * r, :]
        tot_ref[...] = acc

    return pl.pallas_call(
        body, name='all_reduce_small', out_shape=jax.ShapeDtypeStruct((r, LANES), F32),
        in_specs=[pl.BlockSpec(memory_space=pltpu.VMEM)], out_specs=pl.BlockSpec(memory_space=pltpu.VMEM),
        scratch_shapes=[pltpu.VMEM((N_DEV * r, LANES), F32), pltpu.SemaphoreType.DMA((7,)), pltpu.SemaphoreType.DMA((7,))],
    )(v)


def _adamw(w, g, m, v, *, name, g2=None, ts=256):
    R, C = w.shape
    Cg = g.shape[1]
    ts = _tile(R, ts, 8)
    c1 = 1.0 - ADAM_B1 ** ADAM_STEP
    c2 = 1.0 - ADAM_B2 ** ADAM_STEP
    two = g2 is not None

    def body(w_ref, g_ref, *rest):
        m_ref, v_ref, go_ref, d_ref, nm_ref, nv_ref = rest[two:]
        gv = g_ref[:, 0:C]
        if two:
            gv = gv + rest[0][:, 0:C]
        nm = ADAM_B1 * m_ref[...] + (1.0 - ADAM_B1) * gv
        nv = ADAM_B2 * v_ref[...] + (1.0 - ADAM_B2) * (gv * gv)
        d_ref[...] = -ADAM_LR * ((nm / c1) / (jnp.sqrt(nv / c2) + ADAM_EPS) + ADAM_WD * w_ref[...])
        go_ref[...] = gv
        nm_ref[...] = nm
        nv_ref[...] = nv

    blk = pl.BlockSpec((ts, C), lambda i: (i, 0))
    gblk = pl.BlockSpec((ts, Cg), lambda i: (i, 0))
    shape = jax.ShapeDtypeStruct((R, C), F32)
    return pl.pallas_call(body, name=name, out_shape=(shape, shape, shape, shape), grid=(R // ts,),
                          in_specs=[blk, gblk] + ([gblk] if two else []) + [blk, blk], out_specs=(blk, blk, blk, blk),
                          compiler_params=_params('parallel'))(w, g, *((g2,) if two else ()), m, v)


def _pack_rows(parts, row_unit):
    flat = jnp.concatenate(parts)
    flat = _pad_axis(flat, -(-flat.shape[0] // (row_unit * LANES)) * row_unit * LANES, 0)
    return flat.reshape(-1, LANES)


def _as_2d(a):
    return a.reshape(-1, a.shape[-1])


def _mesh_place():
    cx, cy, cc = _place()
    chip = 2 * cx + cy
    as_arr = lambda v: jnp.reshape(v, (1,)).astype(jnp.int32)
    return chip, as_arr(cc), as_arr(chip), as_arr(_up_pos(chip))


class _LayerWeights:
    def __init__(self, groups):
        self.groups = groups
        self.ready = {}

    def get(self, name, after):
        if name not in self.ready:
            for names, wait in self.groups:
                if name in names:
                    self.ready.update({k: v[None] for k, v in wait(after).items()})
        return self.ready[name]


def _gather_full(w, place):
    chip, _, chip_arr, up_pos_arr = place
    L, D = w['w_in'].shape[:2]
    w_in_pad = _pad_axis(w['w_in'], W_IN_PAD, 2)

    def placed(k, l):
        if k == 'w_in':
            return _place_shard(w_in_pad, l, chip_arr, (N_CHIPS, D, W_IN_PAD), 'lead', name=f'place_w_in_{l}')
        if k == 'w_up':
            return _place_shard(w[k], l, up_pos_arr, (w[k].shape[1], N_CHIPS * w[k].shape[2]), 'cols', name=f'place_w_up_{l}')
        if k in ('conv_dw_w', 'ffn_dw_w'):
            return lax.dynamic_update_slice_in_dim(jnp.zeros((N_CHIPS,) + w[k].shape[1:], F32), w[k][l][None], chip, axis=0)
        return _place_shard(w[k], l, chip_arr, (N_CHIPS * w[k].shape[1], w[k].shape[2]), 'rows', name=f'place_{k}_{l}')

    bufs, sems = _gather_start([[placed(k, l) for k in keys] for l, keys in GATHER_GROUPS])
    unchunk = lambda a: jnp.moveaxis(a, 0, 1).reshape(a.shape[1], -1)

    def waiter(g):
        l, keys = GATHER_GROUPS[g]

        def wait(after):
            full = dict(zip(keys, _gather_wait(g, bufs[g], sems[g], after)))
            out = {}
            if 'w_in' in full:
                out['w_a'], w_fg = _select_w_in(full['w_in'], name=f'select_w_in_{l}')
                out['w_fg_t'] = w_fg.T
            if 'conv_dw_w' in full:
                out['dw_w'] = _pad_axis(unchunk(full['conv_dw_w']), CONV_HALO, 0)
            if 'ffn_dw_w' in full:
                out['ffn_w'] = _pad_axis(_pair_cols(unchunk(full['ffn_dw_w'])), FFN_HALO, 0)
            if 'conv_pw_w' in full:
                out['pw_w'] = full['conv_pw_w']
            out.update({k: full[k] for k in ('w_up', 'w_down', 'w_out') if k in full})
            return out

        names = {'w_in': ('w_a', 'w_fg_t'), 'conv_dw_w': ('dw_w',), 'ffn_dw_w': ('ffn_w',), 'conv_pw_w': ('pw_w',)}
        return tuple(n for k in keys for n in names.get(k, (k,))), wait

    return [_LayerWeights([waiter(g) for g in range(len(GATHER_GROUPS)) if GATHER_GROUPS[g][0] == l]) for l in range(L)]


RS_WIRE = ('w_in', 'w_up', 'w_down', 'w_out')
RS_GROUPS = (('ffn', ('w_down', 'w_up')), ('mix', ('w_out', 'w_in')))


class _GradReducer:
    def __init__(self, place, depth):
        _, _, self.chip_arr, self.up_pos_arr = place
        self.depth = depth
        self.got = {}
        self.flying = {}
        self.sums = {}

    def put(self, l, key, g32, g16):
        if key == 'w_in':
            g32, g16 = _select_w_in_grads(g32, g16, name=f'l{l}_select_w_in_grads')
        self.got[(l, key)] = (g32, g16)
        for tag, keys in RS_GROUPS:
            if key == keys[-1]:
                kinds = [CHUNKS[k][0] for k in keys]
                bufs, lands, sems, token = _rs_start(f'l{l}_{tag}', [self.got[(l, k)][1] for k in keys], kinds)
                self.flying[(l, tag)] = (bufs, lands, sems, kinds)
                return token
        return None

    def point(self, l, where, after):
        if where == 'mid':
            self._land(l + 1, 'mix', after)
        else:
            self._land(l, 'ffn', after)

    def _land(self, l, tag, after):
        if (l, tag) not in self.flying:
            return
        bufs, lands, sems, kinds = self.flying.pop((l, tag))
        lands = _rs_wait(f'l{l}_{tag}', bufs, lands, sems, kinds, after)
        for k, rb, kind in zip(dict(RS_GROUPS)[tag], lands, kinds):
            pos = self.up_pos_arr if kind == 'cols' else self.chip_arr
            self.sums[k] = _rs_sum(self.got.pop((l, k))[0], rb, kind, pos, l, self.depth, self.sums.get(k), name=f'l{l}_rs_sum_{k}')

    def finish(self, after):
        for l, tag in list(self.flying):
            self._land(l, tag, after)
        mine = [self.sums[k] for k in RS_WIRE]
        return {k: pair for k, pair in zip(RS_WIRE, zip(mine, _swap_with_sibling(mine)))}


def kernel(x, norm1_g, w_in, b_f, q_norm_g, k_norm_g, conv_dw_w, conv_dw_b, conv_ln_g, conv_ln_b, conv_pw_w, pool_w, pool_scale, w_out, norm2_g, w_up, ffn_dw_w, w_down, loss_target, m_norm1_g, m_w_in, m_b_f, m_q_norm_g, m_k_norm_g, m_conv_dw_w, m_conv_dw_b, m_conv_ln_g, m_conv_ln_b, m_conv_pw_w, m_pool_w, m_pool_scale, m_w_out, m_norm2_g, m_w_up, m_ffn_dw_w, m_w_down, v_norm1_g, v_w_in, v_b_f, v_q_norm_g, v_k_norm_g, v_conv_dw_w, v_conv_dw_b, v_conv_ln_g, v_conv_ln_b, v_conv_pw_w, v_pool_w, v_pool_scale, v_w_out, v_norm2_g, v_w_up, v_ffn_dw_w, v_w_down):
    given = dict(locals())
    w = {k: given[k] for k in WEIGHTS}
    mom_m = {k: given['m_' + k] for k in WEIGHTS}
    mom_v = {k: given['v_' + k] for k in WEIGHTS}
    place = _mesh_place()
    chip = place[0]
    W = _gather_full(w, place)

    reducer = _GradReducer(place, norm1_g.shape[0])
    loss_part, grad_x, g_small = _local_step(x[0], loss_target[0], W, {k: w[k] for k in REPLICATED}, reducer)
    loss = lax.psum(loss_part[0, 0], ('x', 'y', 'c'))
    sums = reducer.finish(grad_x)

    small = _pack_rows([g_small[k].reshape(-1) for k in SMALL_GRADS], 8)
    small_sum = _all_reduce_small(small)

    g_sum, delta, new_m, new_v = {}, {}, {}, {}
    for k in RS_WIRE:
        outs = _adamw(_as_2d(w[k]), _as_2d(sums[k][0]), _as_2d(mom_m[k]), _as_2d(mom_v[k]), g2=_as_2d(sums[k][1]), name='adamw_' + k)
        g_sum[k], delta[k], new_m[k], new_v[k] = [o.reshape(w[k].shape) for o in outs]
    off = 0
    small_full = {}
    for k in SMALL_GRADS:
        small_full[k] = small_sum.reshape(-1)[off:off + g_small[k].size].reshape(g_small[k].shape)
        off += g_small[k].size
    small_g = {k: small_full[k] for k in REPLICATED}
    small_g['conv_dw_w'] = lax.dynamic_slice_in_dim(small_full['conv_dw_w'], chip * w['conv_dw_w'].shape[2], w['conv_dw_w'].shape[2], axis=2)
    small_g['conv_pw_w'] = lax.dynamic_slice_in_dim(small_full['conv_pw_w'], chip * w['conv_pw_w'].shape[1], w['conv_pw_w'].shape[1], axis=1)
    small_g['ffn_dw_w'] = lax.dynamic_slice_in_dim(small_full['ffn_dw_w'], chip * w['ffn_dw_w'].shape[2], w['ffn_dw_w'].shape[2], axis=2)
    pack_small = lambda t: _pack_rows([t[k].reshape(-1) for k in SMALL_GRADS], 256)
    outs = _adamw(pack_small(w), pack_small(small_g), pack_small(mom_m), pack_small(mom_v), name='adamw_small')
    off = 0
    for k in SMALL_GRADS:
        pieces = [o.reshape(-1)[off:off + w[k].size].reshape(w[k].shape) for o in outs]
        g_sum[k], delta[k], new_m[k], new_v[k] = pieces
        off += w[k].size

    return (loss, grad_x[None], *[g_sum[k] for k in WEIGHTS], *[delta[k] for k in WEIGHTS],
            *[new_m[k] for k in WEIGHTS], *[new_v[k] for k in WEIGHTS])
```

```python
import functools

import jax
import jax.numpy as jnp
from jax import lax
from jax.experimental import pallas as pl
from jax.experimental.pallas import tpu as pltpu

F32 = jnp.float32
BF16 = jnp.bfloat16

N_HEADS = 8
HEAD_DIM = 64
D_ATT = N_HEADS * HEAD_DIM
D_CONV = 256
D_POOL = 256
D_MIX = D_ATT + D_CONV + D_POOL
D_QKV = 3 * D_ATT
D_PROJ_A = D_QKV + 2 * D_CONV + D_POOL
D_IN = D_PROJ_A + N_HEADS
FG_ROWS = 128
CONV_WIDTH = 31
CONV_HALO = 32
POOL_WINDOWS = (2, 4, 8, 16)
POOL_GROUP = 64
POOL_HALO = 16
FFN_CONV_WIDTH = 3
FFN_HALO = 8
ATT_SCALE = HEAD_DIM ** -0.5
EPS = 1e-6
NEG = -1e30
LANES = 128

ADAM_LR = 0.001
ADAM_B1 = 0.9
ADAM_B2 = 0.999
ADAM_EPS = 1e-08
ADAM_WD = 0.01
ADAM_STEP = 10

N_CHIPS = 4
N_DEV = 8
VMEM_LIMIT_BYTES = 56 * 1024 * 1024

REPLICATED = ('norm1_g', 'b_f', 'q_norm_g', 'k_norm_g', 'conv_dw_b', 'conv_ln_g', 'conv_ln_b',
              'pool_w', 'pool_scale', 'norm2_g')
WEIGHTS = ('norm1_g', 'w_in', 'b_f', 'q_norm_g', 'k_norm_g', 'conv_dw_w', 'conv_dw_b', 'conv_ln_g',
           'conv_ln_b', 'conv_pw_w', 'pool_w', 'pool_scale', 'w_out', 'norm2_g', 'w_up', 'ffn_dw_w', 'w_down')


def _tile(dim, pref, unit=LANES):
    if dim <= pref:
        return dim
    t = (pref // unit) * unit
    while t >= unit:
        if dim % t == 0:
            return t
        t -= unit
    raise ValueError(f'no tile for {dim} (preferred {pref})')


def _params(*sem):
    return pltpu.CompilerParams(dimension_semantics=sem, vmem_limit_bytes=VMEM_LIMIT_BYTES)


def _sigmoid(x):
    return 1.0 / (1.0 + jnp.exp(-x))


def _dot(a, b, ca, cb):
    return lax.dot_general(a, b, (((ca,), (cb,)), ((), ())), preferred_element_type=F32)


def _split3(y):
    y1 = y.astype(BF16)
    r1 = y - y1.astype(F32)
    y2 = r1.astype(BF16)
    y3 = (r1 - y2.astype(F32)).astype(BF16)
    return y1, y2, y3


def _dot3(y, e, ca=1, cb=0):
    y1, y2, y3 = _split3(y)
    return _dot(y1, e, ca, cb) + _dot(y2, e, ca, cb) + _dot(y3, e, ca, cb)


def _lead(spec_shape, imap, lead):
    if lead is None:
        return pl.BlockSpec(spec_shape, imap)
    return pl.BlockSpec((None,) + spec_shape, lambda *g: (lead,) + imap(*g))


ANY_SPEC = pl.BlockSpec(memory_space=pl.ANY)


def _mm(a, b, *, name, ta=False, tb=False, res=None, out_dtype=F32, tm=512, tn=512, tk=1024,
        a_lead=None, b_lead=None, copy16=False, after=None, cols_outer=False):
    a2, b2 = a.shape[-2:], b.shape[-2:]
    K, M = a2 if ta else a2[::-1]
    N, Kb = b2 if tb else b2[::-1]
    assert K == Kb, (a.shape, b.shape)
    tm, tn, tk = _tile(M, tm), _tile(N, tn), _tile(K, tk)
    nk = K // tk
    ca = 0 if ta else 1
    cb = 1 if tb else 0
    has_res = res is not None
    n_in = 2 + has_res + (after is not None)
    n_out = 1 + copy16

    def body(*refs):
        a_ref, b_ref = refs[:2]
        r_ref = refs[2] if has_res else None
        o_refs = refs[n_in:n_in + n_out]
        scratch = refs[n_in + n_out:]

        def write(r):
            if has_res:
                r = r + r_ref[...]
            o_refs[0][...] = r.astype(out_dtype)
            if copy16:
                o_refs[1][...] = r.astype(BF16)

        p = _dot(a_ref[...].astype(BF16), b_ref[...].astype(BF16), ca, cb)
        if nk == 1:
            write(p)
        else:
            acc = scratch[0]
            k = pl.program_id(2)

            @pl.when(k == 0)
            def _():
                acc[...] = p

            @pl.when(k > 0)
            def _():
                acc[...] += p

            @pl.when(k == nk - 1)
            def _():
                write(acc[...])

    ij = (lambda g0, g1: (g1, g0)) if cols_outer else (lambda g0, g1: (g0, g1))
    at = lambda f: (lambda g0, g1, k: f(*ij(g0, g1), k))
    a_spec = _lead((tk, tm), at(lambda i, j, k: (k, i)), a_lead) if ta else _lead((tm, tk), at(lambda i, j, k: (i, k)), a_lead)
    b_spec = _lead((tn, tk), at(lambda i, j, k: (j, k)), b_lead) if tb else _lead((tk, tn), at(lambda i, j, k: (k, j)), b_lead)
    o_spec = pl.BlockSpec((tm, tn), at(lambda i, j, k: (i, j)))
    in_specs = [a_spec, b_spec] + ([o_spec] if has_res else []) + ([ANY_SPEC] if after is not None else [])
    args = (a, b) + ((res,) if has_res else ()) + ((after,) if after is not None else ())
    out_shape = [jax.ShapeDtypeStruct((M, N), out_dtype)] + ([jax.ShapeDtypeStruct((M, N), BF16)] if copy16 else [])
    out = pl.pallas_call(
        body, name=name,
        out_shape=tuple(out_shape),
        grid=ij(M // tm, N // tn) + (nk,),
        in_specs=in_specs, out_specs=tuple([o_spec] * n_out),
        scratch_shapes=[pltpu.VMEM((tm, tn), F32)] if nk > 1 else [],
        compiler_params=_params('parallel', 'parallel', 'arbitrary'),
    )(*args)
    return out if copy16 else out[0]


def _rms_fwd(x, g, *, name, ts=512, after=None):
    S, D = x.shape
    ts = _tile(S, ts, 8)

    def body(x_ref, g_ref, *rest):
        xv = x_ref[...]
        r = lax.rsqrt(jnp.mean(xv * xv, axis=-1, keepdims=True) + EPS)
        rest[-1][...] = (xv * r * g_ref[...]).astype(BF16)

    extra = () if after is None else (after,)
    return pl.pallas_call(
        body, name=name, out_shape=jax.ShapeDtypeStruct((S, D), BF16), grid=(S // ts,),
        in_specs=[pl.BlockSpec((ts, D), lambda i: (i, 0)), pl.BlockSpec((1, D), lambda i: (0, 0))] + [ANY_SPEC] * len(extra),
        out_specs=pl.BlockSpec((ts, D), lambda i: (i, 0)),
        compiler_params=_params('parallel'),
    )(x, g, *extra)


def _rms_bwd(x, g, dh, dres, *, name, ts=512):
    S, D = x.shape
    ts = _tile(S, ts, 8)

    def body(x_ref, g_ref, dh_ref, dr_ref, dx_ref, dg_ref):
        i = pl.program_id(0)
        xv = x_ref[...]
        r = lax.rsqrt(jnp.mean(xv * xv, axis=-1, keepdims=True) + EPS)
        y = xv * r
        dh_v = dh_ref[...]
        dy = dh_v * g_ref[...]
        dx_ref[...] = dr_ref[...] + r * (dy - y * jnp.mean(dy * y, axis=-1, keepdims=True))
        part = jnp.sum(dh_v * y, axis=0, keepdims=True)

        @pl.when(i == 0)
        def _():
            dg_ref[...] = part

        @pl.when(i > 0)
        def _():
            dg_ref[...] += part

    row = pl.BlockSpec((ts, D), lambda i: (i, 0))
    vec = pl.BlockSpec((1, D), lambda i: (0, 0))
    return pl.pallas_call(
        body, name=name,
        out_shape=(jax.ShapeDtypeStruct((S, D), F32), jax.ShapeDtypeStruct((1, D), F32)),
        grid=(S // ts,), in_specs=[row, vec, row, row], out_specs=(row, vec),
        compiler_params=_params('arbitrary'),
    )(x, g, dh, dres)


def _pair_ones():
    i = lax.broadcasted_iota(jnp.int32, (LANES, LANES), 0) // HEAD_DIM
    j = lax.broadcasted_iota(jnp.int32, (LANES, LANES), 1) // HEAD_DIM
    return (i == j).astype(BF16)


def _head_sums(y, e):
    return jnp.concatenate([_dot3(y[:, b * LANES:(b + 1) * LANES], e) for b in range(D_ATT // LANES)], axis=1)


def _qk_prep_fwd(proj_a, qg, kg, *, name, ts=512):
    S = proj_a.shape[0]
    ts = _tile(S, ts, 16)

    def body(q_ref, k_ref, v_ref, qg_ref, kg_ref, e_ref, o_ref):
        e = e_ref[...]

        def norm(xv, gain):
            ms = _head_sums(xv * xv, e) * (1.0 / HEAD_DIM)
            return xv * lax.rsqrt(ms + EPS) * gain

        o_ref[:, 0:D_ATT] = (norm(q_ref[...], qg_ref[...]) * ATT_SCALE).astype(BF16)
        o_ref[:, D_ATT:2 * D_ATT] = norm(k_ref[...], kg_ref[...]).astype(BF16)
        o_ref[:, 2 * D_ATT:3 * D_ATT] = v_ref[...].astype(BF16)

    col = lambda c: pl.BlockSpec((ts, D_ATT), lambda i: (i, c))
    vec = pl.BlockSpec((1, D_ATT), lambda i: (0, 0))
    return pl.pallas_call(
        body, name=name, out_shape=jax.ShapeDtypeStruct((S, D_QKV), BF16), grid=(S // ts,),
        in_specs=[col(0), col(1), col(2), vec, vec, pl.BlockSpec((LANES, LANES), lambda i: (0, 0))],
        out_specs=pl.BlockSpec((ts, D_QKV), lambda i: (i, 0)),
        compiler_params=_params('parallel'),
    )(proj_a, proj_a, proj_a, qg, kg, _pair_ones())


def _qk_prep_bwd(proj_a, dq, dk, dv, qg, kg, *, name, ts=512):
    S = proj_a.shape[0]
    ts = _tile(S, ts, 16)

    def body(q_ref, k_ref, dq_ref, dk_ref, dv_ref, qg_ref, kg_ref, e_ref, o_ref, dqg_ref, dkg_ref):
        i = pl.program_id(0)
        e = e_ref[...]

        def norm_bwd(xv, dn, gain, scale):
            ms = _head_sums(xv * xv, e) * (1.0 / HEAD_DIM)
            r = lax.rsqrt(ms + EPS)
            y = xv * r
            dy = dn * (gain * scale)
            mean = _head_sums(dy * y, e) * (1.0 / HEAD_DIM)
            return r * (dy - y * mean), jnp.sum(dn * y, axis=0, keepdims=True) * scale

        dq_raw, dqg = norm_bwd(q_ref[...], dq_ref[...], qg_ref[...], ATT_SCALE)
        dk_raw, dkg = norm_bwd(k_ref[...], dk_ref[...], kg_ref[...], 1.0)
        o_ref[:, 0:D_ATT] = dq_raw.astype(BF16)
        o_ref[:, D_ATT:2 * D_ATT] = dk_raw.astype(BF16)
        o_ref[:, 2 * D_ATT:3 * D_ATT] = dv_ref[...].astype(BF16)

        @pl.when(i == 0)
        def _():
            dqg_ref[...] = dqg
            dkg_ref[...] = dkg

        @pl.when(i > 0)
        def _():
            dqg_ref[...] += dqg
            dkg_ref[...] += dkg

    col = lambda c: pl.BlockSpec((ts, D_ATT), lambda i: (i, c))
    vec = pl.BlockSpec((1, D_ATT), lambda i: (0, 0))
    return pl.pallas_call(
        body, name=name,
        out_shape=(jax.ShapeDtypeStruct((S, D_PROJ_A), BF16), jax.ShapeDtypeStruct((1, D_ATT), F32),
                   jax.ShapeDtypeStruct((1, D_ATT), F32)),
        grid=(S // ts,),
        in_specs=[col(0), col(1), col(0), col(0), col(0), vec, vec, pl.BlockSpec((LANES, LANES), lambda i: (0, 0))],
        out_specs=(pl.BlockSpec((ts, D_QKV), lambda i: (i, 0)), vec, vec),
        compiler_params=_params('arbitrary'),
    )(proj_a, proj_a, dq, dk, dv, qg, kg, _pair_ones())


def _tri_ones(upper):
    i = lax.broadcasted_iota(jnp.int32, (LANES, LANES), 0)
    j = lax.broadcasted_iota(jnp.int32, (LANES, LANES), 1)
    return ((i <= j) if upper else (i >= j)).astype(BF16)


def _forget_fwd(z_raw, b_col, *, name):
    R, S = z_raw.shape
    nb = S // LANES

    def body(z_ref, b_ref, u_ref, f_ref):
        u = u_ref[...]
        carry = jnp.zeros((R, 1), F32)
        for j in range(nb):
            z = z_ref[:, j * LANES:(j + 1) * LANES] + b_ref[...]
            logf = jnp.minimum(z, 0.0) - jnp.log(1.0 + jnp.exp(-jnp.abs(z)))
            f_ref[:, j * LANES:(j + 1) * LANES] = _dot3(logf, u) + carry
            carry = carry + jnp.sum(logf, axis=1, keepdims=True)

    return pl.pallas_call(
        body, name=name, out_shape=jax.ShapeDtypeStruct((R, S), F32),
        compiler_params=pltpu.CompilerParams(vmem_limit_bytes=VMEM_LIMIT_BYTES),
    )(z_raw, b_col, _tri_ones(True))


def _forget_bwd(z_raw, b_col, df, *, name):
    R, S = z_raw.shape
    nb = S // LANES

    def body(z_ref, b_ref, df_ref, l_ref, dz_ref, db_ref):
        low = l_ref[...]
        carry = jnp.zeros((R, 1), F32)
        db = jnp.zeros((R, 1), F32)
        for j in reversed(range(nb)):
            d = df_ref[:, j * LANES:(j + 1) * LANES]
            dlogf = _dot3(d, low) + carry
            carry = carry + jnp.sum(d, axis=1, keepdims=True)
            z = z_ref[:, j * LANES:(j + 1) * LANES] + b_ref[...]
            dz = dlogf * _sigmoid(-z)
            dz_ref[:, j * LANES:(j + 1) * LANES] = dz
            db = db + jnp.sum(dz, axis=1, keepdims=True)
        db_ref[...] = db

    return pl.pallas_call(
        body, name=name,
        out_shape=(jax.ShapeDtypeStruct((R, S), F32), jax.ShapeDtypeStruct((R, 1), F32)),
        compiler_params=pltpu.CompilerParams(vmem_limit_bytes=VMEM_LIMIT_BYTES),
    )(z_raw, b_col, df, _tri_ones(False))


def _head_mask(hh):
    lane = lax.broadcasted_iota(jnp.int32, (1, LANES), 1)
    return (lane // HEAD_DIM) == hh


def _causal(s, qi, ki, t):
    rows = qi * t + lax.broadcasted_iota(jnp.int32, (t, t), 0)
    cols = ki * t + lax.broadcasted_iota(jnp.int32, (t, t), 1)
    return jnp.where(cols <= rows, s, NEG)


AUG = 2 * HEAD_DIM


def _aug_consts():
    i = lax.broadcasted_iota(jnp.int32, (D_ATT, N_HEADS * AUG), 0)
    j = lax.broadcasted_iota(jnp.int32, (D_ATT, N_HEADS * AUG), 1)
    spread = (j == (i // HEAD_DIM) * AUG + i % HEAD_DIM).astype(BF16)
    h = lax.broadcasted_iota(jnp.int32, (LANES, N_HEADS * AUG), 0)
    c = lax.broadcasted_iota(jnp.int32, (LANES, N_HEADS * AUG), 1)
    gate = [((c == h * AUG + HEAD_DIM + t) & (h < N_HEADS)).astype(BF16) for t in range(3)]
    lane = lax.broadcasted_iota(jnp.int32, (1, N_HEADS * AUG), 1) % AUG
    ones_q = ((lane >= HEAD_DIM) & (lane < HEAD_DIM + 3)).astype(F32)
    ones_v = (lane == HEAD_DIM).astype(F32)
    return spread, gate, ones_q, ones_v


def _attn_aug(qkv, f_cum, *, name, ts=512):
    S = qkv.shape[0]
    ts = _tile(S, ts)
    spread, gate, ones_q, ones_v = _aug_consts()
    W = N_HEADS * AUG

    def body(q_ref, k_ref, v_ref, f_ref, sp_ref, g0_ref, g1_ref, g2_ref, oq_ref, ov_ref, qa_ref, ka_ref, va_ref):
        sp = sp_ref[...]
        qa_ref[...] = (_dot(q_ref[...], sp, 1, 0) + oq_ref[...]).astype(BF16)
        va_ref[...] = (_dot(v_ref[...], sp, 1, 0) + ov_ref[...]).astype(BF16)
        terms = _split3(-jnp.transpose(f_ref[...]))
        ka = _dot(k_ref[...], sp, 1, 0)
        for t, g_ref in zip(terms, (g0_ref, g1_ref, g2_ref)):
            ka = ka + _dot(t, g_ref[...], 1, 0)
        ka_ref[...] = ka.astype(BF16)

    col = lambda c: pl.BlockSpec((ts, D_ATT), lambda i: (i, c))
    full = lambda a: pl.BlockSpec(a.shape, lambda i: (0, 0))
    out = pl.BlockSpec((ts, W), lambda i: (i, 0))
    shape = jax.ShapeDtypeStruct((S, W), BF16)
    consts = (spread, *gate, ones_q, ones_v)
    return pl.pallas_call(
        body, name=name, out_shape=(shape, shape, shape), grid=(S // ts,),
        in_specs=[col(0), col(1), col(2), pl.BlockSpec((FG_ROWS, ts), lambda i: (0, i))] + [full(a) for a in consts],
        out_specs=(out, out, out),
        compiler_params=_params('parallel'),
    )(qkv, qkv, qkv, f_cum, *consts)


def _attn_fwd(qa, ka, va, *, name, tq=1024, tk=1024):
    S = qa.shape[0]
    tq, tk = _tile(S, tq), _tile(S, tk)
    nq, nk = S // tq, S // tk
    npair = N_HEADS // 2

    def body(q_ref, k_ref, v_ref, mix_ref, o_ref, lse_ref, m_s, acc_s):
        qi, ki = pl.program_id(1), pl.program_id(2)
        last = (qi * tq + tq - 1) // tk
        first_masked = (qi * tq) // tk

        @pl.when(ki == 0)
        def _():
            m_s[...] = jnp.full(m_s.shape, NEG, F32)
            acc_s[...] = jnp.zeros(acc_s.shape, F32)

        def step(masked):
            if masked:
                rows = qi * tq + lax.broadcasted_iota(jnp.int32, (tq, tk), 0)
                cols = ki * tk + lax.broadcasted_iota(jnp.int32, (tq, tk), 1)
                keep = cols <= rows
            m_prev = [m_s[hh] for hh in range(2)]
            acc_prev = [acc_s[hh] for hh in range(2)]
            ss = []
            for hh in range(2):
                s = _dot(q_ref[:, hh * AUG:(hh + 1) * AUG], k_ref[:, hh * AUG:(hh + 1) * AUG], 1, 1)
                ss.append(jnp.where(keep, s, NEG) if masked else s)
            m_new = [jnp.maximum(m_prev[hh], jnp.max(ss[hh], axis=1, keepdims=True)) for hh in range(2)]
            ps = [jnp.exp(ss[hh] - jnp.tile(m_new[hh], (1, tk // LANES))).astype(BF16) for hh in range(2)]
            for hh in range(2):
                alpha = jnp.exp(m_prev[hh] - m_new[hh])
                acc_s[hh] = alpha * acc_prev[hh] + _dot(ps[hh], v_ref[:, hh * AUG:(hh + 1) * AUG], 1, 0)
                m_s[hh] = m_new[hh]

        @pl.when(ki < first_masked)
        def _():
            step(False)

        @pl.when((ki >= first_masked) & (ki <= last))
        def _():
            step(True)

        @pl.when(ki == last)
        def _():
            lane = lax.broadcasted_iota(jnp.int32, (1, LANES), 1)
            outs, lses = [], []
            for hh in range(2):
                acc = acc_s[hh]
                denom = jnp.sum(jnp.where(lane == HEAD_DIM, acc, 0.0), axis=1, keepdims=True)
                outs.append(acc / denom)
                lses.append(m_s[hh] + jnp.log(denom))
            o = jnp.where(lane < HEAD_DIM, outs[0], pltpu.roll(outs[1], HEAD_DIM, 1))
            o_ref[...] = o
            mix_ref[...] = o.astype(BF16)
            lse_ref[...] = jnp.where(lane < HEAD_DIM, lses[0], lses[1])

    def kmap(h, i, j):
        return (jnp.minimum(j, (i * tq + tq - 1) // tk), h)

    out = pl.BlockSpec((tq, LANES), lambda h, i, j: (i, h))
    return pl.pallas_call(
        body, name=name,
        out_shape=(jax.ShapeDtypeStruct((S, D_MIX), BF16), jax.ShapeDtypeStruct((S, D_ATT), F32),
                   jax.ShapeDtypeStruct((S, D_ATT), F32)),
        grid=(npair, nq, nk),
        in_specs=[pl.BlockSpec((tq, 2 * AUG), lambda h, i, j: (i, h)),
                  pl.BlockSpec((tk, 2 * AUG), kmap), pl.BlockSpec((tk, 2 * AUG), kmap)],
        out_specs=(out, out, out),
        scratch_shapes=[pltpu.VMEM((2, tq, LANES), F32), pltpu.VMEM((2, tq, LANES), F32)],
        compiler_params=_params('parallel', 'parallel', 'arbitrary'),
    )(qa, ka, va)


def _attn_bwd(qkv, f3, att, lse, d_mix, *, name, t=1024):
    S = qkv.shape[0]
    t = _tile(S, t)
    n = S // t
    npair = N_HEADS // 2

    def body(q_ref, k_ref, v_ref, f_ref, o_ref, lse_ref, do_ref, dq_ref, dk_ref, dv_ref, df_ref, dr_ref, dk_s, dv_s, df_s):
        ki, qi = pl.program_id(1), pl.program_id(2)

        @pl.when(qi == ki)
        def _():
            dk_s[...] = jnp.zeros(dk_s.shape, F32)
            dv_s[...] = jnp.zeros(dv_s.shape, F32)
            df_s[...] = jnp.zeros(df_s.shape, F32)

        def step(masked):
            q, k, v = q_ref[...], k_ref[...], v_ref[...]
            do, o, lse = do_ref[...], o_ref[...], lse_ref[...]
            lane = lax.broadcasted_iota(jnp.int32, (1, LANES), 1)
            lse_sw = pltpu.roll(lse, HEAD_DIM, 1)
            delta = _dot3(do.astype(BF16).astype(F32) * o, _pair_ones())
            delta_sw = pltpu.roll(delta, HEAD_DIM, 1)
            dq_blk = jnp.zeros((t, LANES), F32)
            dr_blk = jnp.zeros((t, LANES), F32)
            for hh in range(2):
                msk = _head_mask(hh)
                first = lane < HEAD_DIM if hh == 0 else lane >= HEAD_DIM
                qm = jnp.where(msk, q, jnp.zeros_like(q))
                km = jnp.where(msk, k, jnp.zeros_like(k))
                do_h = jnp.where(msk, do, 0.0)
                dom = do_h.astype(BF16)
                s = _dot(qm, k, 1, 1) - f_ref[0, hh:hh + 1, :]
                if masked:
                    s = _causal(s, qi, ki, t)
                lse_h = jnp.where(first, lse, lse_sw)
                delta_h = jnp.where(first, delta, delta_sw)
                p = jnp.exp(s - jnp.tile(lse_h, (1, t // LANES)))
                dp = _dot(dom, v, 1, 1)
                ds = p * (dp - jnp.tile(delta_h, (1, t // LANES)))
                dsb = ds.astype(BF16)
                dv_s[...] += _dot(jnp.transpose(do_h).astype(BF16), p.astype(BF16), 1, 0)
                dk_s[...] += _dot(jnp.transpose(qm.astype(F32)).astype(BF16), dsb, 1, 0)
                dq_blk = dq_blk + _dot(dsb, km, 1, 0)
                df_s[hh] -= jnp.sum(ds, axis=0, keepdims=True)
                dr_blk = dr_blk + jnp.where(msk, jnp.sum(ds, axis=1, keepdims=True), 0.0)
            rows = pl.ds(pl.multiple_of(qi * t, t), t)

            @pl.when(ki == 0)
            def _():
                dq_ref[rows, :] = dq_blk
                dr_ref[rows, :] = dr_blk

            @pl.when(ki > 0)
            def _():
                dq_ref[rows, :] += dq_blk
                dr_ref[rows, :] += dr_blk

        @pl.when(qi > ki)
        def _():
            step(False)

        @pl.when(qi == ki)
        def _():
            step(True)

        @pl.when(qi == n - 1)
        def _():
            dk_ref[...] = jnp.transpose(dk_s[...])
            dv_ref[...] = jnp.transpose(dv_s[...])
            df_ref[0, 0:1, :] = df_s[0]
            df_ref[0, 1:2, :] = df_s[1]

    qrow = lambda h, j, i: (jnp.maximum(i, j), h)
    return pl.pallas_call(
        body, name=name,
        out_shape=(jax.ShapeDtypeStruct((S, D_ATT), F32), jax.ShapeDtypeStruct((S, D_ATT), F32),
                   jax.ShapeDtypeStruct((S, D_ATT), F32), jax.ShapeDtypeStruct((npair, 2, S), F32),
                   jax.ShapeDtypeStruct((S, D_ATT), F32)),
        grid=(npair, n, n),
        in_specs=[pl.BlockSpec((t, LANES), qrow),
                  pl.BlockSpec((t, LANES), lambda h, j, i: (j, npair + h)),
                  pl.BlockSpec((t, LANES), lambda h, j, i: (j, 2 * npair + h)),
                  pl.BlockSpec((1, 2, t), lambda h, j, i: (h, 0, j)),
                  pl.BlockSpec((t, LANES), qrow),
                  pl.BlockSpec((t, LANES), qrow),
                  pl.BlockSpec((t, LANES), qrow)],
        out_specs=(pl.BlockSpec((S, LANES), lambda h, j, i: (0, h)),
                   pl.BlockSpec((t, LANES), lambda h, j, i: (j, h)),
                   pl.BlockSpec((t, LANES), lambda h, j, i: (j, h)),
                   pl.BlockSpec((1, 2, t), lambda h, j, i: (h, 0, j)),
                   pl.BlockSpec((S, LANES), lambda h, j, i: (0, h))),
        scratch_shapes=[pltpu.VMEM((LANES, t), F32), pltpu.VMEM((LANES, t), F32), pltpu.VMEM((2, 1, t), F32)],
        compiler_params=_params('parallel', 'arbitrary', 'arbitrary'),
    )(qkv, qkv, qkv, f3, att, lse, d_mix)


A_COL = D_QKV // D_CONV
B_COL = A_COL + 1
P_COL = B_COL + 1


CONV_BLOCKS = D_CONV // LANES
CONV_GROUP = 8 * 8


def _rows8(ref, c, row):
    return ref.at[c][pl.ds(row, 8, stride=8), :]


def _put8(ref, c, row, val):
    ref.at[c][pl.ds(row, 8, stride=8), :] = val


def _lanes(c):
    return slice(c * LANES, (c + 1) * LANES)


def _glu_into(buf, a_ref, b_ref, ah_ref, bh_ref, first, ts):
    for c in range(CONV_BLOCKS):
        halo = ah_ref[:, _lanes(c)] * _sigmoid(bh_ref[:, _lanes(c)])
        buf[c, 0:CONV_HALO, :] = jnp.where(first, 0.0, halo)
        buf[c, CONV_HALO:CONV_HALO + ts, :] = a_ref[:, _lanes(c)] * _sigmoid(b_ref[:, _lanes(c)])


def _conv_taps(buf, c, r0):
    return [_rows8(buf, c, CONV_HALO + r0 + i - (CONV_WIDTH - 1)) for i in range(CONV_WIDTH - 1 + 8)]


def _dwconv8(xs, ws, bias):
    outs = []
    for j in range(8):
        acc = ws[0] * xs[j]
        for k in range(1, CONV_WIDTH):
            acc = acc + ws[k] * xs[j + k]
        outs.append(acc + bias)
    return outs


def _ln8(cs):
    inv = 1.0 / D_CONV
    mu = sum(jnp.sum(c, axis=1, keepdims=True) for c in cs) * inv
    xc = [c - mu for c in cs]
    rstd = lax.rsqrt(sum(jnp.sum(x * x, axis=1, keepdims=True) for x in xc) * inv + EPS)
    return [x * rstd for x in xc], rstd


def _conv_specs(ts, tmap):
    hb = ts // CONV_HALO
    cur = lambda c: pl.BlockSpec((ts, D_CONV), lambda i: (tmap(i), c))
    halo = lambda c: pl.BlockSpec((CONV_HALO, D_CONV), lambda i: (jnp.maximum(tmap(i) * hb - 1, 0), c))
    return cur, halo


def _conv_fwd(proj_a, mix, dw_w, dw_b, ln_g, ln_b, pw_w, l, *, name, ts=512):
    S = proj_a.shape[0]
    ts = _tile(S, ts, CONV_GROUP)

    def body(a_ref, b_ref, ah_ref, bh_ref, w_ref, wb_ref, g_ref, bb_ref, pw_ref, mix_in, o_ref, buf, stage):
        _glu_into(buf, a_ref, b_ref, ah_ref, bh_ref, pl.program_id(0) == 0, ts)
        ws = [[w_ref[k:k + 1, _lanes(c)] for k in range(CONV_WIDTH)] for c in range(CONV_BLOCKS)]
        for r0 in range(0, ts, CONV_GROUP):
            conv = [_dwconv8(_conv_taps(buf, c, r0), ws[c], wb_ref[:, _lanes(c)]) for c in range(CONV_BLOCKS)]
            for j in range(8):
                yhat, _ = _ln8([conv[c][j] for c in range(CONV_BLOCKS)])
                for c in range(CONV_BLOCKS):
                    y = yhat[c] * g_ref[:, _lanes(c)] + bb_ref[:, _lanes(c)]
                    _put8(stage, c, r0 + j, y * _sigmoid(y))
        hs = jnp.concatenate([stage[c] for c in range(CONV_BLOCKS)], axis=1)
        o_ref[...] = _dot(hs.astype(BF16), pw_ref[...], 1, 0).astype(BF16)

    cur, halo = _conv_specs(ts, lambda i: i)
    vec = pl.BlockSpec((1, D_CONV), lambda i: (0, 0))
    return pl.pallas_call(
        body, name=name, out_shape=jax.ShapeDtypeStruct(mix.shape, BF16), grid=(S // ts,),
        in_specs=[cur(A_COL), cur(B_COL), halo(A_COL), halo(B_COL),
                  pl.BlockSpec((None, CONV_HALO, D_CONV), lambda i: (l, 0, 0)), vec, vec, vec,
                  pl.BlockSpec((None, D_CONV, D_CONV), lambda i: (l, 0, 0)), ANY_SPEC],
        out_specs=pl.BlockSpec((ts, D_CONV), lambda i: (i, D_ATT // D_CONV)),
        scratch_shapes=[pltpu.VMEM((CONV_BLOCKS, CONV_HALO + ts, LANES), F32), pltpu.VMEM((CONV_BLOCKS, ts, LANES), F32)],
        input_output_aliases={9: 0},
        compiler_params=_params('parallel'),
    )(proj_a, proj_a, proj_a, proj_a, dw_w, dw_b, ln_g, ln_b, pw_w, mix)


def _conv_bwd(proj_a, d_mix, d_proj, dw_w, dw_b, ln_g, ln_b, pw_w, l, *, name, ts=512):
    S = proj_a.shape[0]
    ts = _tile(S, ts, CONV_GROUP)
    n = S // ts
    d_col = D_ATT // D_CONV
    groups = range(0, ts, CONV_GROUP)

    def body(a_ref, b_ref, ah_ref, bh_ref, dy_ref, w_ref, wb_ref, g_ref, bb_ref, pw_ref, dp_in,
             o_ref, dw_ref, dwb_ref, dg_ref, dbb_ref, dpw_ref, buf, dcbuf, stage, stage2):
        i = pl.program_id(0)
        _glu_into(buf, a_ref, b_ref, ah_ref, bh_ref, i == n - 1, ts)

        @pl.when(i == 0)
        def _():
            dcbuf[:, ts:ts + CONV_HALO, :] = jnp.zeros((CONV_BLOCKS, CONV_HALO, LANES), F32)
            dw_ref[...] = jnp.zeros(dw_ref.shape, F32)
            dwb_ref[...] = jnp.zeros(dwb_ref.shape, F32)
            dg_ref[...] = jnp.zeros(dg_ref.shape, F32)
            dbb_ref[...] = jnp.zeros(dbb_ref.shape, F32)
            dpw_ref[...] = jnp.zeros(dpw_ref.shape, F32)

        dout = dy_ref[...].astype(BF16)
        d_hs = _dot(dout, pw_ref[...], 1, 1)
        for c in range(CONV_BLOCKS):
            stage2[c, :, :] = d_hs[:, _lanes(c)]
        ws = [[w_ref[k:k + 1, _lanes(c)] for k in range(CONV_WIDTH)] for c in range(CONV_BLOCKS)]
        zero8 = jnp.zeros((8, LANES), F32)
        dg = [zero8] * CONV_BLOCKS
        dbb = [zero8] * CONV_BLOCKS
        dwb = [zero8] * CONV_BLOCKS
        for r0 in groups:
            conv = [_dwconv8(_conv_taps(buf, c, r0), ws[c], wb_ref[:, _lanes(c)]) for c in range(CONV_BLOCKS)]
            for j in range(8):
                yhat, rstd = _ln8([conv[c][j] for c in range(CONV_BLOCKS)])
                d_yhat = []
                for c in range(CONV_BLOCKS):
                    y = yhat[c] * g_ref[:, _lanes(c)] + bb_ref[:, _lanes(c)]
                    sg = _sigmoid(y)
                    _put8(stage, c, r0 + j, y * sg)
                    d_y = _rows8(stage2, c, r0 + j) * (sg * (1.0 + y * (1.0 - sg)))
                    dg[c] = dg[c] + d_y * yhat[c]
                    dbb[c] = dbb[c] + d_y
                    d_yhat.append(d_y * g_ref[:, _lanes(c)])
                inv = 1.0 / D_CONV
                m1 = sum(jnp.sum(d, axis=1, keepdims=True) for d in d_yhat) * inv
                m2 = sum(jnp.sum(d * yh, axis=1, keepdims=True) for d, yh in zip(d_yhat, yhat)) * inv
                for c in range(CONV_BLOCKS):
                    d_c = rstd * (d_yhat[c] - m1 - yhat[c] * m2)
                    dwb[c] = dwb[c] + d_c
                    _put8(dcbuf, c, r0 + j, d_c)
        for c in range(CONV_BLOCKS):
            dg_ref[:, _lanes(c)] += jnp.sum(dg[c], axis=0, keepdims=True)
            dbb_ref[:, _lanes(c)] += jnp.sum(dbb[c], axis=0, keepdims=True)
            dwb_ref[:, _lanes(c)] += jnp.sum(dwb[c], axis=0, keepdims=True)
        hs = jnp.concatenate([stage[c] for c in range(CONV_BLOCKS)], axis=1)
        dpw_ref[...] += _dot(hs.astype(BF16), dout, 0, 0)
        for c in range(CONV_BLOCKS):
            for r0 in groups:
                dcs = [_rows8(dcbuf, c, r0 + i_) for i_ in range(CONV_WIDTH - 1 + 8)]
                for j in range(8):
                    acc = ws[c][0] * dcs[j + CONV_WIDTH - 1]
                    for k in range(1, CONV_WIDTH):
                        acc = acc + ws[c][k] * dcs[j + CONV_WIDTH - 1 - k]
                    _put8(stage2, c, r0 + j, acc)
            for k in range(CONV_WIDTH):
                acc = zero8
                for r0 in groups:
                    for j in range(8):
                        acc = acc + _rows8(dcbuf, c, r0 + j) * _rows8(buf, c, CONV_HALO + r0 + j - (CONV_WIDTH - 1) + k)
                dw_ref[k:k + 1, _lanes(c)] += jnp.sum(acc, axis=0, keepdims=True)
            dcbuf[c, ts:ts + CONV_HALO, :] = dcbuf[c, 0:CONV_HALO, :]
        d_h = jnp.concatenate([stage2[c] for c in range(CONV_BLOCKS)], axis=1)
        a, sb = a_ref[...], _sigmoid(b_ref[...])
        o_ref[:, 0:D_CONV] = (d_h * sb).astype(BF16)
        o_ref[:, D_CONV:2 * D_CONV] = (d_h * a * sb * (1.0 - sb)).astype(BF16)

    rev = lambda i: n - 1 - i
    cur, halo = _conv_specs(ts, rev)
    vec = pl.BlockSpec((1, D_CONV), lambda i: (0, 0))
    wspec = pl.BlockSpec((CONV_HALO, D_CONV), lambda i: (0, 0))
    sq = pl.BlockSpec((D_CONV, D_CONV), lambda i: (0, 0))
    tile3 = pltpu.VMEM((CONV_BLOCKS, ts, LANES), F32)
    return pl.pallas_call(
        body, name=name,
        out_shape=(jax.ShapeDtypeStruct(d_proj.shape, BF16), jax.ShapeDtypeStruct((CONV_HALO, D_CONV), F32),
                   jax.ShapeDtypeStruct((1, D_CONV), F32), jax.ShapeDtypeStruct((1, D_CONV), F32),
                   jax.ShapeDtypeStruct((1, D_CONV), F32), jax.ShapeDtypeStruct((D_CONV, D_CONV), F32)),
        grid=(n,),
        in_specs=[cur(A_COL), cur(B_COL), halo(A_COL), halo(B_COL),
                  pl.BlockSpec((ts, D_CONV), lambda i: (rev(i), d_col)),
                  pl.BlockSpec((None, CONV_HALO, D_CONV), lambda i: (l, 0, 0)), vec, vec, vec,
                  pl.BlockSpec((None, D_CONV, D_CONV), lambda i: (l, 0, 0)), ANY_SPEC],
        out_specs=(pl.BlockSpec((ts, 2 * D_CONV), lambda i: (rev(i), D_QKV // (2 * D_CONV))), wspec, vec, vec, vec, sq),
        scratch_shapes=[pltpu.VMEM((CONV_BLOCKS, CONV_HALO + ts, LANES), F32),
                        pltpu.VMEM((CONV_BLOCKS, ts + CONV_HALO, LANES), F32), tile3, tile3],
        input_output_aliases={10: 0},
        compiler_params=_params('arbitrary'),
    )(proj_a, proj_a, proj_a, proj_a, d_mix, dw_w, dw_b, ln_g, ln_b, pw_w, d_proj)


POOL_BLOCKS = D_POOL // LANES
POOL_SPAN = max(POOL_WINDOWS) - 1


def _pool_sum8(xs, j, c, step):
    lo, hi = POOL_WINDOWS[2 * c], POOL_WINDOWS[2 * c + 1]
    acc = xs[j]
    for d in range(1, lo):
        acc = acc + xs[j + step * d]
    more = xs[j + step * lo]
    for d in range(lo + 1, hi):
        more = more + xs[j + step * d]
    lane = lax.broadcasted_iota(jnp.int32, (1, LANES), 1)
    return acc + jnp.where(lane >= POOL_GROUP, more, 0.0)


def _pool_count8(c, row):
    lane = lax.broadcasted_iota(jnp.int32, (1, LANES), 1)
    wl = jnp.where(lane >= POOL_GROUP, POOL_WINDOWS[2 * c + 1], POOL_WINDOWS[2 * c])
    pos = row + 8 * lax.broadcasted_iota(jnp.int32, (8, 1), 0)
    return jnp.minimum(pos + 1, wl).astype(F32)


def _pool_diff_into(stage, buf, u_ref, uh_ref, first, tile, ts):
    for c in range(POOL_BLOCKS):
        buf[c, 0:POOL_HALO, :] = jnp.where(first, 0.0, uh_ref[:, _lanes(c)])
        buf[c, POOL_HALO:POOL_HALO + ts, :] = u_ref[:, _lanes(c)]
        for r0 in range(0, ts, CONV_GROUP):
            xs = [_rows8(buf, c, POOL_HALO + r0 + i - POOL_SPAN) for i in range(POOL_SPAN + 8)]
            for j in range(8):
                mean = _pool_sum8(xs, j + POOL_SPAN, c, -1) / _pool_count8(c, tile * ts + r0 + j)
                _put8(stage, c, r0 + j, mean - xs[j + POOL_SPAN])


def _pool_specs(ts, tmap):
    hb = ts // POOL_HALO
    cur = pl.BlockSpec((ts, D_POOL), lambda i: (tmap(i), P_COL))
    halo = pl.BlockSpec((POOL_HALO, D_POOL), lambda i: (jnp.maximum(tmap(i) * hb - 1, 0), P_COL))
    return cur, halo


def _pool_fwd(proj_a, mix, wbd, scale, *, name, ts=512):
    S = proj_a.shape[0]
    ts = _tile(S, ts, CONV_GROUP)

    def body(u_ref, uh_ref, w_ref, s_ref, mix_in, o_ref, buf, stage):
        i = pl.program_id(0)
        _pool_diff_into(stage, buf, u_ref, uh_ref, i == 0, i, ts)
        d = jnp.concatenate([stage[c] for c in range(POOL_BLOCKS)], axis=1)
        o_ref[...] = (_dot(d.astype(BF16), w_ref[...], 1, 0) * s_ref[...]).astype(BF16)

    cur, halo = _pool_specs(ts, lambda i: i)
    return pl.pallas_call(
        body, name=name, out_shape=jax.ShapeDtypeStruct(mix.shape, BF16), grid=(S // ts,),
        in_specs=[cur, halo, pl.BlockSpec((D_POOL, D_POOL), lambda i: (0, 0)), pl.BlockSpec((1, D_POOL), lambda i: (0, 0)),
                  ANY_SPEC],
        out_specs=pl.BlockSpec((ts, D_POOL), lambda i: (i, (D_ATT + D_CONV) // D_POOL)),
        scratch_shapes=[pltpu.VMEM((POOL_BLOCKS, POOL_HALO + ts, LANES), F32), pltpu.VMEM((POOL_BLOCKS, ts, LANES), F32)],
        input_output_aliases={4: 0},
        compiler_params=_params('parallel'),
    )(proj_a, proj_a, wbd, scale, mix)


def _pool_bwd(proj_a, d_mix, d_proj, wbd, scale, *, name, ts=512):
    S = proj_a.shape[0]
    ts = _tile(S, ts, CONV_GROUP)
    n = S // ts
    d_col = (D_ATT + D_CONV) // D_POOL

    def body(u_ref, uh_ref, dy_ref, w_ref, s_ref, dp_in, o_ref, dw_ref, ds_ref, buf, ebuf, stage):
        i = pl.program_id(0)
        tile = n - 1 - i
        _pool_diff_into(stage, buf, u_ref, uh_ref, tile == 0, tile, ts)
        db = jnp.concatenate([stage[c] for c in range(POOL_BLOCKS)], axis=1).astype(BF16)
        ypre = _dot(db, w_ref[...], 1, 0)
        dout = dy_ref[...]
        d_y = (dout * s_ref[...]).astype(BF16)
        d_d = _dot(d_y, w_ref[...], 1, 1)

        @pl.when(i == 0)
        def _():
            ebuf[:, ts:ts + POOL_HALO, :] = jnp.zeros((POOL_BLOCKS, POOL_HALO, LANES), F32)
            dw_ref[...] = jnp.zeros(dw_ref.shape, F32)
            ds_ref[...] = jnp.zeros(ds_ref.shape, F32)

        dw_ref[...] += _dot(db, d_y, 0, 0)
        ds_ref[...] += jnp.sum(dout * ypre, axis=0, keepdims=True)
        for c in range(POOL_BLOCKS):
            stage[c, :, :] = d_d[:, _lanes(c)]
            for r0 in range(0, ts, CONV_GROUP):
                for j in range(8):
                    _put8(ebuf, c, r0 + j, _rows8(stage, c, r0 + j) / _pool_count8(c, tile * ts + r0 + j))
            for r0 in range(0, ts, CONV_GROUP):
                es = [_rows8(ebuf, c, r0 + i_) for i_ in range(POOL_SPAN + 8)]
                for j in range(8):
                    _put8(stage, c, r0 + j, _pool_sum8(es, j, c, 1) - _rows8(stage, c, r0 + j))
            ebuf[c, ts:ts + POOL_HALO, :] = ebuf[c, 0:POOL_HALO, :]
        o_ref[...] = jnp.concatenate([stage[c] for c in range(POOL_BLOCKS)], axis=1).astype(BF16)

    rev = lambda i: n - 1 - i
    cur, halo = _pool_specs(ts, rev)
    sq = pl.BlockSpec((D_POOL, D_POOL), lambda i: (0, 0))
    vec = pl.BlockSpec((1, D_POOL), lambda i: (0, 0))
    return pl.pallas_call(
        body, name=name,
        out_shape=(jax.ShapeDtypeStruct(d_proj.shape, BF16), jax.ShapeDtypeStruct((D_POOL, D_POOL), F32),
                   jax.ShapeDtypeStruct((1, D_POOL), F32)),
        grid=(n,),
        in_specs=[cur, halo, pl.BlockSpec((ts, D_POOL), lambda i: (rev(i), d_col)), sq, vec, ANY_SPEC],
        out_specs=(pl.BlockSpec((ts, D_POOL), lambda i: (rev(i), P_COL)), sq, vec),
        scratch_shapes=[pltpu.VMEM((POOL_BLOCKS, POOL_HALO + ts, LANES), F32),
                        pltpu.VMEM((POOL_BLOCKS, ts + POOL_HALO, LANES), F32), pltpu.VMEM((POOL_BLOCKS, ts, LANES), F32)],
        input_output_aliases={5: 0},
        compiler_params=_params('arbitrary'),
    )(proj_a, proj_a, d_mix, wbd, scale, d_proj)


FFN_LANES = 128
FFN_GROUP = 8 * 8


def _ffn_rows(ref, c, row0, j):
    return ref.at[c][pl.ds(row0 + j, 8, stride=8), :]


def _ffn_specs(ts, tc2, tmap, l):
    hb = ts // FFN_HALO
    cur = pl.BlockSpec((ts, tc2), lambda c, i: (tmap(i), c))
    halo = pl.BlockSpec((FFN_HALO, tc2), lambda c, i: (jnp.maximum(tmap(i) * hb - 1, 0), c))
    wspec = pl.BlockSpec((None, FFN_HALO, tc2), lambda c, i: (l, 0, c))
    return cur, halo, wspec


def _ffn_fill(buf, x_ref, xh_ref, first, ts, nblk):
    for c in range(nblk):
        cs = slice(c * FFN_LANES, (c + 1) * FFN_LANES)
        buf[c, 0:FFN_HALO, :] = jnp.where(first, 0.0, xh_ref[:, cs])
        buf[c, FFN_HALO:FFN_HALO + ts, :] = x_ref[:, cs]


def _ffn_conv_piece(buf, w_ref, r0, c):
    ws = [w_ref[k:k + 1, c * FFN_LANES:(c + 1) * FFN_LANES] for k in range(FFN_CONV_WIDTH)]
    xs = [_ffn_rows(buf, c, FFN_HALO + r0, j) for j in range(1 - FFN_CONV_WIDTH, 8)]
    outs = []
    for j in range(8):
        acc = ws[0] * xs[j]
        for k in range(1, FFN_CONV_WIDTH):
            acc = acc + ws[k] * xs[j + k]
        outs.append(acc)
    return outs, xs


def _ffn_act_fwd(up, w, l, *, name, ts=256):
    S, F2 = up.shape
    tc = F2 // 4
    nb = tc // FFN_LANES
    ts = _tile(S, ts, FFN_GROUP)

    def body(x_ref, xh_ref, w_ref, o_ref, buf, stage):
        _ffn_fill(buf, x_ref, xh_ref, pl.program_id(1) == 0, ts, 2 * nb)
        for c in range(nb):
            for r0 in range(0, ts, FFN_GROUP):
                gates, _ = _ffn_conv_piece(buf, w_ref, r0, c)
                vals, _ = _ffn_conv_piece(buf, w_ref, r0, nb + c)
                for j in range(8):
                    stage.at[c][pl.ds(r0 + j, 8, stride=8), :] = gates[j] * _sigmoid(gates[j]) * vals[j]
            o_ref[:, c * FFN_LANES:(c + 1) * FFN_LANES] = stage[c].astype(BF16)

    cur, halo, wspec = _ffn_specs(ts, 2 * tc, lambda i: i, l)
    return pl.pallas_call(
        body, name=name, out_shape=jax.ShapeDtypeStruct((S, F2 // 2), BF16), grid=(2, S // ts),
        in_specs=[cur, halo, wspec],
        out_specs=pl.BlockSpec((ts, tc), lambda c, i: (i, c)),
        scratch_shapes=[pltpu.VMEM((2 * nb, FFN_HALO + ts, FFN_LANES), F32), pltpu.VMEM((nb, ts, FFN_LANES), F32)],
        compiler_params=_params('parallel', 'parallel'),
    )(up, up, w)


def _ffn_act_bwd(up, d_act, w, l, *, name, ts=256):
    S, F2 = up.shape
    tc = F2 // 4
    nb = tc // FFN_LANES
    ts = _tile(S, ts, FFN_GROUP)
    n = S // ts

    def body(x_ref, xh_ref, da_ref, w_ref, o_ref, dw_ref, buf, dcbuf, stage):
        i = pl.program_id(1)
        _ffn_fill(buf, x_ref, xh_ref, i == n - 1, ts, 2 * nb)

        @pl.when(i == 0)
        def _():
            dcbuf[:, ts:ts + FFN_HALO, :] = jnp.zeros((2 * nb, FFN_HALO, FFN_LANES), F32)
            dw_ref[...] = jnp.zeros(dw_ref.shape, F32)

        for c in range(nb):
            blocks = (c, nb + c)
            stage[c, :, :] = da_ref[:, c * FFN_LANES:(c + 1) * FFN_LANES]
            dws = [[jnp.zeros((8, FFN_LANES), F32) for _ in range(FFN_CONV_WIDTH)] for _ in range(2)]
            for r0 in range(0, ts, FFN_GROUP):
                gates, xg = _ffn_conv_piece(buf, w_ref, r0, blocks[0])
                vals, xv = _ffn_conv_piece(buf, w_ref, r0, blocks[1])
                for j in range(8):
                    sg = _sigmoid(gates[j])
                    da = _ffn_rows(stage, c, r0, j)
                    d_cs = (da * vals[j] * (sg * (1.0 + gates[j] * (1.0 - sg))), da * (gates[j] * sg))
                    for half, (d_c, xs) in enumerate(zip(d_cs, (xg, xv))):
                        dcbuf.at[blocks[half]][pl.ds(r0 + j, 8, stride=8), :] = d_c
                        for k in range(FFN_CONV_WIDTH):
                            dws[half][k] = dws[half][k] + d_c * xs[j + k]
            for half in range(2):
                cs = slice(blocks[half] * FFN_LANES, (blocks[half] + 1) * FFN_LANES)
                for k in range(FFN_CONV_WIDTH):
                    dw_ref[k:k + 1, cs] += jnp.sum(dws[half][k], axis=0, keepdims=True)
            for b in blocks:
                cs = slice(b * FFN_LANES, (b + 1) * FFN_LANES)
                ws = [w_ref[k:k + 1, cs] for k in range(FFN_CONV_WIDTH)]
                for r0 in range(0, ts, FFN_GROUP):
                    ds = [_ffn_rows(dcbuf, b, r0, j) for j in range(8 + FFN_CONV_WIDTH - 1)]
                    for j in range(8):
                        d_x = ws[FFN_CONV_WIDTH - 1] * ds[j]
                        for k in range(FFN_CONV_WIDTH - 1):
                            d_x = d_x + ws[k] * ds[j + FFN_CONV_WIDTH - 1 - k]
                        stage.at[c][pl.ds(r0 + j, 8, stride=8), :] = d_x
                o_ref[:, cs] = stage[c].astype(BF16)
                dcbuf[b, ts:ts + FFN_HALO, :] = dcbuf[b, 0:FFN_HALO, :]

    rev = lambda i: n - 1 - i
    cur, halo, wspec = _ffn_specs(ts, 2 * tc, rev, l)
    return pl.pallas_call(
        body, name=name,
        out_shape=(jax.ShapeDtypeStruct((S, F2), BF16), jax.ShapeDtypeStruct((FFN_HALO, F2), F32)),
        grid=(2, n),
        in_specs=[cur, halo, pl.BlockSpec((ts, tc), lambda c, i: (rev(i), c)), wspec],
        out_specs=(cur, pl.BlockSpec((FFN_HALO, 2 * tc), lambda c, i: (0, c))),
        scratch_shapes=[pltpu.VMEM((2 * nb, FFN_HALO + ts, FFN_LANES), F32), pltpu.VMEM((2 * nb, ts + FFN_HALO, FFN_LANES), F32),
                        pltpu.VMEM((nb, ts, FFN_LANES), F32)],
        compiler_params=_params('parallel', 'arbitrary'),
    )(up, up, d_act, w)


def _loss_head(y, target, *, name, ts=512):
    S, D = y.shape
    ts = _tile(S, ts, 8)

    def body(y_ref, t_ref, l_ref, dy_ref):
        i = pl.program_id(0)
        err = y_ref[...] - t_ref[...]
        dy_ref[...] = err * (1.0 / D)
        part = jnp.sum(jnp.sum(err * err, axis=1, keepdims=True), axis=0, keepdims=True) * (0.5 / D)

        @pl.when(i == 0)
        def _():
            l_ref[...] = part

        @pl.when(i > 0)
        def _():
            l_ref[...] += part

    row = pl.BlockSpec((ts, D), lambda i: (i, 0))
    return pl.pallas_call(
        body, name=name,
        out_shape=(jax.ShapeDtypeStruct((1, 1), F32), jax.ShapeDtypeStruct((S, D), F32)),
        grid=(S // ts,), in_specs=[row, row], out_specs=(pl.BlockSpec((1, 1), lambda i: (0, 0)), row),
        compiler_params=_params('arbitrary'),
    )(y, target)


def _pair_cols(w):
    lead, f2 = w.shape[:-1], w.shape[-1]
    return w.reshape(lead + (2, 2, f2 // 4)).swapaxes(-3, -2).reshape(lead + (f2,))


def _pad_axis(w, size, axis):
    pad = [(0, 0)] * w.ndim
    pad[axis] = (0, size - w.shape[axis])
    return jnp.pad(w, pad)


def _block_diag(pool_w):
    g = pool_w.shape[0]
    rows = [jnp.concatenate([pool_w[i] if i == j else jnp.zeros_like(pool_w[i]) for j in range(g)], axis=1) for i in range(g)]
    return jnp.concatenate(rows, axis=0)


def _small_weights(w, l):
    return dict(
        norm1_g=w['norm1_g'][l][None, :],
        b_col=_pad_axis(w['b_f'][l][:, None], FG_ROWS, 0),
        qg=jnp.tile(w['q_norm_g'][l], N_HEADS)[None, :],
        kg=jnp.tile(w['k_norm_g'][l], N_HEADS)[None, :],
        dw_b=w['conv_dw_b'][l][None, :], ln_g=w['conv_ln_g'][l][None, :], ln_b=w['conv_ln_b'][l][None, :],
        wbd=_block_diag(w['pool_w'][l]).astype(BF16),
        pool_scale=w['pool_scale'][l][None, :],
        norm2_g=w['norm2_g'][l][None, :],
    )


def _layer_fwd(x, W, p, l):
    n = lambda s: f'l{l}_{s}'
    S = x.shape[0]
    h = _rms_fwd(x, p['norm1_g'], name=n('norm1'), after=W.started)
    proj_a = _mm(h, W.get('w_a', h), b_lead=0, name=n('proj_a'), tn=D_PROJ_A)
    z_raw = _mm(W.get('w_fg_t', h), h, a_lead=0, tb=True, name=n('proj_fg'))
    qkv = _qk_prep_fwd(proj_a, p['qg'], p['kg'], name=n('qk_norm'))
    f_cum = _forget_fwd(z_raw, p['b_col'], name=n('forget'))
    f3 = f_cum[:N_HEADS].reshape(N_HEADS // 2, 2, S)
    mix, att, lse = _attn_fwd(*_attn_aug(qkv, f_cum, name=n('attn_aug')), name=n('attn'))
    mix = _conv_fwd(proj_a, mix, W.get('dw_w', h), p['dw_b'], p['ln_g'], p['ln_b'], W.get('pw_w', h), 0, name=n('conv'))
    mix = _pool_fwd(proj_a, mix, p['wbd'], p['pool_scale'], name=n('pool'))
    x1 = _mm(mix, W.get('w_out', mix), b_lead=0, res=x, name=n('out_proj'), tn=1024)
    h2 = _rms_fwd(x1, p['norm2_g'], name=n('norm2'))
    up = _mm(h2, W.get('w_up', mix), b_lead=0, name=n('up_proj'), tn=1408, cols_outer=True)
    act = _ffn_act_fwd(up, W.get('ffn_w', h), 0, name=n('ffn_act'))
    x2 = _mm(act, W.get('w_down', mix), b_lead=0, res=x1, name=n('down_proj'), tn=1024, tk=2816)
    saved = dict(x=x, h=h, proj_a=proj_a, z_raw=z_raw, qkv=qkv, f3=f3, att=att, lse=lse, mix=mix, x1=x1, h2=h2, up=up, act=act)
    return x2, saved


def _layer_bwd(dx2, W, p, s, l, sink):
    n = lambda t: f'l{l}_{t}'
    S = dx2.shape[0]
    g = {}
    W = W.ready

    def large(key, a, b, **kw):
        return sink.put(l, key, *_mm(a, b, ta=True, copy16=True, name=n('d_' + key), **kw))

    d_act = _mm(dx2, W['w_down'], b_lead=0, tb=True, name=n('d_act'), tn=1408, cols_outer=True)
    large('w_down', s['act'], dx2, tm=1408, tn=1024)
    d_up, d_ffn_w = _ffn_act_bwd(s['up'], d_act, W['ffn_w'], 0, name=n('ffn_act_bwd'))
    g['ffn_dw_w'] = _pair_cols(d_ffn_w[:FFN_CONV_WIDTH])
    d_h2 = _mm(d_up, W['w_up'], b_lead=0, tb=True, name=n('d_h2'), tn=1024, tk=5632)
    started = large('w_up', s['h2'], d_up, tm=1024, tn=512, tk=4096)
    dx1, dg2 = _rms_bwd(s['x1'], p['norm2_g'], d_h2, dx2, name=n('norm2_bwd'))
    g['norm2_g'] = dg2[0]
    sink.point(l, 'mid', dx1)
    d_mix = _mm(dx1, W['w_out'], b_lead=0, tb=True, name=n('d_mix'), tn=1024, after=started)
    large('w_out', s['mix'], dx1, tm=1024, tn=1024)
    dq, dk, dv, df3, dr = _attn_bwd(s['qkv'], s['f3'], s['att'], s['lse'], d_mix, name=n('attn_bwd'))
    df = _pad_axis(df3.reshape(N_HEADS, S) + dr[:, ::HEAD_DIM].T, FG_ROWS, 0)
    d_z, d_b = _forget_bwd(s['z_raw'], p['b_col'], df, name=n('forget_bwd'))
    g['b_f'] = d_b[:N_HEADS, 0]
    d_proj, d_qg, d_kg = _qk_prep_bwd(s['proj_a'], dq, dk, dv, p['qg'], p['kg'], name=n('qk_norm_bwd'))
    g['q_norm_g'] = d_qg.reshape(N_HEADS, HEAD_DIM).sum(axis=0)
    g['k_norm_g'] = d_kg.reshape(N_HEADS, HEAD_DIM).sum(axis=0)
    d_proj, d_dw_w, d_dw_b, d_ln_g, d_ln_b, d_pw = _conv_bwd(
        s['proj_a'], d_mix, d_proj, W['dw_w'], p['dw_b'], p['ln_g'], p['ln_b'], W['pw_w'], 0, name=n('conv_bwd'))
    g['conv_dw_w'], g['conv_dw_b'] = d_dw_w[:CONV_WIDTH], d_dw_b[0]
    g['conv_ln_g'], g['conv_ln_b'], g['conv_pw_w'] = d_ln_g[0], d_ln_b[0], d_pw
    d_proj, d_wbd, d_scale = _pool_bwd(s['proj_a'], d_mix, d_proj, p['wbd'], p['pool_scale'], name=n('pool_bwd'))
    g['pool_w'] = jnp.stack([d_wbd[i * POOL_GROUP:(i + 1) * POOL_GROUP, i * POOL_GROUP:(i + 1) * POOL_GROUP]
                             for i in range(len(POOL_WINDOWS))])
    g['pool_scale'] = d_scale[0]
    d_w_a = _mm(s['h'], d_proj, ta=True, name=n('d_w_a'), tm=1024, tn=768, tk=4096)
    started = sink.put(l, 'w_in', d_w_a, _mm(d_z, s['h'], name=n('d_w_fg'), tn=1024).T)
    d_h_fg = _mm(d_z, W['w_fg_t'], b_lead=0, ta=True, name=n('d_h_fg'), tn=1024, after=started)
    d_h = _mm(d_proj, W['w_a'], b_lead=0, tb=True, res=d_h_fg, name=n('d_h'), tn=1024, tk=D_PROJ_A)
    dx, dg1 = _rms_bwd(s['x'], p['norm1_g'], d_h, dx1, name=n('norm1_bwd'))
    g['norm1_g'] = dg1[0]
    sink.point(l, 'end', dx)
    return dx, g


SMALL_GRADS = REPLICATED + ('conv_dw_w', 'conv_pw_w', 'ffn_dw_w')


def _local_step(x, target, W, w_small, sink):
    depth = w_small['norm1_g'].shape[0]
    ps, saved = [], []
    for l in range(depth):
        p = _small_weights(w_small, l)
        x, s = _layer_fwd(x, W[l], p, l)
        ps.append(p)
        saved.append(s)
    loss, dx = _loss_head(x, target, name='loss_head')
    small = [None] * depth
    for l in reversed(range(depth)):
        dx, small[l] = _layer_bwd(dx, W[l], ps[l], saved[l], l, sink)
    return loss, dx, {k: jnp.stack([small[l][k] for l in range(depth)]) for k in SMALL_GRADS}


W_IN_SHARD = D_IN // N_CHIPS
W_IN_PAD = 640
N_A_TILES = D_PROJ_A // LANES
FG_COL0 = D_QKV


def _a_tile_base(j):
    if j == N_A_TILES:
        return FG_COL0, N_HEADS
    return (j * LANES if j * LANES < FG_COL0 else j * LANES + N_HEADS), LANES


def _shift_select(rows, cols, shift, row_max, col_max):
    r = lax.broadcasted_iota(jnp.int32, (rows, cols), 0)
    c = lax.broadcasted_iota(jnp.int32, (rows, cols), 1)
    return ((r + shift == c) & (r < row_max) & (c < col_max)).astype(BF16)


def _select_w_in(raw, *, name, tm=256):
    _, D, _ = raw.shape
    tm = _tile(D, tm, 16)
    plan = []
    for j in range(N_A_TILES + 1):
        base, cmax = _a_tile_base(j)
        parts = []
        for p in range(N_CHIPS):
            delta = base - W_IN_SHARD * p
            lo, hi = max(0, delta), min(W_IN_SHARD - 1, delta + cmax - 1)
            if lo > hi:
                continue
            a0 = (lo // LANES) * LANES
            kw = min(-(-(hi + 1 - a0) // LANES) * LANES, W_IN_PAD - a0)
            parts.append((p, a0, kw, delta))
        plan.append((cmax, parts))

    def body(raw_ref, wa_ref, fg_ref):
        for j, (cmax, parts) in enumerate(plan):
            acc = None
            for p, a0, kw, delta in parts:
                sel = _shift_select(kw, LANES, a0 - delta, W_IN_SHARD - a0, cmax)
                t = _dot(raw_ref[p, :, a0:a0 + kw], sel, 1, 0)
                acc = t if acc is None else acc + t
            if j == N_A_TILES:
                fg_ref[...] = acc.astype(BF16)
            else:
                wa_ref[:, j * LANES:(j + 1) * LANES] = acc.astype(BF16)

    return pl.pallas_call(
        body, name=name,
        out_shape=(jax.ShapeDtypeStruct((D, D_PROJ_A), BF16), jax.ShapeDtypeStruct((D, LANES), BF16)),
        grid=(D // tm,),
        in_specs=[pl.BlockSpec((N_CHIPS, tm, W_IN_PAD), lambda i: (0, i, 0))],
        out_specs=(pl.BlockSpec((tm, D_PROJ_A), lambda i: (i, 0)), pl.BlockSpec((tm, LANES), lambda i: (i, 0))),
        compiler_params=_params('parallel'),
    )(raw)


def _select_w_in_grads(p_a, p_fg, *, name, tm=256):
    D = p_a.shape[0]
    tm = _tile(D, tm, 16)
    n_local = W_IN_PAD // LANES
    plan = []
    for p in range(N_CHIPS):
        for i in range(n_local):
            cmax = max(0, min(LANES, W_IN_SHARD - i * LANES))
            parts = []
            for j in range(N_A_TILES + 1):
                base, rmax = _a_tile_base(j)
                e = base - W_IN_SHARD * p - i * LANES
                if e + rmax - 1 < 0 or e > cmax - 1:
                    continue
                parts.append((j, e, rmax))
            plan.append((p, i, cmax, parts))

    def body(a_ref, fg_ref, o32_ref, o16_ref):
        terms = {}

        def src(j):
            if j not in terms:
                v = fg_ref[...] if j == N_A_TILES else a_ref[:, j * LANES:(j + 1) * LANES]
                terms[j] = _split3(v)
            return terms[j]

        for p, i, cmax, parts in plan:
            acc = jnp.zeros((tm, LANES), F32)
            for j, e, rmax in parts:
                sel = _shift_select(LANES, LANES, e, rmax, cmax)
                for term in src(j):
                    acc = acc + _dot(term, sel, 1, 0)
            o32_ref[p, :, i * LANES:(i + 1) * LANES] = acc
            o16_ref[p, :, i * LANES:(i + 1) * LANES] = acc.astype(BF16)

    out = pl.BlockSpec((N_CHIPS, tm, W_IN_PAD), lambda i: (0, i, 0))
    return pl.pallas_call(
        body, name=name,
        out_shape=(jax.ShapeDtypeStruct((N_CHIPS, D, W_IN_PAD), F32), jax.ShapeDtypeStruct((N_CHIPS, D, W_IN_PAD), BF16)),
        grid=(D // tm,),
        in_specs=[pl.BlockSpec((tm, D_PROJ_A), lambda i: (i, 0)), pl.BlockSpec((tm, LANES), lambda i: (i, 0))],
        out_specs=(out, out),
        compiler_params=_params('parallel'),
    )(p_a, p_fg)


MESH = pl.DeviceIdType.MESH
HBM_SPEC = pl.BlockSpec(memory_space=pltpu.HBM)


def _place():
    return lax.axis_index('x'), lax.axis_index('y'), lax.axis_index('c')


def _other_chips(x, y):
    return [(1 - x, y), (x, 1 - y), (1 - x, 1 - y)]


def _up_pos(q):
    return (q % 2) * 2 + q // 2


CHUNKS = {
    'w_in': ('lead', None),
    'w_up': ('cols', None),
    'w_down': ('rows', None),
    'w_out': ('rows', None),
    'conv_pw_w': ('rows', None),
    'conv_dw_w': ('lead', None),
    'ffn_dw_w': ('lead', None),
}


def _window(ref, kind, l, q):
    at = (lambda *idx: ref.at[idx]) if l is None else (lambda *idx: ref.at[(l,) + idx])
    shape = ref.shape if l is None else ref.shape[1:]
    if kind == 'lead':
        return at(q)
    if kind == 'rows':
        cs = shape[0] // N_CHIPS
        return at(pl.ds(pl.multiple_of(q * cs, 16), cs), slice(None))
    cs = shape[1] // N_CHIPS
    return at(slice(None), pl.ds(pl.multiple_of(_up_pos(q) * cs, LANES), cs))


def _place_shard(src, l, pos_arr, full_shape, kind, *, name, tm=256, after=None):
    _, m, n = src.shape
    bm = _tile(m, tm, 16) if kind != 'rows' else m
    extra = () if after is None else (after,)

    def body(pos_ref, s_ref, *rest):
        rest[-1][...] = s_ref[...].astype(BF16)

    if kind == 'lead':
        out = pl.BlockSpec((None, bm, n), lambda i, pos: (pos[0], i, 0))
    elif kind == 'rows':
        out = pl.BlockSpec((bm, n), lambda i, pos: (pos[0], 0))
    else:
        out = pl.BlockSpec((bm, n), lambda i, pos: (i, pos[0]))
    return pl.pallas_call(
        body, name=name, out_shape=jax.ShapeDtypeStruct(full_shape, BF16),
        grid_spec=pltpu.PrefetchScalarGridSpec(
            num_scalar_prefetch=1, grid=(m // bm,),
            in_specs=[pl.BlockSpec((None, bm, n), lambda i, pos: (l, i, 0))] + [ANY_SPEC] * len(extra), out_specs=out),
        compiler_params=_params('parallel'),
    )(pos_arr, src, *extra)


GATHERED = ('w_in', 'w_up', 'w_down', 'w_out', 'conv_pw_w', 'conv_dw_w', 'ffn_dw_w')
GATHER_GROUPS = ((0, ('w_in', 'conv_dw_w', 'ffn_dw_w', 'conv_pw_w')), (0, ('w_out', 'w_up', 'w_down')), (1, GATHERED))
SEM_SPEC = pl.BlockSpec(memory_space=pltpu.SEMAPHORE)
SPLIT_COPY_PARAMS = pltpu.CompilerParams(has_side_effects=pltpu.SideEffectType.DATAFLOW_SIDE_EFFECTING)


def _gather_start(tag, groups, bufs):
    flat = [b for group in bufs for b in group]
    nb = len(flat)

    def body(*refs):
        outs, sems, token = refs[nb:2 * nb], refs[2 * nb:-1], refs[-1]
        token[...] = jnp.zeros(token.shape, F32)
        x, y, c = _place()
        pos = 0
        for g, keys in enumerate(GATHER_GROUPS[n][1] for n in groups):
            for i, k in enumerate(keys):
                w = _window(outs[pos], CHUNKS[k][0], None, 2 * x + y)
                pos += 1
                for j, chip in enumerate(_other_chips(x, y)):
                    pltpu.make_async_remote_copy(src_ref=w, dst_ref=w, send_sem=sems[2 * g].at[3 * i + j],
                                                 recv_sem=sems[2 * g + 1].at[3 * i + j], device_id=(*chip, c),
                                                 device_id_type=MESH).start()

    sem_shapes = [pltpu.SemaphoreType.DMA((3 * len(GATHER_GROUPS[n][1]),)) for n in groups for _ in range(2)]
    res = pl.pallas_call(
        body, name=f'gather_start_{tag}',
        out_shape=tuple(jax.ShapeDtypeStruct(b.shape, b.dtype) for b in flat) + tuple(sem_shapes)
        + (jax.ShapeDtypeStruct((8, LANES), F32),),
        in_specs=[HBM_SPEC] * nb,
        out_specs=tuple([HBM_SPEC] * nb + [SEM_SPEC] * len(sem_shapes) + [pl.BlockSpec(memory_space=pltpu.VMEM)]),
        input_output_aliases={b: b for b in range(nb)},
        compiler_params=SPLIT_COPY_PARAMS,
    )(*[pltpu.with_memory_space_constraint(b, pltpu.HBM) for b in flat])
    out_bufs, sems, pos = [], res[nb:-1], 0
    for group in bufs:
        out_bufs.append(list(res[pos:pos + len(group)]))
        pos += len(group)
    return out_bufs, [(sems[2 * g], sems[2 * g + 1]) for g in range(len(groups))], res[-1]


def _gather_wait(g, bufs, sems, after):
    keys = GATHER_GROUPS[g][1]
    nb = len(bufs)

    def body(*refs):
        send_sems, recv_sems = refs[nb], refs[nb + 1]
        outs = refs[nb + 3:]
        x, y, c = _place()
        for i, k in enumerate(keys):
            mine = _window(outs[i], CHUNKS[k][0], None, 2 * x + y)
            for j, (cx, cy) in enumerate(_other_chips(x, y)):
                theirs = _window(outs[i], CHUNKS[k][0], None, 2 * cx + cy)
                cp = pltpu.make_async_remote_copy(src_ref=mine, dst_ref=theirs, send_sem=send_sems.at[3 * i + j],
                                                  recv_sem=recv_sems.at[3 * i + j], device_id=(cx, cy, c), device_id_type=MESH)
                cp.wait_send()
                cp.wait_recv()

    return pl.pallas_call(
        body, name=f'gather_wait_{g}',
        out_shape=tuple(jax.ShapeDtypeStruct(b.shape, b.dtype) for b in bufs),
        in_specs=[HBM_SPEC] * nb + [SEM_SPEC, SEM_SPEC, ANY_SPEC], out_specs=tuple([HBM_SPEC] * nb),
        input_output_aliases={b: b for b in range(nb)},
        compiler_params=SPLIT_COPY_PARAMS,
    )(*bufs, *sems, after)


def _rs_block(M, N):
    return (_tile(M, 256, 16), _tile(N, 2048))


def _chunk_shape(shape, kind):
    if kind == 'lead':
        return tuple(shape[1:])
    if kind == 'rows':
        return (shape[0] // N_CHIPS, shape[1])
    return (shape[0], shape[1] // N_CHIPS)


def _rs_start(tag, bufs, kinds):
    nb = len(bufs)
    lands = [lax.empty((N_CHIPS - 1,) + _chunk_shape(b.shape, k), b.dtype) for b, k in zip(bufs, kinds)]

    def body(*refs):
        src, land = refs[2 * nb:3 * nb], refs[3 * nb:4 * nb]
        send_sems, recv_sems, token = refs[4 * nb:]
        token[...] = jnp.zeros(token.shape, F32)
        x, y, c = _place()
        for b in range(nb):
            for j, (cx, cy) in enumerate(_other_chips(x, y)):
                pltpu.make_async_remote_copy(
                    src_ref=_window(src[b], kinds[b], None, 2 * cx + cy), dst_ref=land[b].at[j],
                    send_sem=send_sems.at[3 * b + j], recv_sem=recv_sems.at[3 * b + j],
                    device_id=(cx, cy, c), device_id_type=MESH).start()

    sem = pltpu.SemaphoreType.DMA((3 * nb,))
    res = pl.pallas_call(
        body, name=f'rs_start_{tag}',
        out_shape=tuple(jax.ShapeDtypeStruct(b.shape, b.dtype) for b in list(bufs) + lands)
        + (sem, sem, jax.ShapeDtypeStruct((8, LANES), F32)),
        in_specs=[HBM_SPEC] * (2 * nb),
        out_specs=tuple([HBM_SPEC] * (2 * nb) + [SEM_SPEC, SEM_SPEC, pl.BlockSpec(memory_space=pltpu.VMEM)]),
        input_output_aliases={b: b for b in range(2 * nb)},
        compiler_params=SPLIT_COPY_PARAMS,
    )(*[pltpu.with_memory_space_constraint(b, pltpu.HBM) for b in list(bufs) + lands])
    return res[:nb], res[nb:2 * nb], res[2 * nb:2 * nb + 2], res[2 * nb + 2]


def _rs_wait(tag, bufs, lands, sems, kinds, after):
    nb = len(bufs)

    def body(*refs):
        send_sems, recv_sems = refs[2 * nb], refs[2 * nb + 1]
        src, land = refs[2 * nb + 3:3 * nb + 3], refs[3 * nb + 3:]
        x, y, c = _place()
        for b in range(nb):
            for j, (cx, cy) in enumerate(_other_chips(x, y)):
                cp = pltpu.make_async_remote_copy(
                    src_ref=_window(src[b], kinds[b], None, 2 * cx + cy), dst_ref=land[b].at[j],
                    send_sem=send_sems.at[3 * b + j], recv_sem=recv_sems.at[3 * b + j],
                    device_id=(cx, cy, c), device_id_type=MESH)
                cp.wait_send()
                cp.wait_recv()

    res = pl.pallas_call(
        body, name=f'rs_wait_{tag}',
        out_shape=tuple(jax.ShapeDtypeStruct(b.shape, b.dtype) for b in list(bufs) + list(lands)),
        in_specs=[HBM_SPEC] * (2 * nb) + [SEM_SPEC, SEM_SPEC, ANY_SPEC], out_specs=tuple([HBM_SPEC] * (2 * nb)),
        input_output_aliases={b: b for b in range(2 * nb)},
        compiler_params=SPLIT_COPY_PARAMS,
    )(*bufs, *lands, *sems, after)
    return res[nb:]


def _rs_sum(p, rb, kind, pos_arr, l, depth, buf, *, name):
    m, n = rb.shape[1:]
    bm, bn = _rs_block(m, n)
    nbm, nbn = m // bm, n // bn
    has_buf = buf is not None

    def body(q_ref, p_ref, r_ref, *rest):
        acc = p_ref[...]
        for j in range(N_CHIPS - 1):
            acc = acc + r_ref[j].astype(F32)
        rest[-1][...] = acc

    if kind == 'lead':
        p_map = lambda i, j, q: (q[0], i, j)
    elif kind == 'rows':
        p_map = lambda i, j, q: (q[0] * nbm + i, j)
    else:
        p_map = lambda i, j, q: (i, q[0] * nbn + j)
    r_spec = pl.BlockSpec((N_CHIPS - 1, bm, bn), lambda i, j, q: (0, i, j))
    p_spec = pl.BlockSpec(((None,) if kind == 'lead' else ()) + (bm, bn), p_map)
    return pl.pallas_call(
        body, name=name, out_shape=jax.ShapeDtypeStruct((depth, m, n), F32),
        grid_spec=pltpu.PrefetchScalarGridSpec(
            num_scalar_prefetch=1, grid=(nbm, nbn), in_specs=[p_spec, r_spec] + ([ANY_SPEC] if has_buf else []),
            out_specs=pl.BlockSpec((None, bm, bn), lambda i, j, q: (l, i, j))),
        input_output_aliases={3: 0} if has_buf else {},
        compiler_params=_params('parallel', 'parallel'),
    )(pos_arr, p, rb, *((buf,) if has_buf else ()))


def _swap_with_sibling(bufs):
    nb = len(bufs)

    def body(*refs):
        ins, outs = refs[:nb], refs[nb:2 * nb]
        send_sems, recv_sems = refs[2 * nb:]
        x, y, c = _place()
        copies = [pltpu.make_async_remote_copy(src_ref=ins[b], dst_ref=outs[b], send_sem=send_sems.at[b],
                                               recv_sem=recv_sems.at[b], device_id=(x, y, 1 - c), device_id_type=MESH)
                  for b in range(nb)]
        for cp in copies:
            cp.start()
        for cp in copies:
            cp.wait()

    return pl.pallas_call(
        body, name='rs_swap_sums', out_shape=tuple(jax.ShapeDtypeStruct(b.shape, b.dtype) for b in bufs),
        in_specs=[HBM_SPEC] * nb, out_specs=tuple([HBM_SPEC] * nb),
        scratch_shapes=[pltpu.SemaphoreType.DMA((nb,)), pltpu.SemaphoreType.DMA((nb,))],
    )(*bufs)


def _all_reduce_small(v):
    r = v.shape[0]

    def body(x_ref, tot_ref, all_ref, send_sems, recv_sems):
        x, y, c = _place()
        me, sibling = (x, y, c), (x, y, 1 - c)
        chips = _other_chips(x, y)

        def rows(px, py, pc):
            return all_ref.at[pl.ds((4 * px + 2 * py + pc) * r, r), :]

        def copy(k, block, to, src=None):
            return pltpu.make_async_remote_copy(
                src_ref=rows(*block) if src is None else src, dst_ref=rows(*block),
                send_sem=send_sems.at[k], recv_sem=recv_sems.at[k], device_id=to, device_id_type=MESH)

        rows(*me)[...] = x_ref[...]
        first = [copy(0, me, sibling, src=x_ref)]
        first += [copy(1 + j, me, (*chip, c), src=x_ref) for j, chip in enumerate(chips)]
        for cp in first:
            cp.start()
        passed = [copy(4 + j, (*chip, c), sibling) for j, chip in enumerate(chips)]
        for j, chip in enumerate(chips):
            copy(1 + j, (*chip, c), me).wait_recv()
            passed[j].start()
        copy(0, sibling, me).wait_recv()
        for j, chip in enumerate(chips):
            copy(4 + j, (*chip, 1 - c), me).wait_recv()
        for cp in first + passed:
            cp.wait_send()
        acc = all_ref[0:r, :]
        for d in range(1, N_DEV):
            acc = acc + all_ref[d * r:(d + 1) * r, :]
        tot_ref[...] = acc

    return pl.pallas_call(
        body, name='all_reduce_small', out_shape=jax.ShapeDtypeStruct((r, LANES), F32),
        in_specs=[pl.BlockSpec(memory_space=pltpu.VMEM)], out_specs=pl.BlockSpec(memory_space=pltpu.VMEM),
        scratch_shapes=[pltpu.VMEM((N_DEV * r, LANES), F32), pltpu.SemaphoreType.DMA((7,)), pltpu.SemaphoreType.DMA((7,))],
    )(v)


def _adamw(w, g, m, v, *, name, g2=None, ts=256):
    R, C = w.shape
    Cg = g.shape[1]
    ts = _tile(R, ts, 8)
    c1 = 1.0 - ADAM_B1 ** ADAM_STEP
    c2 = 1.0 - ADAM_B2 ** ADAM_STEP
    two = g2 is not None

    def body(w_ref, g_ref, *rest):
        m_ref, v_ref, go_ref, d_ref, nm_ref, nv_ref = rest[two:]
        gv = g_ref[:, 0:C]
        if two:
            gv = gv + rest[0][:, 0:C]
        nm = ADAM_B1 * m_ref[...] + (1.0 - ADAM_B1) * gv
        nv = ADAM_B2 * v_ref[...] + (1.0 - ADAM_B2) * (gv * gv)
        d_ref[...] = -ADAM_LR * ((nm / c1) / (jnp.sqrt(nv / c2) + ADAM_EPS) + ADAM_WD * w_ref[...])
        go_ref[...] = gv
        nm_ref[...] = nm
        nv_ref[...] = nv

    blk = pl.BlockSpec((ts, C), lambda i: (i, 0))
    gblk = pl.BlockSpec((ts, Cg), lambda i: (i, 0))
    shape = jax.ShapeDtypeStruct((R, C), F32)
    return pl.pallas_call(body, name=name, out_shape=(shape, shape, shape, shape), grid=(R // ts,),
                          in_specs=[blk, gblk] + ([gblk] if two else []) + [blk, blk], out_specs=(blk, blk, blk, blk),
                          compiler_params=_params('parallel'))(w, g, *((g2,) if two else ()), m, v)


def _pack_rows(parts, row_unit):
    flat = jnp.concatenate(parts)
    flat = _pad_axis(flat, -(-flat.shape[0] // (row_unit * LANES)) * row_unit * LANES, 0)
    return flat.reshape(-1, LANES)


def _as_2d(a):
    return a.reshape(-1, a.shape[-1])


def _mesh_place():
    cx, cy, cc = _place()
    chip = 2 * cx + cy
    as_arr = lambda v: jnp.reshape(v, (1,)).astype(jnp.int32)
    return chip, as_arr(cc), as_arr(chip), as_arr(_up_pos(chip))


class _LayerWeights:
    def __init__(self, groups, started=None):
        self.groups = groups
        self.started = started
        self.ready = {}

    def get(self, name, after):
        if name not in self.ready:
            for names, wait in self.groups:
                if name in names:
                    self.ready.update({k: v[None] for k, v in wait(after).items()})
        return self.ready[name]


def _gather_full(w, place):
    chip, _, chip_arr, up_pos_arr = place
    L, D = w['w_in'].shape[:2]
    w_in_pad = _pad_axis(w['w_in'], W_IN_PAD, 2)

    def placed(k, l, after=None):
        if k == 'w_in':
            return _place_shard(w_in_pad, l, chip_arr, (N_CHIPS, D, W_IN_PAD), 'lead', name=f'place_w_in_{l}', after=after)
        if k == 'w_up':
            return _place_shard(w[k], l, up_pos_arr, (w[k].shape[1], N_CHIPS * w[k].shape[2]), 'cols', name=f'place_w_up_{l}',
                                after=after)
        if k in ('conv_dw_w', 'ffn_dw_w'):
            return lax.dynamic_update_slice_in_dim(jnp.zeros((N_CHIPS,) + w[k].shape[1:], F32), w[k][l][None], chip, axis=0)
        return _place_shard(w[k], l, chip_arr, (N_CHIPS * w[k].shape[1], w[k].shape[2]), 'rows', name=f'place_{k}_{l}',
                            after=after)

    first, rest = [0], list(range(1, len(GATHER_GROUPS)))
    bufs0, sems0, token = _gather_start('first', first, [[placed(k, GATHER_GROUPS[0][0]) for k in GATHER_GROUPS[0][1]]])
    bufs1, sems1, started = _gather_start('rest', rest, [[placed(k, GATHER_GROUPS[g][0], token) for k in GATHER_GROUPS[g][1]]
                                                         for g in rest])
    bufs, sems = bufs0 + bufs1, sems0 + sems1
    unchunk = lambda a: jnp.moveaxis(a, 0, 1).reshape(a.shape[1], -1)

    def waiter(g):
        l, keys = GATHER_GROUPS[g]

        def wait(after):
            full = dict(zip(keys, _gather_wait(g, bufs[g], sems[g], after)))
            out = {}
            if 'w_in' in full:
                out['w_a'], w_fg = _select_w_in(full['w_in'], name=f'select_w_in_{l}')
                out['w_fg_t'] = w_fg.T
            if 'conv_dw_w' in full:
                out['dw_w'] = _pad_axis(unchunk(full['conv_dw_w']), CONV_HALO, 0)
            if 'ffn_dw_w' in full:
                out['ffn_w'] = _pad_axis(_pair_cols(unchunk(full['ffn_dw_w'])), FFN_HALO, 0)
            if 'conv_pw_w' in full:
                out['pw_w'] = full['conv_pw_w']
            out.update({k: full[k] for k in ('w_up', 'w_down', 'w_out') if k in full})
            return out

        names = {'w_in': ('w_a', 'w_fg_t'), 'conv_dw_w': ('dw_w',), 'ffn_dw_w': ('ffn_w',), 'conv_pw_w': ('pw_w',)}
        return tuple(n for k in keys for n in names.get(k, (k,))), wait

    return [_LayerWeights([waiter(g) for g in range(len(GATHER_GROUPS)) if GATHER_GROUPS[g][0] == l],
                          started if l == 0 else None) for l in range(L)]


RS_WIRE = ('w_in', 'w_up', 'w_down', 'w_out')
RS_GROUPS = (('ffn', ('w_down', 'w_up')), ('mix', ('w_out', 'w_in')))


class _GradReducer:
    def __init__(self, place, depth):
        _, _, self.chip_arr, self.up_pos_arr = place
        self.depth = depth
        self.got = {}
        self.flying = {}
        self.sums = {}

    def put(self, l, key, g32, g16):
        if key == 'w_in':
            g32, g16 = _select_w_in_grads(g32, g16, name=f'l{l}_select_w_in_grads')
        self.got[(l, key)] = (g32, g16)
        for tag, keys in RS_GROUPS:
            if key == keys[-1]:
                kinds = [CHUNKS[k][0] for k in keys]
                bufs, lands, sems, token = _rs_start(f'l{l}_{tag}', [self.got[(l, k)][1] for k in keys], kinds)
                self.flying[(l, tag)] = (bufs, lands, sems, kinds)
                return token
        return None

    def point(self, l, where, after):
        if where == 'mid':
            self._land(l + 1, 'mix', after)
        else:
            self._land(l, 'ffn', after)

    def _land(self, l, tag, after):
        if (l, tag) not in self.flying:
            return
        bufs, lands, sems, kinds = self.flying.pop((l, tag))
        lands = _rs_wait(f'l{l}_{tag}', bufs, lands, sems, kinds, after)
        for k, rb, kind in zip(dict(RS_GROUPS)[tag], lands, kinds):
            pos = self.up_pos_arr if kind == 'cols' else self.chip_arr
            self.sums[k] = _rs_sum(self.got.pop((l, k))[0], rb, kind, pos, l, self.depth, self.sums.get(k), name=f'l{l}_rs_sum_{k}')

    def finish(self, after):
        for l, tag in list(self.flying):
            self._land(l, tag, after)
        mine = [self.sums[k] for k in RS_WIRE]
        return {k: pair for k, pair in zip(RS_WIRE, zip(mine, _swap_with_sibling(mine)))}


def kernel(x, norm1_g, w_in, b_f, q_norm_g, k_norm_g, conv_dw_w, conv_dw_b, conv_ln_g, conv_ln_b, conv_pw_w, pool_w, pool_scale, w_out, norm2_g, w_up, ffn_dw_w, w_down, loss_target, m_norm1_g, m_w_in, m_b_f, m_q_norm_g, m_k_norm_g, m_conv_dw_w, m_conv_dw_b, m_conv_ln_g, m_conv_ln_b, m_conv_pw_w, m_pool_w, m_pool_scale, m_w_out, m_norm2_g, m_w_up, m_ffn_dw_w, m_w_down, v_norm1_g, v_w_in, v_b_f, v_q_norm_g, v_k_norm_g, v_conv_dw_w, v_conv_dw_b, v_conv_ln_g, v_conv_ln_b, v_conv_pw_w, v_pool_w, v_pool_scale, v_w_out, v_norm2_g, v_w_up, v_ffn_dw_w, v_w_down):
    given = dict(locals())
    w = {k: given[k] for k in WEIGHTS}
    mom_m = {k: given['m_' + k] for k in WEIGHTS}
    mom_v = {k: given['v_' + k] for k in WEIGHTS}
    place = _mesh_place()
    chip = place[0]
    W = _gather_full(w, place)

    reducer = _GradReducer(place, norm1_g.shape[0])
    loss_part, grad_x, g_small = _local_step(x[0], loss_target[0], W, {k: w[k] for k in REPLICATED}, reducer)
    loss = lax.psum(loss_part[0, 0], ('x', 'y', 'c'))
    sums = reducer.finish(grad_x)

    small = _pack_rows([g_small[k].reshape(-1) for k in SMALL_GRADS], 8)
    small_sum = _all_reduce_small(small)

    g_sum, delta, new_m, new_v = {}, {}, {}, {}
    for k in RS_WIRE:
        outs = _adamw(_as_2d(w[k]), _as_2d(sums[k][0]), _as_2d(mom_m[k]), _as_2d(mom_v[k]), g2=_as_2d(sums[k][1]), name='adamw_' + k)
        g_sum[k], delta[k], new_m[k], new_v[k] = [o.reshape(w[k].shape) for o in outs]
    off = 0
    small_full = {}
    for k in SMALL_GRADS:
        small_full[k] = small_sum.reshape(-1)[off:off + g_small[k].size].reshape(g_small[k].shape)
        off += g_small[k].size
    small_g = {k: small_full[k] for k in REPLICATED}
    small_g['conv_dw_w'] = lax.dynamic_slice_in_dim(small_full['conv_dw_w'], chip * w['conv_dw_w'].shape[2], w['conv_dw_w'].shape[2], axis=2)
    small_g['conv_pw_w'] = lax.dynamic_slice_in_dim(small_full['conv_pw_w'], chip * w['conv_pw_w'].shape[1], w['conv_pw_w'].shape[1], axis=1)
    small_g['ffn_dw_w'] = lax.dynamic_slice_in_dim(small_full['ffn_dw_w'], chip * w['ffn_dw_w'].shape[2], w['ffn_dw_w'].shape[2], axis=2)
    pack_small = lambda t: _pack_rows([t[k].reshape(-1) for k in SMALL_GRADS], 256)
    outs = _adamw(pack_small(w), pack_small(small_g), pack_small(mom_m), pack_small(mom_v), name='adamw_small')
    off = 0
    for k in SMALL_GRADS:
        pieces = [o.reshape(-1)[off:off + w[k].size].reshape(w[k].shape) for o in outs]
        g_sum[k], delta[k], new_m[k], new_v[k] = pieces
        off += w[k].size

    return (loss, grad_x[None], *[g_sum[k] for k in WEIGHTS], *[delta[k] for k in WEIGHTS],
            *[new_m[k] for k in WEIGHTS], *[new_v[k] for k in WEIGHTS])
```

```python
import functools

import jax
import jax.numpy as jnp
from jax import lax
from jax.experimental import pallas as pl
from jax.experimental.pallas import tpu as pltpu

F32 = jnp.float32
BF16 = jnp.bfloat16

N_HEADS = 8
HEAD_DIM = 64
D_ATT = N_HEADS * HEAD_DIM
D_CONV = 256
D_POOL = 256
D_MIX = D_ATT + D_CONV + D_POOL
D_QKV = 3 * D_ATT
D_PROJ_A = D_QKV + 2 * D_CONV + D_POOL
D_IN = D_PROJ_A + N_HEADS
FG_ROWS = 128
CONV_WIDTH = 31
CONV_HALO = 32
POOL_WINDOWS = (2, 4, 8, 16)
POOL_GROUP = 64
POOL_HALO = 16
FFN_CONV_WIDTH = 3
FFN_HALO = 8
ATT_SCALE = HEAD_DIM ** -0.5
EPS = 1e-6
NEG = -1e30
LANES = 128

ADAM_LR = 0.001
ADAM_B1 = 0.9
ADAM_B2 = 0.999
ADAM_EPS = 1e-08
ADAM_WD = 0.01
ADAM_STEP = 10

N_CHIPS = 4
N_DEV = 8
VMEM_LIMIT_BYTES = 56 * 1024 * 1024

REPLICATED = ('norm1_g', 'b_f', 'q_norm_g', 'k_norm_g', 'conv_dw_b', 'conv_ln_g', 'conv_ln_b',
              'pool_w', 'pool_scale', 'norm2_g')
WEIGHTS = ('norm1_g', 'w_in', 'b_f', 'q_norm_g', 'k_norm_g', 'conv_dw_w', 'conv_dw_b', 'conv_ln_g',
           'conv_ln_b', 'conv_pw_w', 'pool_w', 'pool_scale', 'w_out', 'norm2_g', 'w_up', 'ffn_dw_w', 'w_down')


def _tile(dim, pref, unit=LANES):
    if dim <= pref:
        return dim
    t = (pref // unit) * unit
    while t >= unit:
        if dim % t == 0:
            return t
        t -= unit
    raise ValueError(f'no tile for {dim} (preferred {pref})')


def _params(*sem):
    return pltpu.CompilerParams(dimension_semantics=sem, vmem_limit_bytes=VMEM_LIMIT_BYTES)


def _sigmoid(x):
    return 1.0 / (1.0 + jnp.exp(-x))


def _dot(a, b, ca, cb):
    return lax.dot_general(a, b, (((ca,), (cb,)), ((), ())), preferred_element_type=F32)


def _split3(y):
    y1 = y.astype(BF16)
    r1 = y - y1.astype(F32)
    y2 = r1.astype(BF16)
    y3 = (r1 - y2.astype(F32)).astype(BF16)
    return y1, y2, y3


def _dot3(y, e, ca=1, cb=0):
    y1, y2, y3 = _split3(y)
    return _dot(y1, e, ca, cb) + _dot(y2, e, ca, cb) + _dot(y3, e, ca, cb)


def _lead(spec_shape, imap, lead):
    if lead is None:
        return pl.BlockSpec(spec_shape, imap)
    return pl.BlockSpec((None,) + spec_shape, lambda *g: (lead,) + imap(*g))


ANY_SPEC = pl.BlockSpec(memory_space=pl.ANY)


def _mm(a, b, *, name, ta=False, tb=False, res=None, out_dtype=F32, tm=512, tn=512, tk=1024,
        a_lead=None, b_lead=None, copy16=False, after=None, cols_outer=False):
    a2, b2 = a.shape[-2:], b.shape[-2:]
    K, M = a2 if ta else a2[::-1]
    N, Kb = b2 if tb else b2[::-1]
    assert K == Kb, (a.shape, b.shape)
    tm, tn, tk = _tile(M, tm), _tile(N, tn), _tile(K, tk)
    nk = K // tk
    ca = 0 if ta else 1
    cb = 1 if tb else 0
    has_res = res is not None
    n_in = 2 + has_res + (after is not None)
    n_out = 1 + copy16

    def body(*refs):
        a_ref, b_ref = refs[:2]
        r_ref = refs[2] if has_res else None
        o_refs = refs[n_in:n_in + n_out]
        scratch = refs[n_in + n_out:]

        def write(r):
            if has_res:
                r = r + r_ref[...]
            o_refs[0][...] = r.astype(out_dtype)
            if copy16:
                o_refs[1][...] = r.astype(BF16)

        p = _dot(a_ref[...].astype(BF16), b_ref[...].astype(BF16), ca, cb)
        if nk == 1:
            write(p)
        else:
            acc = scratch[0]
            k = pl.program_id(2)

            @pl.when(k == 0)
            def _():
                acc[...] = p

            @pl.when(k > 0)
            def _():
                acc[...] += p

            @pl.when(k == nk - 1)
            def _():
                write(acc[...])

    ij = (lambda g0, g1: (g1, g0)) if cols_outer else (lambda g0, g1: (g0, g1))
    at = lambda f: (lambda g0, g1, k: f(*ij(g0, g1), k))
    a_spec = _lead((tk, tm), at(lambda i, j, k: (k, i)), a_lead) if ta else _lead((tm, tk), at(lambda i, j, k: (i, k)), a_lead)
    b_spec = _lead((tn, tk), at(lambda i, j, k: (j, k)), b_lead) if tb else _lead((tk, tn), at(lambda i, j, k: (k, j)), b_lead)
    o_spec = pl.BlockSpec((tm, tn), at(lambda i, j, k: (i, j)))
    in_specs = [a_spec, b_spec] + ([o_spec] if has_res else []) + ([ANY_SPEC] if after is not None else [])
    args = (a, b) + ((res,) if has_res else ()) + ((after,) if after is not None else ())
    out_shape = [jax.ShapeDtypeStruct((M, N), out_dtype)] + ([jax.ShapeDtypeStruct((M, N), BF16)] if copy16 else [])
    out = pl.pallas_call(
        body, name=name,
        out_shape=tuple(out_shape),
        grid=ij(M // tm, N // tn) + (nk,),
        in_specs=in_specs, out_specs=tuple([o_spec] * n_out),
        scratch_shapes=[pltpu.VMEM((tm, tn), F32)] if nk > 1 else [],
        compiler_params=_params('parallel', 'parallel', 'arbitrary'),
    )(*args)
    return out if copy16 else out[0]


def _rms_fwd(x, g, *, name, ts=512, after=None):
    S, D = x.shape
    ts = _tile(S, ts, 8)

    def body(x_ref, g_ref, *rest):
        xv = x_ref[...]
        r = lax.rsqrt(jnp.mean(xv * xv, axis=-1, keepdims=True) + EPS)
        rest[-1][...] = (xv * r * g_ref[...]).astype(BF16)

    extra = () if after is None else (after,)
    return pl.pallas_call(
        body, name=name, out_shape=jax.ShapeDtypeStruct((S, D), BF16), grid=(S // ts,),
        in_specs=[pl.BlockSpec((ts, D), lambda i: (i, 0)), pl.BlockSpec((1, D), lambda i: (0, 0))] + [ANY_SPEC] * len(extra),
        out_specs=pl.BlockSpec((ts, D), lambda i: (i, 0)),
        compiler_params=_params('parallel'),
    )(x, g, *extra)


def _rms_bwd(x, g, dh, dres, *, name, ts=512):
    S, D = x.shape
    ts = _tile(S, ts, 8)

    def body(x_ref, g_ref, dh_ref, dr_ref, dx_ref, dg_ref):
        i = pl.program_id(0)
        xv = x_ref[...]
        r = lax.rsqrt(jnp.mean(xv * xv, axis=-1, keepdims=True) + EPS)
        y = xv * r
        dh_v = dh_ref[...]
        dy = dh_v * g_ref[...]
        dx_ref[...] = dr_ref[...] + r * (dy - y * jnp.mean(dy * y, axis=-1, keepdims=True))
        part = jnp.sum(dh_v * y, axis=0, keepdims=True)

        @pl.when(i == 0)
        def _():
            dg_ref[...] = part

        @pl.when(i > 0)
        def _():
            dg_ref[...] += part

    row = pl.BlockSpec((ts, D), lambda i: (i, 0))
    vec = pl.BlockSpec((1, D), lambda i: (0, 0))
    return pl.pallas_call(
        body, name=name,
        out_shape=(jax.ShapeDtypeStruct((S, D), F32), jax.ShapeDtypeStruct((1, D), F32)),
        grid=(S // ts,), in_specs=[row, vec, row, row], out_specs=(row, vec),
        compiler_params=_params('arbitrary'),
    )(x, g, dh, dres)


def _pair_ones():
    i = lax.broadcasted_iota(jnp.int32, (LANES, LANES), 0) // HEAD_DIM
    j = lax.broadcasted_iota(jnp.int32, (LANES, LANES), 1) // HEAD_DIM
    return (i == j).astype(BF16)


def _head_sums(y, e):
    return jnp.concatenate([_dot3(y[:, b * LANES:(b + 1) * LANES], e) for b in range(D_ATT // LANES)], axis=1)


def _qk_prep_fwd(proj_a, qg, kg, *, name, ts=512):
    S = proj_a.shape[0]
    ts = _tile(S, ts, 16)

    def body(q_ref, k_ref, v_ref, qg_ref, kg_ref, e_ref, o_ref):
        e = e_ref[...]

        def norm(xv, gain):
            ms = _head_sums(xv * xv, e) * (1.0 / HEAD_DIM)
            return xv * lax.rsqrt(ms + EPS) * gain

        o_ref[:, 0:D_ATT] = (norm(q_ref[...], qg_ref[...]) * ATT_SCALE).astype(BF16)
        o_ref[:, D_ATT:2 * D_ATT] = norm(k_ref[...], kg_ref[...]).astype(BF16)
        o_ref[:, 2 * D_ATT:3 * D_ATT] = v_ref[...].astype(BF16)

    col = lambda c: pl.BlockSpec((ts, D_ATT), lambda i: (i, c))
    vec = pl.BlockSpec((1, D_ATT), lambda i: (0, 0))
    return pl.pallas_call(
        body, name=name, out_shape=jax.ShapeDtypeStruct((S, D_QKV), BF16), grid=(S // ts,),
        in_specs=[col(0), col(1), col(2), vec, vec, pl.BlockSpec((LANES, LANES), lambda i: (0, 0))],
        out_specs=pl.BlockSpec((ts, D_QKV), lambda i: (i, 0)),
        compiler_params=_params('parallel'),
    )(proj_a, proj_a, proj_a, qg, kg, _pair_ones())


def _qk_prep_bwd(proj_a, dq, dk, dv, qg, kg, *, name, ts=512):
    S = proj_a.shape[0]
    ts = _tile(S, ts, 16)

    def body(q_ref, k_ref, dq_ref, dk_ref, dv_ref, qg_ref, kg_ref, e_ref, o_ref, dqg_ref, dkg_ref):
        i = pl.program_id(0)
        e = e_ref[...]

        def norm_bwd(xv, dn, gain, scale):
            ms = _head_sums(xv * xv, e) * (1.0 / HEAD_DIM)
            r = lax.rsqrt(ms + EPS)
            y = xv * r
            dy = dn * (gain * scale)
            mean = _head_sums(dy * y, e) * (1.0 / HEAD_DIM)
            return r * (dy - y * mean), jnp.sum(dn * y, axis=0, keepdims=True) * scale

        dq_raw, dqg = norm_bwd(q_ref[...], dq_ref[...], qg_ref[...], ATT_SCALE)
        dk_raw, dkg = norm_bwd(k_ref[...], dk_ref[...], kg_ref[...], 1.0)
        o_ref[:, 0:D_ATT] = dq_raw.astype(BF16)
        o_ref[:, D_ATT:2 * D_ATT] = dk_raw.astype(BF16)
        o_ref[:, 2 * D_ATT:3 * D_ATT] = dv_ref[...].astype(BF16)

        @pl.when(i == 0)
        def _():
            dqg_ref[...] = dqg
            dkg_ref[...] = dkg

        @pl.when(i > 0)
        def _():
            dqg_ref[...] += dqg
            dkg_ref[...] += dkg

    col = lambda c: pl.BlockSpec((ts, D_ATT), lambda i: (i, c))
    vec = pl.BlockSpec((1, D_ATT), lambda i: (0, 0))
    return pl.pallas_call(
        body, name=name,
        out_shape=(jax.ShapeDtypeStruct((S, D_PROJ_A), BF16), jax.ShapeDtypeStruct((1, D_ATT), F32),
                   jax.ShapeDtypeStruct((1, D_ATT), F32)),
        grid=(S // ts,),
        in_specs=[col(0), col(1), col(0), col(0), col(0), vec, vec, pl.BlockSpec((LANES, LANES), lambda i: (0, 0))],
        out_specs=(pl.BlockSpec((ts, D_QKV), lambda i: (i, 0)), vec, vec),
        compiler_params=_params('arbitrary'),
    )(proj_a, proj_a, dq, dk, dv, qg, kg, _pair_ones())


def _tri_ones(upper):
    i = lax.broadcasted_iota(jnp.int32, (LANES, LANES), 0)
    j = lax.broadcasted_iota(jnp.int32, (LANES, LANES), 1)
    return ((i <= j) if upper else (i >= j)).astype(BF16)


def _forget_fwd(z_raw, b_col, *, name):
    R, S = z_raw.shape
    nb = S // LANES

    def body(z_ref, b_ref, u_ref, f_ref):
        u = u_ref[...]
        carry = jnp.zeros((R, 1), F32)
        for j in range(nb):
            z = z_ref[:, j * LANES:(j + 1) * LANES] + b_ref[...]
            logf = jnp.minimum(z, 0.0) - jnp.log(1.0 + jnp.exp(-jnp.abs(z)))
            f_ref[:, j * LANES:(j + 1) * LANES] = _dot3(logf, u) + carry
            carry = carry + jnp.sum(logf, axis=1, keepdims=True)

    return pl.pallas_call(
        body, name=name, out_shape=jax.ShapeDtypeStruct((R, S), F32),
        compiler_params=pltpu.CompilerParams(vmem_limit_bytes=VMEM_LIMIT_BYTES),
    )(z_raw, b_col, _tri_ones(True))


def _forget_bwd(z_raw, b_col, df, *, name):
    R, S = z_raw.shape
    nb = S // LANES

    def body(z_ref, b_ref, df_ref, l_ref, dz_ref, db_ref):
        low = l_ref[...]
        carry = jnp.zeros((R, 1), F32)
        db = jnp.zeros((R, 1), F32)
        for j in reversed(range(nb)):
            d = df_ref[:, j * LANES:(j + 1) * LANES]
            dlogf = _dot3(d, low) + carry
            carry = carry + jnp.sum(d, axis=1, keepdims=True)
            z = z_ref[:, j * LANES:(j + 1) * LANES] + b_ref[...]
            dz = dlogf * _sigmoid(-z)
            dz_ref[:, j * LANES:(j + 1) * LANES] = dz
            db = db + jnp.sum(dz, axis=1, keepdims=True)
        db_ref[...] = db

    return pl.pallas_call(
        body, name=name,
        out_shape=(jax.ShapeDtypeStruct((R, S), F32), jax.ShapeDtypeStruct((R, 1), F32)),
        compiler_params=pltpu.CompilerParams(vmem_limit_bytes=VMEM_LIMIT_BYTES),
    )(z_raw, b_col, df, _tri_ones(False))


def _head_mask(hh):
    lane = lax.broadcasted_iota(jnp.int32, (1, LANES), 1)
    return (lane // HEAD_DIM) == hh


def _causal(s, qi, ki, t):
    rows = qi * t + lax.broadcasted_iota(jnp.int32, (t, t), 0)
    cols = ki * t + lax.broadcasted_iota(jnp.int32, (t, t), 1)
    return jnp.where(cols <= rows, s, NEG)


AUG = 2 * HEAD_DIM


def _aug_consts():
    i = lax.broadcasted_iota(jnp.int32, (D_ATT, N_HEADS * AUG), 0)
    j = lax.broadcasted_iota(jnp.int32, (D_ATT, N_HEADS * AUG), 1)
    spread = (j == (i // HEAD_DIM) * AUG + i % HEAD_DIM).astype(BF16)
    h = lax.broadcasted_iota(jnp.int32, (LANES, N_HEADS * AUG), 0)
    c = lax.broadcasted_iota(jnp.int32, (LANES, N_HEADS * AUG), 1)
    gate = [((c == h * AUG + HEAD_DIM + t) & (h < N_HEADS)).astype(BF16) for t in range(3)]
    lane = lax.broadcasted_iota(jnp.int32, (1, N_HEADS * AUG), 1) % AUG
    ones_q = ((lane >= HEAD_DIM) & (lane < HEAD_DIM + 3)).astype(F32)
    ones_v = (lane == HEAD_DIM).astype(F32)
    return spread, gate, ones_q, ones_v


def _attn_aug(qkv, f_cum, *, name, ts=512):
    S = qkv.shape[0]
    ts = _tile(S, ts)
    spread, gate, ones_q, ones_v = _aug_consts()
    W = N_HEADS * AUG

    def body(q_ref, k_ref, v_ref, f_ref, sp_ref, g0_ref, g1_ref, g2_ref, oq_ref, ov_ref, qa_ref, ka_ref, va_ref):
        sp = sp_ref[...]
        qa_ref[...] = (_dot(q_ref[...], sp, 1, 0) + oq_ref[...]).astype(BF16)
        va_ref[...] = (_dot(v_ref[...], sp, 1, 0) + ov_ref[...]).astype(BF16)
        terms = _split3(-jnp.transpose(f_ref[...]))
        ka = _dot(k_ref[...], sp, 1, 0)
        for t, g_ref in zip(terms, (g0_ref, g1_ref, g2_ref)):
            ka = ka + _dot(t, g_ref[...], 1, 0)
        ka_ref[...] = ka.astype(BF16)

    col = lambda c: pl.BlockSpec((ts, D_ATT), lambda i: (i, c))
    full = lambda a: pl.BlockSpec(a.shape, lambda i: (0, 0))
    out = pl.BlockSpec((ts, W), lambda i: (i, 0))
    shape = jax.ShapeDtypeStruct((S, W), BF16)
    consts = (spread, *gate, ones_q, ones_v)
    return pl.pallas_call(
        body, name=name, out_shape=(shape, shape, shape), grid=(S // ts,),
        in_specs=[col(0), col(1), col(2), pl.BlockSpec((FG_ROWS, ts), lambda i: (0, i))] + [full(a) for a in consts],
        out_specs=(out, out, out),
        compiler_params=_params('parallel'),
    )(qkv, qkv, qkv, f_cum, *consts)


def _attn_fwd(qa, ka, va, *, name, tq=1024, tk=1024):
    S = qa.shape[0]
    tq, tk = _tile(S, tq), _tile(S, tk)
    nq, nk = S // tq, S // tk
    npair = N_HEADS // 2

    def body(q_ref, k_ref, v_ref, mix_ref, o_ref, lse_ref, m_s, acc_s):
        qi, ki = pl.program_id(1), pl.program_id(2)
        last = (qi * tq + tq - 1) // tk
        first_masked = (qi * tq) // tk

        @pl.when(ki == 0)
        def _():
            m_s[...] = jnp.full(m_s.shape, NEG, F32)
            acc_s[...] = jnp.zeros(acc_s.shape, F32)

        def step(masked):
            if masked:
                rows = qi * tq + lax.broadcasted_iota(jnp.int32, (tq, tk), 0)
                cols = ki * tk + lax.broadcasted_iota(jnp.int32, (tq, tk), 1)
                keep = cols <= rows
            m_prev = [m_s[hh] for hh in range(2)]
            acc_prev = [acc_s[hh] for hh in range(2)]
            ss = []
            for hh in range(2):
                s = _dot(q_ref[:, hh * AUG:(hh + 1) * AUG], k_ref[:, hh * AUG:(hh + 1) * AUG], 1, 1)
                ss.append(jnp.where(keep, s, NEG) if masked else s)
            m_new = [jnp.maximum(m_prev[hh], jnp.max(ss[hh], axis=1, keepdims=True)) for hh in range(2)]
            ps = [jnp.exp(ss[hh] - jnp.tile(m_new[hh], (1, tk // LANES))).astype(BF16) for hh in range(2)]
            for hh in range(2):
                alpha = jnp.exp(m_prev[hh] - m_new[hh])
                acc_s[hh] = alpha * acc_prev[hh] + _dot(ps[hh], v_ref[:, hh * AUG:(hh + 1) * AUG], 1, 0)
                m_s[hh] = m_new[hh]

        @pl.when(ki < first_masked)
        def _():
            step(False)

        @pl.when((ki >= first_masked) & (ki <= last))
        def _():
            step(True)

        @pl.when(ki == last)
        def _():
            lane = lax.broadcasted_iota(jnp.int32, (1, LANES), 1)
            outs, lses = [], []
            for hh in range(2):
                acc = acc_s[hh]
                denom = jnp.sum(jnp.where(lane == HEAD_DIM, acc, 0.0), axis=1, keepdims=True)
                outs.append(acc / denom)
                lses.append(m_s[hh] + jnp.log(denom))
            o = jnp.where(lane < HEAD_DIM, outs[0], pltpu.roll(outs[1], HEAD_DIM, 1))
            o_ref[...] = o
            mix_ref[...] = o.astype(BF16)
            lse_ref[...] = jnp.where(lane < HEAD_DIM, lses[0], lses[1])

    def kmap(h, i, j):
        return (jnp.minimum(j, (i * tq + tq - 1) // tk), h)

    out = pl.BlockSpec((tq, LANES), lambda h, i, j: (i, h))
    return pl.pallas_call(
        body, name=name,
        out_shape=(jax.ShapeDtypeStruct((S, D_MIX), BF16), jax.ShapeDtypeStruct((S, D_ATT), F32),
                   jax.ShapeDtypeStruct((S, D_ATT), F32)),
        grid=(npair, nq, nk),
        in_specs=[pl.BlockSpec((tq, 2 * AUG), lambda h, i, j: (i, h)),
                  pl.BlockSpec((tk, 2 * AUG), kmap), pl.BlockSpec((tk, 2 * AUG), kmap)],
        out_specs=(out, out, out),
        scratch_shapes=[pltpu.VMEM((2, tq, LANES), F32), pltpu.VMEM((2, tq, LANES), F32)],
        compiler_params=_params('parallel', 'parallel', 'arbitrary'),
    )(qa, ka, va)


def _attn_bwd(qkv, f3, att, lse, d_mix, *, name, t=1024):
    S = qkv.shape[0]
    t = _tile(S, t)
    n = S // t
    npair = N_HEADS // 2

    def body(q_ref, k_ref, v_ref, f_ref, o_ref, lse_ref, do_ref, dq_ref, dk_ref, dv_ref, df_ref, dr_ref, dk_s, dv_s, df_s):
        ki, qi = pl.program_id(1), pl.program_id(2)

        @pl.when(qi == ki)
        def _():
            dk_s[...] = jnp.zeros(dk_s.shape, F32)
            dv_s[...] = jnp.zeros(dv_s.shape, F32)
            df_s[...] = jnp.zeros(df_s.shape, F32)

        def step(masked):
            q, k, v = q_ref[...], k_ref[...], v_ref[...]
            do, o, lse = do_ref[...], o_ref[...], lse_ref[...]
            lane = lax.broadcasted_iota(jnp.int32, (1, LANES), 1)
            lse_sw = pltpu.roll(lse, HEAD_DIM, 1)
            delta = _dot3(do.astype(BF16).astype(F32) * o, _pair_ones())
            delta_sw = pltpu.roll(delta, HEAD_DIM, 1)
            dq_blk = jnp.zeros((t, LANES), F32)
            dr_blk = jnp.zeros((t, LANES), F32)
            for hh in range(2):
                msk = _head_mask(hh)
                first = lane < HEAD_DIM if hh == 0 else lane >= HEAD_DIM
                qm = jnp.where(msk, q, jnp.zeros_like(q))
                km = jnp.where(msk, k, jnp.zeros_like(k))
                do_h = jnp.where(msk, do, 0.0)
                dom = do_h.astype(BF16)
                s = _dot(qm, k, 1, 1) - f_ref[0, hh:hh + 1, :]
                if masked:
                    s = _causal(s, qi, ki, t)
                lse_h = jnp.where(first, lse, lse_sw)
                delta_h = jnp.where(first, delta, delta_sw)
                p = jnp.exp(s - jnp.tile(lse_h, (1, t // LANES)))
                dp = _dot(dom, v, 1, 1)
                ds = p * (dp - jnp.tile(delta_h, (1, t // LANES)))
                dsb = ds.astype(BF16)
                dv_s[...] += _dot(jnp.transpose(do_h).astype(BF16), p.astype(BF16), 1, 0)
                dk_s[...] += _dot(jnp.transpose(qm.astype(F32)).astype(BF16), dsb, 1, 0)
                dq_blk = dq_blk + _dot(dsb, km, 1, 0)
                df_s[hh] -= jnp.sum(ds, axis=0, keepdims=True)
                dr_blk = dr_blk + jnp.where(msk, jnp.sum(ds, axis=1, keepdims=True), 0.0)
            rows = pl.ds(pl.multiple_of(qi * t, t), t)

            @pl.when(ki == 0)
            def _():
                dq_ref[rows, :] = dq_blk
                dr_ref[rows, :] = dr_blk

            @pl.when(ki > 0)
            def _():
                dq_ref[rows, :] += dq_blk
                dr_ref[rows, :] += dr_blk

        @pl.when(qi > ki)
        def _():
            step(False)

        @pl.when(qi == ki)
        def _():
            step(True)

        @pl.when(qi == n - 1)
        def _():
            dk_ref[...] = jnp.transpose(dk_s[...])
            dv_ref[...] = jnp.transpose(dv_s[...])
            df_ref[0, 0:1, :] = df_s[0]
            df_ref[0, 1:2, :] = df_s[1]

    qrow = lambda h, j, i: (jnp.maximum(i, j), h)
    return pl.pallas_call(
        body, name=name,
        out_shape=(jax.ShapeDtypeStruct((S, D_ATT), F32), jax.ShapeDtypeStruct((S, D_ATT), F32),
                   jax.ShapeDtypeStruct((S, D_ATT), F32), jax.ShapeDtypeStruct((npair, 2, S), F32),
                   jax.ShapeDtypeStruct((S, D_ATT), F32)),
        grid=(npair, n, n),
        in_specs=[pl.BlockSpec((t, LANES), qrow),
                  pl.BlockSpec((t, LANES), lambda h, j, i: (j, npair + h)),
                  pl.BlockSpec((t, LANES), lambda h, j, i: (j, 2 * npair + h)),
                  pl.BlockSpec((1, 2, t), lambda h, j, i: (h, 0, j)),
                  pl.BlockSpec((t, LANES), qrow),
                  pl.BlockSpec((t, LANES), qrow),
                  pl.BlockSpec((t, LANES), qrow)],
        out_specs=(pl.BlockSpec((S, LANES), lambda h, j, i: (0, h)),
                   pl.BlockSpec((t, LANES), lambda h, j, i: (j, h)),
                   pl.BlockSpec((t, LANES), lambda h, j, i: (j, h)),
                   pl.BlockSpec((1, 2, t), lambda h, j, i: (h, 0, j)),
                   pl.BlockSpec((S, LANES), lambda h, j, i: (0, h))),
        scratch_shapes=[pltpu.VMEM((LANES, t), F32), pltpu.VMEM((LANES, t), F32), pltpu.VMEM((2, 1, t), F32)],
        compiler_params=_params('parallel', 'arbitrary', 'arbitrary'),
    )(qkv, qkv, qkv, f3, att, lse, d_mix)


A_COL = D_QKV // D_CONV
B_COL = A_COL + 1
P_COL = B_COL + 1


CONV_BLOCKS = D_CONV // LANES
CONV_GROUP = 8 * 8


def _rows8(ref, c, row):
    return ref.at[c][pl.ds(row, 8, stride=8), :]


def _put8(ref, c, row, val):
    ref.at[c][pl.ds(row, 8, stride=8), :] = val


def _lanes(c):
    return slice(c * LANES, (c + 1) * LANES)


def _glu_into(buf, a_ref, b_ref, ah_ref, bh_ref, first, ts):
    for c in range(CONV_BLOCKS):
        halo = ah_ref[:, _lanes(c)] * _sigmoid(bh_ref[:, _lanes(c)])
        buf[c, 0:CONV_HALO, :] = jnp.where(first, 0.0, halo)
        buf[c, CONV_HALO:CONV_HALO + ts, :] = a_ref[:, _lanes(c)] * _sigmoid(b_ref[:, _lanes(c)])


def _conv_taps(buf, c, r0):
    return [_rows8(buf, c, CONV_HALO + r0 + i - (CONV_WIDTH - 1)) for i in range(CONV_WIDTH - 1 + 8)]


def _dwconv8(xs, ws, bias):
    outs = []
    for j in range(8):
        acc = ws[0] * xs[j]
        for k in range(1, CONV_WIDTH):
            acc = acc + ws[k] * xs[j + k]
        outs.append(acc + bias)
    return outs


def _ln8(cs):
    inv = 1.0 / D_CONV
    mu = sum(jnp.sum(c, axis=1, keepdims=True) for c in cs) * inv
    xc = [c - mu for c in cs]
    rstd = lax.rsqrt(sum(jnp.sum(x * x, axis=1, keepdims=True) for x in xc) * inv + EPS)
    return [x * rstd for x in xc], rstd


def _conv_specs(ts, tmap):
    hb = ts // CONV_HALO
    cur = lambda c: pl.BlockSpec((ts, D_CONV), lambda i: (tmap(i), c))
    halo = lambda c: pl.BlockSpec((CONV_HALO, D_CONV), lambda i: (jnp.maximum(tmap(i) * hb - 1, 0), c))
    return cur, halo


def _conv_fwd(proj_a, mix, dw_w, dw_b, ln_g, ln_b, pw_w, l, *, name, ts=512):
    S = proj_a.shape[0]
    ts = _tile(S, ts, CONV_GROUP)

    def body(a_ref, b_ref, ah_ref, bh_ref, w_ref, wb_ref, g_ref, bb_ref, pw_ref, mix_in, o_ref, buf, stage):
        _glu_into(buf, a_ref, b_ref, ah_ref, bh_ref, pl.program_id(0) == 0, ts)
        ws = [[w_ref[k:k + 1, _lanes(c)] for k in range(CONV_WIDTH)] for c in range(CONV_BLOCKS)]
        for r0 in range(0, ts, CONV_GROUP):
            conv = [_dwconv8(_conv_taps(buf, c, r0), ws[c], wb_ref[:, _lanes(c)]) for c in range(CONV_BLOCKS)]
            for j in range(8):
                yhat, _ = _ln8([conv[c][j] for c in range(CONV_BLOCKS)])
                for c in range(CONV_BLOCKS):
                    y = yhat[c] * g_ref[:, _lanes(c)] + bb_ref[:, _lanes(c)]
                    _put8(stage, c, r0 + j, y * _sigmoid(y))
        hs = jnp.concatenate([stage[c] for c in range(CONV_BLOCKS)], axis=1)
        o_ref[...] = _dot(hs.astype(BF16), pw_ref[...], 1, 0).astype(BF16)

    cur, halo = _conv_specs(ts, lambda i: i)
    vec = pl.BlockSpec((1, D_CONV), lambda i: (0, 0))
    return pl.pallas_call(
        body, name=name, out_shape=jax.ShapeDtypeStruct(mix.shape, BF16), grid=(S // ts,),
        in_specs=[cur(A_COL), cur(B_COL), halo(A_COL), halo(B_COL),
                  pl.BlockSpec((None, CONV_HALO, D_CONV), lambda i: (l, 0, 0)), vec, vec, vec,
                  pl.BlockSpec((None, D_CONV, D_CONV), lambda i: (l, 0, 0)), ANY_SPEC],
        out_specs=pl.BlockSpec((ts, D_CONV), lambda i: (i, D_ATT // D_CONV)),
        scratch_shapes=[pltpu.VMEM((CONV_BLOCKS, CONV_HALO + ts, LANES), F32), pltpu.VMEM((CONV_BLOCKS, ts, LANES), F32)],
        input_output_aliases={9: 0},
        compiler_params=_params('parallel'),
    )(proj_a, proj_a, proj_a, proj_a, dw_w, dw_b, ln_g, ln_b, pw_w, mix)


def _conv_bwd(proj_a, d_mix, d_proj, dw_w, dw_b, ln_g, ln_b, pw_w, l, *, name, ts=512):
    S = proj_a.shape[0]
    ts = _tile(S, ts, CONV_GROUP)
    n = S // ts
    d_col = D_ATT // D_CONV
    groups = range(0, ts, CONV_GROUP)

    def body(a_ref, b_ref, ah_ref, bh_ref, dy_ref, w_ref, wb_ref, g_ref, bb_ref, pw_ref, dp_in,
             o_ref, dw_ref, dwb_ref, dg_ref, dbb_ref, dpw_ref, buf, dcbuf, stage, stage2):
        i = pl.program_id(0)
        _glu_into(buf, a_ref, b_ref, ah_ref, bh_ref, i == n - 1, ts)

        @pl.when(i == 0)
        def _():
            dcbuf[:, ts:ts + CONV_HALO, :] = jnp.zeros((CONV_BLOCKS, CONV_HALO, LANES), F32)
            dw_ref[...] = jnp.zeros(dw_ref.shape, F32)
            dwb_ref[...] = jnp.zeros(dwb_ref.shape, F32)
            dg_ref[...] = jnp.zeros(dg_ref.shape, F32)
            dbb_ref[...] = jnp.zeros(dbb_ref.shape, F32)
            dpw_ref[...] = jnp.zeros(dpw_ref.shape, F32)

        dout = dy_ref[...].astype(BF16)
        d_hs = _dot(dout, pw_ref[...], 1, 1)
        for c in range(CONV_BLOCKS):
            stage2[c, :, :] = d_hs[:, _lanes(c)]
        ws = [[w_ref[k:k + 1, _lanes(c)] for k in range(CONV_WIDTH)] for c in range(CONV_BLOCKS)]
        zero8 = jnp.zeros((8, LANES), F32)
        dg = [zero8] * CONV_BLOCKS
        dbb = [zero8] * CONV_BLOCKS
        dwb = [zero8] * CONV_BLOCKS
        for r0 in groups:
            conv = [_dwconv8(_conv_taps(buf, c, r0), ws[c], wb_ref[:, _lanes(c)]) for c in range(CONV_BLOCKS)]
            for j in range(8):
                yhat, rstd = _ln8([conv[c][j] for c in range(CONV_BLOCKS)])
                d_yhat = []
                for c in range(CONV_BLOCKS):
                    y = yhat[c] * g_ref[:, _lanes(c)] + bb_ref[:, _lanes(c)]
                    sg = _sigmoid(y)
                    _put8(stage, c, r0 + j, y * sg)
                    d_y = _rows8(stage2, c, r0 + j) * (sg * (1.0 + y * (1.0 - sg)))
                    dg[c] = dg[c] + d_y * yhat[c]
                    dbb[c] = dbb[c] + d_y
                    d_yhat.append(d_y * g_ref[:, _lanes(c)])
                inv = 1.0 / D_CONV
                m1 = sum(jnp.sum(d, axis=1, keepdims=True) for d in d_yhat) * inv
                m2 = sum(jnp.sum(d * yh, axis=1, keepdims=True) for d, yh in zip(d_yhat, yhat)) * inv
                for c in range(CONV_BLOCKS):
                    d_c = rstd * (d_yhat[c] - m1 - yhat[c] * m2)
                    dwb[c] = dwb[c] + d_c
                    _put8(dcbuf, c, r0 + j, d_c)
        for c in range(CONV_BLOCKS):
            dg_ref[:, _lanes(c)] += jnp.sum(dg[c], axis=0, keepdims=True)
            dbb_ref[:, _lanes(c)] += jnp.sum(dbb[c], axis=0, keepdims=True)
            dwb_ref[:, _lanes(c)] += jnp.sum(dwb[c], axis=0, keepdims=True)
        hs = jnp.concatenate([stage[c] for c in range(CONV_BLOCKS)], axis=1)
        dpw_ref[...] += _dot(hs.astype(BF16), dout, 0, 0)
        for c in range(CONV_BLOCKS):
            for r0 in groups:
                dcs = [_rows8(dcbuf, c, r0 + i_) for i_ in range(CONV_WIDTH - 1 + 8)]
                for j in range(8):
                    acc = ws[c][0] * dcs[j + CONV_WIDTH - 1]
                    for k in range(1, CONV_WIDTH):
                        acc = acc + ws[c][k] * dcs[j + CONV_WIDTH - 1 - k]
                    _put8(stage2, c, r0 + j, acc)
            for k in range(CONV_WIDTH):
                acc = zero8
                for r0 in groups:
                    for j in range(8):
                        acc = acc + _rows8(dcbuf, c, r0 + j) * _rows8(buf, c, CONV_HALO + r0 + j - (CONV_WIDTH - 1) + k)
                dw_ref[k:k + 1, _lanes(c)] += jnp.sum(acc, axis=0, keepdims=True)
            dcbuf[c, ts:ts + CONV_HALO, :] = dcbuf[c, 0:CONV_HALO, :]
        d_h = jnp.concatenate([stage2[c] for c in range(CONV_BLOCKS)], axis=1)
        a, sb = a_ref[...], _sigmoid(b_ref[...])
        o_ref[:, 0:D_CONV] = (d_h * sb).astype(BF16)
        o_ref[:, D_CONV:2 * D_CONV] = (d_h * a * sb * (1.0 - sb)).astype(BF16)

    rev = lambda i: n - 1 - i
    cur, halo = _conv_specs(ts, rev)
    vec = pl.BlockSpec((1, D_CONV), lambda i: (0, 0))
    wspec = pl.BlockSpec((CONV_HALO, D_CONV), lambda i: (0, 0))
    sq = pl.BlockSpec((D_CONV, D_CONV), lambda i: (0, 0))
    tile3 = pltpu.VMEM((CONV_BLOCKS, ts, LANES), F32)
    return pl.pallas_call(
        body, name=name,
        out_shape=(jax.ShapeDtypeStruct(d_proj.shape, BF16), jax.ShapeDtypeStruct((CONV_HALO, D_CONV), F32),
                   jax.ShapeDtypeStruct((1, D_CONV), F32), jax.ShapeDtypeStruct((1, D_CONV), F32),
                   jax.ShapeDtypeStruct((1, D_CONV), F32), jax.ShapeDtypeStruct((D_CONV, D_CONV), F32)),
        grid=(n,),
        in_specs=[cur(A_COL), cur(B_COL), halo(A_COL), halo(B_COL),
                  pl.BlockSpec((ts, D_CONV), lambda i: (rev(i), d_col)),
                  pl.BlockSpec((None, CONV_HALO, D_CONV), lambda i: (l, 0, 0)), vec, vec, vec,
                  pl.BlockSpec((None, D_CONV, D_CONV), lambda i: (l, 0, 0)), ANY_SPEC],
        out_specs=(pl.BlockSpec((ts, 2 * D_CONV), lambda i: (rev(i), D_QKV // (2 * D_CONV))), wspec, vec, vec, vec, sq),
        scratch_shapes=[pltpu.VMEM((CONV_BLOCKS, CONV_HALO + ts, LANES), F32),
                        pltpu.VMEM((CONV_BLOCKS, ts + CONV_HALO, LANES), F32), tile3, tile3],
        input_output_aliases={10: 0},
        compiler_params=_params('arbitrary'),
    )(proj_a, proj_a, proj_a, proj_a, d_mix, dw_w, dw_b, ln_g, ln_b, pw_w, d_proj)


POOL_BLOCKS = D_POOL // LANES
POOL_SPAN = max(POOL_WINDOWS) - 1


def _pool_sum8(xs, j, c, step):
    lo, hi = POOL_WINDOWS[2 * c], POOL_WINDOWS[2 * c + 1]
    acc = xs[j]
    for d in range(1, lo):
        acc = acc + xs[j + step * d]
    more = xs[j + step * lo]
    for d in range(lo + 1, hi):
        more = more + xs[j + step * d]
    lane = lax.broadcasted_iota(jnp.int32, (1, LANES), 1)
    return acc + jnp.where(lane >= POOL_GROUP, more, 0.0)


def _pool_count8(c, row):
    lane = lax.broadcasted_iota(jnp.int32, (1, LANES), 1)
    wl = jnp.where(lane >= POOL_GROUP, POOL_WINDOWS[2 * c + 1], POOL_WINDOWS[2 * c])
    pos = row + 8 * lax.broadcasted_iota(jnp.int32, (8, 1), 0)
    return jnp.minimum(pos + 1, wl).astype(F32)


def _pool_diff_into(stage, buf, u_ref, uh_ref, first, tile, ts):
    for c in range(POOL_BLOCKS):
        buf[c, 0:POOL_HALO, :] = jnp.where(first, 0.0, uh_ref[:, _lanes(c)])
        buf[c, POOL_HALO:POOL_HALO + ts, :] = u_ref[:, _lanes(c)]
        for r0 in range(0, ts, CONV_GROUP):
            xs = [_rows8(buf, c, POOL_HALO + r0 + i - POOL_SPAN) for i in range(POOL_SPAN + 8)]
            for j in range(8):
                mean = _pool_sum8(xs, j + POOL_SPAN, c, -1) / _pool_count8(c, tile * ts + r0 + j)
                _put8(stage, c, r0 + j, mean - xs[j + POOL_SPAN])


def _pool_specs(ts, tmap):
    hb = ts // POOL_HALO
    cur = pl.BlockSpec((ts, D_POOL), lambda i: (tmap(i), P_COL))
    halo = pl.BlockSpec((POOL_HALO, D_POOL), lambda i: (jnp.maximum(tmap(i) * hb - 1, 0), P_COL))
    return cur, halo


def _pool_fwd(proj_a, mix, wbd, scale, *, name, ts=512):
    S = proj_a.shape[0]
    ts = _tile(S, ts, CONV_GROUP)

    def body(u_ref, uh_ref, w_ref, s_ref, mix_in, o_ref, buf, stage):
        i = pl.program_id(0)
        _pool_diff_into(stage, buf, u_ref, uh_ref, i == 0, i, ts)
        d = jnp.concatenate([stage[c] for c in range(POOL_BLOCKS)], axis=1)
        o_ref[...] = (_dot(d.astype(BF16), w_ref[...], 1, 0) * s_ref[...]).astype(BF16)

    cur, halo = _pool_specs(ts, lambda i: i)
    return pl.pallas_call(
        body, name=name, out_shape=jax.ShapeDtypeStruct(mix.shape, BF16), grid=(S // ts,),
        in_specs=[cur, halo, pl.BlockSpec((D_POOL, D_POOL), lambda i: (0, 0)), pl.BlockSpec((1, D_POOL), lambda i: (0, 0)),
                  ANY_SPEC],
        out_specs=pl.BlockSpec((ts, D_POOL), lambda i: (i, (D_ATT + D_CONV) // D_POOL)),
        scratch_shapes=[pltpu.VMEM((POOL_BLOCKS, POOL_HALO + ts, LANES), F32), pltpu.VMEM((POOL_BLOCKS, ts, LANES), F32)],
        input_output_aliases={4: 0},
        compiler_params=_params('parallel'),
    )(proj_a, proj_a, wbd, scale, mix)


def _pool_bwd(proj_a, d_mix, d_proj, wbd, scale, *, name, ts=512):
    S = proj_a.shape[0]
    ts = _tile(S, ts, CONV_GROUP)
    n = S // ts
    d_col = (D_ATT + D_CONV) // D_POOL

    def body(u_ref, uh_ref, dy_ref, w_ref, s_ref, dp_in, o_ref, dw_ref, ds_ref, buf, ebuf, stage):
        i = pl.program_id(0)
        tile = n - 1 - i
        _pool_diff_into(stage, buf, u_ref, uh_ref, tile == 0, tile, ts)
        db = jnp.concatenate([stage[c] for c in range(POOL_BLOCKS)], axis=1).astype(BF16)
        ypre = _dot(db, w_ref[...], 1, 0)
        dout = dy_ref[...]
        d_y = (dout * s_ref[...]).astype(BF16)
        d_d = _dot(d_y, w_ref[...], 1, 1)

        @pl.when(i == 0)
        def _():
            ebuf[:, ts:ts + POOL_HALO, :] = jnp.zeros((POOL_BLOCKS, POOL_HALO, LANES), F32)
            dw_ref[...] = jnp.zeros(dw_ref.shape, F32)
            ds_ref[...] = jnp.zeros(ds_ref.shape, F32)

        dw_ref[...] += _dot(db, d_y, 0, 0)
        ds_ref[...] += jnp.sum(dout * ypre, axis=0, keepdims=True)
        for c in range(POOL_BLOCKS):
            stage[c, :, :] = d_d[:, _lanes(c)]
            for r0 in range(0, ts, CONV_GROUP):
                for j in range(8):
                    _put8(ebuf, c, r0 + j, _rows8(stage, c, r0 + j) / _pool_count8(c, tile * ts + r0 + j))
            for r0 in range(0, ts, CONV_GROUP):
                es = [_rows8(ebuf, c, r0 + i_) for i_ in range(POOL_SPAN + 8)]
                for j in range(8):
                    _put8(stage, c, r0 + j, _pool_sum8(es, j, c, 1) - _rows8(stage, c, r0 + j))
            ebuf[c, ts:ts + POOL_HALO, :] = ebuf[c, 0:POOL_HALO, :]
        o_ref[...] = jnp.concatenate([stage[c] for c in range(POOL_BLOCKS)], axis=1).astype(BF16)

    rev = lambda i: n - 1 - i
    cur, halo = _pool_specs(ts, rev)
    sq = pl.BlockSpec((D_POOL, D_POOL), lambda i: (0, 0))
    vec = pl.BlockSpec((1, D_POOL), lambda i: (0, 0))
    return pl.pallas_call(
        body, name=name,
        out_shape=(jax.ShapeDtypeStruct(d_proj.shape, BF16), jax.ShapeDtypeStruct((D_POOL, D_POOL), F32),
                   jax.ShapeDtypeStruct((1, D_POOL), F32)),
        grid=(n,),
        in_specs=[cur, halo, pl.BlockSpec((ts, D_POOL), lambda i: (rev(i), d_col)), sq, vec, ANY_SPEC],
        out_specs=(pl.BlockSpec((ts, D_POOL), lambda i: (rev(i), P_COL)), sq, vec),
        scratch_shapes=[pltpu.VMEM((POOL_BLOCKS, POOL_HALO + ts, LANES), F32),
                        pltpu.VMEM((POOL_BLOCKS, ts + POOL_HALO, LANES), F32), pltpu.VMEM((POOL_BLOCKS, ts, LANES), F32)],
        input_output_aliases={5: 0},
        compiler_params=_params('arbitrary'),
    )(proj_a, proj_a, d_mix, wbd, scale, d_proj)


FFN_LANES = 128
FFN_GROUP = 8 * 8


def _ffn_rows(ref, c, row0, j):
    return ref.at[c][pl.ds(row0 + j, 8, stride=8), :]


def _ffn_specs(ts, tc2, tmap, l):
    hb = ts // FFN_HALO
    cur = pl.BlockSpec((ts, tc2), lambda c, i: (tmap(i), c))
    halo = pl.BlockSpec((FFN_HALO, tc2), lambda c, i: (jnp.maximum(tmap(i) * hb - 1, 0), c))
    wspec = pl.BlockSpec((None, FFN_HALO, tc2), lambda c, i: (l, 0, c))
    return cur, halo, wspec


def _ffn_fill(buf, x_ref, xh_ref, first, ts, nblk):
    for c in range(nblk):
        cs = slice(c * FFN_LANES, (c + 1) * FFN_LANES)
        buf[c, 0:FFN_HALO, :] = jnp.where(first, 0.0, xh_ref[:, cs])
        buf[c, FFN_HALO:FFN_HALO + ts, :] = x_ref[:, cs]


def _ffn_conv_piece(buf, w_ref, r0, c):
    ws = [w_ref[k:k + 1, c * FFN_LANES:(c + 1) * FFN_LANES] for k in range(FFN_CONV_WIDTH)]
    xs = [_ffn_rows(buf, c, FFN_HALO + r0, j) for j in range(1 - FFN_CONV_WIDTH, 8)]
    outs = []
    for j in range(8):
        acc = ws[0] * xs[j]
        for k in range(1, FFN_CONV_WIDTH):
            acc = acc + ws[k] * xs[j + k]
        outs.append(acc)
    return outs, xs


def _ffn_act_fwd(up, w, l, *, name, ts=256):
    S, F2 = up.shape
    tc = F2 // 4
    nb = tc // FFN_LANES
    ts = _tile(S, ts, FFN_GROUP)

    def body(x_ref, xh_ref, w_ref, o_ref, buf, stage):
        _ffn_fill(buf, x_ref, xh_ref, pl.program_id(1) == 0, ts, 2 * nb)
        for c in range(nb):
            for r0 in range(0, ts, FFN_GROUP):
                gates, _ = _ffn_conv_piece(buf, w_ref, r0, c)
                vals, _ = _ffn_conv_piece(buf, w_ref, r0, nb + c)
                for j in range(8):
                    stage.at[c][pl.ds(r0 + j, 8, stride=8), :] = gates[j] * _sigmoid(gates[j]) * vals[j]
            o_ref[:, c * FFN_LANES:(c + 1) * FFN_LANES] = stage[c].astype(BF16)

    cur, halo, wspec = _ffn_specs(ts, 2 * tc, lambda i: i, l)
    return pl.pallas_call(
        body, name=name, out_shape=jax.ShapeDtypeStruct((S, F2 // 2), BF16), grid=(2, S // ts),
        in_specs=[cur, halo, wspec],
        out_specs=pl.BlockSpec((ts, tc), lambda c, i: (i, c)),
        scratch_shapes=[pltpu.VMEM((2 * nb, FFN_HALO + ts, FFN_LANES), F32), pltpu.VMEM((nb, ts, FFN_LANES), F32)],
        compiler_params=_params('parallel', 'parallel'),
    )(up, up, w)


def _ffn_act_bwd(up, d_act, w, l, *, name, ts=256):
    S, F2 = up.shape
    tc = F2 // 4
    nb = tc // FFN_LANES
    ts = _tile(S, ts, FFN_GROUP)
    n = S // ts

    def body(x_ref, xh_ref, da_ref, w_ref, o_ref, dw_ref, buf, dcbuf, stage):
        i = pl.program_id(1)
        _ffn_fill(buf, x_ref, xh_ref, i == n - 1, ts, 2 * nb)

        @pl.when(i == 0)
        def _():
            dcbuf[:, ts:ts + FFN_HALO, :] = jnp.zeros((2 * nb, FFN_HALO, FFN_LANES), F32)
            dw_ref[...] = jnp.zeros(dw_ref.shape, F32)

        for c in range(nb):
            blocks = (c, nb + c)
            stage[c, :, :] = da_ref[:, c * FFN_LANES:(c + 1) * FFN_LANES]
            dws = [[jnp.zeros((8, FFN_LANES), F32) for _ in range(FFN_CONV_WIDTH)] for _ in range(2)]
            for r0 in range(0, ts, FFN_GROUP):
                gates, xg = _ffn_conv_piece(buf, w_ref, r0, blocks[0])
                vals, xv = _ffn_conv_piece(buf, w_ref, r0, blocks[1])
                for j in range(8):
                    sg = _sigmoid(gates[j])
                    da = _ffn_rows(stage, c, r0, j)
                    d_cs = (da * vals[j] * (sg * (1.0 + gates[j] * (1.0 - sg))), da * (gates[j] * sg))
                    for half, (d_c, xs) in enumerate(zip(d_cs, (xg, xv))):
                        dcbuf.at[blocks[half]][pl.ds(r0 + j, 8, stride=8), :] = d_c
                        for k in range(FFN_CONV_WIDTH):
                            dws[half][k] = dws[half][k] + d_c * xs[j + k]
            for half in range(2):
                cs = slice(blocks[half] * FFN_LANES, (blocks[half] + 1) * FFN_LANES)
                for k in range(FFN_CONV_WIDTH):
                    dw_ref[k:k + 1, cs] += jnp.sum(dws[half][k], axis=0, keepdims=True)
            for b in blocks:
                cs = slice(b * FFN_LANES, (b + 1) * FFN_LANES)
                ws = [w_ref[k:k + 1, cs] for k in range(FFN_CONV_WIDTH)]
                for r0 in range(0, ts, FFN_GROUP):
                    ds = [_ffn_rows(dcbuf, b, r0, j) for j in range(8 + FFN_CONV_WIDTH - 1)]
                    for j in range(8):
                        d_x = ws[FFN_CONV_WIDTH - 1] * ds[j]
                        for k in range(FFN_CONV_WIDTH - 1):
                            d_x = d_x + ws[k] * ds[j + FFN_CONV_WIDTH - 1 - k]
                        stage.at[c][pl.ds(r0 + j, 8, stride=8), :] = d_x
                o_ref[:, cs] = stage[c].astype(BF16)
                dcbuf[b, ts:ts + FFN_HALO, :] = dcbuf[b, 0:FFN_HALO, :]

    rev = lambda i: n - 1 - i
    cur, halo, wspec = _ffn_specs(ts, 2 * tc, rev, l)
    return pl.pallas_call(
        body, name=name,
        out_shape=(jax.ShapeDtypeStruct((S, F2), BF16), jax.ShapeDtypeStruct((FFN_HALO, F2), F32)),
        grid=(2, n),
        in_specs=[cur, halo, pl.BlockSpec((ts, tc), lambda c, i: (rev(i), c)), wspec],
        out_specs=(cur, pl.BlockSpec((FFN_HALO, 2 * tc), lambda c, i: (0, c))),
        scratch_shapes=[pltpu.VMEM((2 * nb, FFN_HALO + ts, FFN_LANES), F32), pltpu.VMEM((2 * nb, ts + FFN_HALO, FFN_LANES), F32),
                        pltpu.VMEM((nb, ts, FFN_LANES), F32)],
        compiler_params=_params('parallel', 'arbitrary'),
    )(up, up, d_act, w)


def _loss_head(y, target, *, name, ts=512):
    S, D = y.shape
    ts = _tile(S, ts, 8)

    def body(y_ref, t_ref, l_ref, dy_ref):
        i = pl.program_id(0)
        err = y_ref[...] - t_ref[...]
        dy_ref[...] = err * (1.0 / D)
        part = jnp.sum(jnp.sum(err * err, axis=1, keepdims=True), axis=0, keepdims=True) * (0.5 / D)

        @pl.when(i == 0)
        def _():
            l_ref[...] = part

        @pl.when(i > 0)
        def _():
            l_ref[...] += part

    row = pl.BlockSpec((ts, D), lambda i: (i, 0))
    return pl.pallas_call(
        body, name=name,
        out_shape=(jax.ShapeDtypeStruct((1, 1), F32), jax.ShapeDtypeStruct((S, D), F32)),
        grid=(S // ts,), in_specs=[row, row], out_specs=(pl.BlockSpec((1, 1), lambda i: (0, 0)), row),
        compiler_params=_params('arbitrary'),
    )(y, target)


def _pair_cols(w):
    lead, f2 = w.shape[:-1], w.shape[-1]
    return w.reshape(lead + (2, 2, f2 // 4)).swapaxes(-3, -2).reshape(lead + (f2,))


def _pad_axis(w, size, axis):
    pad = [(0, 0)] * w.ndim
    pad[axis] = (0, size - w.shape[axis])
    return jnp.pad(w, pad)


def _block_diag(pool_w):
    g = pool_w.shape[0]
    rows = [jnp.concatenate([pool_w[i] if i == j else jnp.zeros_like(pool_w[i]) for j in range(g)], axis=1) for i in range(g)]
    return jnp.concatenate(rows, axis=0)


def _small_weights(w, l):
    return dict(
        norm1_g=w['norm1_g'][l][None, :],
        b_col=_pad_axis(w['b_f'][l][:, None], FG_ROWS, 0),
        qg=jnp.tile(w['q_norm_g'][l], N_HEADS)[None, :],
        kg=jnp.tile(w['k_norm_g'][l], N_HEADS)[None, :],
        dw_b=w['conv_dw_b'][l][None, :], ln_g=w['conv_ln_g'][l][None, :], ln_b=w['conv_ln_b'][l][None, :],
        wbd=_block_diag(w['pool_w'][l]).astype(BF16),
        pool_scale=w['pool_scale'][l][None, :],
        norm2_g=w['norm2_g'][l][None, :],
    )


def _layer_fwd(x, W, p, l):
    n = lambda s: f'l{l}_{s}'
    S = x.shape[0]
    h = _rms_fwd(x, p['norm1_g'], name=n('norm1'), after=W.started)
    proj_a = _mm(h, W.get('w_a', h), b_lead=0, name=n('proj_a'), tn=D_PROJ_A)
    z_raw = _mm(W.get('w_fg_t', h), h, a_lead=0, tb=True, name=n('proj_fg'))
    qkv = _qk_prep_fwd(proj_a, p['qg'], p['kg'], name=n('qk_norm'))
    f_cum = _forget_fwd(z_raw, p['b_col'], name=n('forget'))
    f3 = f_cum[:N_HEADS].reshape(N_HEADS // 2, 2, S)
    mix, att, lse = _attn_fwd(*_attn_aug(qkv, f_cum, name=n('attn_aug')), name=n('attn'))
    mix = _conv_fwd(proj_a, mix, W.get('dw_w', h), p['dw_b'], p['ln_g'], p['ln_b'], W.get('pw_w', h), 0, name=n('conv'))
    mix = _pool_fwd(proj_a, mix, p['wbd'], p['pool_scale'], name=n('pool'))
    x1 = _mm(mix, W.get('w_out', mix), b_lead=0, res=x, name=n('out_proj'), tn=1024)
    h2 = _rms_fwd(x1, p['norm2_g'], name=n('norm2'))
    up = _mm(h2, W.get('w_up', mix), b_lead=0, name=n('up_proj'), tn=1408, cols_outer=True)
    act = _ffn_act_fwd(up, W.get('ffn_w', h), 0, name=n('ffn_act'))
    x2 = _mm(act, W.get('w_down', mix), b_lead=0, res=x1, name=n('down_proj'), tn=1024, tk=2816)
    saved = dict(x=x, h=h, proj_a=proj_a, z_raw=z_raw, qkv=qkv, f3=f3, att=att, lse=lse, mix=mix, x1=x1, h2=h2, up=up, act=act)
    return x2, saved


def _layer_bwd(dx2, W, p, s, l, sink):
    n = lambda t: f'l{l}_{t}'
    S = dx2.shape[0]
    g = {}
    W = W.ready

    def large(key, a, b, **kw):
        return sink.put(l, key, *_mm(a, b, ta=True, copy16=True, name=n('d_' + key), **kw))

    d_act = _mm(dx2, W['w_down'], b_lead=0, tb=True, name=n('d_act'), tn=1408, cols_outer=True)
    large('w_down', s['act'], dx2, tm=1408, tn=1024)
    d_up, d_ffn_w = _ffn_act_bwd(s['up'], d_act, W['ffn_w'], 0, name=n('ffn_act_bwd'))
    g['ffn_dw_w'] = _pair_cols(d_ffn_w[:FFN_CONV_WIDTH])
    d_h2 = _mm(d_up, W['w_up'], b_lead=0, tb=True, name=n('d_h2'), tn=1024, tk=5632)
    started = large('w_up', s['h2'], d_up, tm=1024, tn=512, tk=4096)
    dx1, dg2 = _rms_bwd(s['x1'], p['norm2_g'], d_h2, dx2, name=n('norm2_bwd'))
    g['norm2_g'] = dg2[0]
    sink.point(l, 'mid', dx1)
    d_mix = _mm(dx1, W['w_out'], b_lead=0, tb=True, name=n('d_mix'), tn=1024, after=started)
    large('w_out', s['mix'], dx1, tm=1024, tn=1024)
    dq, dk, dv, df3, dr = _attn_bwd(s['qkv'], s['f3'], s['att'], s['lse'], d_mix, name=n('attn_bwd'))
    df = _pad_axis(df3.reshape(N_HEADS, S) + dr[:, ::HEAD_DIM].T, FG_ROWS, 0)
    d_z, d_b = _forget_bwd(s['z_raw'], p['b_col'], df, name=n('forget_bwd'))
    g['b_f'] = d_b[:N_HEADS, 0]
    d_proj, d_qg, d_kg = _qk_prep_bwd(s['proj_a'], dq, dk, dv, p['qg'], p['kg'], name=n('qk_norm_bwd'))
    g['q_norm_g'] = d_qg.reshape(N_HEADS, HEAD_DIM).sum(axis=0)
    g['k_norm_g'] = d_kg.reshape(N_HEADS, HEAD_DIM).sum(axis=0)
    d_proj, d_dw_w, d_dw_b, d_ln_g, d_ln_b, d_pw = _conv_bwd(
        s['proj_a'], d_mix, d_proj, W['dw_w'], p['dw_b'], p['ln_g'], p['ln_b'], W['pw_w'], 0, name=n('conv_bwd'))
    g['conv_dw_w'], g['conv_dw_b'] = d_dw_w[:CONV_WIDTH], d_dw_b[0]
    g['conv_ln_g'], g['conv_ln_b'], g['conv_pw_w'] = d_ln_g[0], d_ln_b[0], d_pw
    d_proj, d_wbd, d_scale = _pool_bwd(s['proj_a'], d_mix, d_proj, p['wbd'], p['pool_scale'], name=n('pool_bwd'))
    g['pool_w'] = jnp.stack([d_wbd[i * POOL_GROUP:(i + 1) * POOL_GROUP, i * POOL_GROUP:(i + 1) * POOL_GROUP]
                             for i in range(len(POOL_WINDOWS))])
    g['pool_scale'] = d_scale[0]
    d_w_a = _mm(s['h'], d_proj, ta=True, name=n('d_w_a'), tm=1024, tn=768, tk=4096)
    started = sink.put(l, 'w_in', d_w_a, _mm(d_z, s['h'], name=n('d_w_fg'), tn=1024).T)
    d_h_fg = _mm(d_z, W['w_fg_t'], b_lead=0, ta=True, name=n('d_h_fg'), tn=1024, after=started)
    d_h = _mm(d_proj, W['w_a'], b_lead=0, tb=True, res=d_h_fg, name=n('d_h'), tn=1024, tk=D_PROJ_A)
    dx, dg1 = _rms_bwd(s['x'], p['norm1_g'], d_h, dx1, name=n('norm1_bwd'))
    g['norm1_g'] = dg1[0]
    sink.point(l, 'end', dx)
    return dx, g


SMALL_GRADS = REPLICATED + ('conv_dw_w', 'conv_pw_w', 'ffn_dw_w')


def _local_step(x, target, W, w_small, sink):
    depth = w_small['norm1_g'].shape[0]
    ps, saved = [], []
    for l in range(depth):
        p = _small_weights(w_small, l)
        x, s = _layer_fwd(x, W[l], p, l)
        ps.append(p)
        saved.append(s)
    loss, dx = _loss_head(x, target, name='loss_head')
    small = [None] * depth
    for l in reversed(range(depth)):
        dx, small[l] = _layer_bwd(dx, W[l], ps[l], saved[l], l, sink)
    return loss, dx, {k: jnp.stack([small[l][k] for l in range(depth)]) for k in SMALL_GRADS}


W_IN_SHARD = D_IN // N_CHIPS
W_IN_PAD = 640
N_A_TILES = D_PROJ_A // LANES
FG_COL0 = D_QKV


def _a_tile_base(j):
    if j == N_A_TILES:
        return FG_COL0, N_HEADS
    return (j * LANES if j * LANES < FG_COL0 else j * LANES + N_HEADS), LANES


def _shift_select(rows, cols, shift, row_max, col_max):
    r = lax.broadcasted_iota(jnp.int32, (rows, cols), 0)
    c = lax.broadcasted_iota(jnp.int32, (rows, cols), 1)
    return ((r + shift == c) & (r < row_max) & (c < col_max)).astype(BF16)


def _select_w_in(raw, *, name, tm=256):
    _, D, _ = raw.shape
    tm = _tile(D, tm, 16)
    plan = []
    for j in range(N_A_TILES + 1):
        base, cmax = _a_tile_base(j)
        parts = []
        for p in range(N_CHIPS):
            delta = base - W_IN_SHARD * p
            lo, hi = max(0, delta), min(W_IN_SHARD - 1, delta + cmax - 1)
            if lo > hi:
                continue
            a0 = (lo // LANES) * LANES
            kw = min(-(-(hi + 1 - a0) // LANES) * LANES, W_IN_PAD - a0)
            parts.append((p, a0, kw, delta))
        plan.append((cmax, parts))

    def body(raw_ref, wa_ref, fg_ref):
        for j, (cmax, parts) in enumerate(plan):
            acc = None
            for p, a0, kw, delta in parts:
                sel = _shift_select(kw, LANES, a0 - delta, W_IN_SHARD - a0, cmax)
                t = _dot(raw_ref[p, :, a0:a0 + kw], sel, 1, 0)
                acc = t if acc is None else acc + t
            if j == N_A_TILES:
                fg_ref[...] = acc.astype(BF16)
            else:
                wa_ref[:, j * LANES:(j + 1) * LANES] = acc.astype(BF16)

    return pl.pallas_call(
        body, name=name,
        out_shape=(jax.ShapeDtypeStruct((D, D_PROJ_A), BF16), jax.ShapeDtypeStruct((D, LANES), BF16)),
        grid=(D // tm,),
        in_specs=[pl.BlockSpec((N_CHIPS, tm, W_IN_PAD), lambda i: (0, i, 0))],
        out_specs=(pl.BlockSpec((tm, D_PROJ_A), lambda i: (i, 0)), pl.BlockSpec((tm, LANES), lambda i: (i, 0))),
        compiler_params=_params('parallel'),
    )(raw)


def _select_w_in_grads(p_a, p_fg, *, name, tm=256):
    D = p_a.shape[0]
    tm = _tile(D, tm, 16)
    n_local = W_IN_PAD // LANES
    plan = []
    for p in range(N_CHIPS):
        for i in range(n_local):
            cmax = max(0, min(LANES, W_IN_SHARD - i * LANES))
            parts = []
            for j in range(N_A_TILES + 1):
                base, rmax = _a_tile_base(j)
                e = base - W_IN_SHARD * p - i * LANES
                if e + rmax - 1 < 0 or e > cmax - 1:
                    continue
                parts.append((j, e, rmax))
            plan.append((p, i, cmax, parts))

    def body(a_ref, fg_ref, o32_ref, o16_ref):
        terms = {}

        def src(j):
            if j not in terms:
                v = fg_ref[...] if j == N_A_TILES else a_ref[:, j * LANES:(j + 1) * LANES]
                terms[j] = _split3(v)
            return terms[j]

        for p, i, cmax, parts in plan:
            acc = jnp.zeros((tm, LANES), F32)
            for j, e, rmax in parts:
                sel = _shift_select(LANES, LANES, e, rmax, cmax)
                for term in src(j):
                    acc = acc + _dot(term, sel, 1, 0)
            o32_ref[p, :, i * LANES:(i + 1) * LANES] = acc
            o16_ref[p, :, i * LANES:(i + 1) * LANES] = acc.astype(BF16)

    out = pl.BlockSpec((N_CHIPS, tm, W_IN_PAD), lambda i: (0, i, 0))
    return pl.pallas_call(
        body, name=name,
        out_shape=(jax.ShapeDtypeStruct((N_CHIPS, D, W_IN_PAD), F32), jax.ShapeDtypeStruct((N_CHIPS, D, W_IN_PAD), BF16)),
        grid=(D // tm,),
        in_specs=[pl.BlockSpec((tm, D_PROJ_A), lambda i: (i, 0)), pl.BlockSpec((tm, LANES), lambda i: (i, 0))],
        out_specs=(out, out),
        compiler_params=_params('parallel'),
    )(p_a, p_fg)


MESH = pl.DeviceIdType.MESH
HBM_SPEC = pl.BlockSpec(memory_space=pltpu.HBM)


def _place():
    return lax.axis_index('x'), lax.axis_index('y'), lax.axis_index('c')


def _other_chips(x, y):
    return [(1 - x, y), (x, 1 - y), (1 - x, 1 - y)]


def _up_pos(q):
    return (q % 2) * 2 + q // 2


CHUNKS = {
    'w_in': ('lead', None),
    'w_up': ('cols', None),
    'w_down': ('rows', None),
    'w_out': ('rows', None),
    'conv_pw_w': ('rows', None),
    'conv_dw_w': ('lead', None),
    'ffn_dw_w': ('lead', None),
}


def _window(ref, kind, l, q):
    at = (lambda *idx: ref.at[idx]) if l is None else (lambda *idx: ref.at[(l,) + idx])
    shape = ref.shape if l is None else ref.shape[1:]
    if kind == 'lead':
        return at(q)
    if kind == 'rows':
        cs = shape[0] // N_CHIPS
        return at(pl.ds(pl.multiple_of(q * cs, 16), cs), slice(None))
    cs = shape[1] // N_CHIPS
    return at(slice(None), pl.ds(pl.multiple_of(_up_pos(q) * cs, LANES), cs))


def _place_shard(src, l, pos_arr, full_shape, kind, *, name, tm=256, after=None):
    _, m, n = src.shape
    bm = _tile(m, tm, 16) if kind != 'rows' else m
    extra = () if after is None else (after,)

    def body(pos_ref, s_ref, *rest):
        rest[-1][...] = s_ref[...].astype(BF16)

    if kind == 'lead':
        out = pl.BlockSpec((None, bm, n), lambda i, pos: (pos[0], i, 0))
    elif kind == 'rows':
        out = pl.BlockSpec((bm, n), lambda i, pos: (pos[0], 0))
    else:
        out = pl.BlockSpec((bm, n), lambda i, pos: (i, pos[0]))
    return pl.pallas_call(
        body, name=name, out_shape=jax.ShapeDtypeStruct(full_shape, BF16),
        grid_spec=pltpu.PrefetchScalarGridSpec(
            num_scalar_prefetch=1, grid=(m // bm,),
            in_specs=[pl.BlockSpec((None, bm, n), lambda i, pos: (l, i, 0))] + [ANY_SPEC] * len(extra), out_specs=out),
        compiler_params=_params('parallel'),
    )(pos_arr, src, *extra)


GATHERED = ('w_in', 'w_up', 'w_down', 'w_out', 'conv_pw_w', 'conv_dw_w', 'ffn_dw_w')
GATHER_GROUPS = ((0, ('w_in', 'conv_dw_w', 'ffn_dw_w', 'conv_pw_w')), (0, ('w_out', 'w_up', 'w_down')), (1, GATHERED))
SEM_SPEC = pl.BlockSpec(memory_space=pltpu.SEMAPHORE)
SPLIT_COPY_PARAMS = pltpu.CompilerParams(has_side_effects=pltpu.SideEffectType.DATAFLOW_SIDE_EFFECTING)


def _gather_start(tag, groups, bufs):
    flat = [b for group in bufs for b in group]
    nb = len(flat)

    def body(*refs):
        outs, sems, token = refs[nb:2 * nb], refs[2 * nb:-1], refs[-1]
        token[...] = jnp.zeros(token.shape, F32)
        x, y, c = _place()
        pos = 0
        for g, keys in enumerate(GATHER_GROUPS[n][1] for n in groups):
            for i, k in enumerate(keys):
                w = _window(outs[pos], CHUNKS[k][0], None, 2 * x + y)
                pos += 1
                for j, chip in enumerate(_other_chips(x, y)):
                    pltpu.make_async_remote_copy(src_ref=w, dst_ref=w, send_sem=sems[2 * g].at[3 * i + j],
                                                 recv_sem=sems[2 * g + 1].at[3 * i + j], device_id=(*chip, c),
                                                 device_id_type=MESH).start()

    sem_shapes = [pltpu.SemaphoreType.DMA((3 * len(GATHER_GROUPS[n][1]),)) for n in groups for _ in range(2)]
    res = pl.pallas_call(
        body, name=f'gather_start_{tag}',
        out_shape=tuple(jax.ShapeDtypeStruct(b.shape, b.dtype) for b in flat) + tuple(sem_shapes)
        + (jax.ShapeDtypeStruct((8, LANES), F32),),
        in_specs=[HBM_SPEC] * nb,
        out_specs=tuple([HBM_SPEC] * nb + [SEM_SPEC] * len(sem_shapes) + [pl.BlockSpec(memory_space=pltpu.VMEM)]),
        input_output_aliases={b: b for b in range(nb)},
        compiler_params=SPLIT_COPY_PARAMS,
    )(*[pltpu.with_memory_space_constraint(b, pltpu.HBM) for b in flat])
    out_bufs, sems, pos = [], res[nb:-1], 0
    for group in bufs:
        out_bufs.append(list(res[pos:pos + len(group)]))
        pos += len(group)
    return out_bufs, [(sems[2 * g], sems[2 * g + 1]) for g in range(len(groups))], res[-1]


def _gather_wait(g, bufs, sems, after):
    keys = GATHER_GROUPS[g][1]
    nb = len(bufs)

    def body(*refs):
        send_sems, recv_sems = refs[nb], refs[nb + 1]
        outs = refs[nb + 3:]
        x, y, c = _place()
        for i, k in enumerate(keys):
            mine = _window(outs[i], CHUNKS[k][0], None, 2 * x + y)
            for j, (cx, cy) in enumerate(_other_chips(x, y)):
                theirs = _window(outs[i], CHUNKS[k][0], None, 2 * cx + cy)
                cp = pltpu.make_async_remote_copy(src_ref=mine, dst_ref=theirs, send_sem=send_sems.at[3 * i + j],
                                                  recv_sem=recv_sems.at[3 * i + j], device_id=(cx, cy, c), device_id_type=MESH)
                cp.wait_send()
                cp.wait_recv()

    return pl.pallas_call(
        body, name=f'gather_wait_{g}',
        out_shape=tuple(jax.ShapeDtypeStruct(b.shape, b.dtype) for b in bufs),
        in_specs=[HBM_SPEC] * nb + [SEM_SPEC, SEM_SPEC, ANY_SPEC], out_specs=tuple([HBM_SPEC] * nb),
        input_output_aliases={b: b for b in range(nb)},
        compiler_params=SPLIT_COPY_PARAMS,
    )(*bufs, *sems, after)


def _rs_block(M, N):
    return (_tile(M, 256, 16), _tile(N, 2048))


def _chunk_shape(shape, kind):
    if kind == 'lead':
        return tuple(shape[1:])
    if kind == 'rows':
        return (shape[0] // N_CHIPS, shape[1])
    return (shape[0], shape[1] // N_CHIPS)


def _rs_start(tag, bufs, kinds):
    nb = len(bufs)
    lands = [lax.empty((N_CHIPS - 1,) + _chunk_shape(b.shape, k), b.dtype) for b, k in zip(bufs, kinds)]

    def body(*refs):
        src, land = refs[2 * nb:3 * nb], refs[3 * nb:4 * nb]
        send_sems, recv_sems, token = refs[4 * nb:]
        token[...] = jnp.zeros(token.shape, F32)
        x, y, c = _place()
        for b in range(nb):
            for j, (cx, cy) in enumerate(_other_chips(x, y)):
                pltpu.make_async_remote_copy(
                    src_ref=_window(src[b], kinds[b], None, 2 * cx + cy), dst_ref=land[b].at[j],
                    send_sem=send_sems.at[3 * b + j], recv_sem=recv_sems.at[3 * b + j],
                    device_id=(cx, cy, c), device_id_type=MESH).start()

    sem = pltpu.SemaphoreType.DMA((3 * nb,))
    res = pl.pallas_call(
        body, name=f'rs_start_{tag}',
        out_shape=tuple(jax.ShapeDtypeStruct(b.shape, b.dtype) for b in list(bufs) + lands)
        + (sem, sem, jax.ShapeDtypeStruct((8, LANES), F32)),
        in_specs=[HBM_SPEC] * (2 * nb),
        out_specs=tuple([HBM_SPEC] * (2 * nb) + [SEM_SPEC, SEM_SPEC, pl.BlockSpec(memory_space=pltpu.VMEM)]),
        input_output_aliases={b: b for b in range(2 * nb)},
        compiler_params=SPLIT_COPY_PARAMS,
    )(*[pltpu.with_memory_space_constraint(b, pltpu.HBM) for b in list(bufs) + lands])
    return res[:nb], res[nb:2 * nb], res[2 * nb:2 * nb + 2], res[2 * nb + 2]


def _rs_wait(tag, bufs, lands, sems, kinds, after):
    nb = len(bufs)

    def body(*refs):
        send_sems, recv_sems = refs[2 * nb], refs[2 * nb + 1]
        src, land = refs[2 * nb + 3:3 * nb + 3], refs[3 * nb + 3:]
        x, y, c = _place()
        for b in range(nb):
            for j, (cx, cy) in enumerate(_other_chips(x, y)):
                cp = pltpu.make_async_remote_copy(
                    src_ref=_window(src[b], kinds[b], None, 2 * cx + cy), dst_ref=land[b].at[j],
                    send_sem=send_sems.at[3 * b + j], recv_sem=recv_sems.at[3 * b + j],
                    device_id=(cx, cy, c), device_id_type=MESH)
                cp.wait_send()
                cp.wait_recv()

    res = pl.pallas_call(
        body, name=f'rs_wait_{tag}',
        out_shape=tuple(jax.ShapeDtypeStruct(b.shape, b.dtype) for b in list(bufs) + list(lands)),
        in_specs=[HBM_SPEC] * (2 * nb) + [SEM_SPEC, SEM_SPEC, ANY_SPEC], out_specs=tuple([HBM_SPEC] * (2 * nb)),
        input_output_aliases={b: b for b in range(2 * nb)},
        compiler_params=SPLIT_COPY_PARAMS,
    )(*bufs, *lands, *sems, after)
    return res[nb:]


def _rs_sum(p, rb, kind, pos_arr, l, depth, buf, *, name):
    m, n = rb.shape[1:]
    bm, bn = _rs_block(m, n)
    nbm, nbn = m // bm, n // bn
    has_buf = buf is not None

    def body(q_ref, p_ref, r_ref, *rest):
        acc = p_ref[...]
        for j in range(N_CHIPS - 1):
            acc = acc + r_ref[j].astype(F32)
        rest[-1][...] = acc

    if kind == 'lead':
        p_map = lambda i, j, q: (q[0], i, j)
    elif kind == 'rows':
        p_map = lambda i, j, q: (q[0] * nbm + i, j)
    else:
        p_map = lambda i, j, q: (i, q[0] * nbn + j)
    r_spec = pl.BlockSpec((N_CHIPS - 1, bm, bn), lambda i, j, q: (0, i, j))
    p_spec = pl.BlockSpec(((None,) if kind == 'lead' else ()) + (bm, bn), p_map)
    return pl.pallas_call(
        body, name=name, out_shape=jax.ShapeDtypeStruct((depth, m, n), F32),
        grid_spec=pltpu.PrefetchScalarGridSpec(
            num_scalar_prefetch=1, grid=(nbm, nbn), in_specs=[p_spec, r_spec] + ([ANY_SPEC] if has_buf else []),
            out_specs=pl.BlockSpec((None, bm, bn), lambda i, j, q: (l, i, j))),
        input_output_aliases={3: 0} if has_buf else {},
        compiler_params=_params('parallel', 'parallel'),
    )(pos_arr, p, rb, *((buf,) if has_buf else ()))


def _swap_start(bufs):
    nb = len(bufs)
    lands = [lax.empty(b.shape, b.dtype) for b in bufs]

    def body(*refs):
        src, land = refs[2 * nb:3 * nb], refs[3 * nb:4 * nb]
        send_sems, recv_sems, token = refs[4 * nb:]
        token[...] = jnp.zeros(token.shape, F32)
        x, y, c = _place()
        for b in range(nb):
            pltpu.make_async_remote_copy(src_ref=src[b], dst_ref=land[b], send_sem=send_sems.at[b], recv_sem=recv_sems.at[b],
                                         device_id=(x, y, 1 - c), device_id_type=MESH).start()

    sem = pltpu.SemaphoreType.DMA((nb,))
    res = pl.pallas_call(
        body, name='rs_swap_start',
        out_shape=tuple(jax.ShapeDtypeStruct(b.shape, b.dtype) for b in list(bufs) + lands)
        + (sem, sem, jax.ShapeDtypeStruct((8, LANES), F32)),
        in_specs=[HBM_SPEC] * (2 * nb),
        out_specs=tuple([HBM_SPEC] * (2 * nb) + [SEM_SPEC, SEM_SPEC, pl.BlockSpec(memory_space=pltpu.VMEM)]),
        input_output_aliases={b: b for b in range(2 * nb)},
        compiler_params=SPLIT_COPY_PARAMS,
    )(*[pltpu.with_memory_space_constraint(b, pltpu.HBM) for b in list(bufs) + lands])
    return res[:nb], res[nb:2 * nb], res[2 * nb:2 * nb + 2], res[2 * nb + 2]


def _swap_wait(bufs, lands, sems, after):
    nb = len(bufs)

    def body(*refs):
        send_sems, recv_sems = refs[2 * nb], refs[2 * nb + 1]
        src, land = refs[2 * nb + 3:3 * nb + 3], refs[3 * nb + 3:]
        x, y, c = _place()
        for b in range(nb):
            cp = pltpu.make_async_remote_copy(src_ref=src[b], dst_ref=land[b], send_sem=send_sems.at[b],
                                              recv_sem=recv_sems.at[b], device_id=(x, y, 1 - c), device_id_type=MESH)
            cp.wait_send()
            cp.wait_recv()

    res = pl.pallas_call(
        body, name='rs_swap_wait',
        out_shape=tuple(jax.ShapeDtypeStruct(b.shape, b.dtype) for b in list(bufs) + list(lands)),
        in_specs=[HBM_SPEC] * (2 * nb) + [SEM_SPEC, SEM_SPEC, ANY_SPEC], out_specs=tuple([HBM_SPEC] * (2 * nb)),
        input_output_aliases={b: b for b in range(2 * nb)},
        compiler_params=SPLIT_COPY_PARAMS,
    )(*bufs, *lands, *sems, after)
    return res[:nb], res[nb:]


def _all_reduce_small(v, after):
    r = v.shape[0]

    def body(x_ref, after_ref, tot_ref, all_ref, send_sems, recv_sems):
        x, y, c = _place()
        me, sibling = (x, y, c), (x, y, 1 - c)
        chips = _other_chips(x, y)

        def rows(px, py, pc):
            return all_ref.at[pl.ds((4 * px + 2 * py + pc) * r, r), :]

        def copy(k, block, to, src=None):
            return pltpu.make_async_remote_copy(
                src_ref=rows(*block) if src is None else src, dst_ref=rows(*block),
                send_sem=send_sems.at[k], recv_sem=recv_sems.at[k], device_id=to, device_id_type=MESH)

        rows(*me)[...] = x_ref[...]
        first = [copy(0, me, sibling, src=x_ref)]
        first += [copy(1 + j, me, (*chip, c), src=x_ref) for j, chip in enumerate(chips)]
        for cp in first:
            cp.start()
        passed = [copy(4 + j, (*chip, c), sibling) for j, chip in enumerate(chips)]
        for j, chip in enumerate(chips):
            copy(1 + j, (*chip, c), me).wait_recv()
            passed[j].start()
        copy(0, sibling, me).wait_recv()
        for j, chip in enumerate(chips):
            copy(4 + j, (*chip, 1 - c), me).wait_recv()
        for cp in first + passed:
            cp.wait_send()
        acc = all_ref[0:r, :]
        for d in range(1, N_DEV):
            acc = acc + all_ref[d * r:(d + 1) * r, :]
        tot_ref[...] = acc

    return pl.pallas_call(
        body, name='all_reduce_small', out_shape=jax.ShapeDtypeStruct((r, LANES), F32),
        in_specs=[pl.BlockSpec(memory_space=pltpu.VMEM), ANY_SPEC], out_specs=pl.BlockSpec(memory_space=pltpu.VMEM),
        scratch_shapes=[pltpu.VMEM((N_DEV * r, LANES), F32), pltpu.SemaphoreType.DMA((7,)), pltpu.SemaphoreType.DMA((7,))],
    )(v, after)


def _adamw(w, g, m, v, *, name, g2=None, ts=256):
    R, C = w.shape
    Cg = g.shape[1]
    ts = _tile(R, ts, 8)
    c1 = 1.0 - ADAM_B1 ** ADAM_STEP
    c2 = 1.0 - ADAM_B2 ** ADAM_STEP
    two = g2 is not None

    def body(w_ref, g_ref, *rest):
        m_ref, v_ref, go_ref, d_ref, nm_ref, nv_ref = rest[two:]
        gv = g_ref[:, 0:C]
        if two:
            gv = gv + rest[0][:, 0:C]
        nm = ADAM_B1 * m_ref[...] + (1.0 - ADAM_B1) * gv
        nv = ADAM_B2 * v_ref[...] + (1.0 - ADAM_B2) * (gv * gv)
        d_ref[...] = -ADAM_LR * ((nm / c1) / (jnp.sqrt(nv / c2) + ADAM_EPS) + ADAM_WD * w_ref[...])
        go_ref[...] = gv
        nm_ref[...] = nm
        nv_ref[...] = nv

    blk = pl.BlockSpec((ts, C), lambda i: (i, 0))
    gblk = pl.BlockSpec((ts, Cg), lambda i: (i, 0))
    shape = jax.ShapeDtypeStruct((R, C), F32)
    return pl.pallas_call(body, name=name, out_shape=(shape, shape, shape, shape), grid=(R // ts,),
                          in_specs=[blk, gblk] + ([gblk] if two else []) + [blk, blk], out_specs=(blk, blk, blk, blk),
                          compiler_params=_params('parallel'))(w, g, *((g2,) if two else ()), m, v)


def _pack_rows(parts, row_unit):
    flat = jnp.concatenate(parts)
    flat = _pad_axis(flat, -(-flat.shape[0] // (row_unit * LANES)) * row_unit * LANES, 0)
    return flat.reshape(-1, LANES)


def _as_2d(a):
    return a.reshape(-1, a.shape[-1])


def _mesh_place():
    cx, cy, cc = _place()
    chip = 2 * cx + cy
    as_arr = lambda v: jnp.reshape(v, (1,)).astype(jnp.int32)
    return chip, as_arr(cc), as_arr(chip), as_arr(_up_pos(chip))


class _LayerWeights:
    def __init__(self, groups, started=None):
        self.groups = groups
        self.started = started
        self.ready = {}

    def get(self, name, after):
        if name not in self.ready:
            for names, wait in self.groups:
                if name in names:
                    self.ready.update({k: v[None] for k, v in wait(after).items()})
        return self.ready[name]


def _gather_full(w, place):
    chip, _, chip_arr, up_pos_arr = place
    L, D = w['w_in'].shape[:2]
    w_in_pad = _pad_axis(w['w_in'], W_IN_PAD, 2)

    def placed(k, l, after=None):
        if k == 'w_in':
            return _place_shard(w_in_pad, l, chip_arr, (N_CHIPS, D, W_IN_PAD), 'lead', name=f'place_w_in_{l}', after=after)
        if k == 'w_up':
            return _place_shard(w[k], l, up_pos_arr, (w[k].shape[1], N_CHIPS * w[k].shape[2]), 'cols', name=f'place_w_up_{l}',
                                after=after)
        if k in ('conv_dw_w', 'ffn_dw_w'):
            return lax.dynamic_update_slice_in_dim(jnp.zeros((N_CHIPS,) + w[k].shape[1:], F32), w[k][l][None], chip, axis=0)
        return _place_shard(w[k], l, chip_arr, (N_CHIPS * w[k].shape[1], w[k].shape[2]), 'rows', name=f'place_{k}_{l}',
                            after=after)

    first, rest = [0], list(range(1, len(GATHER_GROUPS)))
    bufs0, sems0, token = _gather_start('first', first, [[placed(k, GATHER_GROUPS[0][0]) for k in GATHER_GROUPS[0][1]]])
    bufs1, sems1, started = _gather_start('rest', rest, [[placed(k, GATHER_GROUPS[g][0], token) for k in GATHER_GROUPS[g][1]]
                                                         for g in rest])
    bufs, sems = bufs0 + bufs1, sems0 + sems1
    unchunk = lambda a: jnp.moveaxis(a, 0, 1).reshape(a.shape[1], -1)

    def waiter(g):
        l, keys = GATHER_GROUPS[g]

        def wait(after):
            full = dict(zip(keys, _gather_wait(g, bufs[g], sems[g], after)))
            out = {}
            if 'w_in' in full:
                out['w_a'], w_fg = _select_w_in(full['w_in'], name=f'select_w_in_{l}')
                out['w_fg_t'] = w_fg.T
            if 'conv_dw_w' in full:
                out['dw_w'] = _pad_axis(unchunk(full['conv_dw_w']), CONV_HALO, 0)
            if 'ffn_dw_w' in full:
                out['ffn_w'] = _pad_axis(_pair_cols(unchunk(full['ffn_dw_w'])), FFN_HALO, 0)
            if 'conv_pw_w' in full:
                out['pw_w'] = full['conv_pw_w']
            out.update({k: full[k] for k in ('w_up', 'w_down', 'w_out') if k in full})
            return out

        names = {'w_in': ('w_a', 'w_fg_t'), 'conv_dw_w': ('dw_w',), 'ffn_dw_w': ('ffn_w',), 'conv_pw_w': ('pw_w',)}
        return tuple(n for k in keys for n in names.get(k, (k,))), wait

    return [_LayerWeights([waiter(g) for g in range(len(GATHER_GROUPS)) if GATHER_GROUPS[g][0] == l],
                          started if l == 0 else None) for l in range(L)]


RS_WIRE = ('w_in', 'w_up', 'w_down', 'w_out')
RS_GROUPS = (('ffn', ('w_down', 'w_up')), ('mix', ('w_out', 'w_in')))


class _GradReducer:
    def __init__(self, place, depth):
        _, _, self.chip_arr, self.up_pos_arr = place
        self.depth = depth
        self.got = {}
        self.flying = {}
        self.sums = {}

    def put(self, l, key, g32, g16):
        if key == 'w_in':
            g32, g16 = _select_w_in_grads(g32, g16, name=f'l{l}_select_w_in_grads')
        self.got[(l, key)] = (g32, g16)
        for tag, keys in RS_GROUPS:
            if key == keys[-1]:
                kinds = [CHUNKS[k][0] for k in keys]
                bufs, lands, sems, token = _rs_start(f'l{l}_{tag}', [self.got[(l, k)][1] for k in keys], kinds)
                self.flying[(l, tag)] = (bufs, lands, sems, kinds)
                return token
        return None

    def point(self, l, where, after):
        if where == 'mid':
            self._land(l + 1, 'mix', after)
        else:
            self._land(l, 'ffn', after)

    def _land(self, l, tag, after):
        if (l, tag) not in self.flying:
            return
        bufs, lands, sems, kinds = self.flying.pop((l, tag))
        lands = _rs_wait(f'l{l}_{tag}', bufs, lands, sems, kinds, after)
        for k, rb, kind in zip(dict(RS_GROUPS)[tag], lands, kinds):
            pos = self.up_pos_arr if kind == 'cols' else self.chip_arr
            self.sums[k] = _rs_sum(self.got.pop((l, k))[0], rb, kind, pos, l, self.depth, self.sums.get(k), name=f'l{l}_rs_sum_{k}')

    def finish_start(self, after):
        for l, tag in list(self.flying):
            self._land(l, tag, after)
        *self.swap, token = _swap_start([self.sums[k] for k in RS_WIRE])
        return token

    def finish_wait(self, after):
        return dict(zip(RS_WIRE, zip(*_swap_wait(*self.swap, after))))


def kernel(x, norm1_g, w_in, b_f, q_norm_g, k_norm_g, conv_dw_w, conv_dw_b, conv_ln_g, conv_ln_b, conv_pw_w, pool_w, pool_scale, w_out, norm2_g, w_up, ffn_dw_w, w_down, loss_target, m_norm1_g, m_w_in, m_b_f, m_q_norm_g, m_k_norm_g, m_conv_dw_w, m_conv_dw_b, m_conv_ln_g, m_conv_ln_b, m_conv_pw_w, m_pool_w, m_pool_scale, m_w_out, m_norm2_g, m_w_up, m_ffn_dw_w, m_w_down, v_norm1_g, v_w_in, v_b_f, v_q_norm_g, v_k_norm_g, v_conv_dw_w, v_conv_dw_b, v_conv_ln_g, v_conv_ln_b, v_conv_pw_w, v_pool_w, v_pool_scale, v_w_out, v_norm2_g, v_w_up, v_ffn_dw_w, v_w_down):
    given = dict(locals())
    w = {k: given[k] for k in WEIGHTS}
    mom_m = {k: given['m_' + k] for k in WEIGHTS}
    mom_v = {k: given['v_' + k] for k in WEIGHTS}
    place = _mesh_place()
    chip = place[0]
    W = _gather_full(w, place)

    reducer = _GradReducer(place, norm1_g.shape[0])
    loss_part, grad_x, g_small = _local_step(x[0], loss_target[0], W, {k: w[k] for k in REPLICATED}, reducer)
    loss = lax.psum(loss_part[0, 0], ('x', 'y', 'c'))
    swapping = reducer.finish_start(grad_x)

    small = _pack_rows([g_small[k].reshape(-1) for k in SMALL_GRADS], 8)
    small_sum = _all_reduce_small(small, swapping)
    g_sum, delta, new_m, new_v = {}, {}, {}, {}
    off = 0
    small_full = {}
    for k in SMALL_GRADS:
        small_full[k] = small_sum.reshape(-1)[off:off + g_small[k].size].reshape(g_small[k].shape)
        off += g_small[k].size
    small_g = {k: small_full[k] for k in REPLICATED}
    small_g['conv_dw_w'] = lax.dynamic_slice_in_dim(small_full['conv_dw_w'], chip * w['conv_dw_w'].shape[2], w['conv_dw_w'].shape[2], axis=2)
    small_g['conv_pw_w'] = lax.dynamic_slice_in_dim(small_full['conv_pw_w'], chip * w['conv_pw_w'].shape[1], w['conv_pw_w'].shape[1], axis=1)
    small_g['ffn_dw_w'] = lax.dynamic_slice_in_dim(small_full['ffn_dw_w'], chip * w['ffn_dw_w'].shape[2], w['ffn_dw_w'].shape[2], axis=2)
    pack_small = lambda t: _pack_rows([t[k].reshape(-1) for k in SMALL_GRADS], 256)
    outs = _adamw(pack_small(w), pack_small(small_g), pack_small(mom_m), pack_small(mom_v), name='adamw_small')
    off = 0
    for k in SMALL_GRADS:
        pieces = [o.reshape(-1)[off:off + w[k].size].reshape(w[k].shape) for o in outs]
        g_sum[k], delta[k], new_m[k], new_v[k] = pieces
        off += w[k].size

    sums = reducer.finish_wait(outs[1])
    for k in RS_WIRE:
        outs = _adamw(_as_2d(w[k]), _as_2d(sums[k][0]), _as_2d(mom_m[k]), _as_2d(mom_v[k]), g2=_as_2d(sums[k][1]), name='adamw_' + k)
        g_sum[k], delta[k], new_m[k], new_v[k] = [o.reshape(w[k].shape) for o in outs]

    return (loss, grad_x[None], *[g_sum[k] for k in WEIGHTS], *[delta[k] for k in WEIGHTS],
            *[new_m[k] for k in WEIGHTS], *[new_v[k] for k in WEIGHTS])
```

```python
import functools

import jax
import jax.numpy as jnp
from jax import lax
from jax.experimental import pallas as pl
from jax.experimental.pallas import tpu as pltpu

F32 = jnp.float32
BF16 = jnp.bfloat16

N_HEADS = 8
HEAD_DIM = 64
D_ATT = N_HEADS * HEAD_DIM
D_CONV = 256
D_POOL = 256
D_MIX = D_ATT + D_CONV + D_POOL
D_QKV = 3 * D_ATT
D_PROJ_A = D_QKV + 2 * D_CONV + D_POOL
D_IN = D_PROJ_A + N_HEADS
FG_ROWS = 128
CONV_WIDTH = 31
CONV_HALO = 32
POOL_WINDOWS = (2, 4, 8, 16)
POOL_GROUP = 64
POOL_HALO = 16
FFN_CONV_WIDTH = 3
FFN_HALO = 8
ATT_SCALE = HEAD_DIM ** -0.5
EPS = 1e-6
NEG = -1e30
LANES = 128

ADAM_LR = 0.001
ADAM_B1 = 0.9
ADAM_B2 = 0.999
ADAM_EPS = 1e-08
ADAM_WD = 0.01
ADAM_STEP = 10

N_CHIPS = 4
N_DEV = 8
VMEM_LIMIT_BYTES = 56 * 1024 * 1024

REPLICATED = ('norm1_g', 'b_f', 'q_norm_g', 'k_norm_g', 'conv_dw_b', 'conv_ln_g', 'conv_ln_b',
              'pool_w', 'pool_scale', 'norm2_g')
WEIGHTS = ('norm1_g', 'w_in', 'b_f', 'q_norm_g', 'k_norm_g', 'conv_dw_w', 'conv_dw_b', 'conv_ln_g',
           'conv_ln_b', 'conv_pw_w', 'pool_w', 'pool_scale', 'w_out', 'norm2_g', 'w_up', 'ffn_dw_w', 'w_down')


def _tile(dim, pref, unit=LANES):
    if dim <= pref:
        return dim
    t = (pref // unit) * unit
    while t >= unit:
        if dim % t == 0:
            return t
        t -= unit
    raise ValueError(f'no tile for {dim} (preferred {pref})')


def _params(*sem):
    return pltpu.CompilerParams(dimension_semantics=sem, vmem_limit_bytes=VMEM_LIMIT_BYTES)


def _sigmoid(x):
    return 1.0 / (1.0 + jnp.exp(-x))


def _dot(a, b, ca, cb):
    return lax.dot_general(a, b, (((ca,), (cb,)), ((), ())), preferred_element_type=F32)


def _split3(y):
    y1 = y.astype(BF16)
    r1 = y - y1.astype(F32)
    y2 = r1.astype(BF16)
    y3 = (r1 - y2.astype(F32)).astype(BF16)
    return y1, y2, y3


def _dot3(y, e, ca=1, cb=0):
    y1, y2, y3 = _split3(y)
    return _dot(y1, e, ca, cb) + _dot(y2, e, ca, cb) + _dot(y3, e, ca, cb)


def _lead(spec_shape, imap, lead):
    if lead is None:
        return pl.BlockSpec(spec_shape, imap)
    return pl.BlockSpec((None,) + spec_shape, lambda *g: (lead,) + imap(*g))


ANY_SPEC = pl.BlockSpec(memory_space=pl.ANY)


def _mm(a, b, *, name, ta=False, tb=False, res=None, out_dtype=F32, tm=512, tn=512, tk=1024,
        a_lead=None, b_lead=None, copy16=False, after=None, cols_outer=False):
    a2, b2 = a.shape[-2:], b.shape[-2:]
    K, M = a2 if ta else a2[::-1]
    N, Kb = b2 if tb else b2[::-1]
    assert K == Kb, (a.shape, b.shape)
    tm, tn, tk = _tile(M, tm), _tile(N, tn), _tile(K, tk)
    nk = K // tk
    ca = 0 if ta else 1
    cb = 1 if tb else 0
    has_res = res is not None
    n_in = 2 + has_res + (after is not None)
    n_out = 1 + copy16

    def body(*refs):
        a_ref, b_ref = refs[:2]
        r_ref = refs[2] if has_res else None
        o_refs = refs[n_in:n_in + n_out]
        scratch = refs[n_in + n_out:]

        def write(r):
            if has_res:
                r = r + r_ref[...]
            o_refs[0][...] = r.astype(out_dtype)
            if copy16:
                o_refs[1][...] = r.astype(BF16)

        p = _dot(a_ref[...].astype(BF16), b_ref[...].astype(BF16), ca, cb)
        if nk == 1:
            write(p)
        else:
            acc = scratch[0]
            k = pl.program_id(2)

            @pl.when(k == 0)
            def _():
                acc[...] = p

            @pl.when(k > 0)
            def _():
                acc[...] += p

            @pl.when(k == nk - 1)
            def _():
                write(acc[...])

    ij = (lambda g0, g1: (g1, g0)) if cols_outer else (lambda g0, g1: (g0, g1))
    at = lambda f: (lambda g0, g1, k: f(*ij(g0, g1), k))
    a_spec = _lead((tk, tm), at(lambda i, j, k: (k, i)), a_lead) if ta else _lead((tm, tk), at(lambda i, j, k: (i, k)), a_lead)
    b_spec = _lead((tn, tk), at(lambda i, j, k: (j, k)), b_lead) if tb else _lead((tk, tn), at(lambda i, j, k: (k, j)), b_lead)
    o_spec = pl.BlockSpec((tm, tn), at(lambda i, j, k: (i, j)))
    in_specs = [a_spec, b_spec] + ([o_spec] if has_res else []) + ([ANY_SPEC] if after is not None else [])
    args = (a, b) + ((res,) if has_res else ()) + ((after,) if after is not None else ())
    out_shape = [jax.ShapeDtypeStruct((M, N), out_dtype)] + ([jax.ShapeDtypeStruct((M, N), BF16)] if copy16 else [])
    out = pl.pallas_call(
        body, name=name,
        out_shape=tuple(out_shape),
        grid=ij(M // tm, N // tn) + (nk,),
        in_specs=in_specs, out_specs=tuple([o_spec] * n_out),
        scratch_shapes=[pltpu.VMEM((tm, tn), F32)] if nk > 1 else [],
        compiler_params=_params('parallel', 'parallel', 'arbitrary'),
    )(*args)
    return out if copy16 else out[0]


def _rms_fwd(x, g, *, name, ts=512, after=None):
    S, D = x.shape
    ts = _tile(S, ts, 8)

    def body(x_ref, g_ref, *rest):
        xv = x_ref[...]
        r = lax.rsqrt(jnp.mean(xv * xv, axis=-1, keepdims=True) + EPS)
        rest[-1][...] = (xv * r * g_ref[...]).astype(BF16)

    extra = () if after is None else (after,)
    return pl.pallas_call(
        body, name=name, out_shape=jax.ShapeDtypeStruct((S, D), BF16), grid=(S // ts,),
        in_specs=[pl.BlockSpec((ts, D), lambda i: (i, 0)), pl.BlockSpec((1, D), lambda i: (0, 0))] + [ANY_SPEC] * len(extra),
        out_specs=pl.BlockSpec((ts, D), lambda i: (i, 0)),
        compiler_params=_params('parallel'),
    )(x, g, *extra)


def _rms_bwd(x, g, dh, dres, *, name, ts=512):
    S, D = x.shape
    ts = _tile(S, ts, 8)

    def body(x_ref, g_ref, dh_ref, dr_ref, dx_ref, dg_ref):
        i = pl.program_id(0)
        xv = x_ref[...]
        r = lax.rsqrt(jnp.mean(xv * xv, axis=-1, keepdims=True) + EPS)
        y = xv * r
        dh_v = dh_ref[...]
        dy = dh_v * g_ref[...]
        dx_ref[...] = dr_ref[...] + r * (dy - y * jnp.mean(dy * y, axis=-1, keepdims=True))
        part = jnp.sum(dh_v * y, axis=0, keepdims=True)

        @pl.when(i == 0)
        def _():
            dg_ref[...] = part

        @pl.when(i > 0)
        def _():
            dg_ref[...] += part

    row = pl.BlockSpec((ts, D), lambda i: (i, 0))
    vec = pl.BlockSpec((1, D), lambda i: (0, 0))
    return pl.pallas_call(
        body, name=name,
        out_shape=(jax.ShapeDtypeStruct((S, D), F32), jax.ShapeDtypeStruct((1, D), F32)),
        grid=(S // ts,), in_specs=[row, vec, row, row], out_specs=(row, vec),
        compiler_params=_params('arbitrary'),
    )(x, g, dh, dres)


def _pair_ones():
    i = lax.broadcasted_iota(jnp.int32, (LANES, LANES), 0) // HEAD_DIM
    j = lax.broadcasted_iota(jnp.int32, (LANES, LANES), 1) // HEAD_DIM
    return (i == j).astype(BF16)


def _head_sums(y, e):
    return jnp.concatenate([_dot3(y[:, b * LANES:(b + 1) * LANES], e) for b in range(D_ATT // LANES)], axis=1)


def _qk_prep_fwd(proj_a, qg, kg, *, name, ts=512):
    S = proj_a.shape[0]
    ts = _tile(S, ts, 16)

    def body(q_ref, k_ref, v_ref, qg_ref, kg_ref, e_ref, o_ref):
        e = e_ref[...]

        def norm(xv, gain):
            ms = _head_sums(xv * xv, e) * (1.0 / HEAD_DIM)
            return xv * lax.rsqrt(ms + EPS) * gain

        o_ref[:, 0:D_ATT] = (norm(q_ref[...], qg_ref[...]) * ATT_SCALE).astype(BF16)
        o_ref[:, D_ATT:2 * D_ATT] = norm(k_ref[...], kg_ref[...]).astype(BF16)
        o_ref[:, 2 * D_ATT:3 * D_ATT] = v_ref[...].astype(BF16)

    col = lambda c: pl.BlockSpec((ts, D_ATT), lambda i: (i, c))
    vec = pl.BlockSpec((1, D_ATT), lambda i: (0, 0))
    return pl.pallas_call(
        body, name=name, out_shape=jax.ShapeDtypeStruct((S, D_QKV), BF16), grid=(S // ts,),
        in_specs=[col(0), col(1), col(2), vec, vec, pl.BlockSpec((LANES, LANES), lambda i: (0, 0))],
        out_specs=pl.BlockSpec((ts, D_QKV), lambda i: (i, 0)),
        compiler_params=_params('parallel'),
    )(proj_a, proj_a, proj_a, qg, kg, _pair_ones())


def _qk_prep_bwd(proj_a, dq, dk, dv, qg, kg, *, name, ts=512):
    S = proj_a.shape[0]
    ts = _tile(S, ts, 16)

    def body(q_ref, k_ref, dq_ref, dk_ref, dv_ref, qg_ref, kg_ref, e_ref, o_ref, dqg_ref, dkg_ref):
        i = pl.program_id(0)
        e = e_ref[...]

        def norm_bwd(xv, dn, gain, scale):
            ms = _head_sums(xv * xv, e) * (1.0 / HEAD_DIM)
            r = lax.rsqrt(ms + EPS)
            y = xv * r
            dy = dn * (gain * scale)
            mean = _head_sums(dy * y, e) * (1.0 / HEAD_DIM)
            return r * (dy - y * mean), jnp.sum(dn * y, axis=0, keepdims=True) * scale

        dq_raw, dqg = norm_bwd(q_ref[...], dq_ref[...], qg_ref[...], ATT_SCALE)
        dk_raw, dkg = norm_bwd(k_ref[...], dk_ref[...], kg_ref[...], 1.0)
        o_ref[:, 0:D_ATT] = dq_raw.astype(BF16)
        o_ref[:, D_ATT:2 * D_ATT] = dk_raw.astype(BF16)
        o_ref[:, 2 * D_ATT:3 * D_ATT] = dv_ref[...].astype(BF16)

        @pl.when(i == 0)
        def _():
            dqg_ref[...] = dqg
            dkg_ref[...] = dkg

        @pl.when(i > 0)
        def _():
            dqg_ref[...] += dqg
            dkg_ref[...] += dkg

    col = lambda c: pl.BlockSpec((ts, D_ATT), lambda i: (i, c))
    vec = pl.BlockSpec((1, D_ATT), lambda i: (0, 0))
    return pl.pallas_call(
        body, name=name,
        out_shape=(jax.ShapeDtypeStruct((S, D_PROJ_A), BF16), jax.ShapeDtypeStruct((1, D_ATT), F32),
                   jax.ShapeDtypeStruct((1, D_ATT), F32)),
        grid=(S // ts,),
        in_specs=[col(0), col(1), col(0), col(0), col(0), vec, vec, pl.BlockSpec((LANES, LANES), lambda i: (0, 0))],
        out_specs=(pl.BlockSpec((ts, D_QKV), lambda i: (i, 0)), vec, vec),
        compiler_params=_params('arbitrary'),
    )(proj_a, proj_a, dq, dk, dv, qg, kg, _pair_ones())


def _tri_ones(upper):
    i = lax.broadcasted_iota(jnp.int32, (LANES, LANES), 0)
    j = lax.broadcasted_iota(jnp.int32, (LANES, LANES), 1)
    return ((i <= j) if upper else (i >= j)).astype(BF16)


def _forget_fwd(z_raw, b_col, *, name):
    R, S = z_raw.shape
    nb = S // LANES

    def body(z_ref, b_ref, u_ref, f_ref):
        u = u_ref[...]
        carry = jnp.zeros((R, 1), F32)
        for j in range(nb):
            z = z_ref[:, j * LANES:(j + 1) * LANES] + b_ref[...]
            logf = jnp.minimum(z, 0.0) - jnp.log(1.0 + jnp.exp(-jnp.abs(z)))
            f_ref[:, j * LANES:(j + 1) * LANES] = _dot3(logf, u) + carry
            carry = carry + jnp.sum(logf, axis=1, keepdims=True)

    return pl.pallas_call(
        body, name=name, out_shape=jax.ShapeDtypeStruct((R, S), F32),
        compiler_params=pltpu.CompilerParams(vmem_limit_bytes=VMEM_LIMIT_BYTES),
    )(z_raw, b_col, _tri_ones(True))


def _forget_bwd(z_raw, b_col, df, *, name):
    R, S = z_raw.shape
    nb = S // LANES

    def body(z_ref, b_ref, df_ref, l_ref, dz_ref, db_ref):
        low = l_ref[...]
        carry = jnp.zeros((R, 1), F32)
        db = jnp.zeros((R, 1), F32)
        for j in reversed(range(nb)):
            d = df_ref[:, j * LANES:(j + 1) * LANES]
            dlogf = _dot3(d, low) + carry
            carry = carry + jnp.sum(d, axis=1, keepdims=True)
            z = z_ref[:, j * LANES:(j + 1) * LANES] + b_ref[...]
            dz = dlogf * _sigmoid(-z)
            dz_ref[:, j * LANES:(j + 1) * LANES] = dz
            db = db + jnp.sum(dz, axis=1, keepdims=True)
        db_ref[...] = db

    return pl.pallas_call(
        body, name=name,
        out_shape=(jax.ShapeDtypeStruct((R, S), F32), jax.ShapeDtypeStruct((R, 1), F32)),
        compiler_params=pltpu.CompilerParams(vmem_limit_bytes=VMEM_LIMIT_BYTES),
    )(z_raw, b_col, df, _tri_ones(False))


def _head_mask(hh):
    lane = lax.broadcasted_iota(jnp.int32, (1, LANES), 1)
    return (lane // HEAD_DIM) == hh


def _causal(s, qi, ki, t):
    rows = qi * t + lax.broadcasted_iota(jnp.int32, (t, t), 0)
    cols = ki * t + lax.broadcasted_iota(jnp.int32, (t, t), 1)
    return jnp.where(cols <= rows, s, NEG)


AUG = 2 * HEAD_DIM


def _aug_consts():
    i = lax.broadcasted_iota(jnp.int32, (D_ATT, N_HEADS * AUG), 0)
    j = lax.broadcasted_iota(jnp.int32, (D_ATT, N_HEADS * AUG), 1)
    spread = (j == (i // HEAD_DIM) * AUG + i % HEAD_DIM).astype(BF16)
    h = lax.broadcasted_iota(jnp.int32, (LANES, N_HEADS * AUG), 0)
    c = lax.broadcasted_iota(jnp.int32, (LANES, N_HEADS * AUG), 1)
    gate = [((c == h * AUG + HEAD_DIM + t) & (h < N_HEADS)).astype(BF16) for t in range(3)]
    lane = lax.broadcasted_iota(jnp.int32, (1, N_HEADS * AUG), 1) % AUG
    ones_q = ((lane >= HEAD_DIM) & (lane < HEAD_DIM + 3)).astype(F32)
    ones_v = (lane == HEAD_DIM).astype(F32)
    return spread, gate, ones_q, ones_v


def _attn_aug(qkv, f_cum, *, name, ts=512):
    S = qkv.shape[0]
    ts = _tile(S, ts)
    spread, gate, ones_q, ones_v = _aug_consts()
    W = N_HEADS * AUG

    def body(q_ref, k_ref, v_ref, f_ref, sp_ref, g0_ref, g1_ref, g2_ref, oq_ref, ov_ref, qa_ref, ka_ref, va_ref):
        sp = sp_ref[...]
        qa_ref[...] = (_dot(q_ref[...], sp, 1, 0) + oq_ref[...]).astype(BF16)
        va_ref[...] = (_dot(v_ref[...], sp, 1, 0) + ov_ref[...]).astype(BF16)
        terms = _split3(-jnp.transpose(f_ref[...]))
        ka = _dot(k_ref[...], sp, 1, 0)
        for t, g_ref in zip(terms, (g0_ref, g1_ref, g2_ref)):
            ka = ka + _dot(t, g_ref[...], 1, 0)
        ka_ref[...] = ka.astype(BF16)

    col = lambda c: pl.BlockSpec((ts, D_ATT), lambda i: (i, c))
    full = lambda a: pl.BlockSpec(a.shape, lambda i: (0, 0))
    out = pl.BlockSpec((ts, W), lambda i: (i, 0))
    shape = jax.ShapeDtypeStruct((S, W), BF16)
    consts = (spread, *gate, ones_q, ones_v)
    return pl.pallas_call(
        body, name=name, out_shape=(shape, shape, shape), grid=(S // ts,),
        in_specs=[col(0), col(1), col(2), pl.BlockSpec((FG_ROWS, ts), lambda i: (0, i))] + [full(a) for a in consts],
        out_specs=(out, out, out),
        compiler_params=_params('parallel'),
    )(qkv, qkv, qkv, f_cum, *consts)


def _attn_fwd(qa, ka, va, *, name, tq=1024, tk=1024):
    S = qa.shape[0]
    tq, tk = _tile(S, tq), _tile(S, tk)
    nq, nk = S // tq, S // tk
    npair = N_HEADS // 2

    def body(q_ref, k_ref, v_ref, mix_ref, o_ref, lse_ref, m_s, acc_s):
        qi, ki = pl.program_id(1), pl.program_id(2)
        last = (qi * tq + tq - 1) // tk
        first_masked = (qi * tq) // tk

        @pl.when(ki == 0)
        def _():
            m_s[...] = jnp.full(m_s.shape, NEG, F32)
            acc_s[...] = jnp.zeros(acc_s.shape, F32)

        def step(masked):
            if masked:
                rows = qi * tq + lax.broadcasted_iota(jnp.int32, (tq, tk), 0)
                cols = ki * tk + lax.broadcasted_iota(jnp.int32, (tq, tk), 1)
                keep = cols <= rows
            m_prev = [m_s[hh] for hh in range(2)]
            acc_prev = [acc_s[hh] for hh in range(2)]
            ss = []
            for hh in range(2):
                s = _dot(q_ref[:, hh * AUG:(hh + 1) * AUG], k_ref[:, hh * AUG:(hh + 1) * AUG], 1, 1)
                ss.append(jnp.where(keep, s, NEG) if masked else s)
            m_new = [jnp.maximum(m_prev[hh], jnp.max(ss[hh], axis=1, keepdims=True)) for hh in range(2)]
            ps = [jnp.exp(ss[hh] - jnp.tile(m_new[hh], (1, tk // LANES))).astype(BF16) for hh in range(2)]
            for hh in range(2):
                alpha = jnp.exp(m_prev[hh] - m_new[hh])
                acc_s[hh] = alpha * acc_prev[hh] + _dot(ps[hh], v_ref[:, hh * AUG:(hh + 1) * AUG], 1, 0)
                m_s[hh] = m_new[hh]

        @pl.when(ki < first_masked)
        def _():
            step(False)

        @pl.when((ki >= first_masked) & (ki <= last))
        def _():
            step(True)

        @pl.when(ki == last)
        def _():
            lane = lax.broadcasted_iota(jnp.int32, (1, LANES), 1)
            outs, lses = [], []
            for hh in range(2):
                acc = acc_s[hh]
                denom = jnp.sum(jnp.where(lane == HEAD_DIM, acc, 0.0), axis=1, keepdims=True)
                outs.append(acc / denom)
                lses.append(m_s[hh] + jnp.log(denom))
            o = jnp.where(lane < HEAD_DIM, outs[0], pltpu.roll(outs[1], HEAD_DIM, 1))
            o_ref[...] = o
            mix_ref[...] = o.astype(BF16)
            lse_ref[...] = jnp.where(lane < HEAD_DIM, lses[0], lses[1])

    def kmap(h, i, j):
        return (jnp.minimum(j, (i * tq + tq - 1) // tk), h)

    out = pl.BlockSpec((tq, LANES), lambda h, i, j: (i, h))
    return pl.pallas_call(
        body, name=name,
        out_shape=(jax.ShapeDtypeStruct((S, D_MIX), BF16), jax.ShapeDtypeStruct((S, D_ATT), F32),
                   jax.ShapeDtypeStruct((S, D_ATT), F32)),
        grid=(npair, nq, nk),
        in_specs=[pl.BlockSpec((tq, 2 * AUG), lambda h, i, j: (i, h)),
                  pl.BlockSpec((tk, 2 * AUG), kmap), pl.BlockSpec((tk, 2 * AUG), kmap)],
        out_specs=(out, out, out),
        scratch_shapes=[pltpu.VMEM((2, tq, LANES), F32), pltpu.VMEM((2, tq, LANES), F32)],
        compiler_params=_params('parallel', 'parallel', 'arbitrary'),
    )(qa, ka, va)


def _attn_bwd(qkv, f3, att, lse, d_mix, *, name, t=1024):
    S = qkv.shape[0]
    t = _tile(S, t)
    n = S // t
    npair = N_HEADS // 2

    def body(q_ref, k_ref, v_ref, f_ref, o_ref, lse_ref, do_ref, dq_ref, dk_ref, dv_ref, df_ref, dr_ref, dk_s, dv_s, df_s):
        ki, qi = pl.program_id(1), pl.program_id(2)

        @pl.when(qi == ki)
        def _():
            dk_s[...] = jnp.zeros(dk_s.shape, F32)
            dv_s[...] = jnp.zeros(dv_s.shape, F32)
            df_s[...] = jnp.zeros(df_s.shape, F32)

        def step(masked):
            q, k, v = q_ref[...], k_ref[...], v_ref[...]
            do, o, lse = do_ref[...], o_ref[...], lse_ref[...]
            lane = lax.broadcasted_iota(jnp.int32, (1, LANES), 1)
            lse_sw = pltpu.roll(lse, HEAD_DIM, 1)
            delta = _dot3(do.astype(BF16).astype(F32) * o, _pair_ones())
            delta_sw = pltpu.roll(delta, HEAD_DIM, 1)
            dq_blk = jnp.zeros((t, LANES), F32)
            dr_blk = jnp.zeros((t, LANES), F32)
            for hh in range(2):
                msk = _head_mask(hh)
                first = lane < HEAD_DIM if hh == 0 else lane >= HEAD_DIM
                qm = jnp.where(msk, q, jnp.zeros_like(q))
                km = jnp.where(msk, k, jnp.zeros_like(k))
                do_h = jnp.where(msk, do, 0.0)
                dom = do_h.astype(BF16)
                s = _dot(qm, k, 1, 1) - f_ref[0, hh:hh + 1, :]
                if masked:
                    s = _causal(s, qi, ki, t)
                lse_h = jnp.where(first, lse, lse_sw)
                delta_h = jnp.where(first, delta, delta_sw)
                p = jnp.exp(s - jnp.tile(lse_h, (1, t // LANES)))
                dp = _dot(dom, v, 1, 1)
                ds = p * (dp - jnp.tile(delta_h, (1, t // LANES)))
                dsb = ds.astype(BF16)
                dv_s[...] += _dot(jnp.transpose(do_h).astype(BF16), p.astype(BF16), 1, 0)
                dk_s[...] += _dot(jnp.transpose(qm.astype(F32)).astype(BF16), dsb, 1, 0)
                dq_blk = dq_blk + _dot(dsb, km, 1, 0)
                df_s[hh] -= jnp.sum(ds, axis=0, keepdims=True)
                dr_blk = dr_blk + jnp.where(msk, jnp.sum(ds, axis=1, keepdims=True), 0.0)
            rows = pl.ds(pl.multiple_of(qi * t, t), t)

            @pl.when(ki == 0)
            def _():
                dq_ref[rows, :] = dq_blk
                dr_ref[rows, :] = dr_blk

            @pl.when(ki > 0)
            def _():
                dq_ref[rows, :] += dq_blk
                dr_ref[rows, :] += dr_blk

        @pl.when(qi > ki)
        def _():
            step(False)

        @pl.when(qi == ki)
        def _():
            step(True)

        @pl.when(qi == n - 1)
        def _():
            dk_ref[...] = jnp.transpose(dk_s[...])
            dv_ref[...] = jnp.transpose(dv_s[...])
            df_ref[0, 0:1, :] = df_s[0]
            df_ref[0, 1:2, :] = df_s[1]

    qrow = lambda h, j, i: (jnp.maximum(i, j), h)
    return pl.pallas_call(
        body, name=name,
        out_shape=(jax.ShapeDtypeStruct((S, D_ATT), F32), jax.ShapeDtypeStruct((S, D_ATT), F32),
                   jax.ShapeDtypeStruct((S, D_ATT), F32), jax.ShapeDtypeStruct((npair, 2, S), F32),
                   jax.ShapeDtypeStruct((S, D_ATT), F32)),
        grid=(npair, n, n),
        in_specs=[pl.BlockSpec((t, LANES), qrow),
                  pl.BlockSpec((t, LANES), lambda h, j, i: (j, npair + h)),
                  pl.BlockSpec((t, LANES), lambda h, j, i: (j, 2 * npair + h)),
                  pl.BlockSpec((1, 2, t), lambda h, j, i: (h, 0, j)),
                  pl.BlockSpec((t, LANES), qrow),
                  pl.BlockSpec((t, LANES), qrow),
                  pl.BlockSpec((t, LANES), qrow)],
        out_specs=(pl.BlockSpec((S, LANES), lambda h, j, i: (0, h)),
                   pl.BlockSpec((t, LANES), lambda h, j, i: (j, h)),
                   pl.BlockSpec((t, LANES), lambda h, j, i: (j, h)),
                   pl.BlockSpec((1, 2, t), lambda h, j, i: (h, 0, j)),
                   pl.BlockSpec((S, LANES), lambda h, j, i: (0, h))),
        scratch_shapes=[pltpu.VMEM((LANES, t), F32), pltpu.VMEM((LANES, t), F32), pltpu.VMEM((2, 1, t), F32)],
        compiler_params=_params('parallel', 'arbitrary', 'arbitrary'),
    )(qkv, qkv, qkv, f3, att, lse, d_mix)


A_COL = D_QKV // D_CONV
B_COL = A_COL + 1
P_COL = B_COL + 1


CONV_BLOCKS = D_CONV // LANES
CONV_GROUP = 8 * 8


def _rows8(ref, c, row):
    return ref.at[c][pl.ds(row, 8, stride=8), :]


def _put8(ref, c, row, val):
    ref.at[c][pl.ds(row, 8, stride=8), :] = val


def _lanes(c):
    return slice(c * LANES, (c + 1) * LANES)


def _glu_into(buf, a_ref, b_ref, ah_ref, bh_ref, first, ts):
    for c in range(CONV_BLOCKS):
        halo = ah_ref[:, _lanes(c)] * _sigmoid(bh_ref[:, _lanes(c)])
        buf[c, 0:CONV_HALO, :] = jnp.where(first, 0.0, halo)
        buf[c, CONV_HALO:CONV_HALO + ts, :] = a_ref[:, _lanes(c)] * _sigmoid(b_ref[:, _lanes(c)])


def _conv_taps(buf, c, r0):
    return [_rows8(buf, c, CONV_HALO + r0 + i - (CONV_WIDTH - 1)) for i in range(CONV_WIDTH - 1 + 8)]


def _dwconv8(xs, ws, bias):
    outs = []
    for j in range(8):
        acc = ws[0] * xs[j]
        for k in range(1, CONV_WIDTH):
            acc = acc + ws[k] * xs[j + k]
        outs.append(acc + bias)
    return outs


def _ln8(cs):
    inv = 1.0 / D_CONV
    mu = sum(jnp.sum(c, axis=1, keepdims=True) for c in cs) * inv
    xc = [c - mu for c in cs]
    rstd = lax.rsqrt(sum(jnp.sum(x * x, axis=1, keepdims=True) for x in xc) * inv + EPS)
    return [x * rstd for x in xc], rstd


def _conv_specs(ts, tmap):
    hb = ts // CONV_HALO
    cur = lambda c: pl.BlockSpec((ts, D_CONV), lambda i: (tmap(i), c))
    halo = lambda c: pl.BlockSpec((CONV_HALO, D_CONV), lambda i: (jnp.maximum(tmap(i) * hb - 1, 0), c))
    return cur, halo


def _conv_fwd(proj_a, mix, dw_w, dw_b, ln_g, ln_b, pw_w, l, *, name, ts=512):
    S = proj_a.shape[0]
    ts = _tile(S, ts, CONV_GROUP)

    def body(a_ref, b_ref, ah_ref, bh_ref, w_ref, wb_ref, g_ref, bb_ref, pw_ref, mix_in, o_ref, buf, stage):
        _glu_into(buf, a_ref, b_ref, ah_ref, bh_ref, pl.program_id(0) == 0, ts)
        ws = [[w_ref[k:k + 1, _lanes(c)] for k in range(CONV_WIDTH)] for c in range(CONV_BLOCKS)]
        for r0 in range(0, ts, CONV_GROUP):
            conv = [_dwconv8(_conv_taps(buf, c, r0), ws[c], wb_ref[:, _lanes(c)]) for c in range(CONV_BLOCKS)]
            for j in range(8):
                yhat, _ = _ln8([conv[c][j] for c in range(CONV_BLOCKS)])
                for c in range(CONV_BLOCKS):
                    y = yhat[c] * g_ref[:, _lanes(c)] + bb_ref[:, _lanes(c)]
                    _put8(stage, c, r0 + j, y * _sigmoid(y))
        hs = jnp.concatenate([stage[c] for c in range(CONV_BLOCKS)], axis=1)
        o_ref[...] = _dot(hs.astype(BF16), pw_ref[...], 1, 0).astype(BF16)

    cur, halo = _conv_specs(ts, lambda i: i)
    vec = pl.BlockSpec((1, D_CONV), lambda i: (0, 0))
    return pl.pallas_call(
        body, name=name, out_shape=jax.ShapeDtypeStruct(mix.shape, BF16), grid=(S // ts,),
        in_specs=[cur(A_COL), cur(B_COL), halo(A_COL), halo(B_COL),
                  pl.BlockSpec((None, CONV_HALO, D_CONV), lambda i: (l, 0, 0)), vec, vec, vec,
                  pl.BlockSpec((None, D_CONV, D_CONV), lambda i: (l, 0, 0)), ANY_SPEC],
        out_specs=pl.BlockSpec((ts, D_CONV), lambda i: (i, D_ATT // D_CONV)),
        scratch_shapes=[pltpu.VMEM((CONV_BLOCKS, CONV_HALO + ts, LANES), F32), pltpu.VMEM((CONV_BLOCKS, ts, LANES), F32)],
        input_output_aliases={9: 0},
        compiler_params=_params('parallel'),
    )(proj_a, proj_a, proj_a, proj_a, dw_w, dw_b, ln_g, ln_b, pw_w, mix)


def _conv_bwd(proj_a, d_mix, d_proj, dw_w, dw_b, ln_g, ln_b, pw_w, l, *, name, ts=512):
    S = proj_a.shape[0]
    ts = _tile(S, ts, CONV_GROUP)
    n = S // ts
    d_col = D_ATT // D_CONV
    groups = range(0, ts, CONV_GROUP)

    def body(a_ref, b_ref, ah_ref, bh_ref, dy_ref, w_ref, wb_ref, g_ref, bb_ref, pw_ref, dp_in,
             o_ref, dw_ref, dwb_ref, dg_ref, dbb_ref, dpw_ref, buf, dcbuf, stage, stage2):
        i = pl.program_id(0)
        _glu_into(buf, a_ref, b_ref, ah_ref, bh_ref, i == n - 1, ts)

        @pl.when(i == 0)
        def _():
            dcbuf[:, ts:ts + CONV_HALO, :] = jnp.zeros((CONV_BLOCKS, CONV_HALO, LANES), F32)
            dw_ref[...] = jnp.zeros(dw_ref.shape, F32)
            dwb_ref[...] = jnp.zeros(dwb_ref.shape, F32)
            dg_ref[...] = jnp.zeros(dg_ref.shape, F32)
            dbb_ref[...] = jnp.zeros(dbb_ref.shape, F32)
            dpw_ref[...] = jnp.zeros(dpw_ref.shape, F32)

        dout = dy_ref[...].astype(BF16)
        d_hs = _dot(dout, pw_ref[...], 1, 1)
        for c in range(CONV_BLOCKS):
            stage2[c, :, :] = d_hs[:, _lanes(c)]
        ws = [[w_ref[k:k + 1, _lanes(c)] for k in range(CONV_WIDTH)] for c in range(CONV_BLOCKS)]
        zero8 = jnp.zeros((8, LANES), F32)
        dg = [zero8] * CONV_BLOCKS
        dbb = [zero8] * CONV_BLOCKS
        dwb = [zero8] * CONV_BLOCKS
        for r0 in groups:
            conv = [_dwconv8(_conv_taps(buf, c, r0), ws[c], wb_ref[:, _lanes(c)]) for c in range(CONV_BLOCKS)]
            for j in range(8):
                yhat, rstd = _ln8([conv[c][j] for c in range(CONV_BLOCKS)])
                d_yhat = []
                for c in range(CONV_BLOCKS):
                    y = yhat[c] * g_ref[:, _lanes(c)] + bb_ref[:, _lanes(c)]
                    sg = _sigmoid(y)
                    _put8(stage, c, r0 + j, y * sg)
                    d_y = _rows8(stage2, c, r0 + j) * (sg * (1.0 + y * (1.0 - sg)))
                    dg[c] = dg[c] + d_y * yhat[c]
                    dbb[c] = dbb[c] + d_y
                    d_yhat.append(d_y * g_ref[:, _lanes(c)])
                inv = 1.0 / D_CONV
                m1 = sum(jnp.sum(d, axis=1, keepdims=True) for d in d_yhat) * inv
                m2 = sum(jnp.sum(d * yh, axis=1, keepdims=True) for d, yh in zip(d_yhat, yhat)) * inv
                for c in range(CONV_BLOCKS):
                    d_c = rstd * (d_yhat[c] - m1 - yhat[c] * m2)
                    dwb[c] = dwb[c] + d_c
                    _put8(dcbuf, c, r0 + j, d_c)
        for c in range(CONV_BLOCKS):
            dg_ref[:, _lanes(c)] += jnp.sum(dg[c], axis=0, keepdims=True)
            dbb_ref[:, _lanes(c)] += jnp.sum(dbb[c], axis=0, keepdims=True)
            dwb_ref[:, _lanes(c)] += jnp.sum(dwb[c], axis=0, keepdims=True)
        hs = jnp.concatenate([stage[c] for c in range(CONV_BLOCKS)], axis=1)
        dpw_ref[...] += _dot(hs.astype(BF16), dout, 0, 0)
        for c in range(CONV_BLOCKS):
            for r0 in groups:
                dcs = [_rows8(dcbuf, c, r0 + i_) for i_ in range(CONV_WIDTH - 1 + 8)]
                for j in range(8):
                    acc = ws[c][0] * dcs[j + CONV_WIDTH - 1]
                    for k in range(1, CONV_WIDTH):
                        acc = acc + ws[c][k] * dcs[j + CONV_WIDTH - 1 - k]
                    _put8(stage2, c, r0 + j, acc)
            for k in range(CONV_WIDTH):
                acc = zero8
                for r0 in groups:
                    for j in range(8):
                        acc = acc + _rows8(dcbuf, c, r0 + j) * _rows8(buf, c, CONV_HALO + r0 + j - (CONV_WIDTH - 1) + k)
                dw_ref[k:k + 1, _lanes(c)] += jnp.sum(acc, axis=0, keepdims=True)
            dcbuf[c, ts:ts + CONV_HALO, :] = dcbuf[c, 0:CONV_HALO, :]
        d_h = jnp.concatenate([stage2[c] for c in range(CONV_BLOCKS)], axis=1)
        a, sb = a_ref[...], _sigmoid(b_ref[...])
        o_ref[:, 0:D_CONV] = (d_h * sb).astype(BF16)
        o_ref[:, D_CONV:2 * D_CONV] = (d_h * a * sb * (1.0 - sb)).astype(BF16)

    rev = lambda i: n - 1 - i
    cur, halo = _conv_specs(ts, rev)
    vec = pl.BlockSpec((1, D_CONV), lambda i: (0, 0))
    wspec = pl.BlockSpec((CONV_HALO, D_CONV), lambda i: (0, 0))
    sq = pl.BlockSpec((D_CONV, D_CONV), lambda i: (0, 0))
    tile3 = pltpu.VMEM((CONV_BLOCKS, ts, LANES), F32)
    return pl.pallas_call(
        body, name=name,
        out_shape=(jax.ShapeDtypeStruct(d_proj.shape, BF16), jax.ShapeDtypeStruct((CONV_HALO, D_CONV), F32),
                   jax.ShapeDtypeStruct((1, D_CONV), F32), jax.ShapeDtypeStruct((1, D_CONV), F32),
                   jax.ShapeDtypeStruct((1, D_CONV), F32), jax.ShapeDtypeStruct((D_CONV, D_CONV), F32)),
        grid=(n,),
        in_specs=[cur(A_COL), cur(B_COL), halo(A_COL), halo(B_COL),
                  pl.BlockSpec((ts, D_CONV), lambda i: (rev(i), d_col)),
                  pl.BlockSpec((None, CONV_HALO, D_CONV), lambda i: (l, 0, 0)), vec, vec, vec,
                  pl.BlockSpec((None, D_CONV, D_CONV), lambda i: (l, 0, 0)), ANY_SPEC],
        out_specs=(pl.BlockSpec((ts, 2 * D_CONV), lambda i: (rev(i), D_QKV // (2 * D_CONV))), wspec, vec, vec, vec, sq),
        scratch_shapes=[pltpu.VMEM((CONV_BLOCKS, CONV_HALO + ts, LANES), F32),
                        pltpu.VMEM((CONV_BLOCKS, ts + CONV_HALO, LANES), F32), tile3, tile3],
        input_output_aliases={10: 0},
        compiler_params=_params('arbitrary'),
    )(proj_a, proj_a, proj_a, proj_a, d_mix, dw_w, dw_b, ln_g, ln_b, pw_w, d_proj)


POOL_BLOCKS = D_POOL // LANES
POOL_SPAN = max(POOL_WINDOWS) - 1


def _pool_sum8(xs, j, c, step):
    lo, hi = POOL_WINDOWS[2 * c], POOL_WINDOWS[2 * c + 1]
    acc = xs[j]
    for d in range(1, lo):
        acc = acc + xs[j + step * d]
    more = xs[j + step * lo]
    for d in range(lo + 1, hi):
        more = more + xs[j + step * d]
    lane = lax.broadcasted_iota(jnp.int32, (1, LANES), 1)
    return acc + jnp.where(lane >= POOL_GROUP, more, 0.0)


def _pool_count8(c, row):
    lane = lax.broadcasted_iota(jnp.int32, (1, LANES), 1)
    wl = jnp.where(lane >= POOL_GROUP, POOL_WINDOWS[2 * c + 1], POOL_WINDOWS[2 * c])
    pos = row + 8 * lax.broadcasted_iota(jnp.int32, (8, 1), 0)
    return jnp.minimum(pos + 1, wl).astype(F32)


def _pool_diff_into(stage, buf, u_ref, uh_ref, first, tile, ts):
    for c in range(POOL_BLOCKS):
        buf[c, 0:POOL_HALO, :] = jnp.where(first, 0.0, uh_ref[:, _lanes(c)])
        buf[c, POOL_HALO:POOL_HALO + ts, :] = u_ref[:, _lanes(c)]
        for r0 in range(0, ts, CONV_GROUP):
            xs = [_rows8(buf, c, POOL_HALO + r0 + i - POOL_SPAN) for i in range(POOL_SPAN + 8)]
            for j in range(8):
                mean = _pool_sum8(xs, j + POOL_SPAN, c, -1) / _pool_count8(c, tile * ts + r0 + j)
                _put8(stage, c, r0 + j, mean - xs[j + POOL_SPAN])


def _pool_specs(ts, tmap):
    hb = ts // POOL_HALO
    cur = pl.BlockSpec((ts, D_POOL), lambda i: (tmap(i), P_COL))
    halo = pl.BlockSpec((POOL_HALO, D_POOL), lambda i: (jnp.maximum(tmap(i) * hb - 1, 0), P_COL))
    return cur, halo


def _pool_fwd(proj_a, mix, wbd, scale, *, name, ts=512):
    S = proj_a.shape[0]
    ts = _tile(S, ts, CONV_GROUP)

    def body(u_ref, uh_ref, w_ref, s_ref, mix_in, o_ref, buf, stage):
        i = pl.program_id(0)
        _pool_diff_into(stage, buf, u_ref, uh_ref, i == 0, i, ts)
        d = jnp.concatenate([stage[c] for c in range(POOL_BLOCKS)], axis=1)
        o_ref[...] = (_dot(d.astype(BF16), w_ref[...], 1, 0) * s_ref[...]).astype(BF16)

    cur, halo = _pool_specs(ts, lambda i: i)
    return pl.pallas_call(
        body, name=name, out_shape=jax.ShapeDtypeStruct(mix.shape, BF16), grid=(S // ts,),
        in_specs=[cur, halo, pl.BlockSpec((D_POOL, D_POOL), lambda i: (0, 0)), pl.BlockSpec((1, D_POOL), lambda i: (0, 0)),
                  ANY_SPEC],
        out_specs=pl.BlockSpec((ts, D_POOL), lambda i: (i, (D_ATT + D_CONV) // D_POOL)),
        scratch_shapes=[pltpu.VMEM((POOL_BLOCKS, POOL_HALO + ts, LANES), F32), pltpu.VMEM((POOL_BLOCKS, ts, LANES), F32)],
        input_output_aliases={4: 0},
        compiler_params=_params('parallel'),
    )(proj_a, proj_a, wbd, scale, mix)


def _pool_bwd(proj_a, d_mix, d_proj, wbd, scale, *, name, ts=512):
    S = proj_a.shape[0]
    ts = _tile(S, ts, CONV_GROUP)
    n = S // ts
    d_col = (D_ATT + D_CONV) // D_POOL

    def body(u_ref, uh_ref, dy_ref, w_ref, s_ref, dp_in, o_ref, dw_ref, ds_ref, buf, ebuf, stage):
        i = pl.program_id(0)
        tile = n - 1 - i
        _pool_diff_into(stage, buf, u_ref, uh_ref, tile == 0, tile, ts)
        db = jnp.concatenate([stage[c] for c in range(POOL_BLOCKS)], axis=1).astype(BF16)
        ypre = _dot(db, w_ref[...], 1, 0)
        dout = dy_ref[...]
        d_y = (dout * s_ref[...]).astype(BF16)
        d_d = _dot(d_y, w_ref[...], 1, 1)

        @pl.when(i == 0)
        def _():
            ebuf[:, ts:ts + POOL_HALO, :] = jnp.zeros((POOL_BLOCKS, POOL_HALO, LANES), F32)
            dw_ref[...] = jnp.zeros(dw_ref.shape, F32)
            ds_ref[...] = jnp.zeros(ds_ref.shape, F32)

        dw_ref[...] += _dot(db, d_y, 0, 0)
        ds_ref[...] += jnp.sum(dout * ypre, axis=0, keepdims=True)
        for c in range(POOL_BLOCKS):
            stage[c, :, :] = d_d[:, _lanes(c)]
            for r0 in range(0, ts, CONV_GROUP):
                for j in range(8):
                    _put8(ebuf, c, r0 + j, _rows8(stage, c, r0 + j) / _pool_count8(c, tile * ts + r0 + j))
            for r0 in range(0, ts, CONV_GROUP):
                es = [_rows8(ebuf, c, r0 + i_) for i_ in range(POOL_SPAN + 8)]
                for j in range(8):
                    _put8(stage, c, r0 + j, _pool_sum8(es, j, c, 1) - _rows8(stage, c, r0 + j))
            ebuf[c, ts:ts + POOL_HALO, :] = ebuf[c, 0:POOL_HALO, :]
        o_ref[...] = jnp.concatenate([stage[c] for c in range(POOL_BLOCKS)], axis=1).astype(BF16)

    rev = lambda i: n - 1 - i
    cur, halo = _pool_specs(ts, rev)
    sq = pl.BlockSpec((D_POOL, D_POOL), lambda i: (0, 0))
    vec = pl.BlockSpec((1, D_POOL), lambda i: (0, 0))
    return pl.pallas_call(
        body, name=name,
        out_shape=(jax.ShapeDtypeStruct(d_proj.shape, BF16), jax.ShapeDtypeStruct((D_POOL, D_POOL), F32),
                   jax.ShapeDtypeStruct((1, D_POOL), F32)),
        grid=(n,),
        in_specs=[cur, halo, pl.BlockSpec((ts, D_POOL), lambda i: (rev(i), d_col)), sq, vec, ANY_SPEC],
        out_specs=(pl.BlockSpec((ts, D_POOL), lambda i: (rev(i), P_COL)), sq, vec),
        scratch_shapes=[pltpu.VMEM((POOL_BLOCKS, POOL_HALO + ts, LANES), F32),
                        pltpu.VMEM((POOL_BLOCKS, ts + POOL_HALO, LANES), F32), pltpu.VMEM((POOL_BLOCKS, ts, LANES), F32)],
        input_output_aliases={5: 0},
        compiler_params=_params('arbitrary'),
    )(proj_a, proj_a, d_mix, wbd, scale, d_proj)


FFN_LANES = 128
FFN_GROUP = 8 * 8


def _ffn_rows(ref, c, row0, j):
    return ref.at[c][pl.ds(row0 + j, 8, stride=8), :]


def _ffn_specs(ts, tc2, tmap, l):
    hb = ts // FFN_HALO
    cur = pl.BlockSpec((ts, tc2), lambda c, i: (tmap(i), c))
    halo = pl.BlockSpec((FFN_HALO, tc2), lambda c, i: (jnp.maximum(tmap(i) * hb - 1, 0), c))
    wspec = pl.BlockSpec((None, FFN_HALO, tc2), lambda c, i: (l, 0, c))
    return cur, halo, wspec


def _ffn_fill(buf, x_ref, xh_ref, first, ts, nblk):
    for c in range(nblk):
        cs = slice(c * FFN_LANES, (c + 1) * FFN_LANES)
        buf[c, 0:FFN_HALO, :] = jnp.where(first, 0.0, xh_ref[:, cs])
        buf[c, FFN_HALO:FFN_HALO + ts, :] = x_ref[:, cs]


def _ffn_conv_piece(buf, w_ref, r0, c):
    ws = [w_ref[k:k + 1, c * FFN_LANES:(c + 1) * FFN_LANES] for k in range(FFN_CONV_WIDTH)]
    xs = [_ffn_rows(buf, c, FFN_HALO + r0, j) for j in range(1 - FFN_CONV_WIDTH, 8)]
    outs = []
    for j in range(8):
        acc = ws[0] * xs[j]
        for k in range(1, FFN_CONV_WIDTH):
            acc = acc + ws[k] * xs[j + k]
        outs.append(acc)
    return outs, xs


def _ffn_act_fwd(up, w, l, *, name, ts=512):
    S, F2 = up.shape
    tc = F2 // 4
    nb = tc // FFN_LANES
    ts = _tile(S, ts, FFN_GROUP)

    def body(x_ref, xh_ref, w_ref, o_ref, buf, stage):
        _ffn_fill(buf, x_ref, xh_ref, pl.program_id(1) == 0, ts, 2 * nb)
        for c in range(nb):
            for r0 in range(0, ts, FFN_GROUP):
                gates, _ = _ffn_conv_piece(buf, w_ref, r0, c)
                vals, _ = _ffn_conv_piece(buf, w_ref, r0, nb + c)
                for j in range(8):
                    stage.at[c][pl.ds(r0 + j, 8, stride=8), :] = gates[j] * _sigmoid(gates[j]) * vals[j]
            o_ref[:, c * FFN_LANES:(c + 1) * FFN_LANES] = stage[c].astype(BF16)

    cur, halo, wspec = _ffn_specs(ts, 2 * tc, lambda i: i, l)
    return pl.pallas_call(
        body, name=name, out_shape=jax.ShapeDtypeStruct((S, F2 // 2), BF16), grid=(2, S // ts),
        in_specs=[cur, halo, wspec],
        out_specs=pl.BlockSpec((ts, tc), lambda c, i: (i, c)),
        scratch_shapes=[pltpu.VMEM((2 * nb, FFN_HALO + ts, FFN_LANES), F32), pltpu.VMEM((nb, ts, FFN_LANES), F32)],
        compiler_params=_params('parallel', 'parallel'),
    )(up, up, w)


def _ffn_act_bwd(up, d_act, w, l, *, name, ts=512):
    S, F2 = up.shape
    tc = F2 // 4
    nb = tc // FFN_LANES
    ts = _tile(S, ts, FFN_GROUP)
    n = S // ts

    def body(x_ref, xh_ref, da_ref, w_ref, o_ref, dw_ref, buf, dcbuf, stage):
        i = pl.program_id(1)
        _ffn_fill(buf, x_ref, xh_ref, i == n - 1, ts, 2 * nb)

        @pl.when(i == 0)
        def _():
            dcbuf[:, ts:ts + FFN_HALO, :] = jnp.zeros((2 * nb, FFN_HALO, FFN_LANES), F32)
            dw_ref[...] = jnp.zeros(dw_ref.shape, F32)

        for c in range(nb):
            blocks = (c, nb + c)
            stage[c, :, :] = da_ref[:, c * FFN_LANES:(c + 1) * FFN_LANES]
            dws = [[jnp.zeros((8, FFN_LANES), F32) for _ in range(FFN_CONV_WIDTH)] for _ in range(2)]
            for r0 in range(0, ts, FFN_GROUP):
                gates, xg = _ffn_conv_piece(buf, w_ref, r0, blocks[0])
                vals, xv = _ffn_conv_piece(buf, w_ref, r0, blocks[1])
                for j in range(8):
                    sg = _sigmoid(gates[j])
                    da = _ffn_rows(stage, c, r0, j)
                    d_cs = (da * vals[j] * (sg * (1.0 + gates[j] * (1.0 - sg))), da * (gates[j] * sg))
                    for half, (d_c, xs) in enumerate(zip(d_cs, (xg, xv))):
                        dcbuf.at[blocks[half]][pl.ds(r0 + j, 8, stride=8), :] = d_c
                        for k in range(FFN_CONV_WIDTH):
                            dws[half][k] = dws[half][k] + d_c * xs[j + k]
            for half in range(2):
                cs = slice(blocks[half] * FFN_LANES, (blocks[half] + 1) * FFN_LANES)
                for k in range(FFN_CONV_WIDTH):
                    dw_ref[k:k + 1, cs] += jnp.sum(dws[half][k], axis=0, keepdims=True)
            for b in blocks:
                cs = slice(b * FFN_LANES, (b + 1) * FFN_LANES)
                ws = [w_ref[k:k + 1, cs] for k in range(FFN_CONV_WIDTH)]
                for r0 in range(0, ts, FFN_GROUP):
                    ds = [_ffn_rows(dcbuf, b, r0, j) for j in range(8 + FFN_CONV_WIDTH - 1)]
                    for j in range(8):
                        d_x = ws[FFN_CONV_WIDTH - 1] * ds[j]
                        for k in range(FFN_CONV_WIDTH - 1):
                            d_x = d_x + ws[k] * ds[j + FFN_CONV_WIDTH - 1 - k]
                        stage.at[c][pl.ds(r0 + j, 8, stride=8), :] = d_x
                o_ref[:, cs] = stage[c].astype(BF16)
                dcbuf[b, ts:ts + FFN_HALO, :] = dcbuf[b, 0:FFN_HALO, :]

    rev = lambda i: n - 1 - i
    cur, halo, wspec = _ffn_specs(ts, 2 * tc, rev, l)
    return pl.pallas_call(
        body, name=name,
        out_shape=(jax.ShapeDtypeStruct((S, F2), BF16), jax.ShapeDtypeStruct((FFN_HALO, F2), F32)),
        grid=(2, n),
        in_specs=[cur, halo, pl.BlockSpec((ts, tc), lambda c, i: (rev(i), c)), wspec],
        out_specs=(cur, pl.BlockSpec((FFN_HALO, 2 * tc), lambda c, i: (0, c))),
        scratch_shapes=[pltpu.VMEM((2 * nb, FFN_HALO + ts, FFN_LANES), F32), pltpu.VMEM((2 * nb, ts + FFN_HALO, FFN_LANES), F32),
                        pltpu.VMEM((nb, ts, FFN_LANES), F32)],
        compiler_params=_params('parallel', 'arbitrary'),
    )(up, up, d_act, w)


def _loss_head(y, target, *, name, ts=512):
    S, D = y.shape
    ts = _tile(S, ts, 8)

    def body(y_ref, t_ref, l_ref, dy_ref):
        i = pl.program_id(0)
        err = y_ref[...] - t_ref[...]
        dy_ref[...] = err * (1.0 / D)
        part = jnp.sum(jnp.sum(err * err, axis=1, keepdims=True), axis=0, keepdims=True) * (0.5 / D)

        @pl.when(i == 0)
        def _():
            l_ref[...] = part

        @pl.when(i > 0)
        def _():
            l_ref[...] += part

    row = pl.BlockSpec((ts, D), lambda i: (i, 0))
    return pl.pallas_call(
        body, name=name,
        out_shape=(jax.ShapeDtypeStruct((1, 1), F32), jax.ShapeDtypeStruct((S, D), F32)),
        grid=(S // ts,), in_specs=[row, row], out_specs=(pl.BlockSpec((1, 1), lambda i: (0, 0)), row),
        compiler_params=_params('arbitrary'),
    )(y, target)


def _pair_cols(w):
    lead, f2 = w.shape[:-1], w.shape[-1]
    return w.reshape(lead + (2, 2, f2 // 4)).swapaxes(-3, -2).reshape(lead + (f2,))


def _pad_axis(w, size, axis):
    pad = [(0, 0)] * w.ndim
    pad[axis] = (0, size - w.shape[axis])
    return jnp.pad(w, pad)


def _block_diag(pool_w):
    g = pool_w.shape[0]
    rows = [jnp.concatenate([pool_w[i] if i == j else jnp.zeros_like(pool_w[i]) for j in range(g)], axis=1) for i in range(g)]
    return jnp.concatenate(rows, axis=0)


def _small_weights(w, l):
    return dict(
        norm1_g=w['norm1_g'][l][None, :],
        b_col=_pad_axis(w['b_f'][l][:, None], FG_ROWS, 0),
        qg=jnp.tile(w['q_norm_g'][l], N_HEADS)[None, :],
        kg=jnp.tile(w['k_norm_g'][l], N_HEADS)[None, :],
        dw_b=w['conv_dw_b'][l][None, :], ln_g=w['conv_ln_g'][l][None, :], ln_b=w['conv_ln_b'][l][None, :],
        wbd=_block_diag(w['pool_w'][l]).astype(BF16),
        pool_scale=w['pool_scale'][l][None, :],
        norm2_g=w['norm2_g'][l][None, :],
    )


def _layer_fwd(x, W, p, l):
    n = lambda s: f'l{l}_{s}'
    S = x.shape[0]
    h = _rms_fwd(x, p['norm1_g'], name=n('norm1'), after=W.started)
    proj_a = _mm(h, W.get('w_a', h), b_lead=0, name=n('proj_a'), tn=D_PROJ_A)
    z_raw = _mm(W.get('w_fg_t', h), h, a_lead=0, tb=True, name=n('proj_fg'))
    qkv = _qk_prep_fwd(proj_a, p['qg'], p['kg'], name=n('qk_norm'))
    f_cum = _forget_fwd(z_raw, p['b_col'], name=n('forget'))
    f3 = f_cum[:N_HEADS].reshape(N_HEADS // 2, 2, S)
    mix, att, lse = _attn_fwd(*_attn_aug(qkv, f_cum, name=n('attn_aug')), name=n('attn'))
    mix = _conv_fwd(proj_a, mix, W.get('dw_w', h), p['dw_b'], p['ln_g'], p['ln_b'], W.get('pw_w', h), 0, name=n('conv'))
    mix = _pool_fwd(proj_a, mix, p['wbd'], p['pool_scale'], name=n('pool'))
    x1 = _mm(mix, W.get('w_out', mix), b_lead=0, res=x, name=n('out_proj'), tn=1024)
    h2 = _rms_fwd(x1, p['norm2_g'], name=n('norm2'))
    up = _mm(h2, W.get('w_up', mix), b_lead=0, name=n('up_proj'), tn=1408, cols_outer=True)
    act = _ffn_act_fwd(up, W.get('ffn_w', h), 0, name=n('ffn_act'))
    x2 = _mm(act, W.get('w_down', mix), b_lead=0, res=x1, name=n('down_proj'), tn=1024, tk=2816)
    saved = dict(x=x, h=h, proj_a=proj_a, z_raw=z_raw, qkv=qkv, f3=f3, att=att, lse=lse, mix=mix, x1=x1, h2=h2, up=up, act=act)
    return x2, saved


def _layer_bwd(dx2, W, p, s, l, sink):
    n = lambda t: f'l{l}_{t}'
    S = dx2.shape[0]
    g = {}
    W = W.ready

    def large(key, a, b, **kw):
        return sink.put(l, key, *_mm(a, b, ta=True, copy16=True, name=n('d_' + key), **kw))

    d_act = _mm(dx2, W['w_down'], b_lead=0, tb=True, name=n('d_act'), tn=1408, cols_outer=True)
    large('w_down', s['act'], dx2, tm=1408, tn=1024)
    d_up, d_ffn_w = _ffn_act_bwd(s['up'], d_act, W['ffn_w'], 0, name=n('ffn_act_bwd'))
    g['ffn_dw_w'] = _pair_cols(d_ffn_w[:FFN_CONV_WIDTH])
    d_h2 = _mm(d_up, W['w_up'], b_lead=0, tb=True, name=n('d_h2'), tn=1024, tk=5632)
    started = large('w_up', s['h2'], d_up, tm=1024, tn=512, tk=4096)
    dx1, dg2 = _rms_bwd(s['x1'], p['norm2_g'], d_h2, dx2, name=n('norm2_bwd'))
    g['norm2_g'] = dg2[0]
    sink.point(l, 'mid', dx1)
    d_mix = _mm(dx1, W['w_out'], b_lead=0, tb=True, name=n('d_mix'), tn=1024, after=started)
    large('w_out', s['mix'], dx1, tm=1024, tn=1024)
    dq, dk, dv, df3, dr = _attn_bwd(s['qkv'], s['f3'], s['att'], s['lse'], d_mix, name=n('attn_bwd'))
    df = _pad_axis(df3.reshape(N_HEADS, S) + dr[:, ::HEAD_DIM].T, FG_ROWS, 0)
    d_z, d_b = _forget_bwd(s['z_raw'], p['b_col'], df, name=n('forget_bwd'))
    g['b_f'] = d_b[:N_HEADS, 0]
    d_proj, d_qg, d_kg = _qk_prep_bwd(s['proj_a'], dq, dk, dv, p['qg'], p['kg'], name=n('qk_norm_bwd'))
    g['q_norm_g'] = d_qg.reshape(N_HEADS, HEAD_DIM).sum(axis=0)
    g['k_norm_g'] = d_kg.reshape(N_HEADS, HEAD_DIM).sum(axis=0)
    d_proj, d_dw_w, d_dw_b, d_ln_g, d_ln_b, d_pw = _conv_bwd(
        s['proj_a'], d_mix, d_proj, W['dw_w'], p['dw_b'], p['ln_g'], p['ln_b'], W['pw_w'], 0, name=n('conv_bwd'))
    g['conv_dw_w'], g['conv_dw_b'] = d_dw_w[:CONV_WIDTH], d_dw_b[0]
    g['conv_ln_g'], g['conv_ln_b'], g['conv_pw_w'] = d_ln_g[0], d_ln_b[0], d_pw
    d_proj, d_wbd, d_scale = _pool_bwd(s['proj_a'], d_mix, d_proj, p['wbd'], p['pool_scale'], name=n('pool_bwd'))
    g['pool_w'] = jnp.stack([d_wbd[i * POOL_GROUP:(i + 1) * POOL_GROUP, i * POOL_GROUP:(i + 1) * POOL_GROUP]
                             for i in range(len(POOL_WINDOWS))])
    g['pool_scale'] = d_scale[0]
    d_w_a = _mm(s['h'], d_proj, ta=True, name=n('d_w_a'), tm=1024, tn=768, tk=4096)
    started = sink.put(l, 'w_in', d_w_a, _mm(d_z, s['h'], name=n('d_w_fg'), tn=1024).T)
    d_h_fg = _mm(d_z, W['w_fg_t'], b_lead=0, ta=True, name=n('d_h_fg'), tn=1024, after=started)
    d_h = _mm(d_proj, W['w_a'], b_lead=0, tb=True, res=d_h_fg, name=n('d_h'), tn=1024, tk=D_PROJ_A)
    dx, dg1 = _rms_bwd(s['x'], p['norm1_g'], d_h, dx1, name=n('norm1_bwd'))
    g['norm1_g'] = dg1[0]
    sink.point(l, 'end', dx)
    return dx, g


SMALL_GRADS = REPLICATED + ('conv_dw_w', 'conv_pw_w', 'ffn_dw_w')


def _local_step(x, target, W, w_small, sink):
    depth = w_small['norm1_g'].shape[0]
    ps, saved = [], []
    for l in range(depth):
        p = _small_weights(w_small, l)
        x, s = _layer_fwd(x, W[l], p, l)
        ps.append(p)
        saved.append(s)
    loss, dx = _loss_head(x, target, name='loss_head')
    small = [None] * depth
    for l in reversed(range(depth)):
        dx, small[l] = _layer_bwd(dx, W[l], ps[l], saved[l], l, sink)
    return loss, dx, {k: jnp.stack([small[l][k] for l in range(depth)]) for k in SMALL_GRADS}


W_IN_SHARD = D_IN // N_CHIPS
W_IN_PAD = 640
N_A_TILES = D_PROJ_A // LANES
FG_COL0 = D_QKV


def _a_tile_base(j):
    if j == N_A_TILES:
        return FG_COL0, N_HEADS
    return (j * LANES if j * LANES < FG_COL0 else j * LANES + N_HEADS), LANES


def _shift_select(rows, cols, shift, row_max, col_max):
    r = lax.broadcasted_iota(jnp.int32, (rows, cols), 0)
    c = lax.broadcasted_iota(jnp.int32, (rows, cols), 1)
    return ((r + shift == c) & (r < row_max) & (c < col_max)).astype(BF16)


def _select_w_in(raw, *, name, tm=256):
    _, D, _ = raw.shape
    tm = _tile(D, tm, 16)
    plan = []
    for j in range(N_A_TILES + 1):
        base, cmax = _a_tile_base(j)
        parts = []
        for p in range(N_CHIPS):
            delta = base - W_IN_SHARD * p
            lo, hi = max(0, delta), min(W_IN_SHARD - 1, delta + cmax - 1)
            if lo > hi:
                continue
            a0 = (lo // LANES) * LANES
            kw = min(-(-(hi + 1 - a0) // LANES) * LANES, W_IN_PAD - a0)
            parts.append((p, a0, kw, delta))
        plan.append((cmax, parts))

    def body(raw_ref, wa_ref, fg_ref):
        for j, (cmax, parts) in enumerate(plan):
            acc = None
            for p, a0, kw, delta in parts:
                sel = _shift_select(kw, LANES, a0 - delta, W_IN_SHARD - a0, cmax)
                t = _dot(raw_ref[p, :, a0:a0 + kw], sel, 1, 0)
                acc = t if acc is None else acc + t
            if j == N_A_TILES:
                fg_ref[...] = acc.astype(BF16)
            else:
                wa_ref[:, j * LANES:(j + 1) * LANES] = acc.astype(BF16)

    return pl.pallas_call(
        body, name=name,
        out_shape=(jax.ShapeDtypeStruct((D, D_PROJ_A), BF16), jax.ShapeDtypeStruct((D, LANES), BF16)),
        grid=(D // tm,),
        in_specs=[pl.BlockSpec((N_CHIPS, tm, W_IN_PAD), lambda i: (0, i, 0))],
        out_specs=(pl.BlockSpec((tm, D_PROJ_A), lambda i: (i, 0)), pl.BlockSpec((tm, LANES), lambda i: (i, 0))),
        compiler_params=_params('parallel'),
    )(raw)


def _select_w_in_grads(p_a, p_fg, *, name, tm=256):
    D = p_a.shape[0]
    tm = _tile(D, tm, 16)
    n_local = W_IN_PAD // LANES
    plan = []
    for p in range(N_CHIPS):
        for i in range(n_local):
            cmax = max(0, min(LANES, W_IN_SHARD - i * LANES))
            parts = []
            for j in range(N_A_TILES + 1):
                base, rmax = _a_tile_base(j)
                e = base - W_IN_SHARD * p - i * LANES
                if e + rmax - 1 < 0 or e > cmax - 1:
                    continue
                parts.append((j, e, rmax))
            plan.append((p, i, cmax, parts))

    def body(a_ref, fg_ref, o32_ref, o16_ref):
        terms = {}

        def src(j):
            if j not in terms:
                v = fg_ref[...] if j == N_A_TILES else a_ref[:, j * LANES:(j + 1) * LANES]
                terms[j] = _split3(v)
            return terms[j]

        for p, i, cmax, parts in plan:
            acc = jnp.zeros((tm, LANES), F32)
            for j, e, rmax in parts:
                sel = _shift_select(LANES, LANES, e, rmax, cmax)
                for term in src(j):
                    acc = acc + _dot(term, sel, 1, 0)
            o32_ref[p, :, i * LANES:(i + 1) * LANES] = acc
            o16_ref[p, :, i * LANES:(i + 1) * LANES] = acc.astype(BF16)

    out = pl.BlockSpec((N_CHIPS, tm, W_IN_PAD), lambda i: (0, i, 0))
    return pl.pallas_call(
        body, name=name,
        out_shape=(jax.ShapeDtypeStruct((N_CHIPS, D, W_IN_PAD), F32), jax.ShapeDtypeStruct((N_CHIPS, D, W_IN_PAD), BF16)),
        grid=(D // tm,),
        in_specs=[pl.BlockSpec((tm, D_PROJ_A), lambda i: (i, 0)), pl.BlockSpec((tm, LANES), lambda i: (i, 0))],
        out_specs=(out, out),
        compiler_params=_params('parallel'),
    )(p_a, p_fg)


MESH = pl.DeviceIdType.MESH
HBM_SPEC = pl.BlockSpec(memory_space=pltpu.HBM)


def _place():
    return lax.axis_index('x'), lax.axis_index('y'), lax.axis_index('c')


def _other_chips(x, y):
    return [(1 - x, y), (x, 1 - y), (1 - x, 1 - y)]


def _up_pos(q):
    return (q % 2) * 2 + q // 2


CHUNKS = {
    'w_in': ('lead', None),
    'w_up': ('cols', None),
    'w_down': ('rows', None),
    'w_out': ('rows', None),
    'conv_pw_w': ('rows', None),
    'conv_dw_w': ('lead', None),
    'ffn_dw_w': ('lead', None),
}


def _window(ref, kind, l, q):
    at = (lambda *idx: ref.at[idx]) if l is None else (lambda *idx: ref.at[(l,) + idx])
    shape = ref.shape if l is None else ref.shape[1:]
    if kind == 'lead':
        return at(q)
    if kind == 'rows':
        cs = shape[0] // N_CHIPS
        return at(pl.ds(pl.multiple_of(q * cs, 16), cs), slice(None))
    cs = shape[1] // N_CHIPS
    return at(slice(None), pl.ds(pl.multiple_of(_up_pos(q) * cs, LANES), cs))


def _place_shard(src, l, pos_arr, full_shape, kind, *, name, tm=256, after=None):
    _, m, n = src.shape
    bm = _tile(m, tm, 16) if kind != 'rows' else m
    extra = () if after is None else (after,)

    def body(pos_ref, s_ref, *rest):
        rest[-1][...] = s_ref[...].astype(BF16)

    if kind == 'lead':
        out = pl.BlockSpec((None, bm, n), lambda i, pos: (pos[0], i, 0))
    elif kind == 'rows':
        out = pl.BlockSpec((bm, n), lambda i, pos: (pos[0], 0))
    else:
        out = pl.BlockSpec((bm, n), lambda i, pos: (i, pos[0]))
    return pl.pallas_call(
        body, name=name, out_shape=jax.ShapeDtypeStruct(full_shape, BF16),
        grid_spec=pltpu.PrefetchScalarGridSpec(
            num_scalar_prefetch=1, grid=(m // bm,),
            in_specs=[pl.BlockSpec((None, bm, n), lambda i, pos: (l, i, 0))] + [ANY_SPEC] * len(extra), out_specs=out),
        compiler_params=_params('parallel'),
    )(pos_arr, src, *extra)


GATHERED = ('w_in', 'w_up', 'w_down', 'w_out', 'conv_pw_w', 'conv_dw_w', 'ffn_dw_w')
GATHER_GROUPS = ((0, ('w_in', 'conv_dw_w', 'ffn_dw_w', 'conv_pw_w')), (0, ('w_out', 'w_up', 'w_down')), (1, GATHERED))
SEM_SPEC = pl.BlockSpec(memory_space=pltpu.SEMAPHORE)
SPLIT_COPY_PARAMS = pltpu.CompilerParams(has_side_effects=pltpu.SideEffectType.DATAFLOW_SIDE_EFFECTING)


def _gather_start(tag, groups, bufs):
    flat = [b for group in bufs for b in group]
    nb = len(flat)

    def body(*refs):
        outs, sems, token = refs[nb:2 * nb], refs[2 * nb:-1], refs[-1]
        token[...] = jnp.zeros(token.shape, F32)
        x, y, c = _place()
        pos = 0
        for g, keys in enumerate(GATHER_GROUPS[n][1] for n in groups):
            for i, k in enumerate(keys):
                w = _window(outs[pos], CHUNKS[k][0], None, 2 * x + y)
                pos += 1
                for j, chip in enumerate(_other_chips(x, y)):
                    pltpu.make_async_remote_copy(src_ref=w, dst_ref=w, send_sem=sems[2 * g].at[3 * i + j],
                                                 recv_sem=sems[2 * g + 1].at[3 * i + j], device_id=(*chip, c),
                                                 device_id_type=MESH).start()

    sem_shapes = [pltpu.SemaphoreType.DMA((3 * len(GATHER_GROUPS[n][1]),)) for n in groups for _ in range(2)]
    res = pl.pallas_call(
        body, name=f'gather_start_{tag}',
        out_shape=tuple(jax.ShapeDtypeStruct(b.shape, b.dtype) for b in flat) + tuple(sem_shapes)
        + (jax.ShapeDtypeStruct((8, LANES), F32),),
        in_specs=[HBM_SPEC] * nb,
        out_specs=tuple([HBM_SPEC] * nb + [SEM_SPEC] * len(sem_shapes) + [pl.BlockSpec(memory_space=pltpu.VMEM)]),
        input_output_aliases={b: b for b in range(nb)},
        compiler_params=SPLIT_COPY_PARAMS,
    )(*[pltpu.with_memory_space_constraint(b, pltpu.HBM) for b in flat])
    out_bufs, sems, pos = [], res[nb:-1], 0
    for group in bufs:
        out_bufs.append(list(res[pos:pos + len(group)]))
        pos += len(group)
    return out_bufs, [(sems[2 * g], sems[2 * g + 1]) for g in range(len(groups))], res[-1]


def _gather_wait(g, bufs, sems, after):
    keys = GATHER_GROUPS[g][1]
    nb = len(bufs)

    def body(*refs):
        send_sems, recv_sems = refs[nb], refs[nb + 1]
        outs = refs[nb + 3:]
        x, y, c = _place()
        for i, k in enumerate(keys):
            mine = _window(outs[i], CHUNKS[k][0], None, 2 * x + y)
            for j, (cx, cy) in enumerate(_other_chips(x, y)):
                theirs = _window(outs[i], CHUNKS[k][0], None, 2 * cx + cy)
                cp = pltpu.make_async_remote_copy(src_ref=mine, dst_ref=theirs, send_sem=send_sems.at[3 * i + j],
                                                  recv_sem=recv_sems.at[3 * i + j], device_id=(cx, cy, c), device_id_type=MESH)
                cp.wait_send()
                cp.wait_recv()

    return pl.pallas_call(
        body, name=f'gather_wait_{g}',
        out_shape=tuple(jax.ShapeDtypeStruct(b.shape, b.dtype) for b in bufs),
        in_specs=[HBM_SPEC] * nb + [SEM_SPEC, SEM_SPEC, ANY_SPEC], out_specs=tuple([HBM_SPEC] * nb),
        input_output_aliases={b: b for b in range(nb)},
        compiler_params=SPLIT_COPY_PARAMS,
    )(*bufs, *sems, after)


def _rs_block(M, N):
    return (_tile(M, 256, 16), _tile(N, 2048))


def _chunk_shape(shape, kind):
    if kind == 'lead':
        return tuple(shape[1:])
    if kind == 'rows':
        return (shape[0] // N_CHIPS, shape[1])
    return (shape[0], shape[1] // N_CHIPS)


def _rs_start(tag, bufs, kinds):
    nb = len(bufs)
    lands = [lax.empty((N_CHIPS - 1,) + _chunk_shape(b.shape, k), b.dtype) for b, k in zip(bufs, kinds)]

    def body(*refs):
        src, land = refs[2 * nb:3 * nb], refs[3 * nb:4 * nb]
        send_sems, recv_sems, token = refs[4 * nb:]
        token[...] = jnp.zeros(token.shape, F32)
        x, y, c = _place()
        for b in range(nb):
            for j, (cx, cy) in enumerate(_other_chips(x, y)):
                pltpu.make_async_remote_copy(
                    src_ref=_window(src[b], kinds[b], None, 2 * cx + cy), dst_ref=land[b].at[j],
                    send_sem=send_sems.at[3 * b + j], recv_sem=recv_sems.at[3 * b + j],
                    device_id=(cx, cy, c), device_id_type=MESH).start()

    sem = pltpu.SemaphoreType.DMA((3 * nb,))
    res = pl.pallas_call(
        body, name=f'rs_start_{tag}',
        out_shape=tuple(jax.ShapeDtypeStruct(b.shape, b.dtype) for b in list(bufs) + lands)
        + (sem, sem, jax.ShapeDtypeStruct((8, LANES), F32)),
        in_specs=[HBM_SPEC] * (2 * nb),
        out_specs=tuple([HBM_SPEC] * (2 * nb) + [SEM_SPEC, SEM_SPEC, pl.BlockSpec(memory_space=pltpu.VMEM)]),
        input_output_aliases={b: b for b in range(2 * nb)},
        compiler_params=SPLIT_COPY_PARAMS,
    )(*[pltpu.with_memory_space_constraint(b, pltpu.HBM) for b in list(bufs) + lands])
    return res[:nb], res[nb:2 * nb], res[2 * nb:2 * nb + 2], res[2 * nb + 2]


def _rs_wait(tag, bufs, lands, sems, kinds, after):
    nb = len(bufs)

    def body(*refs):
        send_sems, recv_sems = refs[2 * nb], refs[2 * nb + 1]
        src, land = refs[2 * nb + 3:3 * nb + 3], refs[3 * nb + 3:]
        x, y, c = _place()
        for b in range(nb):
            for j, (cx, cy) in enumerate(_other_chips(x, y)):
                cp = pltpu.make_async_remote_copy(
                    src_ref=_window(src[b], kinds[b], None, 2 * cx + cy), dst_ref=land[b].at[j],
                    send_sem=send_sems.at[3 * b + j], recv_sem=recv_sems.at[3 * b + j],
                    device_id=(cx, cy, c), device_id_type=MESH)
                cp.wait_send()
                cp.wait_recv()

    res = pl.pallas_call(
        body, name=f'rs_wait_{tag}',
        out_shape=tuple(jax.ShapeDtypeStruct(b.shape, b.dtype) for b in list(bufs) + list(lands)),
        in_specs=[HBM_SPEC] * (2 * nb) + [SEM_SPEC, SEM_SPEC, ANY_SPEC], out_specs=tuple([HBM_SPEC] * (2 * nb)),
        input_output_aliases={b: b for b in range(2 * nb)},
        compiler_params=SPLIT_COPY_PARAMS,
    )(*bufs, *lands, *sems, after)
    return res[nb:]


def _rs_sum(p, rb, kind, pos_arr, l, depth, buf, *, name):
    m, n = rb.shape[1:]
    bm, bn = _rs_block(m, n)
    nbm, nbn = m // bm, n // bn
    has_buf = buf is not None

    def body(q_ref, p_ref, r_ref, *rest):
        acc = p_ref[...]
        for j in range(N_CHIPS - 1):
            acc = acc + r_ref[j].astype(F32)
        rest[-1][...] = acc

    if kind == 'lead':
        p_map = lambda i, j, q: (q[0], i, j)
    elif kind == 'rows':
        p_map = lambda i, j, q: (q[0] * nbm + i, j)
    else:
        p_map = lambda i, j, q: (i, q[0] * nbn + j)
    r_spec = pl.BlockSpec((N_CHIPS - 1, bm, bn), lambda i, j, q: (0, i, j))
    p_spec = pl.BlockSpec(((None,) if kind == 'lead' else ()) + (bm, bn), p_map)
    return pl.pallas_call(
        body, name=name, out_shape=jax.ShapeDtypeStruct((depth, m, n), F32),
        grid_spec=pltpu.PrefetchScalarGridSpec(
            num_scalar_prefetch=1, grid=(nbm, nbn), in_specs=[p_spec, r_spec] + ([ANY_SPEC] if has_buf else []),
            out_specs=pl.BlockSpec((None, bm, bn), lambda i, j, q: (l, i, j))),
        input_output_aliases={3: 0} if has_buf else {},
        compiler_params=_params('parallel', 'parallel'),
    )(pos_arr, p, rb, *((buf,) if has_buf else ()))


def _swap_start(bufs):
    nb = len(bufs)
    lands = [lax.empty(b.shape, b.dtype) for b in bufs]

    def body(*refs):
        src, land = refs[2 * nb:3 * nb], refs[3 * nb:4 * nb]
        send_sems, recv_sems, token = refs[4 * nb:]
        token[...] = jnp.zeros(token.shape, F32)
        x, y, c = _place()
        for b in range(nb):
            pltpu.make_async_remote_copy(src_ref=src[b], dst_ref=land[b], send_sem=send_sems.at[b], recv_sem=recv_sems.at[b],
                                         device_id=(x, y, 1 - c), device_id_type=MESH).start()

    sem = pltpu.SemaphoreType.DMA((nb,))
    res = pl.pallas_call(
        body, name='rs_swap_start',
        out_shape=tuple(jax.ShapeDtypeStruct(b.shape, b.dtype) for b in list(bufs) + lands)
        + (sem, sem, jax.ShapeDtypeStruct((8, LANES), F32)),
        in_specs=[HBM_SPEC] * (2 * nb),
        out_specs=tuple([HBM_SPEC] * (2 * nb) + [SEM_SPEC, SEM_SPEC, pl.BlockSpec(memory_space=pltpu.VMEM)]),
        input_output_aliases={b: b for b in range(2 * nb)},
        compiler_params=SPLIT_COPY_PARAMS,
    )(*[pltpu.with_memory_space_constraint(b, pltpu.HBM) for b in list(bufs) + lands])
    return res[:nb], res[nb:2 * nb], res[2 * nb:2 * nb + 2], res[2 * nb + 2]


def _swap_wait(bufs, lands, sems, after):
    nb = len(bufs)

    def body(*refs):
        send_sems, recv_sems = refs[2 * nb], refs[2 * nb + 1]
        src, land = refs[2 * nb + 3:3 * nb + 3], refs[3 * nb + 3:]
        x, y, c = _place()
        for b in range(nb):
            cp = pltpu.make_async_remote_copy(src_ref=src[b], dst_ref=land[b], send_sem=send_sems.at[b],
                                              recv_sem=recv_sems.at[b], device_id=(x, y, 1 - c), device_id_type=MESH)
            cp.wait_send()
            cp.wait_recv()

    res = pl.pallas_call(
        body, name='rs_swap_wait',
        out_shape=tuple(jax.ShapeDtypeStruct(b.shape, b.dtype) for b in list(bufs) + list(lands)),
        in_specs=[HBM_SPEC] * (2 * nb) + [SEM_SPEC, SEM_SPEC, ANY_SPEC], out_specs=tuple([HBM_SPEC] * (2 * nb)),
        input_output_aliases={b: b for b in range(2 * nb)},
        compiler_params=SPLIT_COPY_PARAMS,
    )(*bufs, *lands, *sems, after)
    return res[:nb], res[nb:]


def _all_reduce_small(v, after):
    r = v.shape[0]

    def body(x_ref, after_ref, tot_ref, all_ref, send_sems, recv_sems):
        x, y, c = _place()
        me, sibling = (x, y, c), (x, y, 1 - c)
        chips = _other_chips(x, y)

        def rows(px, py, pc):
            return all_ref.at[pl.ds((4 * px + 2 * py + pc) * r, r), :]

        def copy(k, block, to, src=None):
            return pltpu.make_async_remote_copy(
                src_ref=rows(*block) if src is None else src, dst_ref=rows(*block),
                send_sem=send_sems.at[k], recv_sem=recv_sems.at[k], device_id=to, device_id_type=MESH)

        rows(*me)[...] = x_ref[...]
        first = [copy(0, me, sibling, src=x_ref)]
        first += [copy(1 + j, me, (*chip, c), src=x_ref) for j, chip in enumerate(chips)]
        for cp in first:
            cp.start()
        passed = [copy(4 + j, (*chip, c), sibling) for j, chip in enumerate(chips)]
        for j, chip in enumerate(chips):
            copy(1 + j, (*chip, c), me).wait_recv()
            passed[j].start()
        copy(0, sibling, me).wait_recv()
        for j, chip in enumerate(chips):
            copy(4 + j, (*chip, 1 - c), me).wait_recv()
        for cp in first + passed:
            cp.wait_send()
        acc = all_ref[0:r, :]
        for d in range(1, N_DEV):
            acc = acc + all_ref[d * r:(d + 1) * r, :]
        tot_ref[...] = acc

    return pl.pallas_call(
        body, name='all_reduce_small', out_shape=jax.ShapeDtypeStruct((r, LANES), F32),
        in_specs=[pl.BlockSpec(memory_space=pltpu.VMEM), ANY_SPEC], out_specs=pl.BlockSpec(memory_space=pltpu.VMEM),
        scratch_shapes=[pltpu.VMEM((N_DEV * r, LANES), F32), pltpu.SemaphoreType.DMA((7,)), pltpu.SemaphoreType.DMA((7,))],
    )(v, after)


def _adamw(w, g, m, v, *, name, g2=None, ts=256):
    R, C = w.shape
    Cg = g.shape[1]
    ts = _tile(R, ts, 8)
    c1 = 1.0 - ADAM_B1 ** ADAM_STEP
    c2 = 1.0 - ADAM_B2 ** ADAM_STEP
    two = g2 is not None

    def body(w_ref, g_ref, *rest):
        m_ref, v_ref, go_ref, d_ref, nm_ref, nv_ref = rest[two:]
        gv = g_ref[:, 0:C]
        if two:
            gv = gv + rest[0][:, 0:C]
        nm = ADAM_B1 * m_ref[...] + (1.0 - ADAM_B1) * gv
        nv = ADAM_B2 * v_ref[...] + (1.0 - ADAM_B2) * (gv * gv)
        d_ref[...] = -ADAM_LR * ((nm / c1) / (jnp.sqrt(nv / c2) + ADAM_EPS) + ADAM_WD * w_ref[...])
        go_ref[...] = gv
        nm_ref[...] = nm
        nv_ref[...] = nv

    blk = pl.BlockSpec((ts, C), lambda i: (i, 0))
    gblk = pl.BlockSpec((ts, Cg), lambda i: (i, 0))
    shape = jax.ShapeDtypeStruct((R, C), F32)
    return pl.pallas_call(body, name=name, out_shape=(shape, shape, shape, shape), grid=(R // ts,),
                          in_specs=[blk, gblk] + ([gblk] if two else []) + [blk, blk], out_specs=(blk, blk, blk, blk),
                          compiler_params=_params('parallel'))(w, g, *((g2,) if two else ()), m, v)


def _pack_rows(parts, row_unit):
    flat = jnp.concatenate(parts)
    flat = _pad_axis(flat, -(-flat.shape[0] // (row_unit * LANES)) * row_unit * LANES, 0)
    return flat.reshape(-1, LANES)


def _as_2d(a):
    return a.reshape(-1, a.shape[-1])


def _mesh_place():
    cx, cy, cc = _place()
    chip = 2 * cx + cy
    as_arr = lambda v: jnp.reshape(v, (1,)).astype(jnp.int32)
    return chip, as_arr(cc), as_arr(chip), as_arr(_up_pos(chip))


class _LayerWeights:
    def __init__(self, groups, started=None):
        self.groups = groups
        self.started = started
        self.ready = {}

    def get(self, name, after):
        if name not in self.ready:
            for names, wait in self.groups:
                if name in names:
                    self.ready.update({k: v[None] for k, v in wait(after).items()})
        return self.ready[name]


def _gather_full(w, place):
    chip, _, chip_arr, up_pos_arr = place
    L, D = w['w_in'].shape[:2]
    w_in_pad = _pad_axis(w['w_in'], W_IN_PAD, 2)

    def placed(k, l, after=None):
        if k == 'w_in':
            return _place_shard(w_in_pad, l, chip_arr, (N_CHIPS, D, W_IN_PAD), 'lead', name=f'place_w_in_{l}', after=after)
        if k == 'w_up':
            return _place_shard(w[k], l, up_pos_arr, (w[k].shape[1], N_CHIPS * w[k].shape[2]), 'cols', name=f'place_w_up_{l}',
                                after=after)
        if k in ('conv_dw_w', 'ffn_dw_w'):
            return lax.dynamic_update_slice_in_dim(jnp.zeros((N_CHIPS,) + w[k].shape[1:], F32), w[k][l][None], chip, axis=0)
        return _place_shard(w[k], l, chip_arr, (N_CHIPS * w[k].shape[1], w[k].shape[2]), 'rows', name=f'place_{k}_{l}',
                            after=after)

    first, rest = [0], list(range(1, len(GATHER_GROUPS)))
    bufs0, sems0, token = _gather_start('first', first, [[placed(k, GATHER_GROUPS[0][0]) for k in GATHER_GROUPS[0][1]]])
    bufs1, sems1, started = _gather_start('rest', rest, [[placed(k, GATHER_GROUPS[g][0], token) for k in GATHER_GROUPS[g][1]]
                                                         for g in rest])
    bufs, sems = bufs0 + bufs1, sems0 + sems1
    unchunk = lambda a: jnp.moveaxis(a, 0, 1).reshape(a.shape[1], -1)

    def waiter(g):
        l, keys = GATHER_GROUPS[g]

        def wait(after):
            full = dict(zip(keys, _gather_wait(g, bufs[g], sems[g], after)))
            out = {}
            if 'w_in' in full:
                out['w_a'], w_fg = _select_w_in(full['w_in'], name=f'select_w_in_{l}')
                out['w_fg_t'] = w_fg.T
            if 'conv_dw_w' in full:
                out['dw_w'] = _pad_axis(unchunk(full['conv_dw_w']), CONV_HALO, 0)
            if 'ffn_dw_w' in full:
                out['ffn_w'] = _pad_axis(_pair_cols(unchunk(full['ffn_dw_w'])), FFN_HALO, 0)
            if 'conv_pw_w' in full:
                out['pw_w'] = full['conv_pw_w']
            out.update({k: full[k] for k in ('w_up', 'w_down', 'w_out') if k in full})
            return out

        names = {'w_in': ('w_a', 'w_fg_t'), 'conv_dw_w': ('dw_w',), 'ffn_dw_w': ('ffn_w',), 'conv_pw_w': ('pw_w',)}
        return tuple(n for k in keys for n in names.get(k, (k,))), wait

    return [_LayerWeights([waiter(g) for g in range(len(GATHER_GROUPS)) if GATHER_GROUPS[g][0] == l],
                          started if l == 0 else None) for l in range(L)]


RS_WIRE = ('w_in', 'w_up', 'w_down', 'w_out')
RS_GROUPS = (('ffn', ('w_down', 'w_up')), ('mix', ('w_out', 'w_in')))


class _GradReducer:
    def __init__(self, place, depth):
        _, _, self.chip_arr, self.up_pos_arr = place
        self.depth = depth
        self.got = {}
        self.flying = {}
        self.sums = {}

    def put(self, l, key, g32, g16):
        if key == 'w_in':
            g32, g16 = _select_w_in_grads(g32, g16, name=f'l{l}_select_w_in_grads')
        self.got[(l, key)] = (g32, g16)
        for tag, keys in RS_GROUPS:
            if key == keys[-1]:
                kinds = [CHUNKS[k][0] for k in keys]
                bufs, lands, sems, token = _rs_start(f'l{l}_{tag}', [self.got[(l, k)][1] for k in keys], kinds)
                self.flying[(l, tag)] = (bufs, lands, sems, kinds)
                return token
        return None

    def point(self, l, where, after):
        if where == 'mid':
            self._land(l + 1, 'mix', after)
        else:
            self._land(l, 'ffn', after)

    def _land(self, l, tag, after):
        if (l, tag) not in self.flying:
            return
        bufs, lands, sems, kinds = self.flying.pop((l, tag))
        lands = _rs_wait(f'l{l}_{tag}', bufs, lands, sems, kinds, after)
        for k, rb, kind in zip(dict(RS_GROUPS)[tag], lands, kinds):
            pos = self.up_pos_arr if kind == 'cols' else self.chip_arr
            self.sums[k] = _rs_sum(self.got.pop((l, k))[0], rb, kind, pos, l, self.depth, self.sums.get(k), name=f'l{l}_rs_sum_{k}')

    def finish_start(self, after):
        for l, tag in list(self.flying):
            self._land(l, tag, after)
        *self.swap, token = _swap_start([self.sums[k] for k in RS_WIRE])
        return token

    def finish_wait(self, after):
        return dict(zip(RS_WIRE, zip(*_swap_wait(*self.swap, after))))


def kernel(x, norm1_g, w_in, b_f, q_norm_g, k_norm_g, conv_dw_w, conv_dw_b, conv_ln_g, conv_ln_b, conv_pw_w, pool_w, pool_scale, w_out, norm2_g, w_up, ffn_dw_w, w_down, loss_target, m_norm1_g, m_w_in, m_b_f, m_q_norm_g, m_k_norm_g, m_conv_dw_w, m_conv_dw_b, m_conv_ln_g, m_conv_ln_b, m_conv_pw_w, m_pool_w, m_pool_scale, m_w_out, m_norm2_g, m_w_up, m_ffn_dw_w, m_w_down, v_norm1_g, v_w_in, v_b_f, v_q_norm_g, v_k_norm_g, v_conv_dw_w, v_conv_dw_b, v_conv_ln_g, v_conv_ln_b, v_conv_pw_w, v_pool_w, v_pool_scale, v_w_out, v_norm2_g, v_w_up, v_ffn_dw_w, v_w_down):
    given = dict(locals())
    w = {k: given[k] for k in WEIGHTS}
    mom_m = {k: given['m_' + k] for k in WEIGHTS}
    mom_v = {k: given['v_' + k] for k in WEIGHTS}
    place = _mesh_place()
    chip = place[0]
    W = _gather_full(w, place)

    reducer = _GradReducer(place, norm1_g.shape[0])
    loss_part, grad_x, g_small = _local_step(x[0], loss_target[0], W, {k: w[k] for k in REPLICATED}, reducer)
    loss = lax.psum(loss_part[0, 0], ('x', 'y', 'c'))
    swapping = reducer.finish_start(grad_x)

    small = _pack_rows([g_small[k].reshape(-1) for k in SMALL_GRADS], 8)
    small_sum = _all_reduce_small(small, swapping)
    g_sum, delta, new_m, new_v = {}, {}, {}, {}
    off = 0
    small_full = {}
    for k in SMALL_GRADS:
        small_full[k] = small_sum.reshape(-1)[off:off + g_small[k].size].reshape(g_small[k].shape)
        off += g_small[k].size
    small_g = {k: small_full[k] for k in REPLICATED}
    small_g['conv_dw_w'] = lax.dynamic_slice_in_dim(small_full['conv_dw_w'], chip * w['conv_dw_w'].shape[2], w['conv_dw_w'].shape[2], axis=2)
    small_g['conv_pw_w'] = lax.dynamic_slice_in_dim(small_full['conv_pw_w'], chip * w['conv_pw_w'].shape[1], w['conv_pw_w'].shape[1], axis=1)
    small_g['ffn_dw_w'] = lax.dynamic_slice_in_dim(small_full['ffn_dw_w'], chip * w['ffn_dw_w'].shape[2], w['ffn_dw_w'].shape[2], axis=2)
    pack_small = lambda t: _pack_rows([t[k].reshape(-1) for k in SMALL_GRADS], 256)
    outs = _adamw(pack_small(w), pack_small(small_g), pack_small(mom_m), pack_small(mom_v), name='adamw_small')
    off = 0
    for k in SMALL_GRADS:
        pieces = [o.reshape(-1)[off:off + w[k].size].reshape(w[k].shape) for o in outs]
        g_sum[k], delta[k], new_m[k], new_v[k] = pieces
        off += w[k].size

    sums = reducer.finish_wait(outs[1])
    for k in RS_WIRE:
        outs = _adamw(_as_2d(w[k]), _as_2d(sums[k][0]), _as_2d(mom_m[k]), _as_2d(mom_v[k]), g2=_as_2d(sums[k][1]), name='adamw_' + k)
        g_sum[k], delta[k], new_m[k], new_v[k] = [o.reshape(w[k].shape) for o in outs]

    return (loss, grad_x[None], *[g_sum[k] for k in WEIGHTS], *[delta[k] for k in WEIGHTS],
            *[new_m[k] for k in WEIGHTS], *[new_v[k] for k in WEIGHTS])
```

```python
import functools

import jax
import jax.numpy as jnp
from jax import lax
from jax.experimental import pallas as pl
from jax.experimental.pallas import tpu as pltpu

F32 = jnp.float32
BF16 = jnp.bfloat16

N_HEADS = 8
HEAD_DIM = 64
D_ATT = N_HEADS * HEAD_DIM
D_CONV = 256
D_POOL = 256
D_MIX = D_ATT + D_CONV + D_POOL
D_QKV = 3 * D_ATT
D_PROJ_A = D_QKV + 2 * D_CONV + D_POOL
D_IN = D_PROJ_A + N_HEADS
FG_ROWS = 128
CONV_WIDTH = 31
CONV_HALO = 32
POOL_WINDOWS = (2, 4, 8, 16)
POOL_GROUP = 64
POOL_HALO = 16
FFN_CONV_WIDTH = 3
FFN_HALO = 8
ATT_SCALE = HEAD_DIM ** -0.5
EPS = 1e-6
NEG = -1e30
LANES = 128

ADAM_LR = 0.001
ADAM_B1 = 0.9
ADAM_B2 = 0.999
ADAM_EPS = 1e-08
ADAM_WD = 0.01
ADAM_STEP = 10

N_CHIPS = 4
N_DEV = 8
VMEM_LIMIT_BYTES = 56 * 1024 * 1024

REPLICATED = ('norm1_g', 'b_f', 'q_norm_g', 'k_norm_g', 'conv_dw_b', 'conv_ln_g', 'conv_ln_b',
              'pool_w', 'pool_scale', 'norm2_g')
WEIGHTS = ('norm1_g', 'w_in', 'b_f', 'q_norm_g', 'k_norm_g', 'conv_dw_w', 'conv_dw_b', 'conv_ln_g',
           'conv_ln_b', 'conv_pw_w', 'pool_w', 'pool_scale', 'w_out', 'norm2_g', 'w_up', 'ffn_dw_w', 'w_down')


def _tile(dim, pref, unit=LANES):
    if dim <= pref:
        return dim
    t = (pref // unit) * unit
    while t >= unit:
        if dim % t == 0:
            return t
        t -= unit
    raise ValueError(f'no tile for {dim} (preferred {pref})')


def _params(*sem):
    return pltpu.CompilerParams(dimension_semantics=sem, vmem_limit_bytes=VMEM_LIMIT_BYTES)


def _sigmoid(x):
    return 1.0 / (1.0 + jnp.exp(-x))


def _dot(a, b, ca, cb):
    return lax.dot_general(a, b, (((ca,), (cb,)), ((), ())), preferred_element_type=F32)


def _split3(y):
    y1 = y.astype(BF16)
    r1 = y - y1.astype(F32)
    y2 = r1.astype(BF16)
    y3 = (r1 - y2.astype(F32)).astype(BF16)
    return y1, y2, y3


def _dot3(y, e, ca=1, cb=0):
    y1, y2, y3 = _split3(y)
    return _dot(y1, e, ca, cb) + _dot(y2, e, ca, cb) + _dot(y3, e, ca, cb)


def _lead(spec_shape, imap, lead):
    if lead is None:
        return pl.BlockSpec(spec_shape, imap)
    return pl.BlockSpec((None,) + spec_shape, lambda *g: (lead,) + imap(*g))


ANY_SPEC = pl.BlockSpec(memory_space=pl.ANY)


def _mm(a, b, *, name, ta=False, tb=False, res=None, out_dtype=F32, tm=512, tn=512, tk=1024,
        a_lead=None, b_lead=None, copy16=False, after=None, cols_outer=False, rms_gain=None):
    a2, b2 = a.shape[-2:], b.shape[-2:]
    K, M = a2 if ta else a2[::-1]
    N, Kb = b2 if tb else b2[::-1]
    assert K == Kb, (a.shape, b.shape)
    tm, tn, tk = _tile(M, tm), _tile(N, tn), _tile(K, tk)
    nk = K // tk
    ca = 0 if ta else 1
    cb = 1 if tb else 0
    has_res = res is not None
    has_rms = rms_gain is not None
    assert not (has_rms and (copy16 or tn != N))
    n_in = 2 + has_res + (after is not None) + has_rms
    n_out = 1 + copy16 + has_rms

    def body(*refs):
        a_ref, b_ref = refs[:2]
        r_ref = refs[2] if has_res else None
        o_refs = refs[n_in:n_in + n_out]
        scratch = refs[n_in + n_out:]

        def write(r):
            if has_res:
                r = r + r_ref[...]
            o_refs[0][...] = r.astype(out_dtype)
            if copy16:
                o_refs[1][...] = r.astype(BF16)
            if has_rms:
                scale = lax.rsqrt(jnp.mean(r * r, axis=-1, keepdims=True) + EPS)
                o_refs[1][...] = (r * scale * refs[n_in - 1][...]).astype(BF16)

        p = _dot(a_ref[...].astype(BF16), b_ref[...].astype(BF16), ca, cb)
        if nk == 1:
            write(p)
        else:
            acc = scratch[0]
            k = pl.program_id(2)

            @pl.when(k == 0)
            def _():
                acc[...] = p

            @pl.when(k > 0)
            def _():
                acc[...] += p

            @pl.when(k == nk - 1)
            def _():
                write(acc[...])

    ij = (lambda g0, g1: (g1, g0)) if cols_outer else (lambda g0, g1: (g0, g1))
    at = lambda f: (lambda g0, g1, k: f(*ij(g0, g1), k))
    a_spec = _lead((tk, tm), at(lambda i, j, k: (k, i)), a_lead) if ta else _lead((tm, tk), at(lambda i, j, k: (i, k)), a_lead)
    b_spec = _lead((tn, tk), at(lambda i, j, k: (j, k)), b_lead) if tb else _lead((tk, tn), at(lambda i, j, k: (k, j)), b_lead)
    o_spec = pl.BlockSpec((tm, tn), at(lambda i, j, k: (i, j)))
    in_specs = ([a_spec, b_spec] + ([o_spec] if has_res else []) + ([ANY_SPEC] if after is not None else [])
                + ([pl.BlockSpec((1, tn), at(lambda i, j, k: (0, j)))] if has_rms else []))
    args = (a, b) + ((res,) if has_res else ()) + ((after,) if after is not None else ()) + ((rms_gain,) if has_rms else ())
    out_shape = [jax.ShapeDtypeStruct((M, N), out_dtype)] + [jax.ShapeDtypeStruct((M, N), BF16)] * (n_out - 1)
    out = pl.pallas_call(
        body, name=name,
        out_shape=tuple(out_shape),
        grid=ij(M // tm, N // tn) + (nk,),
        in_specs=in_specs, out_specs=tuple([o_spec] * n_out),
        scratch_shapes=[pltpu.VMEM((tm, tn), F32)] if nk > 1 else [],
        compiler_params=_params('parallel', 'parallel', 'arbitrary'),
    )(*args)
    return out if n_out > 1 else out[0]


def _rms_fwd(x, g, *, name, ts=512, after=None):
    S, D = x.shape
    ts = _tile(S, ts, 8)

    def body(x_ref, g_ref, *rest):
        xv = x_ref[...]
        r = lax.rsqrt(jnp.mean(xv * xv, axis=-1, keepdims=True) + EPS)
        rest[-1][...] = (xv * r * g_ref[...]).astype(BF16)

    extra = () if after is None else (after,)
    return pl.pallas_call(
        body, name=name, out_shape=jax.ShapeDtypeStruct((S, D), BF16), grid=(S // ts,),
        in_specs=[pl.BlockSpec((ts, D), lambda i: (i, 0)), pl.BlockSpec((1, D), lambda i: (0, 0))] + [ANY_SPEC] * len(extra),
        out_specs=pl.BlockSpec((ts, D), lambda i: (i, 0)),
        compiler_params=_params('parallel'),
    )(x, g, *extra)


def _rms_bwd(x, g, dh, dres, *, name, ts=512):
    S, D = x.shape
    ts = _tile(S, ts, 8)

    def body(x_ref, g_ref, dh_ref, dr_ref, dx_ref, dg_ref):
        i = pl.program_id(0)
        xv = x_ref[...]
        r = lax.rsqrt(jnp.mean(xv * xv, axis=-1, keepdims=True) + EPS)
        y = xv * r
        dh_v = dh_ref[...]
        dy = dh_v * g_ref[...]
        dx_ref[...] = dr_ref[...] + r * (dy - y * jnp.mean(dy * y, axis=-1, keepdims=True))
        part = jnp.sum(dh_v * y, axis=0, keepdims=True)

        @pl.when(i == 0)
        def _():
            dg_ref[...] = part

        @pl.when(i > 0)
        def _():
            dg_ref[...] += part

    row = pl.BlockSpec((ts, D), lambda i: (i, 0))
    vec = pl.BlockSpec((1, D), lambda i: (0, 0))
    return pl.pallas_call(
        body, name=name,
        out_shape=(jax.ShapeDtypeStruct((S, D), F32), jax.ShapeDtypeStruct((1, D), F32)),
        grid=(S // ts,), in_specs=[row, vec, row, row], out_specs=(row, vec),
        compiler_params=_params('arbitrary'),
    )(x, g, dh, dres)


def _pair_ones():
    i = lax.broadcasted_iota(jnp.int32, (LANES, LANES), 0) // HEAD_DIM
    j = lax.broadcasted_iota(jnp.int32, (LANES, LANES), 1) // HEAD_DIM
    return (i == j).astype(BF16)


def _head_sums(y, e):
    return jnp.concatenate([_dot3(y[:, b * LANES:(b + 1) * LANES], e) for b in range(D_ATT // LANES)], axis=1)


def _qk_prep_fwd(proj_a, qg, kg, *, name, ts=512):
    S = proj_a.shape[0]
    ts = _tile(S, ts, 16)

    def body(q_ref, k_ref, v_ref, qg_ref, kg_ref, e_ref, o_ref):
        e = e_ref[...]

        def norm(xv, gain):
            ms = _head_sums(xv * xv, e) * (1.0 / HEAD_DIM)
            return xv * lax.rsqrt(ms + EPS) * gain

        o_ref[:, 0:D_ATT] = (norm(q_ref[...], qg_ref[...]) * ATT_SCALE).astype(BF16)
        o_ref[:, D_ATT:2 * D_ATT] = norm(k_ref[...], kg_ref[...]).astype(BF16)
        o_ref[:, 2 * D_ATT:3 * D_ATT] = v_ref[...].astype(BF16)

    col = lambda c: pl.BlockSpec((ts, D_ATT), lambda i: (i, c))
    vec = pl.BlockSpec((1, D_ATT), lambda i: (0, 0))
    return pl.pallas_call(
        body, name=name, out_shape=jax.ShapeDtypeStruct((S, D_QKV), BF16), grid=(S // ts,),
        in_specs=[col(0), col(1), col(2), vec, vec, pl.BlockSpec((LANES, LANES), lambda i: (0, 0))],
        out_specs=pl.BlockSpec((ts, D_QKV), lambda i: (i, 0)),
        compiler_params=_params('parallel'),
    )(proj_a, proj_a, proj_a, qg, kg, _pair_ones())


def _qk_prep_bwd(proj_a, dq, dk, dv, qg, kg, *, name, ts=512):
    S = proj_a.shape[0]
    ts = _tile(S, ts, 16)

    def body(q_ref, k_ref, dq_ref, dk_ref, dv_ref, qg_ref, kg_ref, e_ref, o_ref, dqg_ref, dkg_ref):
        i = pl.program_id(0)
        e = e_ref[...]

        def norm_bwd(xv, dn, gain, scale):
            ms = _head_sums(xv * xv, e) * (1.0 / HEAD_DIM)
            r = lax.rsqrt(ms + EPS)
            y = xv * r
            dy = dn * (gain * scale)
            mean = _head_sums(dy * y, e) * (1.0 / HEAD_DIM)
            return r * (dy - y * mean), jnp.sum(dn * y, axis=0, keepdims=True) * scale

        dq_raw, dqg = norm_bwd(q_ref[...], dq_ref[...], qg_ref[...], ATT_SCALE)
        dk_raw, dkg = norm_bwd(k_ref[...], dk_ref[...], kg_ref[...], 1.0)
        o_ref[:, 0:D_ATT] = dq_raw.astype(BF16)
        o_ref[:, D_ATT:2 * D_ATT] = dk_raw.astype(BF16)
        o_ref[:, 2 * D_ATT:3 * D_ATT] = dv_ref[...].astype(BF16)

        @pl.when(i == 0)
        def _():
            dqg_ref[...] = dqg
            dkg_ref[...] = dkg

        @pl.when(i > 0)
        def _():
            dqg_ref[...] += dqg
            dkg_ref[...] += dkg

    col = lambda c: pl.BlockSpec((ts, D_ATT), lambda i: (i, c))
    vec = pl.BlockSpec((1, D_ATT), lambda i: (0, 0))
    return pl.pallas_call(
        body, name=name,
        out_shape=(jax.ShapeDtypeStruct((S, D_PROJ_A), BF16), jax.ShapeDtypeStruct((1, D_ATT), F32),
                   jax.ShapeDtypeStruct((1, D_ATT), F32)),
        grid=(S // ts,),
        in_specs=[col(0), col(1), col(0), col(0), col(0), vec, vec, pl.BlockSpec((LANES, LANES), lambda i: (0, 0))],
        out_specs=(pl.BlockSpec((ts, D_QKV), lambda i: (i, 0)), vec, vec),
        compiler_params=_params('arbitrary'),
    )(proj_a, proj_a, dq, dk, dv, qg, kg, _pair_ones())


def _tri_ones(upper):
    i = lax.broadcasted_iota(jnp.int32, (LANES, LANES), 0)
    j = lax.broadcasted_iota(jnp.int32, (LANES, LANES), 1)
    return ((i <= j) if upper else (i >= j)).astype(BF16)


def _forget_fwd(z_raw, b_col, *, name):
    R, S = z_raw.shape
    nb = S // LANES

    def body(z_ref, b_ref, u_ref, f_ref):
        u = u_ref[...]
        carry = jnp.zeros((R, 1), F32)
        for j in range(nb):
            z = z_ref[:, j * LANES:(j + 1) * LANES] + b_ref[...]
            logf = jnp.minimum(z, 0.0) - jnp.log(1.0 + jnp.exp(-jnp.abs(z)))
            f_ref[:, j * LANES:(j + 1) * LANES] = _dot3(logf, u) + carry
            carry = carry + jnp.sum(logf, axis=1, keepdims=True)

    return pl.pallas_call(
        body, name=name, out_shape=jax.ShapeDtypeStruct((R, S), F32),
        compiler_params=pltpu.CompilerParams(vmem_limit_bytes=VMEM_LIMIT_BYTES),
    )(z_raw, b_col, _tri_ones(True))


def _forget_bwd(z_raw, b_col, df, *, name):
    R, S = z_raw.shape
    nb = S // LANES

    def body(z_ref, b_ref, df_ref, l_ref, dz_ref, db_ref):
        low = l_ref[...]
        carry = jnp.zeros((R, 1), F32)
        db = jnp.zeros((R, 1), F32)
        for j in reversed(range(nb)):
            d = df_ref[:, j * LANES:(j + 1) * LANES]
            dlogf = _dot3(d, low) + carry
            carry = carry + jnp.sum(d, axis=1, keepdims=True)
            z = z_ref[:, j * LANES:(j + 1) * LANES] + b_ref[...]
            dz = dlogf * _sigmoid(-z)
            dz_ref[:, j * LANES:(j + 1) * LANES] = dz
            db = db + jnp.sum(dz, axis=1, keepdims=True)
        db_ref[...] = db

    return pl.pallas_call(
        body, name=name,
        out_shape=(jax.ShapeDtypeStruct((R, S), F32), jax.ShapeDtypeStruct((R, 1), F32)),
        compiler_params=pltpu.CompilerParams(vmem_limit_bytes=VMEM_LIMIT_BYTES),
    )(z_raw, b_col, df, _tri_ones(False))


def _head_mask(hh):
    lane = lax.broadcasted_iota(jnp.int32, (1, LANES), 1)
    return (lane // HEAD_DIM) == hh


def _causal(s, qi, ki, t):
    rows = qi * t + lax.broadcasted_iota(jnp.int32, (t, t), 0)
    cols = ki * t + lax.broadcasted_iota(jnp.int32, (t, t), 1)
    return jnp.where(cols <= rows, s, NEG)


AUG = 2 * HEAD_DIM


def _aug_consts():
    i = lax.broadcasted_iota(jnp.int32, (D_ATT, N_HEADS * AUG), 0)
    j = lax.broadcasted_iota(jnp.int32, (D_ATT, N_HEADS * AUG), 1)
    spread = (j == (i // HEAD_DIM) * AUG + i % HEAD_DIM).astype(BF16)
    h = lax.broadcasted_iota(jnp.int32, (LANES, N_HEADS * AUG), 0)
    c = lax.broadcasted_iota(jnp.int32, (LANES, N_HEADS * AUG), 1)
    gate = [((c == h * AUG + HEAD_DIM + t) & (h < N_HEADS)).astype(BF16) for t in range(3)]
    lane = lax.broadcasted_iota(jnp.int32, (1, N_HEADS * AUG), 1) % AUG
    ones_q = ((lane >= HEAD_DIM) & (lane < HEAD_DIM + 3)).astype(F32)
    ones_v = (lane == HEAD_DIM).astype(F32)
    return spread, gate, ones_q, ones_v


def _attn_aug(qkv, f_cum, *, name, ts=512):
    S = qkv.shape[0]
    ts = _tile(S, ts)
    spread, gate, ones_q, ones_v = _aug_consts()
    W = N_HEADS * AUG

    def body(q_ref, k_ref, v_ref, f_ref, sp_ref, g0_ref, g1_ref, g2_ref, oq_ref, ov_ref, qa_ref, ka_ref, va_ref):
        sp = sp_ref[...]
        qa_ref[...] = (_dot(q_ref[...], sp, 1, 0) + oq_ref[...]).astype(BF16)
        va_ref[...] = (_dot(v_ref[...], sp, 1, 0) + ov_ref[...]).astype(BF16)
        terms = _split3(-jnp.transpose(f_ref[...]))
        ka = _dot(k_ref[...], sp, 1, 0)
        for t, g_ref in zip(terms, (g0_ref, g1_ref, g2_ref)):
            ka = ka + _dot(t, g_ref[...], 1, 0)
        ka_ref[...] = ka.astype(BF16)

    col = lambda c: pl.BlockSpec((ts, D_ATT), lambda i: (i, c))
    full = lambda a: pl.BlockSpec(a.shape, lambda i: (0, 0))
    out = pl.BlockSpec((ts, W), lambda i: (i, 0))
    shape = jax.ShapeDtypeStruct((S, W), BF16)
    consts = (spread, *gate, ones_q, ones_v)
    return pl.pallas_call(
        body, name=name, out_shape=(shape, shape, shape), grid=(S // ts,),
        in_specs=[col(0), col(1), col(2), pl.BlockSpec((FG_ROWS, ts), lambda i: (0, i))] + [full(a) for a in consts],
        out_specs=(out, out, out),
        compiler_params=_params('parallel'),
    )(qkv, qkv, qkv, f_cum, *consts)


def _attn_fwd(qa, ka, va, *, name, tq=1024, tk=1024):
    S = qa.shape[0]
    tq, tk = _tile(S, tq), _tile(S, tk)
    nq, nk = S // tq, S // tk
    npair = N_HEADS // 2

    def body(q_ref, k_ref, v_ref, mix_ref, o_ref, lse_ref, m_s, acc_s):
        qi, ki = pl.program_id(1), pl.program_id(2)
        last = (qi * tq + tq - 1) // tk
        first_masked = (qi * tq) // tk

        @pl.when(ki == 0)
        def _():
            m_s[...] = jnp.full(m_s.shape, NEG, F32)
            acc_s[...] = jnp.zeros(acc_s.shape, F32)

        def step(masked):
            if masked:
                rows = qi * tq + lax.broadcasted_iota(jnp.int32, (tq, tk), 0)
                cols = ki * tk + lax.broadcasted_iota(jnp.int32, (tq, tk), 1)
                keep = cols <= rows
            m_prev = [m_s[hh] for hh in range(2)]
            acc_prev = [acc_s[hh] for hh in range(2)]
            ss = []
            for hh in range(2):
                s = _dot(q_ref[:, hh * AUG:(hh + 1) * AUG], k_ref[:, hh * AUG:(hh + 1) * AUG], 1, 1)
                ss.append(jnp.where(keep, s, NEG) if masked else s)
            m_new = [jnp.maximum(m_prev[hh], jnp.max(ss[hh], axis=1, keepdims=True)) for hh in range(2)]
            ps = [jnp.exp(ss[hh] - jnp.tile(m_new[hh], (1, tk // LANES))).astype(BF16) for hh in range(2)]
            for hh in range(2):
                alpha = jnp.exp(m_prev[hh] - m_new[hh])
                acc_s[hh] = alpha * acc_prev[hh] + _dot(ps[hh], v_ref[:, hh * AUG:(hh + 1) * AUG], 1, 0)
                m_s[hh] = m_new[hh]

        @pl.when(ki < first_masked)
        def _():
            step(False)

        @pl.when((ki >= first_masked) & (ki <= last))
        def _():
            step(True)

        @pl.when(ki == last)
        def _():
            lane = lax.broadcasted_iota(jnp.int32, (1, LANES), 1)
            outs, lses = [], []
            for hh in range(2):
                acc = acc_s[hh]
                denom = jnp.sum(jnp.where(lane == HEAD_DIM, acc, 0.0), axis=1, keepdims=True)
                outs.append(acc / denom)
                lses.append(m_s[hh] + jnp.log(denom))
            o = jnp.where(lane < HEAD_DIM, outs[0], pltpu.roll(outs[1], HEAD_DIM, 1))
            o_ref[...] = o
            mix_ref[...] = o.astype(BF16)
            lse_ref[...] = jnp.where(lane < HEAD_DIM, lses[0], lses[1])

    def kmap(h, i, j):
        return (jnp.minimum(j, (i * tq + tq - 1) // tk), h)

    out = pl.BlockSpec((tq, LANES), lambda h, i, j: (i, h))
    return pl.pallas_call(
        body, name=name,
        out_shape=(jax.ShapeDtypeStruct((S, D_MIX), BF16), jax.ShapeDtypeStruct((S, D_ATT), F32),
                   jax.ShapeDtypeStruct((S, D_ATT), F32)),
        grid=(npair, nq, nk),
        in_specs=[pl.BlockSpec((tq, 2 * AUG), lambda h, i, j: (i, h)),
                  pl.BlockSpec((tk, 2 * AUG), kmap), pl.BlockSpec((tk, 2 * AUG), kmap)],
        out_specs=(out, out, out),
        scratch_shapes=[pltpu.VMEM((2, tq, LANES), F32), pltpu.VMEM((2, tq, LANES), F32)],
        compiler_params=_params('parallel', 'parallel', 'arbitrary'),
    )(qa, ka, va)


def _attn_bwd(qkv, f3, att, lse, d_mix, *, name, t=1024):
    S = qkv.shape[0]
    t = _tile(S, t)
    n = S // t
    npair = N_HEADS // 2

    def body(q_ref, k_ref, v_ref, f_ref, o_ref, lse_ref, do_ref, dq_ref, dk_ref, dv_ref, df_ref, dr_ref, dk_s, dv_s, df_s):
        ki, qi = pl.program_id(1), pl.program_id(2)

        @pl.when(qi == ki)
        def _():
            dk_s[...] = jnp.zeros(dk_s.shape, F32)
            dv_s[...] = jnp.zeros(dv_s.shape, F32)
            df_s[...] = jnp.zeros(df_s.shape, F32)

        def step(masked):
            q, k, v = q_ref[...], k_ref[...], v_ref[...]
            do, o, lse = do_ref[...], o_ref[...], lse_ref[...]
            lane = lax.broadcasted_iota(jnp.int32, (1, LANES), 1)
            lse_sw = pltpu.roll(lse, HEAD_DIM, 1)
            delta = _dot3(do.astype(BF16).astype(F32) * o, _pair_ones())
            delta_sw = pltpu.roll(delta, HEAD_DIM, 1)
            dq_blk = jnp.zeros((t, LANES), F32)
            dr_blk = jnp.zeros((t, LANES), F32)
            for hh in range(2):
                msk = _head_mask(hh)
                first = lane < HEAD_DIM if hh == 0 else lane >= HEAD_DIM
                qm = jnp.where(msk, q, jnp.zeros_like(q))
                km = jnp.where(msk, k, jnp.zeros_like(k))
                do_h = jnp.where(msk, do, 0.0)
                dom = do_h.astype(BF16)
                s = _dot(qm, k, 1, 1) - f_ref[0, hh:hh + 1, :]
                if masked:
                    s = _causal(s, qi, ki, t)
                lse_h = jnp.where(first, lse, lse_sw)
                delta_h = jnp.where(first, delta, delta_sw)
                p = jnp.exp(s - jnp.tile(lse_h, (1, t // LANES)))
                dp = _dot(dom, v, 1, 1)
                ds = p * (dp - jnp.tile(delta_h, (1, t // LANES)))
                dsb = ds.astype(BF16)
                dv_s[...] += _dot(jnp.transpose(do_h).astype(BF16), p.astype(BF16), 1, 0)
                dk_s[...] += _dot(jnp.transpose(qm.astype(F32)).astype(BF16), dsb, 1, 0)
                dq_blk = dq_blk + _dot(dsb, km, 1, 0)
                df_s[hh] -= jnp.sum(ds, axis=0, keepdims=True)
                dr_blk = dr_blk + jnp.where(msk, jnp.sum(ds, axis=1, keepdims=True), 0.0)
            rows = pl.ds(pl.multiple_of(qi * t, t), t)

            @pl.when(ki == 0)
            def _():
                dq_ref[rows, :] = dq_blk
                dr_ref[rows, :] = dr_blk

            @pl.when(ki > 0)
            def _():
                dq_ref[rows, :] += dq_blk
                dr_ref[rows, :] += dr_blk

        @pl.when(qi > ki)
        def _():
            step(False)

        @pl.when(qi == ki)
        def _():
            step(True)

        @pl.when(qi == n - 1)
        def _():
            dk_ref[...] = jnp.transpose(dk_s[...])
            dv_ref[...] = jnp.transpose(dv_s[...])
            df_ref[0, 0:1, :] = df_s[0]
            df_ref[0, 1:2, :] = df_s[1]

    qrow = lambda h, j, i: (jnp.maximum(i, j), h)
    return pl.pallas_call(
        body, name=name,
        out_shape=(jax.ShapeDtypeStruct((S, D_ATT), F32), jax.ShapeDtypeStruct((S, D_ATT), F32),
                   jax.ShapeDtypeStruct((S, D_ATT), F32), jax.ShapeDtypeStruct((npair, 2, S), F32),
                   jax.ShapeDtypeStruct((S, D_ATT), F32)),
        grid=(npair, n, n),
        in_specs=[pl.BlockSpec((t, LANES), qrow),
                  pl.BlockSpec((t, LANES), lambda h, j, i: (j, npair + h)),
                  pl.BlockSpec((t, LANES), lambda h, j, i: (j, 2 * npair + h)),
                  pl.BlockSpec((1, 2, t), lambda h, j, i: (h, 0, j)),
                  pl.BlockSpec((t, LANES), qrow),
                  pl.BlockSpec((t, LANES), qrow),
                  pl.BlockSpec((t, LANES), qrow)],
        out_specs=(pl.BlockSpec((S, LANES), lambda h, j, i: (0, h)),
                   pl.BlockSpec((t, LANES), lambda h, j, i: (j, h)),
                   pl.BlockSpec((t, LANES), lambda h, j, i: (j, h)),
                   pl.BlockSpec((1, 2, t), lambda h, j, i: (h, 0, j)),
                   pl.BlockSpec((S, LANES), lambda h, j, i: (0, h))),
        scratch_shapes=[pltpu.VMEM((LANES, t), F32), pltpu.VMEM((LANES, t), F32), pltpu.VMEM((2, 1, t), F32)],
        compiler_params=_params('parallel', 'arbitrary', 'arbitrary'),
    )(qkv, qkv, qkv, f3, att, lse, d_mix)


A_COL = D_QKV // D_CONV
B_COL = A_COL + 1
P_COL = B_COL + 1


CONV_BLOCKS = D_CONV // LANES
CONV_GROUP = 8 * 8


def _rows8(ref, c, row):
    return ref.at[c][pl.ds(row, 8, stride=8), :]


def _put8(ref, c, row, val):
    ref.at[c][pl.ds(row, 8, stride=8), :] = val


def _lanes(c):
    return slice(c * LANES, (c + 1) * LANES)


def _glu_into(buf, a_ref, b_ref, ah_ref, bh_ref, first, ts):
    for c in range(CONV_BLOCKS):
        halo = ah_ref[:, _lanes(c)] * _sigmoid(bh_ref[:, _lanes(c)])
        buf[c, 0:CONV_HALO, :] = jnp.where(first, 0.0, halo)
        buf[c, CONV_HALO:CONV_HALO + ts, :] = a_ref[:, _lanes(c)] * _sigmoid(b_ref[:, _lanes(c)])


def _conv_taps(buf, c, r0):
    return [_rows8(buf, c, CONV_HALO + r0 + i - (CONV_WIDTH - 1)) for i in range(CONV_WIDTH - 1 + 8)]


def _dwconv8(xs, ws, bias):
    outs = []
    for j in range(8):
        acc = ws[0] * xs[j]
        for k in range(1, CONV_WIDTH):
            acc = acc + ws[k] * xs[j + k]
        outs.append(acc + bias)
    return outs


def _ln8(cs):
    inv = 1.0 / D_CONV
    mu = sum(jnp.sum(c, axis=1, keepdims=True) for c in cs) * inv
    xc = [c - mu for c in cs]
    rstd = lax.rsqrt(sum(jnp.sum(x * x, axis=1, keepdims=True) for x in xc) * inv + EPS)
    return [x * rstd for x in xc], rstd


def _conv_specs(ts, tmap):
    hb = ts // CONV_HALO
    cur = lambda c: pl.BlockSpec((ts, D_CONV), lambda i: (tmap(i), c))
    halo = lambda c: pl.BlockSpec((CONV_HALO, D_CONV), lambda i: (jnp.maximum(tmap(i) * hb - 1, 0), c))
    return cur, halo


def _conv_fwd(proj_a, mix, dw_w, dw_b, ln_g, ln_b, pw_w, l, *, name, ts=512):
    S = proj_a.shape[0]
    ts = _tile(S, ts, CONV_GROUP)

    def body(a_ref, b_ref, ah_ref, bh_ref, w_ref, wb_ref, g_ref, bb_ref, pw_ref, mix_in, o_ref, buf, stage):
        _glu_into(buf, a_ref, b_ref, ah_ref, bh_ref, pl.program_id(0) == 0, ts)
        ws = [[w_ref[k:k + 1, _lanes(c)] for k in range(CONV_WIDTH)] for c in range(CONV_BLOCKS)]
        for r0 in range(0, ts, CONV_GROUP):
            conv = [_dwconv8(_conv_taps(buf, c, r0), ws[c], wb_ref[:, _lanes(c)]) for c in range(CONV_BLOCKS)]
            for j in range(8):
                yhat, _ = _ln8([conv[c][j] for c in range(CONV_BLOCKS)])
                for c in range(CONV_BLOCKS):
                    y = yhat[c] * g_ref[:, _lanes(c)] + bb_ref[:, _lanes(c)]
                    _put8(stage, c, r0 + j, y * _sigmoid(y))
        hs = jnp.concatenate([stage[c] for c in range(CONV_BLOCKS)], axis=1)
        o_ref[...] = _dot(hs.astype(BF16), pw_ref[...], 1, 0).astype(BF16)

    cur, halo = _conv_specs(ts, lambda i: i)
    vec = pl.BlockSpec((1, D_CONV), lambda i: (0, 0))
    return pl.pallas_call(
        body, name=name, out_shape=jax.ShapeDtypeStruct(mix.shape, BF16), grid=(S // ts,),
        in_specs=[cur(A_COL), cur(B_COL), halo(A_COL), halo(B_COL),
                  pl.BlockSpec((None, CONV_HALO, D_CONV), lambda i: (l, 0, 0)), vec, vec, vec,
                  pl.BlockSpec((None, D_CONV, D_CONV), lambda i: (l, 0, 0)), ANY_SPEC],
        out_specs=pl.BlockSpec((ts, D_CONV), lambda i: (i, D_ATT // D_CONV)),
        scratch_shapes=[pltpu.VMEM((CONV_BLOCKS, CONV_HALO + ts, LANES), F32), pltpu.VMEM((CONV_BLOCKS, ts, LANES), F32)],
        input_output_aliases={9: 0},
        compiler_params=_params('parallel'),
    )(proj_a, proj_a, proj_a, proj_a, dw_w, dw_b, ln_g, ln_b, pw_w, mix)


def _conv_bwd(proj_a, d_mix, d_proj, dw_w, dw_b, ln_g, ln_b, pw_w, l, *, name, ts=512):
    S = proj_a.shape[0]
    ts = _tile(S, ts, CONV_GROUP)
    n = S // ts
    d_col = D_ATT // D_CONV
    groups = range(0, ts, CONV_GROUP)

    def body(a_ref, b_ref, ah_ref, bh_ref, dy_ref, w_ref, wb_ref, g_ref, bb_ref, pw_ref, dp_in,
             o_ref, dw_ref, dwb_ref, dg_ref, dbb_ref, dpw_ref, buf, dcbuf, stage, stage2):
        i = pl.program_id(0)
        _glu_into(buf, a_ref, b_ref, ah_ref, bh_ref, i == n - 1, ts)

        @pl.when(i == 0)
        def _():
            dcbuf[:, ts:ts + CONV_HALO, :] = jnp.zeros((CONV_BLOCKS, CONV_HALO, LANES), F32)
            dw_ref[...] = jnp.zeros(dw_ref.shape, F32)
            dwb_ref[...] = jnp.zeros(dwb_ref.shape, F32)
            dg_ref[...] = jnp.zeros(dg_ref.shape, F32)
            dbb_ref[...] = jnp.zeros(dbb_ref.shape, F32)
            dpw_ref[...] = jnp.zeros(dpw_ref.shape, F32)

        dout = dy_ref[...].astype(BF16)
        d_hs = _dot(dout, pw_ref[...], 1, 1)
        for c in range(CONV_BLOCKS):
            stage2[c, :, :] = d_hs[:, _lanes(c)]
        ws = [[w_ref[k:k + 1, _lanes(c)] for k in range(CONV_WIDTH)] for c in range(CONV_BLOCKS)]
        zero8 = jnp.zeros((8, LANES), F32)
        dg = [zero8] * CONV_BLOCKS
        dbb = [zero8] * CONV_BLOCKS
        dwb = [zero8] * CONV_BLOCKS
        for r0 in groups:
            conv = [_dwconv8(_conv_taps(buf, c, r0), ws[c], wb_ref[:, _lanes(c)]) for c in range(CONV_BLOCKS)]
            for j in range(8):
                yhat, rstd = _ln8([conv[c][j] for c in range(CONV_BLOCKS)])
                d_yhat = []
                for c in range(CONV_BLOCKS):
                    y = yhat[c] * g_ref[:, _lanes(c)] + bb_ref[:, _lanes(c)]
                    sg = _sigmoid(y)
                    _put8(stage, c, r0 + j, y * sg)
                    d_y = _rows8(stage2, c, r0 + j) * (sg * (1.0 + y * (1.0 - sg)))
                    dg[c] = dg[c] + d_y * yhat[c]
                    dbb[c] = dbb[c] + d_y
                    d_yhat.append(d_y * g_ref[:, _lanes(c)])
                inv = 1.0 / D_CONV
                m1 = sum(jnp.sum(d, axis=1, keepdims=True) for d in d_yhat) * inv
                m2 = sum(jnp.sum(d * yh, axis=1, keepdims=True) for d, yh in zip(d_yhat, yhat)) * inv
                for c in range(CONV_BLOCKS):
                    d_c = rstd * (d_yhat[c] - m1 - yhat[c] * m2)
                    dwb[c] = dwb[c] + d_c
                    _put8(dcbuf, c, r0 + j, d_c)
        for c in range(CONV_BLOCKS):
            dg_ref[:, _lanes(c)] += jnp.sum(dg[c], axis=0, keepdims=True)
            dbb_ref[:, _lanes(c)] += jnp.sum(dbb[c], axis=0, keepdims=True)
            dwb_ref[:, _lanes(c)] += jnp.sum(dwb[c], axis=0, keepdims=True)
        hs = jnp.concatenate([stage[c] for c in range(CONV_BLOCKS)], axis=1)
        dpw_ref[...] += _dot(hs.astype(BF16), dout, 0, 0)
        for c in range(CONV_BLOCKS):
            for r0 in groups:
                dcs = [_rows8(dcbuf, c, r0 + i_) for i_ in range(CONV_WIDTH - 1 + 8)]
                for j in range(8):
                    acc = ws[c][0] * dcs[j + CONV_WIDTH - 1]
                    for k in range(1, CONV_WIDTH):
                        acc = acc + ws[c][k] * dcs[j + CONV_WIDTH - 1 - k]
                    _put8(stage2, c, r0 + j, acc)
            for k in range(CONV_WIDTH):
                acc = zero8
                for r0 in groups:
                    for j in range(8):
                        acc = acc + _rows8(dcbuf, c, r0 + j) * _rows8(buf, c, CONV_HALO + r0 + j - (CONV_WIDTH - 1) + k)
                dw_ref[k:k + 1, _lanes(c)] += jnp.sum(acc, axis=0, keepdims=True)
            dcbuf[c, ts:ts + CONV_HALO, :] = dcbuf[c, 0:CONV_HALO, :]
        d_h = jnp.concatenate([stage2[c] for c in range(CONV_BLOCKS)], axis=1)
        a, sb = a_ref[...], _sigmoid(b_ref[...])
        o_ref[:, 0:D_CONV] = (d_h * sb).astype(BF16)
        o_ref[:, D_CONV:2 * D_CONV] = (d_h * a * sb * (1.0 - sb)).astype(BF16)

    rev = lambda i: n - 1 - i
    cur, halo = _conv_specs(ts, rev)
    vec = pl.BlockSpec((1, D_CONV), lambda i: (0, 0))
    wspec = pl.BlockSpec((CONV_HALO, D_CONV), lambda i: (0, 0))
    sq = pl.BlockSpec((D_CONV, D_CONV), lambda i: (0, 0))
    tile3 = pltpu.VMEM((CONV_BLOCKS, ts, LANES), F32)
    return pl.pallas_call(
        body, name=name,
        out_shape=(jax.ShapeDtypeStruct(d_proj.shape, BF16), jax.ShapeDtypeStruct((CONV_HALO, D_CONV), F32),
                   jax.ShapeDtypeStruct((1, D_CONV), F32), jax.ShapeDtypeStruct((1, D_CONV), F32),
                   jax.ShapeDtypeStruct((1, D_CONV), F32), jax.ShapeDtypeStruct((D_CONV, D_CONV), F32)),
        grid=(n,),
        in_specs=[cur(A_COL), cur(B_COL), halo(A_COL), halo(B_COL),
                  pl.BlockSpec((ts, D_CONV), lambda i: (rev(i), d_col)),
                  pl.BlockSpec((None, CONV_HALO, D_CONV), lambda i: (l, 0, 0)), vec, vec, vec,
                  pl.BlockSpec((None, D_CONV, D_CONV), lambda i: (l, 0, 0)), ANY_SPEC],
        out_specs=(pl.BlockSpec((ts, 2 * D_CONV), lambda i: (rev(i), D_QKV // (2 * D_CONV))), wspec, vec, vec, vec, sq),
        scratch_shapes=[pltpu.VMEM((CONV_BLOCKS, CONV_HALO + ts, LANES), F32),
                        pltpu.VMEM((CONV_BLOCKS, ts + CONV_HALO, LANES), F32), tile3, tile3],
        input_output_aliases={10: 0},
        compiler_params=_params('arbitrary'),
    )(proj_a, proj_a, proj_a, proj_a, d_mix, dw_w, dw_b, ln_g, ln_b, pw_w, d_proj)


POOL_BLOCKS = D_POOL // LANES
POOL_SPAN = max(POOL_WINDOWS) - 1


def _pool_sum8(xs, j, c, step):
    lo, hi = POOL_WINDOWS[2 * c], POOL_WINDOWS[2 * c + 1]
    acc = xs[j]
    for d in range(1, lo):
        acc = acc + xs[j + step * d]
    more = xs[j + step * lo]
    for d in range(lo + 1, hi):
        more = more + xs[j + step * d]
    lane = lax.broadcasted_iota(jnp.int32, (1, LANES), 1)
    return acc + jnp.where(lane >= POOL_GROUP, more, 0.0)


def _pool_count8(c, row):
    lane = lax.broadcasted_iota(jnp.int32, (1, LANES), 1)
    wl = jnp.where(lane >= POOL_GROUP, POOL_WINDOWS[2 * c + 1], POOL_WINDOWS[2 * c])
    pos = row + 8 * lax.broadcasted_iota(jnp.int32, (8, 1), 0)
    return jnp.minimum(pos + 1, wl).astype(F32)


def _pool_diff_into(stage, buf, u_ref, uh_ref, first, tile, ts):
    for c in range(POOL_BLOCKS):
        buf[c, 0:POOL_HALO, :] = jnp.where(first, 0.0, uh_ref[:, _lanes(c)])
        buf[c, POOL_HALO:POOL_HALO + ts, :] = u_ref[:, _lanes(c)]
        for r0 in range(0, ts, CONV_GROUP):
            xs = [_rows8(buf, c, POOL_HALO + r0 + i - POOL_SPAN) for i in range(POOL_SPAN + 8)]
            for j in range(8):
                mean = _pool_sum8(xs, j + POOL_SPAN, c, -1) / _pool_count8(c, tile * ts + r0 + j)
                _put8(stage, c, r0 + j, mean - xs[j + POOL_SPAN])


def _pool_specs(ts, tmap):
    hb = ts // POOL_HALO
    cur = pl.BlockSpec((ts, D_POOL), lambda i: (tmap(i), P_COL))
    halo = pl.BlockSpec((POOL_HALO, D_POOL), lambda i: (jnp.maximum(tmap(i) * hb - 1, 0), P_COL))
    return cur, halo


def _pool_fwd(proj_a, mix, wbd, scale, *, name, ts=512):
    S = proj_a.shape[0]
    ts = _tile(S, ts, CONV_GROUP)

    def body(u_ref, uh_ref, w_ref, s_ref, mix_in, o_ref, buf, stage):
        i = pl.program_id(0)
        _pool_diff_into(stage, buf, u_ref, uh_ref, i == 0, i, ts)
        d = jnp.concatenate([stage[c] for c in range(POOL_BLOCKS)], axis=1)
        o_ref[...] = (_dot(d.astype(BF16), w_ref[...], 1, 0) * s_ref[...]).astype(BF16)

    cur, halo = _pool_specs(ts, lambda i: i)
    return pl.pallas_call(
        body, name=name, out_shape=jax.ShapeDtypeStruct(mix.shape, BF16), grid=(S // ts,),
        in_specs=[cur, halo, pl.BlockSpec((D_POOL, D_POOL), lambda i: (0, 0)), pl.BlockSpec((1, D_POOL), lambda i: (0, 0)),
                  ANY_SPEC],
        out_specs=pl.BlockSpec((ts, D_POOL), lambda i: (i, (D_ATT + D_CONV) // D_POOL)),
        scratch_shapes=[pltpu.VMEM((POOL_BLOCKS, POOL_HALO + ts, LANES), F32), pltpu.VMEM((POOL_BLOCKS, ts, LANES), F32)],
        input_output_aliases={4: 0},
        compiler_params=_params('parallel'),
    )(proj_a, proj_a, wbd, scale, mix)


def _pool_bwd(proj_a, d_mix, d_proj, wbd, scale, *, name, ts=512):
    S = proj_a.shape[0]
    ts = _tile(S, ts, CONV_GROUP)
    n = S // ts
    d_col = (D_ATT + D_CONV) // D_POOL

    def body(u_ref, uh_ref, dy_ref, w_ref, s_ref, dp_in, o_ref, dw_ref, ds_ref, buf, ebuf, stage):
        i = pl.program_id(0)
        tile = n - 1 - i
        _pool_diff_into(stage, buf, u_ref, uh_ref, tile == 0, tile, ts)
        db = jnp.concatenate([stage[c] for c in range(POOL_BLOCKS)], axis=1).astype(BF16)
        ypre = _dot(db, w_ref[...], 1, 0)
        dout = dy_ref[...]
        d_y = (dout * s_ref[...]).astype(BF16)
        d_d = _dot(d_y, w_ref[...], 1, 1)

        @pl.when(i == 0)
        def _():
            ebuf[:, ts:ts + POOL_HALO, :] = jnp.zeros((POOL_BLOCKS, POOL_HALO, LANES), F32)
            dw_ref[...] = jnp.zeros(dw_ref.shape, F32)
            ds_ref[...] = jnp.zeros(ds_ref.shape, F32)

        dw_ref[...] += _dot(db, d_y, 0, 0)
        ds_ref[...] += jnp.sum(dout * ypre, axis=0, keepdims=True)
        for c in range(POOL_BLOCKS):
            stage[c, :, :] = d_d[:, _lanes(c)]
            for r0 in range(0, ts, CONV_GROUP):
                for j in range(8):
                    _put8(ebuf, c, r0 + j, _rows8(stage, c, r0 + j) / _pool_count8(c, tile * ts + r0 + j))
            for r0 in range(0, ts, CONV_GROUP):
                es = [_rows8(ebuf, c, r0 + i_) for i_ in range(POOL_SPAN + 8)]
                for j in range(8):
                    _put8(stage, c, r0 + j, _pool_sum8(es, j, c, 1) - _rows8(stage, c, r0 + j))
            ebuf[c, ts:ts + POOL_HALO, :] = ebuf[c, 0:POOL_HALO, :]
        o_ref[...] = jnp.concatenate([stage[c] for c in range(POOL_BLOCKS)], axis=1).astype(BF16)

    rev = lambda i: n - 1 - i
    cur, halo = _pool_specs(ts, rev)
    sq = pl.BlockSpec((D_POOL, D_POOL), lambda i: (0, 0))
    vec = pl.BlockSpec((1, D_POOL), lambda i: (0, 0))
    return pl.pallas_call(
        body, name=name,
        out_shape=(jax.ShapeDtypeStruct(d_proj.shape, BF16), jax.ShapeDtypeStruct((D_POOL, D_POOL), F32),
                   jax.ShapeDtypeStruct((1, D_POOL), F32)),
        grid=(n,),
        in_specs=[cur, halo, pl.BlockSpec((ts, D_POOL), lambda i: (rev(i), d_col)), sq, vec, ANY_SPEC],
        out_specs=(pl.BlockSpec((ts, D_POOL), lambda i: (rev(i), P_COL)), sq, vec),
        scratch_shapes=[pltpu.VMEM((POOL_BLOCKS, POOL_HALO + ts, LANES), F32),
                        pltpu.VMEM((POOL_BLOCKS, ts + POOL_HALO, LANES), F32), pltpu.VMEM((POOL_BLOCKS, ts, LANES), F32)],
        input_output_aliases={5: 0},
        compiler_params=_params('arbitrary'),
    )(proj_a, proj_a, d_mix, wbd, scale, d_proj)


FFN_LANES = 128
FFN_GROUP = 8 * 8


def _ffn_rows(ref, c, row0, j):
    return ref.at[c][pl.ds(row0 + j, 8, stride=8), :]


def _ffn_specs(ts, tc2, tmap, l):
    hb = ts // FFN_HALO
    cur = pl.BlockSpec((ts, tc2), lambda c, i: (tmap(i), c))
    halo = pl.BlockSpec((FFN_HALO, tc2), lambda c, i: (jnp.maximum(tmap(i) * hb - 1, 0), c))
    wspec = pl.BlockSpec((None, FFN_HALO, tc2), lambda c, i: (l, 0, c))
    return cur, halo, wspec


def _ffn_fill(buf, x_ref, xh_ref, first, ts, nblk):
    for c in range(nblk):
        cs = slice(c * FFN_LANES, (c + 1) * FFN_LANES)
        buf[c, 0:FFN_HALO, :] = jnp.where(first, 0.0, xh_ref[:, cs])
        buf[c, FFN_HALO:FFN_HALO + ts, :] = x_ref[:, cs]


def _ffn_conv_piece(buf, w_ref, r0, c):
    ws = [w_ref[k:k + 1, c * FFN_LANES:(c + 1) * FFN_LANES] for k in range(FFN_CONV_WIDTH)]
    xs = [_ffn_rows(buf, c, FFN_HALO + r0, j) for j in range(1 - FFN_CONV_WIDTH, 8)]
    outs = []
    for j in range(8):
        acc = ws[0] * xs[j]
        for k in range(1, FFN_CONV_WIDTH):
            acc = acc + ws[k] * xs[j + k]
        outs.append(acc)
    return outs, xs


def _ffn_act_fwd(up, w, l, *, name, ts=512):
    S, F2 = up.shape
    tc = F2 // 4
    nb = tc // FFN_LANES
    ts = _tile(S, ts, FFN_GROUP)

    def body(x_ref, xh_ref, w_ref, o_ref, buf, stage):
        _ffn_fill(buf, x_ref, xh_ref, pl.program_id(1) == 0, ts, 2 * nb)
        for c in range(nb):
            for r0 in range(0, ts, FFN_GROUP):
                gates, _ = _ffn_conv_piece(buf, w_ref, r0, c)
                vals, _ = _ffn_conv_piece(buf, w_ref, r0, nb + c)
                for j in range(8):
                    stage.at[c][pl.ds(r0 + j, 8, stride=8), :] = gates[j] * _sigmoid(gates[j]) * vals[j]
            o_ref[:, c * FFN_LANES:(c + 1) * FFN_LANES] = stage[c].astype(BF16)

    cur, halo, wspec = _ffn_specs(ts, 2 * tc, lambda i: i, l)
    return pl.pallas_call(
        body, name=name, out_shape=jax.ShapeDtypeStruct((S, F2 // 2), BF16), grid=(2, S // ts),
        in_specs=[cur, halo, wspec],
        out_specs=pl.BlockSpec((ts, tc), lambda c, i: (i, c)),
        scratch_shapes=[pltpu.VMEM((2 * nb, FFN_HALO + ts, FFN_LANES), F32), pltpu.VMEM((nb, ts, FFN_LANES), F32)],
        compiler_params=_params('parallel', 'parallel'),
    )(up, up, w)


def _ffn_act_bwd(up, d_act, w, l, *, name, ts=512):
    S, F2 = up.shape
    tc = F2 // 4
    nb = tc // FFN_LANES
    ts = _tile(S, ts, FFN_GROUP)
    n = S // ts

    def body(x_ref, xh_ref, da_ref, w_ref, o_ref, dw_ref, buf, dcbuf, stage):
        i = pl.program_id(1)
        _ffn_fill(buf, x_ref, xh_ref, i == n - 1, ts, 2 * nb)

        @pl.when(i == 0)
        def _():
            dcbuf[:, ts:ts + FFN_HALO, :] = jnp.zeros((2 * nb, FFN_HALO, FFN_LANES), F32)
            dw_ref[...] = jnp.zeros(dw_ref.shape, F32)

        for c in range(nb):
            blocks = (c, nb + c)
            stage[c, :, :] = da_ref[:, c * FFN_LANES:(c + 1) * FFN_LANES]
            dws = [[jnp.zeros((8, FFN_LANES), F32) for _ in range(FFN_CONV_WIDTH)] for _ in range(2)]
            for r0 in range(0, ts, FFN_GROUP):
                gates, xg = _ffn_conv_piece(buf, w_ref, r0, blocks[0])
                vals, xv = _ffn_conv_piece(buf, w_ref, r0, blocks[1])
                for j in range(8):
                    sg = _sigmoid(gates[j])
                    da = _ffn_rows(stage, c, r0, j)
                    d_cs = (da * vals[j] * (sg * (1.0 + gates[j] * (1.0 - sg))), da * (gates[j] * sg))
                    for half, (d_c, xs) in enumerate(zip(d_cs, (xg, xv))):
                        dcbuf.at[blocks[half]][pl.ds(r0 + j, 8, stride=8), :] = d_c
                        for k in range(FFN_CONV_WIDTH):
                            dws[half][k] = dws[half][k] + d_c * xs[j + k]
            for half in range(2):
                cs = slice(blocks[half] * FFN_LANES, (blocks[half] + 1) * FFN_LANES)
                for k in range(FFN_CONV_WIDTH):
                    dw_ref[k:k + 1, cs] += jnp.sum(dws[half][k], axis=0, keepdims=True)
            for b in blocks:
                cs = slice(b * FFN_LANES, (b + 1) * FFN_LANES)
                ws = [w_ref[k:k + 1, cs] for k in range(FFN_CONV_WIDTH)]
                for r0 in range(0, ts, FFN_GROUP):
                    ds = [_ffn_rows(dcbuf, b, r0, j) for j in range(8 + FFN_CONV_WIDTH - 1)]
                    for j in range(8):
                        d_x = ws[FFN_CONV_WIDTH - 1] * ds[j]
                        for k in range(FFN_CONV_WIDTH - 1):
                            d_x = d_x + ws[k] * ds[j + FFN_CONV_WIDTH - 1 - k]
                        stage.at[c][pl.ds(r0 + j, 8, stride=8), :] = d_x
                o_ref[:, cs] = stage[c].astype(BF16)
                dcbuf[b, ts:ts + FFN_HALO, :] = dcbuf[b, 0:FFN_HALO, :]

    rev = lambda i: n - 1 - i
    cur, halo, wspec = _ffn_specs(ts, 2 * tc, rev, l)
    return pl.pallas_call(
        body, name=name,
        out_shape=(jax.ShapeDtypeStruct((S, F2), BF16), jax.ShapeDtypeStruct((FFN_HALO, F2), F32)),
        grid=(2, n),
        in_specs=[cur, halo, pl.BlockSpec((ts, tc), lambda c, i: (rev(i), c)), wspec],
        out_specs=(cur, pl.BlockSpec((FFN_HALO, 2 * tc), lambda c, i: (0, c))),
        scratch_shapes=[pltpu.VMEM((2 * nb, FFN_HALO + ts, FFN_LANES), F32), pltpu.VMEM((2 * nb, ts + FFN_HALO, FFN_LANES), F32),
                        pltpu.VMEM((nb, ts, FFN_LANES), F32)],
        compiler_params=_params('parallel', 'arbitrary'),
    )(up, up, d_act, w)


def _loss_head(y, target, *, name, ts=512):
    S, D = y.shape
    ts = _tile(S, ts, 8)

    def body(y_ref, t_ref, l_ref, dy_ref):
        i = pl.program_id(0)
        err = y_ref[...] - t_ref[...]
        dy_ref[...] = err * (1.0 / D)
        part = jnp.sum(jnp.sum(err * err, axis=1, keepdims=True), axis=0, keepdims=True) * (0.5 / D)

        @pl.when(i == 0)
        def _():
            l_ref[...] = part

        @pl.when(i > 0)
        def _():
            l_ref[...] += part

    row = pl.BlockSpec((ts, D), lambda i: (i, 0))
    return pl.pallas_call(
        body, name=name,
        out_shape=(jax.ShapeDtypeStruct((1, 1), F32), jax.ShapeDtypeStruct((S, D), F32)),
        grid=(S // ts,), in_specs=[row, row], out_specs=(pl.BlockSpec((1, 1), lambda i: (0, 0)), row),
        compiler_params=_params('arbitrary'),
    )(y, target)


def _pair_cols(w):
    lead, f2 = w.shape[:-1], w.shape[-1]
    return w.reshape(lead + (2, 2, f2 // 4)).swapaxes(-3, -2).reshape(lead + (f2,))


def _pad_axis(w, size, axis):
    pad = [(0, 0)] * w.ndim
    pad[axis] = (0, size - w.shape[axis])
    return jnp.pad(w, pad)


def _block_diag(pool_w):
    g = pool_w.shape[0]
    rows = [jnp.concatenate([pool_w[i] if i == j else jnp.zeros_like(pool_w[i]) for j in range(g)], axis=1) for i in range(g)]
    return jnp.concatenate(rows, axis=0)


def _small_weights(w, l):
    return dict(
        norm1_g=w['norm1_g'][l][None, :],
        b_col=_pad_axis(w['b_f'][l][:, None], FG_ROWS, 0),
        qg=jnp.tile(w['q_norm_g'][l], N_HEADS)[None, :],
        kg=jnp.tile(w['k_norm_g'][l], N_HEADS)[None, :],
        dw_b=w['conv_dw_b'][l][None, :], ln_g=w['conv_ln_g'][l][None, :], ln_b=w['conv_ln_b'][l][None, :],
        wbd=_block_diag(w['pool_w'][l]).astype(BF16),
        pool_scale=w['pool_scale'][l][None, :],
        norm2_g=w['norm2_g'][l][None, :],
    )


def _layer_fwd(x, W, p, l):
    n = lambda s: f'l{l}_{s}'
    S = x.shape[0]
    h = _rms_fwd(x, p['norm1_g'], name=n('norm1'), after=W.started)
    proj_a = _mm(h, W.get('w_a', h), b_lead=0, name=n('proj_a'), tn=D_PROJ_A)
    z_raw = _mm(W.get('w_fg_t', h), h, a_lead=0, tb=True, name=n('proj_fg'))
    qkv = _qk_prep_fwd(proj_a, p['qg'], p['kg'], name=n('qk_norm'))
    f_cum = _forget_fwd(z_raw, p['b_col'], name=n('forget'))
    f3 = f_cum[:N_HEADS].reshape(N_HEADS // 2, 2, S)
    mix, att, lse = _attn_fwd(*_attn_aug(qkv, f_cum, name=n('attn_aug')), name=n('attn'))
    mix = _conv_fwd(proj_a, mix, W.get('dw_w', h), p['dw_b'], p['ln_g'], p['ln_b'], W.get('pw_w', h), 0, name=n('conv'))
    mix = _pool_fwd(proj_a, mix, p['wbd'], p['pool_scale'], name=n('pool'))
    x1, h2 = _mm(mix, W.get('w_out', mix), b_lead=0, res=x, name=n('out_proj'), tn=1024, rms_gain=p['norm2_g'])
    up = _mm(h2, W.get('w_up', mix), b_lead=0, name=n('up_proj'), tn=1408, cols_outer=True)
    act = _ffn_act_fwd(up, W.get('ffn_w', h), 0, name=n('ffn_act'))
    x2 = _mm(act, W.get('w_down', mix), b_lead=0, res=x1, name=n('down_proj'), tn=1024, tk=2816)
    saved = dict(x=x, h=h, proj_a=proj_a, z_raw=z_raw, qkv=qkv, f3=f3, att=att, lse=lse, mix=mix, x1=x1, h2=h2, up=up, act=act)
    return x2, saved


def _layer_bwd(dx2, W, p, s, l, sink):
    n = lambda t: f'l{l}_{t}'
    S = dx2.shape[0]
    g = {}
    W = W.ready

    def large(key, a, b, **kw):
        return sink.put(l, key, *_mm(a, b, ta=True, copy16=True, name=n('d_' + key), **kw))

    d_act = _mm(dx2, W['w_down'], b_lead=0, tb=True, name=n('d_act'), tn=1408, cols_outer=True)
    large('w_down', s['act'], dx2, tm=1408, tn=1024)
    d_up, d_ffn_w = _ffn_act_bwd(s['up'], d_act, W['ffn_w'], 0, name=n('ffn_act_bwd'))
    g['ffn_dw_w'] = _pair_cols(d_ffn_w[:FFN_CONV_WIDTH])
    d_h2 = _mm(d_up, W['w_up'], b_lead=0, tb=True, name=n('d_h2'), tn=1024, tk=5632)
    started = large('w_up', s['h2'], d_up, tm=1024, tn=512, tk=4096)
    dx1, dg2 = _rms_bwd(s['x1'], p['norm2_g'], d_h2, dx2, name=n('norm2_bwd'))
    g['norm2_g'] = dg2[0]
    sink.point(l, 'mid', dx1)
    d_mix = _mm(dx1, W['w_out'], b_lead=0, tb=True, name=n('d_mix'), tn=1024, after=started)
    large('w_out', s['mix'], dx1, tm=1024, tn=1024)
    dq, dk, dv, df3, dr = _attn_bwd(s['qkv'], s['f3'], s['att'], s['lse'], d_mix, name=n('attn_bwd'))
    df = _pad_axis(df3.reshape(N_HEADS, S) + dr[:, ::HEAD_DIM].T, FG_ROWS, 0)
    d_z, d_b = _forget_bwd(s['z_raw'], p['b_col'], df, name=n('forget_bwd'))
    g['b_f'] = d_b[:N_HEADS, 0]
    d_proj, d_qg, d_kg = _qk_prep_bwd(s['proj_a'], dq, dk, dv, p['qg'], p['kg'], name=n('qk_norm_bwd'))
    g['q_norm_g'] = d_qg.reshape(N_HEADS, HEAD_DIM).sum(axis=0)
    g['k_norm_g'] = d_kg.reshape(N_HEADS, HEAD_DIM).sum(axis=0)
    d_proj, d_dw_w, d_dw_b, d_ln_g, d_ln_b, d_pw = _conv_bwd(
        s['proj_a'], d_mix, d_proj, W['dw_w'], p['dw_b'], p['ln_g'], p['ln_b'], W['pw_w'], 0, name=n('conv_bwd'))
    g['conv_dw_w'], g['conv_dw_b'] = d_dw_w[:CONV_WIDTH], d_dw_b[0]
    g['conv_ln_g'], g['conv_ln_b'], g['conv_pw_w'] = d_ln_g[0], d_ln_b[0], d_pw
    d_proj, d_wbd, d_scale = _pool_bwd(s['proj_a'], d_mix, d_proj, p['wbd'], p['pool_scale'], name=n('pool_bwd'))
    g['pool_w'] = jnp.stack([d_wbd[i * POOL_GROUP:(i + 1) * POOL_GROUP, i * POOL_GROUP:(i + 1) * POOL_GROUP]
                             for i in range(len(POOL_WINDOWS))])
    g['pool_scale'] = d_scale[0]
    d_w_a = _mm(s['h'], d_proj, ta=True, name=n('d_w_a'), tm=1024, tn=768, tk=4096)
    started = sink.put(l, 'w_in', d_w_a, _mm(d_z, s['h'], name=n('d_w_fg'), tn=1024).T)
    d_h_fg = _mm(d_z, W['w_fg_t'], b_lead=0, ta=True, name=n('d_h_fg'), tn=1024, after=started)
    d_h = _mm(d_proj, W['w_a'], b_lead=0, tb=True, res=d_h_fg, name=n('d_h'), tn=1024, tk=D_PROJ_A)
    dx, dg1 = _rms_bwd(s['x'], p['norm1_g'], d_h, dx1, name=n('norm1_bwd'))
    g['norm1_g'] = dg1[0]
    sink.point(l, 'end', dx)
    return dx, g


SMALL_GRADS = REPLICATED + ('conv_dw_w', 'conv_pw_w', 'ffn_dw_w')


def _local_step(x, target, W, w_small, sink):
    depth = w_small['norm1_g'].shape[0]
    ps, saved = [], []
    for l in range(depth):
        p = _small_weights(w_small, l)
        x, s = _layer_fwd(x, W[l], p, l)
        ps.append(p)
        saved.append(s)
    loss, dx = _loss_head(x, target, name='loss_head')
    small = [None] * depth
    for l in reversed(range(depth)):
        dx, small[l] = _layer_bwd(dx, W[l], ps[l], saved[l], l, sink)
    return loss, dx, {k: jnp.stack([small[l][k] for l in range(depth)]) for k in SMALL_GRADS}


W_IN_SHARD = D_IN // N_CHIPS
W_IN_PAD = 640
N_A_TILES = D_PROJ_A // LANES
FG_COL0 = D_QKV


def _a_tile_base(j):
    if j == N_A_TILES:
        return FG_COL0, N_HEADS
    return (j * LANES if j * LANES < FG_COL0 else j * LANES + N_HEADS), LANES


def _shift_select(rows, cols, shift, row_max, col_max):
    r = lax.broadcasted_iota(jnp.int32, (rows, cols), 0)
    c = lax.broadcasted_iota(jnp.int32, (rows, cols), 1)
    return ((r + shift == c) & (r < row_max) & (c < col_max)).astype(BF16)


def _select_w_in(raw, *, name, tm=256):
    _, D, _ = raw.shape
    tm = _tile(D, tm, 16)
    plan = []
    for j in range(N_A_TILES + 1):
        base, cmax = _a_tile_base(j)
        parts = []
        for p in range(N_CHIPS):
            delta = base - W_IN_SHARD * p
            lo, hi = max(0, delta), min(W_IN_SHARD - 1, delta + cmax - 1)
            if lo > hi:
                continue
            a0 = (lo // LANES) * LANES
            kw = min(-(-(hi + 1 - a0) // LANES) * LANES, W_IN_PAD - a0)
            parts.append((p, a0, kw, delta))
        plan.append((cmax, parts))

    def body(raw_ref, wa_ref, fg_ref):
        for j, (cmax, parts) in enumerate(plan):
            acc = None
            for p, a0, kw, delta in parts:
                sel = _shift_select(kw, LANES, a0 - delta, W_IN_SHARD - a0, cmax)
                t = _dot(raw_ref[p, :, a0:a0 + kw], sel, 1, 0)
                acc = t if acc is None else acc + t
            if j == N_A_TILES:
                fg_ref[...] = acc.astype(BF16)
            else:
                wa_ref[:, j * LANES:(j + 1) * LANES] = acc.astype(BF16)

    return pl.pallas_call(
        body, name=name,
        out_shape=(jax.ShapeDtypeStruct((D, D_PROJ_A), BF16), jax.ShapeDtypeStruct((D, LANES), BF16)),
        grid=(D // tm,),
        in_specs=[pl.BlockSpec((N_CHIPS, tm, W_IN_PAD), lambda i: (0, i, 0))],
        out_specs=(pl.BlockSpec((tm, D_PROJ_A), lambda i: (i, 0)), pl.BlockSpec((tm, LANES), lambda i: (i, 0))),
        compiler_params=_params('parallel'),
    )(raw)


def _select_w_in_grads(p_a, p_fg, *, name, tm=256):
    D = p_a.shape[0]
    tm = _tile(D, tm, 16)
    n_local = W_IN_PAD // LANES
    plan = []
    for p in range(N_CHIPS):
        for i in range(n_local):
            cmax = max(0, min(LANES, W_IN_SHARD - i * LANES))
            parts = []
            for j in range(N_A_TILES + 1):
                base, rmax = _a_tile_base(j)
                e = base - W_IN_SHARD * p - i * LANES
                if e + rmax - 1 < 0 or e > cmax - 1:
                    continue
                parts.append((j, e, rmax))
            plan.append((p, i, cmax, parts))

    def body(a_ref, fg_ref, o32_ref, o16_ref):
        terms = {}

        def src(j):
            if j not in terms:
                v = fg_ref[...] if j == N_A_TILES else a_ref[:, j * LANES:(j + 1) * LANES]
                terms[j] = _split3(v)
            return terms[j]

        for p, i, cmax, parts in plan:
            acc = jnp.zeros((tm, LANES), F32)
            for j, e, rmax in parts:
                sel = _shift_select(LANES, LANES, e, rmax, cmax)
                for term in src(j):
                    acc = acc + _dot(term, sel, 1, 0)
            o32_ref[p, :, i * LANES:(i + 1) * LANES] = acc
            o16_ref[p, :, i * LANES:(i + 1) * LANES] = acc.astype(BF16)

    out = pl.BlockSpec((N_CHIPS, tm, W_IN_PAD), lambda i: (0, i, 0))
    return pl.pallas_call(
        body, name=name,
        out_shape=(jax.ShapeDtypeStruct((N_CHIPS, D, W_IN_PAD), F32), jax.ShapeDtypeStruct((N_CHIPS, D, W_IN_PAD), BF16)),
        grid=(D // tm,),
        in_specs=[pl.BlockSpec((tm, D_PROJ_A), lambda i: (i, 0)), pl.BlockSpec((tm, LANES), lambda i: (i, 0))],
        out_specs=(out, out),
        compiler_params=_params('parallel'),
    )(p_a, p_fg)


MESH = pl.DeviceIdType.MESH
HBM_SPEC = pl.BlockSpec(memory_space=pltpu.HBM)


def _place():
    return lax.axis_index('x'), lax.axis_index('y'), lax.axis_index('c')


def _other_chips(x, y):
    return [(1 - x, y), (x, 1 - y), (1 - x, 1 - y)]


def _up_pos(q):
    return (q % 2) * 2 + q // 2


CHUNKS = {
    'w_in': ('lead', None),
    'w_up': ('cols', None),
    'w_down': ('rows', None),
    'w_out': ('rows', None),
    'conv_pw_w': ('rows', None),
    'conv_dw_w': ('lead', None),
    'ffn_dw_w': ('lead', None),
}


def _window(ref, kind, l, q):
    at = (lambda *idx: ref.at[idx]) if l is None else (lambda *idx: ref.at[(l,) + idx])
    shape = ref.shape if l is None else ref.shape[1:]
    if kind == 'lead':
        return at(q)
    if kind == 'rows':
        cs = shape[0] // N_CHIPS
        return at(pl.ds(pl.multiple_of(q * cs, 16), cs), slice(None))
    cs = shape[1] // N_CHIPS
    return at(slice(None), pl.ds(pl.multiple_of(_up_pos(q) * cs, LANES), cs))


def _place_shard(src, l, pos_arr, full_shape, kind, *, name, tm=256, after=None):
    _, m, n = src.shape
    bm = _tile(m, tm, 16) if kind != 'rows' else m
    extra = () if after is None else (after,)

    def body(pos_ref, s_ref, *rest):
        rest[-1][...] = s_ref[...].astype(BF16)

    if kind == 'lead':
        out = pl.BlockSpec((None, bm, n), lambda i, pos: (pos[0], i, 0))
    elif kind == 'rows':
        out = pl.BlockSpec((bm, n), lambda i, pos: (pos[0], 0))
    else:
        out = pl.BlockSpec((bm, n), lambda i, pos: (i, pos[0]))
    return pl.pallas_call(
        body, name=name, out_shape=jax.ShapeDtypeStruct(full_shape, BF16),
        grid_spec=pltpu.PrefetchScalarGridSpec(
            num_scalar_prefetch=1, grid=(m // bm,),
            in_specs=[pl.BlockSpec((None, bm, n), lambda i, pos: (l, i, 0))] + [ANY_SPEC] * len(extra), out_specs=out),
        compiler_params=_params('parallel'),
    )(pos_arr, src, *extra)


GATHERED = ('w_in', 'w_up', 'w_down', 'w_out', 'conv_pw_w', 'conv_dw_w', 'ffn_dw_w')
GATHER_GROUPS = ((0, ('w_in', 'conv_dw_w', 'ffn_dw_w', 'conv_pw_w')), (0, ('w_out', 'w_up', 'w_down')), (1, GATHERED))
SEM_SPEC = pl.BlockSpec(memory_space=pltpu.SEMAPHORE)
SPLIT_COPY_PARAMS = pltpu.CompilerParams(has_side_effects=pltpu.SideEffectType.DATAFLOW_SIDE_EFFECTING)


def _gather_start(tag, groups, bufs):
    flat = [b for group in bufs for b in group]
    nb = len(flat)

    def body(*refs):
        outs, sems, token = refs[nb:2 * nb], refs[2 * nb:-1], refs[-1]
        token[...] = jnp.zeros(token.shape, F32)
        x, y, c = _place()
        pos = 0
        for g, keys in enumerate(GATHER_GROUPS[n][1] for n in groups):
            for i, k in enumerate(keys):
                w = _window(outs[pos], CHUNKS[k][0], None, 2 * x + y)
                pos += 1
                for j, chip in enumerate(_other_chips(x, y)):
                    pltpu.make_async_remote_copy(src_ref=w, dst_ref=w, send_sem=sems[2 * g].at[3 * i + j],
                                                 recv_sem=sems[2 * g + 1].at[3 * i + j], device_id=(*chip, c),
                                                 device_id_type=MESH).start()

    sem_shapes = [pltpu.SemaphoreType.DMA((3 * len(GATHER_GROUPS[n][1]),)) for n in groups for _ in range(2)]
    res = pl.pallas_call(
        body, name=f'gather_start_{tag}',
        out_shape=tuple(jax.ShapeDtypeStruct(b.shape, b.dtype) for b in flat) + tuple(sem_shapes)
        + (jax.ShapeDtypeStruct((8, LANES), F32),),
        in_specs=[HBM_SPEC] * nb,
        out_specs=tuple([HBM_SPEC] * nb + [SEM_SPEC] * len(sem_shapes) + [pl.BlockSpec(memory_space=pltpu.VMEM)]),
        input_output_aliases={b: b for b in range(nb)},
        compiler_params=SPLIT_COPY_PARAMS,
    )(*[pltpu.with_memory_space_constraint(b, pltpu.HBM) for b in flat])
    out_bufs, sems, pos = [], res[nb:-1], 0
    for group in bufs:
        out_bufs.append(list(res[pos:pos + len(group)]))
        pos += len(group)
    return out_bufs, [(sems[2 * g], sems[2 * g + 1]) for g in range(len(groups))], res[-1]


def _gather_wait(g, bufs, sems, after):
    keys = GATHER_GROUPS[g][1]
    nb = len(bufs)

    def body(*refs):
        send_sems, recv_sems = refs[nb], refs[nb + 1]
        outs = refs[nb + 3:]
        x, y, c = _place()
        for i, k in enumerate(keys):
            mine = _window(outs[i], CHUNKS[k][0], None, 2 * x + y)
            for j, (cx, cy) in enumerate(_other_chips(x, y)):
                theirs = _window(outs[i], CHUNKS[k][0], None, 2 * cx + cy)
                cp = pltpu.make_async_remote_copy(src_ref=mine, dst_ref=theirs, send_sem=send_sems.at[3 * i + j],
                                                  recv_sem=recv_sems.at[3 * i + j], device_id=(cx, cy, c), device_id_type=MESH)
                cp.wait_send()
                cp.wait_recv()

    return pl.pallas_call(
        body, name=f'gather_wait_{g}',
        out_shape=tuple(jax.ShapeDtypeStruct(b.shape, b.dtype) for b in bufs),
        in_specs=[HBM_SPEC] * nb + [SEM_SPEC, SEM_SPEC, ANY_SPEC], out_specs=tuple([HBM_SPEC] * nb),
        input_output_aliases={b: b for b in range(nb)},
        compiler_params=SPLIT_COPY_PARAMS,
    )(*bufs, *sems, after)


def _rs_block(M, N):
    return (_tile(M, 256, 16), _tile(N, 2048))


def _chunk_shape(shape, kind):
    if kind == 'lead':
        return tuple(shape[1:])
    if kind == 'rows':
        return (shape[0] // N_CHIPS, shape[1])
    return (shape[0], shape[1] // N_CHIPS)


def _rs_start(tag, bufs, kinds):
    nb = len(bufs)
    lands = [lax.empty((N_CHIPS - 1,) + _chunk_shape(b.shape, k), b.dtype) for b, k in zip(bufs, kinds)]

    def body(*refs):
        src, land = refs[2 * nb:3 * nb], refs[3 * nb:4 * nb]
        send_sems, recv_sems, token = refs[4 * nb:]
        token[...] = jnp.zeros(token.shape, F32)
        x, y, c = _place()
        for b in range(nb):
            for j, (cx, cy) in enumerate(_other_chips(x, y)):
                pltpu.make_async_remote_copy(
                    src_ref=_window(src[b], kinds[b], None, 2 * cx + cy), dst_ref=land[b].at[j],
                    send_sem=send_sems.at[3 * b + j], recv_sem=recv_sems.at[3 * b + j],
                    device_id=(cx, cy, c), device_id_type=MESH).start()

    sem = pltpu.SemaphoreType.DMA((3 * nb,))
    res = pl.pallas_call(
        body, name=f'rs_start_{tag}',
        out_shape=tuple(jax.ShapeDtypeStruct(b.shape, b.dtype) for b in list(bufs) + lands)
        + (sem, sem, jax.ShapeDtypeStruct((8, LANES), F32)),
        in_specs=[HBM_SPEC] * (2 * nb),
        out_specs=tuple([HBM_SPEC] * (2 * nb) + [SEM_SPEC, SEM_SPEC, pl.BlockSpec(memory_space=pltpu.VMEM)]),
        input_output_aliases={b: b for b in range(2 * nb)},
        compiler_params=SPLIT_COPY_PARAMS,
    )(*[pltpu.with_memory_space_constraint(b, pltpu.HBM) for b in list(bufs) + lands])
    return res[:nb], res[nb:2 * nb], res[2 * nb:2 * nb + 2], res[2 * nb + 2]


def _rs_wait(tag, bufs, lands, sems, kinds, after):
    nb = len(bufs)

    def body(*refs):
        send_sems, recv_sems = refs[2 * nb], refs[2 * nb + 1]
        src, land = refs[2 * nb + 3:3 * nb + 3], refs[3 * nb + 3:]
        x, y, c = _place()
        for b in range(nb):
            for j, (cx, cy) in enumerate(_other_chips(x, y)):
                cp = pltpu.make_async_remote_copy(
                    src_ref=_window(src[b], kinds[b], None, 2 * cx + cy), dst_ref=land[b].at[j],
                    send_sem=send_sems.at[3 * b + j], recv_sem=recv_sems.at[3 * b + j],
                    device_id=(cx, cy, c), device_id_type=MESH)
                cp.wait_send()
                cp.wait_recv()

    res = pl.pallas_call(
        body, name=f'rs_wait_{tag}',
        out_shape=tuple(jax.ShapeDtypeStruct(b.shape, b.dtype) for b in list(bufs) + list(lands)),
        in_specs=[HBM_SPEC] * (2 * nb) + [SEM_SPEC, SEM_SPEC, ANY_SPEC], out_specs=tuple([HBM_SPEC] * (2 * nb)),
        input_output_aliases={b: b for b in range(2 * nb)},
        compiler_params=SPLIT_COPY_PARAMS,
    )(*bufs, *lands, *sems, after)
    return res[nb:]


def _rs_sum(p, rb, kind, pos_arr, l, depth, buf, *, name):
    m, n = rb.shape[1:]
    bm, bn = _rs_block(m, n)
    nbm, nbn = m // bm, n // bn
    has_buf = buf is not None

    def body(q_ref, p_ref, r_ref, *rest):
        acc = p_ref[...]
        for j in range(N_CHIPS - 1):
            acc = acc + r_ref[j].astype(F32)
        rest[-1][...] = acc

    if kind == 'lead':
        p_map = lambda i, j, q: (q[0], i, j)
    elif kind == 'rows':
        p_map = lambda i, j, q: (q[0] * nbm + i, j)
    else:
        p_map = lambda i, j, q: (i, q[0] * nbn + j)
    r_spec = pl.BlockSpec((N_CHIPS - 1, bm, bn), lambda i, j, q: (0, i, j))
    p_spec = pl.BlockSpec(((None,) if kind == 'lead' else ()) + (bm, bn), p_map)
    return pl.pallas_call(
        body, name=name, out_shape=jax.ShapeDtypeStruct((depth, m, n), F32),
        grid_spec=pltpu.PrefetchScalarGridSpec(
            num_scalar_prefetch=1, grid=(nbm, nbn), in_specs=[p_spec, r_spec] + ([ANY_SPEC] if has_buf else []),
            out_specs=pl.BlockSpec((None, bm, bn), lambda i, j, q: (l, i, j))),
        input_output_aliases={3: 0} if has_buf else {},
        compiler_params=_params('parallel', 'parallel'),
    )(pos_arr, p, rb, *((buf,) if has_buf else ()))


def _swap_start(bufs):
    nb = len(bufs)
    lands = [lax.empty(b.shape, b.dtype) for b in bufs]

    def body(*refs):
        src, land = refs[2 * nb:3 * nb], refs[3 * nb:4 * nb]
        send_sems, recv_sems, token = refs[4 * nb:]
        token[...] = jnp.zeros(token.shape, F32)
        x, y, c = _place()
        for b in range(nb):
            pltpu.make_async_remote_copy(src_ref=src[b], dst_ref=land[b], send_sem=send_sems.at[b], recv_sem=recv_sems.at[b],
                                         device_id=(x, y, 1 - c), device_id_type=MESH).start()

    sem = pltpu.SemaphoreType.DMA((nb,))
    res = pl.pallas_call(
        body, name='rs_swap_start',
        out_shape=tuple(jax.ShapeDtypeStruct(b.shape, b.dtype) for b in list(bufs) + lands)
        + (sem, sem, jax.ShapeDtypeStruct((8, LANES), F32)),
        in_specs=[HBM_SPEC] * (2 * nb),
        out_specs=tuple([HBM_SPEC] * (2 * nb) + [SEM_SPEC, SEM_SPEC, pl.BlockSpec(memory_space=pltpu.VMEM)]),
        input_output_aliases={b: b for b in range(2 * nb)},
        compiler_params=SPLIT_COPY_PARAMS,
    )(*[pltpu.with_memory_space_constraint(b, pltpu.HBM) for b in list(bufs) + lands])
    return res[:nb], res[nb:2 * nb], res[2 * nb:2 * nb + 2], res[2 * nb + 2]


def _swap_wait(bufs, lands, sems, after):
    nb = len(bufs)

    def body(*refs):
        send_sems, recv_sems = refs[2 * nb], refs[2 * nb + 1]
        src, land = refs[2 * nb + 3:3 * nb + 3], refs[3 * nb + 3:]
        x, y, c = _place()
        for b in range(nb):
            cp = pltpu.make_async_remote_copy(src_ref=src[b], dst_ref=land[b], send_sem=send_sems.at[b],
                                              recv_sem=recv_sems.at[b], device_id=(x, y, 1 - c), device_id_type=MESH)
            cp.wait_send()
            cp.wait_recv()

    res = pl.pallas_call(
        body, name='rs_swap_wait',
        out_shape=tuple(jax.ShapeDtypeStruct(b.shape, b.dtype) for b in list(bufs) + list(lands)),
        in_specs=[HBM_SPEC] * (2 * nb) + [SEM_SPEC, SEM_SPEC, ANY_SPEC], out_specs=tuple([HBM_SPEC] * (2 * nb)),
        input_output_aliases={b: b for b in range(2 * nb)},
        compiler_params=SPLIT_COPY_PARAMS,
    )(*bufs, *lands, *sems, after)
    return res[:nb], res[nb:]


def _all_reduce_small(v, after):
    r = v.shape[0]

    def body(x_ref, after_ref, tot_ref, all_ref, send_sems, recv_sems):
        x, y, c = _place()
        me, sibling = (x, y, c), (x, y, 1 - c)
        chips = _other_chips(x, y)

        def rows(px, py, pc):
            return all_ref.at[pl.ds((4 * px + 2 * py + pc) * r, r), :]

        def copy(k, block, to, src=None):
            return pltpu.make_async_remote_copy(
                src_ref=rows(*block) if src is None else src, dst_ref=rows(*block),
                send_sem=send_sems.at[k], recv_sem=recv_sems.at[k], device_id=to, device_id_type=MESH)

        rows(*me)[...] = x_ref[...]
        first = [copy(0, me, sibling, src=x_ref)]
        first += [copy(1 + j, me, (*chip, c), src=x_ref) for j, chip in enumerate(chips)]
        for cp in first:
            cp.start()
        passed = [copy(4 + j, (*chip, c), sibling) for j, chip in enumerate(chips)]
        for j, chip in enumerate(chips):
            copy(1 + j, (*chip, c), me).wait_recv()
            passed[j].start()
        copy(0, sibling, me).wait_recv()
        for j, chip in enumerate(chips):
            copy(4 + j, (*chip, 1 - c), me).wait_recv()
        for cp in first + passed:
            cp.wait_send()
        acc = all_ref[0:r, :]
        for d in range(1, N_DEV):
            acc = acc + all_ref[d * r:(d + 1) * r, :]
        tot_ref[...] = acc

    return pl.pallas_call(
        body, name='all_reduce_small', out_shape=jax.ShapeDtypeStruct((r, LANES), F32),
        in_specs=[pl.BlockSpec(memory_space=pltpu.VMEM), ANY_SPEC], out_specs=pl.BlockSpec(memory_space=pltpu.VMEM),
        scratch_shapes=[pltpu.VMEM((N_DEV * r, LANES), F32), pltpu.SemaphoreType.DMA((7,)), pltpu.SemaphoreType.DMA((7,))],
    )(v, after)


def _adamw(w, g, m, v, *, name, g2=None, ts=256):
    R, C = w.shape
    Cg = g.shape[1]
    ts = _tile(R, ts, 8)
    c1 = 1.0 - ADAM_B1 ** ADAM_STEP
    c2 = 1.0 - ADAM_B2 ** ADAM_STEP
    two = g2 is not None

    def body(w_ref, g_ref, *rest):
        m_ref, v_ref, go_ref, d_ref, nm_ref, nv_ref = rest[two:]
        gv = g_ref[:, 0:C]
        if two:
            gv = gv + rest[0][:, 0:C]
        nm = ADAM_B1 * m_ref[...] + (1.0 - ADAM_B1) * gv
        nv = ADAM_B2 * v_ref[...] + (1.0 - ADAM_B2) * (gv * gv)
        d_ref[...] = -ADAM_LR * ((nm / c1) / (jnp.sqrt(nv / c2) + ADAM_EPS) + ADAM_WD * w_ref[...])
        go_ref[...] = gv
        nm_ref[...] = nm
        nv_ref[...] = nv

    blk = pl.BlockSpec((ts, C), lambda i: (i, 0))
    gblk = pl.BlockSpec((ts, Cg), lambda i: (i, 0))
    shape = jax.ShapeDtypeStruct((R, C), F32)
    return pl.pallas_call(body, name=name, out_shape=(shape, shape, shape, shape), grid=(R // ts,),
                          in_specs=[blk, gblk] + ([gblk] if two else []) + [blk, blk], out_specs=(blk, blk, blk, blk),
                          compiler_params=_params('parallel'))(w, g, *((g2,) if two else ()), m, v)


def _pack_rows(parts, row_unit):
    flat = jnp.concatenate(parts)
    flat = _pad_axis(flat, -(-flat.shape[0] // (row_unit * LANES)) * row_unit * LANES, 0)
    return flat.reshape(-1, LANES)


def _as_2d(a):
    return a.reshape(-1, a.shape[-1])


def _mesh_place():
    cx, cy, cc = _place()
    chip = 2 * cx + cy
    as_arr = lambda v: jnp.reshape(v, (1,)).astype(jnp.int32)
    return chip, as_arr(cc), as_arr(chip), as_arr(_up_pos(chip))


class _LayerWeights:
    def __init__(self, groups, started=None):
        self.groups = groups
        self.started = started
        self.ready = {}

    def get(self, name, after):
        if name not in self.ready:
            for names, wait in self.groups:
                if name in names:
                    self.ready.update({k: v[None] for k, v in wait(after).items()})
        return self.ready[name]


def _gather_full(w, place):
    chip, _, chip_arr, up_pos_arr = place
    L, D = w['w_in'].shape[:2]
    w_in_pad = _pad_axis(w['w_in'], W_IN_PAD, 2)

    def placed(k, l, after=None):
        if k == 'w_in':
            return _place_shard(w_in_pad, l, chip_arr, (N_CHIPS, D, W_IN_PAD), 'lead', name=f'place_w_in_{l}', after=after)
        if k == 'w_up':
            return _place_shard(w[k], l, up_pos_arr, (w[k].shape[1], N_CHIPS * w[k].shape[2]), 'cols', name=f'place_w_up_{l}',
                                after=after)
        if k in ('conv_dw_w', 'ffn_dw_w'):
            return lax.dynamic_update_slice_in_dim(jnp.zeros((N_CHIPS,) + w[k].shape[1:], F32), w[k][l][None], chip, axis=0)
        return _place_shard(w[k], l, chip_arr, (N_CHIPS * w[k].shape[1], w[k].shape[2]), 'rows', name=f'place_{k}_{l}',
                            after=after)

    first, rest = [0], list(range(1, len(GATHER_GROUPS)))
    bufs0, sems0, token = _gather_start('first', first, [[placed(k, GATHER_GROUPS[0][0]) for k in GATHER_GROUPS[0][1]]])
    bufs1, sems1, started = _gather_start('rest', rest, [[placed(k, GATHER_GROUPS[g][0], token) for k in GATHER_GROUPS[g][1]]
                                                         for g in rest])
    bufs, sems = bufs0 + bufs1, sems0 + sems1
    unchunk = lambda a: jnp.moveaxis(a, 0, 1).reshape(a.shape[1], -1)

    def waiter(g):
        l, keys = GATHER_GROUPS[g]

        def wait(after):
            full = dict(zip(keys, _gather_wait(g, bufs[g], sems[g], after)))
            out = {}
            if 'w_in' in full:
                out['w_a'], w_fg = _select_w_in(full['w_in'], name=f'select_w_in_{l}')
                out['w_fg_t'] = w_fg.T
            if 'conv_dw_w' in full:
                out['dw_w'] = _pad_axis(unchunk(full['conv_dw_w']), CONV_HALO, 0)
            if 'ffn_dw_w' in full:
                out['ffn_w'] = _pad_axis(_pair_cols(unchunk(full['ffn_dw_w'])), FFN_HALO, 0)
            if 'conv_pw_w' in full:
                out['pw_w'] = full['conv_pw_w']
            out.update({k: full[k] for k in ('w_up', 'w_down', 'w_out') if k in full})
            return out

        names = {'w_in': ('w_a', 'w_fg_t'), 'conv_dw_w': ('dw_w',), 'ffn_dw_w': ('ffn_w',), 'conv_pw_w': ('pw_w',)}
        return tuple(n for k in keys for n in names.get(k, (k,))), wait

    return [_LayerWeights([waiter(g) for g in range(len(GATHER_GROUPS)) if GATHER_GROUPS[g][0] == l],
                          started if l == 0 else None) for l in range(L)]


RS_WIRE = ('w_in', 'w_up', 'w_down', 'w_out')
RS_GROUPS = (('ffn', ('w_down', 'w_up')), ('mix', ('w_out', 'w_in')))


class _GradReducer:
    def __init__(self, place, depth):
        _, _, self.chip_arr, self.up_pos_arr = place
        self.depth = depth
        self.got = {}
        self.flying = {}
        self.sums = {}

    def put(self, l, key, g32, g16):
        if key == 'w_in':
            g32, g16 = _select_w_in_grads(g32, g16, name=f'l{l}_select_w_in_grads')
        self.got[(l, key)] = (g32, g16)
        for tag, keys in RS_GROUPS:
            if key == keys[-1]:
                kinds = [CHUNKS[k][0] for k in keys]
                bufs, lands, sems, token = _rs_start(f'l{l}_{tag}', [self.got[(l, k)][1] for k in keys], kinds)
                self.flying[(l, tag)] = (bufs, lands, sems, kinds)
                return token
        return None

    def point(self, l, where, after):
        if where == 'mid':
            self._land(l + 1, 'mix', after)
        else:
            self._land(l, 'ffn', after)

    def _land(self, l, tag, after):
        if (l, tag) not in self.flying:
            return
        bufs, lands, sems, kinds = self.flying.pop((l, tag))
        lands = _rs_wait(f'l{l}_{tag}', bufs, lands, sems, kinds, after)
        for k, rb, kind in zip(dict(RS_GROUPS)[tag], lands, kinds):
            pos = self.up_pos_arr if kind == 'cols' else self.chip_arr
            self.sums[k] = _rs_sum(self.got.pop((l, k))[0], rb, kind, pos, l, self.depth, self.sums.get(k), name=f'l{l}_rs_sum_{k}')

    def finish_start(self, after):
        for l, tag in list(self.flying):
            self._land(l, tag, after)
        *self.swap, token = _swap_start([self.sums[k] for k in RS_WIRE])
        return token

    def finish_wait(self, after):
        return dict(zip(RS_WIRE, zip(*_swap_wait(*self.swap, after))))


def kernel(x, norm1_g, w_in, b_f, q_norm_g, k_norm_g, conv_dw_w, conv_dw_b, conv_ln_g, conv_ln_b, conv_pw_w, pool_w, pool_scale, w_out, norm2_g, w_up, ffn_dw_w, w_down, loss_target, m_norm1_g, m_w_in, m_b_f, m_q_norm_g, m_k_norm_g, m_conv_dw_w, m_conv_dw_b, m_conv_ln_g, m_conv_ln_b, m_conv_pw_w, m_pool_w, m_pool_scale, m_w_out, m_norm2_g, m_w_up, m_ffn_dw_w, m_w_down, v_norm1_g, v_w_in, v_b_f, v_q_norm_g, v_k_norm_g, v_conv_dw_w, v_conv_dw_b, v_conv_ln_g, v_conv_ln_b, v_conv_pw_w, v_pool_w, v_pool_scale, v_w_out, v_norm2_g, v_w_up, v_ffn_dw_w, v_w_down):
    given = dict(locals())
    w = {k: given[k] for k in WEIGHTS}
    mom_m = {k: given['m_' + k] for k in WEIGHTS}
    mom_v = {k: given['v_' + k] for k in WEIGHTS}
    place = _mesh_place()
    chip = place[0]
    W = _gather_full(w, place)

    reducer = _GradReducer(place, norm1_g.shape[0])
    loss_part, grad_x, g_small = _local_step(x[0], loss_target[0], W, {k: w[k] for k in REPLICATED}, reducer)
    loss = lax.psum(loss_part[0, 0], ('x', 'y', 'c'))
    swapping = reducer.finish_start(grad_x)

    small = _pack_rows([g_small[k].reshape(-1) for k in SMALL_GRADS], 8)
    small_sum = _all_reduce_small(small, swapping)
    g_sum, delta, new_m, new_v = {}, {}, {}, {}
    off = 0
    small_full = {}
    for k in SMALL_GRADS:
        small_full[k] = small_sum.reshape(-1)[off:off + g_small[k].size].reshape(g_small[k].shape)
        off += g_small[k].size
    small_g = {k: small_full[k] for k in REPLICATED}
    small_g['conv_dw_w'] = lax.dynamic_slice_in_dim(small_full['conv_dw_w'], chip * w['conv_dw_w'].shape[2], w['conv_dw_w'].shape[2], axis=2)
    small_g['conv_pw_w'] = lax.dynamic_slice_in_dim(small_full['conv_pw_w'], chip * w['conv_pw_w'].shape[1], w['conv_pw_w'].shape[1], axis=1)
    small_g['ffn_dw_w'] = lax.dynamic_slice_in_dim(small_full['ffn_dw_w'], chip * w['ffn_dw_w'].shape[2], w['ffn_dw_w'].shape[2], axis=2)
    pack_small = lambda t: _pack_rows([t[k].reshape(-1) for k in SMALL_GRADS], 256)
    outs = _adamw(pack_small(w), pack_small(small_g), pack_small(mom_m), pack_small(mom_v), name='adamw_small')
    off = 0
    for k in SMALL_GRADS:
        pieces = [o.reshape(-1)[off:off + w[k].size].reshape(w[k].shape) for o in outs]
        g_sum[k], delta[k], new_m[k], new_v[k] = pieces
        off += w[k].size

    sums = reducer.finish_wait(outs[1])
    for k in RS_WIRE:
        outs = _adamw(_as_2d(w[k]), _as_2d(sums[k][0]), _as_2d(mom_m[k]), _as_2d(mom_v[k]), g2=_as_2d(sums[k][1]), name='adamw_' + k)
        g_sum[k], delta[k], new_m[k], new_v[k] = [o.reshape(w[k].shape) for o in outs]

    return (loss, grad_x[None], *[g_sum[k] for k in WEIGHTS], *[delta[k] for k in WEIGHTS],
            *[new_m[k] for k in WEIGHTS], *[new_v[k] for k in WEIGHTS])
```
